```python
import math
import jax, jax.numpy as jnp
from jax import lax
import numpy as np

D_MODEL = 2048
BATCH = 8
SEQ = 4096
DEPTH = 1

MIX_WIDTH = D_MODEL
SSM_WIDTH = MIX_WIDTH // 2
POOL_WIDTH = MIX_WIDTH - SSM_WIDTH
SSM_GROUP = 16
SSM_GROUPS = SSM_WIDTH // SSM_GROUP
SSM_STATE = 64
POOL_WINDOWS = (2, 4, 8, 16)
POOL_GROUPS = len(POOL_WINDOWS)
POOL_GROUP_WIDTH = POOL_WIDTH // POOL_GROUPS
N_MEM = 256
MEM_HEADS = 4
MEM_HEAD_DIM = D_MODEL // MEM_HEADS
D_FF = ((8 * D_MODEL // 3 + 255) // 256) * 256
DT_MIN = 1e-3
DT_MAX = 1e-1
EPS = 1e-6

kernel_name = 'hymba_s5_pool_macaron_block'

F32 = jnp.float32


def rmsnorm(x, g):
    xf = x.astype(F32)
    y = xf * lax.rsqrt(jnp.mean(xf * xf, axis=-1, keepdims=True) + EPS) * g.astype(F32)
    return y.astype(x.dtype)


def swiglu(x, w_gate, w_up, w_down):
    return (jax.nn.silu(x @ w_gate) * (x @ w_up)) @ w_down


def _complex_scan_combine(e1, e2):
    a1r, a1i, b1r, b1i = e1
    a2r, a2i, b2r, b2i = e2
    ar = a2r * a1r - a2i * a1i
    ai = a2r * a1i + a2i * a1r
    br = a2r * b1r - a2i * b1i + b2r
    bi = a2r * b1i + a2i * b1r + b2i
    return (ar, ai, br, bi)


def s5_mixer(u, a_re, a_im, log_dt, b_re, b_im, c_re, c_im, d_skip, w_glu, b_glu):
    bsz, L, _ = u.shape
    uf = u.astype(F32)
    ug = uf.reshape(bsz, L, SSM_GROUPS, SSM_GROUP)
    dt = jnp.exp(log_dt.astype(F32))[:, None]
    lr, li = a_re.astype(F32), a_im.astype(F32)
    mag = jnp.exp(lr * dt)
    abar_re = mag * jnp.cos(li * dt)
    abar_im = mag * jnp.sin(li * dt)
    nr, ni = abar_re - 1.0, abar_im
    den = lr * lr + li * li
    fr = (nr * lr + ni * li) / den
    fi = (ni * lr - nr * li) / den
    br, bi = b_re.astype(F32), b_im.astype(F32)
    bbar_re = fr[..., None] * br - fi[..., None] * bi
    bbar_im = fr[..., None] * bi + fi[..., None] * br
    bu_re = jnp.einsum('blgh,gph->blgp', ug, bbar_re)
    bu_im = jnp.einsum('blgh,gph->blgp', ug, bbar_im)
    a_full_re = jnp.broadcast_to(abar_re, bu_re.shape)
    a_full_im = jnp.broadcast_to(abar_im, bu_im.shape)
    _, _, s_re, s_im = lax.associative_scan(
        _complex_scan_combine, (a_full_re, a_full_im, bu_re, bu_im), axis=1)
    y = (jnp.einsum('blgp,ghp->blgh', s_re, c_re.astype(F32))
         - jnp.einsum('blgp,ghp->blgh', s_im, c_im.astype(F32)))
    y = y.reshape(bsz, L, SSM_WIDTH) + d_skip.astype(F32) * uf
    y = jax.nn.gelu(y)
    y = y * jax.nn.sigmoid(y @ w_glu.astype(F32) + b_glu.astype(F32))
    return y.astype(u.dtype)


def pool_mixer(v, w_pool, pool_scale):
    bsz, L, _ = v.shape
    vf = v.astype(F32).reshape(bsz, L, POOL_GROUPS, POOL_GROUP_WIDTH)
    csum = jnp.cumsum(vf, axis=1)
    t = jnp.arange(L)
    pooled = []
    for gi, w in enumerate(POOL_WINDOWS):
        cg = csum[:, :, gi]
        shifted = jnp.pad(cg, ((0, 0), (w, 0), (0, 0)))[:, :L]
        cnt = jnp.minimum(t + 1, w).astype(F32)[None, :, None]
        pooled.append((cg - shifted) / cnt)
    pooled = jnp.stack(pooled, axis=2) - vf
    z = jnp.einsum('blgc,gcd->blgd', pooled, w_pool.astype(F32))
    z = z.reshape(bsz, L, POOL_WIDTH) * pool_scale.astype(F32)
    return z.astype(v.dtype)


def memory_cross_attention(h, memn, w_q, w_k, w_v, w_o):
    bsz, L, _ = h.shape
    q = (h @ w_q).reshape(bsz, L, MEM_HEADS, MEM_HEAD_DIM)
    k = (memn @ w_k).reshape(bsz, N_MEM, MEM_HEADS, MEM_HEAD_DIM)
    v = (memn @ w_v).reshape(bsz, N_MEM, MEM_HEADS, MEM_HEAD_DIM)
    s = jnp.einsum('blhd,bmhd->bhlm', q.astype(F32), k.astype(F32)) * (MEM_HEAD_DIM ** -0.5)
    p = jax.nn.softmax(s, axis=-1).astype(h.dtype)
    o = jnp.einsum('bhlm,bmhd->blhd', p, v).reshape(bsz, L, D_MODEL)
    return o @ w_o


def _fwd_setup_inputs(seed: int = 0) -> dict:
    key = jax.random.key(seed)
    ks = iter(jax.random.split(key, 40))
    nrm = lambda shape, scale: jax.random.normal(next(ks), shape, F32) * scale
    gain = lambda shape: 1.0 + 0.02 * jax.random.normal(next(ks), shape, F32)
    Ly = DEPTH
    G, P, H = SSM_GROUPS, SSM_STATE, SSM_GROUP
    inp = {}
    inp['x'] = nrm((BATCH, SEQ, D_MODEL), 1.0)
    inp['mem'] = nrm((BATCH, N_MEM, D_MODEL), 1.0)
    inp['g_ffn1'] = gain((Ly, D_MODEL))
    inp['w1_gate'] = nrm((Ly, D_MODEL, D_FF), D_MODEL ** -0.5)
    inp['w1_up'] = nrm((Ly, D_MODEL, D_FF), D_MODEL ** -0.5)
    inp['w1_down'] = nrm((Ly, D_FF, D_MODEL), D_FF ** -0.5)
    inp['g_mix'] = gain((Ly, D_MODEL))
    inp['w_in'] = nrm((Ly, D_MODEL, MIX_WIDTH), D_MODEL ** -0.5)
    inp['ssm_a_re'] = -0.5 + nrm((Ly, G, P), 0.01)
    inp['ssm_a_im'] = math.pi * jnp.arange(P, dtype=F32)[None, None, :] + nrm((Ly, G, P), 0.01)
    inp['ssm_log_dt'] = jax.random.uniform(next(ks), (Ly, G), F32, math.log(DT_MIN), math.log(DT_MAX))
    inp['ssm_b_re'] = nrm((Ly, G, P, H), (2 * H) ** -0.5)
    inp['ssm_b_im'] = nrm((Ly, G, P, H), (2 * H) ** -0.5)
    inp['ssm_c_re'] = nrm((Ly, G, H, P), (2 * P) ** -0.5)
    inp['ssm_c_im'] = nrm((Ly, G, H, P), (2 * P) ** -0.5)
    inp['ssm_d'] = nrm((Ly, SSM_WIDTH), 1.0)
    inp['w_glu'] = nrm((Ly, SSM_WIDTH, SSM_WIDTH), SSM_WIDTH ** -0.5)
    inp['b_glu'] = nrm((Ly, SSM_WIDTH), 0.02)
    inp['w_pool'] = nrm((Ly, POOL_GROUPS, POOL_GROUP_WIDTH, POOL_GROUP_WIDTH), POOL_GROUP_WIDTH ** -0.5)
    inp['pool_scale'] = 1.0 + nrm((Ly, POOL_WIDTH), 0.1)
    inp['g_out_ssm'] = gain((Ly, SSM_WIDTH))
    inp['g_out_pool'] = gain((Ly, POOL_WIDTH))
    inp['w_out'] = nrm((Ly, MIX_WIDTH, D_MODEL), MIX_WIDTH ** -0.5)
    inp['g_xattn'] = gain((Ly, D_MODEL))
    inp['g_mem'] = gain((Ly, D_MODEL))
    inp['w_q'] = nrm((Ly, D_MODEL, D_MODEL), D_MODEL ** -0.5)
    inp['w_k'] = nrm((Ly, D_MODEL, D_MODEL), D_MODEL ** -0.5)
    inp['w_v'] = nrm((Ly, D_MODEL, D_MODEL), D_MODEL ** -0.5)
    inp['w_o'] = nrm((Ly, D_MODEL, D_MODEL), D_MODEL ** -0.5)
    inp['g_ffn2'] = gain((Ly, D_MODEL))
    inp['w2_gate'] = nrm((Ly, D_MODEL, D_FF), D_MODEL ** -0.5)
    inp['w2_up'] = nrm((Ly, D_MODEL, D_FF), D_MODEL ** -0.5)
    inp['w2_down'] = nrm((Ly, D_FF, D_MODEL), D_FF ** -0.5)
    inp['g_final'] = gain((D_MODEL,))
    return inp


def _fwd_reference(x, mem, g_ffn1, w1_gate, w1_up, w1_down, g_mix, w_in,
              ssm_a_re, ssm_a_im, ssm_log_dt, ssm_b_re, ssm_b_im, ssm_c_re, ssm_c_im,
              ssm_d, w_glu, b_glu, w_pool, pool_scale, g_out_ssm, g_out_pool, w_out,
              g_xattn, g_mem, w_q, w_k, w_v, w_o,
              g_ffn2, w2_gate, w2_up, w2_down, g_final):
    h = x
    for l in range(DEPTH):
        h = h + 0.5 * swiglu(rmsnorm(h, g_ffn1[l]), w1_gate[l], w1_up[l], w1_down[l])
        u = rmsnorm(h, g_mix[l]) @ w_in[l]
        u_ssm, u_pool = u[..., :SSM_WIDTH], u[..., SSM_WIDTH:]
        y_ssm = s5_mixer(u_ssm, ssm_a_re[l], ssm_a_im[l], ssm_log_dt[l], ssm_b_re[l], ssm_b_im[l],
                         ssm_c_re[l], ssm_c_im[l], ssm_d[l], w_glu[l], b_glu[l])
        y_pool = pool_mixer(u_pool, w_pool[l], pool_scale[l])
        merged = jnp.concatenate([rmsnorm(y_ssm, g_out_ssm[l]), rmsnorm(y_pool, g_out_pool[l])], axis=-1)
        h = h + merged @ w_out[l]
        memn = rmsnorm(mem, g_mem[l])
        h = h + memory_cross_attention(rmsnorm(h, g_xattn[l]), memn, w_q[l], w_k[l], w_v[l], w_o[l])
        h = h + 0.5 * swiglu(rmsnorm(h, g_ffn2[l]), w2_gate[l], w2_up[l], w2_down[l])
    return rmsnorm(h, g_final)


import jax as _jax
import jax.numpy as _jnp

TWIN_FORMAT = 'train_step'
FWD_PARAMS = ['x', 'mem', 'g_ffn1', 'w1_gate', 'w1_up', 'w1_down', 'g_mix', 'w_in', 'ssm_a_re', 'ssm_a_im', 'ssm_log_dt', 'ssm_b_re', 'ssm_b_im', 'ssm_c_re', 'ssm_c_im', 'ssm_d', 'w_glu', 'b_glu', 'w_pool', 'pool_scale', 'g_out_ssm', 'g_out_pool', 'w_out', 'g_xattn', 'g_mem', 'w_q', 'w_k', 'w_v', 'w_o', 'g_ffn2', 'w2_gate', 'w2_up', 'w2_down', 'g_final']
TWIN_WEIGHTS = ['g_ffn1', 'w1_gate', 'w1_up', 'w1_down', 'g_mix', 'w_in', 'ssm_a_re', 'ssm_a_im', 'ssm_log_dt', 'ssm_b_re', 'ssm_b_im', 'ssm_c_re', 'ssm_c_im', 'ssm_d', 'w_glu', 'b_glu', 'w_pool', 'pool_scale', 'g_out_ssm', 'g_out_pool', 'w_out', 'g_xattn', 'g_mem', 'w_q', 'w_k', 'w_v', 'w_o', 'g_ffn2', 'w2_gate', 'w2_up', 'w2_down', 'g_final']
TWIN_DIFF_INPUT = 'x'
TWIN_INPUTS = ['x', 'mem', 'g_ffn1', 'w1_gate', 'w1_up', 'w1_down', 'g_mix', 'w_in', 'ssm_a_re', 'ssm_a_im', 'ssm_log_dt', 'ssm_b_re', 'ssm_b_im', 'ssm_c_re', 'ssm_c_im', 'ssm_d', 'w_glu', 'b_glu', 'w_pool', 'pool_scale', 'g_out_ssm', 'g_out_pool', 'w_out', 'g_xattn', 'g_mem', 'w_q', 'w_k', 'w_v', 'w_o', 'g_ffn2', 'w2_gate', 'w2_up', 'w2_down', 'g_final', 'loss_target', 'm_g_ffn1', 'm_w1_gate', 'm_w1_up', 'm_w1_down', 'm_g_mix', 'm_w_in', 'm_ssm_a_re', 'm_ssm_a_im', 'm_ssm_log_dt', 'm_ssm_b_re', 'm_ssm_b_im', 'm_ssm_c_re', 'm_ssm_c_im', 'm_ssm_d', 'm_w_glu', 'm_b_glu', 'm_w_pool', 'm_pool_scale', 'm_g_out_ssm', 'm_g_out_pool', 'm_w_out', 'm_g_xattn', 'm_g_mem', 'm_w_q', 'm_w_k', 'm_w_v', 'm_w_o', 'm_g_ffn2', 'm_w2_gate', 'm_w2_up', 'm_w2_down', 'm_g_final', 'v_g_ffn1', 'v_w1_gate', 'v_w1_up', 'v_w1_down', 'v_g_mix', 'v_w_in', 'v_ssm_a_re', 'v_ssm_a_im', 'v_ssm_log_dt', 'v_ssm_b_re', 'v_ssm_b_im', 'v_ssm_c_re', 'v_ssm_c_im', 'v_ssm_d', 'v_w_glu', 'v_b_glu', 'v_w_pool', 'v_pool_scale', 'v_g_out_ssm', 'v_g_out_pool', 'v_w_out', 'v_g_xattn', 'v_g_mem', 'v_w_q', 'v_w_k', 'v_w_v', 'v_w_o', 'v_g_ffn2', 'v_w2_gate', 'v_w2_up', 'v_w2_down', 'v_g_final']
TWIN_OUTPUTS = ['loss', 'grad_x', 'grad_g_ffn1', 'grad_w1_gate', 'grad_w1_up', 'grad_w1_down', 'grad_g_mix', 'grad_w_in', 'grad_ssm_a_re', 'grad_ssm_a_im', 'grad_ssm_log_dt', 'grad_ssm_b_re', 'grad_ssm_b_im', 'grad_ssm_c_re', 'grad_ssm_c_im', 'grad_ssm_d', 'grad_w_glu', 'grad_b_glu', 'grad_w_pool', 'grad_pool_scale', 'grad_g_out_ssm', 'grad_g_out_pool', 'grad_w_out', 'grad_g_xattn', 'grad_g_mem', 'grad_w_q', 'grad_w_k', 'grad_w_v', 'grad_w_o', 'grad_g_ffn2', 'grad_w2_gate', 'grad_w2_up', 'grad_w2_down', 'grad_g_final', 'delta_g_ffn1', 'delta_w1_gate', 'delta_w1_up', 'delta_w1_down', 'delta_g_mix', 'delta_w_in', 'delta_ssm_a_re', 'delta_ssm_a_im', 'delta_ssm_log_dt', 'delta_ssm_b_re', 'delta_ssm_b_im', 'delta_ssm_c_re', 'delta_ssm_c_im', 'delta_ssm_d', 'delta_w_glu', 'delta_b_glu', 'delta_w_pool', 'delta_pool_scale', 'delta_g_out_ssm', 'delta_g_out_pool', 'delta_w_out', 'delta_g_xattn', 'delta_g_mem', 'delta_w_q', 'delta_w_k', 'delta_w_v', 'delta_w_o', 'delta_g_ffn2', 'delta_w2_gate', 'delta_w2_up', 'delta_w2_down', 'delta_g_final', 'new_m_g_ffn1', 'new_m_w1_gate', 'new_m_w1_up', 'new_m_w1_down', 'new_m_g_mix', 'new_m_w_in', 'new_m_ssm_a_re', 'new_m_ssm_a_im', 'new_m_ssm_log_dt', 'new_m_ssm_b_re', 'new_m_ssm_b_im', 'new_m_ssm_c_re', 'new_m_ssm_c_im', 'new_m_ssm_d', 'new_m_w_glu', 'new_m_b_glu', 'new_m_w_pool', 'new_m_pool_scale', 'new_m_g_out_ssm', 'new_m_g_out_pool', 'new_m_w_out', 'new_m_g_xattn', 'new_m_g_mem', 'new_m_w_q', 'new_m_w_k', 'new_m_w_v', 'new_m_w_o', 'new_m_g_ffn2', 'new_m_w2_gate', 'new_m_w2_up', 'new_m_w2_down', 'new_m_g_final', 'new_v_g_ffn1', 'new_v_w1_gate', 'new_v_w1_up', 'new_v_w1_down', 'new_v_g_mix', 'new_v_w_in', 'new_v_ssm_a_re', 'new_v_ssm_a_im', 'new_v_ssm_log_dt', 'new_v_ssm_b_re', 'new_v_ssm_b_im', 'new_v_ssm_c_re', 'new_v_ssm_c_im', 'new_v_ssm_d', 'new_v_w_glu', 'new_v_b_glu', 'new_v_w_pool', 'new_v_pool_scale', 'new_v_g_out_ssm', 'new_v_g_out_pool', 'new_v_w_out', 'new_v_g_xattn', 'new_v_g_mem', 'new_v_w_q', 'new_v_w_k', 'new_v_w_v', 'new_v_w_o', 'new_v_g_ffn2', 'new_v_w2_gate', 'new_v_w2_up', 'new_v_w2_down', 'new_v_g_final']
TWIN_LEAF_KINDS = {'loss': 'loss', 'grad_x': 'grad_x', 'grad_g_ffn1': 'grad_w', 'grad_w1_gate': 'grad_w', 'grad_w1_up': 'grad_w', 'grad_w1_down': 'grad_w', 'grad_g_mix': 'grad_w', 'grad_w_in': 'grad_w', 'grad_ssm_a_re': 'grad_w', 'grad_ssm_a_im': 'grad_w', 'grad_ssm_log_dt': 'grad_w', 'grad_ssm_b_re': 'grad_w', 'grad_ssm_b_im': 'grad_w', 'grad_ssm_c_re': 'grad_w', 'grad_ssm_c_im': 'grad_w', 'grad_ssm_d': 'grad_w', 'grad_w_glu': 'grad_w', 'grad_b_glu': 'grad_w', 'grad_w_pool': 'grad_w', 'grad_pool_scale': 'grad_w', 'grad_g_out_ssm': 'grad_w', 'grad_g_out_pool': 'grad_w', 'grad_w_out': 'grad_w', 'grad_g_xattn': 'grad_w', 'grad_g_mem': 'grad_w', 'grad_w_q': 'grad_w', 'grad_w_k': 'grad_w', 'grad_w_v': 'grad_w', 'grad_w_o': 'grad_w', 'grad_g_ffn2': 'grad_w', 'grad_w2_gate': 'grad_w', 'grad_w2_up': 'grad_w', 'grad_w2_down': 'grad_w', 'grad_g_final': 'grad_w', 'delta_g_ffn1': 'delta_w', 'delta_w1_gate': 'delta_w', 'delta_w1_up': 'delta_w', 'delta_w1_down': 'delta_w', 'delta_g_mix': 'delta_w', 'delta_w_in': 'delta_w', 'delta_ssm_a_re': 'delta_w', 'delta_ssm_a_im': 'delta_w', 'delta_ssm_log_dt': 'delta_w', 'delta_ssm_b_re': 'delta_w', 'delta_ssm_b_im': 'delta_w', 'delta_ssm_c_re': 'delta_w', 'delta_ssm_c_im': 'delta_w', 'delta_ssm_d': 'delta_w', 'delta_w_glu': 'delta_w', 'delta_b_glu': 'delta_w', 'delta_w_pool': 'delta_w', 'delta_pool_scale': 'delta_w', 'delta_g_out_ssm': 'delta_w', 'delta_g_out_pool': 'delta_w', 'delta_w_out': 'delta_w', 'delta_g_xattn': 'delta_w', 'delta_g_mem': 'delta_w', 'delta_w_q': 'delta_w', 'delta_w_k': 'delta_w', 'delta_w_v': 'delta_w', 'delta_w_o': 'delta_w', 'delta_g_ffn2': 'delta_w', 'delta_w2_gate': 'delta_w', 'delta_w2_up': 'delta_w', 'delta_w2_down': 'delta_w', 'delta_g_final': 'delta_w', 'new_m_g_ffn1': 'new_m', 'new_m_w1_gate': 'new_m', 'new_m_w1_up': 'new_m', 'new_m_w1_down': 'new_m', 'new_m_g_mix': 'new_m', 'new_m_w_in': 'new_m', 'new_m_ssm_a_re': 'new_m', 'new_m_ssm_a_im': 'new_m', 'new_m_ssm_log_dt': 'new_m', 'new_m_ssm_b_re': 'new_m', 'new_m_ssm_b_im': 'new_m', 'new_m_ssm_c_re': 'new_m', 'new_m_ssm_c_im': 'new_m', 'new_m_ssm_d': 'new_m', 'new_m_w_glu': 'new_m', 'new_m_b_glu': 'new_m', 'new_m_w_pool': 'new_m', 'new_m_pool_scale': 'new_m', 'new_m_g_out_ssm': 'new_m', 'new_m_g_out_pool': 'new_m', 'new_m_w_out': 'new_m', 'new_m_g_xattn': 'new_m', 'new_m_g_mem': 'new_m', 'new_m_w_q': 'new_m', 'new_m_w_k': 'new_m', 'new_m_w_v': 'new_m', 'new_m_w_o': 'new_m', 'new_m_g_ffn2': 'new_m', 'new_m_w2_gate': 'new_m', 'new_m_w2_up': 'new_m', 'new_m_w2_down': 'new_m', 'new_m_g_final': 'new_m', 'new_v_g_ffn1': 'new_v', 'new_v_w1_gate': 'new_v', 'new_v_w1_up': 'new_v', 'new_v_w1_down': 'new_v', 'new_v_g_mix': 'new_v', 'new_v_w_in': 'new_v', 'new_v_ssm_a_re': 'new_v', 'new_v_ssm_a_im': 'new_v', 'new_v_ssm_log_dt': 'new_v', 'new_v_ssm_b_re': 'new_v', 'new_v_ssm_b_im': 'new_v', 'new_v_ssm_c_re': 'new_v', 'new_v_ssm_c_im': 'new_v', 'new_v_ssm_d': 'new_v', 'new_v_w_glu': 'new_v', 'new_v_b_glu': 'new_v', 'new_v_w_pool': 'new_v', 'new_v_pool_scale': 'new_v', 'new_v_g_out_ssm': 'new_v', 'new_v_g_out_pool': 'new_v', 'new_v_w_out': 'new_v', 'new_v_g_xattn': 'new_v', 'new_v_g_mem': 'new_v', 'new_v_w_q': 'new_v', 'new_v_w_k': 'new_v', 'new_v_w_v': 'new_v', 'new_v_w_o': 'new_v', 'new_v_g_ffn2': 'new_v', 'new_v_w2_gate': 'new_v', 'new_v_w2_up': 'new_v', 'new_v_w2_down': 'new_v', 'new_v_g_final': 'new_v'}


def _forward(args):
    return _fwd_reference(*[args[k] for k in FWD_PARAMS])


def _output_shape():
    def fwd():
        inp = _fwd_setup_inputs(0)
        return _fwd_reference(*[inp[k] for k in FWD_PARAMS])
    out = _jax.eval_shape(fwd)
    return out.shape, out.dtype

N_MICROBATCH = 1
ADAM_LR = 0.001
ADAM_B1 = 0.9
ADAM_B2 = 0.999
ADAM_EPS = 1e-08
ADAM_WD = 0.01
ADAM_STEP = 10
PER_EXAMPLE_BATCH_AXIS = {'x': 0, 'mem': 0, 'loss_target': 0}
SHARED_INPUTS = []
_WEIGHT_DTYPES = {'g_ffn1': _jnp.float32, 'w1_gate': _jnp.float32, 'w1_up': _jnp.float32, 'w1_down': _jnp.float32, 'g_mix': _jnp.float32, 'w_in': _jnp.float32, 'ssm_a_re': _jnp.float32, 'ssm_a_im': _jnp.float32, 'ssm_log_dt': _jnp.float32, 'ssm_b_re': _jnp.float32, 'ssm_b_im': _jnp.float32, 'ssm_c_re': _jnp.float32, 'ssm_c_im': _jnp.float32, 'ssm_d': _jnp.float32, 'w_glu': _jnp.float32, 'b_glu': _jnp.float32, 'w_pool': _jnp.float32, 'pool_scale': _jnp.float32, 'g_out_ssm': _jnp.float32, 'g_out_pool': _jnp.float32, 'w_out': _jnp.float32, 'g_xattn': _jnp.float32, 'g_mem': _jnp.float32, 'w_q': _jnp.float32, 'w_k': _jnp.float32, 'w_v': _jnp.float32, 'w_o': _jnp.float32, 'g_ffn2': _jnp.float32, 'w2_gate': _jnp.float32, 'w2_up': _jnp.float32, 'w2_down': _jnp.float32, 'g_final': _jnp.float32}
MOMENT_SCALE = {'g_ffn1': 3.817084e-02, 'w1_gate': 1.645892e-02, 'w1_up': 1.593974e-02, 'w1_down': 2.645801e-02, 'g_mix': 6.893019e-02, 'w_in': 6.564425e-02, 'ssm_a_re': 2.981056e-03, 'ssm_a_im': 3.175010e-03, 'ssm_log_dt': 1.910485e+00, 'ssm_b_re': 2.073461e-03, 'ssm_b_im': 2.077920e-03, 'ssm_c_re': 4.088619e-03, 'ssm_c_im': 4.064943e-03, 'ssm_d': 7.549524e-02, 'w_glu': 1.863629e-02, 'b_glu': 2.739404e-02, 'w_pool': 6.316194e-02, 'pool_scale': 6.332034e-02, 'g_out_ssm': 6.225891e-02, 'g_out_pool': 6.320554e-02, 'w_out': 6.703160e-02, 'g_xattn': 6.323439e-03, 'g_mem': 9.638141e-03, 'w_q': 6.432039e-03, 'w_k': 6.447859e-03, 'w_v': 6.697091e-03, 'w_o': 6.688667e-03, 'g_ffn2': 2.595761e-02, 'w2_gate': 1.116208e-02, 'w2_up': 1.085505e-02, 'w2_down': 1.799780e-02, 'g_final': 1.603852e+01}


def _to_microbatches(a, axis):
    t = _jnp.moveaxis(a, axis, 0)
    t = t.reshape((N_MICROBATCH, t.shape[0] // N_MICROBATCH) + t.shape[1:])
    return _jnp.moveaxis(t, 1, axis + 1)


def setup_inputs(seed: int = 0) -> dict:
    inp = _fwd_setup_inputs(seed)
    key = _jax.random.fold_in(_jax.random.key(seed), 7919)
    shape, _ = _output_shape()
    out = dict(inp)
    out["loss_target"] = _jax.random.normal(_jax.random.fold_in(key, 0), shape, _jnp.float32)
    for i, name in enumerate(TWIN_WEIGHTS):
        w = inp[name].astype(_jnp.float32)
        if MOMENT_SCALE is None:
            s = _jnp.sqrt(_jnp.mean(_jnp.square(w)) + 1e-30)
        else:
            s = MOMENT_SCALE[name]
        km, kv = _jax.random.split(_jax.random.fold_in(key, i + 1))
        out[name] = w
        out["m_" + name] = s * _jax.random.normal(km, w.shape, _jnp.float32)
        out["v_" + name] = (s * s) * _jax.random.uniform(kv, w.shape, _jnp.float32, 0.5, 1.5)
    if N_MICROBATCH > 1:
        for name, axis in PER_EXAMPLE_BATCH_AXIS.items():
            out[name] = _to_microbatches(out[name], axis)
    return {'x': out['x'], 'mem': out['mem'], 'g_ffn1': out['g_ffn1'], 'w1_gate': out['w1_gate'], 'w1_up': out['w1_up'], 'w1_down': out['w1_down'], 'g_mix': out['g_mix'], 'w_in': out['w_in'], 'ssm_a_re': out['ssm_a_re'], 'ssm_a_im': out['ssm_a_im'], 'ssm_log_dt': out['ssm_log_dt'], 'ssm_b_re': out['ssm_b_re'], 'ssm_b_im': out['ssm_b_im'], 'ssm_c_re': out['ssm_c_re'], 'ssm_c_im': out['ssm_c_im'], 'ssm_d': out['ssm_d'], 'w_glu': out['w_glu'], 'b_glu': out['b_glu'], 'w_pool': out['w_pool'], 'pool_scale': out['pool_scale'], 'g_out_ssm': out['g_out_ssm'], 'g_out_pool': out['g_out_pool'], 'w_out': out['w_out'], 'g_xattn': out['g_xattn'], 'g_mem': out['g_mem'], 'w_q': out['w_q'], 'w_k': out['w_k'], 'w_v': out['w_v'], 'w_o': out['w_o'], 'g_ffn2': out['g_ffn2'], 'w2_gate': out['w2_gate'], 'w2_up': out['w2_up'], 'w2_down': out['w2_down'], 'g_final': out['g_final'], 'loss_target': out['loss_target'], 'm_g_ffn1': out['m_g_ffn1'], 'm_w1_gate': out['m_w1_gate'], 'm_w1_up': out['m_w1_up'], 'm_w1_down': out['m_w1_down'], 'm_g_mix': out['m_g_mix'], 'm_w_in': out['m_w_in'], 'm_ssm_a_re': out['m_ssm_a_re'], 'm_ssm_a_im': out['m_ssm_a_im'], 'm_ssm_log_dt': out['m_ssm_log_dt'], 'm_ssm_b_re': out['m_ssm_b_re'], 'm_ssm_b_im': out['m_ssm_b_im'], 'm_ssm_c_re': out['m_ssm_c_re'], 'm_ssm_c_im': out['m_ssm_c_im'], 'm_ssm_d': out['m_ssm_d'], 'm_w_glu': out['m_w_glu'], 'm_b_glu': out['m_b_glu'], 'm_w_pool': out['m_w_pool'], 'm_pool_scale': out['m_pool_scale'], 'm_g_out_ssm': out['m_g_out_ssm'], 'm_g_out_pool': out['m_g_out_pool'], 'm_w_out': out['m_w_out'], 'm_g_xattn': out['m_g_xattn'], 'm_g_mem': out['m_g_mem'], 'm_w_q': out['m_w_q'], 'm_w_k': out['m_w_k'], 'm_w_v': out['m_w_v'], 'm_w_o': out['m_w_o'], 'm_g_ffn2': out['m_g_ffn2'], 'm_w2_gate': out['m_w2_gate'], 'm_w2_up': out['m_w2_up'], 'm_w2_down': out['m_w2_down'], 'm_g_final': out['m_g_final'], 'v_g_ffn1': out['v_g_ffn1'], 'v_w1_gate': out['v_w1_gate'], 'v_w1_up': out['v_w1_up'], 'v_w1_down': out['v_w1_down'], 'v_g_mix': out['v_g_mix'], 'v_w_in': out['v_w_in'], 'v_ssm_a_re': out['v_ssm_a_re'], 'v_ssm_a_im': out['v_ssm_a_im'], 'v_ssm_log_dt': out['v_ssm_log_dt'], 'v_ssm_b_re': out['v_ssm_b_re'], 'v_ssm_b_im': out['v_ssm_b_im'], 'v_ssm_c_re': out['v_ssm_c_re'], 'v_ssm_c_im': out['v_ssm_c_im'], 'v_ssm_d': out['v_ssm_d'], 'v_w_glu': out['v_w_glu'], 'v_b_glu': out['v_b_glu'], 'v_w_pool': out['v_w_pool'], 'v_pool_scale': out['v_pool_scale'], 'v_g_out_ssm': out['v_g_out_ssm'], 'v_g_out_pool': out['v_g_out_pool'], 'v_w_out': out['v_w_out'], 'v_g_xattn': out['v_g_xattn'], 'v_g_mem': out['v_g_mem'], 'v_w_q': out['v_w_q'], 'v_w_k': out['v_w_k'], 'v_w_v': out['v_w_v'], 'v_w_o': out['v_w_o'], 'v_g_ffn2': out['v_g_ffn2'], 'v_w2_gate': out['v_w2_gate'], 'v_w2_up': out['v_w2_up'], 'v_w2_down': out['v_w2_down'], 'v_g_final': out['v_g_final']}


def _loss(weights, diff, rest, loss_target):
    with _jax.named_scope("forward"):
        args = {**rest, TWIN_DIFF_INPUT: diff, **{k: w.astype(_WEIGHT_DTYPES[k]) for k, w in weights.items()}}
        y = _forward(args)
    with _jax.named_scope("loss_head"):
        err = _jnp.square(y.astype(_jnp.float32) - loss_target)
        return 0.5 * _jnp.sum(_jnp.mean(err, axis=-1)) if err.ndim else 0.5 * err


def _adamw(w, g, m, v):
    m = ADAM_B1 * m + (1.0 - ADAM_B1) * g
    v = ADAM_B2 * v + (1.0 - ADAM_B2) * _jnp.square(g)
    m_hat = m / (1.0 - ADAM_B1 ** ADAM_STEP)
    v_hat = v / (1.0 - ADAM_B2 ** ADAM_STEP)
    delta = -ADAM_LR * (m_hat / (_jnp.sqrt(v_hat) + ADAM_EPS) + ADAM_WD * w)
    return delta, m, v


def reference(x, mem, g_ffn1, w1_gate, w1_up, w1_down, g_mix, w_in, ssm_a_re, ssm_a_im, ssm_log_dt, ssm_b_re, ssm_b_im, ssm_c_re, ssm_c_im, ssm_d, w_glu, b_glu, w_pool, pool_scale, g_out_ssm, g_out_pool, w_out, g_xattn, g_mem, w_q, w_k, w_v, w_o, g_ffn2, w2_gate, w2_up, w2_down, g_final, loss_target, m_g_ffn1, m_w1_gate, m_w1_up, m_w1_down, m_g_mix, m_w_in, m_ssm_a_re, m_ssm_a_im, m_ssm_log_dt, m_ssm_b_re, m_ssm_b_im, m_ssm_c_re, m_ssm_c_im, m_ssm_d, m_w_glu, m_b_glu, m_w_pool, m_pool_scale, m_g_out_ssm, m_g_out_pool, m_w_out, m_g_xattn, m_g_mem, m_w_q, m_w_k, m_w_v, m_w_o, m_g_ffn2, m_w2_gate, m_w2_up, m_w2_down, m_g_final, v_g_ffn1, v_w1_gate, v_w1_up, v_w1_down, v_g_mix, v_w_in, v_ssm_a_re, v_ssm_a_im, v_ssm_log_dt, v_ssm_b_re, v_ssm_b_im, v_ssm_c_re, v_ssm_c_im, v_ssm_d, v_w_glu, v_b_glu, v_w_pool, v_pool_scale, v_g_out_ssm, v_g_out_pool, v_w_out, v_g_xattn, v_g_mem, v_w_q, v_w_k, v_w_v, v_w_o, v_g_ffn2, v_w2_gate, v_w2_up, v_w2_down, v_g_final):
    given = dict(x=x, mem=mem, g_ffn1=g_ffn1, w1_gate=w1_gate, w1_up=w1_up, w1_down=w1_down, g_mix=g_mix, w_in=w_in, ssm_a_re=ssm_a_re, ssm_a_im=ssm_a_im, ssm_log_dt=ssm_log_dt, ssm_b_re=ssm_b_re, ssm_b_im=ssm_b_im, ssm_c_re=ssm_c_re, ssm_c_im=ssm_c_im, ssm_d=ssm_d, w_glu=w_glu, b_glu=b_glu, w_pool=w_pool, pool_scale=pool_scale, g_out_ssm=g_out_ssm, g_out_pool=g_out_pool, w_out=w_out, g_xattn=g_xattn, g_mem=g_mem, w_q=w_q, w_k=w_k, w_v=w_v, w_o=w_o, g_ffn2=g_ffn2, w2_gate=w2_gate, w2_up=w2_up, w2_down=w2_down, g_final=g_final, loss_target=loss_target, m_g_ffn1=m_g_ffn1, m_w1_gate=m_w1_gate, m_w1_up=m_w1_up, m_w1_down=m_w1_down, m_g_mix=m_g_mix, m_w_in=m_w_in, m_ssm_a_re=m_ssm_a_re, m_ssm_a_im=m_ssm_a_im, m_ssm_log_dt=m_ssm_log_dt, m_ssm_b_re=m_ssm_b_re, m_ssm_b_im=m_ssm_b_im, m_ssm_c_re=m_ssm_c_re, m_ssm_c_im=m_ssm_c_im, m_ssm_d=m_ssm_d, m_w_glu=m_w_glu, m_b_glu=m_b_glu, m_w_pool=m_w_pool, m_pool_scale=m_pool_scale, m_g_out_ssm=m_g_out_ssm, m_g_out_pool=m_g_out_pool, m_w_out=m_w_out, m_g_xattn=m_g_xattn, m_g_mem=m_g_mem, m_w_q=m_w_q, m_w_k=m_w_k, m_w_v=m_w_v, m_w_o=m_w_o, m_g_ffn2=m_g_ffn2, m_w2_gate=m_w2_gate, m_w2_up=m_w2_up, m_w2_down=m_w2_down, m_g_final=m_g_final, v_g_ffn1=v_g_ffn1, v_w1_gate=v_w1_gate, v_w1_up=v_w1_up, v_w1_down=v_w1_down, v_g_mix=v_g_mix, v_w_in=v_w_in, v_ssm_a_re=v_ssm_a_re, v_ssm_a_im=v_ssm_a_im, v_ssm_log_dt=v_ssm_log_dt, v_ssm_b_re=v_ssm_b_re, v_ssm_b_im=v_ssm_b_im, v_ssm_c_re=v_ssm_c_re, v_ssm_c_im=v_ssm_c_im, v_ssm_d=v_ssm_d, v_w_glu=v_w_glu, v_b_glu=v_b_glu, v_w_pool=v_w_pool, v_pool_scale=v_pool_scale, v_g_out_ssm=v_g_out_ssm, v_g_out_pool=v_g_out_pool, v_w_out=v_w_out, v_g_xattn=v_g_xattn, v_g_mem=v_g_mem, v_w_q=v_w_q, v_w_k=v_w_k, v_w_v=v_w_v, v_w_o=v_w_o, v_g_ffn2=v_g_ffn2, v_w2_gate=v_w2_gate, v_w2_up=v_w2_up, v_w2_down=v_w2_down, v_g_final=v_g_final)
    weights = {n: given[n] for n in TWIN_WEIGHTS}
    shared = {n: given[n] for n in SHARED_INPUTS}
    per_example = {n: given[n] for n in ['x', 'mem']}
    grad_fn = _jax.value_and_grad(_loss, argnums=(0, 1))

    def one_microbatch(ex, loss_target):
        ex = dict(ex)
        diff = ex.pop(TWIN_DIFF_INPUT)
        return grad_fn(weights, diff, {**shared, **ex}, loss_target)

    if N_MICROBATCH == 1:
        loss, (grad_w, grad_x) = one_microbatch(per_example, given["loss_target"])
    else:
        def body(carry, xs):
            loss_sum, grad_sum = carry
            l_k, (gw_k, gx_k) = one_microbatch(xs[0], xs[1])
            with _jax.named_scope("update"):
                return (loss_sum + l_k, _jax.tree.map(_jnp.add, grad_sum, gw_k)), gx_k

        init = (_jnp.zeros((), _jnp.float32), _jax.tree.map(_jnp.zeros_like, weights))
        (loss, grad_w), grad_x = _jax.lax.scan(body, init, (per_example, given["loss_target"]))
    with _jax.named_scope("update"):
        delta_w, new_m, new_v = {}, {}, {}
        for n in TWIN_WEIGHTS:
            delta_w[n], new_m[n], new_v[n] = _adamw(weights[n], grad_w[n], given["m_" + n], given["v_" + n])
    return (loss, grad_x, *[grad_w[n] for n in TWIN_WEIGHTS], *[delta_w[n] for n in TWIN_WEIGHTS],
            *[new_m[n] for n in TWIN_WEIGHTS], *[new_v[n] for n in TWIN_WEIGHTS])
```

```python
import functools
import math

import jax
import jax.numpy as jnp
from jax import lax
from jax.experimental import pallas as pl
from jax.experimental.pallas import tpu as pltpu

F32 = jnp.float32
BF16 = jnp.bfloat16
EPS = 1e-6
ADAM_LR, ADAM_B1, ADAM_B2, ADAM_EPS, ADAM_WD, ADAM_STEP = 0.001, 0.9, 0.999, 1e-08, 0.01, 10
POOL_WINDOWS = (2, 4, 8, 16)
SSM_GROUP = 16
S5_GROUPS_PER_BLOCK = 16
S5_LANES = 8
MEM_HEADS = 4
N_CHIPS = 4
VMEM_LIMIT_V7X = 56 * 1024 * 1024
MESH = pl.DeviceIdType.MESH

WEIGHTS = ['g_ffn1', 'w1_gate', 'w1_up', 'w1_down', 'g_mix', 'w_in', 'ssm_a_re', 'ssm_a_im', 'ssm_log_dt',
           'ssm_b_re', 'ssm_b_im', 'ssm_c_re', 'ssm_c_im', 'ssm_d', 'w_glu', 'b_glu', 'w_pool', 'pool_scale',
           'g_out_ssm', 'g_out_pool', 'w_out', 'g_xattn', 'g_mem', 'w_q', 'w_k', 'w_v', 'w_o', 'g_ffn2',
           'w2_gate', 'w2_up', 'w2_down', 'g_final']
BIG = ['w1_gate', 'w1_up', 'w1_down', 'w_in', 'w_glu', 'w_pool', 'w_out', 'w_q', 'w_k', 'w_v', 'w_o',
       'w2_gate', 'w2_up', 'w2_down']
SMALL = [n for n in WEIGHTS if n not in BIG]


def _tile(n, target, mult=8):
    best = None
    for d in range(1, n + 1):
        if n % d == 0 and d <= target and d % mult == 0:
            best = d
    return best if best is not None else n


def _params(sem=None):
    if sem is None:
        return pltpu.CompilerParams(vmem_limit_bytes=VMEM_LIMIT_V7X)
    return pltpu.CompilerParams(dimension_semantics=sem, vmem_limit_bytes=VMEM_LIMIT_V7X)


def _sigmoid(x):
    return 1.0 / (1.0 + jnp.exp(-x))


def _rms_fwd(x, g):
    r = lax.rsqrt(jnp.mean(x * x, axis=-1, keepdims=True) + EPS)
    return x * r * g


def _rms_bwd(dy, x, g):
    r = lax.rsqrt(jnp.mean(x * x, axis=-1, keepdims=True) + EPS)
    dxh = dy * g
    dx = r * dxh - x * (r * r * r) * jnp.mean(dxh * x, axis=-1, keepdims=True)
    return dx, dy * x * r


def _colsum(v):
    return jnp.sum(v, axis=0, keepdims=True)


def _rowwise(name, fn, rows, vecs, out_defs, red_defs=(), tm=256):
    rows = [r if isinstance(r, tuple) else (r, 0, r.shape[1]) for r in rows]
    t_rows = rows[0][0].shape[0]
    tm = _tile(t_rows, tm)
    nr, nv, no, nd = len(rows), len(vecs), len(out_defs), len(red_defs)

    def body(*refs):
        r, v = refs[:nr], refs[nr:nr + nv]
        o, d = refs[nr + nv:nr + nv + no], refs[nr + nv + no:]
        outs, reds = fn([x[...] for x in r], [x[...] for x in v])
        for ref, val in zip(o, outs):
            ref[...] = val.astype(ref.dtype)
        if nd:
            @pl.when(pl.program_id(0) == 0)
            def _():
                for ref in d:
                    ref[...] = jnp.zeros(ref.shape, ref.dtype)
            for ref, val in zip(d, reds):
                ref[...] += val

    in_specs = [pl.BlockSpec((tm, w), functools.partial(lambda i, cb: (i, cb), cb=cb)) for (_, cb, w) in rows]
    in_specs += [pl.BlockSpec(v.shape, lambda i: (0, 0)) for v in vecs]
    out_specs = [pl.BlockSpec((tm, w), lambda i: (i, 0)) for (w, _) in out_defs]
    out_specs += [pl.BlockSpec((1, w), lambda i: (0, 0)) for w in red_defs]
    out_shape = [jax.ShapeDtypeStruct((t_rows, w), dt) for (w, dt) in out_defs]
    out_shape += [jax.ShapeDtypeStruct((1, w), F32) for w in red_defs]
    res = pl.pallas_call(
        body, name=name, grid=(t_rows // tm,), in_specs=in_specs, out_specs=out_specs, out_shape=out_shape,
        compiler_params=_params(("arbitrary",)),
    )(*[r[0] for r in rows], *vecs)
    return res


def _rmsnorm(name, x, g, tm=256):
    return _rowwise(name, lambda r, v: ([_rms_fwd(r[0].astype(F32), v[0])], []), [x], [g],
                    [(x.shape[1], BF16)], tm=tm)[0]


def _mm_nn(name, a, b, b_block, b_idx, nk, n_out, out_dtype, res=None, alpha=1.0, tm=512):
    t_rows = a.shape[0]
    bk, tn = b_block[-2], b_block[-1]
    tm = _tile(t_rows, tm)
    nj = n_out // tn
    has_res = res is not None

    def body(*refs):
        a_ref, b_ref = refs[0], refs[1]
        res_ref = refs[2] if has_res else None
        o_ref = refs[2 + has_res]
        acc_ref = refs[3 + has_res]
        k = pl.program_id(2)
        p = jnp.dot(a_ref[...], b_ref[...], preferred_element_type=F32)

        @pl.when(k == 0)
        def _():
            acc_ref[...] = p

        @pl.when(k > 0)
        def _():
            acc_ref[...] += p

        @pl.when(k == nk - 1)
        def _():
            r = acc_ref[...]
            if has_res:
                r = res_ref[...] + alpha * r
            o_ref[...] = r.astype(o_ref.dtype)

    in_specs = [pl.BlockSpec((tm, bk), lambda j, i, k: (i, k)),
                pl.BlockSpec(b_block, lambda j, i, k: b_idx(j, k))]
    args = [a, b]
    if has_res:
        in_specs.append(pl.BlockSpec((tm, tn), lambda j, i, k: (i, j)))
        args.append(res)
    return pl.pallas_call(
        body, name=name, grid=(nj, t_rows // tm, nk), in_specs=in_specs,
        out_specs=pl.BlockSpec((tm, tn), lambda j, i, k: (i, j)),
        out_shape=jax.ShapeDtypeStruct((t_rows, n_out), out_dtype),
        scratch_shapes=[pltpu.VMEM((tm, tn), F32)],
        compiler_params=_params(("arbitrary", "arbitrary", "arbitrary")),
    )(*args)


def _dot_nt(x, w):
    return lax.dot_general(x, w, (((1,), (1,)), ((), ())), preferred_element_type=F32)


def _dot_tn(x, y):
    return lax.dot_general(x, y, (((0,), (0,)), ((), ())), preferred_element_type=F32)


def _mm_nt_cols(name, pairs, ns, bn, out_defs, epi=None, extras=(), tm=512):
    t_rows = pairs[0][0].shape[0]
    tm = _tile(t_rows, tm)
    npair, nex, no = len(pairs), len(extras), len(out_defs)

    def body(*refs):
        acc = None
        for p in range(npair):
            part = _dot_nt(refs[2 * p][...], refs[2 * p + 1][...])
            acc = part if acc is None else acc + part
        ex = [r[...] for r in refs[2 * npair:2 * npair + nex]]
        outs = epi(acc, *ex) if epi is not None else (acc,)
        for ref, val in zip(refs[2 * npair + nex:], outs):
            ref[...] = val.astype(ref.dtype)

    in_specs, args = [], []
    for (dy, w, w_block, w_idx) in pairs:
        in_specs.append(pl.BlockSpec((tm, dy.shape[1]), lambda s, i: (i, 0)))
        in_specs.append(pl.BlockSpec(w_block, functools.partial(lambda s, i, f: f(s), f=w_idx)))
        args += [dy, w]
    for e in extras:
        in_specs.append(pl.BlockSpec((tm, bn), lambda s, i: (i, s)))
        args.append(e)
    res = pl.pallas_call(
        body, name=name, grid=(ns, t_rows // tm), in_specs=in_specs,
        out_specs=[pl.BlockSpec((tm, bn), lambda s, i: (i, s)) for _ in range(no)],
        out_shape=[jax.ShapeDtypeStruct((t_rows, ns * bn), dt) for dt in out_defs],
        compiler_params=_params(("arbitrary", "arbitrary")),
    )(*args)
    return res


def _mm_nt_k(name, pairs, ns, n_out, out_dtype, tm=512):
    t_rows = pairs[0][0].shape[0]
    tm = _tile(t_rows, tm)
    npair = len(pairs)

    def body(*refs):
        o_ref, acc_ref = refs[2 * npair], refs[2 * npair + 1]
        s = pl.program_id(1)
        acc = None
        for p in range(npair):
            part = _dot_nt(refs[2 * p][...], refs[2 * p + 1][...])
            acc = part if acc is None else acc + part

        @pl.when(s == 0)
        def _():
            acc_ref[...] = acc

        @pl.when(s > 0)
        def _():
            acc_ref[...] += acc

        @pl.when(s == ns - 1)
        def _():
            o_ref[...] = acc_ref[...].astype(o_ref.dtype)

    in_specs, args = [], []
    for (a, w, w_block, w_idx) in pairs:
        in_specs.append(pl.BlockSpec((tm, w_block[-1]), lambda i, s: (i, s)))
        in_specs.append(pl.BlockSpec(w_block, functools.partial(lambda i, s, f: f(s), f=w_idx)))
        args += [a, w]
    return pl.pallas_call(
        body, name=name, grid=(t_rows // tm, ns), in_specs=in_specs,
        out_specs=pl.BlockSpec((tm, n_out), lambda i, s: (i, 0)),
        out_shape=jax.ShapeDtypeStruct((t_rows, n_out), out_dtype),
        scratch_shapes=[pltpu.VMEM((tm, n_out), F32)],
        compiler_params=_params(("arbitrary", "arbitrary")),
    )(*args)


def _mm_tn(name, a, b, bk, bn, n_p, n_q, out_shape, out_block, out_idx, into=None, a_off=0, b_off=0, tt=512):
    t_rows = a.shape[0]
    tt = _tile(t_rows, tt, 16)
    nt = t_rows // tt
    has_into = into is not None

    def body(*refs):
        a_ref, b_ref = refs[0], refs[1]
        o_ref, acc_ref = refs[2 + has_into], refs[3 + has_into]
        t = pl.program_id(2)
        part = _dot_tn(a_ref[...], b_ref[...])

        @pl.when(t == 0)
        def _():
            acc_ref[...] = part

        @pl.when(t > 0)
        def _():
            acc_ref[...] += part

        @pl.when(t == nt - 1)
        def _():
            o_ref[...] = acc_ref[...].astype(o_ref.dtype)

    in_specs = [pl.BlockSpec((tt, bk), lambda p, q, t: (t, p + a_off)),
                pl.BlockSpec((tt, bn), lambda p, q, t: (t, q + b_off))]
    args = [a, b]
    aliases = {}
    if has_into:
        in_specs.append(pl.BlockSpec(memory_space=pl.ANY))
        args.append(into)
        aliases = {2: 0}
        out_shape = jax.ShapeDtypeStruct(into.shape, into.dtype)
    return pl.pallas_call(
        body, name=name, grid=(n_p, n_q, nt), in_specs=in_specs,
        out_specs=pl.BlockSpec(out_block, lambda p, q, t: out_idx(p, q)),
        out_shape=out_shape, scratch_shapes=[pltpu.VMEM((bk, bn), F32)],
        input_output_aliases=aliases,
        compiler_params=_params(("arbitrary", "arbitrary", "arbitrary")),
    )(*args)


def _ffn_up(name, n, ga, half, d_model, fs, tm=512):
    t_rows = n.shape[0]
    tm = _tile(t_rows, tm)

    def body(n_ref, wg_ref, wu_ref, a_ref, b_ref, h_ref):
        x = n_ref[...]
        a = jnp.dot(x, wg_ref[...], preferred_element_type=F32)
        b = jnp.dot(x, wu_ref[...], preferred_element_type=F32)
        a_ref[...] = a.astype(a_ref.dtype)
        b_ref[...] = b.astype(b_ref.dtype)
        h_ref[...] = (a * _sigmoid(a) * b).astype(h_ref.dtype)

    w_block = (None, None, d_model, fs)
    out = jax.ShapeDtypeStruct((t_rows, N_CHIPS * fs), BF16)
    return pl.pallas_call(
        body, name=name, grid=(N_CHIPS, t_rows // tm),
        in_specs=[pl.BlockSpec((tm, d_model), lambda s, i: (i, 0)),
                  pl.BlockSpec(w_block, lambda s, i: (s, half, 0, 0)),
                  pl.BlockSpec(w_block, lambda s, i: (s, half, 1, 0))],
        out_specs=[pl.BlockSpec((tm, fs), lambda s, i: (i, s))] * 3,
        out_shape=[out, out, out],
        compiler_params=_params(("arbitrary", "arbitrary")),
    )(n, ga, ga)


def _swiglu_bwd(dh, a, b):
    a = a.astype(F32)
    b = b.astype(F32)
    sg = _sigmoid(a)
    return dh * b * sg * (1.0 + a * (1.0 - sg)), dh * a * sg


def _attn_fwd(q, k, v, tm=512):
    t_rows, d_model = q.shape
    n_mem = k.shape[0]
    hd = d_model // MEM_HEADS
    scale = hd ** -0.5
    tm = _tile(t_rows, tm)

    def body(q_ref, k_ref, v_ref, o_ref):
        for h in range(MEM_HEADS):
            cols = slice(h * hd, (h + 1) * hd)
            s = _dot_nt(q_ref[:, cols], k_ref[:, cols]) * scale
            s = s - jnp.max(s, axis=-1, keepdims=True)
            e = jnp.exp(s)
            p = e / jnp.sum(e, axis=-1, keepdims=True)
            o_ref[:, cols] = jnp.dot(p.astype(BF16), v_ref[:, cols], preferred_element_type=F32).astype(o_ref.dtype)

    return pl.pallas_call(
        body, name="attn_fwd", grid=(t_rows // tm,),
        in_specs=[pl.BlockSpec((tm, d_model), lambda i: (i, 0)),
                  pl.BlockSpec((n_mem, d_model), lambda i: (0, 0)),
                  pl.BlockSpec((n_mem, d_model), lambda i: (0, 0))],
        out_specs=pl.BlockSpec((tm, d_model), lambda i: (i, 0)),
        out_shape=jax.ShapeDtypeStruct((t_rows, d_model), BF16),
        compiler_params=_params(("arbitrary",)),
    )(q, k, v)


def _attn_bwd(q, k, v, do, tm=512):
    t_rows, d_model = q.shape
    n_mem = k.shape[0]
    hd = d_model // MEM_HEADS
    scale = hd ** -0.5
    tm = _tile(t_rows, tm, 16)

    def body(q_ref, k_ref, v_ref, do_ref, dq_ref, dk_ref, dv_ref):
        @pl.when(pl.program_id(0) == 0)
        def _():
            dk_ref[...] = jnp.zeros(dk_ref.shape, F32)
            dv_ref[...] = jnp.zeros(dv_ref.shape, F32)

        for h in range(MEM_HEADS):
            cols = slice(h * hd, (h + 1) * hd)
            qh, kh, vh, doh = q_ref[:, cols], k_ref[:, cols], v_ref[:, cols], do_ref[:, cols]
            s = _dot_nt(qh, kh) * scale
            s = s - jnp.max(s, axis=-1, keepdims=True)
            e = jnp.exp(s)
            p = e / jnp.sum(e, axis=-1, keepdims=True)
            dv_ref[:, cols] += _dot_tn(p.astype(BF16), doh)
            dp = _dot_nt(doh, vh)
            ds = (p * (dp - jnp.sum(dp * p, axis=-1, keepdims=True)) * scale).astype(BF16)
            dq_ref[:, cols] = jnp.dot(ds, kh, preferred_element_type=F32).astype(dq_ref.dtype)
            dk_ref[:, cols] += _dot_tn(ds, qh)

    full = pl.BlockSpec((n_mem, d_model), lambda i: (0, 0))
    tile = pl.BlockSpec((tm, d_model), lambda i: (i, 0))
    return pl.pallas_call(
        body, name="attn_bwd", grid=(t_rows // tm,),
        in_specs=[tile, full, full, tile], out_specs=[tile, full, full],
        out_shape=[jax.ShapeDtypeStruct((t_rows, d_model), BF16),
                   jax.ShapeDtypeStruct((n_mem, d_model), F32), jax.ShapeDtypeStruct((n_mem, d_model), F32)],
        compiler_params=_params(("arbitrary",)),
    )(q, k, v, do)


def _split_bf16(v):
    hi = v.astype(BF16)
    return hi, (v - hi.astype(F32)).astype(BF16)


def _pool_window(g):
    return jnp.left_shift(jnp.int32(POOL_WINDOWS[0]), g)


def _pool_fwd(u, col_off, w_pool, scale, tt=256):
    t_rows = u.shape[0]
    pw = w_pool.shape[-1]
    ng = w_pool.shape[0]
    tt = _tile(t_rows, tt, 16)
    nt = t_rows // tt
    assert POOL_WINDOWS == tuple(2 << i for i in range(ng)) and tt >= POOL_WINDOWS[-1]

    def body(vc_ref, vp_ref, w_ref, sc_ref, pooled_ref, z_ref):
        g, i = pl.program_id(0), pl.program_id(1)
        w = _pool_window(g)
        r = lax.broadcasted_iota(jnp.int32, (tt, tt), 0)
        c = lax.broadcasted_iota(jnp.int32, (tt, tt), 1)
        band_c = ((c <= r) & (c > r - w)).astype(BF16)
        band_p = (c > r - w + tt).astype(BF16)
        vc = vc_ref[...]
        ch, cl = _split_bf16(vc)
        ph, plo = _split_bf16(vp_ref[...] * (i > 0).astype(F32))
        sums = (jnp.dot(band_c, ch, preferred_element_type=F32) + jnp.dot(band_c, cl, preferred_element_type=F32)
                + jnp.dot(band_p, ph, preferred_element_type=F32) + jnp.dot(band_p, plo, preferred_element_type=F32))
        t = i * tt + lax.broadcasted_iota(jnp.int32, (tt, 1), 0)
        cnt = jnp.minimum(t + 1, w).astype(F32)
        pooled = (sums / cnt - vc).astype(BF16)
        pooled_ref[...] = pooled
        z_ref[...] = jnp.dot(pooled, w_ref[...], preferred_element_type=F32) * sc_ref[...]

    return pl.pallas_call(
        body, name="pool_fwd", grid=(ng, nt),
        in_specs=[pl.BlockSpec((tt, pw), lambda g, i: (i, col_off + g)),
                  pl.BlockSpec((tt, pw), lambda g, i: (jnp.maximum(i - 1, 0), col_off + g)),
                  pl.BlockSpec((None, pw, pw), lambda g, i: (g, 0, 0)),
                  pl.BlockSpec((1, pw), lambda g, i: (0, g))],
        out_specs=[pl.BlockSpec((tt, pw), lambda g, i: (i, g))] * 2,
        out_shape=[jax.ShapeDtypeStruct((t_rows, ng * pw), BF16), jax.ShapeDtypeStruct((t_rows, ng * pw), F32)],
        compiler_params=_params(("arbitrary", "arbitrary")),
    )(u, u, w_pool, scale)


def _pool_bwd1(dz, pooled, w_pool, scale, tt=256):
    t_rows = dz.shape[0]
    pw = w_pool.shape[-1]
    ng = w_pool.shape[0]
    tt = _tile(t_rows, tt, 16)
    nt = t_rows // tt

    def body(dz_ref, p_ref, w_ref, sc_ref, dp_ref, dw_ref, dsc_ref):
        g, i = pl.program_id(0), pl.program_id(1)
        w = _pool_window(g)

        @pl.when(i == 0)
        def _():
            dw_ref[...] = jnp.zeros(dw_ref.shape, F32)
            dsc_ref[...] = jnp.zeros(dsc_ref.shape, F32)

        dz_v = dz_ref[...]
        pooled = p_ref[...]
        zpre = jnp.dot(pooled, w_ref[...], preferred_element_type=F32)
        dsc_ref[...] += _colsum(dz_v * zpre)
        dzs = (dz_v * sc_ref[...]).astype(BF16)
        dw_ref[...] += _dot_tn(pooled, dzs)
        t = i * tt + lax.broadcasted_iota(jnp.int32, (tt, 1), 0)
        cnt = jnp.minimum(t + 1, w).astype(F32)
        dp_ref[...] = _dot_nt(dzs, w_ref[...]) / cnt

    return pl.pallas_call(
        body, name="pool_bwd1", grid=(ng, nt),
        in_specs=[pl.BlockSpec((tt, pw), lambda g, i: (i, g)),
                  pl.BlockSpec((tt, pw), lambda g, i: (i, g)),
                  pl.BlockSpec((None, pw, pw), lambda g, i: (g, 0, 0)),
                  pl.BlockSpec((1, pw), lambda g, i: (0, g))],
        out_specs=[pl.BlockSpec((tt, pw), lambda g, i: (i, g)),
                   pl.BlockSpec((None, pw, pw), lambda g, i: (g, 0, 0)),
                   pl.BlockSpec((1, pw), lambda g, i: (0, g))],
        out_shape=[jax.ShapeDtypeStruct((t_rows, ng * pw), F32), jax.ShapeDtypeStruct((ng, pw, pw), F32),
                   jax.ShapeDtypeStruct((1, ng * pw), F32)],
        compiler_params=_params(("arbitrary", "arbitrary")),
    )(dz, pooled, w_pool, scale)


def _pool_bwd2(dps, ng, tt=256):
    t_rows, width = dps.shape
    pw = width // ng
    tt = _tile(t_rows, tt, 16)
    nt = t_rows // tt

    def body(dc_ref, dn_ref, dv_ref):
        g, i = pl.program_id(0), pl.program_id(1)
        w = _pool_window(g)
        r = lax.broadcasted_iota(jnp.int32, (tt, tt), 0)
        c = lax.broadcasted_iota(jnp.int32, (tt, tt), 1)
        band_c = ((c >= r) & (c < r + w)).astype(BF16)
        band_n = (c < r + w - tt).astype(BF16)
        dc = dc_ref[...]
        ch, cl = _split_bf16(dc)
        nh, nl = _split_bf16(dn_ref[...] * (i < nt - 1).astype(F32))
        sums = (jnp.dot(band_c, ch, preferred_element_type=F32) + jnp.dot(band_c, cl, preferred_element_type=F32)
                + jnp.dot(band_n, nh, preferred_element_type=F32) + jnp.dot(band_n, nl, preferred_element_type=F32))
        t = i * tt + lax.broadcasted_iota(jnp.int32, (tt, 1), 0)
        cnt = jnp.minimum(t + 1, w).astype(F32)
        dv_ref[...] = sums - dc * cnt

    return pl.pallas_call(
        body, name="pool_bwd2", grid=(ng, nt),
        in_specs=[pl.BlockSpec((tt, pw), lambda g, i: (i, g)),
                  pl.BlockSpec((tt, pw), lambda g, i: (jnp.minimum(i + 1, nt - 1), g))],
        out_specs=pl.BlockSpec((tt, pw), lambda g, i: (i, g)),
        out_shape=jax.ShapeDtypeStruct((t_rows, width), F32),
        compiler_params=_params(("arbitrary", "arbitrary")),
    )(dps, dps)


def _cpow(ar, ai, n):
    rr, ri, br, bi = None, None, ar, ai
    while n:
        if n & 1:
            rr, ri = (br, bi) if rr is None else (rr * br - ri * bi, rr * bi + ri * br)
        n >>= 1
        if n:
            br, bi = br * br - bi * bi, 2.0 * br * bi
    return rr, ri


def _chunk_carries(st_re, st_im, pr, pi, order):
    cb = st_re.shape[1]
    sub = lax.broadcasted_iota(jnp.int32, (S5_LANES, cb), 0)
    cr = jnp.zeros((S5_LANES, cb), F32)
    ci = jnp.zeros((S5_LANES, cb), F32)
    prev_r = jnp.zeros((1, cb), F32)
    prev_i = jnp.zeros((1, cb), F32)
    for k, src in order:
        er, ei = st_re[src:src + 1, :], st_im[src:src + 1, :]
        nr = er + pr * prev_r - pi * prev_i
        ni = ei + pr * prev_i + pi * prev_r
        cr = jnp.where(sub == k, jnp.broadcast_to(nr, (S5_LANES, cb)), cr)
        ci = jnp.where(sub == k, jnp.broadcast_to(ni, (S5_LANES, cb)), ci)
        prev_r, prev_i = nr, ni
    return cr, ci


def _s5_fwd(up, bblk, ab, cblk, tt=128):
    n_rows, ws = up.shape
    nb, cw, cb2 = bblk.shape
    cb = cb2 // 2
    lc = n_rows // S5_LANES
    tt = _tile(lc, tt, 1)
    nt = lc // tt
    rt = S5_LANES * tt

    def body(u_ref, b_ref, ab_ref, c_ref, y_ref, s_ref, bu_ref, st_re, st_im):
        ps, ti = pl.program_id(1), pl.program_id(2)
        ar = jnp.broadcast_to(ab_ref[0:1, :], (S5_LANES, cb))
        ai = jnp.broadcast_to(ab_ref[1:2, :], (S5_LANES, cb))

        @pl.when((ps == 0) & (ti == 0))
        def _():
            st_re[...] = jnp.zeros(st_re.shape, F32)
            st_im[...] = jnp.zeros(st_im.shape, F32)

        @pl.when((ps == 1) & (ti == 0))
        def _():
            pr, pi = _cpow(ab_ref[0:1, :], ab_ref[1:2, :], lc)
            cr, ci = _chunk_carries(st_re, st_im, pr, pi, [(k, k - 1) for k in range(1, S5_LANES)])
            st_re[...] = cr
            st_im[...] = ci

        bu_ref[...] = jnp.dot(u_ref[...], b_ref[...], preferred_element_type=F32)

        def step(t, carry, store):
            sr, si = carry
            rows = pl.ds(pl.multiple_of(t * S5_LANES, S5_LANES), S5_LANES)
            nr = ar * sr - ai * si + bu_ref[rows, 0:cb]
            ni = ar * si + ai * sr + bu_ref[rows, cb:cb2]
            if store:
                s_ref[rows, 0:cb] = nr
                s_ref[rows, cb:cb2] = ni
            return nr, ni

        @pl.when(ps == 0)
        def _():
            sr, si = lax.fori_loop(0, tt, functools.partial(step, store=False), (st_re[...], st_im[...]))
            st_re[...] = sr
            st_im[...] = si

        @pl.when(ps == 1)
        def _():
            sr, si = lax.fori_loop(0, tt, functools.partial(step, store=True), (st_re[...], st_im[...]))
            st_re[...] = sr
            st_im[...] = si
            y_ref[...] = jnp.dot(s_ref[...].astype(BF16), c_ref[...], preferred_element_type=F32)

    return pl.pallas_call(
        body, name="s5_fwd", grid=(nb, 2, nt),
        in_specs=[pl.BlockSpec((rt, cw), lambda j, ps, ti: (ti, j)),
                  pl.BlockSpec((None, cw, cb2), lambda j, ps, ti: (j, 0, 0)),
                  pl.BlockSpec((None, 2, cb), lambda j, ps, ti: (j, 0, 0)),
                  pl.BlockSpec((None, cb2, cw), lambda j, ps, ti: (j, 0, 0))],
        out_specs=[pl.BlockSpec((rt, cw), lambda j, ps, ti: (ti * ps, j)),
                   pl.BlockSpec((None, rt, cb2), lambda j, ps, ti: (j, ti * ps, 0))],
        out_shape=[jax.ShapeDtypeStruct((n_rows, ws), F32), jax.ShapeDtypeStruct((nb, n_rows, cb2), F32)],
        scratch_shapes=[pltpu.VMEM((rt, cb2), F32), pltpu.VMEM((S5_LANES, cb), F32), pltpu.VMEM((S5_LANES, cb), F32)],
        compiler_params=_params(("arbitrary", "arbitrary", "arbitrary")),
    )(up, bblk, ab, cblk)


def _s5_bwd(dyp, up, s_all, bblk_t, ab, cblk_t, tt=128):
    n_rows, ws = up.shape
    nb, cb2, cw = bblk_t.shape
    cb = cb2 // 2
    lc = n_rows // S5_LANES
    tt = _tile(lc, tt, 1)
    nt = lc // tt
    rt = S5_LANES * tt

    def body(dy_ref, u_ref, s_ref, bt_ref, ab_ref, ct_ref, du_ref, db_ref, dc_ref, da_ref, ds_ref, st_re, st_im):
        ps, ti = pl.program_id(1), pl.program_id(2)
        ar = jnp.broadcast_to(ab_ref[0:1, :], (S5_LANES, cb))
        ai = jnp.broadcast_to(ab_ref[1:2, :], (S5_LANES, cb))

        @pl.when((ps == 0) & (ti == 0))
        def _():
            st_re[...] = jnp.zeros(st_re.shape, F32)
            st_im[...] = jnp.zeros(st_im.shape, F32)
            db_ref[...] = jnp.zeros(db_ref.shape, F32)
            dc_ref[...] = jnp.zeros(dc_ref.shape, F32)
            da_ref[...] = jnp.zeros(da_ref.shape, F32)

        @pl.when((ps == 1) & (ti == 0))
        def _():
            pr, pi = _cpow(ab_ref[0:1, :], -ab_ref[1:2, :], lc)
            cr, ci = _chunk_carries(st_re, st_im, pr, pi, [(k, k + 1) for k in range(S5_LANES - 2, -1, -1)])
            st_re[...] = cr
            st_im[...] = ci

        ds_ref[...] = jnp.dot(dy_ref[...], ct_ref[...], preferred_element_type=F32)

        def rows_of(i):
            return pl.ds(pl.multiple_of((tt - 1 - i) * S5_LANES, S5_LANES), S5_LANES)

        def step0(i, carry):
            gr, gi = carry
            rows = rows_of(i)
            return (ar * gr + ai * gi + ds_ref[rows, 0:cb], ar * gi - ai * gr + ds_ref[rows, cb:cb2])

        def step1(i, carry):
            gr, gi, acr, aci = carry
            rows = rows_of(i)
            sr, si = s_ref[rows, 0:cb], s_ref[rows, cb:cb2]
            acr = acr + sr * gr + si * gi
            aci = aci + sr * gi - si * gr
            nr = ar * gr + ai * gi + ds_ref[rows, 0:cb]
            ni = ar * gi - ai * gr + ds_ref[rows, cb:cb2]
            ds_ref[rows, 0:cb] = nr
            ds_ref[rows, cb:cb2] = ni
            return nr, ni, acr, aci

        @pl.when(ps == 0)
        def _():
            gr, gi = lax.fori_loop(0, tt, step0, (st_re[...], st_im[...]))
            st_re[...] = gr
            st_im[...] = gi

        @pl.when(ps == 1)
        def _():
            zero = jnp.zeros((S5_LANES, cb), F32)
            gr, gi, acr, aci = lax.fori_loop(0, tt, step1, (st_re[...], st_im[...], zero, zero))
            st_re[...] = gr
            st_im[...] = gi
            da_ref[0] += acr
            da_ref[1] += aci
            dsb = ds_ref[...].astype(BF16)
            du_ref[...] = jnp.dot(dsb, bt_ref[...], preferred_element_type=F32)
            db_ref[...] += _dot_tn(u_ref[...], dsb)
            dc_ref[...] += _dot_tn(dy_ref[...], s_ref[...].astype(BF16))

    def tile_idx(ps, ti):
        return (nt - 1 - ti) * ps + (nt - 1) * (1 - ps)

    return pl.pallas_call(
        body, name="s5_bwd", grid=(nb, 2, nt),
        in_specs=[pl.BlockSpec((rt, cw), lambda j, ps, ti: (nt - 1 - ti, j)),
                  pl.BlockSpec((rt, cw), lambda j, ps, ti: (tile_idx(ps, ti), j)),
                  pl.BlockSpec((None, rt, cb2), lambda j, ps, ti: (j, tile_idx(ps, ti), 0)),
                  pl.BlockSpec((None, cb2, cw), lambda j, ps, ti: (j, 0, 0)),
                  pl.BlockSpec((None, 2, cb), lambda j, ps, ti: (j, 0, 0)),
                  pl.BlockSpec((None, cw, cb2), lambda j, ps, ti: (j, 0, 0))],
        out_specs=[pl.BlockSpec((rt, cw), lambda j, ps, ti: (tile_idx(ps, ti), j)),
                   pl.BlockSpec((None, cw, cb2), lambda j, ps, ti: (j, 0, 0)),
                   pl.BlockSpec((None, cw, cb2), lambda j, ps, ti: (j, 0, 0)),
                   pl.BlockSpec((None, 2, S5_LANES, cb), lambda j, ps, ti: (j, 0, 0, 0))],
        out_shape=[jax.ShapeDtypeStruct((n_rows, ws), F32), jax.ShapeDtypeStruct((nb, cw, cb2), F32),
                   jax.ShapeDtypeStruct((nb, cw, cb2), F32), jax.ShapeDtypeStruct((nb, 2, S5_LANES, cb), F32)],
        scratch_shapes=[pltpu.VMEM((rt, cb2), F32), pltpu.VMEM((S5_LANES, cb), F32), pltpu.VMEM((S5_LANES, cb), F32)],
        compiler_params=_params(("arbitrary", "arbitrary", "arbitrary")),
    )(dyp, up, s_all, bblk_t, ab, cblk_t)


def _s5_discretize(a_re, a_im, log_dt, b_re, b_im):
    dt = jnp.exp(log_dt)[:, None]
    mag = jnp.exp(a_re * dt)
    abar_re = mag * jnp.cos(a_im * dt)
    abar_im = mag * jnp.sin(a_im * dt)
    nr, ni = abar_re - 1.0, abar_im
    den = a_re * a_re + a_im * a_im
    fr = (nr * a_re + ni * a_im) / den
    fi = (ni * a_re - nr * a_im) / den
    bbar_re = fr[..., None] * b_re - fi[..., None] * b_im
    bbar_im = fr[..., None] * b_im + fi[..., None] * b_re
    return abar_re, abar_im, bbar_re, bbar_im


def _perm_rows(a):
    n, c = a.shape
    return a.reshape(S5_LANES, n // S5_LANES, c).transpose(1, 0, 2).reshape(n, c)


def _unperm_rows(a):
    n, c = a.shape
    return a.reshape(n // S5_LANES, S5_LANES, c).transpose(1, 0, 2).reshape(n, c)


def _place():
    x, y, c = lax.axis_index("x"), lax.axis_index("y"), lax.axis_index("c")
    chips = [(1 - x, y), (x, 1 - y), (1 - x, 1 - y)]
    return x, y, c, 2 * x + y, chips


ANY = pl.BlockSpec(memory_space=pl.ANY)


def _all_gather(srcs):
    n = len(srcs)

    def body(*refs):
        src, out = refs[:n], refs[n:2 * n]
        send1, recv1, send2, recv2, lsem = refs[2 * n:]
        x, y, c, s, chips = _place()
        sib = (x, y, 1 - c)
        started = []
        for i in range(n):
            loc = pltpu.make_async_copy(src[i], out[i].at[s], lsem.at[i])
            loc.start()
            started.append(loc)
        first, passed = [], []
        for i in range(n):
            for j, (cx, cy) in enumerate(chips):
                cp = pltpu.make_async_remote_copy(
                    src_ref=src[i].at[c], dst_ref=out[i].at[s, c], send_sem=send1.at[3 * i + j],
                    recv_sem=recv1.at[3 * i + j], device_id=(cx, cy, c), device_id_type=MESH)
                cp.start()
                first.append(cp)
        for i in range(n):
            for j, (cx, cy) in enumerate(chips):
                sj = 2 * cx + cy
                pltpu.make_async_remote_copy(
                    src_ref=src[i].at[c], dst_ref=out[i].at[sj, c], send_sem=send1.at[3 * i + j],
                    recv_sem=recv1.at[3 * i + j], device_id=(cx, cy, c), device_id_type=MESH).wait_recv()
                cp = pltpu.make_async_remote_copy(
                    src_ref=out[i].at[sj, c], dst_ref=out[i].at[sj, c], send_sem=send2.at[3 * i + j],
                    recv_sem=recv2.at[3 * i + j], device_id=sib, device_id_type=MESH)
                cp.start()
                passed.append(cp)
        for i in range(n):
            for j, (cx, cy) in enumerate(chips):
                sj = 2 * cx + cy
                pltpu.make_async_remote_copy(
                    src_ref=src[i].at[c], dst_ref=out[i].at[sj, 1 - c], send_sem=send2.at[3 * i + j],
                    recv_sem=recv2.at[3 * i + j], device_id=sib, device_id_type=MESH).wait_recv()
        for cp in first + passed:
            cp.wait_send()
        for loc in started:
            loc.wait()

    return pl.pallas_call(
        body, name="weights_all_gather", in_specs=[ANY] * n, out_specs=[ANY] * n,
        out_shape=[jax.ShapeDtypeStruct((N_CHIPS,) + a.shape, a.dtype) for a in srcs],
        scratch_shapes=[pltpu.SemaphoreType.DMA((3 * n,))] * 4 + [pltpu.SemaphoreType.DMA((n,))],
    )(*srcs)


def _pair_exchange(grads):
    n = len(grads)

    def body(*refs):
        g, got = refs[:n], refs[n:2 * n]
        send, recv = refs[2 * n:]
        x, y, c, _, _ = _place()
        cps = []
        for i in range(n):
            cp = pltpu.make_async_remote_copy(
                src_ref=g[i].at[1 - c], dst_ref=got[i], send_sem=send.at[i], recv_sem=recv.at[i],
                device_id=(x, y, 1 - c), device_id_type=MESH)
            cp.start()
            cps.append(cp)
        for cp in cps:
            cp.wait()

    return pl.pallas_call(
        body, name="grads_pair_exchange", in_specs=[ANY] * n, out_specs=[ANY] * n,
        out_shape=[jax.ShapeDtypeStruct(a.shape[1:], a.dtype) for a in grads],
        scratch_shapes=[pltpu.SemaphoreType.DMA((n,))] * 2,
    )(*grads)


def _pair_add(g, got, core):
    _, nchip, r, cw = g.shape
    tr = _tile(r, 512, 16)

    def body(c_ref, a_ref, b_ref, o_ref):
        o_ref[...] = (a_ref[...].astype(F32) + b_ref[...].astype(F32)).astype(o_ref.dtype)

    return pl.pallas_call(
        body, name="grads_pair_add",
        grid_spec=pltpu.PrefetchScalarGridSpec(
            num_scalar_prefetch=1, grid=(nchip, r // tr),
            in_specs=[pl.BlockSpec((None, None, tr, cw), lambda s, i, c_ref: (c_ref[0], s, i, 0)),
                      pl.BlockSpec((None, tr, cw), lambda s, i, c_ref: (s, i, 0))],
            out_specs=pl.BlockSpec((None, tr, cw), lambda s, i, c_ref: (s, i, 0))),
        out_shape=jax.ShapeDtypeStruct((nchip, r, cw), BF16),
        compiler_params=_params(("arbitrary", "arbitrary")),
    )(core, g, got)


def _chip_scatter(parts):
    n = len(parts)

    def body(*refs):
        p, got = refs[:n], refs[n:2 * n]
        send, recv, lsem = refs[2 * n:]
        x, y, c, s, chips = _place()
        cps = []
        for i in range(n):
            loc = pltpu.make_async_copy(p[i].at[s], got[i].at[s], lsem.at[i])
            loc.start()
            cps.append(loc)
            for j, (cx, cy) in enumerate(chips):
                cp = pltpu.make_async_remote_copy(
                    src_ref=p[i].at[2 * cx + cy], dst_ref=got[i].at[s], send_sem=send.at[3 * i + j],
                    recv_sem=recv.at[3 * i + j], device_id=(cx, cy, c), device_id_type=MESH)
                cp.start()
                cps.append(cp)
        for i in range(n):
            for j, (cx, cy) in enumerate(chips):
                pltpu.make_async_remote_copy(
                    src_ref=p[i].at[s], dst_ref=got[i].at[2 * cx + cy], send_sem=send.at[3 * i + j],
                    recv_sem=recv.at[3 * i + j], device_id=(cx, cy, c), device_id_type=MESH).wait_recv()
        for k, cp in enumerate(cps):
            if k % 4 == 0:
                cp.wait()
            else:
                cp.wait_send()

    return pl.pallas_call(
        body, name="grads_chip_scatter", in_specs=[ANY] * n, out_specs=[ANY] * n,
        out_shape=[jax.ShapeDtypeStruct(a.shape, a.dtype) for a in parts],
        scratch_shapes=[pltpu.SemaphoreType.DMA((3 * n,))] * 2 + [pltpu.SemaphoreType.DMA((n,))],
    )(*parts)


def _chip_sum(parts):
    nchip, r, cw = parts.shape
    tr = _tile(r, 512, 16)

    def body(p_ref, o_ref):
        acc = p_ref[0].astype(F32)
        for t in range(1, nchip):
            acc = acc + p_ref[t].astype(F32)
        o_ref[...] = acc

    return pl.pallas_call(
        body, name="grads_chip_sum", grid=(r // tr,),
        in_specs=[pl.BlockSpec((nchip, tr, cw), lambda i: (0, i, 0))],
        out_specs=pl.BlockSpec((tr, cw), lambda i: (i, 0)),
        out_shape=jax.ShapeDtypeStruct((r, cw), F32),
        compiler_params=_params(("arbitrary",)),
    )(parts)


def _half_exchange(halves):
    n = len(halves)

    def body(*refs):
        h, full = refs[:n], refs[n:2 * n]
        send, recv, lsem = refs[2 * n:]
        x, y, c, _, _ = _place()
        cps = []
        for i in range(n):
            loc = pltpu.make_async_copy(h[i], full[i].at[c], lsem.at[i])
            loc.start()
            cp = pltpu.make_async_remote_copy(
                src_ref=h[i], dst_ref=full[i].at[c], send_sem=send.at[i], recv_sem=recv.at[i],
                device_id=(x, y, 1 - c), device_id_type=MESH)
            cp.start()
            cps += [loc, cp]
        for i in range(n):
            pltpu.make_async_remote_copy(
                src_ref=h[i], dst_ref=full[i].at[1 - c], send_sem=send.at[i], recv_sem=recv.at[i],
                device_id=(x, y, 1 - c), device_id_type=MESH).wait_recv()
        for k, cp in enumerate(cps):
            if k % 2 == 0:
                cp.wait()
            else:
                cp.wait_send()

    return pl.pallas_call(
        body, name="grads_half_exchange", in_specs=[ANY] * n, out_specs=[ANY] * n,
        out_shape=[jax.ShapeDtypeStruct((2,) + a.shape, a.dtype) for a in halves],
        scratch_shapes=[pltpu.SemaphoreType.DMA((n,))] * 3,
    )(*halves)


def _small_all_reduce(vec):
    r, cw = vec.shape

    def body(v_ref, o_ref, sib_buf, chip_buf, send, recv):
        x, y, c, s, chips = _place()
        cp = pltpu.make_async_remote_copy(
            src_ref=v_ref, dst_ref=sib_buf, send_sem=send.at[0], recv_sem=recv.at[0],
            device_id=(x, y, 1 - c), device_id_type=MESH)
        cp.start()
        cp.wait()
        chip_buf[s] = v_ref[...] + sib_buf[...]
        cps = []
        for j, (cx, cy) in enumerate(chips):
            cp = pltpu.make_async_remote_copy(
                src_ref=chip_buf.at[s], dst_ref=chip_buf.at[s], send_sem=send.at[1 + j], recv_sem=recv.at[1 + j],
                device_id=(cx, cy, c), device_id_type=MESH)
            cp.start()
            cps.append(cp)
        for j, (cx, cy) in enumerate(chips):
            pltpu.make_async_remote_copy(
                src_ref=chip_buf.at[s], dst_ref=chip_buf.at[2 * cx + cy], send_sem=send.at[1 + j],
                recv_sem=recv.at[1 + j], device_id=(cx, cy, c), device_id_type=MESH).wait_recv()
        for cp in cps:
            cp.wait_send()
        o_ref[...] = ((chip_buf[0] + chip_buf[1]) + chip_buf[2]) + chip_buf[3]

    vm = pl.BlockSpec(memory_space=pltpu.VMEM)
    return pl.pallas_call(
        body, name="small_all_reduce", in_specs=[vm], out_specs=vm,
        out_shape=jax.ShapeDtypeStruct((r, cw), F32),
        scratch_shapes=[pltpu.VMEM((r, cw), F32), pltpu.VMEM((N_CHIPS, r, cw), F32),
                        pltpu.SemaphoreType.DMA((4,)), pltpu.SemaphoreType.DMA((4,))],
        compiler_params=_params(),
    )(vec)


def _adamw_math(w, g, m, v):
    m = ADAM_B1 * m + (1.0 - ADAM_B1) * g
    v = ADAM_B2 * v + (1.0 - ADAM_B2) * (g * g)
    m_hat = m / (1.0 - ADAM_B1 ** ADAM_STEP)
    v_hat = v / (1.0 - ADAM_B2 ** ADAM_STEP)
    delta = -ADAM_LR * (m_hat / (jnp.sqrt(v_hat) + ADAM_EPS) + ADAM_WD * w)
    return delta, m, v


def _adamw(name, w, m, v, g, g_half=0, g_row_off=0, tr=256):
    r, cw = w.shape
    tr = _tile(math.gcd(r, g_row_off) if g_row_off else r, tr)
    off = g_row_off // tr

    def body(w_ref, m_ref, v_ref, g_ref, go_ref, d_ref, mo_ref, vo_ref):
        g_v = g_ref[...]
        delta, m_n, v_n = _adamw_math(w_ref[...], g_v, m_ref[...], v_ref[...])
        go_ref[...] = g_v
        d_ref[...] = delta
        mo_ref[...] = m_n
        vo_ref[...] = v_n

    tile = pl.BlockSpec((tr, cw), lambda i: (i, 0))
    out = jax.ShapeDtypeStruct((r, cw), F32)
    return pl.pallas_call(
        body, name=name, grid=(r // tr,),
        in_specs=[tile, tile, tile, pl.BlockSpec((None, tr, cw), lambda i: (g_half, i + off, 0))],
        out_specs=[tile] * 4, out_shape=[out] * 4,
        compiler_params=_params(("arbitrary",)),
    )(w, m, v, g)


def kernel(x, mem, g_ffn1, w1_gate, w1_up, w1_down, g_mix, w_in, ssm_a_re, ssm_a_im, ssm_log_dt, ssm_b_re, ssm_b_im, ssm_c_re, ssm_c_im, ssm_d, w_glu, b_glu, w_pool, pool_scale, g_out_ssm, g_out_pool, w_out, g_xattn, g_mem, w_q, w_k, w_v, w_o, g_ffn2, w2_gate, w2_up, w2_down, g_final, loss_target, m_g_ffn1, m_w1_gate, m_w1_up, m_w1_down, m_g_mix, m_w_in, m_ssm_a_re, m_ssm_a_im, m_ssm_log_dt, m_ssm_b_re, m_ssm_b_im, m_ssm_c_re, m_ssm_c_im, m_ssm_d, m_w_glu, m_b_glu, m_w_pool, m_pool_scale, m_g_out_ssm, m_g_out_pool, m_w_out, m_g_xattn, m_g_mem, m_w_q, m_w_k, m_w_v, m_w_o, m_g_ffn2, m_w2_gate, m_w2_up, m_w2_down, m_g_final, v_g_ffn1, v_w1_gate, v_w1_up, v_w1_down, v_g_mix, v_w_in, v_ssm_a_re, v_ssm_a_im, v_ssm_log_dt, v_ssm_b_re, v_ssm_b_im, v_ssm_c_re, v_ssm_c_im, v_ssm_d, v_w_glu, v_b_glu, v_w_pool, v_pool_scale, v_g_out_ssm, v_g_out_pool, v_w_out, v_g_xattn, v_g_mem, v_w_q, v_w_k, v_w_v, v_w_o, v_g_ffn2, v_w2_gate, v_w2_up, v_w2_down, v_g_final):
    local = dict(locals())
    wts = {n: local[n] for n in WEIGHTS}
    mom = {n: local["m_" + n] for n in WEIGHTS}
    var = {n: local["v_" + n] for n in WEIGHTS}

    x2 = x[0]
    mem2 = mem[0]
    tgt = loss_target[0]
    t_rows, d = x2.shape
    fs = w1_gate.shape[-1]
    ds_ = w_in.shape[1]
    ws = d // 2
    n_pg = len(POOL_WINDOWS)
    pw = ws // n_pg
    n_grp = ws // SSM_GROUP
    n_state = ssm_a_re.shape[-1]
    core = lax.axis_index("c").astype(jnp.int32).reshape(1)

    glu_rows = w_glu[0].reshape(-1, d)
    pool_rows = w_pool[0].reshape(-1, d)
    gh, ph = glu_rows.shape[0] // 2, pool_rows.shape[0] // 2
    rh = 3 * ds_ + gh + ph

    def pack_a(t):
        return jnp.stack([jnp.concatenate([t['w1_gate'][0], t['w1_up'][0]], 0),
                          jnp.concatenate([t['w2_gate'][0], t['w2_up'][0]], 0)])

    def pack_b1(t):
        return jnp.stack([t['w1_down'][0], t['w2_down'][0]])

    src_a = pack_a(wts).astype(BF16)
    src_b1 = pack_b1(wts).astype(BF16)
    src_b2 = jnp.stack([
        jnp.concatenate([w_in[0], w_out[0], w_q[0], glu_rows[:gh], pool_rows[:ph]], 0),
        jnp.concatenate([w_k[0], w_v[0], w_o[0], glu_rows[gh:], pool_rows[ph:]], 0)]).astype(BF16)
    ga, gb1, gb2 = _all_gather([src_a, src_b1, src_b2])
    wglu_full = gb2[:, :, 3 * ds_:3 * ds_ + gh, :].reshape(ws, ws)
    wpool_full = gb2[:, :, 3 * ds_ + gh:, :].reshape(N_CHIPS, n_pg, pw // N_CHIPS, pw)
    wpool_full = wpool_full.transpose(1, 0, 2, 3).reshape(n_pg, pw, pw)
    DD = {'w_in': (0, 0), 'w_out': (0, 1), 'w_q': (0, 2), 'w_k': (1, 0), 'w_v': (1, 1), 'w_o': (1, 2)}

    def mm_dd(name, a, wname, out_dtype, res=None):
        h, q = DD[wname]
        return _mm_nn(name, a, gb2, (None, None, ds_, d), lambda j, k: (k, h, q, 0), N_CHIPS, d, out_dtype, res=res)

    def mm_dd_t(name, pairs, out_dtype):
        ps = [(dy, gb2, (None, None, ds_, d), functools.partial(lambda s, h, q: (s, h, q, 0), h=DD[w][0], q=DD[w][1]))
              for dy, w in pairs]
        return _mm_nt_cols(name, ps, N_CHIPS, ds_, [out_dtype])[0]

    def vec(n):
        return wts[n].reshape(1, -1)

    disc_in = (ssm_a_re[0], ssm_a_im[0], ssm_log_dt[0], ssm_b_re[0], ssm_b_im[0])
    (abar_re, abar_im, bbar_re, bbar_im), disc_vjp = jax.vjp(_s5_discretize, *disc_in)
    gpb = min(S5_GROUPS_PER_BLOCK, n_grp)
    nb = n_grp // gpb
    cb = gpb * n_state
    eye = jnp.eye(gpb, dtype=F32)

    def blockdiag(t):
        return jnp.einsum('jgph,gk->jghkp', t.reshape(nb, gpb, n_state, SSM_GROUP), eye).reshape(nb, gpb * SSM_GROUP, cb)

    def blockdiag_c(t):
        return jnp.einsum('jghp,gk->jkpgh', t.reshape(nb, gpb, SSM_GROUP, n_state), eye).reshape(nb, cb, gpb * SSM_GROUP)

    bblk = jnp.concatenate([blockdiag(bbar_re), blockdiag(bbar_im)], -1).astype(BF16)
    cblk = jnp.concatenate([blockdiag_c(ssm_c_re[0]), -blockdiag_c(ssm_c_im[0])], 1).astype(BF16)
    ab = jnp.stack([abar_re.reshape(nb, cb), abar_im.reshape(nb, cb)], 1)

    n1 = _rmsnorm("norm_ffn1", x2, vec('g_ffn1'))
    a1, b1, hm1 = _ffn_up("ffn1_up", n1, ga, 0, d, fs)
    h1 = _mm_nn("ffn1_down", hm1, gb1, (None, None, fs, d), lambda j, k: (k, 0, 0, 0), N_CHIPS, d, F32, res=x2, alpha=0.5)
    n2 = _rmsnorm("norm_mix", h1, vec('g_mix'))
    u = mm_dd("mix_in", n2, 'w_in', F32)

    up = _perm_rows(u[:, :ws]).astype(BF16)
    ylin_p, s_all = _s5_fwd(up, bblk, ab, cblk)
    ylin = _unperm_rows(ylin_p)

    def gelu_fn(r, v):
        y1 = r[0] + v[0] * r[1]
        y2 = jax.nn.gelu(y1)
        return [y2, y2], []
    y2, y2b = _rowwise("s5_gelu", gelu_fn, [ylin, (u, 0, ws)], [vec('ssm_d')], [(ws, F32), (ws, BF16)])
    z = _mm_nn("s5_glu", y2b, wglu_full, (ws, ws), lambda j, k: (0, 0), 1, ws, F32)

    def glu_fn(r, v):
        y3 = r[0] * _sigmoid(r[1] + v[0])
        return [_rms_fwd(y3, v[1])], []
    m_ssm = _rowwise("s5_gate_norm", glu_fn, [y2, z], [vec('b_glu'), vec('g_out_ssm')], [(ws, BF16)])[0]

    pooled, zp = _pool_fwd(u, ws // pw, wpool_full, vec('pool_scale'))
    m_pool = _rmsnorm("norm_pool", zp, vec('g_out_pool'))
    merged = jnp.concatenate([m_ssm, m_pool], -1)
    h2 = mm_dd("mix_out", merged, 'w_out', F32, res=h1)

    memn = _rmsnorm("norm_mem", mem2, vec('g_mem'))
    k_mem = mm_dd("attn_k", memn, 'w_k', BF16)
    v_mem = mm_dd("attn_v", memn, 'w_v', BF16)
    hn = _rmsnorm("norm_xattn", h2, vec('g_xattn'))
    q = mm_dd("attn_q", hn, 'w_q', BF16)
    o = _attn_fwd(q, k_mem, v_mem)
    h3 = mm_dd("attn_out", o, 'w_o', F32, res=h2)

    n4 = _rmsnorm("norm_ffn2", h3, vec('g_ffn2'))
    a2, b2, hm2 = _ffn_up("ffn2_up", n4, ga, 1, d, fs)
    h4 = _mm_nn("ffn2_down", hm2, gb1, (None, None, fs, d), lambda j, k: (k, 1, 0, 0), N_CHIPS, d, F32, res=h3, alpha=0.5)

    def loss_fn(r, v):
        h, t = r
        e = _rms_fwd(h, v[0]) - t
        dy = e * (1.0 / d)
        dh, dg = _rms_bwd(dy, h, v[0])
        part = jnp.sum(_colsum(e * e), axis=1, keepdims=True) * (0.5 / d)
        return [dh, 0.5 * dh], [_colsum(dg), jnp.broadcast_to(part, (1, 128))]
    dh4, dy_f2, dg_final, loss_row = _rowwise("loss_head", loss_fn, [h4, tgt], [g_final.reshape(1, -1)],
                                              [(d, F32), (d, BF16)], [d, 128])

    grad_a = jnp.zeros((2, N_CHIPS, 2 * d, fs), BF16)
    grad_b1 = jnp.zeros((2, N_CHIPS, fs, d), BF16)
    grad_b2 = jnp.zeros((2, N_CHIPS, rh, d), BF16)

    def ffn_bwd(tag, half, dy_half, a, b, hm, n_in, grad_a, grad_b1):
        da, db = _mm_nt_cols(tag + "_down_bwd", [(dy_half, gb1, (None, None, fs, d), lambda s: (s, half, 0, 0))],
                             N_CHIPS, fs, [BF16, BF16], epi=_swiglu_bwd, extras=[a, b])
        grad_b1 = _mm_tn(tag + "_dw_down", hm, dy_half, fs, d, N_CHIPS, 1, None, (None, None, fs, d),
                         lambda p, q: (half, p, 0, 0), into=grad_b1)
        wblk = (None, None, d, fs)
        dn = _mm_nt_k(tag + "_up_bwd", [(da, ga, wblk, lambda s: (s, half, 0, 0)), (db, ga, wblk, lambda s: (s, half, 1, 0))],
                      N_CHIPS, d, F32)
        grad_a = _mm_tn(tag + "_dw_gate", n_in, da, d, fs, 1, N_CHIPS, None, (None, None, d, fs),
                        lambda p, q: (half, q, 0, 0), into=grad_a)
        grad_a = _mm_tn(tag + "_dw_up", n_in, db, d, fs, 1, N_CHIPS, None, (None, None, d, fs),
                        lambda p, q: (half, q, 1, 0), into=grad_a)
        return dn, grad_a, grad_b1

    def dw_dd(name, a, dy, wname, grad_b2):
        h, q = DD[wname]
        return _mm_tn(name, a, dy, ds_, d, N_CHIPS, 1, None, (None, None, ds_, d), lambda p, qq: (h, p, q, 0), into=grad_b2)

    def norm_bwd(name, dn, h, gname, dres):
        def fn(r, v):
            dx, dg = _rms_bwd(r[0], r[1], v[0])
            tot = dx + r[2]
            return [tot, tot], [_colsum(dg)]
        return _rowwise(name, fn, [dn, h, dres], [vec(gname)], [(d, F32), (d, BF16)], [d])

    dn4, grad_a, grad_b1 = ffn_bwd("ffn2", 1, dy_f2, a2, b2, hm2, n4, grad_a, grad_b1)
    dh3, dh3b, dg_ffn2 = norm_bwd("norm_ffn2_bwd", dn4, h3, 'g_ffn2', dh4)

    do = mm_dd_t("attn_out_bwd", [(dh3b, 'w_o')], BF16)
    grad_b2 = dw_dd("attn_dw_o", o, dh3b, 'w_o', grad_b2)
    dq, dk, dv = _attn_bwd(q, k_mem, v_mem, do)
    dkb, dvb = dk.astype(BF16), dv.astype(BF16)
    grad_b2 = dw_dd("attn_dw_q", hn, dq, 'w_q', grad_b2)
    dhn = mm_dd_t("attn_q_bwd", [(dq, 'w_q')], F32)
    dh2, dh2b, dg_xattn = norm_bwd("norm_xattn_bwd", dhn, h2, 'g_xattn', dh3)
    grad_b2 = dw_dd("attn_dw_k", memn, dkb, 'w_k', grad_b2)
    grad_b2 = dw_dd("attn_dw_v", memn, dvb, 'w_v', grad_b2)
    dmemn = mm_dd_t("attn_kv_bwd", [(dkb, 'w_k'), (dvb, 'w_v')], F32)
    dg_mem = _rowwise("norm_mem_bwd", lambda r, v: ([], [_colsum(_rms_bwd(r[0], r[1], v[0])[1])]),
                      [dmemn, mem2], [vec('g_mem')], [], [d])[0]

    dmerged = mm_dd_t("mix_out_bwd", [(dh2b, 'w_out')], F32)
    grad_b2 = dw_dd("mix_dw_out", merged, dh2b, 'w_out', grad_b2)

    def gate_bwd_fn(r, v):
        dm, y2_v, z_v = r
        sg = _sigmoid(z_v + v[0])
        y3 = y2_v * sg
        dy3, dg = _rms_bwd(dm, y3, v[1])
        dz = dy3 * y3 * (1.0 - sg)
        return [dy3 * sg, dz], [_colsum(dg), _colsum(dz)]
    dy2a, dzb, dg_out_ssm, db_glu = _rowwise("s5_gate_norm_bwd", gate_bwd_fn, [(dmerged, 0, ws), y2, z],
                                             [vec('b_glu'), vec('g_out_ssm')], [(ws, F32), (ws, BF16)], [ws, ws])
    dy2b_ = _mm_nt_cols("s5_glu_bwd", [(dzb, wglu_full, (ws, ws), lambda s: (0, 0))], 1, ws, [F32])[0]
    dw_glu = _mm_tn("s5_dw_glu", y2b, dzb, ws, ws, 1, 1, jax.ShapeDtypeStruct((ws, ws), F32), (ws, ws), lambda p, q: (0, 0))

    def gelu_bwd_fn(r, v):
        dy2 = r[0] + r[1]
        us = r[3]
        y1 = r[2] + v[0] * us
        kk = math.sqrt(2.0 / math.pi)
        th = jnp.tanh(kk * (y1 + 0.044715 * y1 * y1 * y1))
        dgelu = 0.5 * (1.0 + th) + 0.5 * y1 * (1.0 - th * th) * kk * (1.0 + 3.0 * 0.044715 * y1 * y1)
        dy1 = dy2 * dgelu
        return [dy1, dy1 * v[0]], [_colsum(dy1 * us)]
    dy1b, du_skip, d_ssm_d = _rowwise("s5_gelu_bwd", gelu_bwd_fn, [dy2a, dy2b_, ylin, (u, 0, ws)], [vec('ssm_d')],
                                      [(ws, BF16), (ws, F32)], [ws])

    bblk_t = jnp.swapaxes(bblk, 1, 2)
    cblk_t = jnp.swapaxes(cblk, 1, 2)
    du_p, d_bblk, d_cblk_t, d_ab = _s5_bwd(_perm_rows(dy1b), up, s_all, bblk_t, ab, cblk_t)
    du_ssm = _unperm_rows(du_p)

    dzp, dg_out_pool = _rowwise("norm_pool_bwd", lambda r, v: (lambda dx, dg: ([dx], [_colsum(dg)]))(*_rms_bwd(r[0], r[1], v[0])),
                                [(dmerged, 1, ws), zp], [vec('g_out_pool')], [(ws, F32)], [ws])
    dps, dw_pool, d_pool_scale = _pool_bwd1(dzp, pooled, wpool_full, vec('pool_scale'))
    du_pool = _pool_bwd2(dps, n_pg)

    dub = _rowwise("mix_du", lambda r, v: ([jnp.concatenate([r[0] + r[1], r[2]], -1)], []),
                   [du_ssm, du_skip, du_pool], [], [(d, BF16)])[0]
    dn2 = mm_dd_t("mix_in_bwd", [(dub, 'w_in')], F32)
    grad_b2 = dw_dd("mix_dw_in", n2, dub, 'w_in', grad_b2)
    dh1, dh1b, dg_mix = norm_bwd("norm_mix_bwd", dn2, h1, 'g_mix', dh2)

    dy_f1 = _rowwise("ffn1_half", lambda r, v: ([0.5 * r[0]], []), [dh1], [], [(d, BF16)])[0]
    dn1, grad_a, grad_b1 = ffn_bwd("ffn1", 0, dy_f1, a1, b1, hm1, n1, grad_a, grad_b1)
    grad_x, _, dg_ffn1 = norm_bwd("norm_ffn1_bwd", dn1, x2, 'g_ffn1', dh1)

    glu_g = dw_glu.reshape(N_CHIPS, 2, gh, d).transpose(1, 0, 2, 3).astype(BF16)
    pool_g = dw_pool.reshape(n_pg, N_CHIPS, pw // N_CHIPS, pw).transpose(1, 0, 2, 3).reshape(N_CHIPS, 2, ph, d)
    pool_g = pool_g.transpose(1, 0, 2, 3).astype(BF16)
    grad_b2 = lax.dynamic_update_slice(grad_b2, glu_g, (0, 0, 3 * ds_, 0))
    grad_b2 = lax.dynamic_update_slice(grad_b2, pool_g, (0, 0, 3 * ds_ + gh, 0))

    def undiag(t):
        return jnp.einsum('jghkp,gk->jgph', t.reshape(nb, gpb, SSM_GROUP, gpb, n_state), eye).reshape(n_grp, n_state, SSM_GROUP)

    d_bbar_re, d_bbar_im = undiag(d_bblk[:, :, :cb]), undiag(d_bblk[:, :, cb:])
    d_c_re = undiag(d_cblk_t[:, :, :cb]).transpose(0, 2, 1)
    d_c_im = -undiag(d_cblk_t[:, :, cb:]).transpose(0, 2, 1)
    d_abar = jnp.sum(d_ab, axis=2).reshape(nb, 2, gpb, n_state)
    d_abar_re = d_abar[:, 0].reshape(n_grp, n_state)
    d_abar_im = d_abar[:, 1].reshape(n_grp, n_state)
    d_a_re, d_a_im, d_log_dt, d_b_re, d_b_im = disc_vjp((d_abar_re, d_abar_im, d_bbar_re, d_bbar_im))

    small_g = {'g_ffn1': dg_ffn1, 'g_mix': dg_mix, 'ssm_a_re': d_a_re, 'ssm_a_im': d_a_im, 'ssm_log_dt': d_log_dt,
               'ssm_b_re': d_b_re, 'ssm_b_im': d_b_im, 'ssm_c_re': d_c_re, 'ssm_c_im': d_c_im, 'ssm_d': d_ssm_d,
               'b_glu': db_glu, 'pool_scale': d_pool_scale, 'g_out_ssm': dg_out_ssm, 'g_out_pool': dg_out_pool,
               'g_xattn': dg_xattn, 'g_mem': dg_mem, 'g_ffn2': dg_ffn2, 'g_final': dg_final}
    sizes = [wts[n].size for n in SMALL]
    total = sum(sizes) + 128
    rows_s = -(-total // 1024) * 8
    flat = jnp.concatenate([small_g[n].reshape(-1) for n in SMALL] + [loss_row.reshape(-1)])
    flat = jnp.pad(flat, (0, rows_s * 128 - total)).reshape(rows_s, 128)
    red = _small_all_reduce(flat).reshape(-1)
    loss = red[sum(sizes)]

    def flat_small(t):
        return jnp.pad(jnp.concatenate([t[n].reshape(-1) for n in SMALL]), (0, rows_s * 128 - sum(sizes))).reshape(rows_s, 128)
    sg_, sd_, sm_, sv_ = _adamw("adamw_small", flat_small(wts), flat_small(mom), flat_small(var), red.reshape(1, rows_s, 128))
    out = {}
    off = 0
    for n, sz in zip(SMALL, sizes):
        for key, arr in (('grad', sg_), ('delta', sd_), ('m', sm_), ('v', sv_)):
            out[key, n] = arr.reshape(-1)[off:off + sz].reshape(wts[n].shape)
        off += sz

    grads = [grad_a, grad_b1, grad_b2]
    got = _pair_exchange(grads)
    parts = [_pair_add(g, r, core) for g, r in zip(grads, got)]
    recv = _chip_scatter(parts)
    halves = [_chip_sum(r) for r in recv]
    full_a, full_b1, full_b2 = _half_exchange(halves)

    def upd(n, g_arr, half, row_off, shape2):
        res = _adamw("adamw_" + n, wts[n].reshape(shape2), mom[n].reshape(shape2), var[n].reshape(shape2), g_arr, half, row_off)
        for key, arr in zip(('grad', 'delta', 'm', 'v'), res):
            out[key, n] = arr.reshape(wts[n].shape)

    upd('w1_gate', full_a, 0, 0, (d, fs))
    upd('w1_up', full_a, 0, d, (d, fs))
    upd('w2_gate', full_a, 1, 0, (d, fs))
    upd('w2_up', full_a, 1, d, (d, fs))
    upd('w1_down', full_b1, 0, 0, (fs, d))
    upd('w2_down', full_b1, 1, 0, (fs, d))
    for n, (h, q) in DD.items():
        upd(n, full_b2, h, q * ds_, (ds_, d))
    glu_shape, pool_shape = (ws // N_CHIPS, ws), (n_pg * pw // N_CHIPS, pw)
    upd('w_glu', full_b2[:, 3 * ds_:3 * ds_ + gh].reshape((1,) + glu_shape), 0, 0, glu_shape)
    upd('w_pool', full_b2[:, 3 * ds_ + gh:].reshape((1,) + pool_shape), 0, 0, pool_shape)

    return (loss, grad_x[None], *[out['grad', n] for n in WEIGHTS], *[out['delta', n] for n in WEIGHTS],
            *[out['m', n] for n in WEIGHTS], *[out['v', n] for n in WEIGHTS])
```

```python
import functools
import math

import jax
import jax.numpy as jnp
from jax import lax
from jax.experimental import pallas as pl
from jax.experimental.pallas import tpu as pltpu

F32 = jnp.float32
BF16 = jnp.bfloat16
EPS = 1e-6
ADAM_LR, ADAM_B1, ADAM_B2, ADAM_EPS, ADAM_WD, ADAM_STEP = 0.001, 0.9, 0.999, 1e-08, 0.01, 10
POOL_WINDOWS = (2, 4, 8, 16)
SSM_GROUP = 16
S5_GROUPS_PER_BLOCK = 16
S5_LANES = 8
MEM_HEADS = 4
N_CHIPS = 4
VMEM_LIMIT_V7X = 56 * 1024 * 1024
MESH = pl.DeviceIdType.MESH

WEIGHTS = ['g_ffn1', 'w1_gate', 'w1_up', 'w1_down', 'g_mix', 'w_in', 'ssm_a_re', 'ssm_a_im', 'ssm_log_dt',
           'ssm_b_re', 'ssm_b_im', 'ssm_c_re', 'ssm_c_im', 'ssm_d', 'w_glu', 'b_glu', 'w_pool', 'pool_scale',
           'g_out_ssm', 'g_out_pool', 'w_out', 'g_xattn', 'g_mem', 'w_q', 'w_k', 'w_v', 'w_o', 'g_ffn2',
           'w2_gate', 'w2_up', 'w2_down', 'g_final']
BIG = ['w1_gate', 'w1_up', 'w1_down', 'w_in', 'w_glu', 'w_pool', 'w_out', 'w_q', 'w_k', 'w_v', 'w_o',
       'w2_gate', 'w2_up', 'w2_down']
SMALL = [n for n in WEIGHTS if n not in BIG]


def _tile(n, target, mult=8):
    best = None
    for d in range(1, n + 1):
        if n % d == 0 and d <= target and d % mult == 0:
            best = d
    return best if best is not None else n


def _params(sem=None):
    if sem is None:
        return pltpu.CompilerParams(vmem_limit_bytes=VMEM_LIMIT_V7X)
    return pltpu.CompilerParams(dimension_semantics=sem, vmem_limit_bytes=VMEM_LIMIT_V7X)


def _sigmoid(x):
    return 1.0 / (1.0 + jnp.exp(-x))


def _rms_fwd(x, g):
    r = lax.rsqrt(jnp.mean(x * x, axis=-1, keepdims=True) + EPS)
    return x * r * g


def _rms_bwd(dy, x, g):
    r = lax.rsqrt(jnp.mean(x * x, axis=-1, keepdims=True) + EPS)
    dxh = dy * g
    dx = r * dxh - x * (r * r * r) * jnp.mean(dxh * x, axis=-1, keepdims=True)
    return dx, dy * x * r


def _colsum(v):
    return jnp.sum(v, axis=0, keepdims=True)


def _rowwise(name, fn, rows, vecs, out_defs, red_defs=(), tm=256, deps=()):
    rows = [r if isinstance(r, tuple) else (r, 0, r.shape[1]) for r in rows]
    t_rows = rows[0][0].shape[0]
    tm = _tile(t_rows, tm)
    nr, nv, no, nd, nx = len(rows), len(vecs), len(out_defs), len(red_defs), len(deps)

    def body(*refs):
        r, v = refs[:nr], refs[nr:nr + nv]
        o, d = refs[nr + nv + nx:nr + nv + nx + no], refs[nr + nv + nx + no:]
        outs, reds = fn([x[...] for x in r], [x[...] for x in v])
        for ref, val in zip(o, outs):
            ref[...] = val.astype(ref.dtype)
        if nd:
            @pl.when(pl.program_id(0) == 0)
            def _():
                for ref in d:
                    ref[...] = jnp.zeros(ref.shape, ref.dtype)
            for ref, val in zip(d, reds):
                ref[...] += val

    in_specs = [pl.BlockSpec((tm, w), functools.partial(lambda i, cb: (i, cb), cb=cb)) for (_, cb, w) in rows]
    in_specs += [pl.BlockSpec(v.shape, lambda i: (0, 0)) for v in vecs]
    in_specs += [pl.BlockSpec(memory_space=pl.ANY)] * nx
    out_specs = [pl.BlockSpec((tm, w), lambda i: (i, 0)) for (w, _) in out_defs]
    out_specs += [pl.BlockSpec((1, w), lambda i: (0, 0)) for w in red_defs]
    out_shape = [jax.ShapeDtypeStruct((t_rows, w), dt) for (w, dt) in out_defs]
    out_shape += [jax.ShapeDtypeStruct((1, w), F32) for w in red_defs]
    res = pl.pallas_call(
        body, name=name, grid=(t_rows // tm,), in_specs=in_specs, out_specs=out_specs, out_shape=out_shape,
        compiler_params=_params(("arbitrary",)),
    )(*[r[0] for r in rows], *vecs, *deps)
    return res


def _rmsnorm(name, x, g, tm=256, deps=()):
    return _rowwise(name, lambda r, v: ([_rms_fwd(r[0].astype(F32), v[0])], []), [x], [g],
                    [(x.shape[1], BF16)], tm=tm, deps=deps)[0]


def _mm_nn(name, a, b, b_block, b_idx, nk, n_out, out_dtype, res=None, alpha=1.0, tm=512):
    t_rows = a.shape[0]
    bk, tn = b_block[-2], b_block[-1]
    tm = _tile(t_rows, tm)
    nj = n_out // tn
    has_res = res is not None

    def body(*refs):
        a_ref, b_ref = refs[0], refs[1]
        res_ref = refs[2] if has_res else None
        o_ref = refs[2 + has_res]
        acc_ref = refs[3 + has_res]
        k = pl.program_id(2)
        p = jnp.dot(a_ref[...], b_ref[...], preferred_element_type=F32)

        @pl.when(k == 0)
        def _():
            acc_ref[...] = p

        @pl.when(k > 0)
        def _():
            acc_ref[...] += p

        @pl.when(k == nk - 1)
        def _():
            r = acc_ref[...]
            if has_res:
                r = res_ref[...] + alpha * r
            o_ref[...] = r.astype(o_ref.dtype)

    in_specs = [pl.BlockSpec((tm, bk), lambda j, i, k: (i, k)),
                pl.BlockSpec(b_block, lambda j, i, k: b_idx(j, k))]
    args = [a, b]
    if has_res:
        in_specs.append(pl.BlockSpec((tm, tn), lambda j, i, k: (i, j)))
        args.append(res)
    return pl.pallas_call(
        body, name=name, grid=(nj, t_rows // tm, nk), in_specs=in_specs,
        out_specs=pl.BlockSpec((tm, tn), lambda j, i, k: (i, j)),
        out_shape=jax.ShapeDtypeStruct((t_rows, n_out), out_dtype),
        scratch_shapes=[pltpu.VMEM((tm, tn), F32)],
        compiler_params=_params(("arbitrary", "arbitrary", "arbitrary")),
    )(*args)


def _dot_nt(x, w):
    return lax.dot_general(x, w, (((1,), (1,)), ((), ())), preferred_element_type=F32)


def _dot_tn(x, y):
    return lax.dot_general(x, y, (((0,), (0,)), ((), ())), preferred_element_type=F32)


def _mm_nt_cols(name, pairs, ns, bn, out_defs, epi=None, extras=(), tm=512):
    t_rows = pairs[0][0].shape[0]
    tm = _tile(t_rows, tm)
    npair, nex, no = len(pairs), len(extras), len(out_defs)

    def body(*refs):
        acc = None
        for p in range(npair):
            part = _dot_nt(refs[2 * p][...], refs[2 * p + 1][...])
            acc = part if acc is None else acc + part
        ex = [r[...] for r in refs[2 * npair:2 * npair + nex]]
        outs = epi(acc, *ex) if epi is not None else (acc,)
        for ref, val in zip(refs[2 * npair + nex:], outs):
            ref[...] = val.astype(ref.dtype)

    in_specs, args = [], []
    for (dy, w, w_block, w_idx) in pairs:
        in_specs.append(pl.BlockSpec((tm, dy.shape[1]), lambda s, i: (i, 0)))
        in_specs.append(pl.BlockSpec(w_block, functools.partial(lambda s, i, f: f(s), f=w_idx)))
        args += [dy, w]
    for e in extras:
        in_specs.append(pl.BlockSpec((tm, bn), lambda s, i: (i, s)))
        args.append(e)
    res = pl.pallas_call(
        body, name=name, grid=(ns, t_rows // tm), in_specs=in_specs,
        out_specs=[pl.BlockSpec((tm, bn), lambda s, i: (i, s)) for _ in range(no)],
        out_shape=[jax.ShapeDtypeStruct((t_rows, ns * bn), dt) for dt in out_defs],
        compiler_params=_params(("arbitrary", "arbitrary")),
    )(*args)
    return res


def _mm_nt_k(name, pairs, ns, n_out, out_dtype, tm=512, deps=()):
    t_rows = pairs[0][0].shape[0]
    tm = _tile(t_rows, tm)
    npair, nx = len(pairs), len(deps)

    def body(*refs):
        o_ref, acc_ref = refs[2 * npair + nx], refs[2 * npair + nx + 1]
        s = pl.program_id(1)
        acc = None
        for p in range(npair):
            part = _dot_nt(refs[2 * p][...], refs[2 * p + 1][...])
            acc = part if acc is None else acc + part

        @pl.when(s == 0)
        def _():
            acc_ref[...] = acc

        @pl.when(s > 0)
        def _():
            acc_ref[...] += acc

        @pl.when(s == ns - 1)
        def _():
            o_ref[...] = acc_ref[...].astype(o_ref.dtype)

    in_specs, args = [], []
    for (a, w, w_block, w_idx) in pairs:
        in_specs.append(pl.BlockSpec((tm, w_block[-1]), lambda i, s: (i, s)))
        in_specs.append(pl.BlockSpec(w_block, functools.partial(lambda i, s, f: f(s), f=w_idx)))
        args += [a, w]
    in_specs += [pl.BlockSpec(memory_space=pl.ANY)] * nx
    args += list(deps)
    return pl.pallas_call(
        body, name=name, grid=(t_rows // tm, ns), in_specs=in_specs,
        out_specs=pl.BlockSpec((tm, n_out), lambda i, s: (i, 0)),
        out_shape=jax.ShapeDtypeStruct((t_rows, n_out), out_dtype),
        scratch_shapes=[pltpu.VMEM((tm, n_out), F32)],
        compiler_params=_params(("arbitrary", "arbitrary")),
    )(*args)


def _mm_tn(name, a, b, bk, bn, n_p, n_q, out_shape, out_block, out_idx, into=None, a_off=0, b_off=0, tt=512):
    t_rows = a.shape[0]
    tt = _tile(t_rows, tt, 16)
    nt = t_rows // tt
    has_into = into is not None

    def body(*refs):
        a_ref, b_ref = refs[0], refs[1]
        o_ref, acc_ref = refs[2 + has_into], refs[3 + has_into]
        t = pl.program_id(2)
        part = _dot_tn(a_ref[...], b_ref[...])

        @pl.when(t == 0)
        def _():
            acc_ref[...] = part

        @pl.when(t > 0)
        def _():
            acc_ref[...] += part

        @pl.when(t == nt - 1)
        def _():
            o_ref[...] = acc_ref[...].astype(o_ref.dtype)

    in_specs = [pl.BlockSpec((tt, bk), lambda p, q, t: (t, p + a_off)),
                pl.BlockSpec((tt, bn), lambda p, q, t: (t, q + b_off))]
    args = [a, b]
    aliases = {}
    if has_into:
        in_specs.append(pl.BlockSpec(memory_space=pl.ANY))
        args.append(into)
        aliases = {2: 0}
        out_shape = jax.ShapeDtypeStruct(into.shape, into.dtype)
    return pl.pallas_call(
        body, name=name, grid=(n_p, n_q, nt), in_specs=in_specs,
        out_specs=pl.BlockSpec(out_block, lambda p, q, t: out_idx(p, q)),
        out_shape=out_shape, scratch_shapes=[pltpu.VMEM((bk, bn), F32)],
        input_output_aliases=aliases,
        compiler_params=_params(("arbitrary", "arbitrary", "arbitrary")),
    )(*args)


def _ffn_up(name, n, ga, d_model, fs, tm=512):
    t_rows = n.shape[0]
    tm = _tile(t_rows, tm)

    def body(n_ref, wg_ref, wu_ref, a_ref, b_ref, h_ref):
        x = n_ref[...]
        a = jnp.dot(x, wg_ref[...], preferred_element_type=F32)
        b = jnp.dot(x, wu_ref[...], preferred_element_type=F32)
        a_ref[...] = a.astype(a_ref.dtype)
        b_ref[...] = b.astype(b_ref.dtype)
        h_ref[...] = (a * _sigmoid(a) * b).astype(h_ref.dtype)

    w_block = (None, None, d_model, fs)
    out = jax.ShapeDtypeStruct((t_rows, N_CHIPS * fs), BF16)
    return pl.pallas_call(
        body, name=name, grid=(N_CHIPS, t_rows // tm),
        in_specs=[pl.BlockSpec((tm, d_model), lambda s, i: (i, 0)),
                  pl.BlockSpec(w_block, lambda s, i: (s, 0, 0, 0)),
                  pl.BlockSpec(w_block, lambda s, i: (s, 1, 0, 0))],
        out_specs=[pl.BlockSpec((tm, fs), lambda s, i: (i, s))] * 3,
        out_shape=[out, out, out],
        compiler_params=_params(("arbitrary", "arbitrary")),
    )(n, ga, ga)


def _swiglu_bwd(dh, a, b):
    a = a.astype(F32)
    b = b.astype(F32)
    sg = _sigmoid(a)
    return dh * b * sg * (1.0 + a * (1.0 - sg)), dh * a * sg


def _attn_fwd(q, k, v, tm=512):
    t_rows, d_model = q.shape
    n_mem = k.shape[0]
    hd = d_model // MEM_HEADS
    scale = hd ** -0.5
    tm = _tile(t_rows, tm)

    def body(q_ref, k_ref, v_ref, o_ref):
        for h in range(MEM_HEADS):
            cols = slice(h * hd, (h + 1) * hd)
            s = _dot_nt(q_ref[:, cols], k_ref[:, cols]) * scale
            s = s - jnp.max(s, axis=-1, keepdims=True)
            e = jnp.exp(s)
            p = e / jnp.sum(e, axis=-1, keepdims=True)
            o_ref[:, cols] = jnp.dot(p.astype(BF16), v_ref[:, cols], preferred_element_type=F32).astype(o_ref.dtype)

    return pl.pallas_call(
        body, name="attn_fwd", grid=(t_rows // tm,),
        in_specs=[pl.BlockSpec((tm, d_model), lambda i: (i, 0)),
                  pl.BlockSpec((n_mem, d_model), lambda i: (0, 0)),
                  pl.BlockSpec((n_mem, d_model), lambda i: (0, 0))],
        out_specs=pl.BlockSpec((tm, d_model), lambda i: (i, 0)),
        out_shape=jax.ShapeDtypeStruct((t_rows, d_model), BF16),
        compiler_params=_params(("arbitrary",)),
    )(q, k, v)


def _attn_bwd(q, k, v, do, tm=512):
    t_rows, d_model = q.shape
    n_mem = k.shape[0]
    hd = d_model // MEM_HEADS
    scale = hd ** -0.5
    tm = _tile(t_rows, tm, 16)

    def body(q_ref, k_ref, v_ref, do_ref, dq_ref, dk_ref, dv_ref):
        @pl.when(pl.program_id(0) == 0)
        def _():
            dk_ref[...] = jnp.zeros(dk_ref.shape, F32)
            dv_ref[...] = jnp.zeros(dv_ref.shape, F32)

        for h in range(MEM_HEADS):
            cols = slice(h * hd, (h + 1) * hd)
            qh, kh, vh, doh = q_ref[:, cols], k_ref[:, cols], v_ref[:, cols], do_ref[:, cols]
            s = _dot_nt(qh, kh) * scale
            s = s - jnp.max(s, axis=-1, keepdims=True)
            e = jnp.exp(s)
            p = e / jnp.sum(e, axis=-1, keepdims=True)
            dv_ref[:, cols] += _dot_tn(p.astype(BF16), doh)
            dp = _dot_nt(doh, vh)
            ds = (p * (dp - jnp.sum(dp * p, axis=-1, keepdims=True)) * scale).astype(BF16)
            dq_ref[:, cols] = jnp.dot(ds, kh, preferred_element_type=F32).astype(dq_ref.dtype)
            dk_ref[:, cols] += _dot_tn(ds, qh)

    full = pl.BlockSpec((n_mem, d_model), lambda i: (0, 0))
    tile = pl.BlockSpec((tm, d_model), lambda i: (i, 0))
    return pl.pallas_call(
        body, name="attn_bwd", grid=(t_rows // tm,),
        in_specs=[tile, full, full, tile], out_specs=[tile, full, full],
        out_shape=[jax.ShapeDtypeStruct((t_rows, d_model), BF16),
                   jax.ShapeDtypeStruct((n_mem, d_model), F32), jax.ShapeDtypeStruct((n_mem, d_model), F32)],
        compiler_params=_params(("arbitrary",)),
    )(q, k, v, do)


def _split_bf16(v):
    hi = v.astype(BF16)
    return hi, (v - hi.astype(F32)).astype(BF16)


def _pool_window(g):
    return jnp.left_shift(jnp.int32(POOL_WINDOWS[0]), g)


def _pool_fwd(u, col_off, w_pool, scale, tt=256):
    t_rows = u.shape[0]
    pw = w_pool.shape[-1]
    ng = w_pool.shape[0]
    tt = _tile(t_rows, tt, 16)
    nt = t_rows // tt
    assert POOL_WINDOWS == tuple(2 << i for i in range(ng)) and tt >= POOL_WINDOWS[-1]

    def body(vc_ref, vp_ref, w_ref, sc_ref, pooled_ref, z_ref):
        g, i = pl.program_id(0), pl.program_id(1)
        w = _pool_window(g)
        r = lax.broadcasted_iota(jnp.int32, (tt, tt), 0)
        c = lax.broadcasted_iota(jnp.int32, (tt, tt), 1)
        band_c = ((c <= r) & (c > r - w)).astype(BF16)
        band_p = (c > r - w + tt).astype(BF16)
        vc = vc_ref[...]
        ch, cl = _split_bf16(vc)
        ph, plo = _split_bf16(vp_ref[...] * (i > 0).astype(F32))
        sums = (jnp.dot(band_c, ch, preferred_element_type=F32) + jnp.dot(band_c, cl, preferred_element_type=F32)
                + jnp.dot(band_p, ph, preferred_element_type=F32) + jnp.dot(band_p, plo, preferred_element_type=F32))
        t = i * tt + lax.broadcasted_iota(jnp.int32, (tt, 1), 0)
        cnt = jnp.minimum(t + 1, w).astype(F32)
        pooled = (sums / cnt - vc).astype(BF16)
        pooled_ref[...] = pooled
        z_ref[...] = jnp.dot(pooled, w_ref[...], preferred_element_type=F32) * sc_ref[...]

    return pl.pallas_call(
        body, name="pool_fwd", grid=(ng, nt),
        in_specs=[pl.BlockSpec((tt, pw), lambda g, i: (i, col_off + g)),
                  pl.BlockSpec((tt, pw), lambda g, i: (jnp.maximum(i - 1, 0), col_off + g)),
                  pl.BlockSpec((None, pw, pw), lambda g, i: (g, 0, 0)),
                  pl.BlockSpec((1, pw), lambda g, i: (0, g))],
        out_specs=[pl.BlockSpec((tt, pw), lambda g, i: (i, g))] * 2,
        out_shape=[jax.ShapeDtypeStruct((t_rows, ng * pw), BF16), jax.ShapeDtypeStruct((t_rows, ng * pw), F32)],
        compiler_params=_params(("arbitrary", "arbitrary")),
    )(u, u, w_pool, scale)


def _pool_bwd1(dz, pooled, w_pool, scale, tt=256):
    t_rows = dz.shape[0]
    pw = w_pool.shape[-1]
    ng = w_pool.shape[0]
    tt = _tile(t_rows, tt, 16)
    nt = t_rows // tt

    def body(dz_ref, p_ref, w_ref, sc_ref, dp_ref, dw_ref, dsc_ref):
        g, i = pl.program_id(0), pl.program_id(1)
        w = _pool_window(g)

        @pl.when(i == 0)
        def _():
            dw_ref[...] = jnp.zeros(dw_ref.shape, F32)
            dsc_ref[...] = jnp.zeros(dsc_ref.shape, F32)

        dz_v = dz_ref[...]
        pooled = p_ref[...]
        zpre = jnp.dot(pooled, w_ref[...], preferred_element_type=F32)
        dsc_ref[...] += _colsum(dz_v * zpre)
        dzs = (dz_v * sc_ref[...]).astype(BF16)
        dw_ref[...] += _dot_tn(pooled, dzs)
        t = i * tt + lax.broadcasted_iota(jnp.int32, (tt, 1), 0)
        cnt = jnp.minimum(t + 1, w).astype(F32)
        dp_ref[...] = _dot_nt(dzs, w_ref[...]) / cnt

    return pl.pallas_call(
        body, name="pool_bwd1", grid=(ng, nt),
        in_specs=[pl.BlockSpec((tt, pw), lambda g, i: (i, g)),
                  pl.BlockSpec((tt, pw), lambda g, i: (i, g)),
                  pl.BlockSpec((None, pw, pw), lambda g, i: (g, 0, 0)),
                  pl.BlockSpec((1, pw), lambda g, i: (0, g))],
        out_specs=[pl.BlockSpec((tt, pw), lambda g, i: (i, g)),
                   pl.BlockSpec((None, pw, pw), lambda g, i: (g, 0, 0)),
                   pl.BlockSpec((1, pw), lambda g, i: (0, g))],
        out_shape=[jax.ShapeDtypeStruct((t_rows, ng * pw), F32), jax.ShapeDtypeStruct((ng, pw, pw), F32),
                   jax.ShapeDtypeStruct((1, ng * pw), F32)],
        compiler_params=_params(("arbitrary", "arbitrary")),
    )(dz, pooled, w_pool, scale)


def _pool_bwd2(dps, ng, tt=256):
    t_rows, width = dps.shape
    pw = width // ng
    tt = _tile(t_rows, tt, 16)
    nt = t_rows // tt

    def body(dc_ref, dn_ref, dv_ref):
        g, i = pl.program_id(0), pl.program_id(1)
        w = _pool_window(g)
        r = lax.broadcasted_iota(jnp.int32, (tt, tt), 0)
        c = lax.broadcasted_iota(jnp.int32, (tt, tt), 1)
        band_c = ((c >= r) & (c < r + w)).astype(BF16)
        band_n = (c < r + w - tt).astype(BF16)
        dc = dc_ref[...]
        ch, cl = _split_bf16(dc)
        nh, nl = _split_bf16(dn_ref[...] * (i < nt - 1).astype(F32))
        sums = (jnp.dot(band_c, ch, preferred_element_type=F32) + jnp.dot(band_c, cl, preferred_element_type=F32)
                + jnp.dot(band_n, nh, preferred_element_type=F32) + jnp.dot(band_n, nl, preferred_element_type=F32))
        t = i * tt + lax.broadcasted_iota(jnp.int32, (tt, 1), 0)
        cnt = jnp.minimum(t + 1, w).astype(F32)
        dv_ref[...] = sums - dc * cnt

    return pl.pallas_call(
        body, name="pool_bwd2", grid=(ng, nt),
        in_specs=[pl.BlockSpec((tt, pw), lambda g, i: (i, g)),
                  pl.BlockSpec((tt, pw), lambda g, i: (jnp.minimum(i + 1, nt - 1), g))],
        out_specs=pl.BlockSpec((tt, pw), lambda g, i: (i, g)),
        out_shape=jax.ShapeDtypeStruct((t_rows, width), F32),
        compiler_params=_params(("arbitrary", "arbitrary")),
    )(dps, dps)


def _cpow(ar, ai, n):
    rr, ri, br, bi = None, None, ar, ai
    while n:
        if n & 1:
            rr, ri = (br, bi) if rr is None else (rr * br - ri * bi, rr * bi + ri * br)
        n >>= 1
        if n:
            br, bi = br * br - bi * bi, 2.0 * br * bi
    return rr, ri


def _chunk_carries(st_re, st_im, pr, pi, order):
    cb = st_re.shape[1]
    sub = lax.broadcasted_iota(jnp.int32, (S5_LANES, cb), 0)
    cr = jnp.zeros((S5_LANES, cb), F32)
    ci = jnp.zeros((S5_LANES, cb), F32)
    prev_r = jnp.zeros((1, cb), F32)
    prev_i = jnp.zeros((1, cb), F32)
    for k, src in order:
        er, ei = st_re[src:src + 1, :], st_im[src:src + 1, :]
        nr = er + pr * prev_r - pi * prev_i
        ni = ei + pr * prev_i + pi * prev_r
        cr = jnp.where(sub == k, jnp.broadcast_to(nr, (S5_LANES, cb)), cr)
        ci = jnp.where(sub == k, jnp.broadcast_to(ni, (S5_LANES, cb)), ci)
        prev_r, prev_i = nr, ni
    return cr, ci


def _s5_fwd(up, bblk, ab, cblk, tt=128):
    n_rows, ws = up.shape
    nb, cw, cb2 = bblk.shape
    cb = cb2 // 2
    lc = n_rows // S5_LANES
    tt = _tile(lc, tt, 1)
    nt = lc // tt
    rt = S5_LANES * tt

    def body(u_ref, b_ref, ab_ref, c_ref, y_ref, s_ref, bu_ref, st_re, st_im):
        ps, ti = pl.program_id(1), pl.program_id(2)
        ar = jnp.broadcast_to(ab_ref[0:1, :], (S5_LANES, cb))
        ai = jnp.broadcast_to(ab_ref[1:2, :], (S5_LANES, cb))

        @pl.when((ps == 0) & (ti == 0))
        def _():
            st_re[...] = jnp.zeros(st_re.shape, F32)
            st_im[...] = jnp.zeros(st_im.shape, F32)

        @pl.when((ps == 1) & (ti == 0))
        def _():
            pr, pi = _cpow(ab_ref[0:1, :], ab_ref[1:2, :], lc)
            cr, ci = _chunk_carries(st_re, st_im, pr, pi, [(k, k - 1) for k in range(1, S5_LANES)])
            st_re[...] = cr
            st_im[...] = ci

        bu_ref[...] = jnp.dot(u_ref[...], b_ref[...], preferred_element_type=F32)

        def step(t, carry, store):
            sr, si = carry
            rows = pl.ds(pl.multiple_of(t * S5_LANES, S5_LANES), S5_LANES)
            nr = ar * sr - ai * si + bu_ref[rows, 0:cb]
            ni = ar * si + ai * sr + bu_ref[rows, cb:cb2]
            if store:
                s_ref[rows, 0:cb] = nr
                s_ref[rows, cb:cb2] = ni
            return nr, ni

        @pl.when(ps == 0)
        def _():
            sr, si = lax.fori_loop(0, tt, functools.partial(step, store=False), (st_re[...], st_im[...]))
            st_re[...] = sr
            st_im[...] = si

        @pl.when(ps == 1)
        def _():
            sr, si = lax.fori_loop(0, tt, functools.partial(step, store=True), (st_re[...], st_im[...]))
            st_re[...] = sr
            st_im[...] = si
            y_ref[...] = jnp.dot(s_ref[...].astype(BF16), c_ref[...], preferred_element_type=F32)

    return pl.pallas_call(
        body, name="s5_fwd", grid=(nb, 2, nt),
        in_specs=[pl.BlockSpec((rt, cw), lambda j, ps, ti: (ti, j)),
                  pl.BlockSpec((None, cw, cb2), lambda j, ps, ti: (j, 0, 0)),
                  pl.BlockSpec((None, 2, cb), lambda j, ps, ti: (j, 0, 0)),
                  pl.BlockSpec((None, cb2, cw), lambda j, ps, ti: (j, 0, 0))],
        out_specs=[pl.BlockSpec((rt, cw), lambda j, ps, ti: (ti * ps, j)),
                   pl.BlockSpec((None, rt, cb2), lambda j, ps, ti: (j, ti * ps, 0))],
        out_shape=[jax.ShapeDtypeStruct((n_rows, ws), F32), jax.ShapeDtypeStruct((nb, n_rows, cb2), F32)],
        scratch_shapes=[pltpu.VMEM((rt, cb2), F32), pltpu.VMEM((S5_LANES, cb), F32), pltpu.VMEM((S5_LANES, cb), F32)],
        compiler_params=_params(("arbitrary", "arbitrary", "arbitrary")),
    )(up, bblk, ab, cblk)


def _s5_bwd(dyp, up, s_all, bblk_t, ab, cblk_t, tt=128):
    n_rows, ws = up.shape
    nb, cb2, cw = bblk_t.shape
    cb = cb2 // 2
    lc = n_rows // S5_LANES
    tt = _tile(lc, tt, 1)
    nt = lc // tt
    rt = S5_LANES * tt

    def body(dy_ref, u_ref, s_ref, bt_ref, ab_ref, ct_ref, du_ref, db_ref, dc_ref, da_ref, ds_ref, st_re, st_im):
        ps, ti = pl.program_id(1), pl.program_id(2)
        ar = jnp.broadcast_to(ab_ref[0:1, :], (S5_LANES, cb))
        ai = jnp.broadcast_to(ab_ref[1:2, :], (S5_LANES, cb))

        @pl.when((ps == 0) & (ti == 0))
        def _():
            st_re[...] = jnp.zeros(st_re.shape, F32)
            st_im[...] = jnp.zeros(st_im.shape, F32)
            db_ref[...] = jnp.zeros(db_ref.shape, F32)
            dc_ref[...] = jnp.zeros(dc_ref.shape, F32)
            da_ref[...] = jnp.zeros(da_ref.shape, F32)

        @pl.when((ps == 1) & (ti == 0))
        def _():
            pr, pi = _cpow(ab_ref[0:1, :], -ab_ref[1:2, :], lc)
            cr, ci = _chunk_carries(st_re, st_im, pr, pi, [(k, k + 1) for k in range(S5_LANES - 2, -1, -1)])
            st_re[...] = cr
            st_im[...] = ci

        ds_ref[...] = jnp.dot(dy_ref[...], ct_ref[...], preferred_element_type=F32)

        def rows_of(i):
            return pl.ds(pl.multiple_of((tt - 1 - i) * S5_LANES, S5_LANES), S5_LANES)

        def step0(i, carry):
            gr, gi = carry
            rows = rows_of(i)
            return (ar * gr + ai * gi + ds_ref[rows, 0:cb], ar * gi - ai * gr + ds_ref[rows, cb:cb2])

        def step1(i, carry):
            gr, gi, acr, aci = carry
            rows = rows_of(i)
            sr, si = s_ref[rows, 0:cb], s_ref[rows, cb:cb2]
            acr = acr + sr * gr + si * gi
            aci = aci + sr * gi - si * gr
            nr = ar * gr + ai * gi + ds_ref[rows, 0:cb]
            ni = ar * gi - ai * gr + ds_ref[rows, cb:cb2]
            ds_ref[rows, 0:cb] = nr
            ds_ref[rows, cb:cb2] = ni
            return nr, ni, acr, aci

        @pl.when(ps == 0)
        def _():
            gr, gi = lax.fori_loop(0, tt, step0, (st_re[...], st_im[...]))
            st_re[...] = gr
            st_im[...] = gi

        @pl.when(ps == 1)
        def _():
            zero = jnp.zeros((S5_LANES, cb), F32)
            gr, gi, acr, aci = lax.fori_loop(0, tt, step1, (st_re[...], st_im[...], zero, zero))
            st_re[...] = gr
            st_im[...] = gi
            da_ref[0] += acr
            da_ref[1] += aci
            dsb = ds_ref[...].astype(BF16)
            du_ref[...] = jnp.dot(dsb, bt_ref[...], preferred_element_type=F32)
            db_ref[...] += _dot_tn(u_ref[...], dsb)
            dc_ref[...] += _dot_tn(dy_ref[...], s_ref[...].astype(BF16))

    def tile_idx(ps, ti):
        return (nt - 1 - ti) * ps + (nt - 1) * (1 - ps)

    return pl.pallas_call(
        body, name="s5_bwd", grid=(nb, 2, nt),
        in_specs=[pl.BlockSpec((rt, cw), lambda j, ps, ti: (nt - 1 - ti, j)),
                  pl.BlockSpec((rt, cw), lambda j, ps, ti: (tile_idx(ps, ti), j)),
                  pl.BlockSpec((None, rt, cb2), lambda j, ps, ti: (j, tile_idx(ps, ti), 0)),
                  pl.BlockSpec((None, cb2, cw), lambda j, ps, ti: (j, 0, 0)),
                  pl.BlockSpec((None, 2, cb), lambda j, ps, ti: (j, 0, 0)),
                  pl.BlockSpec((None, cw, cb2), lambda j, ps, ti: (j, 0, 0))],
        out_specs=[pl.BlockSpec((rt, cw), lambda j, ps, ti: (tile_idx(ps, ti), j)),
                   pl.BlockSpec((None, cw, cb2), lambda j, ps, ti: (j, 0, 0)),
                   pl.BlockSpec((None, cw, cb2), lambda j, ps, ti: (j, 0, 0)),
                   pl.BlockSpec((None, 2, S5_LANES, cb), lambda j, ps, ti: (j, 0, 0, 0))],
        out_shape=[jax.ShapeDtypeStruct((n_rows, ws), F32), jax.ShapeDtypeStruct((nb, cw, cb2), F32),
                   jax.ShapeDtypeStruct((nb, cw, cb2), F32), jax.ShapeDtypeStruct((nb, 2, S5_LANES, cb), F32)],
        scratch_shapes=[pltpu.VMEM((rt, cb2), F32), pltpu.VMEM((S5_LANES, cb), F32), pltpu.VMEM((S5_LANES, cb), F32)],
        compiler_params=_params(("arbitrary", "arbitrary", "arbitrary")),
    )(dyp, up, s_all, bblk_t, ab, cblk_t)


def _s5_discretize(a_re, a_im, log_dt, b_re, b_im):
    dt = jnp.exp(log_dt)[:, None]
    mag = jnp.exp(a_re * dt)
    abar_re = mag * jnp.cos(a_im * dt)
    abar_im = mag * jnp.sin(a_im * dt)
    nr, ni = abar_re - 1.0, abar_im
    den = a_re * a_re + a_im * a_im
    fr = (nr * a_re + ni * a_im) / den
    fi = (ni * a_re - nr * a_im) / den
    bbar_re = fr[..., None] * b_re - fi[..., None] * b_im
    bbar_im = fr[..., None] * b_im + fi[..., None] * b_re
    return abar_re, abar_im, bbar_re, bbar_im


def _perm_rows(a):
    n, c = a.shape
    return a.reshape(S5_LANES, n // S5_LANES, c).transpose(1, 0, 2).reshape(n, c)


def _unperm_rows(a):
    n, c = a.shape
    return a.reshape(n // S5_LANES, S5_LANES, c).transpose(1, 0, 2).reshape(n, c)


HBM = pl.BlockSpec(memory_space=pltpu.HBM)
SEM = pl.BlockSpec(memory_space=pltpu.SEMAPHORE)
ANY = pl.BlockSpec(memory_space=pl.ANY)
EFFECT = pltpu.SideEffectType.DATAFLOW_SIDE_EFFECTING
COPIES_PER_BUFFER = {"ag_ici": 3, "ag_fwd": 3, "pair": N_CHIPS, "scatter": 3, "half": 1}
PAIRED_KINDS = ("pair", "scatter")


def _place():
    x, y, c = lax.axis_index("x"), lax.axis_index("y"), lax.axis_index("c")
    chips = [(1 - x, y), (x, 1 - y), (1 - x, 1 - y)]
    return x, y, c, 2 * x + y, chips


def _n_copies(kind, n_bufs):
    return COPIES_PER_BUFFER[kind] * (n_bufs // 2 if kind in PAIRED_KINDS else n_bufs)


def _comm_copies(kind, bufs):
    x, y, c, s, chips = _place()
    sib = (x, y, 1 - c)
    out = []
    if kind == "ag_ici":
        for w in bufs:
            for cx, cy in chips:
                out.append((w.at[s, c], w.at[s, c], w.at[2 * cx + cy, c], (cx, cy, c)))
    elif kind == "ag_fwd":
        for w in bufs:
            for cx, cy in chips:
                sj = 2 * cx + cy
                out.append((w.at[sj, c], w.at[sj, c], w.at[sj, 1 - c], sib))
    elif kind == "pair":
        n = len(bufs) // 2
        for g, got in zip(bufs[:n], bufs[n:]):
            for t in range(N_CHIPS):
                out.append((g.at[t, 1 - c], got.at[t], got.at[t], sib))
    elif kind == "scatter":
        n = len(bufs) // 2
        for p, got in zip(bufs[:n], bufs[n:]):
            for cx, cy in chips:
                out.append((p.at[2 * cx + cy], got.at[s], got.at[2 * cx + cy], (cx, cy, c)))
    elif kind == "half":
        for f in bufs:
            out.append((f.at[c], f.at[c], f.at[1 - c], sib))
    return out


def _comm_fused(name, kind, bufs):
    n = len(bufs)
    ncp = _n_copies(kind, n)

    def body(*refs):
        outs = refs[n:2 * n]
        send, recv = refs[2 * n:]
        copies = _comm_copies(kind, outs)
        started = []
        for k, (src, dst, _, peer) in enumerate(copies):
            cp = pltpu.make_async_remote_copy(src_ref=src, dst_ref=dst, send_sem=send.at[k], recv_sem=recv.at[k],
                                              device_id=peer, device_id_type=MESH)
            cp.start()
            started.append(cp)
        for k, (_, _, land, peer) in enumerate(copies):
            pltpu.make_async_remote_copy(src_ref=land, dst_ref=land, send_sem=send.at[k], recv_sem=recv.at[k],
                                         device_id=peer, device_id_type=MESH).wait_recv()
        for cp in started:
            cp.wait_send()

    return pl.pallas_call(
        body, name=name, in_specs=[ANY] * n, out_specs=[ANY] * n,
        out_shape=[jax.ShapeDtypeStruct(b.shape, b.dtype) for b in bufs],
        input_output_aliases={k: k for k in range(n)},
        scratch_shapes=[pltpu.SemaphoreType.DMA((ncp,))] * 2,
    )(*bufs)


def _comm_start(name, kind, bufs):
    n = len(bufs)
    ncp = _n_copies(kind, n)

    def body(*refs):
        send, recv = refs[n:n + ncp], refs[n + ncp:n + 2 * ncp]
        outs = refs[n + 2 * ncp:2 * n + 2 * ncp]
        token = refs[2 * n + 2 * ncp]
        for k, (src, dst, _, peer) in enumerate(_comm_copies(kind, outs)):
            pltpu.make_async_remote_copy(src_ref=src, dst_ref=dst, send_sem=send[k], recv_sem=recv[k],
                                         device_id=peer, device_id_type=MESH).start()
        token[...] = jnp.zeros(token.shape, token.dtype)

    res = pl.pallas_call(
        body, name=name, in_specs=[HBM] * n,
        out_specs=[SEM] * (2 * ncp) + [HBM] * n + [pl.BlockSpec(memory_space=pltpu.VMEM)],
        out_shape=[pltpu.SemaphoreType.DMA(())] * (2 * ncp) + [pltpu.HBM(b.shape, b.dtype) for b in bufs]
        + [jax.ShapeDtypeStruct((8, 128), F32)],
        input_output_aliases={k: 2 * ncp + k for k in range(n)},
        compiler_params=pltpu.CompilerParams(has_side_effects=EFFECT),
    )(*[pltpu.with_memory_space_constraint(b, pltpu.HBM) for b in bufs])
    return list(res[:ncp]), list(res[ncp:2 * ncp]), list(res[2 * ncp:2 * ncp + n]), res[2 * ncp + n]


def _comm_wait(name, kind, bufs, send_sems, recv_sems, after):
    n = len(bufs)
    ncp = _n_copies(kind, n)

    def body(*refs):
        send, recv = refs[n:n + ncp], refs[n + ncp:n + 2 * ncp]
        outs = refs[n + 2 * ncp + 1:]
        for k, (src, _, land, peer) in enumerate(_comm_copies(kind, outs)):
            cp = pltpu.make_async_remote_copy(src_ref=src, dst_ref=land, send_sem=send[k], recv_sem=recv[k],
                                              device_id=peer, device_id_type=MESH)
            cp.wait_send()
            cp.wait_recv()

    return pl.pallas_call(
        body, name=name, in_specs=[HBM] * n + [SEM] * (2 * ncp) + [ANY], out_specs=[HBM] * n,
        out_shape=[pltpu.HBM(b.shape, b.dtype) for b in bufs],
        input_output_aliases={k: k for k in range(n)},
        compiler_params=pltpu.CompilerParams(has_side_effects=EFFECT),
    )(*bufs, *send_sems, *recv_sems, after)


def _pair_add(g, got, core):
    nchip, _, r, cw = g.shape
    tr = _tile(r, 512, 16)

    def body(c_ref, a_ref, b_ref, o_ref):
        o_ref[...] = (a_ref[...].astype(F32) + b_ref[...].astype(F32)).astype(o_ref.dtype)

    return pl.pallas_call(
        body, name="grads_pair_add",
        grid_spec=pltpu.PrefetchScalarGridSpec(
            num_scalar_prefetch=1, grid=(nchip, r // tr),
            in_specs=[pl.BlockSpec((None, None, tr, cw), lambda s, i, c_ref: (s, c_ref[0], i, 0)),
                      pl.BlockSpec((None, tr, cw), lambda s, i, c_ref: (s, i, 0))],
            out_specs=pl.BlockSpec((None, tr, cw), lambda s, i, c_ref: (s, i, 0))),
        out_shape=jax.ShapeDtypeStruct((nchip, r, cw), BF16),
        compiler_params=_params(("arbitrary", "arbitrary")),
    )(core, g, got)


def _chip_sum(parts, got, idx):
    _, r, cw = parts.shape
    tr = _tile(r, 512, 16)

    def body(i_ref, own_ref, a_ref, b_ref, c_ref, o_ref):
        o_ref[...] = ((own_ref[...].astype(F32) + a_ref[...].astype(F32)) + b_ref[...].astype(F32)) + c_ref[...].astype(F32)

    def slot(k):
        return pl.BlockSpec((None, tr, cw), lambda i, i_ref: (i_ref[k], i, 0))

    return pl.pallas_call(
        body, name="grads_chip_sum",
        grid_spec=pltpu.PrefetchScalarGridSpec(
            num_scalar_prefetch=1, grid=(r // tr,), in_specs=[slot(0), slot(1), slot(2), slot(3)], out_specs=slot(4)),
        out_shape=jax.ShapeDtypeStruct((2, r, cw), F32),
        compiler_params=_params(("arbitrary",)),
    )(idx, parts, got, got, got)


def _small_all_reduce(vec):
    r, cw = vec.shape

    def body(v_ref, o_ref, sib_buf, chip_buf, send, recv):
        x, y, c, s, chips = _place()
        cp = pltpu.make_async_remote_copy(
            src_ref=v_ref, dst_ref=sib_buf, send_sem=send.at[0], recv_sem=recv.at[0],
            device_id=(x, y, 1 - c), device_id_type=MESH)
        cp.start()
        cp.wait()
        chip_buf[s] = v_ref[...] + sib_buf[...]
        cps = []
        for j, (cx, cy) in enumerate(chips):
            cp = pltpu.make_async_remote_copy(
                src_ref=chip_buf.at[s], dst_ref=chip_buf.at[s], send_sem=send.at[1 + j], recv_sem=recv.at[1 + j],
                device_id=(cx, cy, c), device_id_type=MESH)
            cp.start()
            cps.append(cp)
        for j, (cx, cy) in enumerate(chips):
            pltpu.make_async_remote_copy(
                src_ref=chip_buf.at[s], dst_ref=chip_buf.at[2 * cx + cy], send_sem=send.at[1 + j],
                recv_sem=recv.at[1 + j], device_id=(cx, cy, c), device_id_type=MESH).wait_recv()
        for cp in cps:
            cp.wait_send()
        o_ref[...] = ((chip_buf[0] + chip_buf[1]) + chip_buf[2]) + chip_buf[3]

    vm = pl.BlockSpec(memory_space=pltpu.VMEM)
    return pl.pallas_call(
        body, name="small_all_reduce", in_specs=[vm], out_specs=vm,
        out_shape=jax.ShapeDtypeStruct((r, cw), F32),
        scratch_shapes=[pltpu.VMEM((r, cw), F32), pltpu.VMEM((N_CHIPS, r, cw), F32),
                        pltpu.SemaphoreType.DMA((4,)), pltpu.SemaphoreType.DMA((4,))],
        compiler_params=_params(),
    )(vec)


def _adamw_math(w, g, m, v):
    m = ADAM_B1 * m + (1.0 - ADAM_B1) * g
    v = ADAM_B2 * v + (1.0 - ADAM_B2) * (g * g)
    m_hat = m / (1.0 - ADAM_B1 ** ADAM_STEP)
    v_hat = v / (1.0 - ADAM_B2 ** ADAM_STEP)
    delta = -ADAM_LR * (m_hat / (jnp.sqrt(v_hat) + ADAM_EPS) + ADAM_WD * w)
    return delta, m, v


def _adamw(name, w, m, v, g, g_half=0, g_row_off=0, tr=256):
    r, cw = w.shape
    tr = _tile(math.gcd(r, g_row_off) if g_row_off else r, tr)
    off = g_row_off // tr

    def body(w_ref, m_ref, v_ref, g_ref, go_ref, d_ref, mo_ref, vo_ref):
        g_v = g_ref[...]
        delta, m_n, v_n = _adamw_math(w_ref[...], g_v, m_ref[...], v_ref[...])
        go_ref[...] = g_v
        d_ref[...] = delta
        mo_ref[...] = m_n
        vo_ref[...] = v_n

    tile = pl.BlockSpec((tr, cw), lambda i: (i, 0))
    out = jax.ShapeDtypeStruct((r, cw), F32)
    return pl.pallas_call(
        body, name=name, grid=(r // tr,),
        in_specs=[tile, tile, tile, pl.BlockSpec((None, tr, cw), lambda i: (g_half, i + off, 0))],
        out_specs=[tile] * 4, out_shape=[out] * 4,
        compiler_params=_params(("arbitrary",)),
    )(w, m, v, g)


def kernel(x, mem, g_ffn1, w1_gate, w1_up, w1_down, g_mix, w_in, ssm_a_re, ssm_a_im, ssm_log_dt, ssm_b_re, ssm_b_im, ssm_c_re, ssm_c_im, ssm_d, w_glu, b_glu, w_pool, pool_scale, g_out_ssm, g_out_pool, w_out, g_xattn, g_mem, w_q, w_k, w_v, w_o, g_ffn2, w2_gate, w2_up, w2_down, g_final, loss_target, m_g_ffn1, m_w1_gate, m_w1_up, m_w1_down, m_g_mix, m_w_in, m_ssm_a_re, m_ssm_a_im, m_ssm_log_dt, m_ssm_b_re, m_ssm_b_im, m_ssm_c_re, m_ssm_c_im, m_ssm_d, m_w_glu, m_b_glu, m_w_pool, m_pool_scale, m_g_out_ssm, m_g_out_pool, m_w_out, m_g_xattn, m_g_mem, m_w_q, m_w_k, m_w_v, m_w_o, m_g_ffn2, m_w2_gate, m_w2_up, m_w2_down, m_g_final, v_g_ffn1, v_w1_gate, v_w1_up, v_w1_down, v_g_mix, v_w_in, v_ssm_a_re, v_ssm_a_im, v_ssm_log_dt, v_ssm_b_re, v_ssm_b_im, v_ssm_c_re, v_ssm_c_im, v_ssm_d, v_w_glu, v_b_glu, v_w_pool, v_pool_scale, v_g_out_ssm, v_g_out_pool, v_w_out, v_g_xattn, v_g_mem, v_w_q, v_w_k, v_w_v, v_w_o, v_g_ffn2, v_w2_gate, v_w2_up, v_w2_down, v_g_final):
    local = dict(locals())
    wts = {n: local[n] for n in WEIGHTS}
    mom = {n: local["m_" + n] for n in WEIGHTS}
    var = {n: local["v_" + n] for n in WEIGHTS}

    x2 = x[0]
    mem2 = mem[0]
    tgt = loss_target[0]
    t_rows, d = x2.shape
    fs = w1_gate.shape[-1]
    ds_ = w_in.shape[1]
    ws = d // 2
    n_pg = len(POOL_WINDOWS)
    pw = ws // n_pg
    n_grp = ws // SSM_GROUP
    n_state = ssm_a_re.shape[-1]
    cx_, cy_, cc_ = lax.axis_index("x"), lax.axis_index("y"), lax.axis_index("c")
    chip = (2 * cx_ + cy_).astype(jnp.int32)
    core = cc_.astype(jnp.int32).reshape(1)
    chip_idx = jnp.stack([chip, chip ^ 2, chip ^ 1, chip ^ 3, cc_.astype(jnp.int32)])

    glu_rows = w_glu[0].reshape(-1, d)
    pool_rows = w_pool[0].reshape(-1, d)
    gh, ph = glu_rows.shape[0] // 2, pool_rows.shape[0] // 2
    rh = 3 * ds_ + gh + ph

    def own_slot(src):
        src = src.astype(BF16)
        return lax.dynamic_update_slice(lax.empty((N_CHIPS,) + src.shape, BF16), src[None], (chip, 0, 0, 0))

    src_b2 = jnp.stack([
        jnp.concatenate([w_in[0], w_out[0], w_q[0], glu_rows[:gh], pool_rows[:ph]], 0),
        jnp.concatenate([w_k[0], w_v[0], w_o[0], glu_rows[gh:], pool_rows[ph:]], 0)])
    w_bufs = [own_slot(jnp.stack([w1_gate[0], w1_up[0]])), own_slot(w1_down[0].reshape(2, fs // 2, d)),
              own_slot(src_b2), own_slot(jnp.stack([w2_gate[0], w2_up[0]])),
              own_slot(w2_down[0].reshape(2, fs // 2, d))]
    ag_send, ag_recv, w_bufs, ag_token = _comm_start("weights_start", "ag_ici", w_bufs)

    def gathered(k, after):
        w = _comm_wait("weights_wait_%d" % k, "ag_ici", [w_bufs[k]], ag_send[3 * k:3 * k + 3],
                       ag_recv[3 * k:3 * k + 3], after)
        return _comm_fused("weights_forward_%d" % k, "ag_fwd", w)[0]

    ga1 = gathered(0, ag_token)
    n1 = _rmsnorm("norm_ffn1", x2, wts['g_ffn1'].reshape(1, -1), deps=[ag_token])
    a1, b1, hm1 = _ffn_up("ffn1_up", n1, ga1, d, fs)
    gd1 = gathered(1, hm1).reshape(N_CHIPS, fs, d)
    h1 = _mm_nn("ffn1_down", hm1, gd1, (None, fs, d), lambda j, k: (k, 0, 0), N_CHIPS, d, F32, res=x2, alpha=0.5)
    gb2 = gathered(2, h1)
    wglu_full = gb2[:, :, 3 * ds_:3 * ds_ + gh, :].reshape(ws, ws)
    wpool_full = gb2[:, :, 3 * ds_ + gh:, :].reshape(N_CHIPS, n_pg, pw // N_CHIPS, pw)
    wpool_full = wpool_full.transpose(1, 0, 2, 3).reshape(n_pg, pw, pw)
    DD = {'w_in': (0, 0), 'w_out': (0, 1), 'w_q': (0, 2), 'w_k': (1, 0), 'w_v': (1, 1), 'w_o': (1, 2)}

    def mm_dd(name, a, wname, out_dtype, res=None):
        h, q = DD[wname]
        return _mm_nn(name, a, gb2, (None, None, ds_, d), lambda j, k: (k, h, q, 0), N_CHIPS, d, out_dtype, res=res)

    def mm_dd_t(name, pairs, out_dtype):
        ps = [(dy, gb2, (None, None, ds_, d), functools.partial(lambda s, h, q: (s, h, q, 0), h=DD[w][0], q=DD[w][1]))
              for dy, w in pairs]
        return _mm_nt_cols(name, ps, N_CHIPS, ds_, [out_dtype])[0]

    def vec(n):
        return wts[n].reshape(1, -1)

    disc_in = (ssm_a_re[0], ssm_a_im[0], ssm_log_dt[0], ssm_b_re[0], ssm_b_im[0])
    (abar_re, abar_im, bbar_re, bbar_im), disc_vjp = jax.vjp(_s5_discretize, *disc_in)
    gpb = min(S5_GROUPS_PER_BLOCK, n_grp)
    nb = n_grp // gpb
    cb = gpb * n_state
    eye = jnp.eye(gpb, dtype=F32)

    def blockdiag(t):
        return jnp.einsum('jgph,gk->jghkp', t.reshape(nb, gpb, n_state, SSM_GROUP), eye).reshape(nb, gpb * SSM_GROUP, cb)

    def blockdiag_c(t):
        return jnp.einsum('jghp,gk->jkpgh', t.reshape(nb, gpb, SSM_GROUP, n_state), eye).reshape(nb, cb, gpb * SSM_GROUP)

    bblk = jnp.concatenate([blockdiag(bbar_re), blockdiag(bbar_im)], -1).astype(BF16)
    cblk = jnp.concatenate([blockdiag_c(ssm_c_re[0]), -blockdiag_c(ssm_c_im[0])], 1).astype(BF16)
    ab = jnp.stack([abar_re.reshape(nb, cb), abar_im.reshape(nb, cb)], 1)

    n2 = _rmsnorm("norm_mix", h1, vec('g_mix'))
    u = mm_dd("mix_in", n2, 'w_in', F32)

    up = _perm_rows(u[:, :ws]).astype(BF16)
    ylin_p, s_all = _s5_fwd(up, bblk, ab, cblk)
    ylin = _unperm_rows(ylin_p)

    def gelu_fn(r, v):
        y1 = r[0] + v[0] * r[1]
        y2 = jax.nn.gelu(y1)
        return [y2, y2], []
    y2, y2b = _rowwise("s5_gelu", gelu_fn, [ylin, (u, 0, ws)], [vec('ssm_d')], [(ws, F32), (ws, BF16)])
    z = _mm_nn("s5_glu", y2b, wglu_full, (ws, ws), lambda j, k: (0, 0), 1, ws, F32)

    def glu_fn(r, v):
        y3 = r[0] * _sigmoid(r[1] + v[0])
        return [_rms_fwd(y3, v[1])], []
    m_ssm = _rowwise("s5_gate_norm", glu_fn, [y2, z], [vec('b_glu'), vec('g_out_ssm')], [(ws, BF16)])[0]

    pooled, zp = _pool_fwd(u, ws // pw, wpool_full, vec('pool_scale'))
    m_pool = _rmsnorm("norm_pool", zp, vec('g_out_pool'))
    merged = jnp.concatenate([m_ssm, m_pool], -1)
    h2 = mm_dd("mix_out", merged, 'w_out', F32, res=h1)

    memn = _rmsnorm("norm_mem", mem2, vec('g_mem'))
    k_mem = mm_dd("attn_k", memn, 'w_k', BF16)
    v_mem = mm_dd("attn_v", memn, 'w_v', BF16)
    hn = _rmsnorm("norm_xattn", h2, vec('g_xattn'))
    q = mm_dd("attn_q", hn, 'w_q', BF16)
    o = _attn_fwd(q, k_mem, v_mem)
    h3 = mm_dd("attn_out", o, 'w_o', F32, res=h2)

    n4 = _rmsnorm("norm_ffn2", h3, vec('g_ffn2'))
    ga2 = gathered(3, h3)
    a2, b2, hm2 = _ffn_up("ffn2_up", n4, ga2, d, fs)
    gd2 = gathered(4, hm2).reshape(N_CHIPS, fs, d)
    h4 = _mm_nn("ffn2_down", hm2, gd2, (None, fs, d), lambda j, k: (k, 0, 0), N_CHIPS, d, F32, res=h3, alpha=0.5)

    def loss_fn(r, v):
        h, t = r
        e = _rms_fwd(h, v[0]) - t
        dy = e * (1.0 / d)
        dh, dg = _rms_bwd(dy, h, v[0])
        part = jnp.sum(_colsum(e * e), axis=1, keepdims=True) * (0.5 / d)
        return [dh, 0.5 * dh], [_colsum(dg), jnp.broadcast_to(part, (1, 128))]
    dh4, dy_f2, dg_final, loss_row = _rowwise("loss_head", loss_fn, [h4, tgt], [g_final.reshape(1, -1)],
                                              [(d, F32), (d, BF16)], [d, 128])

    def rs_begin(tag, gbufs):
        n = len(gbufs)
        land = [lax.empty((N_CHIPS,) + g.shape[2:], BF16) for g in gbufs]
        got = _comm_fused(tag + "_pair_exchange", "pair", list(gbufs) + land)[n:]
        parts = [_pair_add(g, r, core) for g, r in zip(gbufs, got)]
        land = [lax.empty(p.shape, BF16) for p in parts]
        send, recv, thru, token = _comm_start(tag + "_scatter_start", "scatter", parts + land)
        return (send, recv, thru), token

    def rs_end(tag, handle, after):
        send, recv, thru = handle
        n = len(thru) // 2
        res = _comm_wait(tag + "_scatter_wait", "scatter", thru, send, recv, after)
        full = [_chip_sum(p, g2, chip_idx) for p, g2 in zip(res[:n], res[n:])]
        return _comm_fused(tag + "_half_exchange", "half", full)

    def ffn_down_bwd(tag, dy_half, a, b, hm, gd_l):
        da, db = _mm_nt_cols(tag + "_down_bwd", [(dy_half, gd_l, (None, fs, d), lambda s: (s, 0, 0))],
                             N_CHIPS, fs, [BF16, BF16], epi=_swiglu_bwd, extras=[a, b])
        g_down = _mm_tn(tag + "_dw_down", hm, dy_half, fs, d, N_CHIPS, 1, jax.ShapeDtypeStruct((N_CHIPS, fs, d), BF16),
                        (None, fs, d), lambda p, q: (p, 0, 0))
        return da, db, g_down.reshape(N_CHIPS, 2, fs // 2, d)

    def ffn_up_bwd(tag, da, db, n_in, ga_l, deps=()):
        wblk = (None, None, d, fs)
        dn = _mm_nt_k(tag + "_up_bwd", [(da, ga_l, wblk, lambda s: (s, 0, 0, 0)), (db, ga_l, wblk, lambda s: (s, 1, 0, 0))],
                      N_CHIPS, d, F32, deps=deps)
        g_up = _mm_tn(tag + "_dw_gate", n_in, da, d, fs, 1, N_CHIPS, jax.ShapeDtypeStruct((N_CHIPS, 2, d, fs), BF16),
                      wblk, lambda p, q: (q, 0, 0, 0))
        g_up = _mm_tn(tag + "_dw_up", n_in, db, d, fs, 1, N_CHIPS, None, wblk, lambda p, q: (q, 1, 0, 0), into=g_up)
        return dn, g_up

    def dw_dd(name, a, dy, wname, grad_b2):
        h, q = DD[wname]
        return _mm_tn(name, a, dy, ds_, d, N_CHIPS, 1, jax.ShapeDtypeStruct((N_CHIPS, 2, rh, d), BF16),
                      (None, None, ds_, d), lambda p, qq: (p, h, q, 0), into=grad_b2)

    def norm_bwd(name, dn, h, gname, dres, deps=()):
        def fn(r, v):
            dx, dg = _rms_bwd(r[0], r[1], v[0])
            tot = dx + r[2]
            return [tot, tot], [_colsum(dg)]
        return _rowwise(name, fn, [dn, h, dres], [vec(gname)], [(d, F32), (d, BF16)], [d], deps=deps)

    da2, db2, g_down2 = ffn_down_bwd("ffn2", dy_f2, a2, b2, hm2, gd2)
    dn4, g_up2 = ffn_up_bwd("ffn2", da2, db2, n4, ga2)
    rs_f2, tok_f2 = rs_begin("ffn2", [g_up2, g_down2])
    dh3, dh3b, dg_ffn2 = norm_bwd("norm_ffn2_bwd", dn4, h3, 'g_ffn2', dh4, deps=[tok_f2])

    do = mm_dd_t("attn_out_bwd", [(dh3b, 'w_o')], BF16)
    grad_b2 = dw_dd("attn_dw_o", o, dh3b, 'w_o', None)
    dq, dk, dv = _attn_bwd(q, k_mem, v_mem, do)
    dkb, dvb = dk.astype(BF16), dv.astype(BF16)
    grad_b2 = dw_dd("attn_dw_q", hn, dq, 'w_q', grad_b2)
    dhn = mm_dd_t("attn_q_bwd", [(dq, 'w_q')], F32)
    dh2, dh2b, dg_xattn = norm_bwd("norm_xattn_bwd", dhn, h2, 'g_xattn', dh3)
    grad_b2 = dw_dd("attn_dw_k", memn, dkb, 'w_k', grad_b2)
    grad_b2 = dw_dd("attn_dw_v", memn, dvb, 'w_v', grad_b2)
    dmemn = mm_dd_t("attn_kv_bwd", [(dkb, 'w_k'), (dvb, 'w_v')], F32)
    dg_mem = _rowwise("norm_mem_bwd", lambda r, v: ([], [_colsum(_rms_bwd(r[0], r[1], v[0])[1])]),
                      [dmemn, mem2], [vec('g_mem')], [], [d])[0]

    dmerged = mm_dd_t("mix_out_bwd", [(dh2b, 'w_out')], F32)
    grad_b2 = dw_dd("mix_dw_out", merged, dh2b, 'w_out', grad_b2)

    def gate_bwd_fn(r, v):
        dm, y2_v, z_v = r
        sg = _sigmoid(z_v + v[0])
        y3 = y2_v * sg
        dy3, dg = _rms_bwd(dm, y3, v[1])
        dz = dy3 * y3 * (1.0 - sg)
        return [dy3 * sg, dz], [_colsum(dg), _colsum(dz)]
    dy2a, dzb, dg_out_ssm, db_glu = _rowwise("s5_gate_norm_bwd", gate_bwd_fn, [(dmerged, 0, ws), y2, z],
                                             [vec('b_glu'), vec('g_out_ssm')], [(ws, F32), (ws, BF16)], [ws, ws])
    dy2b_ = _mm_nt_cols("s5_glu_bwd", [(dzb, wglu_full, (ws, ws), lambda s: (0, 0))], 1, ws, [F32])[0]
    dw_glu = _mm_tn("s5_dw_glu", y2b, dzb, ws, ws, 1, 1, jax.ShapeDtypeStruct((ws, ws), F32), (ws, ws), lambda p, q: (0, 0))

    def gelu_bwd_fn(r, v):
        dy2 = r[0] + r[1]
        us = r[3]
        y1 = r[2] + v[0] * us
        kk = math.sqrt(2.0 / math.pi)
        th = jnp.tanh(kk * (y1 + 0.044715 * y1 * y1 * y1))
        dgelu = 0.5 * (1.0 + th) + 0.5 * y1 * (1.0 - th * th) * kk * (1.0 + 3.0 * 0.044715 * y1 * y1)
        dy1 = dy2 * dgelu
        return [dy1, dy1 * v[0]], [_colsum(dy1 * us)]
    dy1b, du_skip, d_ssm_d = _rowwise("s5_gelu_bwd", gelu_bwd_fn, [dy2a, dy2b_, ylin, (u, 0, ws)], [vec('ssm_d')],
                                      [(ws, BF16), (ws, F32)], [ws])

    bblk_t = jnp.swapaxes(bblk, 1, 2)
    cblk_t = jnp.swapaxes(cblk, 1, 2)
    du_p, d_bblk, d_cblk_t, d_ab = _s5_bwd(_perm_rows(dy1b), up, s_all, bblk_t, ab, cblk_t)
    du_ssm = _unperm_rows(du_p)

    dzp, dg_out_pool = _rowwise("norm_pool_bwd", lambda r, v: (lambda dx, dg: ([dx], [_colsum(dg)]))(*_rms_bwd(r[0], r[1], v[0])),
                                [(dmerged, 1, ws), zp], [vec('g_out_pool')], [(ws, F32)], [ws])
    dps, dw_pool, d_pool_scale = _pool_bwd1(dzp, pooled, wpool_full, vec('pool_scale'))
    du_pool = _pool_bwd2(dps, n_pg)

    dub = _rowwise("mix_du", lambda r, v: ([jnp.concatenate([r[0] + r[1], r[2]], -1)], []),
                   [du_ssm, du_skip, du_pool], [], [(d, BF16)])[0]
    dn2 = mm_dd_t("mix_in_bwd", [(dub, 'w_in')], F32)
    grad_b2 = dw_dd("mix_dw_in", n2, dub, 'w_in', grad_b2)
    glu_g = dw_glu.reshape(N_CHIPS, 2, gh, d).astype(BF16)
    pool_g = dw_pool.reshape(n_pg, N_CHIPS, pw // N_CHIPS, pw).transpose(1, 0, 2, 3).reshape(N_CHIPS, 2, ph, d).astype(BF16)
    grad_b2 = lax.dynamic_update_slice(grad_b2, glu_g, (0, 0, 3 * ds_, 0))
    grad_b2 = lax.dynamic_update_slice(grad_b2, pool_g, (0, 0, 3 * ds_ + gh, 0))
    rs_mix, tok_mix = rs_begin("mixers", [grad_b2])
    dh1, dh1b, dg_mix = norm_bwd("norm_mix_bwd", dn2, h1, 'g_mix', dh2, deps=[tok_mix])

    dy_f1 = _rowwise("ffn1_half", lambda r, v: ([0.5 * r[0]], []), [dh1], [], [(d, BF16)])[0]
    da1, db1, g_down1 = ffn_down_bwd("ffn1", dy_f1, a1, b1, hm1, gd1)
    rs_d1, tok_d1 = rs_begin("ffn1_down", [g_down1])
    dn1, g_up1 = ffn_up_bwd("ffn1", da1, db1, n1, ga1, deps=[tok_d1])
    rs_u1, tok_u1 = rs_begin("ffn1_up", [g_up1])
    grad_x, _, dg_ffn1 = norm_bwd("norm_ffn1_bwd", dn1, x2, 'g_ffn1', dh1, deps=[tok_u1])

    def undiag(t):
        return jnp.einsum('jghkp,gk->jgph', t.reshape(nb, gpb, SSM_GROUP, gpb, n_state), eye).reshape(n_grp, n_state, SSM_GROUP)

    d_bbar_re, d_bbar_im = undiag(d_bblk[:, :, :cb]), undiag(d_bblk[:, :, cb:])
    d_c_re = undiag(d_cblk_t[:, :, :cb]).transpose(0, 2, 1)
    d_c_im = -undiag(d_cblk_t[:, :, cb:]).transpose(0, 2, 1)
    d_abar = jnp.sum(d_ab, axis=2).reshape(nb, 2, gpb, n_state)
    d_abar_re = d_abar[:, 0].reshape(n_grp, n_state)
    d_abar_im = d_abar[:, 1].reshape(n_grp, n_state)
    d_a_re, d_a_im, d_log_dt, d_b_re, d_b_im = disc_vjp((d_abar_re, d_abar_im, d_bbar_re, d_bbar_im))

    small_g = {'g_ffn1': dg_ffn1, 'g_mix': dg_mix, 'ssm_a_re': d_a_re, 'ssm_a_im': d_a_im, 'ssm_log_dt': d_log_dt,
               'ssm_b_re': d_b_re, 'ssm_b_im': d_b_im, 'ssm_c_re': d_c_re, 'ssm_c_im': d_c_im, 'ssm_d': d_ssm_d,
               'b_glu': db_glu, 'pool_scale': d_pool_scale, 'g_out_ssm': dg_out_ssm, 'g_out_pool': dg_out_pool,
               'g_xattn': dg_xattn, 'g_mem': dg_mem, 'g_ffn2': dg_ffn2, 'g_final': dg_final}
    sizes = [wts[n].size for n in SMALL]
    total = sum(sizes) + 128
    rows_s = -(-total // 1024) * 8
    flat = jnp.concatenate([small_g[n].reshape(-1) for n in SMALL] + [loss_row.reshape(-1)])
    flat = jnp.pad(flat, (0, rows_s * 128 - total)).reshape(rows_s, 128)
    red = _small_all_reduce(flat).reshape(-1)
    loss = red[sum(sizes)]

    def flat_small(t):
        return jnp.pad(jnp.concatenate([t[n].reshape(-1) for n in SMALL]), (0, rows_s * 128 - sum(sizes))).reshape(rows_s, 128)
    sg_, sd_, sm_, sv_ = _adamw("adamw_small", flat_small(wts), flat_small(mom), flat_small(var), red.reshape(1, rows_s, 128))
    out = {}
    off = 0
    for n, sz in zip(SMALL, sizes):
        for key, arr in (('grad', sg_), ('delta', sd_), ('m', sm_), ('v', sv_)):
            out[key, n] = arr.reshape(-1)[off:off + sz].reshape(wts[n].shape)
        off += sz

    def upd(n, g_arr, half, row_off, shape2):
        res = _adamw("adamw_" + n, wts[n].reshape(shape2), mom[n].reshape(shape2), var[n].reshape(shape2), g_arr, half, row_off)
        for key, arr in zip(('grad', 'delta', 'm', 'v'), res):
            out[key, n] = arr.reshape(wts[n].shape)
        return res[3]

    full_up2, full_down2 = rs_end("ffn2", rs_f2, tok_u1)
    full_b2, = rs_end("mixers", rs_mix, full_up2)
    full_down1, = rs_end("ffn1_down", rs_d1, full_b2)
    upd('w2_gate', full_up2, 0, 0, (d, fs))
    upd('w2_up', full_up2, 1, 0, (d, fs))
    upd('w2_down', full_down2.reshape(1, fs, d), 0, 0, (fs, d))
    upd('w1_down', full_down1.reshape(1, fs, d), 0, 0, (fs, d))
    for n, (h, q) in DD.items():
        last = upd(n, full_b2, h, q * ds_, (ds_, d))
    glu_shape, pool_shape = (ws // N_CHIPS, ws), (n_pg * pw // N_CHIPS, pw)
    upd('w_glu', full_b2[:, 3 * ds_:3 * ds_ + gh].reshape((1,) + glu_shape), 0, 0, glu_shape)
    upd('w_pool', full_b2[:, 3 * ds_ + gh:].reshape((1,) + pool_shape), 0, 0, pool_shape)
    full_up1, = rs_end("ffn1_up", rs_u1, last)
    upd('w1_gate', full_up1, 0, 0, (d, fs))
    upd('w1_up', full_up1, 1, 0, (d, fs))

    return (loss, grad_x[None], *[out['grad', n] for n in WEIGHTS], *[out['delta', n] for n in WEIGHTS],
            *[out['m', n] for n in WEIGHTS], *[out['v', n] for n in WEIGHTS])
```

```python
import functools
import math

import jax
import jax.numpy as jnp
from jax import lax
from jax.experimental import pallas as pl
from jax.experimental.pallas import tpu as pltpu

F32 = jnp.float32
BF16 = jnp.bfloat16
EPS = 1e-6
ADAM_LR, ADAM_B1, ADAM_B2, ADAM_EPS, ADAM_WD, ADAM_STEP = 0.001, 0.9, 0.999, 1e-08, 0.01, 10
POOL_WINDOWS = (2, 4, 8, 16)
SSM_GROUP = 16
S5_GROUPS_PER_BLOCK = 16
S5_LANES = 8
MEM_HEADS = 4
N_CHIPS = 4
VMEM_LIMIT_V7X = 56 * 1024 * 1024
MESH = pl.DeviceIdType.MESH

WEIGHTS = ['g_ffn1', 'w1_gate', 'w1_up', 'w1_down', 'g_mix', 'w_in', 'ssm_a_re', 'ssm_a_im', 'ssm_log_dt',
           'ssm_b_re', 'ssm_b_im', 'ssm_c_re', 'ssm_c_im', 'ssm_d', 'w_glu', 'b_glu', 'w_pool', 'pool_scale',
           'g_out_ssm', 'g_out_pool', 'w_out', 'g_xattn', 'g_mem', 'w_q', 'w_k', 'w_v', 'w_o', 'g_ffn2',
           'w2_gate', 'w2_up', 'w2_down', 'g_final']
BIG = ['w1_gate', 'w1_up', 'w1_down', 'w_in', 'w_glu', 'w_pool', 'w_out', 'w_q', 'w_k', 'w_v', 'w_o',
       'w2_gate', 'w2_up', 'w2_down']
SMALL = [n for n in WEIGHTS if n not in BIG]


def _tile(n, target, mult=8):
    best = None
    for d in range(1, n + 1):
        if n % d == 0 and d <= target and d % mult == 0:
            best = d
    return best if best is not None else n


def _params(sem=None):
    if sem is None:
        return pltpu.CompilerParams(vmem_limit_bytes=VMEM_LIMIT_V7X)
    return pltpu.CompilerParams(dimension_semantics=sem, vmem_limit_bytes=VMEM_LIMIT_V7X)


def _sigmoid(x):
    return 1.0 / (1.0 + jnp.exp(-x))


def _rms_fwd(x, g):
    r = lax.rsqrt(jnp.mean(x * x, axis=-1, keepdims=True) + EPS)
    return x * r * g


def _rms_bwd(dy, x, g):
    r = lax.rsqrt(jnp.mean(x * x, axis=-1, keepdims=True) + EPS)
    dxh = dy * g
    dx = r * dxh - x * (r * r * r) * jnp.mean(dxh * x, axis=-1, keepdims=True)
    return dx, dy * x * r


def _colsum(v):
    return jnp.sum(v, axis=0, keepdims=True)


def _rowwise(name, fn, rows, vecs, out_defs, red_defs=(), tm=256, deps=()):
    rows = [r if isinstance(r, tuple) else (r, 0, r.shape[1]) for r in rows]
    t_rows = rows[0][0].shape[0]
    tm = _tile(t_rows, tm)
    nr, nv, no, nd, nx = len(rows), len(vecs), len(out_defs), len(red_defs), len(deps)

    def body(*refs):
        r, v = refs[:nr], refs[nr:nr + nv]
        o, d = refs[nr + nv + nx:nr + nv + nx + no], refs[nr + nv + nx + no:]
        outs, reds = fn([x[...] for x in r], [x[...] for x in v])
        for ref, val in zip(o, outs):
            ref[...] = val.astype(ref.dtype)
        if nd:
            @pl.when(pl.program_id(0) == 0)
            def _():
                for ref in d:
                    ref[...] = jnp.zeros(ref.shape, ref.dtype)
            for ref, val in zip(d, reds):
                ref[...] += val

    in_specs = [pl.BlockSpec((tm, w), functools.partial(lambda i, cb: (i, cb), cb=cb)) for (_, cb, w) in rows]
    in_specs += [pl.BlockSpec(v.shape, lambda i: (0, 0)) for v in vecs]
    in_specs += [pl.BlockSpec(memory_space=pl.ANY)] * nx
    out_specs = [pl.BlockSpec((tm, w), lambda i: (i, 0)) for (w, _) in out_defs]
    out_specs += [pl.BlockSpec((1, w), lambda i: (0, 0)) for w in red_defs]
    out_shape = [jax.ShapeDtypeStruct((t_rows, w), dt) for (w, dt) in out_defs]
    out_shape += [jax.ShapeDtypeStruct((1, w), F32) for w in red_defs]
    res = pl.pallas_call(
        body, name=name, grid=(t_rows // tm,), in_specs=in_specs, out_specs=out_specs, out_shape=out_shape,
        compiler_params=_params(("arbitrary",)),
    )(*[r[0] for r in rows], *vecs, *deps)
    return res


def _rmsnorm(name, x, g, tm=256, deps=()):
    return _rowwise(name, lambda r, v: ([_rms_fwd(r[0].astype(F32), v[0])], []), [x], [g],
                    [(x.shape[1], BF16)], tm=tm, deps=deps)[0]


def _mm_nn(name, a, b, b_block, b_idx, nk, n_out, out_dtype, res=None, alpha=1.0, tm=512):
    t_rows = a.shape[0]
    bk, tn = b_block[-2], b_block[-1]
    tm = _tile(t_rows, tm)
    nj = n_out // tn
    has_res = res is not None

    def body(*refs):
        a_ref, b_ref = refs[0], refs[1]
        res_ref = refs[2] if has_res else None
        o_ref = refs[2 + has_res]
        acc_ref = refs[3 + has_res]
        k = pl.program_id(2)
        p = jnp.dot(a_ref[...], b_ref[...], preferred_element_type=F32)

        @pl.when(k == 0)
        def _():
            acc_ref[...] = p

        @pl.when(k > 0)
        def _():
            acc_ref[...] += p

        @pl.when(k == nk - 1)
        def _():
            r = acc_ref[...]
            if has_res:
                r = res_ref[...] + alpha * r
            o_ref[...] = r.astype(o_ref.dtype)

    in_specs = [pl.BlockSpec((tm, bk), lambda j, i, k: (i, k)),
                pl.BlockSpec(b_block, lambda j, i, k: b_idx(j, k))]
    args = [a, b]
    if has_res:
        in_specs.append(pl.BlockSpec((tm, tn), lambda j, i, k: (i, j)))
        args.append(res)
    return pl.pallas_call(
        body, name=name, grid=(nj, t_rows // tm, nk), in_specs=in_specs,
        out_specs=pl.BlockSpec((tm, tn), lambda j, i, k: (i, j)),
        out_shape=jax.ShapeDtypeStruct((t_rows, n_out), out_dtype),
        scratch_shapes=[pltpu.VMEM((tm, tn), F32)],
        compiler_params=_params(("arbitrary", "arbitrary", "arbitrary")),
    )(*args)


def _dot_nt(x, w):
    return lax.dot_general(x, w, (((1,), (1,)), ((), ())), preferred_element_type=F32)


def _dot_tn(x, y):
    return lax.dot_general(x, y, (((0,), (0,)), ((), ())), preferred_element_type=F32)


def _mm_nt_cols(name, pairs, ns, bn, out_defs, epi=None, extras=(), tm=512, deps=()):
    t_rows = pairs[0][0].shape[0]
    tm = _tile(t_rows, tm)
    npair, nex, no, nx = len(pairs), len(extras), len(out_defs), len(deps)

    def body(*refs):
        acc = None
        for p in range(npair):
            part = _dot_nt(refs[2 * p][...], refs[2 * p + 1][...])
            acc = part if acc is None else acc + part
        ex = [r[...] for r in refs[2 * npair:2 * npair + nex]]
        outs = epi(acc, *ex) if epi is not None else (acc,)
        for ref, val in zip(refs[2 * npair + nex + nx:], outs):
            ref[...] = val.astype(ref.dtype)

    in_specs, args = [], []
    for (dy, w, w_block, w_idx) in pairs:
        in_specs.append(pl.BlockSpec((tm, dy.shape[1]), lambda s, i: (i, 0)))
        in_specs.append(pl.BlockSpec(w_block, functools.partial(lambda s, i, f: f(s), f=w_idx)))
        args += [dy, w]
    for e in extras:
        in_specs.append(pl.BlockSpec((tm, bn), lambda s, i: (i, s)))
        args.append(e)
    in_specs += [pl.BlockSpec(memory_space=pl.ANY)] * nx
    args += list(deps)
    res = pl.pallas_call(
        body, name=name, grid=(ns, t_rows // tm), in_specs=in_specs,
        out_specs=[pl.BlockSpec((tm, bn), lambda s, i: (i, s)) for _ in range(no)],
        out_shape=[jax.ShapeDtypeStruct((t_rows, ns * bn), dt) for dt in out_defs],
        compiler_params=_params(("arbitrary", "arbitrary")),
    )(*args)
    return res


def _mm_nt_k(name, pairs, ns, n_out, out_dtype, tm=512, deps=()):
    t_rows = pairs[0][0].shape[0]
    tm = _tile(t_rows, tm)
    npair, nx = len(pairs), len(deps)

    def body(*refs):
        o_ref, acc_ref = refs[2 * npair + nx], refs[2 * npair + nx + 1]
        s = pl.program_id(1)
        acc = None
        for p in range(npair):
            part = _dot_nt(refs[2 * p][...], refs[2 * p + 1][...])
            acc = part if acc is None else acc + part

        @pl.when(s == 0)
        def _():
            acc_ref[...] = acc

        @pl.when(s > 0)
        def _():
            acc_ref[...] += acc

        @pl.when(s == ns - 1)
        def _():
            o_ref[...] = acc_ref[...].astype(o_ref.dtype)

    in_specs, args = [], []
    for (a, w, w_block, w_idx) in pairs:
        in_specs.append(pl.BlockSpec((tm, w_block[-1]), lambda i, s: (i, s)))
        in_specs.append(pl.BlockSpec(w_block, functools.partial(lambda i, s, f: f(s), f=w_idx)))
        args += [a, w]
    in_specs += [pl.BlockSpec(memory_space=pl.ANY)] * nx
    args += list(deps)
    return pl.pallas_call(
        body, name=name, grid=(t_rows // tm, ns), in_specs=in_specs,
        out_specs=pl.BlockSpec((tm, n_out), lambda i, s: (i, 0)),
        out_shape=jax.ShapeDtypeStruct((t_rows, n_out), out_dtype),
        scratch_shapes=[pltpu.VMEM((tm, n_out), F32)],
        compiler_params=_params(("arbitrary", "arbitrary")),
    )(*args)


def _mm_tn(name, a, b, bk, bn, n_p, n_q, out_shape, out_block, out_idx, into=None, a_off=0, b_off=0, tt=512,
           deps=()):
    t_rows = a.shape[0]
    tt = _tile(t_rows, tt, 16)
    nt = t_rows // tt
    has_into = into is not None
    nx = len(deps)

    def body(*refs):
        a_ref, b_ref = refs[0], refs[1]
        o_ref, acc_ref = refs[2 + has_into + nx], refs[3 + has_into + nx]
        t = pl.program_id(2)
        part = _dot_tn(a_ref[...], b_ref[...])

        @pl.when(t == 0)
        def _():
            acc_ref[...] = part

        @pl.when(t > 0)
        def _():
            acc_ref[...] += part

        @pl.when(t == nt - 1)
        def _():
            o_ref[...] = acc_ref[...].astype(o_ref.dtype)

    in_specs = [pl.BlockSpec((tt, bk), lambda p, q, t: (t, p + a_off)),
                pl.BlockSpec((tt, bn), lambda p, q, t: (t, q + b_off))]
    args = [a, b]
    aliases = {}
    if has_into:
        in_specs.append(pl.BlockSpec(memory_space=pl.ANY))
        args.append(into)
        aliases = {2: 0}
        out_shape = jax.ShapeDtypeStruct(into.shape, into.dtype)
    in_specs += [pl.BlockSpec(memory_space=pl.ANY)] * nx
    args += list(deps)
    return pl.pallas_call(
        body, name=name, grid=(n_p, n_q, nt), in_specs=in_specs,
        out_specs=pl.BlockSpec(out_block, lambda p, q, t: out_idx(p, q)),
        out_shape=out_shape, scratch_shapes=[pltpu.VMEM((bk, bn), F32)],
        input_output_aliases=aliases,
        compiler_params=_params(("arbitrary", "arbitrary", "arbitrary")),
    )(*args)


def _ffn_up(name, n, ga, d_model, fs, tm=512):
    t_rows = n.shape[0]
    tm = _tile(t_rows, tm)

    def body(n_ref, wg_ref, wu_ref, a_ref, b_ref, h_ref):
        x = n_ref[...]
        a = jnp.dot(x, wg_ref[...], preferred_element_type=F32)
        b = jnp.dot(x, wu_ref[...], preferred_element_type=F32)
        a_ref[...] = a.astype(a_ref.dtype)
        b_ref[...] = b.astype(b_ref.dtype)
        h_ref[...] = (a * _sigmoid(a) * b).astype(h_ref.dtype)

    w_block = (None, None, d_model, fs)
    out = jax.ShapeDtypeStruct((t_rows, N_CHIPS * fs), BF16)
    return pl.pallas_call(
        body, name=name, grid=(N_CHIPS, t_rows // tm),
        in_specs=[pl.BlockSpec((tm, d_model), lambda s, i: (i, 0)),
                  pl.BlockSpec(w_block, lambda s, i: (s, 0, 0, 0)),
                  pl.BlockSpec(w_block, lambda s, i: (s, 1, 0, 0))],
        out_specs=[pl.BlockSpec((tm, fs), lambda s, i: (i, s))] * 3,
        out_shape=[out, out, out],
        compiler_params=_params(("arbitrary", "arbitrary")),
    )(n, ga, ga)


def _swiglu_bwd(dh, a, b):
    a = a.astype(F32)
    b = b.astype(F32)
    sg = _sigmoid(a)
    return dh * b * sg * (1.0 + a * (1.0 - sg)), dh * a * sg


def _attn_fwd(q, k, v, tm=512):
    t_rows, d_model = q.shape
    n_mem = k.shape[0]
    hd = d_model // MEM_HEADS
    scale = hd ** -0.5
    tm = _tile(t_rows, tm)

    def body(q_ref, k_ref, v_ref, o_ref):
        for h in range(MEM_HEADS):
            cols = slice(h * hd, (h + 1) * hd)
            s = _dot_nt(q_ref[:, cols], k_ref[:, cols]) * scale
            s = s - jnp.max(s, axis=-1, keepdims=True)
            e = jnp.exp(s)
            p = e / jnp.sum(e, axis=-1, keepdims=True)
            o_ref[:, cols] = jnp.dot(p.astype(BF16), v_ref[:, cols], preferred_element_type=F32).astype(o_ref.dtype)

    return pl.pallas_call(
        body, name="attn_fwd", grid=(t_rows // tm,),
        in_specs=[pl.BlockSpec((tm, d_model), lambda i: (i, 0)),
                  pl.BlockSpec((n_mem, d_model), lambda i: (0, 0)),
                  pl.BlockSpec((n_mem, d_model), lambda i: (0, 0))],
        out_specs=pl.BlockSpec((tm, d_model), lambda i: (i, 0)),
        out_shape=jax.ShapeDtypeStruct((t_rows, d_model), BF16),
        compiler_params=_params(("arbitrary",)),
    )(q, k, v)


def _attn_bwd(q, k, v, do, tm=512):
    t_rows, d_model = q.shape
    n_mem = k.shape[0]
    hd = d_model // MEM_HEADS
    scale = hd ** -0.5
    tm = _tile(t_rows, tm, 16)

    def body(q_ref, k_ref, v_ref, do_ref, dq_ref, dk_ref, dv_ref):
        @pl.when(pl.program_id(0) == 0)
        def _():
            dk_ref[...] = jnp.zeros(dk_ref.shape, F32)
            dv_ref[...] = jnp.zeros(dv_ref.shape, F32)

        for h in range(MEM_HEADS):
            cols = slice(h * hd, (h + 1) * hd)
            qh, kh, vh, doh = q_ref[:, cols], k_ref[:, cols], v_ref[:, cols], do_ref[:, cols]
            s = _dot_nt(qh, kh) * scale
            s = s - jnp.max(s, axis=-1, keepdims=True)
            e = jnp.exp(s)
            p = e / jnp.sum(e, axis=-1, keepdims=True)
            dv_ref[:, cols] += _dot_tn(p.astype(BF16), doh)
            dp = _dot_nt(doh, vh)
            ds = (p * (dp - jnp.sum(dp * p, axis=-1, keepdims=True)) * scale).astype(BF16)
            dq_ref[:, cols] = jnp.dot(ds, kh, preferred_element_type=F32).astype(dq_ref.dtype)
            dk_ref[:, cols] += _dot_tn(ds, qh)

    full = pl.BlockSpec((n_mem, d_model), lambda i: (0, 0))
    tile = pl.BlockSpec((tm, d_model), lambda i: (i, 0))
    return pl.pallas_call(
        body, name="attn_bwd", grid=(t_rows // tm,),
        in_specs=[tile, full, full, tile], out_specs=[tile, full, full],
        out_shape=[jax.ShapeDtypeStruct((t_rows, d_model), BF16),
                   jax.ShapeDtypeStruct((n_mem, d_model), F32), jax.ShapeDtypeStruct((n_mem, d_model), F32)],
        compiler_params=_params(("arbitrary",)),
    )(q, k, v, do)


def _split_bf16(v):
    hi = v.astype(BF16)
    return hi, (v - hi.astype(F32)).astype(BF16)


def _pool_window(g):
    return jnp.left_shift(jnp.int32(POOL_WINDOWS[0]), g)


def _pool_fwd(u, col_off, w_pool, scale, tt=256):
    t_rows = u.shape[0]
    pw = w_pool.shape[-1]
    ng = w_pool.shape[0]
    tt = _tile(t_rows, tt, 16)
    nt = t_rows // tt
    assert POOL_WINDOWS == tuple(2 << i for i in range(ng)) and tt >= POOL_WINDOWS[-1]

    def body(vc_ref, vp_ref, w_ref, sc_ref, pooled_ref, z_ref):
        g, i = pl.program_id(0), pl.program_id(1)
        w = _pool_window(g)
        r = lax.broadcasted_iota(jnp.int32, (tt, tt), 0)
        c = lax.broadcasted_iota(jnp.int32, (tt, tt), 1)
        band_c = ((c <= r) & (c > r - w)).astype(BF16)
        band_p = (c > r - w + tt).astype(BF16)
        vc = vc_ref[...]
        ch, cl = _split_bf16(vc)
        ph, plo = _split_bf16(vp_ref[...] * (i > 0).astype(F32))
        sums = (jnp.dot(band_c, ch, preferred_element_type=F32) + jnp.dot(band_c, cl, preferred_element_type=F32)
                + jnp.dot(band_p, ph, preferred_element_type=F32) + jnp.dot(band_p, plo, preferred_element_type=F32))
        t = i * tt + lax.broadcasted_iota(jnp.int32, (tt, 1), 0)
        cnt = jnp.minimum(t + 1, w).astype(F32)
        pooled = (sums / cnt - vc).astype(BF16)
        pooled_ref[...] = pooled
        z_ref[...] = jnp.dot(pooled, w_ref[...], preferred_element_type=F32) * sc_ref[...]

    return pl.pallas_call(
        body, name="pool_fwd", grid=(ng, nt),
        in_specs=[pl.BlockSpec((tt, pw), lambda g, i: (i, col_off + g)),
                  pl.BlockSpec((tt, pw), lambda g, i: (jnp.maximum(i - 1, 0), col_off + g)),
                  pl.BlockSpec((None, pw, pw), lambda g, i: (g, 0, 0)),
                  pl.BlockSpec((1, pw), lambda g, i: (0, g))],
        out_specs=[pl.BlockSpec((tt, pw), lambda g, i: (i, g))] * 2,
        out_shape=[jax.ShapeDtypeStruct((t_rows, ng * pw), BF16), jax.ShapeDtypeStruct((t_rows, ng * pw), F32)],
        compiler_params=_params(("arbitrary", "arbitrary")),
    )(u, u, w_pool, scale)


def _pool_bwd1(dz, pooled, w_pool, scale, tt=256):
    t_rows = dz.shape[0]
    pw = w_pool.shape[-1]
    ng = w_pool.shape[0]
    tt = _tile(t_rows, tt, 16)
    nt = t_rows // tt

    def body(dz_ref, p_ref, w_ref, sc_ref, dp_ref, dw_ref, dsc_ref):
        g, i = pl.program_id(0), pl.program_id(1)
        w = _pool_window(g)

        @pl.when(i == 0)
        def _():
            dw_ref[...] = jnp.zeros(dw_ref.shape, F32)
            dsc_ref[...] = jnp.zeros(dsc_ref.shape, F32)

        dz_v = dz_ref[...]
        pooled = p_ref[...]
        zpre = jnp.dot(pooled, w_ref[...], preferred_element_type=F32)
        dsc_ref[...] += _colsum(dz_v * zpre)
        dzs = (dz_v * sc_ref[...]).astype(BF16)
        dw_ref[...] += _dot_tn(pooled, dzs)
        t = i * tt + lax.broadcasted_iota(jnp.int32, (tt, 1), 0)
        cnt = jnp.minimum(t + 1, w).astype(F32)
        dp_ref[...] = _dot_nt(dzs, w_ref[...]) / cnt

    return pl.pallas_call(
        body, name="pool_bwd1", grid=(ng, nt),
        in_specs=[pl.BlockSpec((tt, pw), lambda g, i: (i, g)),
                  pl.BlockSpec((tt, pw), lambda g, i: (i, g)),
                  pl.BlockSpec((None, pw, pw), lambda g, i: (g, 0, 0)),
                  pl.BlockSpec((1, pw), lambda g, i: (0, g))],
        out_specs=[pl.BlockSpec((tt, pw), lambda g, i: (i, g)),
                   pl.BlockSpec((None, pw, pw), lambda g, i: (g, 0, 0)),
                   pl.BlockSpec((1, pw), lambda g, i: (0, g))],
        out_shape=[jax.ShapeDtypeStruct((t_rows, ng * pw), F32), jax.ShapeDtypeStruct((ng, pw, pw), F32),
                   jax.ShapeDtypeStruct((1, ng * pw), F32)],
        compiler_params=_params(("arbitrary", "arbitrary")),
    )(dz, pooled, w_pool, scale)


def _pool_bwd2(dps, ng, tt=256):
    t_rows, width = dps.shape
    pw = width // ng
    tt = _tile(t_rows, tt, 16)
    nt = t_rows // tt

    def body(dc_ref, dn_ref, dv_ref):
        g, i = pl.program_id(0), pl.program_id(1)
        w = _pool_window(g)
        r = lax.broadcasted_iota(jnp.int32, (tt, tt), 0)
        c = lax.broadcasted_iota(jnp.int32, (tt, tt), 1)
        band_c = ((c >= r) & (c < r + w)).astype(BF16)
        band_n = (c < r + w - tt).astype(BF16)
        dc = dc_ref[...]
        ch, cl = _split_bf16(dc)
        nh, nl = _split_bf16(dn_ref[...] * (i < nt - 1).astype(F32))
        sums = (jnp.dot(band_c, ch, preferred_element_type=F32) + jnp.dot(band_c, cl, preferred_element_type=F32)
                + jnp.dot(band_n, nh, preferred_element_type=F32) + jnp.dot(band_n, nl, preferred_element_type=F32))
        t = i * tt + lax.broadcasted_iota(jnp.int32, (tt, 1), 0)
        cnt = jnp.minimum(t + 1, w).astype(F32)
        dv_ref[...] = sums - dc * cnt

    return pl.pallas_call(
        body, name="pool_bwd2", grid=(ng, nt),
        in_specs=[pl.BlockSpec((tt, pw), lambda g, i: (i, g)),
                  pl.BlockSpec((tt, pw), lambda g, i: (jnp.minimum(i + 1, nt - 1), g))],
        out_specs=pl.BlockSpec((tt, pw), lambda g, i: (i, g)),
        out_shape=jax.ShapeDtypeStruct((t_rows, width), F32),
        compiler_params=_params(("arbitrary", "arbitrary")),
    )(dps, dps)


def _cpow(ar, ai, n):
    rr, ri, br, bi = None, None, ar, ai
    while n:
        if n & 1:
            rr, ri = (br, bi) if rr is None else (rr * br - ri * bi, rr * bi + ri * br)
        n >>= 1
        if n:
            br, bi = br * br - bi * bi, 2.0 * br * bi
    return rr, ri


def _chunk_carries(st_re, st_im, pr, pi, order):
    cb = st_re.shape[1]
    sub = lax.broadcasted_iota(jnp.int32, (S5_LANES, cb), 0)
    cr = jnp.zeros((S5_LANES, cb), F32)
    ci = jnp.zeros((S5_LANES, cb), F32)
    prev_r = jnp.zeros((1, cb), F32)
    prev_i = jnp.zeros((1, cb), F32)
    for k, src in order:
        er, ei = st_re[src:src + 1, :], st_im[src:src + 1, :]
        nr = er + pr * prev_r - pi * prev_i
        ni = ei + pr * prev_i + pi * prev_r
        cr = jnp.where(sub == k, jnp.broadcast_to(nr, (S5_LANES, cb)), cr)
        ci = jnp.where(sub == k, jnp.broadcast_to(ni, (S5_LANES, cb)), ci)
        prev_r, prev_i = nr, ni
    return cr, ci


def _s5_fwd(up, bblk, ab, cblk, tt=128):
    n_rows, ws = up.shape
    nb, cw, cb2 = bblk.shape
    cb = cb2 // 2
    lc = n_rows // S5_LANES
    tt = _tile(lc, tt, 1)
    nt = lc // tt
    rt = S5_LANES * tt

    def body(u_ref, b_ref, ab_ref, c_ref, y_ref, s_ref, bu_ref, st_re, st_im):
        ps, ti = pl.program_id(1), pl.program_id(2)
        ar = jnp.broadcast_to(ab_ref[0:1, :], (S5_LANES, cb))
        ai = jnp.broadcast_to(ab_ref[1:2, :], (S5_LANES, cb))

        @pl.when((ps == 0) & (ti == 0))
        def _():
            st_re[...] = jnp.zeros(st_re.shape, F32)
            st_im[...] = jnp.zeros(st_im.shape, F32)

        @pl.when((ps == 1) & (ti == 0))
        def _():
            pr, pi = _cpow(ab_ref[0:1, :], ab_ref[1:2, :], lc)
            cr, ci = _chunk_carries(st_re, st_im, pr, pi, [(k, k - 1) for k in range(1, S5_LANES)])
            st_re[...] = cr
            st_im[...] = ci

        bu_ref[...] = jnp.dot(u_ref[...], b_ref[...], preferred_element_type=F32)

        def step(t, carry, store):
            sr, si = carry
            rows = pl.ds(pl.multiple_of(t * S5_LANES, S5_LANES), S5_LANES)
            nr = ar * sr - ai * si + bu_ref[rows, 0:cb]
            ni = ar * si + ai * sr + bu_ref[rows, cb:cb2]
            if store:
                s_ref[rows, 0:cb] = nr
                s_ref[rows, cb:cb2] = ni
            return nr, ni

        @pl.when(ps == 0)
        def _():
            sr, si = lax.fori_loop(0, tt, functools.partial(step, store=False), (st_re[...], st_im[...]))
            st_re[...] = sr
            st_im[...] = si

        @pl.when(ps == 1)
        def _():
            sr, si = lax.fori_loop(0, tt, functools.partial(step, store=True), (st_re[...], st_im[...]))
            st_re[...] = sr
            st_im[...] = si
            y_ref[...] = jnp.dot(s_ref[...].astype(BF16), c_ref[...], preferred_element_type=F32)

    return pl.pallas_call(
        body, name="s5_fwd", grid=(nb, 2, nt),
        in_specs=[pl.BlockSpec((rt, cw), lambda j, ps, ti: (ti, j)),
                  pl.BlockSpec((None, cw, cb2), lambda j, ps, ti: (j, 0, 0)),
                  pl.BlockSpec((None, 2, cb), lambda j, ps, ti: (j, 0, 0)),
                  pl.BlockSpec((None, cb2, cw), lambda j, ps, ti: (j, 0, 0))],
        out_specs=[pl.BlockSpec((rt, cw), lambda j, ps, ti: (ti * ps, j)),
                   pl.BlockSpec((None, rt, cb2), lambda j, ps, ti: (j, ti * ps, 0))],
        out_shape=[jax.ShapeDtypeStruct((n_rows, ws), F32), jax.ShapeDtypeStruct((nb, n_rows, cb2), F32)],
        scratch_shapes=[pltpu.VMEM((rt, cb2), F32), pltpu.VMEM((S5_LANES, cb), F32), pltpu.VMEM((S5_LANES, cb), F32)],
        compiler_params=_params(("arbitrary", "arbitrary", "arbitrary")),
    )(up, bblk, ab, cblk)


def _s5_bwd(dyp, up, s_all, bblk_t, ab, cblk_t, tt=128):
    n_rows, ws = up.shape
    nb, cb2, cw = bblk_t.shape
    cb = cb2 // 2
    lc = n_rows // S5_LANES
    tt = _tile(lc, tt, 1)
    nt = lc // tt
    rt = S5_LANES * tt

    def body(dy_ref, u_ref, s_ref, bt_ref, ab_ref, ct_ref, du_ref, db_ref, dc_ref, da_ref, ds_ref, st_re, st_im):
        ps, ti = pl.program_id(1), pl.program_id(2)
        ar = jnp.broadcast_to(ab_ref[0:1, :], (S5_LANES, cb))
        ai = jnp.broadcast_to(ab_ref[1:2, :], (S5_LANES, cb))

        @pl.when((ps == 0) & (ti == 0))
        def _():
            st_re[...] = jnp.zeros(st_re.shape, F32)
            st_im[...] = jnp.zeros(st_im.shape, F32)
            db_ref[...] = jnp.zeros(db_ref.shape, F32)
            dc_ref[...] = jnp.zeros(dc_ref.shape, F32)
            da_ref[...] = jnp.zeros(da_ref.shape, F32)

        @pl.when((ps == 1) & (ti == 0))
        def _():
            pr, pi = _cpow(ab_ref[0:1, :], -ab_ref[1:2, :], lc)
            cr, ci = _chunk_carries(st_re, st_im, pr, pi, [(k, k + 1) for k in range(S5_LANES - 2, -1, -1)])
            st_re[...] = cr
            st_im[...] = ci

        ds_ref[...] = jnp.dot(dy_ref[...], ct_ref[...], preferred_element_type=F32)

        def rows_of(i):
            return pl.ds(pl.multiple_of((tt - 1 - i) * S5_LANES, S5_LANES), S5_LANES)

        def step0(i, carry):
            gr, gi = carry
            rows = rows_of(i)
            return (ar * gr + ai * gi + ds_ref[rows, 0:cb], ar * gi - ai * gr + ds_ref[rows, cb:cb2])

        def step1(i, carry):
            gr, gi, acr, aci = carry
            rows = rows_of(i)
            sr, si = s_ref[rows, 0:cb], s_ref[rows, cb:cb2]
            acr = acr + sr * gr + si * gi
            aci = aci + sr * gi - si * gr
            nr = ar * gr + ai * gi + ds_ref[rows, 0:cb]
            ni = ar * gi - ai * gr + ds_ref[rows, cb:cb2]
            ds_ref[rows, 0:cb] = nr
            ds_ref[rows, cb:cb2] = ni
            return nr, ni, acr, aci

        @pl.when(ps == 0)
        def _():
            gr, gi = lax.fori_loop(0, tt, step0, (st_re[...], st_im[...]))
            st_re[...] = gr
            st_im[...] = gi

        @pl.when(ps == 1)
        def _():
            zero = jnp.zeros((S5_LANES, cb), F32)
            gr, gi, acr, aci = lax.fori_loop(0, tt, step1, (st_re[...], st_im[...], zero, zero))
            st_re[...] = gr
            st_im[...] = gi
            da_ref[0] += acr
            da_ref[1] += aci
            dsb = ds_ref[...].astype(BF16)
            du_ref[...] = jnp.dot(dsb, bt_ref[...], preferred_element_type=F32)
            db_ref[...] += _dot_tn(u_ref[...], dsb)
            dc_ref[...] += _dot_tn(dy_ref[...], s_ref[...].astype(BF16))

    def tile_idx(ps, ti):
        return (nt - 1 - ti) * ps + (nt - 1) * (1 - ps)

    return pl.pallas_call(
        body, name="s5_bwd", grid=(nb, 2, nt),
        in_specs=[pl.BlockSpec((rt, cw), lambda j, ps, ti: (nt - 1 - ti, j)),
                  pl.BlockSpec((rt, cw), lambda j, ps, ti: (tile_idx(ps, ti), j)),
                  pl.BlockSpec((None, rt, cb2), lambda j, ps, ti: (j, tile_idx(ps, ti), 0)),
                  pl.BlockSpec((None, cb2, cw), lambda j, ps, ti: (j, 0, 0)),
                  pl.BlockSpec((None, 2, cb), lambda j, ps, ti: (j, 0, 0)),
                  pl.BlockSpec((None, cw, cb2), lambda j, ps, ti: (j, 0, 0))],
        out_specs=[pl.BlockSpec((rt, cw), lambda j, ps, ti: (tile_idx(ps, ti), j)),
                   pl.BlockSpec((None, cw, cb2), lambda j, ps, ti: (j, 0, 0)),
                   pl.BlockSpec((None, cw, cb2), lambda j, ps, ti: (j, 0, 0)),
                   pl.BlockSpec((None, 2, S5_LANES, cb), lambda j, ps, ti: (j, 0, 0, 0))],
        out_shape=[jax.ShapeDtypeStruct((n_rows, ws), F32), jax.ShapeDtypeStruct((nb, cw, cb2), F32),
                   jax.ShapeDtypeStruct((nb, cw, cb2), F32), jax.ShapeDtypeStruct((nb, 2, S5_LANES, cb), F32)],
        scratch_shapes=[pltpu.VMEM((rt, cb2), F32), pltpu.VMEM((S5_LANES, cb), F32), pltpu.VMEM((S5_LANES, cb), F32)],
        compiler_params=_params(("arbitrary", "arbitrary", "arbitrary")),
    )(dyp, up, s_all, bblk_t, ab, cblk_t)


def _s5_discretize(a_re, a_im, log_dt, b_re, b_im):
    dt = jnp.exp(log_dt)[:, None]
    mag = jnp.exp(a_re * dt)
    abar_re = mag * jnp.cos(a_im * dt)
    abar_im = mag * jnp.sin(a_im * dt)
    nr, ni = abar_re - 1.0, abar_im
    den = a_re * a_re + a_im * a_im
    fr = (nr * a_re + ni * a_im) / den
    fi = (ni * a_re - nr * a_im) / den
    bbar_re = fr[..., None] * b_re - fi[..., None] * b_im
    bbar_im = fr[..., None] * b_im + fi[..., None] * b_re
    return abar_re, abar_im, bbar_re, bbar_im


def _perm_rows(a):
    n, c = a.shape
    return a.reshape(S5_LANES, n // S5_LANES, c).transpose(1, 0, 2).reshape(n, c)


def _unperm_rows(a):
    n, c = a.shape
    return a.reshape(n // S5_LANES, S5_LANES, c).transpose(1, 0, 2).reshape(n, c)


HBM = pl.BlockSpec(memory_space=pltpu.HBM)
SEM = pl.BlockSpec(memory_space=pltpu.SEMAPHORE)
ANY = pl.BlockSpec(memory_space=pl.ANY)
EFFECT = pltpu.SideEffectType.DATAFLOW_SIDE_EFFECTING
COPIES_PER_BUFFER = {"ag_ici": 3, "ag_fwd": 3, "pair": N_CHIPS, "scatter": 3, "half": 1}
PAIRED_KINDS = ("pair", "scatter")


def _place():
    x, y, c = lax.axis_index("x"), lax.axis_index("y"), lax.axis_index("c")
    chips = [(1 - x, y), (x, 1 - y), (1 - x, 1 - y)]
    return x, y, c, 2 * x + y, chips


def _n_copies(kind, n_bufs):
    return COPIES_PER_BUFFER[kind] * (n_bufs // 2 if kind in PAIRED_KINDS else n_bufs)


def _comm_copies(kind, bufs):
    x, y, c, s, chips = _place()
    sib = (x, y, 1 - c)
    out = []
    if kind == "ag_ici":
        for w in bufs:
            for cx, cy in chips:
                out.append((w.at[s, c], w.at[s, c], w.at[2 * cx + cy, c], (cx, cy, c)))
    elif kind == "ag_fwd":
        for w in bufs:
            for cx, cy in chips:
                sj = 2 * cx + cy
                out.append((w.at[sj, c], w.at[sj, c], w.at[sj, 1 - c], sib))
    elif kind == "pair":
        n = len(bufs) // 2
        for g, got in zip(bufs[:n], bufs[n:]):
            for t in range(N_CHIPS):
                out.append((g.at[t, 1 - c], got.at[t], got.at[t], sib))
    elif kind == "scatter":
        n = len(bufs) // 2
        for p, got in zip(bufs[:n], bufs[n:]):
            for cx, cy in chips:
                out.append((p.at[2 * cx + cy], got.at[s], got.at[2 * cx + cy], (cx, cy, c)))
    elif kind == "half":
        for f in bufs:
            out.append((f.at[c], f.at[c], f.at[1 - c], sib))
    return out


def _comm_fused(name, kind, bufs):
    n = len(bufs)
    ncp = _n_copies(kind, n)

    def body(*refs):
        outs = refs[n:2 * n]
        send, recv = refs[2 * n:]
        copies = _comm_copies(kind, outs)
        started = []
        for k, (src, dst, _, peer) in enumerate(copies):
            cp = pltpu.make_async_remote_copy(src_ref=src, dst_ref=dst, send_sem=send.at[k], recv_sem=recv.at[k],
                                              device_id=peer, device_id_type=MESH)
            cp.start()
            started.append(cp)
        for k, (_, _, land, peer) in enumerate(copies):
            pltpu.make_async_remote_copy(src_ref=land, dst_ref=land, send_sem=send.at[k], recv_sem=recv.at[k],
                                         device_id=peer, device_id_type=MESH).wait_recv()
        for cp in started:
            cp.wait_send()

    return pl.pallas_call(
        body, name=name, in_specs=[ANY] * n, out_specs=[ANY] * n,
        out_shape=[jax.ShapeDtypeStruct(b.shape, b.dtype) for b in bufs],
        input_output_aliases={k: k for k in range(n)},
        scratch_shapes=[pltpu.SemaphoreType.DMA((ncp,))] * 2,
    )(*bufs)


def _comm_start(name, kind, bufs, after=None):
    n = len(bufs)
    ncp = _n_copies(kind, n)
    nx = 0 if after is None else 1

    def body(*refs):
        refs = refs[n + nx:]
        send, recv = refs[:ncp], refs[ncp:2 * ncp]
        outs = refs[2 * ncp:n + 2 * ncp]
        token = refs[n + 2 * ncp]
        for k, (src, dst, _, peer) in enumerate(_comm_copies(kind, outs)):
            pltpu.make_async_remote_copy(src_ref=src, dst_ref=dst, send_sem=send[k], recv_sem=recv[k],
                                         device_id=peer, device_id_type=MESH).start()
        token[...] = jnp.zeros(token.shape, token.dtype)

    res = pl.pallas_call(
        body, name=name, in_specs=[HBM] * n + [ANY] * nx,
        out_specs=[SEM] * (2 * ncp) + [HBM] * n + [pl.BlockSpec(memory_space=pltpu.VMEM)],
        out_shape=[pltpu.SemaphoreType.DMA(())] * (2 * ncp) + [pltpu.HBM(b.shape, b.dtype) for b in bufs]
        + [jax.ShapeDtypeStruct((8, 128), F32)],
        input_output_aliases={k: 2 * ncp + k for k in range(n)},
        compiler_params=pltpu.CompilerParams(has_side_effects=EFFECT),
    )(*[pltpu.with_memory_space_constraint(b, pltpu.HBM) for b in bufs], *([after] if nx else []))
    return list(res[:ncp]), list(res[ncp:2 * ncp]), list(res[2 * ncp:2 * ncp + n]), res[2 * ncp + n]


def _comm_wait(name, kind, bufs, send_sems, recv_sems, after):
    n = len(bufs)
    ncp = _n_copies(kind, n)

    def body(*refs):
        send, recv = refs[n:n + ncp], refs[n + ncp:n + 2 * ncp]
        outs = refs[n + 2 * ncp + 1:]
        for k, (src, _, land, peer) in enumerate(_comm_copies(kind, outs)):
            cp = pltpu.make_async_remote_copy(src_ref=src, dst_ref=land, send_sem=send[k], recv_sem=recv[k],
                                              device_id=peer, device_id_type=MESH)
            cp.wait_send()
            cp.wait_recv()

    return pl.pallas_call(
        body, name=name, in_specs=[HBM] * n + [SEM] * (2 * ncp) + [ANY], out_specs=[HBM] * n,
        out_shape=[pltpu.HBM(b.shape, b.dtype) for b in bufs],
        input_output_aliases={k: k for k in range(n)},
        compiler_params=pltpu.CompilerParams(has_side_effects=EFFECT),
    )(*bufs, *send_sems, *recv_sems, after)


def _pair_add(g, got, core):
    nchip, _, r, cw = g.shape
    tr = _tile(r, 512, 16)

    def body(c_ref, a_ref, b_ref, o_ref):
        o_ref[...] = a_ref[...] + b_ref[...]

    return pl.pallas_call(
        body, name="grads_pair_add",
        grid_spec=pltpu.PrefetchScalarGridSpec(
            num_scalar_prefetch=1, grid=(nchip, r // tr),
            in_specs=[pl.BlockSpec((None, None, tr, cw), lambda s, i, c_ref: (s, c_ref[0], i, 0)),
                      pl.BlockSpec((None, tr, cw), lambda s, i, c_ref: (s, i, 0))],
            out_specs=pl.BlockSpec((None, tr, cw), lambda s, i, c_ref: (s, i, 0))),
        out_shape=jax.ShapeDtypeStruct((nchip, r, cw), BF16),
        compiler_params=_params(("arbitrary", "arbitrary")),
    )(core, g, got)


def _chip_sum(parts, got, idx):
    _, r, cw = parts.shape
    tr = _tile(r, 512, 16)

    def body(i_ref, own_ref, a_ref, b_ref, c_ref, o_ref):
        o_ref[...] = ((own_ref[...].astype(F32) + a_ref[...].astype(F32)) + b_ref[...].astype(F32)) + c_ref[...].astype(F32)

    def slot(k):
        return pl.BlockSpec((None, tr, cw), lambda i, i_ref: (i_ref[k], i, 0))

    return pl.pallas_call(
        body, name="grads_chip_sum",
        grid_spec=pltpu.PrefetchScalarGridSpec(
            num_scalar_prefetch=1, grid=(r // tr,), in_specs=[slot(0), slot(1), slot(2), slot(3)], out_specs=slot(4)),
        out_shape=jax.ShapeDtypeStruct((2, r, cw), F32),
        compiler_params=_params(("arbitrary",)),
    )(idx, parts, got, got, got)


def _small_all_reduce(vec):
    r, cw = vec.shape

    def body(v_ref, o_ref, sib_buf, chip_buf, send, recv):
        x, y, c, s, chips = _place()
        cp = pltpu.make_async_remote_copy(
            src_ref=v_ref, dst_ref=sib_buf, send_sem=send.at[0], recv_sem=recv.at[0],
            device_id=(x, y, 1 - c), device_id_type=MESH)
        cp.start()
        cp.wait()
        chip_buf[s] = v_ref[...] + sib_buf[...]
        cps = []
        for j, (cx, cy) in enumerate(chips):
            cp = pltpu.make_async_remote_copy(
                src_ref=chip_buf.at[s], dst_ref=chip_buf.at[s], send_sem=send.at[1 + j], recv_sem=recv.at[1 + j],
                device_id=(cx, cy, c), device_id_type=MESH)
            cp.start()
            cps.append(cp)
        for j, (cx, cy) in enumerate(chips):
            pltpu.make_async_remote_copy(
                src_ref=chip_buf.at[s], dst_ref=chip_buf.at[2 * cx + cy], send_sem=send.at[1 + j],
                recv_sem=recv.at[1 + j], device_id=(cx, cy, c), device_id_type=MESH).wait_recv()
        for cp in cps:
            cp.wait_send()
        o_ref[...] = ((chip_buf[0] + chip_buf[1]) + chip_buf[2]) + chip_buf[3]

    vm = pl.BlockSpec(memory_space=pltpu.VMEM)
    return pl.pallas_call(
        body, name="small_all_reduce", in_specs=[vm], out_specs=vm,
        out_shape=jax.ShapeDtypeStruct((r, cw), F32),
        scratch_shapes=[pltpu.VMEM((r, cw), F32), pltpu.VMEM((N_CHIPS, r, cw), F32),
                        pltpu.SemaphoreType.DMA((4,)), pltpu.SemaphoreType.DMA((4,))],
        compiler_params=_params(),
    )(vec)


def _adamw_math(w, g, m, v):
    m = ADAM_B1 * m + (1.0 - ADAM_B1) * g
    v = ADAM_B2 * v + (1.0 - ADAM_B2) * (g * g)
    m_hat = m / (1.0 - ADAM_B1 ** ADAM_STEP)
    v_hat = v / (1.0 - ADAM_B2 ** ADAM_STEP)
    delta = -ADAM_LR * (m_hat / (jnp.sqrt(v_hat) + ADAM_EPS) + ADAM_WD * w)
    return delta, m, v


def _adamw(name, w, m, v, g, g_half=0, g_row_off=0, tr=256):
    r, cw = w.shape
    tr = _tile(math.gcd(r, g_row_off) if g_row_off else r, tr)
    off = g_row_off // tr

    def body(w_ref, m_ref, v_ref, g_ref, go_ref, d_ref, mo_ref, vo_ref):
        g_v = g_ref[...]
        delta, m_n, v_n = _adamw_math(w_ref[...], g_v, m_ref[...], v_ref[...])
        go_ref[...] = g_v
        d_ref[...] = delta
        mo_ref[...] = m_n
        vo_ref[...] = v_n

    tile = pl.BlockSpec((tr, cw), lambda i: (i, 0))
    out = jax.ShapeDtypeStruct((r, cw), F32)
    return pl.pallas_call(
        body, name=name, grid=(r // tr,),
        in_specs=[tile, tile, tile, pl.BlockSpec((None, tr, cw), lambda i: (g_half, i + off, 0))],
        out_specs=[tile] * 4, out_shape=[out] * 4,
        compiler_params=_params(("arbitrary",)),
    )(w, m, v, g)


def kernel(x, mem, g_ffn1, w1_gate, w1_up, w1_down, g_mix, w_in, ssm_a_re, ssm_a_im, ssm_log_dt, ssm_b_re, ssm_b_im, ssm_c_re, ssm_c_im, ssm_d, w_glu, b_glu, w_pool, pool_scale, g_out_ssm, g_out_pool, w_out, g_xattn, g_mem, w_q, w_k, w_v, w_o, g_ffn2, w2_gate, w2_up, w2_down, g_final, loss_target, m_g_ffn1, m_w1_gate, m_w1_up, m_w1_down, m_g_mix, m_w_in, m_ssm_a_re, m_ssm_a_im, m_ssm_log_dt, m_ssm_b_re, m_ssm_b_im, m_ssm_c_re, m_ssm_c_im, m_ssm_d, m_w_glu, m_b_glu, m_w_pool, m_pool_scale, m_g_out_ssm, m_g_out_pool, m_w_out, m_g_xattn, m_g_mem, m_w_q, m_w_k, m_w_v, m_w_o, m_g_ffn2, m_w2_gate, m_w2_up, m_w2_down, m_g_final, v_g_ffn1, v_w1_gate, v_w1_up, v_w1_down, v_g_mix, v_w_in, v_ssm_a_re, v_ssm_a_im, v_ssm_log_dt, v_ssm_b_re, v_ssm_b_im, v_ssm_c_re, v_ssm_c_im, v_ssm_d, v_w_glu, v_b_glu, v_w_pool, v_pool_scale, v_g_out_ssm, v_g_out_pool, v_w_out, v_g_xattn, v_g_mem, v_w_q, v_w_k, v_w_v, v_w_o, v_g_ffn2, v_w2_gate, v_w2_up, v_w2_down, v_g_final):
    local = dict(locals())
    wts = {n: local[n] for n in WEIGHTS}
    mom = {n: local["m_" + n] for n in WEIGHTS}
    var = {n: local["v_" + n] for n in WEIGHTS}

    x2 = x[0]
    mem2 = mem[0]
    tgt = loss_target[0]
    t_rows, d = x2.shape
    fs = w1_gate.shape[-1]
    ds_ = w_in.shape[1]
    ws = d // 2
    n_pg = len(POOL_WINDOWS)
    pw = ws // n_pg
    n_grp = ws // SSM_GROUP
    n_state = ssm_a_re.shape[-1]
    cx_, cy_, cc_ = lax.axis_index("x"), lax.axis_index("y"), lax.axis_index("c")
    chip = (2 * cx_ + cy_).astype(jnp.int32)
    core = cc_.astype(jnp.int32).reshape(1)
    chip_idx = jnp.stack([chip, chip ^ 2, chip ^ 1, chip ^ 3, cc_.astype(jnp.int32)])

    glu_rows = w_glu[0].reshape(-1, d)
    pool_rows = w_pool[0].reshape(-1, d)
    gh, ph = glu_rows.shape[0] // 2, pool_rows.shape[0] // 2
    rh = 3 * ds_ + gh + ph

    def own_slot(src):
        src = src.astype(BF16)
        return lax.dynamic_update_slice(lax.empty((N_CHIPS,) + src.shape, BF16), src[None], (chip, 0, 0, 0))

    src_b2 = jnp.stack([
        jnp.concatenate([w_in[0], w_out[0], w_q[0], glu_rows[:gh], pool_rows[:ph]], 0),
        jnp.concatenate([w_k[0], w_v[0], w_o[0], glu_rows[gh:], pool_rows[ph:]], 0)])
    w_bufs = [own_slot(jnp.stack([w1_gate[0], w1_up[0]])), own_slot(w1_down[0].reshape(2, fs // 2, d)),
              own_slot(src_b2), own_slot(jnp.stack([w2_gate[0], w2_up[0]])),
              own_slot(w2_down[0].reshape(2, fs // 2, d))]
    ag_send, ag_recv, w_bufs, ag_token = _comm_start("weights_start", "ag_ici", w_bufs)

    def gathered(k, after):
        w = _comm_wait("weights_wait_%d" % k, "ag_ici", [w_bufs[k]], ag_send[3 * k:3 * k + 3],
                       ag_recv[3 * k:3 * k + 3], after)
        return _comm_fused("weights_forward_%d" % k, "ag_fwd", w)[0]

    ga1 = gathered(0, ag_token)
    n1 = _rmsnorm("norm_ffn1", x2, wts['g_ffn1'].reshape(1, -1), deps=[ag_token])
    a1, b1, hm1 = _ffn_up("ffn1_up", n1, ga1, d, fs)
    gd1 = gathered(1, hm1).reshape(N_CHIPS, fs, d)
    h1 = _mm_nn("ffn1_down", hm1, gd1, (None, fs, d), lambda j, k: (k, 0, 0), N_CHIPS, d, F32, res=x2, alpha=0.5)
    gb2 = gathered(2, h1)
    wglu_full = gb2[:, :, 3 * ds_:3 * ds_ + gh, :].reshape(ws, ws)
    wpool_full = gb2[:, :, 3 * ds_ + gh:, :].reshape(N_CHIPS, n_pg, pw // N_CHIPS, pw)
    wpool_full = wpool_full.transpose(1, 0, 2, 3).reshape(n_pg, pw, pw)
    DD = {'w_in': (0, 0), 'w_out': (0, 1), 'w_q': (0, 2), 'w_k': (1, 0), 'w_v': (1, 1), 'w_o': (1, 2)}

    def mm_dd(name, a, wname, out_dtype, res=None):
        h, q = DD[wname]
        return _mm_nn(name, a, gb2, (None, None, ds_, d), lambda j, k: (k, h, q, 0), N_CHIPS, d, out_dtype, res=res)

    def mm_dd_t(name, pairs, out_dtype, deps=()):
        ps = [(dy, gb2, (None, None, ds_, d), functools.partial(lambda s, h, q: (s, h, q, 0), h=DD[w][0], q=DD[w][1]))
              for dy, w in pairs]
        return _mm_nt_cols(name, ps, N_CHIPS, ds_, [out_dtype], deps=deps)[0]

    def vec(n):
        return wts[n].reshape(1, -1)

    disc_in = (ssm_a_re[0], ssm_a_im[0], ssm_log_dt[0], ssm_b_re[0], ssm_b_im[0])
    (abar_re, abar_im, bbar_re, bbar_im), disc_vjp = jax.vjp(_s5_discretize, *disc_in)
    gpb = min(S5_GROUPS_PER_BLOCK, n_grp)
    nb = n_grp // gpb
    cb = gpb * n_state
    eye = jnp.eye(gpb, dtype=F32)

    def blockdiag(t):
        return jnp.einsum('jgph,gk->jghkp', t.reshape(nb, gpb, n_state, SSM_GROUP), eye).reshape(nb, gpb * SSM_GROUP, cb)

    def blockdiag_c(t):
        return jnp.einsum('jghp,gk->jkpgh', t.reshape(nb, gpb, SSM_GROUP, n_state), eye).reshape(nb, cb, gpb * SSM_GROUP)

    bblk = jnp.concatenate([blockdiag(bbar_re), blockdiag(bbar_im)], -1).astype(BF16)
    cblk = jnp.concatenate([blockdiag_c(ssm_c_re[0]), -blockdiag_c(ssm_c_im[0])], 1).astype(BF16)
    ab = jnp.stack([abar_re.reshape(nb, cb), abar_im.reshape(nb, cb)], 1)

    n2 = _rmsnorm("norm_mix", h1, vec('g_mix'))
    u = mm_dd("mix_in", n2, 'w_in', F32)

    up = _perm_rows(u[:, :ws]).astype(BF16)
    ylin_p, s_all = _s5_fwd(up, bblk, ab, cblk)
    ylin = _unperm_rows(ylin_p)

    def gelu_fn(r, v):
        y1 = r[0] + v[0] * r[1]
        y2 = jax.nn.gelu(y1)
        return [y2, y2], []
    y2, y2b = _rowwise("s5_gelu", gelu_fn, [ylin, (u, 0, ws)], [vec('ssm_d')], [(ws, F32), (ws, BF16)])
    z = _mm_nn("s5_glu", y2b, wglu_full, (ws, ws), lambda j, k: (0, 0), 1, ws, F32)

    def glu_fn(r, v):
        y3 = r[0] * _sigmoid(r[1] + v[0])
        return [_rms_fwd(y3, v[1])], []
    m_ssm = _rowwise("s5_gate_norm", glu_fn, [y2, z], [vec('b_glu'), vec('g_out_ssm')], [(ws, BF16)])[0]

    pooled, zp = _pool_fwd(u, ws // pw, wpool_full, vec('pool_scale'))
    m_pool = _rmsnorm("norm_pool", zp, vec('g_out_pool'))
    merged = jnp.concatenate([m_ssm, m_pool], -1)
    h2 = mm_dd("mix_out", merged, 'w_out', F32, res=h1)

    memn = _rmsnorm("norm_mem", mem2, vec('g_mem'))
    k_mem = mm_dd("attn_k", memn, 'w_k', BF16)
    v_mem = mm_dd("attn_v", memn, 'w_v', BF16)
    hn = _rmsnorm("norm_xattn", h2, vec('g_xattn'))
    q = mm_dd("attn_q", hn, 'w_q', BF16)
    o = _attn_fwd(q, k_mem, v_mem)
    h3 = mm_dd("attn_out", o, 'w_o', F32, res=h2)

    n4 = _rmsnorm("norm_ffn2", h3, vec('g_ffn2'))
    ga2 = gathered(3, h3)
    a2, b2, hm2 = _ffn_up("ffn2_up", n4, ga2, d, fs)
    gd2 = gathered(4, hm2).reshape(N_CHIPS, fs, d)
    h4 = _mm_nn("ffn2_down", hm2, gd2, (None, fs, d), lambda j, k: (k, 0, 0), N_CHIPS, d, F32, res=h3, alpha=0.5)

    def loss_fn(r, v):
        h, t = r
        e = _rms_fwd(h, v[0]) - t
        dy = e * (1.0 / d)
        dh, dg = _rms_bwd(dy, h, v[0])
        part = jnp.sum(_colsum(e * e), axis=1, keepdims=True) * (0.5 / d)
        return [dh, 0.5 * dh], [_colsum(dg), jnp.broadcast_to(part, (1, 128))]
    dh4, dy_f2, dg_final, loss_row = _rowwise("loss_head", loss_fn, [h4, tgt], [g_final.reshape(1, -1)],
                                              [(d, F32), (d, BF16)], [d, 128])

    def rs_pair_start(tag, gbufs):
        land = [lax.empty((N_CHIPS,) + g.shape[2:], BF16) for g in gbufs]
        send, recv, thru, token = _comm_start(tag + "_pair_start", "pair", list(gbufs) + land)
        return (send, recv, thru), token

    def rs_scatter_start(tag, handle, after):
        send, recv, thru = handle
        n = len(thru) // 2
        res = _comm_wait(tag + "_pair_wait", "pair", thru, send, recv, after)
        parts = [_pair_add(g, r, core) for g, r in zip(res[:n], res[n:])]
        land = [lax.empty(p.shape, BF16) for p in parts]
        send, recv, thru, token = _comm_start(tag + "_scatter_start", "scatter", parts + land)
        return (send, recv, thru), token

    def rs_half_start(tag, handle, after):
        send, recv, thru = handle
        n = len(thru) // 2
        res = _comm_wait(tag + "_scatter_wait", "scatter", thru, send, recv, after)
        full = [_chip_sum(p, g2, chip_idx) for p, g2 in zip(res[:n], res[n:])]
        send, recv, thru, token = _comm_start(tag + "_half_start", "half", full)
        return (send, recv, thru), token

    def rs_finish(tag, handle, after):
        send, recv, thru = handle
        return _comm_wait(tag + "_half_wait", "half", thru, send, recv, after)

    wblk = (None, None, d, fs)

    def ffn_down_bwd(tag, dy_half, a, b, hm, gd_l, deps=()):
        da, db = _mm_nt_cols(tag + "_down_bwd", [(dy_half, gd_l, (None, fs, d), lambda s: (s, 0, 0))],
                             N_CHIPS, fs, [BF16, BF16], epi=_swiglu_bwd, extras=[a, b], deps=deps)
        g_down = _mm_tn(tag + "_dw_down", hm, dy_half, fs, d, N_CHIPS, 1, jax.ShapeDtypeStruct((N_CHIPS, fs, d), BF16),
                        (None, fs, d), lambda p, q: (p, 0, 0), tt=1024)
        return da, db, g_down.reshape(N_CHIPS, 2, fs // 2, d)

    def ffn_up_bwd(tag, da, db, ga_l, deps=()):
        return _mm_nt_k(tag + "_up_bwd", [(da, ga_l, wblk, lambda s: (s, 0, 0, 0)), (db, ga_l, wblk, lambda s: (s, 1, 0, 0))],
                        N_CHIPS, d, F32, deps=deps)

    def ffn_dw_up(tag, da, db, n_in, deps=()):
        g_up = _mm_tn(tag + "_dw_gate", n_in, da, d, fs, 1, N_CHIPS, jax.ShapeDtypeStruct((N_CHIPS, 2, d, fs), BF16),
                      wblk, lambda p, q: (q, 0, 0, 0), tt=1024, deps=deps)
        return _mm_tn(tag + "_dw_up", n_in, db, d, fs, 1, N_CHIPS, None, wblk, lambda p, q: (q, 1, 0, 0), into=g_up, tt=1024)

    def dw_dd(name, a, dy, wname, grad_b2):
        h, q = DD[wname]
        return _mm_tn(name, a, dy, ds_, d, N_CHIPS, 1, jax.ShapeDtypeStruct((N_CHIPS, 2, rh, d), BF16),
                      (None, None, ds_, d), lambda p, qq: (p, h, q, 0), into=grad_b2)

    def norm_bwd(name, dn, h, gname, dres, deps=()):
        def fn(r, v):
            dx, dg = _rms_bwd(r[0], r[1], v[0])
            tot = dx + r[2]
            return [tot, tot], [_colsum(dg)]
        return _rowwise(name, fn, [dn, h, dres], [vec(gname)], [(d, F32), (d, BF16)], [d], deps=deps)

    da2, db2, g_down2 = ffn_down_bwd("ffn2", dy_f2, a2, b2, hm2, gd2)
    dn4 = ffn_up_bwd("ffn2", da2, db2, ga2)
    g_up2 = ffn_dw_up("ffn2", da2, db2, n4)
    rs_f2, tok = rs_pair_start("ffn2", [g_up2, g_down2])
    dh3, dh3b, dg_ffn2 = norm_bwd("norm_ffn2_bwd", dn4, h3, 'g_ffn2', dh4, deps=[tok])
    rs_f2, tok = rs_scatter_start("ffn2", rs_f2, dh3b)

    do = mm_dd_t("attn_out_bwd", [(dh3b, 'w_o')], BF16, deps=[tok])
    grad_b2 = dw_dd("attn_dw_o", o, dh3b, 'w_o', None)
    dq, dk, dv = _attn_bwd(q, k_mem, v_mem, do)
    dkb, dvb = dk.astype(BF16), dv.astype(BF16)
    grad_b2 = dw_dd("attn_dw_q", hn, dq, 'w_q', grad_b2)
    dhn = mm_dd_t("attn_q_bwd", [(dq, 'w_q')], F32)
    dh2, dh2b, dg_xattn = norm_bwd("norm_xattn_bwd", dhn, h2, 'g_xattn', dh3)
    grad_b2 = dw_dd("attn_dw_k", memn, dkb, 'w_k', grad_b2)
    grad_b2 = dw_dd("attn_dw_v", memn, dvb, 'w_v', grad_b2)
    dmemn = mm_dd_t("attn_kv_bwd", [(dkb, 'w_k'), (dvb, 'w_v')], F32)
    dg_mem = _rowwise("norm_mem_bwd", lambda r, v: ([], [_colsum(_rms_bwd(r[0], r[1], v[0])[1])]),
                      [dmemn, mem2], [vec('g_mem')], [], [d])[0]

    dmerged = mm_dd_t("mix_out_bwd", [(dh2b, 'w_out')], F32)
    grad_b2 = dw_dd("mix_dw_out", merged, dh2b, 'w_out', grad_b2)

    def gate_bwd_fn(r, v):
        dm, y2_v, z_v = r
        sg = _sigmoid(z_v + v[0])
        y3 = y2_v * sg
        dy3, dg = _rms_bwd(dm, y3, v[1])
        dz = dy3 * y3 * (1.0 - sg)
        return [dy3 * sg, dz], [_colsum(dg), _colsum(dz)]
    dy2a, dzb, dg_out_ssm, db_glu = _rowwise("s5_gate_norm_bwd", gate_bwd_fn, [(dmerged, 0, ws), y2, z],
                                             [vec('b_glu'), vec('g_out_ssm')], [(ws, F32), (ws, BF16)], [ws, ws])
    dy2b_ = _mm_nt_cols("s5_glu_bwd", [(dzb, wglu_full, (ws, ws), lambda s: (0, 0))], 1, ws, [F32])[0]
    dw_glu = _mm_tn("s5_dw_glu", y2b, dzb, ws, ws, 1, 1, jax.ShapeDtypeStruct((ws, ws), F32), (ws, ws), lambda p, q: (0, 0))

    def gelu_bwd_fn(r, v):
        dy2 = r[0] + r[1]
        us = r[3]
        y1 = r[2] + v[0] * us
        kk = math.sqrt(2.0 / math.pi)
        th = jnp.tanh(kk * (y1 + 0.044715 * y1 * y1 * y1))
        dgelu = 0.5 * (1.0 + th) + 0.5 * y1 * (1.0 - th * th) * kk * (1.0 + 3.0 * 0.044715 * y1 * y1)
        dy1 = dy2 * dgelu
        return [dy1, dy1 * v[0]], [_colsum(dy1 * us)]
    dy1b, du_skip, d_ssm_d = _rowwise("s5_gelu_bwd", gelu_bwd_fn, [dy2a, dy2b_, ylin, (u, 0, ws)], [vec('ssm_d')],
                                      [(ws, BF16), (ws, F32)], [ws])

    bblk_t = jnp.swapaxes(bblk, 1, 2)
    cblk_t = jnp.swapaxes(cblk, 1, 2)
    du_p, d_bblk, d_cblk_t, d_ab = _s5_bwd(_perm_rows(dy1b), up, s_all, bblk_t, ab, cblk_t)
    du_ssm = _unperm_rows(du_p)

    dzp, dg_out_pool = _rowwise("norm_pool_bwd", lambda r, v: (lambda dx, dg: ([dx], [_colsum(dg)]))(*_rms_bwd(r[0], r[1], v[0])),
                                [(dmerged, 1, ws), zp], [vec('g_out_pool')], [(ws, F32)], [ws])
    dps, dw_pool, d_pool_scale = _pool_bwd1(dzp, pooled, wpool_full, vec('pool_scale'))
    du_pool = _pool_bwd2(dps, n_pg)

    dub = _rowwise("mix_du", lambda r, v: ([jnp.concatenate([r[0] + r[1], r[2]], -1)], []),
                   [du_ssm, du_skip, du_pool], [], [(d, BF16)])[0]
    dn2 = mm_dd_t("mix_in_bwd", [(dub, 'w_in')], F32)
    grad_b2 = dw_dd("mix_dw_in", n2, dub, 'w_in', grad_b2)
    glu_g = dw_glu.reshape(N_CHIPS, 2, gh, d).astype(BF16)
    pool_g = dw_pool.reshape(n_pg, N_CHIPS, pw // N_CHIPS, pw).transpose(1, 0, 2, 3).reshape(N_CHIPS, 2, ph, d).astype(BF16)
    grad_b2 = lax.dynamic_update_slice(grad_b2, glu_g, (0, 0, 3 * ds_, 0))
    grad_b2 = lax.dynamic_update_slice(grad_b2, pool_g, (0, 0, 3 * ds_ + gh, 0))
    rs_mix, tok = rs_pair_start("mixers", [grad_b2])
    dh1, dh1b, dg_mix = norm_bwd("norm_mix_bwd", dn2, h1, 'g_mix', dh2, deps=[tok])
    rs_mix, tok = rs_scatter_start("mixers", rs_mix, dh1b)

    dy_f1 = _rowwise("ffn1_half", lambda r, v: ([0.5 * r[0]], []), [dh1], [], [(d, BF16)], deps=[tok])[0]
    da1, db1, g_down1 = ffn_down_bwd("ffn1", dy_f1, a1, b1, hm1, gd1)
    rs_d1, tok = rs_pair_start("ffn1_down", [g_down1])
    dn1 = ffn_up_bwd("ffn1", da1, db1, ga1, deps=[tok])
    rs_d1, tok = rs_scatter_start("ffn1_down", rs_d1, dn1)
    grad_x, _, dg_ffn1 = norm_bwd("norm_ffn1_bwd", dn1, x2, 'g_ffn1', dh1, deps=[tok])

    def undiag(t):
        return jnp.einsum('jghkp,gk->jgph', t.reshape(nb, gpb, SSM_GROUP, gpb, n_state), eye).reshape(n_grp, n_state, SSM_GROUP)

    d_bbar_re, d_bbar_im = undiag(d_bblk[:, :, :cb]), undiag(d_bblk[:, :, cb:])
    d_c_re = undiag(d_cblk_t[:, :, :cb]).transpose(0, 2, 1)
    d_c_im = -undiag(d_cblk_t[:, :, cb:]).transpose(0, 2, 1)
    d_abar = jnp.sum(d_ab, axis=2).reshape(nb, 2, gpb, n_state)
    d_abar_re = d_abar[:, 0].reshape(n_grp, n_state)
    d_abar_im = d_abar[:, 1].reshape(n_grp, n_state)
    d_a_re, d_a_im, d_log_dt, d_b_re, d_b_im = disc_vjp((d_abar_re, d_abar_im, d_bbar_re, d_bbar_im))

    small_g = {'g_ffn1': dg_ffn1, 'g_mix': dg_mix, 'ssm_a_re': d_a_re, 'ssm_a_im': d_a_im, 'ssm_log_dt': d_log_dt,
               'ssm_b_re': d_b_re, 'ssm_b_im': d_b_im, 'ssm_c_re': d_c_re, 'ssm_c_im': d_c_im, 'ssm_d': d_ssm_d,
               'b_glu': db_glu, 'pool_scale': d_pool_scale, 'g_out_ssm': dg_out_ssm, 'g_out_pool': dg_out_pool,
               'g_xattn': dg_xattn, 'g_mem': dg_mem, 'g_ffn2': dg_ffn2, 'g_final': dg_final}
    sizes = [wts[n].size for n in SMALL]
    total = sum(sizes) + 128
    rows_s = -(-total // 1024) * 8
    flat = jnp.concatenate([small_g[n].reshape(-1) for n in SMALL] + [loss_row.reshape(-1)])
    flat = jnp.pad(flat, (0, rows_s * 128 - total)).reshape(rows_s, 128)
    red = _small_all_reduce(flat).reshape(-1)
    loss = red[sum(sizes)]

    def flat_small(t):
        return jnp.pad(jnp.concatenate([t[n].reshape(-1) for n in SMALL]), (0, rows_s * 128 - sum(sizes))).reshape(rows_s, 128)
    sg_, sd_, sm_, sv_ = _adamw("adamw_small", flat_small(wts), flat_small(mom), flat_small(var), red.reshape(1, rows_s, 128))
    out = {}
    off = 0
    for n, sz in zip(SMALL, sizes):
        for key, arr in (('grad', sg_), ('delta', sd_), ('m', sm_), ('v', sv_)):
            out[key, n] = arr.reshape(-1)[off:off + sz].reshape(wts[n].shape)
        off += sz

    def upd(n, g_arr, half, row_off, shape2):
        res = _adamw("adamw_" + n, wts[n].reshape(shape2), mom[n].reshape(shape2), var[n].reshape(shape2), g_arr, half, row_off)
        for key, arr in zip(('grad', 'delta', 'm', 'v'), res):
            out[key, n] = arr.reshape(wts[n].shape)
        return res[3]

    g_up1 = ffn_dw_up("ffn1", da1, db1, n1, deps=[sv_])
    rs_u1, tok = rs_pair_start("ffn1_up", [g_up1])
    rs_f2, tok = rs_half_start("ffn2", rs_f2, tok)
    rs_u1, tok = rs_scatter_start("ffn1_up", rs_u1, tok)
    rs_mix, tok = rs_half_start("mixers", rs_mix, tok)
    full_up2, full_down2 = rs_finish("ffn2", rs_f2, tok)
    upd('w2_gate', full_up2, 0, 0, (d, fs))
    upd('w2_up', full_up2, 1, 0, (d, fs))
    last = upd('w2_down', full_down2.reshape(1, fs, d), 0, 0, (fs, d))
    rs_d1, tok = rs_half_start("ffn1_down", rs_d1, last)
    full_b2, = rs_finish("mixers", rs_mix, tok)
    for n, (h, q) in DD.items():
        last = upd(n, full_b2, h, q * ds_, (ds_, d))
    glu_shape, pool_shape = (ws // N_CHIPS, ws), (n_pg * pw // N_CHIPS, pw)
    upd('w_glu', full_b2[:, 3 * ds_:3 * ds_ + gh].reshape((1,) + glu_shape), 0, 0, glu_shape)
    upd('w_pool', full_b2[:, 3 * ds_ + gh:].reshape((1,) + pool_shape), 0, 0, pool_shape)
    full_down1, = rs_finish("ffn1_down", rs_d1, last)
    last = upd('w1_down', full_down1.reshape(1, fs, d), 0, 0, (fs, d))
    rs_u1, tok = rs_half_start("ffn1_up", rs_u1, last)
    full_up1, = rs_finish("ffn1_up", rs_u1, tok)
    upd('w1_gate', full_up1, 0, 0, (d, fs))
    upd('w1_up', full_up1, 1, 0, (d, fs))

    return (loss, grad_x[None], *[out['grad', n] for n in WEIGHTS], *[out['delta', n] for n in WEIGHTS],
            *[out['m', n] for n in WEIGHTS], *[out['v', n] for n in WEIGHTS])
```

```python
import functools
import math

import jax
import jax.numpy as jnp
from jax import lax
from jax.experimental import pallas as pl
from jax.experimental.pallas import tpu as pltpu

F32 = jnp.float32
BF16 = jnp.bfloat16
EPS = 1e-6
ADAM_LR, ADAM_B1, ADAM_B2, ADAM_EPS, ADAM_WD, ADAM_STEP = 0.001, 0.9, 0.999, 1e-08, 0.01, 10
POOL_WINDOWS = (2, 4, 8, 16)
SSM_GROUP = 16
S5_GROUPS_PER_BLOCK = 16
S5_LANES = 8
MEM_HEADS = 4
N_CHIPS = 4
VMEM_LIMIT_V7X = 56 * 1024 * 1024
MESH = pl.DeviceIdType.MESH

WEIGHTS = ['g_ffn1', 'w1_gate', 'w1_up', 'w1_down', 'g_mix', 'w_in', 'ssm_a_re', 'ssm_a_im', 'ssm_log_dt',
           'ssm_b_re', 'ssm_b_im', 'ssm_c_re', 'ssm_c_im', 'ssm_d', 'w_glu', 'b_glu', 'w_pool', 'pool_scale',
           'g_out_ssm', 'g_out_pool', 'w_out', 'g_xattn', 'g_mem', 'w_q', 'w_k', 'w_v', 'w_o', 'g_ffn2',
           'w2_gate', 'w2_up', 'w2_down', 'g_final']
BIG = ['w1_gate', 'w1_up', 'w1_down', 'w_in', 'w_glu', 'w_pool', 'w_out', 'w_q', 'w_k', 'w_v', 'w_o',
       'w2_gate', 'w2_up', 'w2_down']
SMALL = [n for n in WEIGHTS if n not in BIG]


def _tile(n, target, mult=8):
    best = None
    for d in range(1, n + 1):
        if n % d == 0 and d <= target and d % mult == 0:
            best = d
    return best if best is not None else n


def _params(sem=None):
    if sem is None:
        return pltpu.CompilerParams(vmem_limit_bytes=VMEM_LIMIT_V7X)
    return pltpu.CompilerParams(dimension_semantics=sem, vmem_limit_bytes=VMEM_LIMIT_V7X)


def _sigmoid(x):
    return 1.0 / (1.0 + jnp.exp(-x))


def _rms_fwd(x, g):
    r = lax.rsqrt(jnp.mean(x * x, axis=-1, keepdims=True) + EPS)
    return x * r * g


def _rms_bwd(dy, x, g):
    r = lax.rsqrt(jnp.mean(x * x, axis=-1, keepdims=True) + EPS)
    dxh = dy * g
    dx = r * dxh - x * (r * r * r) * jnp.mean(dxh * x, axis=-1, keepdims=True)
    return dx, dy * x * r


def _colsum(v):
    return jnp.sum(v, axis=0, keepdims=True)


def _rowwise(name, fn, rows, vecs, out_defs, red_defs=(), tm=256, deps=()):
    rows = [r if isinstance(r, tuple) else (r, 0, r.shape[1]) for r in rows]
    t_rows = rows[0][0].shape[0]
    tm = _tile(t_rows, tm)
    nr, nv, no, nd, nx = len(rows), len(vecs), len(out_defs), len(red_defs), len(deps)

    def body(*refs):
        r, v = refs[:nr], refs[nr:nr + nv]
        o, d = refs[nr + nv + nx:nr + nv + nx + no], refs[nr + nv + nx + no:]
        outs, reds = fn([x[...] for x in r], [x[...] for x in v])
        for ref, val in zip(o, outs):
            ref[...] = val.astype(ref.dtype)
        if nd:
            @pl.when(pl.program_id(0) == 0)
            def _():
                for ref in d:
                    ref[...] = jnp.zeros(ref.shape, ref.dtype)
            for ref, val in zip(d, reds):
                ref[...] += val

    in_specs = [pl.BlockSpec((tm, w), functools.partial(lambda i, cb: (i, cb), cb=cb)) for (_, cb, w) in rows]
    in_specs += [pl.BlockSpec(v.shape, lambda i: (0, 0)) for v in vecs]
    in_specs += [pl.BlockSpec(memory_space=pl.ANY)] * nx
    out_specs = [pl.BlockSpec((tm, w), lambda i: (i, 0)) for (w, _) in out_defs]
    out_specs += [pl.BlockSpec((1, w), lambda i: (0, 0)) for w in red_defs]
    out_shape = [jax.ShapeDtypeStruct((t_rows, w), dt) for (w, dt) in out_defs]
    out_shape += [jax.ShapeDtypeStruct((1, w), F32) for w in red_defs]
    res = pl.pallas_call(
        body, name=name, grid=(t_rows // tm,), in_specs=in_specs, out_specs=out_specs, out_shape=out_shape,
        compiler_params=_params(("arbitrary",)),
    )(*[r[0] for r in rows], *vecs, *deps)
    return res


def _rmsnorm(name, x, g, tm=256, deps=()):
    return _rowwise(name, lambda r, v: ([_rms_fwd(r[0].astype(F32), v[0])], []), [x], [g],
                    [(x.shape[1], BF16)], tm=tm, deps=deps)[0]


def _mm_nn(name, a, b, b_block, b_idx, nk, n_out, out_dtype, res=None, alpha=1.0, norm_g=None, tm=512):
    t_rows = a.shape[0]
    bk, tn = b_block[-2], b_block[-1]
    tm = _tile(t_rows, tm)
    nj = n_out // tn
    has_res = res is not None
    has_norm = norm_g is not None
    assert not has_norm or nj == 1

    def body(*refs):
        a_ref, b_ref = refs[0], refs[1]
        res_ref = refs[2] if has_res else None
        g_ref = refs[2 + has_res] if has_norm else None
        o_ref = refs[2 + has_res + has_norm]
        n_ref = refs[3 + has_res + has_norm] if has_norm else None
        acc_ref = refs[3 + has_res + 2 * has_norm]
        k = pl.program_id(2)
        p = jnp.dot(a_ref[...], b_ref[...], preferred_element_type=F32)

        @pl.when(k == 0)
        def _():
            acc_ref[...] = p

        @pl.when(k > 0)
        def _():
            acc_ref[...] += p

        @pl.when(k == nk - 1)
        def _():
            r = acc_ref[...]
            if has_res:
                r = res_ref[...] + alpha * r
            o_ref[...] = r.astype(o_ref.dtype)
            if has_norm:
                n_ref[...] = _rms_fwd(r, g_ref[...]).astype(n_ref.dtype)

    in_specs = [pl.BlockSpec((tm, bk), lambda j, i, k: (i, k)),
                pl.BlockSpec(b_block, lambda j, i, k: b_idx(j, k))]
    args = [a, b]
    if has_res:
        in_specs.append(pl.BlockSpec((tm, tn), lambda j, i, k: (i, j)))
        args.append(res)
    tile = pl.BlockSpec((tm, tn), lambda j, i, k: (i, j))
    out_specs, out_shape = tile, jax.ShapeDtypeStruct((t_rows, n_out), out_dtype)
    if has_norm:
        in_specs.append(pl.BlockSpec((1, n_out), lambda j, i, k: (0, 0)))
        args.append(norm_g)
        out_specs, out_shape = [tile, tile], [out_shape, jax.ShapeDtypeStruct((t_rows, n_out), BF16)]
    return pl.pallas_call(
        body, name=name, grid=(nj, t_rows // tm, nk), in_specs=in_specs, out_specs=out_specs, out_shape=out_shape,
        scratch_shapes=[pltpu.VMEM((tm, tn), F32)],
        compiler_params=_params(("arbitrary", "arbitrary", "arbitrary")),
    )(*args)


def _dot_nt(x, w):
    return lax.dot_general(x, w, (((1,), (1,)), ((), ())), preferred_element_type=F32)


def _dot_tn(x, y):
    return lax.dot_general(x, y, (((0,), (0,)), ((), ())), preferred_element_type=F32)


def _mm_nt_cols(name, pairs, ns, bn, out_defs, epi=None, extras=(), tm=512, deps=()):
    t_rows = pairs[0][0].shape[0]
    tm = _tile(t_rows, tm)
    npair, nex, no, nx = len(pairs), len(extras), len(out_defs), len(deps)

    def body(*refs):
        acc = None
        for p in range(npair):
            part = _dot_nt(refs[2 * p][...], refs[2 * p + 1][...])
            acc = part if acc is None else acc + part
        ex = [r[...] for r in refs[2 * npair:2 * npair + nex]]
        outs = epi(acc, *ex) if epi is not None else (acc,)
        for ref, val in zip(refs[2 * npair + nex + nx:], outs):
            ref[...] = val.astype(ref.dtype)

    in_specs, args = [], []
    for (dy, w, w_block, w_idx) in pairs:
        in_specs.append(pl.BlockSpec((tm, dy.shape[1]), lambda s, i: (i, 0)))
        in_specs.append(pl.BlockSpec(w_block, functools.partial(lambda s, i, f: f(s), f=w_idx)))
        args += [dy, w]
    for e in extras:
        in_specs.append(pl.BlockSpec((tm, bn), lambda s, i: (i, s)))
        args.append(e)
    in_specs += [pl.BlockSpec(memory_space=pl.ANY)] * nx
    args += list(deps)
    res = pl.pallas_call(
        body, name=name, grid=(ns, t_rows // tm), in_specs=in_specs,
        out_specs=[pl.BlockSpec((tm, bn), lambda s, i: (i, s)) for _ in range(no)],
        out_shape=[jax.ShapeDtypeStruct((t_rows, ns * bn), dt) for dt in out_defs],
        compiler_params=_params(("arbitrary", "arbitrary")),
    )(*args)
    return res


def _mm_nt_k(name, pairs, ns, n_out, out_dtype, tm=512, deps=()):
    t_rows = pairs[0][0].shape[0]
    tm = _tile(t_rows, tm)
    npair, nx = len(pairs), len(deps)

    def body(*refs):
        o_ref, acc_ref = refs[2 * npair + nx], refs[2 * npair + nx + 1]
        s = pl.program_id(1)
        acc = None
        for p in range(npair):
            part = _dot_nt(refs[2 * p][...], refs[2 * p + 1][...])
            acc = part if acc is None else acc + part

        @pl.when(s == 0)
        def _():
            acc_ref[...] = acc

        @pl.when(s > 0)
        def _():
            acc_ref[...] += acc

        @pl.when(s == ns - 1)
        def _():
            o_ref[...] = acc_ref[...].astype(o_ref.dtype)

    in_specs, args = [], []
    for (a, w, w_block, w_idx) in pairs:
        in_specs.append(pl.BlockSpec((tm, w_block[-1]), lambda i, s: (i, s)))
        in_specs.append(pl.BlockSpec(w_block, functools.partial(lambda i, s, f: f(s), f=w_idx)))
        args += [a, w]
    in_specs += [pl.BlockSpec(memory_space=pl.ANY)] * nx
    args += list(deps)
    return pl.pallas_call(
        body, name=name, grid=(t_rows // tm, ns), in_specs=in_specs,
        out_specs=pl.BlockSpec((tm, n_out), lambda i, s: (i, 0)),
        out_shape=jax.ShapeDtypeStruct((t_rows, n_out), out_dtype),
        scratch_shapes=[pltpu.VMEM((tm, n_out), F32)],
        compiler_params=_params(("arbitrary", "arbitrary")),
    )(*args)


def _mm_tn(name, a, b, bk, bn, n_p, n_q, out_shape, out_block, out_idx, into=None, a_off=0, b_off=0, tt=512,
           deps=()):
    t_rows = a.shape[0]
    tt = _tile(t_rows, tt, 16)
    nt = t_rows // tt
    has_into = into is not None
    nx = len(deps)

    def body(*refs):
        a_ref, b_ref = refs[0], refs[1]
        o_ref, acc_ref = refs[2 + has_into + nx], refs[3 + has_into + nx]
        t = pl.program_id(2)
        part = _dot_tn(a_ref[...], b_ref[...])

        @pl.when(t == 0)
        def _():
            acc_ref[...] = part

        @pl.when(t > 0)
        def _():
            acc_ref[...] += part

        @pl.when(t == nt - 1)
        def _():
            o_ref[...] = acc_ref[...].astype(o_ref.dtype)

    in_specs = [pl.BlockSpec((tt, bk), lambda p, q, t: (t, p + a_off)),
                pl.BlockSpec((tt, bn), lambda p, q, t: (t, q + b_off))]
    args = [a, b]
    aliases = {}
    if has_into:
        in_specs.append(pl.BlockSpec(memory_space=pl.ANY))
        args.append(into)
        aliases = {2: 0}
        out_shape = jax.ShapeDtypeStruct(into.shape, into.dtype)
    in_specs += [pl.BlockSpec(memory_space=pl.ANY)] * nx
    args += list(deps)
    return pl.pallas_call(
        body, name=name, grid=(n_p, n_q, nt), in_specs=in_specs,
        out_specs=pl.BlockSpec(out_block, lambda p, q, t: out_idx(p, q)),
        out_shape=out_shape, scratch_shapes=[pltpu.VMEM((bk, bn), F32)],
        input_output_aliases=aliases,
        compiler_params=_params(("arbitrary", "arbitrary", "arbitrary")),
    )(*args)


def _ffn_up(name, n, ga, d_model, fs, tm=512):
    t_rows = n.shape[0]
    tm = _tile(t_rows, tm)

    def body(n_ref, wg_ref, wu_ref, a_ref, b_ref, h_ref):
        x = n_ref[...]
        a = jnp.dot(x, wg_ref[...], preferred_element_type=F32)
        b = jnp.dot(x, wu_ref[...], preferred_element_type=F32)
        a_ref[...] = a.astype(a_ref.dtype)
        b_ref[...] = b.astype(b_ref.dtype)
        h_ref[...] = (a * _sigmoid(a) * b).astype(h_ref.dtype)

    w_block = (None, None, d_model, fs)
    out = jax.ShapeDtypeStruct((t_rows, N_CHIPS * fs), BF16)
    return pl.pallas_call(
        body, name=name, grid=(N_CHIPS, t_rows // tm),
        in_specs=[pl.BlockSpec((tm, d_model), lambda s, i: (i, 0)),
                  pl.BlockSpec(w_block, lambda s, i: (s, 0, 0, 0)),
                  pl.BlockSpec(w_block, lambda s, i: (s, 1, 0, 0))],
        out_specs=[pl.BlockSpec((tm, fs), lambda s, i: (i, s))] * 3,
        out_shape=[out, out, out],
        compiler_params=_params(("arbitrary", "arbitrary")),
    )(n, ga, ga)


def _swiglu_bwd(dh, a, b):
    a = a.astype(F32)
    b = b.astype(F32)
    sg = _sigmoid(a)
    return dh * b * sg * (1.0 + a * (1.0 - sg)), dh * a * sg


def _attn_fwd(q, k, v, tm=512):
    t_rows, d_model = q.shape
    n_mem = k.shape[0]
    hd = d_model // MEM_HEADS
    scale = hd ** -0.5
    tm = _tile(t_rows, tm)

    def body(q_ref, k_ref, v_ref, o_ref):
        for h in range(MEM_HEADS):
            cols = slice(h * hd, (h + 1) * hd)
            s = _dot_nt(q_ref[:, cols], k_ref[:, cols]) * scale
            s = s - jnp.max(s, axis=-1, keepdims=True)
            e = jnp.exp(s)
            p = e / jnp.sum(e, axis=-1, keepdims=True)
            o_ref[:, cols] = jnp.dot(p.astype(BF16), v_ref[:, cols], preferred_element_type=F32).astype(o_ref.dtype)

    return pl.pallas_call(
        body, name="attn_fwd", grid=(t_rows // tm,),
        in_specs=[pl.BlockSpec((tm, d_model), lambda i: (i, 0)),
                  pl.BlockSpec((n_mem, d_model), lambda i: (0, 0)),
                  pl.BlockSpec((n_mem, d_model), lambda i: (0, 0))],
        out_specs=pl.BlockSpec((tm, d_model), lambda i: (i, 0)),
        out_shape=jax.ShapeDtypeStruct((t_rows, d_model), BF16),
        compiler_params=_params(("arbitrary",)),
    )(q, k, v)


def _attn_bwd(q, k, v, do, tm=512):
    t_rows, d_model = q.shape
    n_mem = k.shape[0]
    hd = d_model // MEM_HEADS
    scale = hd ** -0.5
    tm = _tile(t_rows, tm, 16)

    def body(q_ref, k_ref, v_ref, do_ref, dq_ref, dk_ref, dv_ref):
        @pl.when(pl.program_id(0) == 0)
        def _():
            dk_ref[...] = jnp.zeros(dk_ref.shape, F32)
            dv_ref[...] = jnp.zeros(dv_ref.shape, F32)

        for h in range(MEM_HEADS):
            cols = slice(h * hd, (h + 1) * hd)
            qh, kh, vh, doh = q_ref[:, cols], k_ref[:, cols], v_ref[:, cols], do_ref[:, cols]
            s = _dot_nt(qh, kh) * scale
            s = s - jnp.max(s, axis=-1, keepdims=True)
            e = jnp.exp(s)
            p = e / jnp.sum(e, axis=-1, keepdims=True)
            dv_ref[:, cols] += _dot_tn(p.astype(BF16), doh)
            dp = _dot_nt(doh, vh)
            ds = (p * (dp - jnp.sum(dp * p, axis=-1, keepdims=True)) * scale).astype(BF16)
            dq_ref[:, cols] = jnp.dot(ds, kh, preferred_element_type=F32).astype(dq_ref.dtype)
            dk_ref[:, cols] += _dot_tn(ds, qh)

    full = pl.BlockSpec((n_mem, d_model), lambda i: (0, 0))
    tile = pl.BlockSpec((tm, d_model), lambda i: (i, 0))
    return pl.pallas_call(
        body, name="attn_bwd", grid=(t_rows // tm,),
        in_specs=[tile, full, full, tile], out_specs=[tile, full, full],
        out_shape=[jax.ShapeDtypeStruct((t_rows, d_model), BF16),
                   jax.ShapeDtypeStruct((n_mem, d_model), F32), jax.ShapeDtypeStruct((n_mem, d_model), F32)],
        compiler_params=_params(("arbitrary",)),
    )(q, k, v, do)


def _split_bf16(v):
    hi = v.astype(BF16)
    return hi, (v - hi.astype(F32)).astype(BF16)


def _pool_window(g):
    return jnp.left_shift(jnp.int32(POOL_WINDOWS[0]), g)


def _pool_fwd(u, col_off, w_pool, scale, tt=256):
    t_rows = u.shape[0]
    pw = w_pool.shape[-1]
    ng = w_pool.shape[0]
    tt = _tile(t_rows, tt, 16)
    nt = t_rows // tt
    assert POOL_WINDOWS == tuple(2 << i for i in range(ng)) and tt >= POOL_WINDOWS[-1]

    def body(vc_ref, vp_ref, w_ref, sc_ref, pooled_ref, z_ref):
        g, i = pl.program_id(0), pl.program_id(1)
        w = _pool_window(g)
        r = lax.broadcasted_iota(jnp.int32, (tt, tt), 0)
        c = lax.broadcasted_iota(jnp.int32, (tt, tt), 1)
        band_c = ((c <= r) & (c > r - w)).astype(BF16)
        band_p = (c > r - w + tt).astype(BF16)
        vc = vc_ref[...]
        ch, cl = _split_bf16(vc)
        ph, plo = _split_bf16(vp_ref[...] * (i > 0).astype(F32))
        sums = (jnp.dot(band_c, ch, preferred_element_type=F32) + jnp.dot(band_c, cl, preferred_element_type=F32)
                + jnp.dot(band_p, ph, preferred_element_type=F32) + jnp.dot(band_p, plo, preferred_element_type=F32))
        t = i * tt + lax.broadcasted_iota(jnp.int32, (tt, 1), 0)
        cnt = jnp.minimum(t + 1, w).astype(F32)
        pooled = (sums / cnt - vc).astype(BF16)
        pooled_ref[...] = pooled
        z_ref[...] = jnp.dot(pooled, w_ref[...], preferred_element_type=F32) * sc_ref[...]

    return pl.pallas_call(
        body, name="pool_fwd", grid=(ng, nt),
        in_specs=[pl.BlockSpec((tt, pw), lambda g, i: (i, col_off + g)),
                  pl.BlockSpec((tt, pw), lambda g, i: (jnp.maximum(i - 1, 0), col_off + g)),
                  pl.BlockSpec((None, pw, pw), lambda g, i: (g, 0, 0)),
                  pl.BlockSpec((1, pw), lambda g, i: (0, g))],
        out_specs=[pl.BlockSpec((tt, pw), lambda g, i: (i, g))] * 2,
        out_shape=[jax.ShapeDtypeStruct((t_rows, ng * pw), BF16), jax.ShapeDtypeStruct((t_rows, ng * pw), F32)],
        compiler_params=_params(("arbitrary", "arbitrary")),
    )(u, u, w_pool, scale)


def _pool_bwd1(dz, pooled, w_pool, scale, tt=256):
    t_rows = dz.shape[0]
    pw = w_pool.shape[-1]
    ng = w_pool.shape[0]
    tt = _tile(t_rows, tt, 16)
    nt = t_rows // tt

    def body(dz_ref, p_ref, w_ref, sc_ref, dp_ref, dw_ref, dsc_ref):
        g, i = pl.program_id(0), pl.program_id(1)
        w = _pool_window(g)

        @pl.when(i == 0)
        def _():
            dw_ref[...] = jnp.zeros(dw_ref.shape, F32)
            dsc_ref[...] = jnp.zeros(dsc_ref.shape, F32)

        dz_v = dz_ref[...]
        pooled = p_ref[...]
        zpre = jnp.dot(pooled, w_ref[...], preferred_element_type=F32)
        dsc_ref[...] += _colsum(dz_v * zpre)
        dzs = (dz_v * sc_ref[...]).astype(BF16)
        dw_ref[...] += _dot_tn(pooled, dzs)
        t = i * tt + lax.broadcasted_iota(jnp.int32, (tt, 1), 0)
        cnt = jnp.minimum(t + 1, w).astype(F32)
        dp_ref[...] = _dot_nt(dzs, w_ref[...]) / cnt

    return pl.pallas_call(
        body, name="pool_bwd1", grid=(ng, nt),
        in_specs=[pl.BlockSpec((tt, pw), lambda g, i: (i, g)),
                  pl.BlockSpec((tt, pw), lambda g, i: (i, g)),
                  pl.BlockSpec((None, pw, pw), lambda g, i: (g, 0, 0)),
                  pl.BlockSpec((1, pw), lambda g, i: (0, g))],
        out_specs=[pl.BlockSpec((tt, pw), lambda g, i: (i, g)),
                   pl.BlockSpec((None, pw, pw), lambda g, i: (g, 0, 0)),
                   pl.BlockSpec((1, pw), lambda g, i: (0, g))],
        out_shape=[jax.ShapeDtypeStruct((t_rows, ng * pw), F32), jax.ShapeDtypeStruct((ng, pw, pw), F32),
                   jax.ShapeDtypeStruct((1, ng * pw), F32)],
        compiler_params=_params(("arbitrary", "arbitrary")),
    )(dz, pooled, w_pool, scale)


def _pool_bwd2(dps, ng, tt=256):
    t_rows, width = dps.shape
    pw = width // ng
    tt = _tile(t_rows, tt, 16)
    nt = t_rows // tt

    def body(dc_ref, dn_ref, dv_ref):
        g, i = pl.program_id(0), pl.program_id(1)
        w = _pool_window(g)
        r = lax.broadcasted_iota(jnp.int32, (tt, tt), 0)
        c = lax.broadcasted_iota(jnp.int32, (tt, tt), 1)
        band_c = ((c >= r) & (c < r + w)).astype(BF16)
        band_n = (c < r + w - tt).astype(BF16)
        dc = dc_ref[...]
        ch, cl = _split_bf16(dc)
        nh, nl = _split_bf16(dn_ref[...] * (i < nt - 1).astype(F32))
        sums = (jnp.dot(band_c, ch, preferred_element_type=F32) + jnp.dot(band_c, cl, preferred_element_type=F32)
                + jnp.dot(band_n, nh, preferred_element_type=F32) + jnp.dot(band_n, nl, preferred_element_type=F32))
        t = i * tt + lax.broadcasted_iota(jnp.int32, (tt, 1), 0)
        cnt = jnp.minimum(t + 1, w).astype(F32)
        dv_ref[...] = sums - dc * cnt

    return pl.pallas_call(
        body, name="pool_bwd2", grid=(ng, nt),
        in_specs=[pl.BlockSpec((tt, pw), lambda g, i: (i, g)),
                  pl.BlockSpec((tt, pw), lambda g, i: (jnp.minimum(i + 1, nt - 1), g))],
        out_specs=pl.BlockSpec((tt, pw), lambda g, i: (i, g)),
        out_shape=jax.ShapeDtypeStruct((t_rows, width), F32),
        compiler_params=_params(("arbitrary", "arbitrary")),
    )(dps, dps)


def _cpow(ar, ai, n):
    rr, ri, br, bi = None, None, ar, ai
    while n:
        if n & 1:
            rr, ri = (br, bi) if rr is None else (rr * br - ri * bi, rr * bi + ri * br)
        n >>= 1
        if n:
            br, bi = br * br - bi * bi, 2.0 * br * bi
    return rr, ri


def _chunk_carries(st_re, st_im, pr, pi, order):
    cb = st_re.shape[1]
    sub = lax.broadcasted_iota(jnp.int32, (S5_LANES, cb), 0)
    cr = jnp.zeros((S5_LANES, cb), F32)
    ci = jnp.zeros((S5_LANES, cb), F32)
    prev_r = jnp.zeros((1, cb), F32)
    prev_i = jnp.zeros((1, cb), F32)
    for k, src in order:
        er, ei = st_re[src:src + 1, :], st_im[src:src + 1, :]
        nr = er + pr * prev_r - pi * prev_i
        ni = ei + pr * prev_i + pi * prev_r
        cr = jnp.where(sub == k, jnp.broadcast_to(nr, (S5_LANES, cb)), cr)
        ci = jnp.where(sub == k, jnp.broadcast_to(ni, (S5_LANES, cb)), ci)
        prev_r, prev_i = nr, ni
    return cr, ci


def _s5_fwd(up, bblk, ab, cblk, tt=128):
    n_rows, ws = up.shape
    nb, cw, cb2 = bblk.shape
    cb = cb2 // 2
    lc = n_rows // S5_LANES
    tt = _tile(lc, tt, 1)
    nt = lc // tt
    rt = S5_LANES * tt

    def body(u_ref, b_ref, ab_ref, c_ref, y_ref, s_ref, bu_ref, st_re, st_im):
        ps, ti = pl.program_id(1), pl.program_id(2)
        ar = jnp.broadcast_to(ab_ref[0:1, :], (S5_LANES, cb))
        ai = jnp.broadcast_to(ab_ref[1:2, :], (S5_LANES, cb))

        @pl.when((ps == 0) & (ti == 0))
        def _():
            st_re[...] = jnp.zeros(st_re.shape, F32)
            st_im[...] = jnp.zeros(st_im.shape, F32)

        @pl.when((ps == 1) & (ti == 0))
        def _():
            pr, pi = _cpow(ab_ref[0:1, :], ab_ref[1:2, :], lc)
            cr, ci = _chunk_carries(st_re, st_im, pr, pi, [(k, k - 1) for k in range(1, S5_LANES)])
            st_re[...] = cr
            st_im[...] = ci

        bu_ref[...] = jnp.dot(u_ref[...], b_ref[...], preferred_element_type=F32)

        def step(t, carry, store):
            sr, si = carry
            rows = pl.ds(pl.multiple_of(t * S5_LANES, S5_LANES), S5_LANES)
            nr = ar * sr - ai * si + bu_ref[rows, 0:cb]
            ni = ar * si + ai * sr + bu_ref[rows, cb:cb2]
            if store:
                s_ref[rows, 0:cb] = nr
                s_ref[rows, cb:cb2] = ni
            return nr, ni

        @pl.when(ps == 0)
        def _():
            sr, si = lax.fori_loop(0, tt, functools.partial(step, store=False), (st_re[...], st_im[...]))
            st_re[...] = sr
            st_im[...] = si

        @pl.when(ps == 1)
        def _():
            sr, si = lax.fori_loop(0, tt, functools.partial(step, store=True), (st_re[...], st_im[...]))
            st_re[...] = sr
            st_im[...] = si
            y_ref[...] = jnp.dot(s_ref[...].astype(BF16), c_ref[...], preferred_element_type=F32)

    return pl.pallas_call(
        body, name="s5_fwd", grid=(nb, 2, nt),
        in_specs=[pl.BlockSpec((rt, cw), lambda j, ps, ti: (ti, j)),
                  pl.BlockSpec((None, cw, cb2), lambda j, ps, ti: (j, 0, 0)),
                  pl.BlockSpec((None, 2, cb), lambda j, ps, ti: (j, 0, 0)),
                  pl.BlockSpec((None, cb2, cw), lambda j, ps, ti: (j, 0, 0))],
        out_specs=[pl.BlockSpec((rt, cw), lambda j, ps, ti: (ti * ps, j)),
                   pl.BlockSpec((None, rt, cb2), lambda j, ps, ti: (j, ti * ps, 0))],
        out_shape=[jax.ShapeDtypeStruct((n_rows, ws), F32), jax.ShapeDtypeStruct((nb, n_rows, cb2), F32)],
        scratch_shapes=[pltpu.VMEM((rt, cb2), F32), pltpu.VMEM((S5_LANES, cb), F32), pltpu.VMEM((S5_LANES, cb), F32)],
        compiler_params=_params(("arbitrary", "arbitrary", "arbitrary")),
    )(up, bblk, ab, cblk)


def _s5_bwd(dyp, up, s_all, bblk_t, ab, cblk_t, tt=128):
    n_rows, ws = up.shape
    nb, cb2, cw = bblk_t.shape
    cb = cb2 // 2
    lc = n_rows // S5_LANES
    tt = _tile(lc, tt, 1)
    nt = lc // tt
    rt = S5_LANES * tt

    def body(dy_ref, u_ref, s_ref, bt_ref, ab_ref, ct_ref, du_ref, db_ref, dc_ref, da_ref, ds_ref, st_re, st_im):
        ps, ti = pl.program_id(1), pl.program_id(2)
        ar = jnp.broadcast_to(ab_ref[0:1, :], (S5_LANES, cb))
        ai = jnp.broadcast_to(ab_ref[1:2, :], (S5_LANES, cb))

        @pl.when((ps == 0) & (ti == 0))
        def _():
            st_re[...] = jnp.zeros(st_re.shape, F32)
            st_im[...] = jnp.zeros(st_im.shape, F32)
            db_ref[...] = jnp.zeros(db_ref.shape, F32)
            dc_ref[...] = jnp.zeros(dc_ref.shape, F32)
            da_ref[...] = jnp.zeros(da_ref.shape, F32)

        @pl.when((ps == 1) & (ti == 0))
        def _():
            pr, pi = _cpow(ab_ref[0:1, :], -ab_ref[1:2, :], lc)
            cr, ci = _chunk_carries(st_re, st_im, pr, pi, [(k, k + 1) for k in range(S5_LANES - 2, -1, -1)])
            st_re[...] = cr
            st_im[...] = ci

        ds_ref[...] = jnp.dot(dy_ref[...], ct_ref[...], preferred_element_type=F32)

        def rows_of(i):
            return pl.ds(pl.multiple_of((tt - 1 - i) * S5_LANES, S5_LANES), S5_LANES)

        def step0(i, carry):
            gr, gi = carry
            rows = rows_of(i)
            return (ar * gr + ai * gi + ds_ref[rows, 0:cb], ar * gi - ai * gr + ds_ref[rows, cb:cb2])

        def step1(i, carry):
            gr, gi, acr, aci = carry
            rows = rows_of(i)
            sr, si = s_ref[rows, 0:cb], s_ref[rows, cb:cb2]
            acr = acr + sr * gr + si * gi
            aci = aci + sr * gi - si * gr
            nr = ar * gr + ai * gi + ds_ref[rows, 0:cb]
            ni = ar * gi - ai * gr + ds_ref[rows, cb:cb2]
            ds_ref[rows, 0:cb] = nr
            ds_ref[rows, cb:cb2] = ni
            return nr, ni, acr, aci

        @pl.when(ps == 0)
        def _():
            gr, gi = lax.fori_loop(0, tt, step0, (st_re[...], st_im[...]))
            st_re[...] = gr
            st_im[...] = gi

        @pl.when(ps == 1)
        def _():
            zero = jnp.zeros((S5_LANES, cb), F32)
            gr, gi, acr, aci = lax.fori_loop(0, tt, step1, (st_re[...], st_im[...], zero, zero))
            st_re[...] = gr
            st_im[...] = gi
            da_ref[0] += acr
            da_ref[1] += aci
            dsb = ds_ref[...].astype(BF16)
            du_ref[...] = jnp.dot(dsb, bt_ref[...], preferred_element_type=F32)
            db_ref[...] += _dot_tn(u_ref[...], dsb)
            dc_ref[...] += _dot_tn(dy_ref[...], s_ref[...].astype(BF16))

    def tile_idx(ps, ti):
        return (nt - 1 - ti) * ps + (nt - 1) * (1 - ps)

    return pl.pallas_call(
        body, name="s5_bwd", grid=(nb, 2, nt),
        in_specs=[pl.BlockSpec((rt, cw), lambda j, ps, ti: (nt - 1 - ti, j)),
                  pl.BlockSpec((rt, cw), lambda j, ps, ti: (tile_idx(ps, ti), j)),
                  pl.BlockSpec((None, rt, cb2), lambda j, ps, ti: (j, tile_idx(ps, ti), 0)),
                  pl.BlockSpec((None, cb2, cw), lambda j, ps, ti: (j, 0, 0)),
                  pl.BlockSpec((None, 2, cb), lambda j, ps, ti: (j, 0, 0)),
                  pl.BlockSpec((None, cw, cb2), lambda j, ps, ti: (j, 0, 0))],
        out_specs=[pl.BlockSpec((rt, cw), lambda j, ps, ti: (tile_idx(ps, ti), j)),
                   pl.BlockSpec((None, cw, cb2), lambda j, ps, ti: (j, 0, 0)),
                   pl.BlockSpec((None, cw, cb2), lambda j, ps, ti: (j, 0, 0)),
                   pl.BlockSpec((None, 2, S5_LANES, cb), lambda j, ps, ti: (j, 0, 0, 0))],
        out_shape=[jax.ShapeDtypeStruct((n_rows, ws), F32), jax.ShapeDtypeStruct((nb, cw, cb2), F32),
                   jax.ShapeDtypeStruct((nb, cw, cb2), F32), jax.ShapeDtypeStruct((nb, 2, S5_LANES, cb), F32)],
        scratch_shapes=[pltpu.VMEM((rt, cb2), F32), pltpu.VMEM((S5_LANES, cb), F32), pltpu.VMEM((S5_LANES, cb), F32)],
        compiler_params=_params(("arbitrary", "arbitrary", "arbitrary")),
    )(dyp, up, s_all, bblk_t, ab, cblk_t)


def _s5_discretize(a_re, a_im, log_dt, b_re, b_im):
    dt = jnp.exp(log_dt)[:, None]
    mag = jnp.exp(a_re * dt)
    abar_re = mag * jnp.cos(a_im * dt)
    abar_im = mag * jnp.sin(a_im * dt)
    nr, ni = abar_re - 1.0, abar_im
    den = a_re * a_re + a_im * a_im
    fr = (nr * a_re + ni * a_im) / den
    fi = (ni * a_re - nr * a_im) / den
    bbar_re = fr[..., None] * b_re - fi[..., None] * b_im
    bbar_im = fr[..., None] * b_im + fi[..., None] * b_re
    return abar_re, abar_im, bbar_re, bbar_im


def _perm_rows(a):
    n, c = a.shape
    return a.reshape(S5_LANES, n // S5_LANES, c).transpose(1, 0, 2).reshape(n, c)


def _unperm_rows(a):
    n, c = a.shape
    return a.reshape(n // S5_LANES, S5_LANES, c).transpose(1, 0, 2).reshape(n, c)


HBM = pl.BlockSpec(memory_space=pltpu.HBM)
SEM = pl.BlockSpec(memory_space=pltpu.SEMAPHORE)
ANY = pl.BlockSpec(memory_space=pl.ANY)
EFFECT = pltpu.SideEffectType.DATAFLOW_SIDE_EFFECTING
COPIES_PER_BUFFER = {"ag_ici": 3, "ag_fwd": 3, "pair": N_CHIPS, "scatter": 3, "half": 1}
PAIRED_KINDS = ("pair", "scatter")


def _place():
    x, y, c = lax.axis_index("x"), lax.axis_index("y"), lax.axis_index("c")
    chips = [(1 - x, y), (x, 1 - y), (1 - x, 1 - y)]
    return x, y, c, 2 * x + y, chips


def _n_copies(kind, n_bufs):
    return COPIES_PER_BUFFER[kind] * (n_bufs // 2 if kind in PAIRED_KINDS else n_bufs)


def _comm_copies(kind, bufs):
    x, y, c, s, chips = _place()
    sib = (x, y, 1 - c)
    out = []
    if kind == "ag_ici":
        for w in bufs:
            for cx, cy in chips:
                out.append((w.at[s, c], w.at[s, c], w.at[2 * cx + cy, c], (cx, cy, c)))
    elif kind == "ag_fwd":
        for w in bufs:
            for cx, cy in chips:
                sj = 2 * cx + cy
                out.append((w.at[sj, c], w.at[sj, c], w.at[sj, 1 - c], sib))
    elif kind == "pair":
        n = len(bufs) // 2
        for g, got in zip(bufs[:n], bufs[n:]):
            for t in range(N_CHIPS):
                out.append((g.at[t, 1 - c], got.at[t], got.at[t], sib))
    elif kind == "scatter":
        n = len(bufs) // 2
        for p, got in zip(bufs[:n], bufs[n:]):
            for cx, cy in chips:
                out.append((p.at[2 * cx + cy], got.at[s], got.at[2 * cx + cy], (cx, cy, c)))
    elif kind == "half":
        for f in bufs:
            out.append((f.at[c], f.at[c], f.at[1 - c], sib))
    return out


def _comm_fused(name, kind, bufs):
    n = len(bufs)
    ncp = _n_copies(kind, n)

    def body(*refs):
        outs = refs[n:2 * n]
        send, recv = refs[2 * n:]
        copies = _comm_copies(kind, outs)
        started = []
        for k, (src, dst, _, peer) in enumerate(copies):
            cp = pltpu.make_async_remote_copy(src_ref=src, dst_ref=dst, send_sem=send.at[k], recv_sem=recv.at[k],
                                              device_id=peer, device_id_type=MESH)
            cp.start()
            started.append(cp)
        for k, (_, _, land, peer) in enumerate(copies):
            pltpu.make_async_remote_copy(src_ref=land, dst_ref=land, send_sem=send.at[k], recv_sem=recv.at[k],
                                         device_id=peer, device_id_type=MESH).wait_recv()
        for cp in started:
            cp.wait_send()

    return pl.pallas_call(
        body, name=name, in_specs=[ANY] * n, out_specs=[ANY] * n,
        out_shape=[jax.ShapeDtypeStruct(b.shape, b.dtype) for b in bufs],
        input_output_aliases={k: k for k in range(n)},
        scratch_shapes=[pltpu.SemaphoreType.DMA((ncp,))] * 2,
    )(*bufs)


def _comm_start(name, kind, bufs, after=None):
    n = len(bufs)
    ncp = _n_copies(kind, n)
    nx = 0 if after is None else 1

    def body(*refs):
        refs = refs[n + nx:]
        send, recv = refs[:ncp], refs[ncp:2 * ncp]
        outs = refs[2 * ncp:n + 2 * ncp]
        token = refs[n + 2 * ncp]
        for k, (src, dst, _, peer) in enumerate(_comm_copies(kind, outs)):
            pltpu.make_async_remote_copy(src_ref=src, dst_ref=dst, send_sem=send[k], recv_sem=recv[k],
                                         device_id=peer, device_id_type=MESH).start()
        token[...] = jnp.zeros(token.shape, token.dtype)

    res = pl.pallas_call(
        body, name=name, in_specs=[HBM] * n + [ANY] * nx,
        out_specs=[SEM] * (2 * ncp) + [HBM] * n + [pl.BlockSpec(memory_space=pltpu.VMEM)],
        out_shape=[pltpu.SemaphoreType.DMA(())] * (2 * ncp) + [pltpu.HBM(b.shape, b.dtype) for b in bufs]
        + [jax.ShapeDtypeStruct((8, 128), F32)],
        input_output_aliases={k: 2 * ncp + k for k in range(n)},
        compiler_params=pltpu.CompilerParams(has_side_effects=EFFECT),
    )(*[pltpu.with_memory_space_constraint(b, pltpu.HBM) for b in bufs], *([after] if nx else []))
    return list(res[:ncp]), list(res[ncp:2 * ncp]), list(res[2 * ncp:2 * ncp + n]), res[2 * ncp + n]


def _comm_wait(name, kind, bufs, send_sems, recv_sems, after):
    n = len(bufs)
    ncp = _n_copies(kind, n)

    def body(*refs):
        send, recv = refs[n:n + ncp], refs[n + ncp:n + 2 * ncp]
        outs = refs[n + 2 * ncp + 1:]
        for k, (src, _, land, peer) in enumerate(_comm_copies(kind, outs)):
            cp = pltpu.make_async_remote_copy(src_ref=src, dst_ref=land, send_sem=send[k], recv_sem=recv[k],
                                              device_id=peer, device_id_type=MESH)
            cp.wait_send()
            cp.wait_recv()

    return pl.pallas_call(
        body, name=name, in_specs=[HBM] * n + [SEM] * (2 * ncp) + [ANY], out_specs=[HBM] * n,
        out_shape=[pltpu.HBM(b.shape, b.dtype) for b in bufs],
        input_output_aliases={k: k for k in range(n)},
        compiler_params=pltpu.CompilerParams(has_side_effects=EFFECT),
    )(*bufs, *send_sems, *recv_sems, after)


def _pair_add(g, got, core):
    nchip, _, r, cw = g.shape
    tr = _tile(r, 512, 16)

    def body(c_ref, a_ref, b_ref, o_ref):
        o_ref[...] = a_ref[...] + b_ref[...]

    return pl.pallas_call(
        body, name="grads_pair_add",
        grid_spec=pltpu.PrefetchScalarGridSpec(
            num_scalar_prefetch=1, grid=(nchip, r // tr),
            in_specs=[pl.BlockSpec((None, None, tr, cw), lambda s, i, c_ref: (s, c_ref[0], i, 0)),
                      pl.BlockSpec((None, tr, cw), lambda s, i, c_ref: (s, i, 0))],
            out_specs=pl.BlockSpec((None, tr, cw), lambda s, i, c_ref: (s, i, 0))),
        out_shape=jax.ShapeDtypeStruct((nchip, r, cw), BF16),
        compiler_params=_params(("arbitrary", "arbitrary")),
    )(core, g, got)


def _chip_sum(parts, got, idx):
    _, r, cw = parts.shape
    tr = _tile(r, 512, 16)

    def body(i_ref, own_ref, a_ref, b_ref, c_ref, o_ref):
        o_ref[...] = ((own_ref[...].astype(F32) + a_ref[...].astype(F32)) + b_ref[...].astype(F32)) + c_ref[...].astype(F32)

    def slot(k):
        return pl.BlockSpec((None, tr, cw), lambda i, i_ref: (i_ref[k], i, 0))

    return pl.pallas_call(
        body, name="grads_chip_sum",
        grid_spec=pltpu.PrefetchScalarGridSpec(
            num_scalar_prefetch=1, grid=(r // tr,), in_specs=[slot(0), slot(1), slot(2), slot(3)], out_specs=slot(4)),
        out_shape=jax.ShapeDtypeStruct((2, r, cw), F32),
        compiler_params=_params(("arbitrary",)),
    )(idx, parts, got, got, got)


def _small_all_reduce(vec):
    r, cw = vec.shape

    def body(v_ref, o_ref, sib_buf, chip_buf, send, recv):
        x, y, c, s, chips = _place()
        cp = pltpu.make_async_remote_copy(
            src_ref=v_ref, dst_ref=sib_buf, send_sem=send.at[0], recv_sem=recv.at[0],
            device_id=(x, y, 1 - c), device_id_type=MESH)
        cp.start()
        cp.wait()
        chip_buf[s] = v_ref[...] + sib_buf[...]
        cps = []
        for j, (cx, cy) in enumerate(chips):
            cp = pltpu.make_async_remote_copy(
                src_ref=chip_buf.at[s], dst_ref=chip_buf.at[s], send_sem=send.at[1 + j], recv_sem=recv.at[1 + j],
                device_id=(cx, cy, c), device_id_type=MESH)
            cp.start()
            cps.append(cp)
        for j, (cx, cy) in enumerate(chips):
            pltpu.make_async_remote_copy(
                src_ref=chip_buf.at[s], dst_ref=chip_buf.at[2 * cx + cy], send_sem=send.at[1 + j],
                recv_sem=recv.at[1 + j], device_id=(cx, cy, c), device_id_type=MESH).wait_recv()
        for cp in cps:
            cp.wait_send()
        o_ref[...] = ((chip_buf[0] + chip_buf[1]) + chip_buf[2]) + chip_buf[3]

    vm = pl.BlockSpec(memory_space=pltpu.VMEM)
    return pl.pallas_call(
        body, name="small_all_reduce", in_specs=[vm], out_specs=vm,
        out_shape=jax.ShapeDtypeStruct((r, cw), F32),
        scratch_shapes=[pltpu.VMEM((r, cw), F32), pltpu.VMEM((N_CHIPS, r, cw), F32),
                        pltpu.SemaphoreType.DMA((4,)), pltpu.SemaphoreType.DMA((4,))],
        compiler_params=_params(),
    )(vec)


def _adamw_math(w, g, m, v):
    m = ADAM_B1 * m + (1.0 - ADAM_B1) * g
    v = ADAM_B2 * v + (1.0 - ADAM_B2) * (g * g)
    m_hat = m / (1.0 - ADAM_B1 ** ADAM_STEP)
    v_hat = v / (1.0 - ADAM_B2 ** ADAM_STEP)
    delta = -ADAM_LR * (m_hat / (jnp.sqrt(v_hat) + ADAM_EPS) + ADAM_WD * w)
    return delta, m, v


def _adamw(name, w, m, v, g, g_half=0, g_row_off=0, tr=256):
    r, cw = w.shape
    tr = _tile(math.gcd(r, g_row_off) if g_row_off else r, tr)
    off = g_row_off // tr

    def body(w_ref, m_ref, v_ref, g_ref, go_ref, d_ref, mo_ref, vo_ref):
        g_v = g_ref[...]
        delta, m_n, v_n = _adamw_math(w_ref[...], g_v, m_ref[...], v_ref[...])
        go_ref[...] = g_v
        d_ref[...] = delta
        mo_ref[...] = m_n
        vo_ref[...] = v_n

    tile = pl.BlockSpec((tr, cw), lambda i: (i, 0))
    out = jax.ShapeDtypeStruct((r, cw), F32)
    return pl.pallas_call(
        body, name=name, grid=(r // tr,),
        in_specs=[tile, tile, tile, pl.BlockSpec((None, tr, cw), lambda i: (g_half, i + off, 0))],
        out_specs=[tile] * 4, out_shape=[out] * 4,
        compiler_params=_params(("arbitrary",)),
    )(w, m, v, g)


def kernel(x, mem, g_ffn1, w1_gate, w1_up, w1_down, g_mix, w_in, ssm_a_re, ssm_a_im, ssm_log_dt, ssm_b_re, ssm_b_im, ssm_c_re, ssm_c_im, ssm_d, w_glu, b_glu, w_pool, pool_scale, g_out_ssm, g_out_pool, w_out, g_xattn, g_mem, w_q, w_k, w_v, w_o, g_ffn2, w2_gate, w2_up, w2_down, g_final, loss_target, m_g_ffn1, m_w1_gate, m_w1_up, m_w1_down, m_g_mix, m_w_in, m_ssm_a_re, m_ssm_a_im, m_ssm_log_dt, m_ssm_b_re, m_ssm_b_im, m_ssm_c_re, m_ssm_c_im, m_ssm_d, m_w_glu, m_b_glu, m_w_pool, m_pool_scale, m_g_out_ssm, m_g_out_pool, m_w_out, m_g_xattn, m_g_mem, m_w_q, m_w_k, m_w_v, m_w_o, m_g_ffn2, m_w2_gate, m_w2_up, m_w2_down, m_g_final, v_g_ffn1, v_w1_gate, v_w1_up, v_w1_down, v_g_mix, v_w_in, v_ssm_a_re, v_ssm_a_im, v_ssm_log_dt, v_ssm_b_re, v_ssm_b_im, v_ssm_c_re, v_ssm_c_im, v_ssm_d, v_w_glu, v_b_glu, v_w_pool, v_pool_scale, v_g_out_ssm, v_g_out_pool, v_w_out, v_g_xattn, v_g_mem, v_w_q, v_w_k, v_w_v, v_w_o, v_g_ffn2, v_w2_gate, v_w2_up, v_w2_down, v_g_final):
    local = dict(locals())
    wts = {n: local[n] for n in WEIGHTS}
    mom = {n: local["m_" + n] for n in WEIGHTS}
    var = {n: local["v_" + n] for n in WEIGHTS}

    x2 = x[0]
    mem2 = mem[0]
    tgt = loss_target[0]
    t_rows, d = x2.shape
    fs = w1_gate.shape[-1]
    ds_ = w_in.shape[1]
    ws = d // 2
    n_pg = len(POOL_WINDOWS)
    pw = ws // n_pg
    n_grp = ws // SSM_GROUP
    n_state = ssm_a_re.shape[-1]
    cx_, cy_, cc_ = lax.axis_index("x"), lax.axis_index("y"), lax.axis_index("c")
    chip = (2 * cx_ + cy_).astype(jnp.int32)
    core = cc_.astype(jnp.int32).reshape(1)
    chip_idx = jnp.stack([chip, chip ^ 2, chip ^ 1, chip ^ 3, cc_.astype(jnp.int32)])

    glu_rows = w_glu[0].reshape(-1, d)
    pool_rows = w_pool[0].reshape(-1, d)
    gh, ph = glu_rows.shape[0] // 2, pool_rows.shape[0] // 2
    rh = -(-(3 * ds_ + gh + ph) // 128) * 128
    pad_rows = jnp.zeros((rh - 3 * ds_ - gh - ph, d), F32)

    def own_slot(src):
        src = src.astype(BF16)
        return lax.dynamic_update_slice(lax.empty((N_CHIPS,) + src.shape, BF16), src[None], (chip, 0, 0, 0))

    src_b2 = jnp.stack([
        jnp.concatenate([w_in[0], w_out[0], w_q[0], glu_rows[:gh], pool_rows[:ph], pad_rows], 0),
        jnp.concatenate([w_k[0], w_v[0], w_o[0], glu_rows[gh:], pool_rows[ph:], pad_rows], 0)])
    w_bufs = [own_slot(jnp.stack([w1_gate[0], w1_up[0]])), own_slot(w1_down[0].reshape(2, fs // 2, d)),
              own_slot(src_b2), own_slot(jnp.stack([w2_gate[0], w2_up[0]])),
              own_slot(w2_down[0].reshape(2, fs // 2, d))]
    ag_send, ag_recv, w_bufs, ag_token = _comm_start("weights_start", "ag_ici", w_bufs)

    def gathered(k, after):
        w = _comm_wait("weights_wait_%d" % k, "ag_ici", [w_bufs[k]], ag_send[3 * k:3 * k + 3],
                       ag_recv[3 * k:3 * k + 3], after)
        return _comm_fused("weights_forward_%d" % k, "ag_fwd", w)[0]

    def gathered_start(k, after):
        w = _comm_wait("weights_wait_%d" % k, "ag_ici", [w_bufs[k]], ag_send[3 * k:3 * k + 3],
                       ag_recv[3 * k:3 * k + 3], after)
        send, recv, thru, token = _comm_start("weights_forward_start_%d" % k, "ag_fwd", w)
        return (send, recv, thru), token

    def gathered_finish(k, handle, after):
        send, recv, thru = handle
        return _comm_wait("weights_forward_wait_%d" % k, "ag_fwd", thru, send, recv, after)[0]

    ga1 = gathered(0, ag_token)
    n1 = _rmsnorm("norm_ffn1", x2, wts['g_ffn1'].reshape(1, -1), deps=[ag_token])
    a1, b1, hm1 = _ffn_up("ffn1_up", n1, ga1, d, fs)
    gd1 = gathered(1, hm1).reshape(N_CHIPS, fs, d)
    h1, n2 = _mm_nn("ffn1_down", hm1, gd1, (None, fs, d), lambda j, k: (k, 0, 0), N_CHIPS, d, F32, res=x2, alpha=0.5,
                    norm_g=wts['g_mix'].reshape(1, -1))
    gb2 = gathered(2, h1)
    wglu_full = gb2[:, :, 3 * ds_:3 * ds_ + gh, :].reshape(ws, ws)
    wpool_full = gb2[:, :, 3 * ds_ + gh:3 * ds_ + gh + ph, :].reshape(N_CHIPS, n_pg, pw // N_CHIPS, pw)
    wpool_full = wpool_full.transpose(1, 0, 2, 3).reshape(n_pg, pw, pw)
    DD = {'w_in': (0, 0), 'w_out': (0, 1), 'w_q': (0, 2), 'w_k': (1, 0), 'w_v': (1, 1), 'w_o': (1, 2)}

    def mm_dd(name, a, wname, out_dtype, res=None, norm_g=None):
        h, q = DD[wname]
        return _mm_nn(name, a, gb2, (None, None, ds_, d), lambda j, k: (k, h, q, 0), N_CHIPS, d, out_dtype, res=res,
                      norm_g=norm_g)

    def mm_dd_t(name, pairs, out_dtype, deps=()):
        ps = [(dy, gb2, (None, None, ds_, d), functools.partial(lambda s, h, q: (s, h, q, 0), h=DD[w][0], q=DD[w][1]))
              for dy, w in pairs]
        return _mm_nt_cols(name, ps, N_CHIPS, ds_, [out_dtype], deps=deps)[0]

    def vec(n):
        return wts[n].reshape(1, -1)

    disc_in = (ssm_a_re[0], ssm_a_im[0], ssm_log_dt[0], ssm_b_re[0], ssm_b_im[0])
    (abar_re, abar_im, bbar_re, bbar_im), disc_vjp = jax.vjp(_s5_discretize, *disc_in)
    gpb = min(S5_GROUPS_PER_BLOCK, n_grp)
    nb = n_grp // gpb
    cb = gpb * n_state
    eye = jnp.eye(gpb, dtype=F32)

    def blockdiag(t):
        return jnp.einsum('jgph,gk->jghkp', t.reshape(nb, gpb, n_state, SSM_GROUP), eye).reshape(nb, gpb * SSM_GROUP, cb)

    def blockdiag_c(t):
        return jnp.einsum('jghp,gk->jkpgh', t.reshape(nb, gpb, SSM_GROUP, n_state), eye).reshape(nb, cb, gpb * SSM_GROUP)

    bblk = jnp.concatenate([blockdiag(bbar_re), blockdiag(bbar_im)], -1).astype(BF16)
    cblk = jnp.concatenate([blockdiag_c(ssm_c_re[0]), -blockdiag_c(ssm_c_im[0])], 1).astype(BF16)
    ab = jnp.stack([abar_re.reshape(nb, cb), abar_im.reshape(nb, cb)], 1)

    u = mm_dd("mix_in", n2, 'w_in', F32)

    up = _perm_rows(u[:, :ws]).astype(BF16)
    ylin_p, s_all = _s5_fwd(up, bblk, ab, cblk)
    ylin = _unperm_rows(ylin_p)

    def gelu_fn(r, v):
        y1 = r[0] + v[0] * r[1]
        y2 = jax.nn.gelu(y1)
        return [y2, y2], []
    fwd_a2, tok = gathered_start(3, ylin_p)
    y2, y2b = _rowwise("s5_gelu", gelu_fn, [ylin, (u, 0, ws)], [vec('ssm_d')], [(ws, F32), (ws, BF16)], deps=[tok])
    z = _mm_nn("s5_glu", y2b, wglu_full, (ws, ws), lambda j, k: (0, 0), 1, ws, F32)

    def glu_fn(r, v):
        y3 = r[0] * _sigmoid(r[1] + v[0])
        return [_rms_fwd(y3, v[1])], []
    m_ssm = _rowwise("s5_gate_norm", glu_fn, [y2, z], [vec('b_glu'), vec('g_out_ssm')], [(ws, BF16)])[0]

    pooled, zp = _pool_fwd(u, ws // pw, wpool_full, vec('pool_scale'))
    fwd_d2, tok = gathered_start(4, zp)
    m_pool = _rmsnorm("norm_pool", zp, vec('g_out_pool'), deps=[tok])
    merged = jnp.concatenate([m_ssm, m_pool], -1)
    h2, hn = mm_dd("mix_out", merged, 'w_out', F32, res=h1, norm_g=vec('g_xattn'))

    memn = _rmsnorm("norm_mem", mem2, vec('g_mem'))
    k_mem = mm_dd("attn_k", memn, 'w_k', BF16)
    v_mem = mm_dd("attn_v", memn, 'w_v', BF16)
    q = mm_dd("attn_q", hn, 'w_q', BF16)
    o = _attn_fwd(q, k_mem, v_mem)
    h3, n4 = mm_dd("attn_out", o, 'w_o', F32, res=h2, norm_g=vec('g_ffn2'))

    ga2 = gathered_finish(3, fwd_a2, h3)
    a2, b2, hm2 = _ffn_up("ffn2_up", n4, ga2, d, fs)
    gd2 = gathered_finish(4, fwd_d2, hm2).reshape(N_CHIPS, fs, d)
    h4 = _mm_nn("ffn2_down", hm2, gd2, (None, fs, d), lambda j, k: (k, 0, 0), N_CHIPS, d, F32, res=h3, alpha=0.5)

    def loss_fn(r, v):
        h, t = r
        e = _rms_fwd(h, v[0]) - t
        dy = e * (1.0 / d)
        dh, dg = _rms_bwd(dy, h, v[0])
        part = jnp.sum(_colsum(e * e), axis=1, keepdims=True) * (0.5 / d)
        return [dh, 0.5 * dh], [_colsum(dg), jnp.broadcast_to(part, (1, 128))]
    dh4, dy_f2, dg_final, loss_row = _rowwise("loss_head", loss_fn, [h4, tgt], [g_final.reshape(1, -1)],
                                              [(d, F32), (d, BF16)], [d, 128])

    def rs_pair_start(tag, gbufs):
        land = [lax.empty((N_CHIPS,) + g.shape[2:], BF16) for g in gbufs]
        send, recv, thru, token = _comm_start(tag + "_pair_start", "pair", list(gbufs) + land)
        return (send, recv, thru), token

    def rs_scatter_start(tag, handle, after):
        send, recv, thru = handle
        n = len(thru) // 2
        res = _comm_wait(tag + "_pair_wait", "pair", thru, send, recv, after)
        parts = [_pair_add(g, r, core) for g, r in zip(res[:n], res[n:])]
        land = [lax.empty(p.shape, BF16) for p in parts]
        send, recv, thru, token = _comm_start(tag + "_scatter_start", "scatter", parts + land)
        return (send, recv, thru), token

    def rs_half_start(tag, handle, after):
        send, recv, thru = handle
        n = len(thru) // 2
        res = _comm_wait(tag + "_scatter_wait", "scatter", thru, send, recv, after)
        full = [_chip_sum(p, g2, chip_idx) for p, g2 in zip(res[:n], res[n:])]
        send, recv, thru, token = _comm_start(tag + "_half_start", "half", full)
        return (send, recv, thru), token

    def rs_finish(tag, handle, after):
        send, recv, thru = handle
        return _comm_wait(tag + "_half_wait", "half", thru, send, recv, after)

    wblk = (None, None, d, fs)

    def ffn_down_bwd(tag, dy_half, a, b, hm, gd_l, deps=()):
        da, db = _mm_nt_cols(tag + "_down_bwd", [(dy_half, gd_l, (None, fs, d), lambda s: (s, 0, 0))],
                             N_CHIPS, fs, [BF16, BF16], epi=_swiglu_bwd, extras=[a, b], deps=deps)
        g_down = _mm_tn(tag + "_dw_down", hm, dy_half, fs, d // 2, N_CHIPS, 2, jax.ShapeDtypeStruct((N_CHIPS, fs, d), BF16),
                        (None, fs, d // 2), lambda p, q: (p, 0, q), tt=2048)
        return da, db, g_down.reshape(N_CHIPS, 2, fs // 2, d)

    def ffn_up_bwd(tag, da, db, ga_l, deps=()):
        return _mm_nt_k(tag + "_up_bwd", [(da, ga_l, wblk, lambda s: (s, 0, 0, 0)), (db, ga_l, wblk, lambda s: (s, 1, 0, 0))],
                        N_CHIPS, d, F32, deps=deps)

    def ffn_dw_up(tag, da, db, n_in, deps=()):
        hblk = (None, None, d // 2, fs)
        g_up = _mm_tn(tag + "_dw_gate", n_in, da, d // 2, fs, 2, N_CHIPS, jax.ShapeDtypeStruct((N_CHIPS, 2, d, fs), BF16),
                      hblk, lambda p, q: (q, 0, p, 0), tt=2048, deps=deps)
        return _mm_tn(tag + "_dw_up", n_in, db, d // 2, fs, 2, N_CHIPS, None, hblk, lambda p, q: (q, 1, p, 0), into=g_up,
                      tt=2048)

    def dw_dd(name, a, dy, wname, grad_b2):
        h, q = DD[wname]
        return _mm_tn(name, a, dy, ds_, d, N_CHIPS, 1, jax.ShapeDtypeStruct((N_CHIPS, 2, rh, d), BF16),
                      (None, None, ds_, d), lambda p, qq: (p, h, q, 0), into=grad_b2, tt=2048)

    def norm_bwd(name, dn, h, gname, dres, deps=(), scale=1.0):
        def fn(r, v):
            dx, dg = _rms_bwd(r[0], r[1], v[0])
            tot = dx + r[2]
            return [tot, scale * tot], [_colsum(dg)]
        return _rowwise(name, fn, [dn, h, dres], [vec(gname)], [(d, F32), (d, BF16)], [d], deps=deps)

    da2, db2, g_down2 = ffn_down_bwd("ffn2", dy_f2, a2, b2, hm2, gd2)
    dn4 = ffn_up_bwd("ffn2", da2, db2, ga2)
    g_up2 = ffn_dw_up("ffn2", da2, db2, n4)
    rs_f2, tok = rs_pair_start("ffn2", [g_up2, g_down2])
    dh3, dh3b, dg_ffn2 = norm_bwd("norm_ffn2_bwd", dn4, h3, 'g_ffn2', dh4, deps=[tok])
    rs_f2, tok = rs_scatter_start("ffn2", rs_f2, dh3b)

    do = mm_dd_t("attn_out_bwd", [(dh3b, 'w_o')], BF16, deps=[tok])
    grad_b2 = dw_dd("attn_dw_o", o, dh3b, 'w_o', None)
    dq, dk, dv = _attn_bwd(q, k_mem, v_mem, do)
    dkb, dvb = dk.astype(BF16), dv.astype(BF16)
    grad_b2 = dw_dd("attn_dw_q", hn, dq, 'w_q', grad_b2)
    dhn = mm_dd_t("attn_q_bwd", [(dq, 'w_q')], F32)
    dh2, dh2b, dg_xattn = norm_bwd("norm_xattn_bwd", dhn, h2, 'g_xattn', dh3)
    grad_b2 = dw_dd("attn_dw_k", memn, dkb, 'w_k', grad_b2)
    grad_b2 = dw_dd("attn_dw_v", memn, dvb, 'w_v', grad_b2)
    dmemn = mm_dd_t("attn_kv_bwd", [(dkb, 'w_k'), (dvb, 'w_v')], F32)
    dg_mem = _rowwise("norm_mem_bwd", lambda r, v: ([], [_colsum(_rms_bwd(r[0], r[1], v[0])[1])]),
                      [dmemn, mem2], [vec('g_mem')], [], [d])[0]

    dmerged = mm_dd_t("mix_out_bwd", [(dh2b, 'w_out')], F32)
    grad_b2 = dw_dd("mix_dw_out", merged, dh2b, 'w_out', grad_b2)

    def gate_bwd_fn(r, v):
        dm, y2_v, z_v = r
        sg = _sigmoid(z_v + v[0])
        y3 = y2_v * sg
        dy3, dg = _rms_bwd(dm, y3, v[1])
        dz = dy3 * y3 * (1.0 - sg)
        return [dy3 * sg, dz], [_colsum(dg), _colsum(dz)]
    dy2a, dzb, dg_out_ssm, db_glu = _rowwise("s5_gate_norm_bwd", gate_bwd_fn, [(dmerged, 0, ws), y2, z],
                                             [vec('b_glu'), vec('g_out_ssm')], [(ws, F32), (ws, BF16)], [ws, ws])
    dy2b_ = _mm_nt_cols("s5_glu_bwd", [(dzb, wglu_full, (ws, ws), lambda s: (0, 0))], 1, ws, [F32])[0]
    dw_glu = _mm_tn("s5_dw_glu", y2b, dzb, ws, ws, 1, 1, jax.ShapeDtypeStruct((ws, ws), F32), (ws, ws), lambda p, q: (0, 0))

    def gelu_bwd_fn(r, v):
        dy2 = r[0] + r[1]
        us = r[3]
        y1 = r[2] + v[0] * us
        kk = math.sqrt(2.0 / math.pi)
        th = jnp.tanh(kk * (y1 + 0.044715 * y1 * y1 * y1))
        dgelu = 0.5 * (1.0 + th) + 0.5 * y1 * (1.0 - th * th) * kk * (1.0 + 3.0 * 0.044715 * y1 * y1)
        dy1 = dy2 * dgelu
        return [dy1, dy1 * v[0]], [_colsum(dy1 * us)]
    dy1b, du_skip, d_ssm_d = _rowwise("s5_gelu_bwd", gelu_bwd_fn, [dy2a, dy2b_, ylin, (u, 0, ws)], [vec('ssm_d')],
                                      [(ws, BF16), (ws, F32)], [ws])

    bblk_t = jnp.swapaxes(bblk, 1, 2)
    cblk_t = jnp.swapaxes(cblk, 1, 2)
    du_p, d_bblk, d_cblk_t, d_ab = _s5_bwd(_perm_rows(dy1b), up, s_all, bblk_t, ab, cblk_t)
    du_ssm = _unperm_rows(du_p)

    dzp, dg_out_pool = _rowwise("norm_pool_bwd", lambda r, v: (lambda dx, dg: ([dx], [_colsum(dg)]))(*_rms_bwd(r[0], r[1], v[0])),
                                [(dmerged, 1, ws), zp], [vec('g_out_pool')], [(ws, F32)], [ws])
    dps, dw_pool, d_pool_scale = _pool_bwd1(dzp, pooled, wpool_full, vec('pool_scale'))
    du_pool = _pool_bwd2(dps, n_pg)

    dub = _rowwise("mix_du", lambda r, v: ([jnp.concatenate([r[0] + r[1], r[2]], -1)], []),
                   [du_ssm, du_skip, du_pool], [], [(d, BF16)])[0]
    dn2 = mm_dd_t("mix_in_bwd", [(dub, 'w_in')], F32)
    grad_b2 = dw_dd("mix_dw_in", n2, dub, 'w_in', grad_b2)
    glu_g = dw_glu.reshape(N_CHIPS, 2, gh, d).astype(BF16)
    pool_g = dw_pool.reshape(n_pg, N_CHIPS, pw // N_CHIPS, pw).transpose(1, 0, 2, 3).reshape(N_CHIPS, 2, ph, d)
    pool_g = jnp.concatenate([pool_g, jnp.zeros((N_CHIPS, 2, rh - 3 * ds_ - gh - ph, d), F32)], 2).astype(BF16)
    grad_b2 = lax.dynamic_update_slice(grad_b2, glu_g, (0, 0, 3 * ds_, 0))
    grad_b2 = lax.dynamic_update_slice(grad_b2, pool_g, (0, 0, 3 * ds_ + gh, 0))
    rs_mix, tok = rs_pair_start("mixers", [grad_b2])
    dh1, dy_f1, dg_mix = norm_bwd("norm_mix_bwd", dn2, h1, 'g_mix', dh2, deps=[tok], scale=0.5)
    rs_mix, tok = rs_scatter_start("mixers", rs_mix, dy_f1)

    da1, db1, g_down1 = ffn_down_bwd("ffn1", dy_f1, a1, b1, hm1, gd1, deps=[tok])
    rs_d1, tok = rs_pair_start("ffn1_down", [g_down1])
    dn1 = ffn_up_bwd("ffn1", da1, db1, ga1, deps=[tok])
    rs_d1, tok = rs_scatter_start("ffn1_down", rs_d1, dn1)
    grad_x, _, dg_ffn1 = norm_bwd("norm_ffn1_bwd", dn1, x2, 'g_ffn1', dh1, deps=[tok])

    def undiag(t):
        return jnp.einsum('jghkp,gk->jgph', t.reshape(nb, gpb, SSM_GROUP, gpb, n_state), eye).reshape(n_grp, n_state, SSM_GROUP)

    d_bbar_re, d_bbar_im = undiag(d_bblk[:, :, :cb]), undiag(d_bblk[:, :, cb:])
    d_c_re = undiag(d_cblk_t[:, :, :cb]).transpose(0, 2, 1)
    d_c_im = -undiag(d_cblk_t[:, :, cb:]).transpose(0, 2, 1)
    d_abar = jnp.sum(d_ab, axis=2).reshape(nb, 2, gpb, n_state)
    d_abar_re = d_abar[:, 0].reshape(n_grp, n_state)
    d_abar_im = d_abar[:, 1].reshape(n_grp, n_state)
    d_a_re, d_a_im, d_log_dt, d_b_re, d_b_im = disc_vjp((d_abar_re, d_abar_im, d_bbar_re, d_bbar_im))

    small_g = {'g_ffn1': dg_ffn1, 'g_mix': dg_mix, 'ssm_a_re': d_a_re, 'ssm_a_im': d_a_im, 'ssm_log_dt': d_log_dt,
               'ssm_b_re': d_b_re, 'ssm_b_im': d_b_im, 'ssm_c_re': d_c_re, 'ssm_c_im': d_c_im, 'ssm_d': d_ssm_d,
               'b_glu': db_glu, 'pool_scale': d_pool_scale, 'g_out_ssm': dg_out_ssm, 'g_out_pool': dg_out_pool,
               'g_xattn': dg_xattn, 'g_mem': dg_mem, 'g_ffn2': dg_ffn2, 'g_final': dg_final}
    sizes = [wts[n].size for n in SMALL]
    total = sum(sizes) + 128
    rows_s = -(-total // 1024) * 8
    flat = jnp.concatenate([small_g[n].reshape(-1) for n in SMALL] + [loss_row.reshape(-1)])
    flat = jnp.pad(flat, (0, rows_s * 128 - total)).reshape(rows_s, 128)
    red = _small_all_reduce(flat).reshape(-1)
    loss = red[sum(sizes)]

    def flat_small(t):
        return jnp.pad(jnp.concatenate([t[n].reshape(-1) for n in SMALL]), (0, rows_s * 128 - sum(sizes))).reshape(rows_s, 128)
    sg_, sd_, sm_, sv_ = _adamw("adamw_small", flat_small(wts), flat_small(mom), flat_small(var), red.reshape(1, rows_s, 128))
    out = {}
    off = 0
    for n, sz in zip(SMALL, sizes):
        for key, arr in (('grad', sg_), ('delta', sd_), ('m', sm_), ('v', sv_)):
            out[key, n] = arr.reshape(-1)[off:off + sz].reshape(wts[n].shape)
        off += sz

    def upd(n, g_arr, half, row_off, shape2):
        res = _adamw("adamw_" + n, wts[n].reshape(shape2), mom[n].reshape(shape2), var[n].reshape(shape2), g_arr, half, row_off)
        for key, arr in zip(('grad', 'delta', 'm', 'v'), res):
            out[key, n] = arr.reshape(wts[n].shape)
        return res[3]

    g_up1 = ffn_dw_up("ffn1", da1, db1, n1, deps=[sv_])
    rs_u1, tok = rs_pair_start("ffn1_up", [g_up1])
    rs_f2, tok = rs_half_start("ffn2", rs_f2, tok)
    rs_u1, tok = rs_scatter_start("ffn1_up", rs_u1, tok)
    rs_mix, tok = rs_half_start("mixers", rs_mix, tok)
    full_up2, full_down2 = rs_finish("ffn2", rs_f2, tok)
    upd('w2_gate', full_up2, 0, 0, (d, fs))
    upd('w2_up', full_up2, 1, 0, (d, fs))
    last = upd('w2_down', full_down2.reshape(1, fs, d), 0, 0, (fs, d))
    rs_d1, tok = rs_half_start("ffn1_down", rs_d1, last)
    full_b2, = rs_finish("mixers", rs_mix, tok)
    for n, (h, q) in DD.items():
        last = upd(n, full_b2, h, q * ds_, (ds_, d))
    glu_shape, pool_shape = (ws // N_CHIPS, ws), (n_pg * pw // N_CHIPS, pw)
    upd('w_glu', full_b2[:, 3 * ds_:3 * ds_ + gh].reshape((1,) + glu_shape), 0, 0, glu_shape)
    upd('w_pool', full_b2[:, 3 * ds_ + gh:3 * ds_ + gh + ph].reshape((1,) + pool_shape), 0, 0, pool_shape)
    full_down1, = rs_finish("ffn1_down", rs_d1, last)
    last = upd('w1_down', full_down1.reshape(1, fs, d), 0, 0, (fs, d))
    rs_u1, tok = rs_half_start("ffn1_up", rs_u1, last)
    full_up1, = rs_finish("ffn1_up", rs_u1, tok)
    upd('w1_gate', full_up1, 0, 0, (d, fs))
    upd('w1_up', full_up1, 1, 0, (d, fs))

    return (loss, grad_x[None], *[out['grad', n] for n in WEIGHTS], *[out['delta', n] for n in WEIGHTS],
            *[out['m', n] for n in WEIGHTS], *[out['v', n] for n in WEIGHTS])
```

```python
import functools
import math

import jax
import jax.numpy as jnp
from jax import lax
from jax.experimental import pallas as pl
from jax.experimental.pallas import tpu as pltpu

F32 = jnp.float32
BF16 = jnp.bfloat16
EPS = 1e-6
ADAM_LR, ADAM_B1, ADAM_B2, ADAM_EPS, ADAM_WD, ADAM_STEP = 0.001, 0.9, 0.999, 1e-08, 0.01, 10
POOL_WINDOWS = (2, 4, 8, 16)
SSM_GROUP = 16
S5_GROUPS_PER_BLOCK = 16
S5_LANES = 8
MEM_HEADS = 4
N_CHIPS = 4
VMEM_LIMIT_V7X = 56 * 1024 * 1024
MESH = pl.DeviceIdType.MESH

WEIGHTS = ['g_ffn1', 'w1_gate', 'w1_up', 'w1_down', 'g_mix', 'w_in', 'ssm_a_re', 'ssm_a_im', 'ssm_log_dt',
           'ssm_b_re', 'ssm_b_im', 'ssm_c_re', 'ssm_c_im', 'ssm_d', 'w_glu', 'b_glu', 'w_pool', 'pool_scale',
           'g_out_ssm', 'g_out_pool', 'w_out', 'g_xattn', 'g_mem', 'w_q', 'w_k', 'w_v', 'w_o', 'g_ffn2',
           'w2_gate', 'w2_up', 'w2_down', 'g_final']
BIG = ['w1_gate', 'w1_up', 'w1_down', 'w_in', 'w_glu', 'w_pool', 'w_out', 'w_q', 'w_k', 'w_v', 'w_o',
       'w2_gate', 'w2_up', 'w2_down']
SMALL = [n for n in WEIGHTS if n not in BIG]


def _tile(n, target, mult=8):
    best = None
    for d in range(1, n + 1):
        if n % d == 0 and d <= target and d % mult == 0:
            best = d
    return best if best is not None else n


def _params(sem=None):
    if sem is None:
        return pltpu.CompilerParams(vmem_limit_bytes=VMEM_LIMIT_V7X)
    return pltpu.CompilerParams(dimension_semantics=sem, vmem_limit_bytes=VMEM_LIMIT_V7X)


def _sigmoid(x):
    return 1.0 / (1.0 + jnp.exp(-x))


def _rms_fwd(x, g):
    r = lax.rsqrt(jnp.mean(x * x, axis=-1, keepdims=True) + EPS)
    return x * r * g


def _rms_bwd(dy, x, g):
    r = lax.rsqrt(jnp.mean(x * x, axis=-1, keepdims=True) + EPS)
    dxh = dy * g
    dx = r * dxh - x * (r * r * r) * jnp.mean(dxh * x, axis=-1, keepdims=True)
    return dx, dy * x * r


def _colsum(v):
    return jnp.sum(v, axis=0, keepdims=True)


def _rowwise(name, fn, rows, vecs, out_defs, red_defs=(), tm=256, deps=()):
    rows = [r if isinstance(r, tuple) else (r, 0, r.shape[1]) for r in rows]
    t_rows = rows[0][0].shape[0]
    tm = _tile(t_rows, tm)
    nr, nv, no, nd, nx = len(rows), len(vecs), len(out_defs), len(red_defs), len(deps)

    def body(*refs):
        r, v = refs[:nr], refs[nr:nr + nv]
        o, d = refs[nr + nv + nx:nr + nv + nx + no], refs[nr + nv + nx + no:]
        outs, reds = fn([x[...] for x in r], [x[...] for x in v])
        for ref, val in zip(o, outs):
            ref[...] = val.astype(ref.dtype)
        if nd:
            @pl.when(pl.program_id(0) == 0)
            def _():
                for ref in d:
                    ref[...] = jnp.zeros(ref.shape, ref.dtype)
            for ref, val in zip(d, reds):
                ref[...] += val

    in_specs = [pl.BlockSpec((tm, w), functools.partial(lambda i, cb: (i, cb), cb=cb)) for (_, cb, w) in rows]
    in_specs += [pl.BlockSpec(v.shape, lambda i: (0, 0)) for v in vecs]
    in_specs += [pl.BlockSpec(memory_space=pl.ANY)] * nx
    out_specs = [pl.BlockSpec((tm, w), lambda i: (i, 0)) for (w, _) in out_defs]
    out_specs += [pl.BlockSpec((1, w), lambda i: (0, 0)) for w in red_defs]
    out_shape = [jax.ShapeDtypeStruct((t_rows, w), dt) for (w, dt) in out_defs]
    out_shape += [jax.ShapeDtypeStruct((1, w), F32) for w in red_defs]
    res = pl.pallas_call(
        body, name=name, grid=(t_rows // tm,), in_specs=in_specs, out_specs=out_specs, out_shape=out_shape,
        compiler_params=_params(("arbitrary",)),
    )(*[r[0] for r in rows], *vecs, *deps)
    return res


def _rmsnorm(name, x, g, tm=256, deps=()):
    return _rowwise(name, lambda r, v: ([_rms_fwd(r[0].astype(F32), v[0])], []), [x], [g],
                    [(x.shape[1], BF16)], tm=tm, deps=deps)[0]


def _mm_nn(name, a, b, b_block, b_idx, nk, n_out, out_dtype, res=None, alpha=1.0, norm_g=None, tm=512):
    t_rows = a.shape[0]
    bk, tn = b_block[-2], b_block[-1]
    tm = _tile(t_rows, tm)
    nj = n_out // tn
    has_res = res is not None
    has_norm = norm_g is not None
    assert not has_norm or nj == 1

    def body(*refs):
        a_ref, b_ref = refs[0], refs[1]
        res_ref = refs[2] if has_res else None
        g_ref = refs[2 + has_res] if has_norm else None
        o_ref = refs[2 + has_res + has_norm]
        n_ref = refs[3 + has_res + has_norm] if has_norm else None
        acc_ref = refs[3 + has_res + 2 * has_norm]
        k = pl.program_id(2)
        p = jnp.dot(a_ref[...], b_ref[...], preferred_element_type=F32)

        @pl.when(k == 0)
        def _():
            acc_ref[...] = p

        @pl.when(k > 0)
        def _():
            acc_ref[...] += p

        @pl.when(k == nk - 1)
        def _():
            r = acc_ref[...]
            if has_res:
                r = res_ref[...] + alpha * r
            o_ref[...] = r.astype(o_ref.dtype)
            if has_norm:
                n_ref[...] = _rms_fwd(r, g_ref[...]).astype(n_ref.dtype)

    in_specs = [pl.BlockSpec((tm, bk), lambda j, i, k: (i, k)),
                pl.BlockSpec(b_block, lambda j, i, k: b_idx(j, k))]
    args = [a, b]
    if has_res:
        in_specs.append(pl.BlockSpec((tm, tn), lambda j, i, k: (i, j)))
        args.append(res)
    tile = pl.BlockSpec((tm, tn), lambda j, i, k: (i, j))
    out_specs, out_shape = tile, jax.ShapeDtypeStruct((t_rows, n_out), out_dtype)
    if has_norm:
        in_specs.append(pl.BlockSpec((1, n_out), lambda j, i, k: (0, 0)))
        args.append(norm_g)
        out_specs, out_shape = [tile, tile], [out_shape, jax.ShapeDtypeStruct((t_rows, n_out), BF16)]
    return pl.pallas_call(
        body, name=name, grid=(nj, t_rows // tm, nk), in_specs=in_specs, out_specs=out_specs, out_shape=out_shape,
        scratch_shapes=[pltpu.VMEM((tm, tn), F32)],
        compiler_params=_params(("arbitrary", "arbitrary", "arbitrary")),
    )(*args)


def _dot_nt(x, w):
    return lax.dot_general(x, w, (((1,), (1,)), ((), ())), preferred_element_type=F32)


def _dot_tn(x, y):
    return lax.dot_general(x, y, (((0,), (0,)), ((), ())), preferred_element_type=F32)


def _mm_nt_cols(name, pairs, ns, bn, out_defs, epi=None, extras=(), tm=512, deps=()):
    t_rows = pairs[0][0].shape[0]
    tm = _tile(t_rows, tm)
    npair, nex, no, nx = len(pairs), len(extras), len(out_defs), len(deps)

    def body(*refs):
        acc = None
        for p in range(npair):
            part = _dot_nt(refs[2 * p][...], refs[2 * p + 1][...])
            acc = part if acc is None else acc + part
        ex = [r[...] for r in refs[2 * npair:2 * npair + nex]]
        outs = epi(acc, *ex) if epi is not None else (acc,)
        for ref, val in zip(refs[2 * npair + nex + nx:], outs):
            ref[...] = val.astype(ref.dtype)

    in_specs, args = [], []
    for (dy, w, w_block, w_idx) in pairs:
        in_specs.append(pl.BlockSpec((tm, dy.shape[1]), lambda s, i: (i, 0)))
        in_specs.append(pl.BlockSpec(w_block, functools.partial(lambda s, i, f: f(s), f=w_idx)))
        args += [dy, w]
    for e in extras:
        in_specs.append(pl.BlockSpec((tm, bn), lambda s, i: (i, s)))
        args.append(e)
    in_specs += [pl.BlockSpec(memory_space=pl.ANY)] * nx
    args += list(deps)
    res = pl.pallas_call(
        body, name=name, grid=(ns, t_rows // tm), in_specs=in_specs,
        out_specs=[pl.BlockSpec((tm, bn), lambda s, i: (i, s)) for _ in range(no)],
        out_shape=[jax.ShapeDtypeStruct((t_rows, ns * bn), dt) for dt in out_defs],
        compiler_params=_params(("arbitrary", "arbitrary")),
    )(*args)
    return res


def _mm_nt_k(name, pairs, ns, n_out, out_dtype, tm=512, deps=()):
    t_rows = pairs[0][0].shape[0]
    tm = _tile(t_rows, tm)
    npair, nx = len(pairs), len(deps)

    def body(*refs):
        o_ref, acc_ref = refs[2 * npair + nx], refs[2 * npair + nx + 1]
        s = pl.program_id(1)
        acc = None
        for p in range(npair):
            part = _dot_nt(refs[2 * p][...], refs[2 * p + 1][...])
            acc = part if acc is None else acc + part

        @pl.when(s == 0)
        def _():
            acc_ref[...] = acc

        @pl.when(s > 0)
        def _():
            acc_ref[...] += acc

        @pl.when(s == ns - 1)
        def _():
            o_ref[...] = acc_ref[...].astype(o_ref.dtype)

    in_specs, args = [], []
    for (a, w, w_block, w_idx) in pairs:
        in_specs.append(pl.BlockSpec((tm, w_block[-1]), lambda i, s: (i, s)))
        in_specs.append(pl.BlockSpec(w_block, functools.partial(lambda i, s, f: f(s), f=w_idx)))
        args += [a, w]
    in_specs += [pl.BlockSpec(memory_space=pl.ANY)] * nx
    args += list(deps)
    return pl.pallas_call(
        body, name=name, grid=(t_rows // tm, ns), in_specs=in_specs,
        out_specs=pl.BlockSpec((tm, n_out), lambda i, s: (i, 0)),
        out_shape=jax.ShapeDtypeStruct((t_rows, n_out), out_dtype),
        scratch_shapes=[pltpu.VMEM((tm, n_out), F32)],
        compiler_params=_params(("arbitrary", "arbitrary")),
    )(*args)


def _mm_tn(name, a, b, bk, bn, n_p, n_q, out_shape, out_block, out_idx, into=None, a_off=0, b_off=0, tt=512,
           deps=()):
    t_rows = a.shape[0]
    tt = _tile(t_rows, tt, 16)
    nt = t_rows // tt
    has_into = into is not None
    nx = len(deps)

    def body(*refs):
        a_ref, b_ref = refs[0], refs[1]
        o_ref, acc_ref = refs[2 + has_into + nx], refs[3 + has_into + nx]
        t = pl.program_id(2)
        part = _dot_tn(a_ref[...], b_ref[...])

        @pl.when(t == 0)
        def _():
            acc_ref[...] = part

        @pl.when(t > 0)
        def _():
            acc_ref[...] += part

        @pl.when(t == nt - 1)
        def _():
            o_ref[...] = acc_ref[...].astype(o_ref.dtype)

    in_specs = [pl.BlockSpec((tt, bk), lambda p, q, t: (t, p + a_off)),
                pl.BlockSpec((tt, bn), lambda p, q, t: (t, q + b_off))]
    args = [a, b]
    aliases = {}
    if has_into:
        in_specs.append(pl.BlockSpec(memory_space=pl.ANY))
        args.append(into)
        aliases = {2: 0}
        out_shape = jax.ShapeDtypeStruct(into.shape, into.dtype)
    in_specs += [pl.BlockSpec(memory_space=pl.ANY)] * nx
    args += list(deps)
    return pl.pallas_call(
        body, name=name, grid=(n_p, n_q, nt), in_specs=in_specs,
        out_specs=pl.BlockSpec(out_block, lambda p, q, t: out_idx(p, q)),
        out_shape=out_shape, scratch_shapes=[pltpu.VMEM((bk, bn), F32)],
        input_output_aliases=aliases,
        compiler_params=_params(("arbitrary", "arbitrary", "arbitrary")),
    )(*args)


def _ffn_up(name, n, ga, slots, cols, d_model, fs, into=None, tm=512):
    t_rows = n.shape[0]
    tm = _tile(t_rows, tm)
    n_sh = slots.shape[0]
    has_into = into is not None

    def body(slot_ref, col_ref, n_ref, wg_ref, wu_ref, *refs):
        a_ref, b_ref, h_ref = refs[3 * has_into:]
        x = n_ref[...]
        a = jnp.dot(x, wg_ref[...], preferred_element_type=F32)
        b = jnp.dot(x, wu_ref[...], preferred_element_type=F32)
        a_ref[...] = a.astype(a_ref.dtype)
        b_ref[...] = b.astype(b_ref.dtype)
        h_ref[...] = (a * _sigmoid(a) * b).astype(h_ref.dtype)

    w_block = (None, None, d_model, fs)
    out = jax.ShapeDtypeStruct((t_rows, N_CHIPS * fs), BF16)
    in_specs = [pl.BlockSpec((tm, d_model), lambda s, i, sl, co: (i, 0)),
                pl.BlockSpec(w_block, lambda s, i, sl, co: (sl[s], 0, 0, 0)),
                pl.BlockSpec(w_block, lambda s, i, sl, co: (sl[s], 1, 0, 0))]
    args = [slots, cols, n, ga, ga]
    aliases = {}
    if has_into:
        in_specs += [pl.BlockSpec(memory_space=pl.ANY)] * 3
        args += list(into)
        aliases = {5: 0, 6: 1, 7: 2}
    return pl.pallas_call(
        body, name=name,
        grid_spec=pltpu.PrefetchScalarGridSpec(
            num_scalar_prefetch=2, grid=(n_sh, t_rows // tm), in_specs=in_specs,
            out_specs=[pl.BlockSpec((tm, fs), lambda s, i, sl, co: (i, co[s]))] * 3),
        out_shape=[out, out, out], input_output_aliases=aliases,
        compiler_params=_params(("arbitrary", "arbitrary")),
    )(*args)


def _swiglu_bwd(dh, a, b):
    a = a.astype(F32)
    b = b.astype(F32)
    sg = _sigmoid(a)
    return dh * b * sg * (1.0 + a * (1.0 - sg)), dh * a * sg


def _attn_fwd(q, k, v, tm=512):
    t_rows, d_model = q.shape
    n_mem = k.shape[0]
    hd = d_model // MEM_HEADS
    scale = hd ** -0.5
    tm = _tile(t_rows, tm)

    def body(q_ref, k_ref, v_ref, o_ref):
        for h in range(MEM_HEADS):
            cols = slice(h * hd, (h + 1) * hd)
            s = _dot_nt(q_ref[:, cols], k_ref[:, cols]) * scale
            s = s - jnp.max(s, axis=-1, keepdims=True)
            e = jnp.exp(s)
            p = e / jnp.sum(e, axis=-1, keepdims=True)
            o_ref[:, cols] = jnp.dot(p.astype(BF16), v_ref[:, cols], preferred_element_type=F32).astype(o_ref.dtype)

    return pl.pallas_call(
        body, name="attn_fwd", grid=(t_rows // tm,),
        in_specs=[pl.BlockSpec((tm, d_model), lambda i: (i, 0)),
                  pl.BlockSpec((n_mem, d_model), lambda i: (0, 0)),
                  pl.BlockSpec((n_mem, d_model), lambda i: (0, 0))],
        out_specs=pl.BlockSpec((tm, d_model), lambda i: (i, 0)),
        out_shape=jax.ShapeDtypeStruct((t_rows, d_model), BF16),
        compiler_params=_params(("arbitrary",)),
    )(q, k, v)


def _attn_bwd(q, k, v, do, tm=512):
    t_rows, d_model = q.shape
    n_mem = k.shape[0]
    hd = d_model // MEM_HEADS
    scale = hd ** -0.5
    tm = _tile(t_rows, tm, 16)

    def body(q_ref, k_ref, v_ref, do_ref, dq_ref, dk_ref, dv_ref):
        @pl.when(pl.program_id(0) == 0)
        def _():
            dk_ref[...] = jnp.zeros(dk_ref.shape, F32)
            dv_ref[...] = jnp.zeros(dv_ref.shape, F32)

        for h in range(MEM_HEADS):
            cols = slice(h * hd, (h + 1) * hd)
            qh, kh, vh, doh = q_ref[:, cols], k_ref[:, cols], v_ref[:, cols], do_ref[:, cols]
            s = _dot_nt(qh, kh) * scale
            s = s - jnp.max(s, axis=-1, keepdims=True)
            e = jnp.exp(s)
            p = e / jnp.sum(e, axis=-1, keepdims=True)
            dv_ref[:, cols] += _dot_tn(p.astype(BF16), doh)
            dp = _dot_nt(doh, vh)
            ds = (p * (dp - jnp.sum(dp * p, axis=-1, keepdims=True)) * scale).astype(BF16)
            dq_ref[:, cols] = jnp.dot(ds, kh, preferred_element_type=F32).astype(dq_ref.dtype)
            dk_ref[:, cols] += _dot_tn(ds, qh)

    full = pl.BlockSpec((n_mem, d_model), lambda i: (0, 0))
    tile = pl.BlockSpec((tm, d_model), lambda i: (i, 0))
    return pl.pallas_call(
        body, name="attn_bwd", grid=(t_rows // tm,),
        in_specs=[tile, full, full, tile], out_specs=[tile, full, full],
        out_shape=[jax.ShapeDtypeStruct((t_rows, d_model), BF16),
                   jax.ShapeDtypeStruct((n_mem, d_model), F32), jax.ShapeDtypeStruct((n_mem, d_model), F32)],
        compiler_params=_params(("arbitrary",)),
    )(q, k, v, do)


def _split_bf16(v):
    hi = v.astype(BF16)
    return hi, (v - hi.astype(F32)).astype(BF16)


def _pool_window(g):
    return jnp.left_shift(jnp.int32(POOL_WINDOWS[0]), g)


def _pool_fwd(u, col_off, w_pool, scale, tt=256):
    t_rows = u.shape[0]
    pw = w_pool.shape[-1]
    ng = w_pool.shape[0]
    tt = _tile(t_rows, tt, 16)
    nt = t_rows // tt
    assert POOL_WINDOWS == tuple(2 << i for i in range(ng)) and tt >= POOL_WINDOWS[-1]

    def body(vc_ref, vp_ref, w_ref, sc_ref, pooled_ref, z_ref):
        g, i = pl.program_id(0), pl.program_id(1)
        w = _pool_window(g)
        r = lax.broadcasted_iota(jnp.int32, (tt, tt), 0)
        c = lax.broadcasted_iota(jnp.int32, (tt, tt), 1)
        band_c = ((c <= r) & (c > r - w)).astype(BF16)
        band_p = (c > r - w + tt).astype(BF16)
        vc = vc_ref[...]
        ch, cl = _split_bf16(vc)
        ph, plo = _split_bf16(vp_ref[...] * (i > 0).astype(F32))
        sums = (jnp.dot(band_c, ch, preferred_element_type=F32) + jnp.dot(band_c, cl, preferred_element_type=F32)
                + jnp.dot(band_p, ph, preferred_element_type=F32) + jnp.dot(band_p, plo, preferred_element_type=F32))
        t = i * tt + lax.broadcasted_iota(jnp.int32, (tt, 1), 0)
        cnt = jnp.minimum(t + 1, w).astype(F32)
        pooled = (sums / cnt - vc).astype(BF16)
        pooled_ref[...] = pooled
        z_ref[...] = jnp.dot(pooled, w_ref[...], preferred_element_type=F32) * sc_ref[...]

    return pl.pallas_call(
        body, name="pool_fwd", grid=(ng, nt),
        in_specs=[pl.BlockSpec((tt, pw), lambda g, i: (i, col_off + g)),
                  pl.BlockSpec((tt, pw), lambda g, i: (jnp.maximum(i - 1, 0), col_off + g)),
                  pl.BlockSpec((None, pw, pw), lambda g, i: (g, 0, 0)),
                  pl.BlockSpec((1, pw), lambda g, i: (0, g))],
        out_specs=[pl.BlockSpec((tt, pw), lambda g, i: (i, g))] * 2,
        out_shape=[jax.ShapeDtypeStruct((t_rows, ng * pw), BF16), jax.ShapeDtypeStruct((t_rows, ng * pw), F32)],
        compiler_params=_params(("arbitrary", "arbitrary")),
    )(u, u, w_pool, scale)


def _pool_bwd1(dz, pooled, w_pool, scale, tt=256):
    t_rows = dz.shape[0]
    pw = w_pool.shape[-1]
    ng = w_pool.shape[0]
    tt = _tile(t_rows, tt, 16)
    nt = t_rows // tt

    def body(dz_ref, p_ref, w_ref, sc_ref, dp_ref, dw_ref, dsc_ref):
        g, i = pl.program_id(0), pl.program_id(1)
        w = _pool_window(g)

        @pl.when(i == 0)
        def _():
            dw_ref[...] = jnp.zeros(dw_ref.shape, F32)
            dsc_ref[...] = jnp.zeros(dsc_ref.shape, F32)

        dz_v = dz_ref[...]
        pooled = p_ref[...]
        zpre = jnp.dot(pooled, w_ref[...], preferred_element_type=F32)
        dsc_ref[...] += _colsum(dz_v * zpre)
        dzs = (dz_v * sc_ref[...]).astype(BF16)
        dw_ref[...] += _dot_tn(pooled, dzs)
        t = i * tt + lax.broadcasted_iota(jnp.int32, (tt, 1), 0)
        cnt = jnp.minimum(t + 1, w).astype(F32)
        dp_ref[...] = _dot_nt(dzs, w_ref[...]) / cnt

    return pl.pallas_call(
        body, name="pool_bwd1", grid=(ng, nt),
        in_specs=[pl.BlockSpec((tt, pw), lambda g, i: (i, g)),
                  pl.BlockSpec((tt, pw), lambda g, i: (i, g)),
                  pl.BlockSpec((None, pw, pw), lambda g, i: (g, 0, 0)),
                  pl.BlockSpec((1, pw), lambda g, i: (0, g))],
        out_specs=[pl.BlockSpec((tt, pw), lambda g, i: (i, g)),
                   pl.BlockSpec((None, pw, pw), lambda g, i: (g, 0, 0)),
                   pl.BlockSpec((1, pw), lambda g, i: (0, g))],
        out_shape=[jax.ShapeDtypeStruct((t_rows, ng * pw), F32), jax.ShapeDtypeStruct((ng, pw, pw), F32),
                   jax.ShapeDtypeStruct((1, ng * pw), F32)],
        compiler_params=_params(("arbitrary", "arbitrary")),
    )(dz, pooled, w_pool, scale)


def _pool_bwd2(dps, ng, tt=256):
    t_rows, width = dps.shape
    pw = width // ng
    tt = _tile(t_rows, tt, 16)
    nt = t_rows // tt

    def body(dc_ref, dn_ref, dv_ref):
        g, i = pl.program_id(0), pl.program_id(1)
        w = _pool_window(g)
        r = lax.broadcasted_iota(jnp.int32, (tt, tt), 0)
        c = lax.broadcasted_iota(jnp.int32, (tt, tt), 1)
        band_c = ((c >= r) & (c < r + w)).astype(BF16)
        band_n = (c < r + w - tt).astype(BF16)
        dc = dc_ref[...]
        ch, cl = _split_bf16(dc)
        nh, nl = _split_bf16(dn_ref[...] * (i < nt - 1).astype(F32))
        sums = (jnp.dot(band_c, ch, preferred_element_type=F32) + jnp.dot(band_c, cl, preferred_element_type=F32)
                + jnp.dot(band_n, nh, preferred_element_type=F32) + jnp.dot(band_n, nl, preferred_element_type=F32))
        t = i * tt + lax.broadcasted_iota(jnp.int32, (tt, 1), 0)
        cnt = jnp.minimum(t + 1, w).astype(F32)
        dv_ref[...] = sums - dc * cnt

    return pl.pallas_call(
        body, name="pool_bwd2", grid=(ng, nt),
        in_specs=[pl.BlockSpec((tt, pw), lambda g, i: (i, g)),
                  pl.BlockSpec((tt, pw), lambda g, i: (jnp.minimum(i + 1, nt - 1), g))],
        out_specs=pl.BlockSpec((tt, pw), lambda g, i: (i, g)),
        out_shape=jax.ShapeDtypeStruct((t_rows, width), F32),
        compiler_params=_params(("arbitrary", "arbitrary")),
    )(dps, dps)


def _cpow(ar, ai, n):
    rr, ri, br, bi = None, None, ar, ai
    while n:
        if n & 1:
            rr, ri = (br, bi) if rr is None else (rr * br - ri * bi, rr * bi + ri * br)
        n >>= 1
        if n:
            br, bi = br * br - bi * bi, 2.0 * br * bi
    return rr, ri


def _chunk_carries(st_re, st_im, pr, pi, order):
    cb = st_re.shape[1]
    sub = lax.broadcasted_iota(jnp.int32, (S5_LANES, cb), 0)
    cr = jnp.zeros((S5_LANES, cb), F32)
    ci = jnp.zeros((S5_LANES, cb), F32)
    prev_r = jnp.zeros((1, cb), F32)
    prev_i = jnp.zeros((1, cb), F32)
    for k, src in order:
        er, ei = st_re[src:src + 1, :], st_im[src:src + 1, :]
        nr = er + pr * prev_r - pi * prev_i
        ni = ei + pr * prev_i + pi * prev_r
        cr = jnp.where(sub == k, jnp.broadcast_to(nr, (S5_LANES, cb)), cr)
        ci = jnp.where(sub == k, jnp.broadcast_to(ni, (S5_LANES, cb)), ci)
        prev_r, prev_i = nr, ni
    return cr, ci


def _s5_fwd(up, bblk, ab, cblk, tt=128):
    n_rows, ws = up.shape
    nb, cw, cb2 = bblk.shape
    cb = cb2 // 2
    lc = n_rows // S5_LANES
    tt = _tile(lc, tt, 1)
    nt = lc // tt
    rt = S5_LANES * tt

    def body(u_ref, b_ref, ab_ref, c_ref, y_ref, s_ref, bu_ref, st_re, st_im):
        ps, ti = pl.program_id(1), pl.program_id(2)
        ar = jnp.broadcast_to(ab_ref[0:1, :], (S5_LANES, cb))
        ai = jnp.broadcast_to(ab_ref[1:2, :], (S5_LANES, cb))

        @pl.when((ps == 0) & (ti == 0))
        def _():
            st_re[...] = jnp.zeros(st_re.shape, F32)
            st_im[...] = jnp.zeros(st_im.shape, F32)

        @pl.when((ps == 1) & (ti == 0))
        def _():
            pr, pi = _cpow(ab_ref[0:1, :], ab_ref[1:2, :], lc)
            cr, ci = _chunk_carries(st_re, st_im, pr, pi, [(k, k - 1) for k in range(1, S5_LANES)])
            st_re[...] = cr
            st_im[...] = ci

        bu_ref[...] = jnp.dot(u_ref[...], b_ref[...], preferred_element_type=F32)

        def step(t, carry, store):
            sr, si = carry
            rows = pl.ds(pl.multiple_of(t * S5_LANES, S5_LANES), S5_LANES)
            nr = ar * sr - ai * si + bu_ref[rows, 0:cb]
            ni = ar * si + ai * sr + bu_ref[rows, cb:cb2]
            if store:
                s_ref[rows, 0:cb] = nr
                s_ref[rows, cb:cb2] = ni
            return nr, ni

        @pl.when(ps == 0)
        def _():
            sr, si = lax.fori_loop(0, tt, functools.partial(step, store=False), (st_re[...], st_im[...]))
            st_re[...] = sr
            st_im[...] = si

        @pl.when(ps == 1)
        def _():
            sr, si = lax.fori_loop(0, tt, functools.partial(step, store=True), (st_re[...], st_im[...]))
            st_re[...] = sr
            st_im[...] = si
            y_ref[...] = jnp.dot(s_ref[...].astype(BF16), c_ref[...], preferred_element_type=F32)

    return pl.pallas_call(
        body, name="s5_fwd", grid=(nb, 2, nt),
        in_specs=[pl.BlockSpec((rt, cw), lambda j, ps, ti: (ti, j)),
                  pl.BlockSpec((None, cw, cb2), lambda j, ps, ti: (j, 0, 0)),
                  pl.BlockSpec((None, 2, cb), lambda j, ps, ti: (j, 0, 0)),
                  pl.BlockSpec((None, cb2, cw), lambda j, ps, ti: (j, 0, 0))],
        out_specs=[pl.BlockSpec((rt, cw), lambda j, ps, ti: (ti * ps, j)),
                   pl.BlockSpec((None, rt, cb2), lambda j, ps, ti: (j, ti * ps, 0))],
        out_shape=[jax.ShapeDtypeStruct((n_rows, ws), F32), jax.ShapeDtypeStruct((nb, n_rows, cb2), F32)],
        scratch_shapes=[pltpu.VMEM((rt, cb2), F32), pltpu.VMEM((S5_LANES, cb), F32), pltpu.VMEM((S5_LANES, cb), F32)],
        compiler_params=_params(("arbitrary", "arbitrary", "arbitrary")),
    )(up, bblk, ab, cblk)


def _s5_bwd(dyp, up, s_all, bblk_t, ab, cblk_t, tt=128):
    n_rows, ws = up.shape
    nb, cb2, cw = bblk_t.shape
    cb = cb2 // 2
    lc = n_rows // S5_LANES
    tt = _tile(lc, tt, 1)
    nt = lc // tt
    rt = S5_LANES * tt

    def body(dy_ref, u_ref, s_ref, bt_ref, ab_ref, ct_ref, du_ref, db_ref, dc_ref, da_ref, ds_ref, st_re, st_im):
        ps, ti = pl.program_id(1), pl.program_id(2)
        ar = jnp.broadcast_to(ab_ref[0:1, :], (S5_LANES, cb))
        ai = jnp.broadcast_to(ab_ref[1:2, :], (S5_LANES, cb))

        @pl.when((ps == 0) & (ti == 0))
        def _():
            st_re[...] = jnp.zeros(st_re.shape, F32)
            st_im[...] = jnp.zeros(st_im.shape, F32)
            db_ref[...] = jnp.zeros(db_ref.shape, F32)
            dc_ref[...] = jnp.zeros(dc_ref.shape, F32)
            da_ref[...] = jnp.zeros(da_ref.shape, F32)

        @pl.when((ps == 1) & (ti == 0))
        def _():
            pr, pi = _cpow(ab_ref[0:1, :], -ab_ref[1:2, :], lc)
            cr, ci = _chunk_carries(st_re, st_im, pr, pi, [(k, k + 1) for k in range(S5_LANES - 2, -1, -1)])
            st_re[...] = cr
            st_im[...] = ci

        ds_ref[...] = jnp.dot(dy_ref[...], ct_ref[...], preferred_element_type=F32)

        def rows_of(i):
            return pl.ds(pl.multiple_of((tt - 1 - i) * S5_LANES, S5_LANES), S5_LANES)

        def step0(i, carry):
            gr, gi = carry
            rows = rows_of(i)
            return (ar * gr + ai * gi + ds_ref[rows, 0:cb], ar * gi - ai * gr + ds_ref[rows, cb:cb2])

        def step1(i, carry):
            gr, gi, acr, aci = carry
            rows = rows_of(i)
            sr, si = s_ref[rows, 0:cb], s_ref[rows, cb:cb2]
            acr = acr + sr * gr + si * gi
            aci = aci + sr * gi - si * gr
            nr = ar * gr + ai * gi + ds_ref[rows, 0:cb]
            ni = ar * gi - ai * gr + ds_ref[rows, cb:cb2]
            ds_ref[rows, 0:cb] = nr
            ds_ref[rows, cb:cb2] = ni
            return nr, ni, acr, aci

        @pl.when(ps == 0)
        def _():
            gr, gi = lax.fori_loop(0, tt, step0, (st_re[...], st_im[...]))
            st_re[...] = gr
            st_im[...] = gi

        @pl.when(ps == 1)
        def _():
            zero = jnp.zeros((S5_LANES, cb), F32)
            gr, gi, acr, aci = lax.fori_loop(0, tt, step1, (st_re[...], st_im[...], zero, zero))
            st_re[...] = gr
            st_im[...] = gi
            da_ref[0] += acr
            da_ref[1] += aci
            dsb = ds_ref[...].astype(BF16)
            du_ref[...] = jnp.dot(dsb, bt_ref[...], preferred_element_type=F32)
            db_ref[...] += _dot_tn(u_ref[...], dsb)
            dc_ref[...] += _dot_tn(dy_ref[...], s_ref[...].astype(BF16))

    def tile_idx(ps, ti):
        return (nt - 1 - ti) * ps + (nt - 1) * (1 - ps)

    return pl.pallas_call(
        body, name="s5_bwd", grid=(nb, 2, nt),
        in_specs=[pl.BlockSpec((rt, cw), lambda j, ps, ti: (nt - 1 - ti, j)),
                  pl.BlockSpec((rt, cw), lambda j, ps, ti: (tile_idx(ps, ti), j)),
                  pl.BlockSpec((None, rt, cb2), lambda j, ps, ti: (j, tile_idx(ps, ti), 0)),
                  pl.BlockSpec((None, cb2, cw), lambda j, ps, ti: (j, 0, 0)),
                  pl.BlockSpec((None, 2, cb), lambda j, ps, ti: (j, 0, 0)),
                  pl.BlockSpec((None, cw, cb2), lambda j, ps, ti: (j, 0, 0))],
        out_specs=[pl.BlockSpec((rt, cw), lambda j, ps, ti: (tile_idx(ps, ti), j)),
                   pl.BlockSpec((None, cw, cb2), lambda j, ps, ti: (j, 0, 0)),
                   pl.BlockSpec((None, cw, cb2), lambda j, ps, ti: (j, 0, 0)),
                   pl.BlockSpec((None, 2, S5_LANES, cb), lambda j, ps, ti: (j, 0, 0, 0))],
        out_shape=[jax.ShapeDtypeStruct((n_rows, ws), F32), jax.ShapeDtypeStruct((nb, cw, cb2), F32),
                   jax.ShapeDtypeStruct((nb, cw, cb2), F32), jax.ShapeDtypeStruct((nb, 2, S5_LANES, cb), F32)],
        scratch_shapes=[pltpu.VMEM((rt, cb2), F32), pltpu.VMEM((S5_LANES, cb), F32), pltpu.VMEM((S5_LANES, cb), F32)],
        compiler_params=_params(("arbitrary", "arbitrary", "arbitrary")),
    )(dyp, up, s_all, bblk_t, ab, cblk_t)


def _s5_discretize(a_re, a_im, log_dt, b_re, b_im):
    dt = jnp.exp(log_dt)[:, None]
    mag = jnp.exp(a_re * dt)
    abar_re = mag * jnp.cos(a_im * dt)
    abar_im = mag * jnp.sin(a_im * dt)
    nr, ni = abar_re - 1.0, abar_im
    den = a_re * a_re + a_im * a_im
    fr = (nr * a_re + ni * a_im) / den
    fi = (ni * a_re - nr * a_im) / den
    bbar_re = fr[..., None] * b_re - fi[..., None] * b_im
    bbar_im = fr[..., None] * b_im + fi[..., None] * b_re
    return abar_re, abar_im, bbar_re, bbar_im


def _perm_rows(a):
    n, c = a.shape
    return a.reshape(S5_LANES, n // S5_LANES, c).transpose(1, 0, 2).reshape(n, c)


def _unperm_rows(a):
    n, c = a.shape
    return a.reshape(n // S5_LANES, S5_LANES, c).transpose(1, 0, 2).reshape(n, c)


HBM = pl.BlockSpec(memory_space=pltpu.HBM)
SEM = pl.BlockSpec(memory_space=pltpu.SEMAPHORE)
ANY = pl.BlockSpec(memory_space=pl.ANY)
EFFECT = pltpu.SideEffectType.DATAFLOW_SIDE_EFFECTING
COPIES_PER_BUFFER = {"ag_ici": 3, "ag_fwd": 3, "pair": N_CHIPS, "scatter": 3, "half": 1, "swap": 1, "bcast": 3}
PAIRED_KINDS = ("pair", "scatter", "swap")


def _place():
    x, y, c = lax.axis_index("x"), lax.axis_index("y"), lax.axis_index("c")
    chips = [(1 - x, y), (x, 1 - y), (1 - x, 1 - y)]
    return x, y, c, 2 * x + y, chips


def _n_copies(kind, n_bufs):
    if isinstance(kind, tuple):
        return len(kind[1])
    return COPIES_PER_BUFFER[kind] * (n_bufs // 2 if kind in PAIRED_KINDS else n_bufs)


def _comm_copies(kind, bufs):
    if isinstance(kind, tuple):
        full = _comm_copies(kind[0], bufs)
        return [full[k] for k in kind[1]]
    x, y, c, s, chips = _place()
    sib = (x, y, 1 - c)
    out = []
    if kind == "ag_ici":
        for w in bufs:
            for cx, cy in chips:
                out.append((w.at[s, c], w.at[s, c], w.at[2 * cx + cy, c], (cx, cy, c)))
    elif kind == "ag_fwd":
        for w in bufs:
            for cx, cy in chips:
                sj = 2 * cx + cy
                out.append((w.at[sj, c], w.at[sj, c], w.at[sj, 1 - c], sib))
    elif kind == "pair":
        n = len(bufs) // 2
        for g, got in zip(bufs[:n], bufs[n:]):
            for t in range(N_CHIPS):
                out.append((g.at[t, 1 - c], got.at[t], got.at[t], sib))
    elif kind == "scatter":
        n = len(bufs) // 2
        for p, got in zip(bufs[:n], bufs[n:]):
            for cx, cy in chips:
                out.append((p.at[2 * cx + cy], got.at[s], got.at[2 * cx + cy], (cx, cy, c)))
    elif kind == "half":
        for f in bufs:
            out.append((f.at[c], f.at[c], f.at[1 - c], sib))
    elif kind == "swap":
        n = len(bufs) // 2
        for v, got in zip(bufs[:n], bufs[n:]):
            out.append((v, got, got, sib))
    elif kind == "bcast":
        for w in bufs:
            for cx, cy in chips:
                out.append((w.at[s], w.at[s], w.at[2 * cx + cy], (cx, cy, c)))
    return out


def _comm_fused(name, kind, bufs):
    n = len(bufs)
    ncp = _n_copies(kind, n)

    def body(*refs):
        outs = refs[n:2 * n]
        send, recv = refs[2 * n:]
        copies = _comm_copies(kind, outs)
        started = []
        for k, (src, dst, _, peer) in enumerate(copies):
            cp = pltpu.make_async_remote_copy(src_ref=src, dst_ref=dst, send_sem=send.at[k], recv_sem=recv.at[k],
                                              device_id=peer, device_id_type=MESH)
            cp.start()
            started.append(cp)
        for k, (_, _, land, peer) in enumerate(copies):
            pltpu.make_async_remote_copy(src_ref=land, dst_ref=land, send_sem=send.at[k], recv_sem=recv.at[k],
                                         device_id=peer, device_id_type=MESH).wait_recv()
        for cp in started:
            cp.wait_send()

    return pl.pallas_call(
        body, name=name, in_specs=[ANY] * n, out_specs=[ANY] * n,
        out_shape=[jax.ShapeDtypeStruct(b.shape, b.dtype) for b in bufs],
        input_output_aliases={k: k for k in range(n)},
        scratch_shapes=[pltpu.SemaphoreType.DMA((ncp,))] * 2,
    )(*bufs)


def _comm_start(name, kind, bufs, after=None):
    n = len(bufs)
    ncp = _n_copies(kind, n)
    nx = 0 if after is None else 1

    def body(*refs):
        refs = refs[n + nx:]
        send, recv = refs[:ncp], refs[ncp:2 * ncp]
        outs = refs[2 * ncp:n + 2 * ncp]
        token = refs[n + 2 * ncp]
        for k, (src, dst, _, peer) in enumerate(_comm_copies(kind, outs)):
            pltpu.make_async_remote_copy(src_ref=src, dst_ref=dst, send_sem=send[k], recv_sem=recv[k],
                                         device_id=peer, device_id_type=MESH).start()
        token[...] = jnp.zeros(token.shape, token.dtype)

    res = pl.pallas_call(
        body, name=name, in_specs=[HBM] * n + [ANY] * nx,
        out_specs=[SEM] * (2 * ncp) + [HBM] * n + [pl.BlockSpec(memory_space=pltpu.VMEM)],
        out_shape=[pltpu.SemaphoreType.DMA(())] * (2 * ncp) + [pltpu.HBM(b.shape, b.dtype) for b in bufs]
        + [jax.ShapeDtypeStruct((8, 128), F32)],
        input_output_aliases={k: 2 * ncp + k for k in range(n)},
        compiler_params=pltpu.CompilerParams(has_side_effects=EFFECT),
    )(*[pltpu.with_memory_space_constraint(b, pltpu.HBM) for b in bufs], *([after] if nx else []))
    return list(res[:ncp]), list(res[ncp:2 * ncp]), list(res[2 * ncp:2 * ncp + n]), res[2 * ncp + n]


def _comm_wait(name, kind, bufs, send_sems, recv_sems, after):
    n = len(bufs)
    ncp = _n_copies(kind, n)

    def body(*refs):
        send, recv = refs[n:n + ncp], refs[n + ncp:n + 2 * ncp]
        outs = refs[n + 2 * ncp + 1:]
        for k, (src, _, land, peer) in enumerate(_comm_copies(kind, outs)):
            cp = pltpu.make_async_remote_copy(src_ref=src, dst_ref=land, send_sem=send[k], recv_sem=recv[k],
                                              device_id=peer, device_id_type=MESH)
            cp.wait_send()
            cp.wait_recv()

    return pl.pallas_call(
        body, name=name, in_specs=[HBM] * n + [SEM] * (2 * ncp) + [ANY], out_specs=[HBM] * n,
        out_shape=[pltpu.HBM(b.shape, b.dtype) for b in bufs],
        input_output_aliases={k: k for k in range(n)},
        compiler_params=pltpu.CompilerParams(has_side_effects=EFFECT),
    )(*bufs, *send_sems, *recv_sems, after)


def _pair_add(g, got, core):
    nchip, _, r, cw = g.shape
    tr = _tile(r, 512, 16)

    def body(c_ref, a_ref, b_ref, o_ref):
        o_ref[...] = a_ref[...] + b_ref[...]

    return pl.pallas_call(
        body, name="grads_pair_add",
        grid_spec=pltpu.PrefetchScalarGridSpec(
            num_scalar_prefetch=1, grid=(nchip, r // tr),
            in_specs=[pl.BlockSpec((None, None, tr, cw), lambda s, i, c_ref: (s, c_ref[0], i, 0)),
                      pl.BlockSpec((None, tr, cw), lambda s, i, c_ref: (s, i, 0))],
            out_specs=pl.BlockSpec((None, tr, cw), lambda s, i, c_ref: (s, i, 0))),
        out_shape=jax.ShapeDtypeStruct((nchip, r, cw), BF16),
        compiler_params=_params(("arbitrary", "arbitrary")),
    )(core, g, got)


def _chip_sum(parts, got, idx):
    _, r, cw = parts.shape
    tr = _tile(r, 512, 16)

    def body(i_ref, own_ref, a_ref, b_ref, c_ref, o_ref):
        o_ref[...] = ((own_ref[...].astype(F32) + a_ref[...].astype(F32)) + b_ref[...].astype(F32)) + c_ref[...].astype(F32)

    def slot(k):
        return pl.BlockSpec((None, tr, cw), lambda i, i_ref: (i_ref[k], i, 0))

    return pl.pallas_call(
        body, name="grads_chip_sum",
        grid_spec=pltpu.PrefetchScalarGridSpec(
            num_scalar_prefetch=1, grid=(r // tr,), in_specs=[slot(0), slot(1), slot(2), slot(3)], out_specs=slot(4)),
        out_shape=jax.ShapeDtypeStruct((2, r, cw), F32),
        compiler_params=_params(("arbitrary",)),
    )(idx, parts, got, got, got)


def _add_into_slot(v, got, chip):
    r, cw = v.shape
    tr = _tile(r, 256)

    def body(c_ref, a_ref, b_ref, o_ref):
        o_ref[...] = a_ref[...] + b_ref[...]

    tile = pl.BlockSpec((tr, cw), lambda i, c_ref: (i, 0))
    return pl.pallas_call(
        body, name="small_pair_add",
        grid_spec=pltpu.PrefetchScalarGridSpec(
            num_scalar_prefetch=1, grid=(r // tr,), in_specs=[tile, tile],
            out_specs=pl.BlockSpec((None, tr, cw), lambda i, c_ref: (c_ref[0], i, 0))),
        out_shape=jax.ShapeDtypeStruct((N_CHIPS, r, cw), F32),
        compiler_params=_params(("arbitrary",)),
    )(chip, v, got)


def _sum_slots(w):
    _, r, cw = w.shape
    tr = _tile(r, 256)

    def body(w_ref, o_ref):
        o_ref[...] = ((w_ref[0] + w_ref[1]) + w_ref[2]) + w_ref[3]

    return pl.pallas_call(
        body, name="small_chip_sum", grid=(r // tr,),
        in_specs=[pl.BlockSpec((N_CHIPS, tr, cw), lambda i: (0, i, 0))],
        out_specs=pl.BlockSpec((tr, cw), lambda i: (i, 0)),
        out_shape=jax.ShapeDtypeStruct((r, cw), F32),
        compiler_params=_params(("arbitrary",)),
    )(w)


def _adamw_math(w, g, m, v):
    m = ADAM_B1 * m + (1.0 - ADAM_B1) * g
    v = ADAM_B2 * v + (1.0 - ADAM_B2) * (g * g)
    m_hat = m / (1.0 - ADAM_B1 ** ADAM_STEP)
    v_hat = v / (1.0 - ADAM_B2 ** ADAM_STEP)
    delta = -ADAM_LR * (m_hat / (jnp.sqrt(v_hat) + ADAM_EPS) + ADAM_WD * w)
    return delta, m, v


def _adamw(name, w, m, v, g, g_half=0, g_row_off=0, tr=256):
    r, cw = w.shape
    tr = _tile(math.gcd(r, g_row_off) if g_row_off else r, tr)
    off = g_row_off // tr

    def body(w_ref, m_ref, v_ref, g_ref, go_ref, d_ref, mo_ref, vo_ref):
        g_v = g_ref[...]
        delta, m_n, v_n = _adamw_math(w_ref[...], g_v, m_ref[...], v_ref[...])
        go_ref[...] = g_v
        d_ref[...] = delta
        mo_ref[...] = m_n
        vo_ref[...] = v_n

    tile = pl.BlockSpec((tr, cw), lambda i: (i, 0))
    out = jax.ShapeDtypeStruct((r, cw), F32)
    return pl.pallas_call(
        body, name=name, grid=(r // tr,),
        in_specs=[tile, tile, tile, pl.BlockSpec((None, tr, cw), lambda i: (g_half, i + off, 0))],
        out_specs=[tile] * 4, out_shape=[out] * 4,
        compiler_params=_params(("arbitrary",)),
    )(w, m, v, g)


def kernel(x, mem, g_ffn1, w1_gate, w1_up, w1_down, g_mix, w_in, ssm_a_re, ssm_a_im, ssm_log_dt, ssm_b_re, ssm_b_im, ssm_c_re, ssm_c_im, ssm_d, w_glu, b_glu, w_pool, pool_scale, g_out_ssm, g_out_pool, w_out, g_xattn, g_mem, w_q, w_k, w_v, w_o, g_ffn2, w2_gate, w2_up, w2_down, g_final, loss_target, m_g_ffn1, m_w1_gate, m_w1_up, m_w1_down, m_g_mix, m_w_in, m_ssm_a_re, m_ssm_a_im, m_ssm_log_dt, m_ssm_b_re, m_ssm_b_im, m_ssm_c_re, m_ssm_c_im, m_ssm_d, m_w_glu, m_b_glu, m_w_pool, m_pool_scale, m_g_out_ssm, m_g_out_pool, m_w_out, m_g_xattn, m_g_mem, m_w_q, m_w_k, m_w_v, m_w_o, m_g_ffn2, m_w2_gate, m_w2_up, m_w2_down, m_g_final, v_g_ffn1, v_w1_gate, v_w1_up, v_w1_down, v_g_mix, v_w_in, v_ssm_a_re, v_ssm_a_im, v_ssm_log_dt, v_ssm_b_re, v_ssm_b_im, v_ssm_c_re, v_ssm_c_im, v_ssm_d, v_w_glu, v_b_glu, v_w_pool, v_pool_scale, v_g_out_ssm, v_g_out_pool, v_w_out, v_g_xattn, v_g_mem, v_w_q, v_w_k, v_w_v, v_w_o, v_g_ffn2, v_w2_gate, v_w2_up, v_w2_down, v_g_final):
    local = dict(locals())
    wts = {n: local[n] for n in WEIGHTS}
    mom = {n: local["m_" + n] for n in WEIGHTS}
    var = {n: local["v_" + n] for n in WEIGHTS}

    x2 = x[0]
    mem2 = mem[0]
    tgt = loss_target[0]
    t_rows, d = x2.shape
    fs = w1_gate.shape[-1]
    ds_ = w_in.shape[1]
    ws = d // 2
    n_pg = len(POOL_WINDOWS)
    pw = ws // n_pg
    n_grp = ws // SSM_GROUP
    n_state = ssm_a_re.shape[-1]
    cx_, cy_, cc_ = lax.axis_index("x"), lax.axis_index("y"), lax.axis_index("c")
    chip = (2 * cx_ + cy_).astype(jnp.int32)
    core = cc_.astype(jnp.int32).reshape(1)
    chip_idx = jnp.stack([chip, chip ^ 2, chip ^ 1, chip ^ 3, cc_.astype(jnp.int32)])

    glu_rows = w_glu[0].reshape(-1, d)
    pool_rows = w_pool[0].reshape(-1, d)
    gh, ph = glu_rows.shape[0] // 2, pool_rows.shape[0] // 2
    rh = -(-(3 * ds_ + gh + ph) // 128) * 128
    pad_rows = jnp.zeros((rh - 3 * ds_ - gh - ph, d), F32)

    def own_slot(src):
        src = src.astype(BF16)
        return lax.dynamic_update_slice(lax.empty((N_CHIPS,) + src.shape, BF16), src[None], (chip, 0, 0, 0))

    src_b2 = jnp.stack([
        jnp.concatenate([w_in[0], w_out[0], w_q[0], glu_rows[:gh], pool_rows[:ph], pad_rows], 0),
        jnp.concatenate([w_k[0], w_v[0], w_o[0], glu_rows[gh:], pool_rows[ph:], pad_rows], 0)])
    src_up1 = jnp.stack([w1_gate[0], w1_up[0]]).astype(BF16)
    w_bufs = [own_slot(src_up1), own_slot(w1_down[0].reshape(2, fs // 2, d)),
              own_slot(src_b2), own_slot(jnp.stack([w2_gate[0], w2_up[0]])),
              own_slot(w2_down[0].reshape(2, fs // 2, d))]
    ag_send, ag_recv, w_bufs, ag_token = _comm_start("weights_start", "ag_ici", w_bufs)

    def gathered(k, after):
        w = _comm_wait("weights_wait_%d" % k, "ag_ici", [w_bufs[k]], ag_send[3 * k:3 * k + 3],
                       ag_recv[3 * k:3 * k + 3], after)
        return _comm_fused("weights_forward_%d" % k, "ag_fwd", w)[0]

    def gathered_start(k, after):
        w = _comm_wait("weights_wait_%d" % k, "ag_ici", [w_bufs[k]], ag_send[3 * k:3 * k + 3],
                       ag_recv[3 * k:3 * k + 3], after)
        send, recv, thru, token = _comm_start("weights_forward_start_%d" % k, "ag_fwd", w)
        return (send, recv, thru), token

    def gathered_finish(k, handle, after):
        send, recv, thru = handle
        return _comm_wait("weights_forward_wait_%d" % k, "ag_fwd", thru, send, recv, after)[0]

    n1 = _rmsnorm("norm_ffn1", x2, wts['g_ffn1'].reshape(1, -1), deps=[ag_token])
    zero1 = jnp.zeros((1,), jnp.int32)
    near, far = jnp.stack([chip ^ 2, chip ^ 1]), (chip ^ 3).reshape(1)
    a1, b1, hm1 = _ffn_up("ffn1_up_own", n1, src_up1[None], zero1, chip.reshape(1), d, fs)
    w0 = _comm_wait("weights_wait_0_near", ("ag_ici", (0, 1)), [w_bufs[0]], ag_send[0:2], ag_recv[0:2], hm1)
    w0 = _comm_fused("weights_forward_0_near", ("ag_fwd", (0, 1)), w0)
    a1, b1, hm1 = _ffn_up("ffn1_up_near", n1, w0[0], near, near, d, fs, into=(a1, b1, hm1))
    w0 = _comm_wait("weights_wait_0_far", ("ag_ici", (2,)), w0, ag_send[2:3], ag_recv[2:3], hm1)
    ga1 = _comm_fused("weights_forward_0_far", ("ag_fwd", (2,)), w0)[0]
    a1, b1, hm1 = _ffn_up("ffn1_up_far", n1, ga1, far, far, d, fs, into=(a1, b1, hm1))
    gd1 = gathered(1, hm1).reshape(N_CHIPS, fs, d)
    h1, n2 = _mm_nn("ffn1_down", hm1, gd1, (None, fs, d), lambda j, k: (k, 0, 0), N_CHIPS, d, F32, res=x2, alpha=0.5,
                    norm_g=wts['g_mix'].reshape(1, -1))
    gb2 = gathered(2, h1)
    wglu_full = gb2[:, :, 3 * ds_:3 * ds_ + gh, :].reshape(ws, ws)
    wpool_full = gb2[:, :, 3 * ds_ + gh:3 * ds_ + gh + ph, :].reshape(N_CHIPS, n_pg, pw // N_CHIPS, pw)
    wpool_full = wpool_full.transpose(1, 0, 2, 3).reshape(n_pg, pw, pw)
    DD = {'w_in': (0, 0), 'w_out': (0, 1), 'w_q': (0, 2), 'w_k': (1, 0), 'w_v': (1, 1), 'w_o': (1, 2)}

    def mm_dd(name, a, wname, out_dtype, res=None, norm_g=None):
        h, q = DD[wname]
        return _mm_nn(name, a, gb2, (None, None, ds_, d), lambda j, k: (k, h, q, 0), N_CHIPS, d, out_dtype, res=res,
                      norm_g=norm_g)

    def mm_dd_t(name, pairs, out_dtype, deps=()):
        ps = [(dy, gb2, (None, None, ds_, d), functools.partial(lambda s, h, q: (s, h, q, 0), h=DD[w][0], q=DD[w][1]))
              for dy, w in pairs]
        return _mm_nt_cols(name, ps, N_CHIPS, ds_, [out_dtype], deps=deps)[0]

    def vec(n):
        return wts[n].reshape(1, -1)

    disc_in = (ssm_a_re[0], ssm_a_im[0], ssm_log_dt[0], ssm_b_re[0], ssm_b_im[0])
    (abar_re, abar_im, bbar_re, bbar_im), disc_vjp = jax.vjp(_s5_discretize, *disc_in)
    gpb = min(S5_GROUPS_PER_BLOCK, n_grp)
    nb = n_grp // gpb
    cb = gpb * n_state
    eye = jnp.eye(gpb, dtype=F32)

    def blockdiag(t):
        return jnp.einsum('jgph,gk->jghkp', t.reshape(nb, gpb, n_state, SSM_GROUP), eye).reshape(nb, gpb * SSM_GROUP, cb)

    def blockdiag_c(t):
        return jnp.einsum('jghp,gk->jkpgh', t.reshape(nb, gpb, SSM_GROUP, n_state), eye).reshape(nb, cb, gpb * SSM_GROUP)

    bblk = jnp.concatenate([blockdiag(bbar_re), blockdiag(bbar_im)], -1).astype(BF16)
    cblk = jnp.concatenate([blockdiag_c(ssm_c_re[0]), -blockdiag_c(ssm_c_im[0])], 1).astype(BF16)
    ab = jnp.stack([abar_re.reshape(nb, cb), abar_im.reshape(nb, cb)], 1)

    u = mm_dd("mix_in", n2, 'w_in', F32)

    up = _perm_rows(u[:, :ws]).astype(BF16)
    ylin_p, s_all = _s5_fwd(up, bblk, ab, cblk)
    ylin = _unperm_rows(ylin_p)

    def gelu_fn(r, v):
        y1 = r[0] + v[0] * r[1]
        y2 = jax.nn.gelu(y1)
        return [y2, y2], []
    fwd_a2, tok = gathered_start(3, ylin_p)
    y2, y2b = _rowwise("s5_gelu", gelu_fn, [ylin, (u, 0, ws)], [vec('ssm_d')], [(ws, F32), (ws, BF16)], deps=[tok])
    z = _mm_nn("s5_glu", y2b, wglu_full, (ws, ws), lambda j, k: (0, 0), 1, ws, F32)

    def glu_fn(r, v):
        y3 = r[0] * _sigmoid(r[1] + v[0])
        return [_rms_fwd(y3, v[1])], []
    m_ssm = _rowwise("s5_gate_norm", glu_fn, [y2, z], [vec('b_glu'), vec('g_out_ssm')], [(ws, BF16)])[0]

    pooled, zp = _pool_fwd(u, ws // pw, wpool_full, vec('pool_scale'))
    fwd_d2, tok = gathered_start(4, zp)
    m_pool = _rmsnorm("norm_pool", zp, vec('g_out_pool'), deps=[tok])
    merged = jnp.concatenate([m_ssm, m_pool], -1)
    h2, hn = mm_dd("mix_out", merged, 'w_out', F32, res=h1, norm_g=vec('g_xattn'))

    memn = _rmsnorm("norm_mem", mem2, vec('g_mem'))
    k_mem = mm_dd("attn_k", memn, 'w_k', BF16)
    v_mem = mm_dd("attn_v", memn, 'w_v', BF16)
    q = mm_dd("attn_q", hn, 'w_q', BF16)
    o = _attn_fwd(q, k_mem, v_mem)
    h3, n4 = mm_dd("attn_out", o, 'w_o', F32, res=h2, norm_g=vec('g_ffn2'))

    ga2 = gathered_finish(3, fwd_a2, h3)
    all_chips = jnp.arange(N_CHIPS, dtype=jnp.int32)
    a2, b2, hm2 = _ffn_up("ffn2_up", n4, ga2, all_chips, all_chips, d, fs)
    gd2 = gathered_finish(4, fwd_d2, hm2).reshape(N_CHIPS, fs, d)
    h4 = _mm_nn("ffn2_down", hm2, gd2, (None, fs, d), lambda j, k: (k, 0, 0), N_CHIPS, d, F32, res=h3, alpha=0.5)

    def loss_fn(r, v):
        h, t = r
        e = _rms_fwd(h, v[0]) - t
        dy = e * (1.0 / d)
        dh, dg = _rms_bwd(dy, h, v[0])
        part = jnp.sum(_colsum(e * e), axis=1, keepdims=True) * (0.5 / d)
        return [dh, 0.5 * dh], [_colsum(dg), jnp.broadcast_to(part, (1, 128))]
    dh4, dy_f2, dg_final, loss_row = _rowwise("loss_head", loss_fn, [h4, tgt], [g_final.reshape(1, -1)],
                                              [(d, F32), (d, BF16)], [d, 128])

    def rs_pair_start(tag, gbufs, after=None):
        land = [lax.empty((N_CHIPS,) + g.shape[2:], BF16) for g in gbufs]
        send, recv, thru, token = _comm_start(tag + "_pair_start", "pair", list(gbufs) + land, after=after)
        return (send, recv, thru), token

    def rs_scatter_start(tag, handle, after):
        send, recv, thru = handle
        n = len(thru) // 2
        res = _comm_wait(tag + "_pair_wait", "pair", thru, send, recv, after)
        parts = [_pair_add(g, r, core) for g, r in zip(res[:n], res[n:])]
        land = [lax.empty(p.shape, BF16) for p in parts]
        send, recv, thru, token = _comm_start(tag + "_scatter_start", "scatter", parts + land)
        return (send, recv, thru), token

    def rs_half_start(tag, handle, after):
        send, recv, thru = handle
        n = len(thru) // 2
        res = _comm_wait(tag + "_scatter_wait", "scatter", thru, send, recv, after)
        full = [_chip_sum(p, g2, chip_idx) for p, g2 in zip(res[:n], res[n:])]
        send, recv, thru, token = _comm_start(tag + "_half_start", "half", full)
        return (send, recv, thru), token

    def rs_finish(tag, handle, after):
        send, recv, thru = handle
        return _comm_wait(tag + "_half_wait", "half", thru, send, recv, after)

    wblk = (None, None, d, fs)

    def ffn_down_bwd(tag, dy_half, a, b, hm, gd_l, deps=()):
        da, db = _mm_nt_cols(tag + "_down_bwd", [(dy_half, gd_l, (None, fs, d), lambda s: (s, 0, 0))],
                             N_CHIPS, fs, [BF16, BF16], epi=_swiglu_bwd, extras=[a, b], deps=deps)
        g_down = _mm_tn(tag + "_dw_down", hm, dy_half, fs, d // 2, N_CHIPS, 2, jax.ShapeDtypeStruct((N_CHIPS, fs, d), BF16),
                        (None, fs, d // 2), lambda p, q: (p, 0, q), tt=2048)
        return da, db, g_down.reshape(N_CHIPS, 2, fs // 2, d)

    def ffn_up_bwd(tag, da, db, ga_l, deps=()):
        return _mm_nt_k(tag + "_up_bwd", [(da, ga_l, wblk, lambda s: (s, 0, 0, 0)), (db, ga_l, wblk, lambda s: (s, 1, 0, 0))],
                        N_CHIPS, d, F32, deps=deps)

    def ffn_dw(name, dact, n_in, deps=()):
        return _mm_tn(name, n_in, dact, d // 2, fs, 2, N_CHIPS, jax.ShapeDtypeStruct((N_CHIPS, 2, d // 2, fs), BF16),
                      (None, None, d // 2, fs), lambda p, q: (q, p, 0, 0), tt=2048, deps=deps)

    def dw_dd(name, a, dy, wname, grad_b2):
        h, q = DD[wname]
        return _mm_tn(name, a, dy, ds_, d, N_CHIPS, 1, jax.ShapeDtypeStruct((N_CHIPS, 2, rh, d), BF16),
                      (None, None, ds_, d), lambda p, qq: (p, h, q, 0), into=grad_b2, tt=2048)

    def norm_bwd(name, dn, h, gname, dres, deps=(), scale=1.0):
        def fn(r, v):
            dx, dg = _rms_bwd(r[0], r[1], v[0])
            tot = dx + r[2]
            return [tot, scale * tot], [_colsum(dg)]
        return _rowwise(name, fn, [dn, h, dres], [vec(gname)], [(d, F32), (d, BF16)], [d], deps=deps)

    da2, db2, g_down2 = ffn_down_bwd("ffn2", dy_f2, a2, b2, hm2, gd2)
    dn4 = ffn_up_bwd("ffn2", da2, db2, ga2)
    g_gate2 = ffn_dw("ffn2_dw_gate", da2, n4)
    g_up2 = ffn_dw("ffn2_dw_up", db2, n4)
    rs_f2, tok = rs_pair_start("ffn2", [g_gate2, g_up2, g_down2])
    dh3, dh3b, dg_ffn2 = norm_bwd("norm_ffn2_bwd", dn4, h3, 'g_ffn2', dh4, deps=[tok])
    rs_f2, tok = rs_scatter_start("ffn2", rs_f2, dh3b)

    do = mm_dd_t("attn_out_bwd", [(dh3b, 'w_o')], BF16, deps=[tok])
    grad_b2 = dw_dd("attn_dw_o", o, dh3b, 'w_o', None)
    dq, dk, dv = _attn_bwd(q, k_mem, v_mem, do)
    dkb, dvb = dk.astype(BF16), dv.astype(BF16)
    grad_b2 = dw_dd("attn_dw_q", hn, dq, 'w_q', grad_b2)
    dhn = mm_dd_t("attn_q_bwd", [(dq, 'w_q')], F32)
    dh2, dh2b, dg_xattn = norm_bwd("norm_xattn_bwd", dhn, h2, 'g_xattn', dh3)
    grad_b2 = dw_dd("attn_dw_k", memn, dkb, 'w_k', grad_b2)
    grad_b2 = dw_dd("attn_dw_v", memn, dvb, 'w_v', grad_b2)
    dmemn = mm_dd_t("attn_kv_bwd", [(dkb, 'w_k'), (dvb, 'w_v')], F32)
    dg_mem = _rowwise("norm_mem_bwd", lambda r, v: ([], [_colsum(_rms_bwd(r[0], r[1], v[0])[1])]),
                      [dmemn, mem2], [vec('g_mem')], [], [d])[0]

    dmerged = mm_dd_t("mix_out_bwd", [(dh2b, 'w_out')], F32)
    grad_b2 = dw_dd("mix_dw_out", merged, dh2b, 'w_out', grad_b2)

    def gate_bwd_fn(r, v):
        dm, y2_v, z_v = r
        sg = _sigmoid(z_v + v[0])
        y3 = y2_v * sg
        dy3, dg = _rms_bwd(dm, y3, v[1])
        dz = dy3 * y3 * (1.0 - sg)
        return [dy3 * sg, dz], [_colsum(dg), _colsum(dz)]
    dy2a, dzb, dg_out_ssm, db_glu = _rowwise("s5_gate_norm_bwd", gate_bwd_fn, [(dmerged, 0, ws), y2, z],
                                             [vec('b_glu'), vec('g_out_ssm')], [(ws, F32), (ws, BF16)], [ws, ws])
    dy2b_ = _mm_nt_cols("s5_glu_bwd", [(dzb, wglu_full, (ws, ws), lambda s: (0, 0))], 1, ws, [F32])[0]
    dw_glu = _mm_tn("s5_dw_glu", y2b, dzb, ws, ws, 1, 1, jax.ShapeDtypeStruct((ws, ws), F32), (ws, ws), lambda p, q: (0, 0))

    def gelu_bwd_fn(r, v):
        dy2 = r[0] + r[1]
        us = r[3]
        y1 = r[2] + v[0] * us
        kk = math.sqrt(2.0 / math.pi)
        th = jnp.tanh(kk * (y1 + 0.044715 * y1 * y1 * y1))
        dgelu = 0.5 * (1.0 + th) + 0.5 * y1 * (1.0 - th * th) * kk * (1.0 + 3.0 * 0.044715 * y1 * y1)
        dy1 = dy2 * dgelu
        return [dy1, dy1 * v[0]], [_colsum(dy1 * us)]
    dy1b, du_skip, d_ssm_d = _rowwise("s5_gelu_bwd", gelu_bwd_fn, [dy2a, dy2b_, ylin, (u, 0, ws)], [vec('ssm_d')],
                                      [(ws, BF16), (ws, F32)], [ws])

    bblk_t = jnp.swapaxes(bblk, 1, 2)
    cblk_t = jnp.swapaxes(cblk, 1, 2)
    du_p, d_bblk, d_cblk_t, d_ab = _s5_bwd(_perm_rows(dy1b), up, s_all, bblk_t, ab, cblk_t)
    du_ssm = _unperm_rows(du_p)

    dzp, dg_out_pool = _rowwise("norm_pool_bwd", lambda r, v: (lambda dx, dg: ([dx], [_colsum(dg)]))(*_rms_bwd(r[0], r[1], v[0])),
                                [(dmerged, 1, ws), zp], [vec('g_out_pool')], [(ws, F32)], [ws])
    dps, dw_pool, d_pool_scale = _pool_bwd1(dzp, pooled, wpool_full, vec('pool_scale'))
    du_pool = _pool_bwd2(dps, n_pg)

    dub = _rowwise("mix_du", lambda r, v: ([jnp.concatenate([r[0] + r[1], r[2]], -1)], []),
                   [du_ssm, du_skip, du_pool], [], [(d, BF16)])[0]
    dn2 = mm_dd_t("mix_in_bwd", [(dub, 'w_in')], F32)
    grad_b2 = dw_dd("mix_dw_in", n2, dub, 'w_in', grad_b2)
    glu_g = dw_glu.reshape(N_CHIPS, 2, gh, d).astype(BF16)
    pool_g = dw_pool.reshape(n_pg, N_CHIPS, pw // N_CHIPS, pw).transpose(1, 0, 2, 3).reshape(N_CHIPS, 2, ph, d)
    pool_g = jnp.concatenate([pool_g, jnp.zeros((N_CHIPS, 2, rh - 3 * ds_ - gh - ph, d), F32)], 2).astype(BF16)
    grad_b2 = lax.dynamic_update_slice(grad_b2, glu_g, (0, 0, 3 * ds_, 0))
    grad_b2 = lax.dynamic_update_slice(grad_b2, pool_g, (0, 0, 3 * ds_ + gh, 0))
    rs_mix, tok = rs_pair_start("mixers", [grad_b2])
    dh1, dy_f1, dg_mix = norm_bwd("norm_mix_bwd", dn2, h1, 'g_mix', dh2, deps=[tok], scale=0.5)
    rs_mix, tok = rs_scatter_start("mixers", rs_mix, dy_f1)

    da1, db1, g_down1 = ffn_down_bwd("ffn1", dy_f1, a1, b1, hm1, gd1, deps=[tok])
    rs_d1, tok = rs_pair_start("ffn1_down", [g_down1])
    dn1 = ffn_up_bwd("ffn1", da1, db1, ga1, deps=[tok])
    rs_d1, tok = rs_scatter_start("ffn1_down", rs_d1, dn1)
    grad_x, _, dg_ffn1 = norm_bwd("norm_ffn1_bwd", dn1, x2, 'g_ffn1', dh1, deps=[tok])

    def undiag(t):
        return jnp.einsum('jghkp,gk->jgph', t.reshape(nb, gpb, SSM_GROUP, gpb, n_state), eye).reshape(n_grp, n_state, SSM_GROUP)

    d_bbar_re, d_bbar_im = undiag(d_bblk[:, :, :cb]), undiag(d_bblk[:, :, cb:])
    d_c_re = undiag(d_cblk_t[:, :, :cb]).transpose(0, 2, 1)
    d_c_im = -undiag(d_cblk_t[:, :, cb:]).transpose(0, 2, 1)
    d_abar = jnp.sum(d_ab, axis=2).reshape(nb, 2, gpb, n_state)
    d_abar_re = d_abar[:, 0].reshape(n_grp, n_state)
    d_abar_im = d_abar[:, 1].reshape(n_grp, n_state)
    d_a_re, d_a_im, d_log_dt, d_b_re, d_b_im = disc_vjp((d_abar_re, d_abar_im, d_bbar_re, d_bbar_im))

    small_g = {'g_ffn1': dg_ffn1, 'g_mix': dg_mix, 'ssm_a_re': d_a_re, 'ssm_a_im': d_a_im, 'ssm_log_dt': d_log_dt,
               'ssm_b_re': d_b_re, 'ssm_b_im': d_b_im, 'ssm_c_re': d_c_re, 'ssm_c_im': d_c_im, 'ssm_d': d_ssm_d,
               'b_glu': db_glu, 'pool_scale': d_pool_scale, 'g_out_ssm': dg_out_ssm, 'g_out_pool': dg_out_pool,
               'g_xattn': dg_xattn, 'g_mem': dg_mem, 'g_ffn2': dg_ffn2, 'g_final': dg_final}
    sizes = [wts[n].size for n in SMALL]
    total = sum(sizes) + 128
    rows_s = -(-total // (128 * 256)) * 256
    flat = jnp.concatenate([small_g[n].reshape(-1) for n in SMALL] + [loss_row.reshape(-1)])
    flat = jnp.pad(flat, (0, rows_s * 128 - total)).reshape(rows_s, 128)
    sw_send, sw_recv, sw_thru, tok = _comm_start("small_swap_start", "swap", [flat, lax.empty(flat.shape, F32)])
    g_gate1 = ffn_dw("ffn1_dw_gate", da1, n1, deps=[tok])
    rs_g1, tok_g1 = rs_pair_start("ffn1_gate", [g_gate1])
    sw_v, sw_got = _comm_wait("small_swap_wait", "swap", sw_thru, sw_send, sw_recv, tok_g1)
    slots = _add_into_slot(sw_v, sw_got, chip.reshape(1))
    bc_send, bc_recv, bc_thru, tok = _comm_start("small_bcast_start", "bcast", [slots])
    g_up1 = ffn_dw("ffn1_dw_up", db1, n1, deps=[tok])
    rs_g1, tok = rs_scatter_start("ffn1_gate", rs_g1, g_up1)
    slots, = _comm_wait("small_bcast_wait", "bcast", bc_thru, bc_send, bc_recv, tok)
    red = _sum_slots(slots).reshape(-1)
    loss = red[sum(sizes)]

    def flat_small(t):
        return jnp.pad(jnp.concatenate([t[n].reshape(-1) for n in SMALL]), (0, rows_s * 128 - sum(sizes))).reshape(rows_s, 128)
    sg_, sd_, sm_, sv_ = _adamw("adamw_small", flat_small(wts), flat_small(mom), flat_small(var), red.reshape(1, rows_s, 128))
    out = {}
    off = 0
    for n, sz in zip(SMALL, sizes):
        for key, arr in (('grad', sg_), ('delta', sd_), ('m', sm_), ('v', sv_)):
            out[key, n] = arr.reshape(-1)[off:off + sz].reshape(wts[n].shape)
        off += sz

    def upd(n, g_arr, half, row_off, shape2):
        res = _adamw("adamw_" + n, wts[n].reshape(shape2), mom[n].reshape(shape2), var[n].reshape(shape2), g_arr, half, row_off)
        for key, arr in zip(('grad', 'delta', 'm', 'v'), res):
            out[key, n] = arr.reshape(wts[n].shape)
        return res[3]

    rs_u1, tok = rs_pair_start("ffn1_up", [g_up1], after=sv_)
    rs_f2, tok = rs_half_start("ffn2", rs_f2, tok)
    rs_u1, tok = rs_scatter_start("ffn1_up", rs_u1, tok)
    rs_mix, tok = rs_half_start("mixers", rs_mix, tok)
    full_gate2, full_up2, full_down2 = rs_finish("ffn2", rs_f2, tok)
    upd('w2_gate', full_gate2.reshape(1, d, fs), 0, 0, (d, fs))
    upd('w2_up', full_up2.reshape(1, d, fs), 0, 0, (d, fs))
    last = upd('w2_down', full_down2.reshape(1, fs, d), 0, 0, (fs, d))
    rs_d1, tok = rs_half_start("ffn1_down", rs_d1, last)
    full_b2, = rs_finish("mixers", rs_mix, tok)
    for n, (h, q) in DD.items():
        last = upd(n, full_b2, h, q * ds_, (ds_, d))
    glu_shape, pool_shape = (ws // N_CHIPS, ws), (n_pg * pw // N_CHIPS, pw)
    upd('w_glu', full_b2[:, 3 * ds_:3 * ds_ + gh].reshape((1,) + glu_shape), 0, 0, glu_shape)
    upd('w_pool', full_b2[:, 3 * ds_ + gh:3 * ds_ + gh + ph].reshape((1,) + pool_shape), 0, 0, pool_shape)
    full_down1, = rs_finish("ffn1_down", rs_d1, last)
    last = upd('w1_down', full_down1.reshape(1, fs, d), 0, 0, (fs, d))
    rs_g1, tok = rs_half_start("ffn1_gate", rs_g1, last)
    rs_u1, tok = rs_half_start("ffn1_up", rs_u1, tok)
    full_gate1, = rs_finish("ffn1_gate", rs_g1, tok)
    last = upd('w1_gate', full_gate1.reshape(1, d, fs), 0, 0, (d, fs))
    full_up1, = rs_finish("ffn1_up", rs_u1, last)
    upd('w1_up', full_up1.reshape(1, d, fs), 0, 0, (d, fs))

    return (loss, grad_x[None], *[out['grad', n] for n in WEIGHTS], *[out['delta', n] for n in WEIGHTS],
            *[out['m', n] for n in WEIGHTS], *[out['v', n] for n in WEIGHTS])
```

```python
import functools
import math

import jax
import jax.numpy as jnp
from jax import lax
from jax.experimental import pallas as pl
from jax.experimental.pallas import tpu as pltpu

F32 = jnp.float32
BF16 = jnp.bfloat16
EPS = 1e-6
ADAM_LR, ADAM_B1, ADAM_B2, ADAM_EPS, ADAM_WD, ADAM_STEP = 0.001, 0.9, 0.999, 1e-08, 0.01, 10
POOL_WINDOWS = (2, 4, 8, 16)
SSM_GROUP = 16
S5_GROUPS_PER_BLOCK = 16
S5_LANES = 8
MEM_HEADS = 4
N_CHIPS = 4
VMEM_LIMIT_V7X = 56 * 1024 * 1024
MESH = pl.DeviceIdType.MESH

WEIGHTS = ['g_ffn1', 'w1_gate', 'w1_up', 'w1_down', 'g_mix', 'w_in', 'ssm_a_re', 'ssm_a_im', 'ssm_log_dt',
           'ssm_b_re', 'ssm_b_im', 'ssm_c_re', 'ssm_c_im', 'ssm_d', 'w_glu', 'b_glu', 'w_pool', 'pool_scale',
           'g_out_ssm', 'g_out_pool', 'w_out', 'g_xattn', 'g_mem', 'w_q', 'w_k', 'w_v', 'w_o', 'g_ffn2',
           'w2_gate', 'w2_up', 'w2_down', 'g_final']
BIG = ['w1_gate', 'w1_up', 'w1_down', 'w_in', 'w_glu', 'w_pool', 'w_out', 'w_q', 'w_k', 'w_v', 'w_o',
       'w2_gate', 'w2_up', 'w2_down']
SMALL = [n for n in WEIGHTS if n not in BIG]


def _tile(n, target, mult=8):
    best = None
    for d in range(1, n + 1):
        if n % d == 0 and d <= target and d % mult == 0:
            best = d
    return best if best is not None else n


def _params(sem=None):
    if sem is None:
        return pltpu.CompilerParams(vmem_limit_bytes=VMEM_LIMIT_V7X)
    return pltpu.CompilerParams(dimension_semantics=sem, vmem_limit_bytes=VMEM_LIMIT_V7X)


def _sigmoid(x):
    return 1.0 / (1.0 + jnp.exp(-x))


def _rms_fwd(x, g):
    r = lax.rsqrt(jnp.mean(x * x, axis=-1, keepdims=True) + EPS)
    return x * r * g


def _rms_bwd(dy, x, g):
    r = lax.rsqrt(jnp.mean(x * x, axis=-1, keepdims=True) + EPS)
    dxh = dy * g
    dx = r * dxh - x * (r * r * r) * jnp.mean(dxh * x, axis=-1, keepdims=True)
    return dx, dy * x * r


def _colsum(v):
    return jnp.sum(v, axis=0, keepdims=True)


def _rowwise(name, fn, rows, vecs, out_defs, red_defs=(), tm=256, deps=()):
    rows = [r if isinstance(r, tuple) else (r, 0, r.shape[1]) for r in rows]
    t_rows = rows[0][0].shape[0]
    tm = _tile(t_rows, tm)
    nr, nv, no, nd, nx = len(rows), len(vecs), len(out_defs), len(red_defs), len(deps)

    def body(*refs):
        r, v = refs[:nr], refs[nr:nr + nv]
        o, d = refs[nr + nv + nx:nr + nv + nx + no], refs[nr + nv + nx + no:]
        outs, reds = fn([x[...] for x in r], [x[...] for x in v])
        for ref, val in zip(o, outs):
            ref[...] = val.astype(ref.dtype)
        if nd:
            @pl.when(pl.program_id(0) == 0)
            def _():
                for ref in d:
                    ref[...] = jnp.zeros(ref.shape, ref.dtype)
            for ref, val in zip(d, reds):
                ref[...] += val

    in_specs = [pl.BlockSpec((tm, w), functools.partial(lambda i, cb: (i, cb), cb=cb)) for (_, cb, w) in rows]
    in_specs += [pl.BlockSpec(v.shape, lambda i: (0, 0)) for v in vecs]
    in_specs += [pl.BlockSpec(memory_space=pl.ANY)] * nx
    out_specs = [pl.BlockSpec((tm, w), lambda i: (i, 0)) for (w, _) in out_defs]
    out_specs += [pl.BlockSpec((1, w), lambda i: (0, 0)) for w in red_defs]
    out_shape = [jax.ShapeDtypeStruct((t_rows, w), dt) for (w, dt) in out_defs]
    out_shape += [jax.ShapeDtypeStruct((1, w), F32) for w in red_defs]
    res = pl.pallas_call(
        body, name=name, grid=(t_rows // tm,), in_specs=in_specs, out_specs=out_specs, out_shape=out_shape,
        compiler_params=_params(("arbitrary",)),
    )(*[r[0] for r in rows], *vecs, *deps)
    return res


def _rmsnorm(name, x, g, tm=256, deps=()):
    return _rowwise(name, lambda r, v: ([_rms_fwd(r[0].astype(F32), v[0])], []), [x], [g],
                    [(x.shape[1], BF16)], tm=tm, deps=deps)[0]


def _mm_nn(name, a, b, b_block, b_idx, nk, n_out, out_dtype, res=None, alpha=1.0, norm_g=None, tm=512, slabs=1):
    t_rows = a.shape[0]
    bk, tn = slabs * b_block[-2], b_block[-1]
    tm = _tile(t_rows, tm)
    nj = n_out // tn
    has_res = res is not None
    has_norm = norm_g is not None
    assert not has_norm or nj == 1

    def body(*refs):
        a_ref, b_ref = refs[0], refs[1]
        res_ref = refs[2] if has_res else None
        g_ref = refs[2 + has_res] if has_norm else None
        o_ref = refs[2 + has_res + has_norm]
        n_ref = refs[3 + has_res + has_norm] if has_norm else None
        k = pl.program_id(2)
        w = b_ref[...].reshape(bk, tn) if slabs > 1 else b_ref[...]
        p = jnp.dot(a_ref[...], w, preferred_element_type=F32)

        def finish(r):
            if has_res:
                r = res_ref[...] + alpha * r
            o_ref[...] = r.astype(o_ref.dtype)
            if has_norm:
                n_ref[...] = _rms_fwd(r, g_ref[...]).astype(n_ref.dtype)

        if nk == 1:
            finish(p)
            return
        acc_ref = refs[3 + has_res + 2 * has_norm]

        @pl.when(k == 0)
        def _():
            acc_ref[...] = p

        @pl.when(k > 0)
        def _():
            acc_ref[...] += p

        @pl.when(k == nk - 1)
        def _():
            finish(acc_ref[...])

    in_specs = [pl.BlockSpec((tm, bk), lambda j, i, k: (i, k)),
                pl.BlockSpec(b_block, lambda j, i, k: b_idx(j, k))]
    args = [a, b]
    if has_res:
        in_specs.append(pl.BlockSpec((tm, tn), lambda j, i, k: (i, j)))
        args.append(res)
    tile = pl.BlockSpec((tm, tn), lambda j, i, k: (i, j))
    out_specs, out_shape = tile, jax.ShapeDtypeStruct((t_rows, n_out), out_dtype)
    if has_norm:
        in_specs.append(pl.BlockSpec((1, n_out), lambda j, i, k: (0, 0)))
        args.append(norm_g)
        out_specs, out_shape = [tile, tile], [out_shape, jax.ShapeDtypeStruct((t_rows, n_out), BF16)]
    return pl.pallas_call(
        body, name=name, grid=(nj, t_rows // tm, nk), in_specs=in_specs, out_specs=out_specs, out_shape=out_shape,
        scratch_shapes=[pltpu.VMEM((tm, tn), F32)] if nk > 1 else [],
        compiler_params=_params(("arbitrary", "arbitrary", "arbitrary")),
    )(*args)


def _dot_nt(x, w):
    return lax.dot_general(x, w, (((1,), (1,)), ((), ())), preferred_element_type=F32)


def _dot_tn(x, y):
    return lax.dot_general(x, y, (((0,), (0,)), ((), ())), preferred_element_type=F32)


def _mm_nt_cols(name, pairs, ns, bn, out_defs, epi=None, extras=(), tm=512, deps=(), slabs=1, row_parts=1):
    t_rows = pairs[0][0].shape[0]
    tm = _tile(t_rows, tm)
    npair, nex, no, nx = len(pairs), len(extras), len(out_defs), len(deps)
    ns, bn = ns // slabs, bn * slabs
    rp = tm // row_parts

    def body(*refs):
        ws = [refs[2 * p + 1][...] for p in range(npair)]
        if slabs > 1:
            ws = [w.reshape(bn, w.shape[-1]) for w in ws]
        for part_i in range(row_parts):
            rows = slice(part_i * rp, (part_i + 1) * rp)
            acc = None
            for p in range(npair):
                part = _dot_nt(refs[2 * p][rows, :], ws[p])
                acc = part if acc is None else acc + part
            ex = [r[rows, :] for r in refs[2 * npair:2 * npair + nex]]
            outs = epi(acc, *ex) if epi is not None else (acc,)
            for ref, val in zip(refs[2 * npair + nex + nx:], outs):
                ref[rows, :] = val.astype(ref.dtype)

    in_specs, args = [], []
    for (dy, w, w_block, w_idx) in pairs:
        in_specs.append(pl.BlockSpec((tm, dy.shape[1]), lambda s, i: (i, 0)))
        in_specs.append(pl.BlockSpec(w_block, functools.partial(lambda s, i, f: f(s), f=w_idx)))
        args += [dy, w]
    for e in extras:
        in_specs.append(pl.BlockSpec((tm, bn), lambda s, i: (i, s)))
        args.append(e)
    in_specs += [pl.BlockSpec(memory_space=pl.ANY)] * nx
    args += list(deps)
    res = pl.pallas_call(
        body, name=name, grid=(ns, t_rows // tm), in_specs=in_specs,
        out_specs=[pl.BlockSpec((tm, bn), lambda s, i: (i, s)) for _ in range(no)],
        out_shape=[jax.ShapeDtypeStruct((t_rows, ns * bn), dt) for dt in out_defs],
        compiler_params=_params(("arbitrary", "arbitrary")),
    )(*args)
    return res


def _mm_nt_k(name, pairs, ns, n_out, out_dtype, tm=512, deps=()):
    t_rows = pairs[0][0].shape[0]
    tm = _tile(t_rows, tm)
    npair, nx = len(pairs), len(deps)

    def body(*refs):
        o_ref, acc_ref = refs[2 * npair + nx], refs[2 * npair + nx + 1]
        s = pl.program_id(1)
        acc = None
        for p in range(npair):
            part = _dot_nt(refs[2 * p][...], refs[2 * p + 1][...])
            acc = part if acc is None else acc + part

        @pl.when(s == 0)
        def _():
            acc_ref[...] = acc

        @pl.when(s > 0)
        def _():
            acc_ref[...] += acc

        @pl.when(s == ns - 1)
        def _():
            o_ref[...] = acc_ref[...].astype(o_ref.dtype)

    in_specs, args = [], []
    for (a, w, w_block, w_idx) in pairs:
        in_specs.append(pl.BlockSpec((tm, w_block[-1]), lambda i, s: (i, s)))
        in_specs.append(pl.BlockSpec(w_block, functools.partial(lambda i, s, f: f(s), f=w_idx)))
        args += [a, w]
    in_specs += [pl.BlockSpec(memory_space=pl.ANY)] * nx
    args += list(deps)
    return pl.pallas_call(
        body, name=name, grid=(t_rows // tm, ns), in_specs=in_specs,
        out_specs=pl.BlockSpec((tm, n_out), lambda i, s: (i, 0)),
        out_shape=jax.ShapeDtypeStruct((t_rows, n_out), out_dtype),
        scratch_shapes=[pltpu.VMEM((tm, n_out), F32)],
        compiler_params=_params(("arbitrary", "arbitrary")),
    )(*args)


def _mm_tn(name, a, b, bk, bn, n_p, n_q, out_shape, out_block, out_idx, into=None, a_off=0, b_off=0, tt=512,
           deps=()):
    t_rows = a.shape[0]
    tt = _tile(t_rows, tt, 16)
    nt = t_rows // tt
    has_into = into is not None
    nx = len(deps)

    def body(*refs):
        a_ref, b_ref = refs[0], refs[1]
        o_ref, acc_ref = refs[2 + has_into + nx], refs[3 + has_into + nx]
        t = pl.program_id(2)
        part = _dot_tn(a_ref[...], b_ref[...])

        @pl.when(t == 0)
        def _():
            acc_ref[...] = part

        @pl.when(t > 0)
        def _():
            acc_ref[...] += part

        @pl.when(t == nt - 1)
        def _():
            o_ref[...] = acc_ref[...].astype(o_ref.dtype)

    in_specs = [pl.BlockSpec((tt, bk), lambda p, q, t: (t, p + a_off)),
                pl.BlockSpec((tt, bn), lambda p, q, t: (t, q + b_off))]
    args = [a, b]
    aliases = {}
    if has_into:
        in_specs.append(pl.BlockSpec(memory_space=pl.ANY))
        args.append(into)
        aliases = {2: 0}
        out_shape = jax.ShapeDtypeStruct(into.shape, into.dtype)
    in_specs += [pl.BlockSpec(memory_space=pl.ANY)] * nx
    args += list(deps)
    return pl.pallas_call(
        body, name=name, grid=(n_p, n_q, nt), in_specs=in_specs,
        out_specs=pl.BlockSpec(out_block, lambda p, q, t: out_idx(p, q)),
        out_shape=out_shape, scratch_shapes=[pltpu.VMEM((bk, bn), F32)],
        input_output_aliases=aliases,
        compiler_params=_params(("arbitrary", "arbitrary", "arbitrary")),
    )(*args)


def _ffn_up(name, n, ga, slots, cols, d_model, fs, into=None, tm=512):
    t_rows = n.shape[0]
    tm = _tile(t_rows, tm)
    n_sh = slots.shape[0]
    has_into = into is not None

    row_parts = 2 if tm % 32 == 0 else 1

    def body(slot_ref, col_ref, n_ref, wg_ref, wu_ref, *refs):
        a_ref, b_ref, h_ref = refs[3 * has_into:]
        wg, wu = wg_ref[...], wu_ref[...]
        for part in range(row_parts):
            rows = slice(part * (tm // row_parts), (part + 1) * (tm // row_parts))
            x = n_ref[rows, :]
            a = jnp.dot(x, wg, preferred_element_type=F32)
            b = jnp.dot(x, wu, preferred_element_type=F32)
            a_ref[rows, :] = a.astype(a_ref.dtype)
            b_ref[rows, :] = b.astype(b_ref.dtype)
            h_ref[rows, :] = (a * _sigmoid(a) * b).astype(h_ref.dtype)

    w_block = (None, None, d_model, fs)
    out = jax.ShapeDtypeStruct((t_rows, N_CHIPS * fs), BF16)
    in_specs = [pl.BlockSpec((tm, d_model), lambda s, i, sl, co: (i, 0)),
                pl.BlockSpec(w_block, lambda s, i, sl, co: (sl[s], 0, 0, 0)),
                pl.BlockSpec(w_block, lambda s, i, sl, co: (sl[s], 1, 0, 0))]
    args = [slots, cols, n, ga, ga]
    aliases = {}
    if has_into:
        in_specs += [pl.BlockSpec(memory_space=pl.ANY)] * 3
        args += list(into)
        aliases = {5: 0, 6: 1, 7: 2}
    return pl.pallas_call(
        body, name=name,
        grid_spec=pltpu.PrefetchScalarGridSpec(
            num_scalar_prefetch=2, grid=(n_sh, t_rows // tm), in_specs=in_specs,
            out_specs=[pl.BlockSpec((tm, fs), lambda s, i, sl, co: (i, co[s]))] * 3),
        out_shape=[out, out, out], input_output_aliases=aliases,
        compiler_params=_params(("arbitrary", "arbitrary")),
    )(*args)


def _swiglu_bwd(dh, a, b):
    a = a.astype(F32)
    b = b.astype(F32)
    sg = _sigmoid(a)
    return dh * b * sg * (1.0 + a * (1.0 - sg)), dh * a * sg


def _attn_fwd(q, k, v, tm=512):
    t_rows, d_model = q.shape
    n_mem = k.shape[0]
    hd = d_model // MEM_HEADS
    scale = hd ** -0.5
    tm = _tile(t_rows, tm)

    def body(q_ref, k_ref, v_ref, o_ref):
        for h in range(MEM_HEADS):
            cols = slice(h * hd, (h + 1) * hd)
            s = _dot_nt(q_ref[:, cols], k_ref[:, cols]) * scale
            s = s - jnp.max(s, axis=-1, keepdims=True)
            e = jnp.exp(s)
            p = e / jnp.sum(e, axis=-1, keepdims=True)
            o_ref[:, cols] = jnp.dot(p.astype(BF16), v_ref[:, cols], preferred_element_type=F32).astype(o_ref.dtype)

    return pl.pallas_call(
        body, name="attn_fwd", grid=(t_rows // tm,),
        in_specs=[pl.BlockSpec((tm, d_model), lambda i: (i, 0)),
                  pl.BlockSpec((n_mem, d_model), lambda i: (0, 0)),
                  pl.BlockSpec((n_mem, d_model), lambda i: (0, 0))],
        out_specs=pl.BlockSpec((tm, d_model), lambda i: (i, 0)),
        out_shape=jax.ShapeDtypeStruct((t_rows, d_model), BF16),
        compiler_params=_params(("arbitrary",)),
    )(q, k, v)


def _attn_bwd(q, k, v, do, tm=512):
    t_rows, d_model = q.shape
    n_mem = k.shape[0]
    hd = d_model // MEM_HEADS
    scale = hd ** -0.5
    tm = _tile(t_rows, tm, 16)

    def body(q_ref, k_ref, v_ref, do_ref, dq_ref, dk_ref, dv_ref):
        @pl.when(pl.program_id(0) == 0)
        def _():
            dk_ref[...] = jnp.zeros(dk_ref.shape, F32)
            dv_ref[...] = jnp.zeros(dv_ref.shape, F32)

        for h in range(MEM_HEADS):
            cols = slice(h * hd, (h + 1) * hd)
            qh, kh, vh, doh = q_ref[:, cols], k_ref[:, cols], v_ref[:, cols], do_ref[:, cols]
            s = _dot_nt(qh, kh) * scale
            s = s - jnp.max(s, axis=-1, keepdims=True)
            e = jnp.exp(s)
            p = e / jnp.sum(e, axis=-1, keepdims=True)
            dv_ref[:, cols] += _dot_tn(p.astype(BF16), doh)
            dp = _dot_nt(doh, vh)
            ds = (p * (dp - jnp.sum(dp * p, axis=-1, keepdims=True)) * scale).astype(BF16)
            dq_ref[:, cols] = jnp.dot(ds, kh, preferred_element_type=F32).astype(dq_ref.dtype)
            dk_ref[:, cols] += _dot_tn(ds, qh)

    full = pl.BlockSpec((n_mem, d_model), lambda i: (0, 0))
    tile = pl.BlockSpec((tm, d_model), lambda i: (i, 0))
    return pl.pallas_call(
        body, name="attn_bwd", grid=(t_rows // tm,),
        in_specs=[tile, full, full, tile], out_specs=[tile, full, full],
        out_shape=[jax.ShapeDtypeStruct((t_rows, d_model), BF16),
                   jax.ShapeDtypeStruct((n_mem, d_model), F32), jax.ShapeDtypeStruct((n_mem, d_model), F32)],
        compiler_params=_params(("arbitrary",)),
    )(q, k, v, do)


def _split_bf16(v):
    hi = v.astype(BF16)
    return hi, (v - hi.astype(F32)).astype(BF16)


def _pool_window(g):
    return jnp.left_shift(jnp.int32(POOL_WINDOWS[0]), g)


def _pool_fwd(u, col_off, w_pool, scale, tt=256):
    t_rows = u.shape[0]
    pw = w_pool.shape[-1]
    ng = w_pool.shape[0]
    tt = _tile(t_rows, tt, 16)
    nt = t_rows // tt
    assert POOL_WINDOWS == tuple(2 << i for i in range(ng)) and tt >= POOL_WINDOWS[-1]

    def body(vc_ref, vp_ref, w_ref, sc_ref, pooled_ref, z_ref):
        g, i = pl.program_id(0), pl.program_id(1)
        w = _pool_window(g)
        r = lax.broadcasted_iota(jnp.int32, (tt, tt), 0)
        c = lax.broadcasted_iota(jnp.int32, (tt, tt), 1)
        band_c = ((c <= r) & (c > r - w)).astype(BF16)
        band_p = (c > r - w + tt).astype(BF16)
        vc = vc_ref[...]
        ch, cl = _split_bf16(vc)
        ph, plo = _split_bf16(vp_ref[...] * (i > 0).astype(F32))
        sums = (jnp.dot(band_c, ch, preferred_element_type=F32) + jnp.dot(band_c, cl, preferred_element_type=F32)
                + jnp.dot(band_p, ph, preferred_element_type=F32) + jnp.dot(band_p, plo, preferred_element_type=F32))
        t = i * tt + lax.broadcasted_iota(jnp.int32, (tt, 1), 0)
        cnt = jnp.minimum(t + 1, w).astype(F32)
        pooled = (sums / cnt - vc).astype(BF16)
        pooled_ref[...] = pooled
        z_ref[...] = jnp.dot(pooled, w_ref[...], preferred_element_type=F32) * sc_ref[...]

    return pl.pallas_call(
        body, name="pool_fwd", grid=(ng, nt),
        in_specs=[pl.BlockSpec((tt, pw), lambda g, i: (i, col_off + g)),
                  pl.BlockSpec((tt, pw), lambda g, i: (jnp.maximum(i - 1, 0), col_off + g)),
                  pl.BlockSpec((None, pw, pw), lambda g, i: (g, 0, 0)),
                  pl.BlockSpec((1, pw), lambda g, i: (0, g))],
        out_specs=[pl.BlockSpec((tt, pw), lambda g, i: (i, g))] * 2,
        out_shape=[jax.ShapeDtypeStruct((t_rows, ng * pw), BF16), jax.ShapeDtypeStruct((t_rows, ng * pw), F32)],
        compiler_params=_params(("arbitrary", "arbitrary")),
    )(u, u, w_pool, scale)


def _pool_bwd1(dz, pooled, w_pool, scale, tt=256):
    t_rows = dz.shape[0]
    pw = w_pool.shape[-1]
    ng = w_pool.shape[0]
    tt = _tile(t_rows, tt, 16)
    nt = t_rows // tt

    def body(dz_ref, p_ref, w_ref, sc_ref, dp_ref, dw_ref, dsc_ref):
        g, i = pl.program_id(0), pl.program_id(1)
        w = _pool_window(g)

        @pl.when(i == 0)
        def _():
            dw_ref[...] = jnp.zeros(dw_ref.shape, F32)
            dsc_ref[...] = jnp.zeros(dsc_ref.shape, F32)

        dz_v = dz_ref[...]
        pooled = p_ref[...]
        zpre = jnp.dot(pooled, w_ref[...], preferred_element_type=F32)
        dsc_ref[...] += _colsum(dz_v * zpre)
        dzs = (dz_v * sc_ref[...]).astype(BF16)
        dw_ref[...] += _dot_tn(pooled, dzs)
        t = i * tt + lax.broadcasted_iota(jnp.int32, (tt, 1), 0)
        cnt = jnp.minimum(t + 1, w).astype(F32)
        dp_ref[...] = _dot_nt(dzs, w_ref[...]) / cnt

    return pl.pallas_call(
        body, name="pool_bwd1", grid=(ng, nt),
        in_specs=[pl.BlockSpec((tt, pw), lambda g, i: (i, g)),
                  pl.BlockSpec((tt, pw), lambda g, i: (i, g)),
                  pl.BlockSpec((None, pw, pw), lambda g, i: (g, 0, 0)),
                  pl.BlockSpec((1, pw), lambda g, i: (0, g))],
        out_specs=[pl.BlockSpec((tt, pw), lambda g, i: (i, g)),
                   pl.BlockSpec((None, pw, pw), lambda g, i: (g, 0, 0)),
                   pl.BlockSpec((1, pw), lambda g, i: (0, g))],
        out_shape=[jax.ShapeDtypeStruct((t_rows, ng * pw), F32), jax.ShapeDtypeStruct((ng, pw, pw), F32),
                   jax.ShapeDtypeStruct((1, ng * pw), F32)],
        compiler_params=_params(("arbitrary", "arbitrary")),
    )(dz, pooled, w_pool, scale)


def _pool_bwd2(dps, ng, tt=256):
    t_rows, width = dps.shape
    pw = width // ng
    tt = _tile(t_rows, tt, 16)
    nt = t_rows // tt

    def body(dc_ref, dn_ref, dv_ref):
        g, i = pl.program_id(0), pl.program_id(1)
        w = _pool_window(g)
        r = lax.broadcasted_iota(jnp.int32, (tt, tt), 0)
        c = lax.broadcasted_iota(jnp.int32, (tt, tt), 1)
        band_c = ((c >= r) & (c < r + w)).astype(BF16)
        band_n = (c < r + w - tt).astype(BF16)
        dc = dc_ref[...]
        ch, cl = _split_bf16(dc)
        nh, nl = _split_bf16(dn_ref[...] * (i < nt - 1).astype(F32))
        sums = (jnp.dot(band_c, ch, preferred_element_type=F32) + jnp.dot(band_c, cl, preferred_element_type=F32)
                + jnp.dot(band_n, nh, preferred_element_type=F32) + jnp.dot(band_n, nl, preferred_element_type=F32))
        t = i * tt + lax.broadcasted_iota(jnp.int32, (tt, 1), 0)
        cnt = jnp.minimum(t + 1, w).astype(F32)
        dv_ref[...] = sums - dc * cnt

    return pl.pallas_call(
        body, name="pool_bwd2", grid=(ng, nt),
        in_specs=[pl.BlockSpec((tt, pw), lambda g, i: (i, g)),
                  pl.BlockSpec((tt, pw), lambda g, i: (jnp.minimum(i + 1, nt - 1), g))],
        out_specs=pl.BlockSpec((tt, pw), lambda g, i: (i, g)),
        out_shape=jax.ShapeDtypeStruct((t_rows, width), F32),
        compiler_params=_params(("arbitrary", "arbitrary")),
    )(dps, dps)


def _cpow(ar, ai, n):
    rr, ri, br, bi = None, None, ar, ai
    while n:
        if n & 1:
            rr, ri = (br, bi) if rr is None else (rr * br - ri * bi, rr * bi + ri * br)
        n >>= 1
        if n:
            br, bi = br * br - bi * bi, 2.0 * br * bi
    return rr, ri


def _chunk_carries(st_re, st_im, pr, pi, order):
    cb = st_re.shape[1]
    sub = lax.broadcasted_iota(jnp.int32, (S5_LANES, cb), 0)
    cr = jnp.zeros((S5_LANES, cb), F32)
    ci = jnp.zeros((S5_LANES, cb), F32)
    prev_r = jnp.zeros((1, cb), F32)
    prev_i = jnp.zeros((1, cb), F32)
    for k, src in order:
        er, ei = st_re[src:src + 1, :], st_im[src:src + 1, :]
        nr = er + pr * prev_r - pi * prev_i
        ni = ei + pr * prev_i + pi * prev_r
        cr = jnp.where(sub == k, jnp.broadcast_to(nr, (S5_LANES, cb)), cr)
        ci = jnp.where(sub == k, jnp.broadcast_to(ni, (S5_LANES, cb)), ci)
        prev_r, prev_i = nr, ni
    return cr, ci


def _s5_fwd(up, bblk, ab, cblk, tt=128):
    n_rows, ws = up.shape
    nb, cw, cb2 = bblk.shape
    cb = cb2 // 2
    lc = n_rows // S5_LANES
    tt = _tile(lc, tt, 1)
    nt = lc // tt
    rt = S5_LANES * tt

    def body(u_ref, b_ref, ab_ref, c_ref, y_ref, s_ref, bu_ref, st_re, st_im):
        ps, ti = pl.program_id(1), pl.program_id(2)
        ar = jnp.broadcast_to(ab_ref[0:1, :], (S5_LANES, cb))
        ai = jnp.broadcast_to(ab_ref[1:2, :], (S5_LANES, cb))

        @pl.when((ps == 0) & (ti == 0))
        def _():
            st_re[...] = jnp.zeros(st_re.shape, F32)
            st_im[...] = jnp.zeros(st_im.shape, F32)

        @pl.when((ps == 1) & (ti == 0))
        def _():
            pr, pi = _cpow(ab_ref[0:1, :], ab_ref[1:2, :], lc)
            cr, ci = _chunk_carries(st_re, st_im, pr, pi, [(k, k - 1) for k in range(1, S5_LANES)])
            st_re[...] = cr
            st_im[...] = ci

        bu_ref[...] = jnp.dot(u_ref[...], b_ref[...], preferred_element_type=F32)

        def step(t, carry, store):
            sr, si = carry
            rows = pl.ds(pl.multiple_of(t * S5_LANES, S5_LANES), S5_LANES)
            nr = ar * sr - ai * si + bu_ref[rows, 0:cb]
            ni = ar * si + ai * sr + bu_ref[rows, cb:cb2]
            if store:
                s_ref[rows, 0:cb] = nr
                s_ref[rows, cb:cb2] = ni
            return nr, ni

        @pl.when(ps == 0)
        def _():
            sr, si = lax.fori_loop(0, tt, functools.partial(step, store=False), (st_re[...], st_im[...]))
            st_re[...] = sr
            st_im[...] = si

        @pl.when(ps == 1)
        def _():
            sr, si = lax.fori_loop(0, tt, functools.partial(step, store=True), (st_re[...], st_im[...]))
            st_re[...] = sr
            st_im[...] = si
            y_ref[...] = jnp.dot(s_ref[...].astype(BF16), c_ref[...], preferred_element_type=F32)

    return pl.pallas_call(
        body, name="s5_fwd", grid=(nb, 2, nt),
        in_specs=[pl.BlockSpec((rt, cw), lambda j, ps, ti: (ti, j)),
                  pl.BlockSpec((None, cw, cb2), lambda j, ps, ti: (j, 0, 0)),
                  pl.BlockSpec((None, 2, cb), lambda j, ps, ti: (j, 0, 0)),
                  pl.BlockSpec((None, cb2, cw), lambda j, ps, ti: (j, 0, 0))],
        out_specs=[pl.BlockSpec((rt, cw), lambda j, ps, ti: (ti * ps, j)),
                   pl.BlockSpec((None, rt, cb2), lambda j, ps, ti: (j, ti * ps, 0))],
        out_shape=[jax.ShapeDtypeStruct((n_rows, ws), F32), jax.ShapeDtypeStruct((nb, n_rows, cb2), F32)],
        scratch_shapes=[pltpu.VMEM((rt, cb2), F32), pltpu.VMEM((S5_LANES, cb), F32), pltpu.VMEM((S5_LANES, cb), F32)],
        compiler_params=_params(("arbitrary", "arbitrary", "arbitrary")),
    )(up, bblk, ab, cblk)


def _s5_bwd(dyp, up, s_all, bblk_t, ab, cblk_t, tt=128):
    n_rows, ws = up.shape
    nb, cb2, cw = bblk_t.shape
    cb = cb2 // 2
    lc = n_rows // S5_LANES
    tt = _tile(lc, tt, 1)
    nt = lc // tt
    rt = S5_LANES * tt

    def body(dy_ref, u_ref, s_ref, bt_ref, ab_ref, ct_ref, du_ref, db_ref, dc_ref, da_ref, ds_ref, st_re, st_im):
        ps, ti = pl.program_id(1), pl.program_id(2)
        ar = jnp.broadcast_to(ab_ref[0:1, :], (S5_LANES, cb))
        ai = jnp.broadcast_to(ab_ref[1:2, :], (S5_LANES, cb))

        @pl.when((ps == 0) & (ti == 0))
        def _():
            st_re[...] = jnp.zeros(st_re.shape, F32)
            st_im[...] = jnp.zeros(st_im.shape, F32)
            db_ref[...] = jnp.zeros(db_ref.shape, F32)
            dc_ref[...] = jnp.zeros(dc_ref.shape, F32)
            da_ref[...] = jnp.zeros(da_ref.shape, F32)

        @pl.when((ps == 1) & (ti == 0))
        def _():
            pr, pi = _cpow(ab_ref[0:1, :], -ab_ref[1:2, :], lc)
            cr, ci = _chunk_carries(st_re, st_im, pr, pi, [(k, k + 1) for k in range(S5_LANES - 2, -1, -1)])
            st_re[...] = cr
            st_im[...] = ci

        ds_ref[...] = jnp.dot(dy_ref[...], ct_ref[...], preferred_element_type=F32)

        def rows_of(i):
            return pl.ds(pl.multiple_of((tt - 1 - i) * S5_LANES, S5_LANES), S5_LANES)

        def step0(i, carry):
            gr, gi = carry
            rows = rows_of(i)
            return (ar * gr + ai * gi + ds_ref[rows, 0:cb], ar * gi - ai * gr + ds_ref[rows, cb:cb2])

        def step1(i, carry):
            gr, gi, acr, aci = carry
            rows = rows_of(i)
            sr, si = s_ref[rows, 0:cb], s_ref[rows, cb:cb2]
            acr = acr + sr * gr + si * gi
            aci = aci + sr * gi - si * gr
            nr = ar * gr + ai * gi + ds_ref[rows, 0:cb]
            ni = ar * gi - ai * gr + ds_ref[rows, cb:cb2]
            ds_ref[rows, 0:cb] = nr
            ds_ref[rows, cb:cb2] = ni
            return nr, ni, acr, aci

        @pl.when(ps == 0)
        def _():
            gr, gi = lax.fori_loop(0, tt, step0, (st_re[...], st_im[...]))
            st_re[...] = gr
            st_im[...] = gi

        @pl.when(ps == 1)
        def _():
            zero = jnp.zeros((S5_LANES, cb), F32)
            gr, gi, acr, aci = lax.fori_loop(0, tt, step1, (st_re[...], st_im[...], zero, zero))
            st_re[...] = gr
            st_im[...] = gi
            da_ref[0] += acr
            da_ref[1] += aci
            dsb = ds_ref[...].astype(BF16)
            du_ref[...] = jnp.dot(dsb, bt_ref[...], preferred_element_type=F32)
            db_ref[...] += _dot_tn(u_ref[...], dsb)
            dc_ref[...] += _dot_tn(dy_ref[...], s_ref[...].astype(BF16))

    def tile_idx(ps, ti):
        return (nt - 1 - ti) * ps + (nt - 1) * (1 - ps)

    return pl.pallas_call(
        body, name="s5_bwd", grid=(nb, 2, nt),
        in_specs=[pl.BlockSpec((rt, cw), lambda j, ps, ti: (nt - 1 - ti, j)),
                  pl.BlockSpec((rt, cw), lambda j, ps, ti: (tile_idx(ps, ti), j)),
                  pl.BlockSpec((None, rt, cb2), lambda j, ps, ti: (j, tile_idx(ps, ti), 0)),
                  pl.BlockSpec((None, cb2, cw), lambda j, ps, ti: (j, 0, 0)),
                  pl.BlockSpec((None, 2, cb), lambda j, ps, ti: (j, 0, 0)),
                  pl.BlockSpec((None, cw, cb2), lambda j, ps, ti: (j, 0, 0))],
        out_specs=[pl.BlockSpec((rt, cw), lambda j, ps, ti: (tile_idx(ps, ti), j)),
                   pl.BlockSpec((None, cw, cb2), lambda j, ps, ti: (j, 0, 0)),
                   pl.BlockSpec((None, cw, cb2), lambda j, ps, ti: (j, 0, 0)),
                   pl.BlockSpec((None, 2, S5_LANES, cb), lambda j, ps, ti: (j, 0, 0, 0))],
        out_shape=[jax.ShapeDtypeStruct((n_rows, ws), F32), jax.ShapeDtypeStruct((nb, cw, cb2), F32),
                   jax.ShapeDtypeStruct((nb, cw, cb2), F32), jax.ShapeDtypeStruct((nb, 2, S5_LANES, cb), F32)],
        scratch_shapes=[pltpu.VMEM((rt, cb2), F32), pltpu.VMEM((S5_LANES, cb), F32), pltpu.VMEM((S5_LANES, cb), F32)],
        compiler_params=_params(("arbitrary", "arbitrary", "arbitrary")),
    )(dyp, up, s_all, bblk_t, ab, cblk_t)


def _s5_discretize(a_re, a_im, log_dt, b_re, b_im):
    dt = jnp.exp(log_dt)[:, None]
    mag = jnp.exp(a_re * dt)
    abar_re = mag * jnp.cos(a_im * dt)
    abar_im = mag * jnp.sin(a_im * dt)
    nr, ni = abar_re - 1.0, abar_im
    den = a_re * a_re + a_im * a_im
    fr = (nr * a_re + ni * a_im) / den
    fi = (ni * a_re - nr * a_im) / den
    bbar_re = fr[..., None] * b_re - fi[..., None] * b_im
    bbar_im = fr[..., None] * b_im + fi[..., None] * b_re
    return abar_re, abar_im, bbar_re, bbar_im


def _perm_rows(a):
    n, c = a.shape
    return a.reshape(S5_LANES, n // S5_LANES, c).transpose(1, 0, 2).reshape(n, c)


def _unperm_rows(a):
    n, c = a.shape
    return a.reshape(n // S5_LANES, S5_LANES, c).transpose(1, 0, 2).reshape(n, c)


HBM = pl.BlockSpec(memory_space=pltpu.HBM)
SEM = pl.BlockSpec(memory_space=pltpu.SEMAPHORE)
ANY = pl.BlockSpec(memory_space=pl.ANY)
EFFECT = pltpu.SideEffectType.DATAFLOW_SIDE_EFFECTING
COPIES_PER_BUFFER = {"ag_ici": 3, "ag_fwd": 3, "pair": N_CHIPS, "scatter": 3, "half": 1, "swap": 1, "bcast": 3}
PAIRED_KINDS = ("pair", "scatter", "swap")


def _place():
    x, y, c = lax.axis_index("x"), lax.axis_index("y"), lax.axis_index("c")
    chips = [(1 - x, y), (x, 1 - y), (1 - x, 1 - y)]
    return x, y, c, 2 * x + y, chips


def _n_copies(kind, n_bufs):
    if isinstance(kind, tuple):
        return len(kind[1])
    return COPIES_PER_BUFFER[kind] * (n_bufs // 2 if kind in PAIRED_KINDS else n_bufs)


def _comm_copies(kind, bufs):
    if isinstance(kind, tuple):
        full = _comm_copies(kind[0], bufs)
        return [full[k] for k in kind[1]]
    x, y, c, s, chips = _place()
    sib = (x, y, 1 - c)
    out = []
    if kind == "ag_ici":
        for w in bufs:
            for cx, cy in chips:
                out.append((w.at[s, c], w.at[s, c], w.at[2 * cx + cy, c], (cx, cy, c)))
    elif kind == "ag_fwd":
        for w in bufs:
            for cx, cy in chips:
                sj = 2 * cx + cy
                out.append((w.at[sj, c], w.at[sj, c], w.at[sj, 1 - c], sib))
    elif kind == "pair":
        n = len(bufs) // 2
        for g, got in zip(bufs[:n], bufs[n:]):
            for t in range(N_CHIPS):
                out.append((g.at[t, 1 - c], got.at[t], got.at[t], sib))
    elif kind == "scatter":
        n = len(bufs) // 2
        for p, got in zip(bufs[:n], bufs[n:]):
            for cx, cy in chips:
                out.append((p.at[2 * cx + cy], got.at[s], got.at[2 * cx + cy], (cx, cy, c)))
    elif kind == "half":
        for f in bufs:
            out.append((f.at[c], f.at[c], f.at[1 - c], sib))
    elif kind == "swap":
        n = len(bufs) // 2
        for v, got in zip(bufs[:n], bufs[n:]):
            out.append((v, got, got, sib))
    elif kind == "bcast":
        for w in bufs:
            for cx, cy in chips:
                out.append((w.at[s], w.at[s], w.at[2 * cx + cy], (cx, cy, c)))
    return out


def _comm_fused(name, kind, bufs):
    n = len(bufs)
    ncp = _n_copies(kind, n)

    def body(*refs):
        outs = refs[n:2 * n]
        send, recv = refs[2 * n:]
        copies = _comm_copies(kind, outs)
        started = []
        for k, (src, dst, _, peer) in enumerate(copies):
            cp = pltpu.make_async_remote_copy(src_ref=src, dst_ref=dst, send_sem=send.at[k], recv_sem=recv.at[k],
                                              device_id=peer, device_id_type=MESH)
            cp.start()
            started.append(cp)
        for k, (_, _, land, peer) in enumerate(copies):
            pltpu.make_async_remote_copy(src_ref=land, dst_ref=land, send_sem=send.at[k], recv_sem=recv.at[k],
                                         device_id=peer, device_id_type=MESH).wait_recv()
        for cp in started:
            cp.wait_send()

    return pl.pallas_call(
        body, name=name, in_specs=[ANY] * n, out_specs=[ANY] * n,
        out_shape=[jax.ShapeDtypeStruct(b.shape, b.dtype) for b in bufs],
        input_output_aliases={k: k for k in range(n)},
        scratch_shapes=[pltpu.SemaphoreType.DMA((ncp,))] * 2,
    )(*bufs)


def _comm_start(name, kind, bufs, after=None):
    n = len(bufs)
    ncp = _n_copies(kind, n)
    nx = 0 if after is None else 1

    def body(*refs):
        refs = refs[n + nx:]
        send, recv = refs[:ncp], refs[ncp:2 * ncp]
        outs = refs[2 * ncp:n + 2 * ncp]
        token = refs[n + 2 * ncp]
        for k, (src, dst, _, peer) in enumerate(_comm_copies(kind, outs)):
            pltpu.make_async_remote_copy(src_ref=src, dst_ref=dst, send_sem=send[k], recv_sem=recv[k],
                                         device_id=peer, device_id_type=MESH).start()
        token[...] = jnp.zeros(token.shape, token.dtype)

    res = pl.pallas_call(
        body, name=name, in_specs=[HBM] * n + [ANY] * nx,
        out_specs=[SEM] * (2 * ncp) + [HBM] * n + [pl.BlockSpec(memory_space=pltpu.VMEM)],
        out_shape=[pltpu.SemaphoreType.DMA(())] * (2 * ncp) + [pltpu.HBM(b.shape, b.dtype) for b in bufs]
        + [jax.ShapeDtypeStruct((8, 128), F32)],
        input_output_aliases={k: 2 * ncp + k for k in range(n)},
        compiler_params=pltpu.CompilerParams(has_side_effects=EFFECT),
    )(*[pltpu.with_memory_space_constraint(b, pltpu.HBM) for b in bufs], *([after] if nx else []))
    return list(res[:ncp]), list(res[ncp:2 * ncp]), list(res[2 * ncp:2 * ncp + n]), res[2 * ncp + n]


def _comm_wait(name, kind, bufs, send_sems, recv_sems, after):
    n = len(bufs)
    ncp = _n_copies(kind, n)

    def body(*refs):
        send, recv = refs[n:n + ncp], refs[n + ncp:n + 2 * ncp]
        outs = refs[n + 2 * ncp + 1:]
        for k, (src, _, land, peer) in enumerate(_comm_copies(kind, outs)):
            cp = pltpu.make_async_remote_copy(src_ref=src, dst_ref=land, send_sem=send[k], recv_sem=recv[k],
                                              device_id=peer, device_id_type=MESH)
            cp.wait_send()
            cp.wait_recv()

    return pl.pallas_call(
        body, name=name, in_specs=[HBM] * n + [SEM] * (2 * ncp) + [ANY], out_specs=[HBM] * n,
        out_shape=[pltpu.HBM(b.shape, b.dtype) for b in bufs],
        input_output_aliases={k: k for k in range(n)},
        compiler_params=pltpu.CompilerParams(has_side_effects=EFFECT),
    )(*bufs, *send_sems, *recv_sems, after)


def _pair_add(g, got, core):
    nchip, _, r, cw = g.shape
    tr = _tile(r, 512, 16)

    def body(c_ref, a_ref, b_ref, o_ref):
        o_ref[...] = a_ref[...] + b_ref[...]

    return pl.pallas_call(
        body, name="grads_pair_add",
        grid_spec=pltpu.PrefetchScalarGridSpec(
            num_scalar_prefetch=1, grid=(nchip, r // tr),
            in_specs=[pl.BlockSpec((None, None, tr, cw), lambda s, i, c_ref: (s, c_ref[0], i, 0)),
                      pl.BlockSpec((None, tr, cw), lambda s, i, c_ref: (s, i, 0))],
            out_specs=pl.BlockSpec((None, tr, cw), lambda s, i, c_ref: (s, i, 0))),
        out_shape=jax.ShapeDtypeStruct((nchip, r, cw), BF16),
        compiler_params=_params(("arbitrary", "arbitrary")),
    )(core, g, got)


def _chip_sum(parts, got, idx):
    _, r, cw = parts.shape
    tr = _tile(r, 512, 16)

    def body(i_ref, own_ref, a_ref, b_ref, c_ref, o_ref):
        o_ref[...] = ((own_ref[...].astype(F32) + a_ref[...].astype(F32)) + b_ref[...].astype(F32)) + c_ref[...].astype(F32)

    def slot(k):
        return pl.BlockSpec((None, tr, cw), lambda i, i_ref: (i_ref[k], i, 0))

    return pl.pallas_call(
        body, name="grads_chip_sum",
        grid_spec=pltpu.PrefetchScalarGridSpec(
            num_scalar_prefetch=1, grid=(r // tr,), in_specs=[slot(0), slot(1), slot(2), slot(3)], out_specs=slot(4)),
        out_shape=jax.ShapeDtypeStruct((2, r, cw), F32),
        compiler_params=_params(("arbitrary",)),
    )(idx, parts, got, got, got)


def _add_into_slot(v, got, chip):
    r, cw = v.shape
    tr = _tile(r, 256)

    def body(c_ref, a_ref, b_ref, o_ref):
        o_ref[...] = a_ref[...] + b_ref[...]

    tile = pl.BlockSpec((tr, cw), lambda i, c_ref: (i, 0))
    return pl.pallas_call(
        body, name="small_pair_add",
        grid_spec=pltpu.PrefetchScalarGridSpec(
            num_scalar_prefetch=1, grid=(r // tr,), in_specs=[tile, tile],
            out_specs=pl.BlockSpec((None, tr, cw), lambda i, c_ref: (c_ref[0], i, 0))),
        out_shape=jax.ShapeDtypeStruct((N_CHIPS, r, cw), F32),
        compiler_params=_params(("arbitrary",)),
    )(chip, v, got)


def _sum_slots(w):
    _, r, cw = w.shape
    tr = _tile(r, 256)

    def body(w_ref, o_ref):
        o_ref[...] = ((w_ref[0] + w_ref[1]) + w_ref[2]) + w_ref[3]

    return pl.pallas_call(
        body, name="small_chip_sum", grid=(r // tr,),
        in_specs=[pl.BlockSpec((N_CHIPS, tr, cw), lambda i: (0, i, 0))],
        out_specs=pl.BlockSpec((tr, cw), lambda i: (i, 0)),
        out_shape=jax.ShapeDtypeStruct((r, cw), F32),
        compiler_params=_params(("arbitrary",)),
    )(w)


def _adamw_math(w, g, m, v):
    m = ADAM_B1 * m + (1.0 - ADAM_B1) * g
    v = ADAM_B2 * v + (1.0 - ADAM_B2) * (g * g)
    m_hat = m / (1.0 - ADAM_B1 ** ADAM_STEP)
    v_hat = v / (1.0 - ADAM_B2 ** ADAM_STEP)
    delta = -ADAM_LR * (m_hat / (jnp.sqrt(v_hat) + ADAM_EPS) + ADAM_WD * w)
    return delta, m, v


def _adamw(name, w, m, v, g, g_half=0, g_row_off=0, tr=256):
    r, cw = w.shape
    tr = _tile(math.gcd(r, g_row_off) if g_row_off else r, tr)
    off = g_row_off // tr

    def body(w_ref, m_ref, v_ref, g_ref, go_ref, d_ref, mo_ref, vo_ref):
        g_v = g_ref[...]
        delta, m_n, v_n = _adamw_math(w_ref[...], g_v, m_ref[...], v_ref[...])
        go_ref[...] = g_v
        d_ref[...] = delta
        mo_ref[...] = m_n
        vo_ref[...] = v_n

    tile = pl.BlockSpec((tr, cw), lambda i: (i, 0))
    out = jax.ShapeDtypeStruct((r, cw), F32)
    return pl.pallas_call(
        body, name=name, grid=(r // tr,),
        in_specs=[tile, tile, tile, pl.BlockSpec((None, tr, cw), lambda i: (g_half, i + off, 0))],
        out_specs=[tile] * 4, out_shape=[out] * 4,
        compiler_params=_params(("arbitrary",)),
    )(w, m, v, g)


def kernel(x, mem, g_ffn1, w1_gate, w1_up, w1_down, g_mix, w_in, ssm_a_re, ssm_a_im, ssm_log_dt, ssm_b_re, ssm_b_im, ssm_c_re, ssm_c_im, ssm_d, w_glu, b_glu, w_pool, pool_scale, g_out_ssm, g_out_pool, w_out, g_xattn, g_mem, w_q, w_k, w_v, w_o, g_ffn2, w2_gate, w2_up, w2_down, g_final, loss_target, m_g_ffn1, m_w1_gate, m_w1_up, m_w1_down, m_g_mix, m_w_in, m_ssm_a_re, m_ssm_a_im, m_ssm_log_dt, m_ssm_b_re, m_ssm_b_im, m_ssm_c_re, m_ssm_c_im, m_ssm_d, m_w_glu, m_b_glu, m_w_pool, m_pool_scale, m_g_out_ssm, m_g_out_pool, m_w_out, m_g_xattn, m_g_mem, m_w_q, m_w_k, m_w_v, m_w_o, m_g_ffn2, m_w2_gate, m_w2_up, m_w2_down, m_g_final, v_g_ffn1, v_w1_gate, v_w1_up, v_w1_down, v_g_mix, v_w_in, v_ssm_a_re, v_ssm_a_im, v_ssm_log_dt, v_ssm_b_re, v_ssm_b_im, v_ssm_c_re, v_ssm_c_im, v_ssm_d, v_w_glu, v_b_glu, v_w_pool, v_pool_scale, v_g_out_ssm, v_g_out_pool, v_w_out, v_g_xattn, v_g_mem, v_w_q, v_w_k, v_w_v, v_w_o, v_g_ffn2, v_w2_gate, v_w2_up, v_w2_down, v_g_final):
    local = dict(locals())
    wts = {n: local[n] for n in WEIGHTS}
    mom = {n: local["m_" + n] for n in WEIGHTS}
    var = {n: local["v_" + n] for n in WEIGHTS}

    x2 = x[0]
    mem2 = mem[0]
    tgt = loss_target[0]
    t_rows, d = x2.shape
    fs = w1_gate.shape[-1]
    ds_ = w_in.shape[1]
    ws = d // 2
    n_pg = len(POOL_WINDOWS)
    pw = ws // n_pg
    n_grp = ws // SSM_GROUP
    n_state = ssm_a_re.shape[-1]
    cx_, cy_, cc_ = lax.axis_index("x"), lax.axis_index("y"), lax.axis_index("c")
    chip = (2 * cx_ + cy_).astype(jnp.int32)
    core = cc_.astype(jnp.int32).reshape(1)
    chip_idx = jnp.stack([chip, chip ^ 2, chip ^ 1, chip ^ 3, cc_.astype(jnp.int32)])

    glu_rows = w_glu[0].reshape(-1, d)
    pool_rows = w_pool[0].reshape(-1, d)
    gh, ph = glu_rows.shape[0] // 2, pool_rows.shape[0] // 2
    rh = -(-(3 * ds_ + gh + ph) // 128) * 128
    pad_rows = jnp.zeros((rh - 3 * ds_ - gh - ph, d), F32)

    def own_slot(src):
        src = src.astype(BF16)
        return lax.dynamic_update_slice(lax.empty((N_CHIPS,) + src.shape, BF16), src[None], (chip, 0, 0, 0))

    src_b2 = jnp.stack([
        jnp.concatenate([w_in[0], w_out[0], w_q[0], glu_rows[:gh], pool_rows[:ph], pad_rows], 0),
        jnp.concatenate([w_k[0], w_v[0], w_o[0], glu_rows[gh:], pool_rows[ph:], pad_rows], 0)])
    src_up1 = jnp.stack([w1_gate[0], w1_up[0]]).astype(BF16)
    w_bufs = [own_slot(src_up1), own_slot(w1_down[0].reshape(2, fs // 2, d)),
              own_slot(src_b2), own_slot(jnp.stack([w2_gate[0], w2_up[0]])),
              own_slot(w2_down[0].reshape(2, fs // 2, d))]
    ag_send, ag_recv, w_bufs, ag_token = _comm_start("weights_start", "ag_ici", w_bufs)

    def gathered(k, after):
        w = _comm_wait("weights_wait_%d" % k, "ag_ici", [w_bufs[k]], ag_send[3 * k:3 * k + 3],
                       ag_recv[3 * k:3 * k + 3], after)
        return _comm_fused("weights_forward_%d" % k, "ag_fwd", w)[0]

    def gathered_start(k, after):
        w = _comm_wait("weights_wait_%d" % k, "ag_ici", [w_bufs[k]], ag_send[3 * k:3 * k + 3],
                       ag_recv[3 * k:3 * k + 3], after)
        send, recv, thru, token = _comm_start("weights_forward_start_%d" % k, "ag_fwd", w)
        return (send, recv, thru), token

    def gathered_finish(k, handle, after):
        send, recv, thru = handle
        return _comm_wait("weights_forward_wait_%d" % k, "ag_fwd", thru, send, recv, after)[0]

    n1 = _rmsnorm("norm_ffn1", x2, wts['g_ffn1'].reshape(1, -1), deps=[ag_token])
    zero1 = jnp.zeros((1,), jnp.int32)
    near, far = jnp.stack([chip ^ 2, chip ^ 1]), (chip ^ 3).reshape(1)
    a1, b1, hm1 = _ffn_up("ffn1_up_own", n1, src_up1[None], zero1, chip.reshape(1), d, fs)
    w0 = _comm_wait("weights_wait_0_near", ("ag_ici", (0, 1)), [w_bufs[0]], ag_send[0:2], ag_recv[0:2], hm1)
    w0 = _comm_fused("weights_forward_0_near", ("ag_fwd", (0, 1)), w0)
    a1, b1, hm1 = _ffn_up("ffn1_up_near", n1, w0[0], near, near, d, fs, into=(a1, b1, hm1))
    w0 = _comm_wait("weights_wait_0_far", ("ag_ici", (2,)), w0, ag_send[2:3], ag_recv[2:3], hm1)
    ga1 = _comm_fused("weights_forward_0_far", ("ag_fwd", (2,)), w0)[0]
    a1, b1, hm1 = _ffn_up("ffn1_up_far", n1, ga1, far, far, d, fs, into=(a1, b1, hm1))
    gd1 = gathered(1, hm1).reshape(N_CHIPS, fs, d)
    h1, n2 = _mm_nn("ffn1_down", hm1, gd1, (None, fs, d), lambda j, k: (k, 0, 0), N_CHIPS, d, F32, res=x2, alpha=0.5,
                    norm_g=wts['g_mix'].reshape(1, -1))
    gb2 = gathered(2, h1)
    wglu_full = gb2[:, :, 3 * ds_:3 * ds_ + gh, :].reshape(ws, ws)
    wpool_full = gb2[:, :, 3 * ds_ + gh:3 * ds_ + gh + ph, :].reshape(N_CHIPS, n_pg, pw // N_CHIPS, pw)
    wpool_full = wpool_full.transpose(1, 0, 2, 3).reshape(n_pg, pw, pw)
    DD = {'w_in': (0, 0), 'w_out': (0, 1), 'w_q': (0, 2), 'w_k': (1, 0), 'w_v': (1, 1), 'w_o': (1, 2)}

    def mm_dd(name, a, wname, out_dtype, res=None, norm_g=None):
        h, q = DD[wname]
        return _mm_nn(name, a, gb2, (N_CHIPS, None, ds_, d), lambda j, k: (0, h, q, 0), 1, d, out_dtype, res=res,
                      norm_g=norm_g, slabs=N_CHIPS)

    def mm_dd_t(name, pairs, out_dtype, deps=()):
        ps = [(dy, gb2, (N_CHIPS, None, ds_, d), functools.partial(lambda s, h, q: (0, h, q, 0), h=DD[w][0], q=DD[w][1]))
              for dy, w in pairs]
        return _mm_nt_cols(name, ps, N_CHIPS, ds_, [out_dtype], deps=deps, slabs=N_CHIPS)[0]

    def vec(n):
        return wts[n].reshape(1, -1)

    disc_in = (ssm_a_re[0], ssm_a_im[0], ssm_log_dt[0], ssm_b_re[0], ssm_b_im[0])
    (abar_re, abar_im, bbar_re, bbar_im), disc_vjp = jax.vjp(_s5_discretize, *disc_in)
    gpb = min(S5_GROUPS_PER_BLOCK, n_grp)
    nb = n_grp // gpb
    cb = gpb * n_state
    eye = jnp.eye(gpb, dtype=F32)

    def blockdiag(t):
        return jnp.einsum('jgph,gk->jghkp', t.reshape(nb, gpb, n_state, SSM_GROUP), eye).reshape(nb, gpb * SSM_GROUP, cb)

    def blockdiag_c(t):
        return jnp.einsum('jghp,gk->jkpgh', t.reshape(nb, gpb, SSM_GROUP, n_state), eye).reshape(nb, cb, gpb * SSM_GROUP)

    bblk = jnp.concatenate([blockdiag(bbar_re), blockdiag(bbar_im)], -1).astype(BF16)
    cblk = jnp.concatenate([blockdiag_c(ssm_c_re[0]), -blockdiag_c(ssm_c_im[0])], 1).astype(BF16)
    ab = jnp.stack([abar_re.reshape(nb, cb), abar_im.reshape(nb, cb)], 1)

    u = mm_dd("mix_in", n2, 'w_in', F32)

    up = _perm_rows(u[:, :ws]).astype(BF16)
    ylin_p, s_all = _s5_fwd(up, bblk, ab, cblk)
    ylin = _unperm_rows(ylin_p)

    def gelu_fn(r, v):
        y1 = r[0] + v[0] * r[1]
        y2 = jax.nn.gelu(y1)
        return [y2, y2], []
    fwd_a2, tok = gathered_start(3, ylin_p)
    y2, y2b = _rowwise("s5_gelu", gelu_fn, [ylin, (u, 0, ws)], [vec('ssm_d')], [(ws, F32), (ws, BF16)], deps=[tok])
    z = _mm_nn("s5_glu", y2b, wglu_full, (ws, ws), lambda j, k: (0, 0), 1, ws, F32)

    def glu_fn(r, v):
        y3 = r[0] * _sigmoid(r[1] + v[0])
        return [_rms_fwd(y3, v[1])], []
    m_ssm = _rowwise("s5_gate_norm", glu_fn, [y2, z], [vec('b_glu'), vec('g_out_ssm')], [(ws, BF16)])[0]

    pooled, zp = _pool_fwd(u, ws // pw, wpool_full, vec('pool_scale'))
    fwd_d2, tok = gathered_start(4, zp)
    m_pool = _rmsnorm("norm_pool", zp, vec('g_out_pool'), deps=[tok])
    merged = jnp.concatenate([m_ssm, m_pool], -1)
    h2, hn = mm_dd("mix_out", merged, 'w_out', F32, res=h1, norm_g=vec('g_xattn'))

    memn = _rmsnorm("norm_mem", mem2, vec('g_mem'))
    k_mem = mm_dd("attn_k", memn, 'w_k', BF16)
    v_mem = mm_dd("attn_v", memn, 'w_v', BF16)
    q = mm_dd("attn_q", hn, 'w_q', BF16)
    o = _attn_fwd(q, k_mem, v_mem)
    h3, n4 = mm_dd("attn_out", o, 'w_o', F32, res=h2, norm_g=vec('g_ffn2'))

    ga2 = gathered_finish(3, fwd_a2, h3)
    all_chips = jnp.arange(N_CHIPS, dtype=jnp.int32)
    a2, b2, hm2 = _ffn_up("ffn2_up", n4, ga2, all_chips, all_chips, d, fs)
    gd2 = gathered_finish(4, fwd_d2, hm2).reshape(N_CHIPS, fs, d)
    h4 = _mm_nn("ffn2_down", hm2, gd2, (None, fs, d), lambda j, k: (k, 0, 0), N_CHIPS, d, F32, res=h3, alpha=0.5)

    def loss_fn(r, v):
        h, t = r
        e = _rms_fwd(h, v[0]) - t
        dy = e * (1.0 / d)
        dh, dg = _rms_bwd(dy, h, v[0])
        part = jnp.sum(_colsum(e * e), axis=1, keepdims=True) * (0.5 / d)
        return [dh, 0.5 * dh], [_colsum(dg), jnp.broadcast_to(part, (1, 128))]
    dh4, dy_f2, dg_final, loss_row = _rowwise("loss_head", loss_fn, [h4, tgt], [g_final.reshape(1, -1)],
                                              [(d, F32), (d, BF16)], [d, 128])

    def rs_pair_start(tag, gbufs, after=None):
        land = [lax.empty((N_CHIPS,) + g.shape[2:], BF16) for g in gbufs]
        send, recv, thru, token = _comm_start(tag + "_pair_start", "pair", list(gbufs) + land, after=after)
        return (send, recv, thru), token

    def rs_scatter_start(tag, handle, after):
        send, recv, thru = handle
        n = len(thru) // 2
        res = _comm_wait(tag + "_pair_wait", "pair", thru, send, recv, after)
        parts = [_pair_add(g, r, core) for g, r in zip(res[:n], res[n:])]
        land = [lax.empty(p.shape, BF16) for p in parts]
        send, recv, thru, token = _comm_start(tag + "_scatter_start", "scatter", parts + land)
        return (send, recv, thru), token

    def rs_half_start(tag, handle, after):
        send, recv, thru = handle
        n = len(thru) // 2
        res = _comm_wait(tag + "_scatter_wait", "scatter", thru, send, recv, after)
        full = [_chip_sum(p, g2, chip_idx) for p, g2 in zip(res[:n], res[n:])]
        send, recv, thru, token = _comm_start(tag + "_half_start", "half", full)
        return (send, recv, thru), token

    def rs_finish(tag, handle, after):
        send, recv, thru = handle
        return _comm_wait(tag + "_half_wait", "half", thru, send, recv, after)

    wblk = (None, None, d, fs)

    def ffn_down_bwd(tag, dy_half, a, b, hm, gd_l, deps=()):
        da, db = _mm_nt_cols(tag + "_down_bwd", [(dy_half, gd_l, (None, fs, d), lambda s: (s, 0, 0))],
                             N_CHIPS, fs, [BF16, BF16], epi=_swiglu_bwd, extras=[a, b], deps=deps, row_parts=2)
        g_down = _mm_tn(tag + "_dw_down", hm, dy_half, fs, d // 2, N_CHIPS, 2, jax.ShapeDtypeStruct((N_CHIPS, fs, d), BF16),
                        (None, fs, d // 2), lambda p, q: (p, 0, q), tt=2048)
        return da, db, g_down.reshape(N_CHIPS, 2, fs // 2, d)

    def ffn_up_bwd(tag, da, db, ga_l, deps=()):
        return _mm_nt_k(tag + "_up_bwd", [(da, ga_l, wblk, lambda s: (s, 0, 0, 0)), (db, ga_l, wblk, lambda s: (s, 1, 0, 0))],
                        N_CHIPS, d, F32, deps=deps)

    def ffn_dw(name, dact, n_in, deps=()):
        return _mm_tn(name, n_in, dact, d // 2, fs, 2, N_CHIPS, jax.ShapeDtypeStruct((N_CHIPS, 2, d // 2, fs), BF16),
                      (None, None, d // 2, fs), lambda p, q: (q, p, 0, 0), tt=2048, deps=deps)

    def dw_dd(name, a, dy, wname, grad_b2):
        h, q = DD[wname]
        return _mm_tn(name, a, dy, ds_, d, N_CHIPS, 1, jax.ShapeDtypeStruct((N_CHIPS, 2, rh, d), BF16),
                      (None, None, ds_, d), lambda p, qq: (p, h, q, 0), into=grad_b2, tt=2048)

    def norm_bwd(name, dn, h, gname, dres, deps=(), scale=1.0):
        def fn(r, v):
            dx, dg = _rms_bwd(r[0], r[1], v[0])
            tot = dx + r[2]
            return [tot, scale * tot], [_colsum(dg)]
        return _rowwise(name, fn, [dn, h, dres], [vec(gname)], [(d, F32), (d, BF16)], [d], deps=deps)

    da2, db2, g_down2 = ffn_down_bwd("ffn2", dy_f2, a2, b2, hm2, gd2)
    dn4 = ffn_up_bwd("ffn2", da2, db2, ga2)
    g_gate2 = ffn_dw("ffn2_dw_gate", da2, n4)
    g_up2 = ffn_dw("ffn2_dw_up", db2, n4)
    rs_f2, tok = rs_pair_start("ffn2", [g_gate2, g_up2, g_down2])
    dh3, dh3b, dg_ffn2 = norm_bwd("norm_ffn2_bwd", dn4, h3, 'g_ffn2', dh4, deps=[tok])
    rs_f2, tok = rs_scatter_start("ffn2", rs_f2, dh3b)

    do = mm_dd_t("attn_out_bwd", [(dh3b, 'w_o')], BF16, deps=[tok])
    grad_b2 = dw_dd("attn_dw_o", o, dh3b, 'w_o', None)
    dq, dk, dv = _attn_bwd(q, k_mem, v_mem, do)
    dkb, dvb = dk.astype(BF16), dv.astype(BF16)
    grad_b2 = dw_dd("attn_dw_q", hn, dq, 'w_q', grad_b2)
    dhn = mm_dd_t("attn_q_bwd", [(dq, 'w_q')], F32)
    dh2, dh2b, dg_xattn = norm_bwd("norm_xattn_bwd", dhn, h2, 'g_xattn', dh3)
    grad_b2 = dw_dd("attn_dw_k", memn, dkb, 'w_k', grad_b2)
    grad_b2 = dw_dd("attn_dw_v", memn, dvb, 'w_v', grad_b2)
    dmemn = mm_dd_t("attn_kv_bwd", [(dkb, 'w_k'), (dvb, 'w_v')], F32)
    dg_mem = _rowwise("norm_mem_bwd", lambda r, v: ([], [_colsum(_rms_bwd(r[0], r[1], v[0])[1])]),
                      [dmemn, mem2], [vec('g_mem')], [], [d])[0]

    dmerged = mm_dd_t("mix_out_bwd", [(dh2b, 'w_out')], F32)
    grad_b2 = dw_dd("mix_dw_out", merged, dh2b, 'w_out', grad_b2)

    def gate_bwd_fn(r, v):
        dm, y2_v, z_v = r
        sg = _sigmoid(z_v + v[0])
        y3 = y2_v * sg
        dy3, dg = _rms_bwd(dm, y3, v[1])
        dz = dy3 * y3 * (1.0 - sg)
        return [dy3 * sg, dz], [_colsum(dg), _colsum(dz)]
    dy2a, dzb, dg_out_ssm, db_glu = _rowwise("s5_gate_norm_bwd", gate_bwd_fn, [(dmerged, 0, ws), y2, z],
                                             [vec('b_glu'), vec('g_out_ssm')], [(ws, F32), (ws, BF16)], [ws, ws])
    dy2b_ = _mm_nt_cols("s5_glu_bwd", [(dzb, wglu_full, (ws, ws), lambda s: (0, 0))], 1, ws, [F32])[0]
    dw_glu = _mm_tn("s5_dw_glu", y2b, dzb, ws, ws, 1, 1, jax.ShapeDtypeStruct((ws, ws), F32), (ws, ws), lambda p, q: (0, 0))

    def gelu_bwd_fn(r, v):
        dy2 = r[0] + r[1]
        us = r[3]
        y1 = r[2] + v[0] * us
        kk = math.sqrt(2.0 / math.pi)
        th = jnp.tanh(kk * (y1 + 0.044715 * y1 * y1 * y1))
        dgelu = 0.5 * (1.0 + th) + 0.5 * y1 * (1.0 - th * th) * kk * (1.0 + 3.0 * 0.044715 * y1 * y1)
        dy1 = dy2 * dgelu
        return [dy1, dy1 * v[0]], [_colsum(dy1 * us)]
    dy1b, du_skip, d_ssm_d = _rowwise("s5_gelu_bwd", gelu_bwd_fn, [dy2a, dy2b_, ylin, (u, 0, ws)], [vec('ssm_d')],
                                      [(ws, BF16), (ws, F32)], [ws])

    bblk_t = jnp.swapaxes(bblk, 1, 2)
    cblk_t = jnp.swapaxes(cblk, 1, 2)
    du_p, d_bblk, d_cblk_t, d_ab = _s5_bwd(_perm_rows(dy1b), up, s_all, bblk_t, ab, cblk_t)
    du_ssm = _unperm_rows(du_p)

    dzp, dg_out_pool = _rowwise("norm_pool_bwd", lambda r, v: (lambda dx, dg: ([dx], [_colsum(dg)]))(*_rms_bwd(r[0], r[1], v[0])),
                                [(dmerged, 1, ws), zp], [vec('g_out_pool')], [(ws, F32)], [ws])
    dps, dw_pool, d_pool_scale = _pool_bwd1(dzp, pooled, wpool_full, vec('pool_scale'))
    du_pool = _pool_bwd2(dps, n_pg)

    dub = _rowwise("mix_du", lambda r, v: ([jnp.concatenate([r[0] + r[1], r[2]], -1)], []),
                   [du_ssm, du_skip, du_pool], [], [(d, BF16)])[0]
    dn2 = mm_dd_t("mix_in_bwd", [(dub, 'w_in')], F32)
    grad_b2 = dw_dd("mix_dw_in", n2, dub, 'w_in', grad_b2)
    glu_g = dw_glu.reshape(N_CHIPS, 2, gh, d).astype(BF16)
    pool_g = dw_pool.reshape(n_pg, N_CHIPS, pw // N_CHIPS, pw).transpose(1, 0, 2, 3).reshape(N_CHIPS, 2, ph, d)
    pool_g = jnp.concatenate([pool_g, jnp.zeros((N_CHIPS, 2, rh - 3 * ds_ - gh - ph, d), F32)], 2).astype(BF16)
    grad_b2 = lax.dynamic_update_slice(grad_b2, glu_g, (0, 0, 3 * ds_, 0))
    grad_b2 = lax.dynamic_update_slice(grad_b2, pool_g, (0, 0, 3 * ds_ + gh, 0))
    rs_mix, tok = rs_pair_start("mixers", [grad_b2])
    dh1, dy_f1, dg_mix = norm_bwd("norm_mix_bwd", dn2, h1, 'g_mix', dh2, deps=[tok], scale=0.5)
    rs_mix, tok = rs_scatter_start("mixers", rs_mix, dy_f1)

    da1, db1, g_down1 = ffn_down_bwd("ffn1", dy_f1, a1, b1, hm1, gd1, deps=[tok])
    rs_d1, tok = rs_pair_start("ffn1_down", [g_down1])
    dn1 = ffn_up_bwd("ffn1", da1, db1, ga1, deps=[tok])
    rs_d1, tok = rs_scatter_start("ffn1_down", rs_d1, dn1)
    grad_x, _, dg_ffn1 = norm_bwd("norm_ffn1_bwd", dn1, x2, 'g_ffn1', dh1, deps=[tok])

    def undiag(t):
        return jnp.einsum('jghkp,gk->jgph', t.reshape(nb, gpb, SSM_GROUP, gpb, n_state), eye).reshape(n_grp, n_state, SSM_GROUP)

    d_bbar_re, d_bbar_im = undiag(d_bblk[:, :, :cb]), undiag(d_bblk[:, :, cb:])
    d_c_re = undiag(d_cblk_t[:, :, :cb]).transpose(0, 2, 1)
    d_c_im = -undiag(d_cblk_t[:, :, cb:]).transpose(0, 2, 1)
    d_abar = jnp.sum(d_ab, axis=2).reshape(nb, 2, gpb, n_state)
    d_abar_re = d_abar[:, 0].reshape(n_grp, n_state)
    d_abar_im = d_abar[:, 1].reshape(n_grp, n_state)
    d_a_re, d_a_im, d_log_dt, d_b_re, d_b_im = disc_vjp((d_abar_re, d_abar_im, d_bbar_re, d_bbar_im))

    small_g = {'g_ffn1': dg_ffn1, 'g_mix': dg_mix, 'ssm_a_re': d_a_re, 'ssm_a_im': d_a_im, 'ssm_log_dt': d_log_dt,
               'ssm_b_re': d_b_re, 'ssm_b_im': d_b_im, 'ssm_c_re': d_c_re, 'ssm_c_im': d_c_im, 'ssm_d': d_ssm_d,
               'b_glu': db_glu, 'pool_scale': d_pool_scale, 'g_out_ssm': dg_out_ssm, 'g_out_pool': dg_out_pool,
               'g_xattn': dg_xattn, 'g_mem': dg_mem, 'g_ffn2': dg_ffn2, 'g_final': dg_final}
    sizes = [wts[n].size for n in SMALL]
    total = sum(sizes) + 128
    rows_s = -(-total // (128 * 256)) * 256
    flat = jnp.concatenate([small_g[n].reshape(-1) for n in SMALL] + [loss_row.reshape(-1)])
    flat = jnp.pad(flat, (0, rows_s * 128 - total)).reshape(rows_s, 128)
    sw_send, sw_recv, sw_thru, tok = _comm_start("small_swap_start", "swap", [flat, lax.empty(flat.shape, F32)])
    g_gate1 = ffn_dw("ffn1_dw_gate", da1, n1, deps=[tok])
    rs_g1, tok_g1 = rs_pair_start("ffn1_gate", [g_gate1])
    sw_v, sw_got = _comm_wait("small_swap_wait", "swap", sw_thru, sw_send, sw_recv, tok_g1)
    slots = _add_into_slot(sw_v, sw_got, chip.reshape(1))
    bc_send, bc_recv, bc_thru, tok = _comm_start("small_bcast_start", "bcast", [slots])
    g_up1 = ffn_dw("ffn1_dw_up", db1, n1, deps=[tok])
    rs_g1, tok = rs_scatter_start("ffn1_gate", rs_g1, g_up1)
    slots, = _comm_wait("small_bcast_wait", "bcast", bc_thru, bc_send, bc_recv, tok)
    red = _sum_slots(slots).reshape(-1)
    loss = red[sum(sizes)]

    def flat_small(t):
        return jnp.pad(jnp.concatenate([t[n].reshape(-1) for n in SMALL]), (0, rows_s * 128 - sum(sizes))).reshape(rows_s, 128)
    sg_, sd_, sm_, sv_ = _adamw("adamw_small", flat_small(wts), flat_small(mom), flat_small(var), red.reshape(1, rows_s, 128))
    out = {}
    off = 0
    for n, sz in zip(SMALL, sizes):
        for key, arr in (('grad', sg_), ('delta', sd_), ('m', sm_), ('v', sv_)):
            out[key, n] = arr.reshape(-1)[off:off + sz].reshape(wts[n].shape)
        off += sz

    def upd(n, g_arr, half, row_off, shape2):
        res = _adamw("adamw_" + n, wts[n].reshape(shape2), mom[n].reshape(shape2), var[n].reshape(shape2), g_arr, half, row_off)
        for key, arr in zip(('grad', 'delta', 'm', 'v'), res):
            out[key, n] = arr.reshape(wts[n].shape)
        return res[3]

    rs_u1, tok = rs_pair_start("ffn1_up", [g_up1], after=sv_)
    rs_f2, tok = rs_half_start("ffn2", rs_f2, tok)
    rs_u1, tok = rs_scatter_start("ffn1_up", rs_u1, tok)
    rs_mix, tok = rs_half_start("mixers", rs_mix, tok)
    full_gate2, full_up2, full_down2 = rs_finish("ffn2", rs_f2, tok)
    upd('w2_gate', full_gate2.reshape(1, d, fs), 0, 0, (d, fs))
    upd('w2_up', full_up2.reshape(1, d, fs), 0, 0, (d, fs))
    last = upd('w2_down', full_down2.reshape(1, fs, d), 0, 0, (fs, d))
    rs_d1, tok = rs_half_start("ffn1_down", rs_d1, last)
    full_b2, = rs_finish("mixers", rs_mix, tok)
    for n, (h, q) in DD.items():
        last = upd(n, full_b2, h, q * ds_, (ds_, d))
    glu_shape, pool_shape = (ws // N_CHIPS, ws), (n_pg * pw // N_CHIPS, pw)
    upd('w_glu', full_b2[:, 3 * ds_:3 * ds_ + gh].reshape((1,) + glu_shape), 0, 0, glu_shape)
    upd('w_pool', full_b2[:, 3 * ds_ + gh:3 * ds_ + gh + ph].reshape((1,) + pool_shape), 0, 0, pool_shape)
    full_down1, = rs_finish("ffn1_down", rs_d1, last)
    last = upd('w1_down', full_down1.reshape(1, fs, d), 0, 0, (fs, d))
    rs_g1, tok = rs_half_start("ffn1_gate", rs_g1, last)
    rs_u1, tok = rs_half_start("ffn1_up", rs_u1, tok)
    full_gate1, = rs_finish("ffn1_gate", rs_g1, tok)
    last = upd('w1_gate', full_gate1.reshape(1, d, fs), 0, 0, (d, fs))
    full_up1, = rs_finish("ffn1_up", rs_u1, last)
    upd('w1_up', full_up1.reshape(1, d, fs), 0, 0, (d, fs))

    return (loss, grad_x[None], *[out['grad', n] for n in WEIGHTS], *[out['delta', n] for n in WEIGHTS],
            *[out['m', n] for n in WEIGHTS], *[out['v', n] for n in WEIGHTS])
```

```python
import functools
import math

import jax
import jax.numpy as jnp
from jax import lax
from jax.experimental import pallas as pl
from jax.experimental.pallas import tpu as pltpu

F32 = jnp.float32
BF16 = jnp.bfloat16
EPS = 1e-6
ADAM_LR, ADAM_B1, ADAM_B2, ADAM_EPS, ADAM_WD, ADAM_STEP = 0.001, 0.9, 0.999, 1e-08, 0.01, 10
POOL_WINDOWS = (2, 4, 8, 16)
SSM_GROUP = 16
S5_GROUPS_PER_BLOCK = 16
S5_LANES = 8
MEM_HEADS = 4
N_CHIPS = 4
VMEM_LIMIT_V7X = 56 * 1024 * 1024
MESH = pl.DeviceIdType.MESH

WEIGHTS = ['g_ffn1', 'w1_gate', 'w1_up', 'w1_down', 'g_mix', 'w_in', 'ssm_a_re', 'ssm_a_im', 'ssm_log_dt',
           'ssm_b_re', 'ssm_b_im', 'ssm_c_re', 'ssm_c_im', 'ssm_d', 'w_glu', 'b_glu', 'w_pool', 'pool_scale',
           'g_out_ssm', 'g_out_pool', 'w_out', 'g_xattn', 'g_mem', 'w_q', 'w_k', 'w_v', 'w_o', 'g_ffn2',
           'w2_gate', 'w2_up', 'w2_down', 'g_final']
BIG = ['w1_gate', 'w1_up', 'w1_down', 'w_in', 'w_glu', 'w_pool', 'w_out', 'w_q', 'w_k', 'w_v', 'w_o',
       'w2_gate', 'w2_up', 'w2_down']
SMALL = [n for n in WEIGHTS if n not in BIG]


def _tile(n, target, mult=8):
    best = None
    for d in range(1, n + 1):
        if n % d == 0 and d <= target and d % mult == 0:
            best = d
    return best if best is not None else n


def _params(sem=None):
    if sem is None:
        return pltpu.CompilerParams(vmem_limit_bytes=VMEM_LIMIT_V7X)
    return pltpu.CompilerParams(dimension_semantics=sem, vmem_limit_bytes=VMEM_LIMIT_V7X)


def _sigmoid(x):
    return 1.0 / (1.0 + jnp.exp(-x))


def _sigmoid_approx(x):
    return pl.reciprocal(1.0 + jnp.exp(-x), approx=True)


def _rms_fwd(x, g):
    r = lax.rsqrt(jnp.mean(x * x, axis=-1, keepdims=True) + EPS)
    return x * r * g


def _rms_bwd(dy, x, g):
    r = lax.rsqrt(jnp.mean(x * x, axis=-1, keepdims=True) + EPS)
    dxh = dy * g
    dx = r * dxh - x * (r * r * r) * jnp.mean(dxh * x, axis=-1, keepdims=True)
    return dx, dy * x * r


def _colsum(v):
    return jnp.sum(v, axis=0, keepdims=True)


def _rowwise(name, fn, rows, vecs, out_defs, red_defs=(), tm=256, deps=()):
    rows = [r if isinstance(r, tuple) else (r, 0, r.shape[1]) for r in rows]
    t_rows = rows[0][0].shape[0]
    tm = _tile(t_rows, tm)
    nr, nv, no, nd, nx = len(rows), len(vecs), len(out_defs), len(red_defs), len(deps)

    def body(*refs):
        r, v = refs[:nr], refs[nr:nr + nv]
        o, d = refs[nr + nv + nx:nr + nv + nx + no], refs[nr + nv + nx + no:]
        outs, reds = fn([x[...] for x in r], [x[...] for x in v])
        for ref, val in zip(o, outs):
            ref[...] = val.astype(ref.dtype)
        if nd:
            @pl.when(pl.program_id(0) == 0)
            def _():
                for ref in d:
                    ref[...] = jnp.zeros(ref.shape, ref.dtype)
            for ref, val in zip(d, reds):
                ref[...] += val

    in_specs = [pl.BlockSpec((tm, w), functools.partial(lambda i, cb: (i, cb), cb=cb)) for (_, cb, w) in rows]
    in_specs += [pl.BlockSpec(v.shape, lambda i: (0, 0)) for v in vecs]
    in_specs += [pl.BlockSpec(memory_space=pl.ANY)] * nx
    out_specs = [pl.BlockSpec((tm, w), lambda i: (i, 0)) for (w, _) in out_defs]
    out_specs += [pl.BlockSpec((1, w), lambda i: (0, 0)) for w in red_defs]
    out_shape = [jax.ShapeDtypeStruct((t_rows, w), dt) for (w, dt) in out_defs]
    out_shape += [jax.ShapeDtypeStruct((1, w), F32) for w in red_defs]
    res = pl.pallas_call(
        body, name=name, grid=(t_rows // tm,), in_specs=in_specs, out_specs=out_specs, out_shape=out_shape,
        compiler_params=_params(("arbitrary",)),
    )(*[r[0] for r in rows], *vecs, *deps)
    return res


def _rmsnorm(name, x, g, tm=256, deps=()):
    return _rowwise(name, lambda r, v: ([_rms_fwd(r[0].astype(F32), v[0])], []), [x], [g],
                    [(x.shape[1], BF16)], tm=tm, deps=deps)[0]


def _mm_nn(name, a, b, b_block, b_idx, nk, n_out, out_dtype, res=None, alpha=1.0, norm_g=None, tm=512, slabs=1):
    t_rows = a.shape[0]
    bk, tn = slabs * b_block[-2], b_block[-1]
    tm = _tile(t_rows, tm)
    nj = n_out // tn
    has_res = res is not None
    has_norm = norm_g is not None
    assert not has_norm or nj == 1

    def body(*refs):
        a_ref, b_ref = refs[0], refs[1]
        res_ref = refs[2] if has_res else None
        g_ref = refs[2 + has_res] if has_norm else None
        o_ref = refs[2 + has_res + has_norm]
        n_ref = refs[3 + has_res + has_norm] if has_norm else None
        k = pl.program_id(2)
        w = b_ref[...].reshape(bk, tn) if slabs > 1 else b_ref[...]
        p = jnp.dot(a_ref[...], w, preferred_element_type=F32)

        def finish(r):
            if has_res:
                r = res_ref[...] + alpha * r
            o_ref[...] = r.astype(o_ref.dtype)
            if has_norm:
                n_ref[...] = _rms_fwd(r, g_ref[...]).astype(n_ref.dtype)

        if nk == 1:
            finish(p)
            return
        acc_ref = refs[3 + has_res + 2 * has_norm]

        @pl.when(k == 0)
        def _():
            acc_ref[...] = p

        @pl.when(k > 0)
        def _():
            acc_ref[...] += p

        @pl.when(k == nk - 1)
        def _():
            finish(acc_ref[...])

    in_specs = [pl.BlockSpec((tm, bk), lambda j, i, k: (i, k)),
                pl.BlockSpec(b_block, lambda j, i, k: b_idx(j, k))]
    args = [a, b]
    if has_res:
        in_specs.append(pl.BlockSpec((tm, tn), lambda j, i, k: (i, j)))
        args.append(res)
    tile = pl.BlockSpec((tm, tn), lambda j, i, k: (i, j))
    out_specs, out_shape = tile, jax.ShapeDtypeStruct((t_rows, n_out), out_dtype)
    if has_norm:
        in_specs.append(pl.BlockSpec((1, n_out), lambda j, i, k: (0, 0)))
        args.append(norm_g)
        out_specs, out_shape = [tile, tile], [out_shape, jax.ShapeDtypeStruct((t_rows, n_out), BF16)]
    return pl.pallas_call(
        body, name=name, grid=(nj, t_rows // tm, nk), in_specs=in_specs, out_specs=out_specs, out_shape=out_shape,
        scratch_shapes=[pltpu.VMEM((tm, tn), F32)] if nk > 1 else [],
        compiler_params=_params(("arbitrary", "arbitrary", "arbitrary")),
    )(*args)


def _dot_nt(x, w):
    return lax.dot_general(x, w, (((1,), (1,)), ((), ())), preferred_element_type=F32)


def _dot_tn(x, y):
    return lax.dot_general(x, y, (((0,), (0,)), ((), ())), preferred_element_type=F32)


def _mm_nt_cols(name, pairs, ns, bn, out_defs, epi=None, extras=(), tm=512, deps=(), slabs=1, row_parts=1):
    t_rows = pairs[0][0].shape[0]
    tm = _tile(t_rows, tm)
    npair, nex, no, nx = len(pairs), len(extras), len(out_defs), len(deps)
    ns, bn = ns // slabs, bn * slabs
    rp = tm // row_parts

    def body(*refs):
        ws = [refs[2 * p + 1][...] for p in range(npair)]
        if slabs > 1:
            ws = [w.reshape(bn, w.shape[-1]) for w in ws]
        for part_i in range(row_parts):
            rows = slice(part_i * rp, (part_i + 1) * rp)
            acc = None
            for p in range(npair):
                part = _dot_nt(refs[2 * p][rows, :], ws[p])
                acc = part if acc is None else acc + part
            ex = [r[rows, :] for r in refs[2 * npair:2 * npair + nex]]
            outs = epi(acc, *ex) if epi is not None else (acc,)
            for ref, val in zip(refs[2 * npair + nex + nx:], outs):
                ref[rows, :] = val.astype(ref.dtype)

    in_specs, args = [], []
    for (dy, w, w_block, w_idx) in pairs:
        in_specs.append(pl.BlockSpec((tm, dy.shape[1]), lambda s, i: (i, 0)))
        in_specs.append(pl.BlockSpec(w_block, functools.partial(lambda s, i, f: f(s), f=w_idx)))
        args += [dy, w]
    for e in extras:
        in_specs.append(pl.BlockSpec((tm, bn), lambda s, i: (i, s)))
        args.append(e)
    in_specs += [pl.BlockSpec(memory_space=pl.ANY)] * nx
    args += list(deps)
    res = pl.pallas_call(
        body, name=name, grid=(ns, t_rows // tm), in_specs=in_specs,
        out_specs=[pl.BlockSpec((tm, bn), lambda s, i: (i, s)) for _ in range(no)],
        out_shape=[jax.ShapeDtypeStruct((t_rows, ns * bn), dt) for dt in out_defs],
        compiler_params=_params(("arbitrary", "arbitrary")),
    )(*args)
    return res


def _mm_nt_k(name, pairs, ns, n_out, out_dtype, tm=512, deps=()):
    t_rows = pairs[0][0].shape[0]
    tm = _tile(t_rows, tm)
    npair, nx = len(pairs), len(deps)

    def body(*refs):
        o_ref, acc_ref = refs[2 * npair + nx], refs[2 * npair + nx + 1]
        s = pl.program_id(1)
        acc = None
        for p in range(npair):
            part = _dot_nt(refs[2 * p][...], refs[2 * p + 1][...])
            acc = part if acc is None else acc + part

        @pl.when(s == 0)
        def _():
            acc_ref[...] = acc

        @pl.when(s > 0)
        def _():
            acc_ref[...] += acc

        @pl.when(s == ns - 1)
        def _():
            o_ref[...] = acc_ref[...].astype(o_ref.dtype)

    in_specs, args = [], []
    for (a, w, w_block, w_idx) in pairs:
        in_specs.append(pl.BlockSpec((tm, w_block[-1]), lambda i, s: (i, s)))
        in_specs.append(pl.BlockSpec(w_block, functools.partial(lambda i, s, f: f(s), f=w_idx)))
        args += [a, w]
    in_specs += [pl.BlockSpec(memory_space=pl.ANY)] * nx
    args += list(deps)
    return pl.pallas_call(
        body, name=name, grid=(t_rows // tm, ns), in_specs=in_specs,
        out_specs=pl.BlockSpec((tm, n_out), lambda i, s: (i, 0)),
        out_shape=jax.ShapeDtypeStruct((t_rows, n_out), out_dtype),
        scratch_shapes=[pltpu.VMEM((tm, n_out), F32)],
        compiler_params=_params(("arbitrary", "arbitrary")),
    )(*args)


def _mm_tn(name, a, b, bk, bn, n_p, n_q, out_shape, out_block, out_idx, into=None, a_off=0, b_off=0, tt=512,
           deps=()):
    t_rows = a.shape[0]
    tt = _tile(t_rows, tt, 16)
    nt = t_rows // tt
    has_into = into is not None
    nx = len(deps)

    def body(*refs):
        a_ref, b_ref = refs[0], refs[1]
        o_ref, acc_ref = refs[2 + has_into + nx], refs[3 + has_into + nx]
        t = pl.program_id(2)
        part = _dot_tn(a_ref[...], b_ref[...])

        @pl.when(t == 0)
        def _():
            acc_ref[...] = part

        @pl.when(t > 0)
        def _():
            acc_ref[...] += part

        @pl.when(t == nt - 1)
        def _():
            o_ref[...] = acc_ref[...].astype(o_ref.dtype)

    in_specs = [pl.BlockSpec((tt, bk), lambda p, q, t: (t, p + a_off)),
                pl.BlockSpec((tt, bn), lambda p, q, t: (t, q + b_off))]
    args = [a, b]
    aliases = {}
    if has_into:
        in_specs.append(pl.BlockSpec(memory_space=pl.ANY))
        args.append(into)
        aliases = {2: 0}
        out_shape = jax.ShapeDtypeStruct(into.shape, into.dtype)
    in_specs += [pl.BlockSpec(memory_space=pl.ANY)] * nx
    args += list(deps)
    return pl.pallas_call(
        body, name=name, grid=(n_p, n_q, nt), in_specs=in_specs,
        out_specs=pl.BlockSpec(out_block, lambda p, q, t: out_idx(p, q)),
        out_shape=out_shape, scratch_shapes=[pltpu.VMEM((bk, bn), F32)],
        input_output_aliases=aliases,
        compiler_params=_params(("arbitrary", "arbitrary", "arbitrary")),
    )(*args)


def _ffn_up(name, n, ga, slots, cols, d_model, fs, into=None, tm=512):
    t_rows = n.shape[0]
    tm = _tile(t_rows, tm)
    n_sh = slots.shape[0]
    has_into = into is not None

    row_parts = 2 if tm % 32 == 0 else 1

    def body(slot_ref, col_ref, n_ref, wg_ref, wu_ref, *refs):
        a_ref, b_ref, h_ref = refs[3 * has_into:]
        wg, wu = wg_ref[...], wu_ref[...]
        for part in range(row_parts):
            rows = slice(part * (tm // row_parts), (part + 1) * (tm // row_parts))
            x = n_ref[rows, :]
            a = jnp.dot(x, wg, preferred_element_type=F32)
            b = jnp.dot(x, wu, preferred_element_type=F32)
            a_ref[rows, :] = a.astype(a_ref.dtype)
            b_ref[rows, :] = b.astype(b_ref.dtype)
            h_ref[rows, :] = (a * _sigmoid_approx(a) * b).astype(h_ref.dtype)

    w_block = (None, None, d_model, fs)
    out = jax.ShapeDtypeStruct((t_rows, N_CHIPS * fs), BF16)
    in_specs = [pl.BlockSpec((tm, d_model), lambda s, i, sl, co: (i, 0)),
                pl.BlockSpec(w_block, lambda s, i, sl, co: (sl[s], 0, 0, 0)),
                pl.BlockSpec(w_block, lambda s, i, sl, co: (sl[s], 1, 0, 0))]
    args = [slots, cols, n, ga, ga]
    aliases = {}
    if has_into:
        in_specs += [pl.BlockSpec(memory_space=pl.ANY)] * 3
        args += list(into)
        aliases = {5: 0, 6: 1, 7: 2}
    return pl.pallas_call(
        body, name=name,
        grid_spec=pltpu.PrefetchScalarGridSpec(
            num_scalar_prefetch=2, grid=(n_sh, t_rows // tm), in_specs=in_specs,
            out_specs=[pl.BlockSpec((tm, fs), lambda s, i, sl, co: (i, co[s]))] * 3),
        out_shape=[out, out, out], input_output_aliases=aliases,
        compiler_params=_params(("arbitrary", "arbitrary")),
    )(*args)


def _swiglu_bwd(dh, a, b):
    a = a.astype(F32)
    b = b.astype(F32)
    sg = _sigmoid_approx(a)
    return dh * b * sg * (1.0 + a * (1.0 - sg)), dh * a * sg


def _attn_fwd(q, k, v, tm=512):
    t_rows, d_model = q.shape
    n_mem = k.shape[0]
    hd = d_model // MEM_HEADS
    scale = hd ** -0.5
    tm = _tile(t_rows, tm)

    def body(q_ref, k_ref, v_ref, o_ref):
        for h in range(MEM_HEADS):
            cols = slice(h * hd, (h + 1) * hd)
            s = _dot_nt(q_ref[:, cols], k_ref[:, cols]) * scale
            s = s - jnp.max(s, axis=-1, keepdims=True)
            e = jnp.exp(s)
            p = e / jnp.sum(e, axis=-1, keepdims=True)
            o_ref[:, cols] = jnp.dot(p.astype(BF16), v_ref[:, cols], preferred_element_type=F32).astype(o_ref.dtype)

    return pl.pallas_call(
        body, name="attn_fwd", grid=(t_rows // tm,),
        in_specs=[pl.BlockSpec((tm, d_model), lambda i: (i, 0)),
                  pl.BlockSpec((n_mem, d_model), lambda i: (0, 0)),
                  pl.BlockSpec((n_mem, d_model), lambda i: (0, 0))],
        out_specs=pl.BlockSpec((tm, d_model), lambda i: (i, 0)),
        out_shape=jax.ShapeDtypeStruct((t_rows, d_model), BF16),
        compiler_params=_params(("arbitrary",)),
    )(q, k, v)


def _attn_bwd(q, k, v, do, tm=512):
    t_rows, d_model = q.shape
    n_mem = k.shape[0]
    hd = d_model // MEM_HEADS
    scale = hd ** -0.5
    tm = _tile(t_rows, tm, 16)

    def body(q_ref, k_ref, v_ref, do_ref, dq_ref, dk_ref, dv_ref):
        @pl.when(pl.program_id(0) == 0)
        def _():
            dk_ref[...] = jnp.zeros(dk_ref.shape, F32)
            dv_ref[...] = jnp.zeros(dv_ref.shape, F32)

        for h in range(MEM_HEADS):
            cols = slice(h * hd, (h + 1) * hd)
            qh, kh, vh, doh = q_ref[:, cols], k_ref[:, cols], v_ref[:, cols], do_ref[:, cols]
            s = _dot_nt(qh, kh) * scale
            s = s - jnp.max(s, axis=-1, keepdims=True)
            e = jnp.exp(s)
            p = e / jnp.sum(e, axis=-1, keepdims=True)
            dv_ref[:, cols] += _dot_tn(p.astype(BF16), doh)
            dp = _dot_nt(doh, vh)
            ds = (p * (dp - jnp.sum(dp * p, axis=-1, keepdims=True)) * scale).astype(BF16)
            dq_ref[:, cols] = jnp.dot(ds, kh, preferred_element_type=F32).astype(dq_ref.dtype)
            dk_ref[:, cols] += _dot_tn(ds, qh)

    full = pl.BlockSpec((n_mem, d_model), lambda i: (0, 0))
    tile = pl.BlockSpec((tm, d_model), lambda i: (i, 0))
    return pl.pallas_call(
        body, name="attn_bwd", grid=(t_rows // tm,),
        in_specs=[tile, full, full, tile], out_specs=[tile, full, full],
        out_shape=[jax.ShapeDtypeStruct((t_rows, d_model), BF16),
                   jax.ShapeDtypeStruct((n_mem, d_model), F32), jax.ShapeDtypeStruct((n_mem, d_model), F32)],
        compiler_params=_params(("arbitrary",)),
    )(q, k, v, do)


def _split_bf16(v):
    hi = v.astype(BF16)
    return hi, (v - hi.astype(F32)).astype(BF16)


def _pool_window(g):
    return jnp.left_shift(jnp.int32(POOL_WINDOWS[0]), g)


def _pool_fwd(u, col_off, w_pool, scale, tt=256):
    t_rows = u.shape[0]
    pw = w_pool.shape[-1]
    ng = w_pool.shape[0]
    tt = _tile(t_rows, tt, 16)
    nt = t_rows // tt
    assert POOL_WINDOWS == tuple(2 << i for i in range(ng)) and tt >= POOL_WINDOWS[-1]

    def body(vc_ref, vp_ref, w_ref, sc_ref, pooled_ref, z_ref):
        g, i = pl.program_id(0), pl.program_id(1)
        w = _pool_window(g)
        r = lax.broadcasted_iota(jnp.int32, (tt, tt), 0)
        c = lax.broadcasted_iota(jnp.int32, (tt, tt), 1)
        band_c = ((c <= r) & (c > r - w)).astype(BF16)
        band_p = (c > r - w + tt).astype(BF16)
        vc = vc_ref[...]
        ch, cl = _split_bf16(vc)
        ph, plo = _split_bf16(vp_ref[...] * (i > 0).astype(F32))
        sums = (jnp.dot(band_c, ch, preferred_element_type=F32) + jnp.dot(band_c, cl, preferred_element_type=F32)
                + jnp.dot(band_p, ph, preferred_element_type=F32) + jnp.dot(band_p, plo, preferred_element_type=F32))
        t = i * tt + lax.broadcasted_iota(jnp.int32, (tt, 1), 0)
        cnt = jnp.minimum(t + 1, w).astype(F32)
        pooled = (sums / cnt - vc).astype(BF16)
        pooled_ref[...] = pooled
        z_ref[...] = jnp.dot(pooled, w_ref[...], preferred_element_type=F32) * sc_ref[...]

    return pl.pallas_call(
        body, name="pool_fwd", grid=(ng, nt),
        in_specs=[pl.BlockSpec((tt, pw), lambda g, i: (i, col_off + g)),
                  pl.BlockSpec((tt, pw), lambda g, i: (jnp.maximum(i - 1, 0), col_off + g)),
                  pl.BlockSpec((None, pw, pw), lambda g, i: (g, 0, 0)),
                  pl.BlockSpec((1, pw), lambda g, i: (0, g))],
        out_specs=[pl.BlockSpec((tt, pw), lambda g, i: (i, g))] * 2,
        out_shape=[jax.ShapeDtypeStruct((t_rows, ng * pw), BF16), jax.ShapeDtypeStruct((t_rows, ng * pw), F32)],
        compiler_params=_params(("arbitrary", "arbitrary")),
    )(u, u, w_pool, scale)


def _pool_bwd1(dz, pooled, w_pool, scale, tt=256):
    t_rows = dz.shape[0]
    pw = w_pool.shape[-1]
    ng = w_pool.shape[0]
    tt = _tile(t_rows, tt, 16)
    nt = t_rows // tt

    def body(dz_ref, p_ref, w_ref, sc_ref, dp_ref, dw_ref, dsc_ref):
        g, i = pl.program_id(0), pl.program_id(1)
        w = _pool_window(g)

        @pl.when(i == 0)
        def _():
            dw_ref[...] = jnp.zeros(dw_ref.shape, F32)
            dsc_ref[...] = jnp.zeros(dsc_ref.shape, F32)

        dz_v = dz_ref[...]
        pooled = p_ref[...]
        zpre = jnp.dot(pooled, w_ref[...], preferred_element_type=F32)
        dsc_ref[...] += _colsum(dz_v * zpre)
        dzs = (dz_v * sc_ref[...]).astype(BF16)
        dw_ref[...] += _dot_tn(pooled, dzs)
        t = i * tt + lax.broadcasted_iota(jnp.int32, (tt, 1), 0)
        cnt = jnp.minimum(t + 1, w).astype(F32)
        dp_ref[...] = _dot_nt(dzs, w_ref[...]) / cnt

    return pl.pallas_call(
        body, name="pool_bwd1", grid=(ng, nt),
        in_specs=[pl.BlockSpec((tt, pw), lambda g, i: (i, g)),
                  pl.BlockSpec((tt, pw), lambda g, i: (i, g)),
                  pl.BlockSpec((None, pw, pw), lambda g, i: (g, 0, 0)),
                  pl.BlockSpec((1, pw), lambda g, i: (0, g))],
        out_specs=[pl.BlockSpec((tt, pw), lambda g, i: (i, g)),
                   pl.BlockSpec((None, pw, pw), lambda g, i: (g, 0, 0)),
                   pl.BlockSpec((1, pw), lambda g, i: (0, g))],
        out_shape=[jax.ShapeDtypeStruct((t_rows, ng * pw), F32), jax.ShapeDtypeStruct((ng, pw, pw), F32),
                   jax.ShapeDtypeStruct((1, ng * pw), F32)],
        compiler_params=_params(("arbitrary", "arbitrary")),
    )(dz, pooled, w_pool, scale)


def _pool_bwd2(dps, ng, tt=256):
    t_rows, width = dps.shape
    pw = width // ng
    tt = _tile(t_rows, tt, 16)
    nt = t_rows // tt

    def body(dc_ref, dn_ref, dv_ref):
        g, i = pl.program_id(0), pl.program_id(1)
        w = _pool_window(g)
        r = lax.broadcasted_iota(jnp.int32, (tt, tt), 0)
        c = lax.broadcasted_iota(jnp.int32, (tt, tt), 1)
        band_c = ((c >= r) & (c < r + w)).astype(BF16)
        band_n = (c < r + w - tt).astype(BF16)
        dc = dc_ref[...]
        ch, cl = _split_bf16(dc)
        nh, nl = _split_bf16(dn_ref[...] * (i < nt - 1).astype(F32))
        sums = (jnp.dot(band_c, ch, preferred_element_type=F32) + jnp.dot(band_c, cl, preferred_element_type=F32)
                + jnp.dot(band_n, nh, preferred_element_type=F32) + jnp.dot(band_n, nl, preferred_element_type=F32))
        t = i * tt + lax.broadcasted_iota(jnp.int32, (tt, 1), 0)
        cnt = jnp.minimum(t + 1, w).astype(F32)
        dv_ref[...] = sums - dc * cnt

    return pl.pallas_call(
        body, name="pool_bwd2", grid=(ng, nt),
        in_specs=[pl.BlockSpec((tt, pw), lambda g, i: (i, g)),
                  pl.BlockSpec((tt, pw), lambda g, i: (jnp.minimum(i + 1, nt - 1), g))],
        out_specs=pl.BlockSpec((tt, pw), lambda g, i: (i, g)),
        out_shape=jax.ShapeDtypeStruct((t_rows, width), F32),
        compiler_params=_params(("arbitrary", "arbitrary")),
    )(dps, dps)


def _cpow(ar, ai, n):
    rr, ri, br, bi = None, None, ar, ai
    while n:
        if n & 1:
            rr, ri = (br, bi) if rr is None else (rr * br - ri * bi, rr * bi + ri * br)
        n >>= 1
        if n:
            br, bi = br * br - bi * bi, 2.0 * br * bi
    return rr, ri


def _chunk_carries(st_re, st_im, pr, pi, order):
    cb = st_re.shape[1]
    sub = lax.broadcasted_iota(jnp.int32, (S5_LANES, cb), 0)
    cr = jnp.zeros((S5_LANES, cb), F32)
    ci = jnp.zeros((S5_LANES, cb), F32)
    prev_r = jnp.zeros((1, cb), F32)
    prev_i = jnp.zeros((1, cb), F32)
    for k, src in order:
        er, ei = st_re[src:src + 1, :], st_im[src:src + 1, :]
        nr = er + pr * prev_r - pi * prev_i
        ni = ei + pr * prev_i + pi * prev_r
        cr = jnp.where(sub == k, jnp.broadcast_to(nr, (S5_LANES, cb)), cr)
        ci = jnp.where(sub == k, jnp.broadcast_to(ni, (S5_LANES, cb)), ci)
        prev_r, prev_i = nr, ni
    return cr, ci


def _s5_fwd(up, bblk, ab, cblk, tt=128):
    n_rows, ws = up.shape
    nb, cw, cb2 = bblk.shape
    cb = cb2 // 2
    lc = n_rows // S5_LANES
    tt = _tile(lc, tt, 1)
    nt = lc // tt
    rt = S5_LANES * tt

    def body(u_ref, b_ref, ab_ref, c_ref, y_ref, s_ref, bu_ref, st_re, st_im):
        ps, ti = pl.program_id(1), pl.program_id(2)
        ar = jnp.broadcast_to(ab_ref[0:1, :], (S5_LANES, cb))
        ai = jnp.broadcast_to(ab_ref[1:2, :], (S5_LANES, cb))

        @pl.when((ps == 0) & (ti == 0))
        def _():
            st_re[...] = jnp.zeros(st_re.shape, F32)
            st_im[...] = jnp.zeros(st_im.shape, F32)

        @pl.when((ps == 1) & (ti == 0))
        def _():
            pr, pi = _cpow(ab_ref[0:1, :], ab_ref[1:2, :], lc)
            cr, ci = _chunk_carries(st_re, st_im, pr, pi, [(k, k - 1) for k in range(1, S5_LANES)])
            st_re[...] = cr
            st_im[...] = ci

        bu_ref[...] = jnp.dot(u_ref[...], b_ref[...], preferred_element_type=F32)

        def step(t, carry, store):
            sr, si = carry
            rows = pl.ds(pl.multiple_of(t * S5_LANES, S5_LANES), S5_LANES)
            nr = ar * sr - ai * si + bu_ref[rows, 0:cb]
            ni = ar * si + ai * sr + bu_ref[rows, cb:cb2]
            if store:
                s_ref[rows, 0:cb] = nr
                s_ref[rows, cb:cb2] = ni
            return nr, ni

        @pl.when(ps == 0)
        def _():
            sr, si = lax.fori_loop(0, tt, functools.partial(step, store=False), (st_re[...], st_im[...]))
            st_re[...] = sr
            st_im[...] = si

        @pl.when(ps == 1)
        def _():
            sr, si = lax.fori_loop(0, tt, functools.partial(step, store=True), (st_re[...], st_im[...]))
            st_re[...] = sr
            st_im[...] = si
            y_ref[...] = jnp.dot(s_ref[...].astype(BF16), c_ref[...], preferred_element_type=F32)

    return pl.pallas_call(
        body, name="s5_fwd", grid=(nb, 2, nt),
        in_specs=[pl.BlockSpec((rt, cw), lambda j, ps, ti: (ti, j)),
                  pl.BlockSpec((None, cw, cb2), lambda j, ps, ti: (j, 0, 0)),
                  pl.BlockSpec((None, 2, cb), lambda j, ps, ti: (j, 0, 0)),
                  pl.BlockSpec((None, cb2, cw), lambda j, ps, ti: (j, 0, 0))],
        out_specs=[pl.BlockSpec((rt, cw), lambda j, ps, ti: (ti * ps, j)),
                   pl.BlockSpec((None, rt, cb2), lambda j, ps, ti: (j, ti * ps, 0))],
        out_shape=[jax.ShapeDtypeStruct((n_rows, ws), F32), jax.ShapeDtypeStruct((nb, n_rows, cb2), F32)],
        scratch_shapes=[pltpu.VMEM((rt, cb2), F32), pltpu.VMEM((S5_LANES, cb), F32), pltpu.VMEM((S5_LANES, cb), F32)],
        compiler_params=_params(("arbitrary", "arbitrary", "arbitrary")),
    )(up, bblk, ab, cblk)


def _s5_bwd(dyp, up, s_all, bblk_t, ab, cblk_t, tt=128):
    n_rows, ws = up.shape
    nb, cb2, cw = bblk_t.shape
    cb = cb2 // 2
    lc = n_rows // S5_LANES
    tt = _tile(lc, tt, 1)
    nt = lc // tt
    rt = S5_LANES * tt

    def body(dy_ref, u_ref, s_ref, bt_ref, ab_ref, ct_ref, du_ref, db_ref, dc_ref, da_ref, ds_ref, st_re, st_im):
        ps, ti = pl.program_id(1), pl.program_id(2)
        ar = jnp.broadcast_to(ab_ref[0:1, :], (S5_LANES, cb))
        ai = jnp.broadcast_to(ab_ref[1:2, :], (S5_LANES, cb))

        @pl.when((ps == 0) & (ti == 0))
        def _():
            st_re[...] = jnp.zeros(st_re.shape, F32)
            st_im[...] = jnp.zeros(st_im.shape, F32)
            db_ref[...] = jnp.zeros(db_ref.shape, F32)
            dc_ref[...] = jnp.zeros(dc_ref.shape, F32)
            da_ref[...] = jnp.zeros(da_ref.shape, F32)

        @pl.when((ps == 1) & (ti == 0))
        def _():
            pr, pi = _cpow(ab_ref[0:1, :], -ab_ref[1:2, :], lc)
            cr, ci = _chunk_carries(st_re, st_im, pr, pi, [(k, k + 1) for k in range(S5_LANES - 2, -1, -1)])
            st_re[...] = cr
            st_im[...] = ci

        ds_ref[...] = jnp.dot(dy_ref[...], ct_ref[...], preferred_element_type=F32)

        def rows_of(i):
            return pl.ds(pl.multiple_of((tt - 1 - i) * S5_LANES, S5_LANES), S5_LANES)

        def step0(i, carry):
            gr, gi = carry
            rows = rows_of(i)
            return (ar * gr + ai * gi + ds_ref[rows, 0:cb], ar * gi - ai * gr + ds_ref[rows, cb:cb2])

        def step1(i, carry):
            gr, gi, acr, aci = carry
            rows = rows_of(i)
            sr, si = s_ref[rows, 0:cb], s_ref[rows, cb:cb2]
            acr = acr + sr * gr + si * gi
            aci = aci + sr * gi - si * gr
            nr = ar * gr + ai * gi + ds_ref[rows, 0:cb]
            ni = ar * gi - ai * gr + ds_ref[rows, cb:cb2]
            ds_ref[rows, 0:cb] = nr
            ds_ref[rows, cb:cb2] = ni
            return nr, ni, acr, aci

        @pl.when(ps == 0)
        def _():
            gr, gi = lax.fori_loop(0, tt, step0, (st_re[...], st_im[...]))
            st_re[...] = gr
            st_im[...] = gi

        @pl.when(ps == 1)
        def _():
            zero = jnp.zeros((S5_LANES, cb), F32)
            gr, gi, acr, aci = lax.fori_loop(0, tt, step1, (st_re[...], st_im[...], zero, zero))
            st_re[...] = gr
            st_im[...] = gi
            da_ref[0] += acr
            da_ref[1] += aci
            dsb = ds_ref[...].astype(BF16)
            du_ref[...] = jnp.dot(dsb, bt_ref[...], preferred_element_type=F32)
            db_ref[...] += _dot_tn(u_ref[...], dsb)
            dc_ref[...] += _dot_tn(dy_ref[...], s_ref[...].astype(BF16))

    def tile_idx(ps, ti):
        return (nt - 1 - ti) * ps + (nt - 1) * (1 - ps)

    return pl.pallas_call(
        body, name="s5_bwd", grid=(nb, 2, nt),
        in_specs=[pl.BlockSpec((rt, cw), lambda j, ps, ti: (nt - 1 - ti, j)),
                  pl.BlockSpec((rt, cw), lambda j, ps, ti: (tile_idx(ps, ti), j)),
                  pl.BlockSpec((None, rt, cb2), lambda j, ps, ti: (j, tile_idx(ps, ti), 0)),
                  pl.BlockSpec((None, cb2, cw), lambda j, ps, ti: (j, 0, 0)),
                  pl.BlockSpec((None, 2, cb), lambda j, ps, ti: (j, 0, 0)),
                  pl.BlockSpec((None, cw, cb2), lambda j, ps, ti: (j, 0, 0))],
        out_specs=[pl.BlockSpec((rt, cw), lambda j, ps, ti: (tile_idx(ps, ti), j)),
                   pl.BlockSpec((None, cw, cb2), lambda j, ps, ti: (j, 0, 0)),
                   pl.BlockSpec((None, cw, cb2), lambda j, ps, ti: (j, 0, 0)),
                   pl.BlockSpec((None, 2, S5_LANES, cb), lambda j, ps, ti: (j, 0, 0, 0))],
        out_shape=[jax.ShapeDtypeStruct((n_rows, ws), F32), jax.ShapeDtypeStruct((nb, cw, cb2), F32),
                   jax.ShapeDtypeStruct((nb, cw, cb2), F32), jax.ShapeDtypeStruct((nb, 2, S5_LANES, cb), F32)],
        scratch_shapes=[pltpu.VMEM((rt, cb2), F32), pltpu.VMEM((S5_LANES, cb), F32), pltpu.VMEM((S5_LANES, cb), F32)],
        compiler_params=_params(("arbitrary", "arbitrary", "arbitrary")),
    )(dyp, up, s_all, bblk_t, ab, cblk_t)


def _s5_discretize(a_re, a_im, log_dt, b_re, b_im):
    dt = jnp.exp(log_dt)[:, None]
    mag = jnp.exp(a_re * dt)
    abar_re = mag * jnp.cos(a_im * dt)
    abar_im = mag * jnp.sin(a_im * dt)
    nr, ni = abar_re - 1.0, abar_im
    den = a_re * a_re + a_im * a_im
    fr = (nr * a_re + ni * a_im) / den
    fi = (ni * a_re - nr * a_im) / den
    bbar_re = fr[..., None] * b_re - fi[..., None] * b_im
    bbar_im = fr[..., None] * b_im + fi[..., None] * b_re
    return abar_re, abar_im, bbar_re, bbar_im


def _perm_rows(a):
    n, c = a.shape
    return a.reshape(S5_LANES, n // S5_LANES, c).transpose(1, 0, 2).reshape(n, c)


def _unperm_rows(a):
    n, c = a.shape
    return a.reshape(n // S5_LANES, S5_LANES, c).transpose(1, 0, 2).reshape(n, c)


HBM = pl.BlockSpec(memory_space=pltpu.HBM)
SEM = pl.BlockSpec(memory_space=pltpu.SEMAPHORE)
ANY = pl.BlockSpec(memory_space=pl.ANY)
EFFECT = pltpu.SideEffectType.DATAFLOW_SIDE_EFFECTING
COPIES_PER_BUFFER = {"ag_ici": 3, "ag_fwd": 3, "pair": N_CHIPS, "scatter": 3, "half": 1, "swap": 1, "bcast": 3}
PAIRED_KINDS = ("pair", "scatter", "swap")


def _place():
    x, y, c = lax.axis_index("x"), lax.axis_index("y"), lax.axis_index("c")
    chips = [(1 - x, y), (x, 1 - y), (1 - x, 1 - y)]
    return x, y, c, 2 * x + y, chips


def _n_copies(kind, n_bufs):
    if isinstance(kind, tuple):
        return len(kind[1])
    return COPIES_PER_BUFFER[kind] * (n_bufs // 2 if kind in PAIRED_KINDS else n_bufs)


def _comm_copies(kind, bufs):
    if isinstance(kind, tuple):
        full = _comm_copies(kind[0], bufs)
        return [full[k] for k in kind[1]]
    x, y, c, s, chips = _place()
    sib = (x, y, 1 - c)
    out = []
    if kind == "ag_ici":
        for w in bufs:
            for cx, cy in chips:
                out.append((w.at[s, c], w.at[s, c], w.at[2 * cx + cy, c], (cx, cy, c)))
    elif kind == "ag_fwd":
        for w in bufs:
            for cx, cy in chips:
                sj = 2 * cx + cy
                out.append((w.at[sj, c], w.at[sj, c], w.at[sj, 1 - c], sib))
    elif kind == "pair":
        n = len(bufs) // 2
        for g, got in zip(bufs[:n], bufs[n:]):
            for t in range(N_CHIPS):
                out.append((g.at[t, 1 - c], got.at[t], got.at[t], sib))
    elif kind == "scatter":
        n = len(bufs) // 2
        for p, got in zip(bufs[:n], bufs[n:]):
            for cx, cy in chips:
                out.append((p.at[2 * cx + cy], got.at[s], got.at[2 * cx + cy], (cx, cy, c)))
    elif kind == "half":
        for f in bufs:
            out.append((f.at[c], f.at[c], f.at[1 - c], sib))
    elif kind == "swap":
        n = len(bufs) // 2
        for v, got in zip(bufs[:n], bufs[n:]):
            out.append((v, got, got, sib))
    elif kind == "bcast":
        for w in bufs:
            for cx, cy in chips:
                out.append((w.at[s], w.at[s], w.at[2 * cx + cy], (cx, cy, c)))
    return out


def _comm_fused(name, kind, bufs):
    n = len(bufs)
    ncp = _n_copies(kind, n)

    def body(*refs):
        outs = refs[n:2 * n]
        send, recv = refs[2 * n:]
        copies = _comm_copies(kind, outs)
        started = []
        for k, (src, dst, _, peer) in enumerate(copies):
            cp = pltpu.make_async_remote_copy(src_ref=src, dst_ref=dst, send_sem=send.at[k], recv_sem=recv.at[k],
                                              device_id=peer, device_id_type=MESH)
            cp.start()
            started.append(cp)
        for k, (_, _, land, peer) in enumerate(copies):
            pltpu.make_async_remote_copy(src_ref=land, dst_ref=land, send_sem=send.at[k], recv_sem=recv.at[k],
                                         device_id=peer, device_id_type=MESH).wait_recv()
        for cp in started:
            cp.wait_send()

    return pl.pallas_call(
        body, name=name, in_specs=[ANY] * n, out_specs=[ANY] * n,
        out_shape=[jax.ShapeDtypeStruct(b.shape, b.dtype) for b in bufs],
        input_output_aliases={k: k for k in range(n)},
        scratch_shapes=[pltpu.SemaphoreType.DMA((ncp,))] * 2,
    )(*bufs)


def _comm_start(name, kind, bufs, after=None):
    n = len(bufs)
    ncp = _n_copies(kind, n)
    nx = 0 if after is None else 1

    def body(*refs):
        refs = refs[n + nx:]
        send, recv = refs[:ncp], refs[ncp:2 * ncp]
        outs = refs[2 * ncp:n + 2 * ncp]
        token = refs[n + 2 * ncp]
        for k, (src, dst, _, peer) in enumerate(_comm_copies(kind, outs)):
            pltpu.make_async_remote_copy(src_ref=src, dst_ref=dst, send_sem=send[k], recv_sem=recv[k],
                                         device_id=peer, device_id_type=MESH).start()
        token[...] = jnp.zeros(token.shape, token.dtype)

    res = pl.pallas_call(
        body, name=name, in_specs=[HBM] * n + [ANY] * nx,
        out_specs=[SEM] * (2 * ncp) + [HBM] * n + [pl.BlockSpec(memory_space=pltpu.VMEM)],
        out_shape=[pltpu.SemaphoreType.DMA(())] * (2 * ncp) + [pltpu.HBM(b.shape, b.dtype) for b in bufs]
        + [jax.ShapeDtypeStruct((8, 128), F32)],
        input_output_aliases={k: 2 * ncp + k for k in range(n)},
        compiler_params=pltpu.CompilerParams(has_side_effects=EFFECT),
    )(*[pltpu.with_memory_space_constraint(b, pltpu.HBM) for b in bufs], *([after] if nx else []))
    return list(res[:ncp]), list(res[ncp:2 * ncp]), list(res[2 * ncp:2 * ncp + n]), res[2 * ncp + n]


def _comm_wait(name, kind, bufs, send_sems, recv_sems, after):
    n = len(bufs)
    ncp = _n_copies(kind, n)

    def body(*refs):
        send, recv = refs[n:n + ncp], refs[n + ncp:n + 2 * ncp]
        outs = refs[n + 2 * ncp + 1:]
        for k, (src, _, land, peer) in enumerate(_comm_copies(kind, outs)):
            cp = pltpu.make_async_remote_copy(src_ref=src, dst_ref=land, send_sem=send[k], recv_sem=recv[k],
                                              device_id=peer, device_id_type=MESH)
            cp.wait_send()
            cp.wait_recv()

    return pl.pallas_call(
        body, name=name, in_specs=[HBM] * n + [SEM] * (2 * ncp) + [ANY], out_specs=[HBM] * n,
        out_shape=[pltpu.HBM(b.shape, b.dtype) for b in bufs],
        input_output_aliases={k: k for k in range(n)},
        compiler_params=pltpu.CompilerParams(has_side_effects=EFFECT),
    )(*bufs, *send_sems, *recv_sems, after)


def _pair_add(g, got, core):
    nchip, _, r, cw = g.shape
    tr = _tile(r, 512, 16)

    def body(c_ref, a_ref, b_ref, o_ref):
        o_ref[...] = a_ref[...] + b_ref[...]

    return pl.pallas_call(
        body, name="grads_pair_add",
        grid_spec=pltpu.PrefetchScalarGridSpec(
            num_scalar_prefetch=1, grid=(nchip, r // tr),
            in_specs=[pl.BlockSpec((None, None, tr, cw), lambda s, i, c_ref: (s, c_ref[0], i, 0)),
                      pl.BlockSpec((None, tr, cw), lambda s, i, c_ref: (s, i, 0))],
            out_specs=pl.BlockSpec((None, tr, cw), lambda s, i, c_ref: (s, i, 0))),
        out_shape=jax.ShapeDtypeStruct((nchip, r, cw), BF16),
        compiler_params=_params(("arbitrary", "arbitrary")),
    )(core, g, got)


def _chip_sum(parts, got, idx):
    _, r, cw = parts.shape
    tr = _tile(r, 512, 16)

    def body(i_ref, own_ref, a_ref, b_ref, c_ref, o_ref):
        o_ref[...] = ((own_ref[...].astype(F32) + a_ref[...].astype(F32)) + b_ref[...].astype(F32)) + c_ref[...].astype(F32)

    def slot(k):
        return pl.BlockSpec((None, tr, cw), lambda i, i_ref: (i_ref[k], i, 0))

    return pl.pallas_call(
        body, name="grads_chip_sum",
        grid_spec=pltpu.PrefetchScalarGridSpec(
            num_scalar_prefetch=1, grid=(r // tr,), in_specs=[slot(0), slot(1), slot(2), slot(3)], out_specs=slot(4)),
        out_shape=jax.ShapeDtypeStruct((2, r, cw), F32),
        compiler_params=_params(("arbitrary",)),
    )(idx, parts, got, got, got)


def _add_into_slot(v, got, chip):
    r, cw = v.shape
    tr = _tile(r, 256)

    def body(c_ref, a_ref, b_ref, o_ref):
        o_ref[...] = a_ref[...] + b_ref[...]

    tile = pl.BlockSpec((tr, cw), lambda i, c_ref: (i, 0))
    return pl.pallas_call(
        body, name="small_pair_add",
        grid_spec=pltpu.PrefetchScalarGridSpec(
            num_scalar_prefetch=1, grid=(r // tr,), in_specs=[tile, tile],
            out_specs=pl.BlockSpec((None, tr, cw), lambda i, c_ref: (c_ref[0], i, 0))),
        out_shape=jax.ShapeDtypeStruct((N_CHIPS, r, cw), F32),
        compiler_params=_params(("arbitrary",)),
    )(chip, v, got)


def _sum_slots(w):
    _, r, cw = w.shape
    tr = _tile(r, 256)

    def body(w_ref, o_ref):
        o_ref[...] = ((w_ref[0] + w_ref[1]) + w_ref[2]) + w_ref[3]

    return pl.pallas_call(
        body, name="small_chip_sum", grid=(r // tr,),
        in_specs=[pl.BlockSpec((N_CHIPS, tr, cw), lambda i: (0, i, 0))],
        out_specs=pl.BlockSpec((tr, cw), lambda i: (i, 0)),
        out_shape=jax.ShapeDtypeStruct((r, cw), F32),
        compiler_params=_params(("arbitrary",)),
    )(w)


def _adamw_math(w, g, m, v):
    m = ADAM_B1 * m + (1.0 - ADAM_B1) * g
    v = ADAM_B2 * v + (1.0 - ADAM_B2) * (g * g)
    m_hat = m / (1.0 - ADAM_B1 ** ADAM_STEP)
    v_hat = v / (1.0 - ADAM_B2 ** ADAM_STEP)
    delta = -ADAM_LR * (m_hat / (jnp.sqrt(v_hat) + ADAM_EPS) + ADAM_WD * w)
    return delta, m, v


def _adamw(name, w, m, v, g, g_half=0, g_row_off=0, tr=256):
    r, cw = w.shape
    tr = _tile(math.gcd(r, g_row_off) if g_row_off else r, tr)
    off = g_row_off // tr

    def body(w_ref, m_ref, v_ref, g_ref, go_ref, d_ref, mo_ref, vo_ref):
        g_v = g_ref[...]
        delta, m_n, v_n = _adamw_math(w_ref[...], g_v, m_ref[...], v_ref[...])
        go_ref[...] = g_v
        d_ref[...] = delta
        mo_ref[...] = m_n
        vo_ref[...] = v_n

    tile = pl.BlockSpec((tr, cw), lambda i: (i, 0))
    out = jax.ShapeDtypeStruct((r, cw), F32)
    return pl.pallas_call(
        body, name=name, grid=(r // tr,),
        in_specs=[tile, tile, tile, pl.BlockSpec((None, tr, cw), lambda i: (g_half, i + off, 0))],
        out_specs=[tile] * 4, out_shape=[out] * 4,
        compiler_params=_params(("arbitrary",)),
    )(w, m, v, g)


def kernel(x, mem, g_ffn1, w1_gate, w1_up, w1_down, g_mix, w_in, ssm_a_re, ssm_a_im, ssm_log_dt, ssm_b_re, ssm_b_im, ssm_c_re, ssm_c_im, ssm_d, w_glu, b_glu, w_pool, pool_scale, g_out_ssm, g_out_pool, w_out, g_xattn, g_mem, w_q, w_k, w_v, w_o, g_ffn2, w2_gate, w2_up, w2_down, g_final, loss_target, m_g_ffn1, m_w1_gate, m_w1_up, m_w1_down, m_g_mix, m_w_in, m_ssm_a_re, m_ssm_a_im, m_ssm_log_dt, m_ssm_b_re, m_ssm_b_im, m_ssm_c_re, m_ssm_c_im, m_ssm_d, m_w_glu, m_b_glu, m_w_pool, m_pool_scale, m_g_out_ssm, m_g_out_pool, m_w_out, m_g_xattn, m_g_mem, m_w_q, m_w_k, m_w_v, m_w_o, m_g_ffn2, m_w2_gate, m_w2_up, m_w2_down, m_g_final, v_g_ffn1, v_w1_gate, v_w1_up, v_w1_down, v_g_mix, v_w_in, v_ssm_a_re, v_ssm_a_im, v_ssm_log_dt, v_ssm_b_re, v_ssm_b_im, v_ssm_c_re, v_ssm_c_im, v_ssm_d, v_w_glu, v_b_glu, v_w_pool, v_pool_scale, v_g_out_ssm, v_g_out_pool, v_w_out, v_g_xattn, v_g_mem, v_w_q, v_w_k, v_w_v, v_w_o, v_g_ffn2, v_w2_gate, v_w2_up, v_w2_down, v_g_final):
    local = dict(locals())
    wts = {n: local[n] for n in WEIGHTS}
    mom = {n: local["m_" + n] for n in WEIGHTS}
    var = {n: local["v_" + n] for n in WEIGHTS}

    x2 = x[0]
    mem2 = mem[0]
    tgt = loss_target[0]
    t_rows, d = x2.shape
    fs = w1_gate.shape[-1]
    ds_ = w_in.shape[1]
    ws = d // 2
    n_pg = len(POOL_WINDOWS)
    pw = ws // n_pg
    n_grp = ws // SSM_GROUP
    n_state = ssm_a_re.shape[-1]
    cx_, cy_, cc_ = lax.axis_index("x"), lax.axis_index("y"), lax.axis_index("c")
    chip = (2 * cx_ + cy_).astype(jnp.int32)
    core = cc_.astype(jnp.int32).reshape(1)
    chip_idx = jnp.stack([chip, chip ^ 2, chip ^ 1, chip ^ 3, cc_.astype(jnp.int32)])

    glu_rows = w_glu[0].reshape(-1, d)
    pool_rows = w_pool[0].reshape(-1, d)
    gh, ph = glu_rows.shape[0] // 2, pool_rows.shape[0] // 2
    rh = -(-(3 * ds_ + gh + ph) // 128) * 128
    pad_rows = jnp.zeros((rh - 3 * ds_ - gh - ph, d), F32)

    def own_slot(src):
        src = src.astype(BF16)
        return lax.dynamic_update_slice(lax.empty((N_CHIPS,) + src.shape, BF16), src[None], (chip, 0, 0, 0))

    src_b2 = jnp.stack([
        jnp.concatenate([w_in[0], w_out[0], w_q[0], glu_rows[:gh], pool_rows[:ph], pad_rows], 0),
        jnp.concatenate([w_k[0], w_v[0], w_o[0], glu_rows[gh:], pool_rows[ph:], pad_rows], 0)])
    src_up1 = jnp.stack([w1_gate[0], w1_up[0]]).astype(BF16)
    w_bufs = [own_slot(src_up1), own_slot(w1_down[0].reshape(2, fs // 2, d)),
              own_slot(src_b2), own_slot(jnp.stack([w2_gate[0], w2_up[0]])),
              own_slot(w2_down[0].reshape(2, fs // 2, d))]
    near_send, near_recv, w0, ag_token = _comm_start("weights_start_near", ("ag_ici", (0, 1)), w_bufs[:1])

    def gathered(k, after):
        w = _comm_wait("weights_wait_%d" % k, "ag_ici", [w_bufs[k]], ag_send[3 * k:3 * k + 3],
                       ag_recv[3 * k:3 * k + 3], after)
        return _comm_fused("weights_forward_%d" % k, "ag_fwd", w)[0]

    def gathered_start(k, after):
        w = _comm_wait("weights_wait_%d" % k, "ag_ici", [w_bufs[k]], ag_send[3 * k:3 * k + 3],
                       ag_recv[3 * k:3 * k + 3], after)
        send, recv, thru, token = _comm_start("weights_forward_start_%d" % k, "ag_fwd", w)
        return (send, recv, thru), token

    def gathered_finish(k, handle, after):
        send, recv, thru = handle
        return _comm_wait("weights_forward_wait_%d" % k, "ag_fwd", thru, send, recv, after)[0]

    n1 = _rmsnorm("norm_ffn1", x2, wts['g_ffn1'].reshape(1, -1), deps=[ag_token])
    zero1 = jnp.zeros((1,), jnp.int32)
    near, far = jnp.stack([chip ^ 2, chip ^ 1]), (chip ^ 3).reshape(1)
    a1, b1, hm1 = _ffn_up("ffn1_up_own", n1, src_up1[None], zero1, chip.reshape(1), d, fs)
    n_rest = 3 * len(w_bufs) - 2
    rest_send, rest_recv, w_bufs, _ = _comm_start("weights_start_rest", ("ag_ici", tuple(range(2, 2 + n_rest))),
                                                  w0 + w_bufs[1:], after=hm1)
    ag_send, ag_recv = near_send + rest_send, near_recv + rest_recv
    w0 = _comm_wait("weights_wait_0_near", ("ag_ici", (0, 1)), [w_bufs[0]], ag_send[0:2], ag_recv[0:2], hm1)
    w0 = _comm_fused("weights_forward_0_near", ("ag_fwd", (0, 1)), w0)
    a1, b1, hm1 = _ffn_up("ffn1_up_near", n1, w0[0], near, near, d, fs, into=(a1, b1, hm1))
    w0 = _comm_wait("weights_wait_0_far", ("ag_ici", (2,)), w0, ag_send[2:3], ag_recv[2:3], hm1)
    ga1 = _comm_fused("weights_forward_0_far", ("ag_fwd", (2,)), w0)[0]
    a1, b1, hm1 = _ffn_up("ffn1_up_far", n1, ga1, far, far, d, fs, into=(a1, b1, hm1))
    gd1 = gathered(1, hm1).reshape(N_CHIPS, fs, d)
    h1, n2 = _mm_nn("ffn1_down", hm1, gd1, (None, fs, d), lambda j, k: (k, 0, 0), N_CHIPS, d, F32, res=x2, alpha=0.5,
                    norm_g=wts['g_mix'].reshape(1, -1))
    gb2 = gathered(2, h1)
    wglu_full = gb2[:, :, 3 * ds_:3 * ds_ + gh, :].reshape(ws, ws)
    wpool_full = gb2[:, :, 3 * ds_ + gh:3 * ds_ + gh + ph, :].reshape(N_CHIPS, n_pg, pw // N_CHIPS, pw)
    wpool_full = wpool_full.transpose(1, 0, 2, 3).reshape(n_pg, pw, pw)
    DD = {'w_in': (0, 0), 'w_out': (0, 1), 'w_q': (0, 2), 'w_k': (1, 0), 'w_v': (1, 1), 'w_o': (1, 2)}

    def mm_dd(name, a, wname, out_dtype, res=None, norm_g=None):
        h, q = DD[wname]
        return _mm_nn(name, a, gb2, (N_CHIPS, None, ds_, d), lambda j, k: (0, h, q, 0), 1, d, out_dtype, res=res,
                      norm_g=norm_g, slabs=N_CHIPS)

    def mm_dd_t(name, pairs, out_dtype, deps=()):
        ps = [(dy, gb2, (N_CHIPS, None, ds_, d), functools.partial(lambda s, h, q: (0, h, q, 0), h=DD[w][0], q=DD[w][1]))
              for dy, w in pairs]
        return _mm_nt_cols(name, ps, N_CHIPS, ds_, [out_dtype], deps=deps, slabs=N_CHIPS)[0]

    def vec(n):
        return wts[n].reshape(1, -1)

    disc_in = (ssm_a_re[0], ssm_a_im[0], ssm_log_dt[0], ssm_b_re[0], ssm_b_im[0])
    (abar_re, abar_im, bbar_re, bbar_im), disc_vjp = jax.vjp(_s5_discretize, *disc_in)
    gpb = min(S5_GROUPS_PER_BLOCK, n_grp)
    nb = n_grp // gpb
    cb = gpb * n_state
    eye = jnp.eye(gpb, dtype=F32)

    def blockdiag(t):
        return jnp.einsum('jgph,gk->jghkp', t.reshape(nb, gpb, n_state, SSM_GROUP), eye).reshape(nb, gpb * SSM_GROUP, cb)

    def blockdiag_c(t):
        return jnp.einsum('jghp,gk->jkpgh', t.reshape(nb, gpb, SSM_GROUP, n_state), eye).reshape(nb, cb, gpb * SSM_GROUP)

    bblk = jnp.concatenate([blockdiag(bbar_re), blockdiag(bbar_im)], -1).astype(BF16)
    cblk = jnp.concatenate([blockdiag_c(ssm_c_re[0]), -blockdiag_c(ssm_c_im[0])], 1).astype(BF16)
    ab = jnp.stack([abar_re.reshape(nb, cb), abar_im.reshape(nb, cb)], 1)

    u = mm_dd("mix_in", n2, 'w_in', F32)

    up = _perm_rows(u[:, :ws]).astype(BF16)
    ylin_p, s_all = _s5_fwd(up, bblk, ab, cblk)
    ylin = _unperm_rows(ylin_p)

    def gelu_fn(r, v):
        y1 = r[0] + v[0] * r[1]
        y2 = jax.nn.gelu(y1)
        return [y2, y2], []
    fwd_a2, tok = gathered_start(3, ylin_p)
    y2, y2b = _rowwise("s5_gelu", gelu_fn, [ylin, (u, 0, ws)], [vec('ssm_d')], [(ws, F32), (ws, BF16)], deps=[tok])
    z = _mm_nn("s5_glu", y2b, wglu_full, (ws, ws), lambda j, k: (0, 0), 1, ws, F32)

    def glu_fn(r, v):
        y3 = r[0] * _sigmoid(r[1] + v[0])
        return [_rms_fwd(y3, v[1])], []
    m_ssm = _rowwise("s5_gate_norm", glu_fn, [y2, z], [vec('b_glu'), vec('g_out_ssm')], [(ws, BF16)])[0]

    pooled, zp = _pool_fwd(u, ws // pw, wpool_full, vec('pool_scale'))
    fwd_d2, tok = gathered_start(4, zp)
    m_pool = _rmsnorm("norm_pool", zp, vec('g_out_pool'), deps=[tok])
    merged = jnp.concatenate([m_ssm, m_pool], -1)
    h2, hn = mm_dd("mix_out", merged, 'w_out', F32, res=h1, norm_g=vec('g_xattn'))

    memn = _rmsnorm("norm_mem", mem2, vec('g_mem'))
    k_mem = mm_dd("attn_k", memn, 'w_k', BF16)
    v_mem = mm_dd("attn_v", memn, 'w_v', BF16)
    q = mm_dd("attn_q", hn, 'w_q', BF16)
    o = _attn_fwd(q, k_mem, v_mem)
    h3, n4 = mm_dd("attn_out", o, 'w_o', F32, res=h2, norm_g=vec('g_ffn2'))

    ga2 = gathered_finish(3, fwd_a2, h3)
    all_chips = jnp.arange(N_CHIPS, dtype=jnp.int32)
    a2, b2, hm2 = _ffn_up("ffn2_up", n4, ga2, all_chips, all_chips, d, fs)
    gd2 = gathered_finish(4, fwd_d2, hm2).reshape(N_CHIPS, fs, d)
    h4 = _mm_nn("ffn2_down", hm2, gd2, (None, fs, d), lambda j, k: (k, 0, 0), N_CHIPS, d, F32, res=h3, alpha=0.5)

    def loss_fn(r, v):
        h, t = r
        e = _rms_fwd(h, v[0]) - t
        dy = e * (1.0 / d)
        dh, dg = _rms_bwd(dy, h, v[0])
        part = jnp.sum(_colsum(e * e), axis=1, keepdims=True) * (0.5 / d)
        return [dh, 0.5 * dh], [_colsum(dg), jnp.broadcast_to(part, (1, 128))]
    dh4, dy_f2, dg_final, loss_row = _rowwise("loss_head", loss_fn, [h4, tgt], [g_final.reshape(1, -1)],
                                              [(d, F32), (d, BF16)], [d, 128])

    def rs_pair_start(tag, gbufs, after=None):
        land = [lax.empty((N_CHIPS,) + g.shape[2:], BF16) for g in gbufs]
        send, recv, thru, token = _comm_start(tag + "_pair_start", "pair", list(gbufs) + land, after=after)
        return (send, recv, thru), token

    def rs_scatter_start(tag, handle, after):
        send, recv, thru = handle
        n = len(thru) // 2
        res = _comm_wait(tag + "_pair_wait", "pair", thru, send, recv, after)
        parts = [_pair_add(g, r, core) for g, r in zip(res[:n], res[n:])]
        land = [lax.empty(p.shape, BF16) for p in parts]
        send, recv, thru, token = _comm_start(tag + "_scatter_start", "scatter", parts + land)
        return (send, recv, thru), token

    def rs_half_start(tag, handle, after):
        send, recv, thru = handle
        n = len(thru) // 2
        res = _comm_wait(tag + "_scatter_wait", "scatter", thru, send, recv, after)
        full = [_chip_sum(p, g2, chip_idx) for p, g2 in zip(res[:n], res[n:])]
        send, recv, thru, token = _comm_start(tag + "_half_start", "half", full)
        return (send, recv, thru), token

    def rs_finish(tag, handle, after):
        send, recv, thru = handle
        return _comm_wait(tag + "_half_wait", "half", thru, send, recv, after)

    wblk = (None, None, d, fs)

    def ffn_down_bwd(tag, dy_half, a, b, hm, gd_l, deps=()):
        da, db = _mm_nt_cols(tag + "_down_bwd", [(dy_half, gd_l, (None, fs, d), lambda s: (s, 0, 0))],
                             N_CHIPS, fs, [BF16, BF16], epi=_swiglu_bwd, extras=[a, b], deps=deps, row_parts=2)
        g_down = _mm_tn(tag + "_dw_down", hm, dy_half, fs, d // 2, N_CHIPS, 2, jax.ShapeDtypeStruct((N_CHIPS, fs, d), BF16),
                        (None, fs, d // 2), lambda p, q: (p, 0, q), tt=2048)
        return da, db, g_down.reshape(N_CHIPS, 2, fs // 2, d)

    def ffn_up_bwd(tag, da, db, ga_l, deps=()):
        return _mm_nt_k(tag + "_up_bwd", [(da, ga_l, wblk, lambda s: (s, 0, 0, 0)), (db, ga_l, wblk, lambda s: (s, 1, 0, 0))],
                        N_CHIPS, d, F32, deps=deps)

    def ffn_dw(name, dact, n_in, deps=()):
        return _mm_tn(name, n_in, dact, d // 2, fs, 2, N_CHIPS, jax.ShapeDtypeStruct((N_CHIPS, 2, d // 2, fs), BF16),
                      (None, None, d // 2, fs), lambda p, q: (q, p, 0, 0), tt=2048, deps=deps)

    def dw_dd(name, a, dy, wname, grad_b2):
        h, q = DD[wname]
        return _mm_tn(name, a, dy, ds_, d, N_CHIPS, 1, jax.ShapeDtypeStruct((N_CHIPS, 2, rh, d), BF16),
                      (None, None, ds_, d), lambda p, qq: (p, h, q, 0), into=grad_b2, tt=2048)

    def norm_bwd(name, dn, h, gname, dres, deps=(), scale=1.0):
        def fn(r, v):
            dx, dg = _rms_bwd(r[0], r[1], v[0])
            tot = dx + r[2]
            return [tot, scale * tot], [_colsum(dg)]
        return _rowwise(name, fn, [dn, h, dres], [vec(gname)], [(d, F32), (d, BF16)], [d], deps=deps)

    da2, db2, g_down2 = ffn_down_bwd("ffn2", dy_f2, a2, b2, hm2, gd2)
    dn4 = ffn_up_bwd("ffn2", da2, db2, ga2)
    g_gate2 = ffn_dw("ffn2_dw_gate", da2, n4)
    g_up2 = ffn_dw("ffn2_dw_up", db2, n4)
    rs_f2, tok = rs_pair_start("ffn2", [g_gate2, g_up2, g_down2])
    dh3, dh3b, dg_ffn2 = norm_bwd("norm_ffn2_bwd", dn4, h3, 'g_ffn2', dh4, deps=[tok])
    rs_f2, tok = rs_scatter_start("ffn2", rs_f2, dh3b)

    do = mm_dd_t("attn_out_bwd", [(dh3b, 'w_o')], BF16, deps=[tok])
    grad_b2 = dw_dd("attn_dw_o", o, dh3b, 'w_o', None)
    dq, dk, dv = _attn_bwd(q, k_mem, v_mem, do)
    dkb, dvb = dk.astype(BF16), dv.astype(BF16)
    grad_b2 = dw_dd("attn_dw_q", hn, dq, 'w_q', grad_b2)
    dhn = mm_dd_t("attn_q_bwd", [(dq, 'w_q')], F32)
    dh2, dh2b, dg_xattn = norm_bwd("norm_xattn_bwd", dhn, h2, 'g_xattn', dh3)
    grad_b2 = dw_dd("attn_dw_k", memn, dkb, 'w_k', grad_b2)
    grad_b2 = dw_dd("attn_dw_v", memn, dvb, 'w_v', grad_b2)
    dmemn = mm_dd_t("attn_kv_bwd", [(dkb, 'w_k'), (dvb, 'w_v')], F32)
    dg_mem = _rowwise("norm_mem_bwd", lambda r, v: ([], [_colsum(_rms_bwd(r[0], r[1], v[0])[1])]),
                      [dmemn, mem2], [vec('g_mem')], [], [d])[0]

    dmerged = mm_dd_t("mix_out_bwd", [(dh2b, 'w_out')], F32)
    grad_b2 = dw_dd("mix_dw_out", merged, dh2b, 'w_out', grad_b2)

    def gate_bwd_fn(r, v):
        dm, y2_v, z_v = r
        sg = _sigmoid(z_v + v[0])
        y3 = y2_v * sg
        dy3, dg = _rms_bwd(dm, y3, v[1])
        dz = dy3 * y3 * (1.0 - sg)
        return [dy3 * sg, dz], [_colsum(dg), _colsum(dz)]
    dy2a, dzb, dg_out_ssm, db_glu = _rowwise("s5_gate_norm_bwd", gate_bwd_fn, [(dmerged, 0, ws), y2, z],
                                             [vec('b_glu'), vec('g_out_ssm')], [(ws, F32), (ws, BF16)], [ws, ws])
    dy2b_ = _mm_nt_cols("s5_glu_bwd", [(dzb, wglu_full, (ws, ws), lambda s: (0, 0))], 1, ws, [F32])[0]
    dw_glu = _mm_tn("s5_dw_glu", y2b, dzb, ws, ws, 1, 1, jax.ShapeDtypeStruct((ws, ws), F32), (ws, ws), lambda p, q: (0, 0))

    def gelu_bwd_fn(r, v):
        dy2 = r[0] + r[1]
        us = r[3]
        y1 = r[2] + v[0] * us
        kk = math.sqrt(2.0 / math.pi)
        th = jnp.tanh(kk * (y1 + 0.044715 * y1 * y1 * y1))
        dgelu = 0.5 * (1.0 + th) + 0.5 * y1 * (1.0 - th * th) * kk * (1.0 + 3.0 * 0.044715 * y1 * y1)
        dy1 = dy2 * dgelu
        return [dy1, dy1 * v[0]], [_colsum(dy1 * us)]
    dy1b, du_skip, d_ssm_d = _rowwise("s5_gelu_bwd", gelu_bwd_fn, [dy2a, dy2b_, ylin, (u, 0, ws)], [vec('ssm_d')],
                                      [(ws, BF16), (ws, F32)], [ws])

    bblk_t = jnp.swapaxes(bblk, 1, 2)
    cblk_t = jnp.swapaxes(cblk, 1, 2)
    du_p, d_bblk, d_cblk_t, d_ab = _s5_bwd(_perm_rows(dy1b), up, s_all, bblk_t, ab, cblk_t)
    du_ssm = _unperm_rows(du_p)

    dzp, dg_out_pool = _rowwise("norm_pool_bwd", lambda r, v: (lambda dx, dg: ([dx], [_colsum(dg)]))(*_rms_bwd(r[0], r[1], v[0])),
                                [(dmerged, 1, ws), zp], [vec('g_out_pool')], [(ws, F32)], [ws])
    dps, dw_pool, d_pool_scale = _pool_bwd1(dzp, pooled, wpool_full, vec('pool_scale'))
    du_pool = _pool_bwd2(dps, n_pg)

    dub = _rowwise("mix_du", lambda r, v: ([jnp.concatenate([r[0] + r[1], r[2]], -1)], []),
                   [du_ssm, du_skip, du_pool], [], [(d, BF16)])[0]
    dn2 = mm_dd_t("mix_in_bwd", [(dub, 'w_in')], F32)
    grad_b2 = dw_dd("mix_dw_in", n2, dub, 'w_in', grad_b2)
    glu_g = dw_glu.reshape(N_CHIPS, 2, gh, d).astype(BF16)
    pool_g = dw_pool.reshape(n_pg, N_CHIPS, pw // N_CHIPS, pw).transpose(1, 0, 2, 3).reshape(N_CHIPS, 2, ph, d)
    pool_g = jnp.concatenate([pool_g, jnp.zeros((N_CHIPS, 2, rh - 3 * ds_ - gh - ph, d), F32)], 2).astype(BF16)
    grad_b2 = lax.dynamic_update_slice(grad_b2, glu_g, (0, 0, 3 * ds_, 0))
    grad_b2 = lax.dynamic_update_slice(grad_b2, pool_g, (0, 0, 3 * ds_ + gh, 0))
    rs_mix, tok = rs_pair_start("mixers", [grad_b2])
    dh1, dy_f1, dg_mix = norm_bwd("norm_mix_bwd", dn2, h1, 'g_mix', dh2, deps=[tok], scale=0.5)
    rs_mix, tok = rs_scatter_start("mixers", rs_mix, dy_f1)

    da1, db1, g_down1 = ffn_down_bwd("ffn1", dy_f1, a1, b1, hm1, gd1, deps=[tok])
    rs_d1, tok = rs_pair_start("ffn1_down", [g_down1])
    dn1 = ffn_up_bwd("ffn1", da1, db1, ga1, deps=[tok])
    rs_d1, tok = rs_scatter_start("ffn1_down", rs_d1, dn1)
    grad_x, _, dg_ffn1 = norm_bwd("norm_ffn1_bwd", dn1, x2, 'g_ffn1', dh1, deps=[tok])

    def undiag(t):
        return jnp.einsum('jghkp,gk->jgph', t.reshape(nb, gpb, SSM_GROUP, gpb, n_state), eye).reshape(n_grp, n_state, SSM_GROUP)

    d_bbar_re, d_bbar_im = undiag(d_bblk[:, :, :cb]), undiag(d_bblk[:, :, cb:])
    d_c_re = undiag(d_cblk_t[:, :, :cb]).transpose(0, 2, 1)
    d_c_im = -undiag(d_cblk_t[:, :, cb:]).transpose(0, 2, 1)
    d_abar = jnp.sum(d_ab, axis=2).reshape(nb, 2, gpb, n_state)
    d_abar_re = d_abar[:, 0].reshape(n_grp, n_state)
    d_abar_im = d_abar[:, 1].reshape(n_grp, n_state)
    d_a_re, d_a_im, d_log_dt, d_b_re, d_b_im = disc_vjp((d_abar_re, d_abar_im, d_bbar_re, d_bbar_im))

    small_g = {'g_ffn1': dg_ffn1, 'g_mix': dg_mix, 'ssm_a_re': d_a_re, 'ssm_a_im': d_a_im, 'ssm_log_dt': d_log_dt,
               'ssm_b_re': d_b_re, 'ssm_b_im': d_b_im, 'ssm_c_re': d_c_re, 'ssm_c_im': d_c_im, 'ssm_d': d_ssm_d,
               'b_glu': db_glu, 'pool_scale': d_pool_scale, 'g_out_ssm': dg_out_ssm, 'g_out_pool': dg_out_pool,
               'g_xattn': dg_xattn, 'g_mem': dg_mem, 'g_ffn2': dg_ffn2, 'g_final': dg_final}
    sizes = [wts[n].size for n in SMALL]
    total = sum(sizes) + 128
    rows_s = -(-total // (128 * 256)) * 256
    flat = jnp.concatenate([small_g[n].reshape(-1) for n in SMALL] + [loss_row.reshape(-1)])
    flat = jnp.pad(flat, (0, rows_s * 128 - total)).reshape(rows_s, 128)
    sw_send, sw_recv, sw_thru, tok = _comm_start("small_swap_start", "swap", [flat, lax.empty(flat.shape, F32)])
    g_gate1 = ffn_dw("ffn1_dw_gate", da1, n1, deps=[tok])
    rs_g1, tok_g1 = rs_pair_start("ffn1_gate", [g_gate1])
    sw_v, sw_got = _comm_wait("small_swap_wait", "swap", sw_thru, sw_send, sw_recv, tok_g1)
    slots = _add_into_slot(sw_v, sw_got, chip.reshape(1))
    bc_send, bc_recv, bc_thru, tok = _comm_start("small_bcast_start", "bcast", [slots])
    g_up1 = ffn_dw("ffn1_dw_up", db1, n1, deps=[tok])
    rs_g1, tok = rs_scatter_start("ffn1_gate", rs_g1, g_up1)
    slots, = _comm_wait("small_bcast_wait", "bcast", bc_thru, bc_send, bc_recv, tok)
    red = _sum_slots(slots).reshape(-1)
    loss = red[sum(sizes)]

    def flat_small(t):
        return jnp.pad(jnp.concatenate([t[n].reshape(-1) for n in SMALL]), (0, rows_s * 128 - sum(sizes))).reshape(rows_s, 128)
    sg_, sd_, sm_, sv_ = _adamw("adamw_small", flat_small(wts), flat_small(mom), flat_small(var), red.reshape(1, rows_s, 128))
    out = {}
    off = 0
    for n, sz in zip(SMALL, sizes):
        for key, arr in (('grad', sg_), ('delta', sd_), ('m', sm_), ('v', sv_)):
            out[key, n] = arr.reshape(-1)[off:off + sz].reshape(wts[n].shape)
        off += sz

    def upd(n, g_arr, half, row_off, shape2):
        res = _adamw("adamw_" + n, wts[n].reshape(shape2), mom[n].reshape(shape2), var[n].reshape(shape2), g_arr, half, row_off)
        for key, arr in zip(('grad', 'delta', 'm', 'v'), res):
            out[key, n] = arr.reshape(wts[n].shape)
        return res[3]

    rs_u1, tok = rs_pair_start("ffn1_up", [g_up1], after=sv_)
    rs_f2, tok = rs_half_start("ffn2", rs_f2, tok)
    rs_u1, tok = rs_scatter_start("ffn1_up", rs_u1, tok)
    rs_mix, tok = rs_half_start("mixers", rs_mix, tok)
    full_gate2, full_up2, full_down2 = rs_finish("ffn2", rs_f2, tok)
    upd('w2_gate', full_gate2.reshape(1, d, fs), 0, 0, (d, fs))
    upd('w2_up', full_up2.reshape(1, d, fs), 0, 0, (d, fs))
    last = upd('w2_down', full_down2.reshape(1, fs, d), 0, 0, (fs, d))
    rs_d1, tok = rs_half_start("ffn1_down", rs_d1, last)
    full_b2, = rs_finish("mixers", rs_mix, tok)
    for n, (h, q) in DD.items():
        last = upd(n, full_b2, h, q * ds_, (ds_, d))
    glu_shape, pool_shape = (ws // N_CHIPS, ws), (n_pg * pw // N_CHIPS, pw)
    upd('w_glu', full_b2[:, 3 * ds_:3 * ds_ + gh].reshape((1,) + glu_shape), 0, 0, glu_shape)
    upd('w_pool', full_b2[:, 3 * ds_ + gh:3 * ds_ + gh + ph].reshape((1,) + pool_shape), 0, 0, pool_shape)
    full_down1, = rs_finish("ffn1_down", rs_d1, last)
    last = upd('w1_down', full_down1.reshape(1, fs, d), 0, 0, (fs, d))
    rs_g1, tok = rs_half_start("ffn1_gate", rs_g1, last)
    rs_u1, tok = rs_half_start("ffn1_up", rs_u1, tok)
    full_gate1, = rs_finish("ffn1_gate", rs_g1, tok)
    last = upd('w1_gate', full_gate1.reshape(1, d, fs), 0, 0, (d, fs))
    full_up1, = rs_finish("ffn1_up", rs_u1, last)
    upd('w1_up', full_up1.reshape(1, d, fs), 0, 0, (d, fs))

    return (loss, grad_x[None], *[out['grad', n] for n in WEIGHTS], *[out['delta', n] for n in WEIGHTS],
            *[out['m', n] for n in WEIGHTS], *[out['v', n] for n in WEIGHTS])
```

```python
import functools
import math

import jax
import jax.numpy as jnp
from jax import lax
from jax.experimental import pallas as pl
from jax.experimental.pallas import tpu as pltpu

F32 = jnp.float32
BF16 = jnp.bfloat16
EPS = 1e-6
ADAM_LR, ADAM_B1, ADAM_B2, ADAM_EPS, ADAM_WD, ADAM_STEP = 0.001, 0.9, 0.999, 1e-08, 0.01, 10
POOL_WINDOWS = (2, 4, 8, 16)
SSM_GROUP = 16
S5_GROUPS_PER_BLOCK = 16
S5_LANES = 8
MEM_HEADS = 4
N_CHIPS = 4
VMEM_LIMIT_V7X = 56 * 1024 * 1024
MESH = pl.DeviceIdType.MESH

WEIGHTS = ['g_ffn1', 'w1_gate', 'w1_up', 'w1_down', 'g_mix', 'w_in', 'ssm_a_re', 'ssm_a_im', 'ssm_log_dt',
           'ssm_b_re', 'ssm_b_im', 'ssm_c_re', 'ssm_c_im', 'ssm_d', 'w_glu', 'b_glu', 'w_pool', 'pool_scale',
           'g_out_ssm', 'g_out_pool', 'w_out', 'g_xattn', 'g_mem', 'w_q', 'w_k', 'w_v', 'w_o', 'g_ffn2',
           'w2_gate', 'w2_up', 'w2_down', 'g_final']
BIG = ['w1_gate', 'w1_up', 'w1_down', 'w_in', 'w_glu', 'w_pool', 'w_out', 'w_q', 'w_k', 'w_v', 'w_o',
       'w2_gate', 'w2_up', 'w2_down']
SMALL = [n for n in WEIGHTS if n not in BIG]


def _tile(n, target, mult=8):
    best = None
    for d in range(1, n + 1):
        if n % d == 0 and d <= target and d % mult == 0:
            best = d
    return best if best is not None else n


def _params(sem=None):
    if sem is None:
        return pltpu.CompilerParams(vmem_limit_bytes=VMEM_LIMIT_V7X)
    return pltpu.CompilerParams(dimension_semantics=sem, vmem_limit_bytes=VMEM_LIMIT_V7X)


def _sigmoid(x):
    return 1.0 / (1.0 + jnp.exp(-x))


def _sigmoid_approx(x):
    return pl.reciprocal(1.0 + jnp.exp(-x), approx=True)


def _rms_fwd(x, g):
    r = lax.rsqrt(jnp.mean(x * x, axis=-1, keepdims=True) + EPS)
    return x * r * g


def _rms_bwd(dy, x, g):
    r = lax.rsqrt(jnp.mean(x * x, axis=-1, keepdims=True) + EPS)
    dxh = dy * g
    dx = r * dxh - x * (r * r * r) * jnp.mean(dxh * x, axis=-1, keepdims=True)
    return dx, dy * x * r


def _colsum(v):
    return jnp.sum(v, axis=0, keepdims=True)


def _rowwise(name, fn, rows, vecs, out_defs, red_defs=(), tm=256, deps=()):
    rows = [r if isinstance(r, tuple) else (r, 0, r.shape[1]) for r in rows]
    t_rows = rows[0][0].shape[0]
    tm = _tile(t_rows, tm)
    nr, nv, no, nd, nx = len(rows), len(vecs), len(out_defs), len(red_defs), len(deps)

    def body(*refs):
        r, v = refs[:nr], refs[nr:nr + nv]
        o, d = refs[nr + nv + nx:nr + nv + nx + no], refs[nr + nv + nx + no:]
        outs, reds = fn([x[...] for x in r], [x[...] for x in v])
        for ref, val in zip(o, outs):
            ref[...] = val.astype(ref.dtype)
        if nd:
            @pl.when(pl.program_id(0) == 0)
            def _():
                for ref in d:
                    ref[...] = jnp.zeros(ref.shape, ref.dtype)
            for ref, val in zip(d, reds):
                ref[...] += val

    in_specs = [pl.BlockSpec((tm, w), functools.partial(lambda i, cb: (i, cb), cb=cb)) for (_, cb, w) in rows]
    in_specs += [pl.BlockSpec(v.shape, lambda i: (0, 0)) for v in vecs]
    in_specs += [pl.BlockSpec(memory_space=pl.ANY)] * nx
    out_specs = [pl.BlockSpec((tm, w), lambda i: (i, 0)) for (w, _) in out_defs]
    out_specs += [pl.BlockSpec((1, w), lambda i: (0, 0)) for w in red_defs]
    out_shape = [jax.ShapeDtypeStruct((t_rows, w), dt) for (w, dt) in out_defs]
    out_shape += [jax.ShapeDtypeStruct((1, w), F32) for w in red_defs]
    res = pl.pallas_call(
        body, name=name, grid=(t_rows // tm,), in_specs=in_specs, out_specs=out_specs, out_shape=out_shape,
        compiler_params=_params(("arbitrary",)),
    )(*[r[0] for r in rows], *vecs, *deps)
    return res


def _rmsnorm(name, x, g, tm=256, deps=()):
    return _rowwise(name, lambda r, v: ([_rms_fwd(r[0].astype(F32), v[0])], []), [x], [g],
                    [(x.shape[1], BF16)], tm=tm, deps=deps)[0]


def _mm_nn(name, a, b, b_block, b_idx, nk, n_out, out_dtype, res=None, alpha=1.0, norm_g=None, tm=512, slabs=1):
    t_rows = a.shape[0]
    bk, tn = slabs * b_block[-2], b_block[-1]
    tm = _tile(t_rows, tm)
    nj = n_out // tn
    has_res = res is not None
    has_norm = norm_g is not None
    assert not has_norm or nj == 1

    def body(*refs):
        a_ref, b_ref = refs[0], refs[1]
        res_ref = refs[2] if has_res else None
        g_ref = refs[2 + has_res] if has_norm else None
        o_ref = refs[2 + has_res + has_norm]
        n_ref = refs[3 + has_res + has_norm] if has_norm else None
        k = pl.program_id(2)
        w = b_ref[...].reshape(bk, tn) if slabs > 1 else b_ref[...]
        p = jnp.dot(a_ref[...], w, preferred_element_type=F32)

        def finish(r):
            if has_res:
                r = res_ref[...] + alpha * r
            o_ref[...] = r.astype(o_ref.dtype)
            if has_norm:
                n_ref[...] = _rms_fwd(r, g_ref[...]).astype(n_ref.dtype)

        if nk == 1:
            finish(p)
            return
        acc_ref = refs[3 + has_res + 2 * has_norm]

        @pl.when(k == 0)
        def _():
            acc_ref[...] = p

        @pl.when(k > 0)
        def _():
            acc_ref[...] += p

        @pl.when(k == nk - 1)
        def _():
            finish(acc_ref[...])

    in_specs = [pl.BlockSpec((tm, bk), lambda j, i, k: (i, k)),
                pl.BlockSpec(b_block, lambda j, i, k: b_idx(j, k))]
    args = [a, b]
    if has_res:
        in_specs.append(pl.BlockSpec((tm, tn), lambda j, i, k: (i, j)))
        args.append(res)
    tile = pl.BlockSpec((tm, tn), lambda j, i, k: (i, j))
    out_specs, out_shape = tile, jax.ShapeDtypeStruct((t_rows, n_out), out_dtype)
    if has_norm:
        in_specs.append(pl.BlockSpec((1, n_out), lambda j, i, k: (0, 0)))
        args.append(norm_g)
        out_specs, out_shape = [tile, tile], [out_shape, jax.ShapeDtypeStruct((t_rows, n_out), BF16)]
    return pl.pallas_call(
        body, name=name, grid=(nj, t_rows // tm, nk), in_specs=in_specs, out_specs=out_specs, out_shape=out_shape,
        scratch_shapes=[pltpu.VMEM((tm, tn), F32)] if nk > 1 else [],
        compiler_params=_params(("arbitrary", "arbitrary", "arbitrary")),
    )(*args)


def _dot_nt(x, w):
    return lax.dot_general(x, w, (((1,), (1,)), ((), ())), preferred_element_type=F32)


def _dot_tn(x, y):
    return lax.dot_general(x, y, (((0,), (0,)), ((), ())), preferred_element_type=F32)


def _mm_nt_cols(name, pairs, ns, bn, out_defs, epi=None, extras=(), tm=512, deps=(), slabs=1, row_parts=1):
    t_rows = pairs[0][0].shape[0]
    tm = _tile(t_rows, tm)
    npair, nex, no, nx = len(pairs), len(extras), len(out_defs), len(deps)
    ns, bn = ns // slabs, bn * slabs
    rp = tm // row_parts

    def body(*refs):
        ws = [refs[2 * p + 1][...] for p in range(npair)]
        if slabs > 1:
            ws = [w.reshape(bn, w.shape[-1]) for w in ws]
        for part_i in range(row_parts):
            rows = slice(part_i * rp, (part_i + 1) * rp)
            acc = None
            for p in range(npair):
                part = _dot_nt(refs[2 * p][rows, :], ws[p])
                acc = part if acc is None else acc + part
            ex = [r[rows, :] for r in refs[2 * npair:2 * npair + nex]]
            outs = epi(acc, *ex) if epi is not None else (acc,)
            for ref, val in zip(refs[2 * npair + nex + nx:], outs):
                ref[rows, :] = val.astype(ref.dtype)

    in_specs, args = [], []
    for (dy, w, w_block, w_idx) in pairs:
        in_specs.append(pl.BlockSpec((tm, dy.shape[1]), lambda s, i: (i, 0)))
        in_specs.append(pl.BlockSpec(w_block, functools.partial(lambda s, i, f: f(s), f=w_idx)))
        args += [dy, w]
    for e in extras:
        in_specs.append(pl.BlockSpec((tm, bn), lambda s, i: (i, s)))
        args.append(e)
    in_specs += [pl.BlockSpec(memory_space=pl.ANY)] * nx
    args += list(deps)
    res = pl.pallas_call(
        body, name=name, grid=(ns, t_rows // tm), in_specs=in_specs,
        out_specs=[pl.BlockSpec((tm, bn), lambda s, i: (i, s)) for _ in range(no)],
        out_shape=[jax.ShapeDtypeStruct((t_rows, ns * bn), dt) for dt in out_defs],
        compiler_params=_params(("arbitrary", "arbitrary")),
    )(*args)
    return res


def _mm_nt_k(name, pairs, ns, n_out, out_dtype, tm=512, deps=()):
    t_rows = pairs[0][0].shape[0]
    tm = _tile(t_rows, tm)
    npair, nx = len(pairs), len(deps)

    def body(*refs):
        o_ref, acc_ref = refs[2 * npair + nx], refs[2 * npair + nx + 1]
        s = pl.program_id(1)
        acc = None
        for p in range(npair):
            part = _dot_nt(refs[2 * p][...], refs[2 * p + 1][...])
            acc = part if acc is None else acc + part

        @pl.when(s == 0)
        def _():
            acc_ref[...] = acc

        @pl.when(s > 0)
        def _():
            acc_ref[...] += acc

        @pl.when(s == ns - 1)
        def _():
            o_ref[...] = acc_ref[...].astype(o_ref.dtype)

    in_specs, args = [], []
    for (a, w, w_block, w_idx) in pairs:
        in_specs.append(pl.BlockSpec((tm, w_block[-1]), lambda i, s: (i, s)))
        in_specs.append(pl.BlockSpec(w_block, functools.partial(lambda i, s, f: f(s), f=w_idx)))
        args += [a, w]
    in_specs += [pl.BlockSpec(memory_space=pl.ANY)] * nx
    args += list(deps)
    return pl.pallas_call(
        body, name=name, grid=(t_rows // tm, ns), in_specs=in_specs,
        out_specs=pl.BlockSpec((tm, n_out), lambda i, s: (i, 0)),
        out_shape=jax.ShapeDtypeStruct((t_rows, n_out), out_dtype),
        scratch_shapes=[pltpu.VMEM((tm, n_out), F32)],
        compiler_params=_params(("arbitrary", "arbitrary")),
    )(*args)


def _mm_tn(name, a, b, bk, bn, n_p, n_q, out_shape, out_block, out_idx, into=None, a_off=0, b_off=0, tt=512,
           deps=()):
    t_rows = a.shape[0]
    tt = _tile(t_rows, tt, 16)
    nt = t_rows // tt
    has_into = into is not None
    nx = len(deps)

    def body(*refs):
        a_ref, b_ref = refs[0], refs[1]
        o_ref, acc_ref = refs[2 + has_into + nx], refs[3 + has_into + nx]
        t = pl.program_id(2)
        part = _dot_tn(a_ref[...], b_ref[...])

        @pl.when(t == 0)
        def _():
            acc_ref[...] = part

        @pl.when(t > 0)
        def _():
            acc_ref[...] += part

        @pl.when(t == nt - 1)
        def _():
            o_ref[...] = acc_ref[...].astype(o_ref.dtype)

    in_specs = [pl.BlockSpec((tt, bk), lambda p, q, t: (t, p + a_off)),
                pl.BlockSpec((tt, bn), lambda p, q, t: (t, q + b_off))]
    args = [a, b]
    aliases = {}
    if has_into:
        in_specs.append(pl.BlockSpec(memory_space=pl.ANY))
        args.append(into)
        aliases = {2: 0}
        out_shape = jax.ShapeDtypeStruct(into.shape, into.dtype)
    in_specs += [pl.BlockSpec(memory_space=pl.ANY)] * nx
    args += list(deps)
    return pl.pallas_call(
        body, name=name, grid=(n_p, n_q, nt), in_specs=in_specs,
        out_specs=pl.BlockSpec(out_block, lambda p, q, t: out_idx(p, q)),
        out_shape=out_shape, scratch_shapes=[pltpu.VMEM((bk, bn), F32)],
        input_output_aliases=aliases,
        compiler_params=_params(("arbitrary", "arbitrary", "arbitrary")),
    )(*args)


def _ffn_up(name, n, ga, slots, cols, d_model, fs, into=None, tm=512):
    t_rows = n.shape[0]
    tm = _tile(t_rows, tm)
    n_sh = slots.shape[0]
    has_into = into is not None

    row_parts = 2 if tm % 32 == 0 else 1

    def body(slot_ref, col_ref, n_ref, wg_ref, wu_ref, *refs):
        a_ref, b_ref, h_ref = refs[3 * has_into:]
        wg, wu = wg_ref[...], wu_ref[...]
        for part in range(row_parts):
            rows = slice(part * (tm // row_parts), (part + 1) * (tm // row_parts))
            x = n_ref[rows, :]
            a = jnp.dot(x, wg, preferred_element_type=F32)
            b = jnp.dot(x, wu, preferred_element_type=F32)
            a_ref[rows, :] = a.astype(a_ref.dtype)
            b_ref[rows, :] = b.astype(b_ref.dtype)
            h_ref[rows, :] = (a * _sigmoid_approx(a) * b).astype(h_ref.dtype)

    w_block = (None, None, d_model, fs)
    out = jax.ShapeDtypeStruct((t_rows, N_CHIPS * fs), BF16)
    in_specs = [pl.BlockSpec((tm, d_model), lambda s, i, sl, co: (i, 0)),
                pl.BlockSpec(w_block, lambda s, i, sl, co: (sl[s], 0, 0, 0)),
                pl.BlockSpec(w_block, lambda s, i, sl, co: (sl[s], 1, 0, 0))]
    args = [slots, cols, n, ga, ga]
    aliases = {}
    if has_into:
        in_specs += [pl.BlockSpec(memory_space=pl.ANY)] * 3
        args += list(into)
        aliases = {5: 0, 6: 1, 7: 2}
    return pl.pallas_call(
        body, name=name,
        grid_spec=pltpu.PrefetchScalarGridSpec(
            num_scalar_prefetch=2, grid=(n_sh, t_rows // tm), in_specs=in_specs,
            out_specs=[pl.BlockSpec((tm, fs), lambda s, i, sl, co: (i, co[s]))] * 3),
        out_shape=[out, out, out], input_output_aliases=aliases,
        compiler_params=_params(("arbitrary", "arbitrary")),
    )(*args)


def _swiglu_bwd(dh, a, b):
    a = a.astype(F32)
    b = b.astype(F32)
    sg = _sigmoid_approx(a)
    return dh * b * sg * (1.0 + a * (1.0 - sg)), dh * a * sg


def _attn_fwd(q, k, v, tm=512):
    t_rows, d_model = q.shape
    n_mem = k.shape[0]
    hd = d_model // MEM_HEADS
    scale = hd ** -0.5
    tm = _tile(t_rows, tm)

    def body(q_ref, k_ref, v_ref, o_ref):
        for h in range(MEM_HEADS):
            cols = slice(h * hd, (h + 1) * hd)
            s = _dot_nt(q_ref[:, cols], k_ref[:, cols]) * scale
            s = s - jnp.max(s, axis=-1, keepdims=True)
            e = jnp.exp(s)
            p = e / jnp.sum(e, axis=-1, keepdims=True)
            o_ref[:, cols] = jnp.dot(p.astype(BF16), v_ref[:, cols], preferred_element_type=F32).astype(o_ref.dtype)

    return pl.pallas_call(
        body, name="attn_fwd", grid=(t_rows // tm,),
        in_specs=[pl.BlockSpec((tm, d_model), lambda i: (i, 0)),
                  pl.BlockSpec((n_mem, d_model), lambda i: (0, 0)),
                  pl.BlockSpec((n_mem, d_model), lambda i: (0, 0))],
        out_specs=pl.BlockSpec((tm, d_model), lambda i: (i, 0)),
        out_shape=jax.ShapeDtypeStruct((t_rows, d_model), BF16),
        compiler_params=_params(("arbitrary",)),
    )(q, k, v)


def _attn_bwd(q, k, v, do, tm=512):
    t_rows, d_model = q.shape
    n_mem = k.shape[0]
    hd = d_model // MEM_HEADS
    scale = hd ** -0.5
    tm = _tile(t_rows, tm, 16)

    def body(q_ref, k_ref, v_ref, do_ref, dq_ref, dk_ref, dv_ref):
        @pl.when(pl.program_id(0) == 0)
        def _():
            dk_ref[...] = jnp.zeros(dk_ref.shape, F32)
            dv_ref[...] = jnp.zeros(dv_ref.shape, F32)

        for h in range(MEM_HEADS):
            cols = slice(h * hd, (h + 1) * hd)
            qh, kh, vh, doh = q_ref[:, cols], k_ref[:, cols], v_ref[:, cols], do_ref[:, cols]
            s = _dot_nt(qh, kh) * scale
            s = s - jnp.max(s, axis=-1, keepdims=True)
            e = jnp.exp(s)
            p = e / jnp.sum(e, axis=-1, keepdims=True)
            dv_ref[:, cols] += _dot_tn(p.astype(BF16), doh)
            dp = _dot_nt(doh, vh)
            ds = (p * (dp - jnp.sum(dp * p, axis=-1, keepdims=True)) * scale).astype(BF16)
            dq_ref[:, cols] = jnp.dot(ds, kh, preferred_element_type=F32).astype(dq_ref.dtype)
            dk_ref[:, cols] += _dot_tn(ds, qh)

    full = pl.BlockSpec((n_mem, d_model), lambda i: (0, 0))
    tile = pl.BlockSpec((tm, d_model), lambda i: (i, 0))
    return pl.pallas_call(
        body, name="attn_bwd", grid=(t_rows // tm,),
        in_specs=[tile, full, full, tile], out_specs=[tile, full, full],
        out_shape=[jax.ShapeDtypeStruct((t_rows, d_model), BF16),
                   jax.ShapeDtypeStruct((n_mem, d_model), F32), jax.ShapeDtypeStruct((n_mem, d_model), F32)],
        compiler_params=_params(("arbitrary",)),
    )(q, k, v, do)


def _split_bf16(v):
    hi = v.astype(BF16)
    return hi, (v - hi.astype(F32)).astype(BF16)


def _pool_window(g):
    return jnp.left_shift(jnp.int32(POOL_WINDOWS[0]), g)


def _pool_fwd(u, col_off, w_pool, scale, tt=256):
    t_rows = u.shape[0]
    pw = w_pool.shape[-1]
    ng = w_pool.shape[0]
    tt = _tile(t_rows, tt, 16)
    nt = t_rows // tt
    assert POOL_WINDOWS == tuple(2 << i for i in range(ng)) and tt >= POOL_WINDOWS[-1]

    def body(vc_ref, vp_ref, w_ref, sc_ref, pooled_ref, z_ref):
        g, i = pl.program_id(0), pl.program_id(1)
        w = _pool_window(g)
        r = lax.broadcasted_iota(jnp.int32, (tt, tt), 0)
        c = lax.broadcasted_iota(jnp.int32, (tt, tt), 1)
        band_c = ((c <= r) & (c > r - w)).astype(BF16)
        band_p = (c > r - w + tt).astype(BF16)
        vc = vc_ref[...]
        ch, cl = _split_bf16(vc)
        ph, plo = _split_bf16(vp_ref[...] * (i > 0).astype(F32))
        sums = (jnp.dot(band_c, ch, preferred_element_type=F32) + jnp.dot(band_c, cl, preferred_element_type=F32)
                + jnp.dot(band_p, ph, preferred_element_type=F32) + jnp.dot(band_p, plo, preferred_element_type=F32))
        t = i * tt + lax.broadcasted_iota(jnp.int32, (tt, 1), 0)
        cnt = jnp.minimum(t + 1, w).astype(F32)
        pooled = (sums / cnt - vc).astype(BF16)
        pooled_ref[...] = pooled
        z_ref[...] = jnp.dot(pooled, w_ref[...], preferred_element_type=F32) * sc_ref[...]

    return pl.pallas_call(
        body, name="pool_fwd", grid=(ng, nt),
        in_specs=[pl.BlockSpec((tt, pw), lambda g, i: (i, col_off + g)),
                  pl.BlockSpec((tt, pw), lambda g, i: (jnp.maximum(i - 1, 0), col_off + g)),
                  pl.BlockSpec((None, pw, pw), lambda g, i: (g, 0, 0)),
                  pl.BlockSpec((1, pw), lambda g, i: (0, g))],
        out_specs=[pl.BlockSpec((tt, pw), lambda g, i: (i, g))] * 2,
        out_shape=[jax.ShapeDtypeStruct((t_rows, ng * pw), BF16), jax.ShapeDtypeStruct((t_rows, ng * pw), F32)],
        compiler_params=_params(("arbitrary", "arbitrary")),
    )(u, u, w_pool, scale)


def _pool_bwd1(dz, pooled, w_pool, scale, tt=256):
    t_rows = dz.shape[0]
    pw = w_pool.shape[-1]
    ng = w_pool.shape[0]
    tt = _tile(t_rows, tt, 16)
    nt = t_rows // tt

    def body(dz_ref, p_ref, w_ref, sc_ref, dp_ref, dw_ref, dsc_ref):
        g, i = pl.program_id(0), pl.program_id(1)
        w = _pool_window(g)

        @pl.when(i == 0)
        def _():
            dw_ref[...] = jnp.zeros(dw_ref.shape, F32)
            dsc_ref[...] = jnp.zeros(dsc_ref.shape, F32)

        dz_v = dz_ref[...]
        pooled = p_ref[...]
        zpre = jnp.dot(pooled, w_ref[...], preferred_element_type=F32)
        dsc_ref[...] += _colsum(dz_v * zpre)
        dzs = (dz_v * sc_ref[...]).astype(BF16)
        dw_ref[...] += _dot_tn(pooled, dzs)
        t = i * tt + lax.broadcasted_iota(jnp.int32, (tt, 1), 0)
        cnt = jnp.minimum(t + 1, w).astype(F32)
        dp_ref[...] = _dot_nt(dzs, w_ref[...]) / cnt

    return pl.pallas_call(
        body, name="pool_bwd1", grid=(ng, nt),
        in_specs=[pl.BlockSpec((tt, pw), lambda g, i: (i, g)),
                  pl.BlockSpec((tt, pw), lambda g, i: (i, g)),
                  pl.BlockSpec((None, pw, pw), lambda g, i: (g, 0, 0)),
                  pl.BlockSpec((1, pw), lambda g, i: (0, g))],
        out_specs=[pl.BlockSpec((tt, pw), lambda g, i: (i, g)),
                   pl.BlockSpec((None, pw, pw), lambda g, i: (g, 0, 0)),
                   pl.BlockSpec((1, pw), lambda g, i: (0, g))],
        out_shape=[jax.ShapeDtypeStruct((t_rows, ng * pw), F32), jax.ShapeDtypeStruct((ng, pw, pw), F32),
                   jax.ShapeDtypeStruct((1, ng * pw), F32)],
        compiler_params=_params(("arbitrary", "arbitrary")),
    )(dz, pooled, w_pool, scale)


def _pool_bwd2(dps, ng, tt=256):
    t_rows, width = dps.shape
    pw = width // ng
    tt = _tile(t_rows, tt, 16)
    nt = t_rows // tt

    def body(dc_ref, dn_ref, dv_ref):
        g, i = pl.program_id(0), pl.program_id(1)
        w = _pool_window(g)
        r = lax.broadcasted_iota(jnp.int32, (tt, tt), 0)
        c = lax.broadcasted_iota(jnp.int32, (tt, tt), 1)
        band_c = ((c >= r) & (c < r + w)).astype(BF16)
        band_n = (c < r + w - tt).astype(BF16)
        dc = dc_ref[...]
        ch, cl = _split_bf16(dc)
        nh, nl = _split_bf16(dn_ref[...] * (i < nt - 1).astype(F32))
        sums = (jnp.dot(band_c, ch, preferred_element_type=F32) + jnp.dot(band_c, cl, preferred_element_type=F32)
                + jnp.dot(band_n, nh, preferred_element_type=F32) + jnp.dot(band_n, nl, preferred_element_type=F32))
        t = i * tt + lax.broadcasted_iota(jnp.int32, (tt, 1), 0)
        cnt = jnp.minimum(t + 1, w).astype(F32)
        dv_ref[...] = sums - dc * cnt

    return pl.pallas_call(
        body, name="pool_bwd2", grid=(ng, nt),
        in_specs=[pl.BlockSpec((tt, pw), lambda g, i: (i, g)),
                  pl.BlockSpec((tt, pw), lambda g, i: (jnp.minimum(i + 1, nt - 1), g))],
        out_specs=pl.BlockSpec((tt, pw), lambda g, i: (i, g)),
        out_shape=jax.ShapeDtypeStruct((t_rows, width), F32),
        compiler_params=_params(("arbitrary", "arbitrary")),
    )(dps, dps)


def _cpow(ar, ai, n):
    rr, ri, br, bi = None, None, ar, ai
    while n:
        if n & 1:
            rr, ri = (br, bi) if rr is None else (rr * br - ri * bi, rr * bi + ri * br)
        n >>= 1
        if n:
            br, bi = br * br - bi * bi, 2.0 * br * bi
    return rr, ri


def _chunk_carries(st_re, st_im, pr, pi, order):
    cb = st_re.shape[1]
    sub = lax.broadcasted_iota(jnp.int32, (S5_LANES, cb), 0)
    cr = jnp.zeros((S5_LANES, cb), F32)
    ci = jnp.zeros((S5_LANES, cb), F32)
    prev_r = jnp.zeros((1, cb), F32)
    prev_i = jnp.zeros((1, cb), F32)
    for k, src in order:
        er, ei = st_re[src:src + 1, :], st_im[src:src + 1, :]
        nr = er + pr * prev_r - pi * prev_i
        ni = ei + pr * prev_i + pi * prev_r
        cr = jnp.where(sub == k, jnp.broadcast_to(nr, (S5_LANES, cb)), cr)
        ci = jnp.where(sub == k, jnp.broadcast_to(ni, (S5_LANES, cb)), ci)
        prev_r, prev_i = nr, ni
    return cr, ci


def _s5_fwd(up, bblk, ab, cblk, tt=128):
    n_rows, ws = up.shape
    nb, cw, cb2 = bblk.shape
    cb = cb2 // 2
    lc = n_rows // S5_LANES
    tt = _tile(lc, tt, 1)
    nt = lc // tt
    rt = S5_LANES * tt

    def body(u_ref, b_ref, ab_ref, c_ref, y_ref, s_ref, bu_ref, st_re, st_im):
        ps, ti = pl.program_id(1), pl.program_id(2)
        ar = jnp.broadcast_to(ab_ref[0:1, :], (S5_LANES, cb))
        ai = jnp.broadcast_to(ab_ref[1:2, :], (S5_LANES, cb))

        @pl.when((ps == 0) & (ti == 0))
        def _():
            st_re[...] = jnp.zeros(st_re.shape, F32)
            st_im[...] = jnp.zeros(st_im.shape, F32)

        @pl.when((ps == 1) & (ti == 0))
        def _():
            pr, pi = _cpow(ab_ref[0:1, :], ab_ref[1:2, :], lc)
            cr, ci = _chunk_carries(st_re, st_im, pr, pi, [(k, k - 1) for k in range(1, S5_LANES)])
            st_re[...] = cr
            st_im[...] = ci

        bu_ref[...] = jnp.dot(u_ref[...], b_ref[...], preferred_element_type=F32)

        def step(t, carry, store):
            sr, si = carry
            rows = pl.ds(pl.multiple_of(t * S5_LANES, S5_LANES), S5_LANES)
            nr = ar * sr - ai * si + bu_ref[rows, 0:cb]
            ni = ar * si + ai * sr + bu_ref[rows, cb:cb2]
            if store:
                s_ref[rows, 0:cb] = nr
                s_ref[rows, cb:cb2] = ni
            return nr, ni

        @pl.when(ps == 0)
        def _():
            sr, si = lax.fori_loop(0, tt, functools.partial(step, store=False), (st_re[...], st_im[...]))
            st_re[...] = sr
            st_im[...] = si

        @pl.when(ps == 1)
        def _():
            sr, si = lax.fori_loop(0, tt, functools.partial(step, store=True), (st_re[...], st_im[...]))
            st_re[...] = sr
            st_im[...] = si
            y_ref[...] = jnp.dot(s_ref[...].astype(BF16), c_ref[...], preferred_element_type=F32)

    return pl.pallas_call(
        body, name="s5_fwd", grid=(nb, 2, nt),
        in_specs=[pl.BlockSpec((rt, cw), lambda j, ps, ti: (ti, j)),
                  pl.BlockSpec((None, cw, cb2), lambda j, ps, ti: (j, 0, 0)),
                  pl.BlockSpec((None, 2, cb), lambda j, ps, ti: (j, 0, 0)),
                  pl.BlockSpec((None, cb2, cw), lambda j, ps, ti: (j, 0, 0))],
        out_specs=[pl.BlockSpec((rt, cw), lambda j, ps, ti: (ti * ps, j)),
                   pl.BlockSpec((None, rt, cb2), lambda j, ps, ti: (j, ti * ps, 0))],
        out_shape=[jax.ShapeDtypeStruct((n_rows, ws), F32), jax.ShapeDtypeStruct((nb, n_rows, cb2), F32)],
        scratch_shapes=[pltpu.VMEM((rt, cb2), F32), pltpu.VMEM((S5_LANES, cb), F32), pltpu.VMEM((S5_LANES, cb), F32)],
        compiler_params=_params(("arbitrary", "arbitrary", "arbitrary")),
    )(up, bblk, ab, cblk)


def _s5_bwd(dyp, up, s_all, bblk_t, ab, cblk_t, tt=128):
    n_rows, ws = up.shape
    nb, cb2, cw = bblk_t.shape
    cb = cb2 // 2
    lc = n_rows // S5_LANES
    tt = _tile(lc, tt, 1)
    nt = lc // tt
    rt = S5_LANES * tt

    def body(dy_ref, u_ref, s_ref, bt_ref, ab_ref, ct_ref, du_ref, db_ref, dc_ref, da_ref, ds_ref, st_re, st_im):
        ps, ti = pl.program_id(1), pl.program_id(2)
        ar = jnp.broadcast_to(ab_ref[0:1, :], (S5_LANES, cb))
        ai = jnp.broadcast_to(ab_ref[1:2, :], (S5_LANES, cb))

        @pl.when((ps == 0) & (ti == 0))
        def _():
            st_re[...] = jnp.zeros(st_re.shape, F32)
            st_im[...] = jnp.zeros(st_im.shape, F32)
            db_ref[...] = jnp.zeros(db_ref.shape, F32)
            dc_ref[...] = jnp.zeros(dc_ref.shape, F32)
            da_ref[...] = jnp.zeros(da_ref.shape, F32)

        @pl.when((ps == 1) & (ti == 0))
        def _():
            pr, pi = _cpow(ab_ref[0:1, :], -ab_ref[1:2, :], lc)
            cr, ci = _chunk_carries(st_re, st_im, pr, pi, [(k, k + 1) for k in range(S5_LANES - 2, -1, -1)])
            st_re[...] = cr
            st_im[...] = ci

        ds_ref[...] = jnp.dot(dy_ref[...], ct_ref[...], preferred_element_type=F32)

        def rows_of(i):
            return pl.ds(pl.multiple_of((tt - 1 - i) * S5_LANES, S5_LANES), S5_LANES)

        def step0(i, carry):
            gr, gi = carry
            rows = rows_of(i)
            return (ar * gr + ai * gi + ds_ref[rows, 0:cb], ar * gi - ai * gr + ds_ref[rows, cb:cb2])

        def step1(i, carry):
            gr, gi, acr, aci = carry
            rows = rows_of(i)
            sr, si = s_ref[rows, 0:cb], s_ref[rows, cb:cb2]
            acr = acr + sr * gr + si * gi
            aci = aci + sr * gi - si * gr
            nr = ar * gr + ai * gi + ds_ref[rows, 0:cb]
            ni = ar * gi - ai * gr + ds_ref[rows, cb:cb2]
            ds_ref[rows, 0:cb] = nr
            ds_ref[rows, cb:cb2] = ni
            return nr, ni, acr, aci

        @pl.when(ps == 0)
        def _():
            gr, gi = lax.fori_loop(0, tt, step0, (st_re[...], st_im[...]))
            st_re[...] = gr
            st_im[...] = gi

        @pl.when(ps == 1)
        def _():
            zero = jnp.zeros((S5_LANES, cb), F32)
            gr, gi, acr, aci = lax.fori_loop(0, tt, step1, (st_re[...], st_im[...], zero, zero))
            st_re[...] = gr
            st_im[...] = gi
            da_ref[0] += acr
            da_ref[1] += aci
            dsb = ds_ref[...].astype(BF16)
            du_ref[...] = jnp.dot(dsb, bt_ref[...], preferred_element_type=F32)
            db_ref[...] += _dot_tn(u_ref[...], dsb)
            dc_ref[...] += _dot_tn(dy_ref[...], s_ref[...].astype(BF16))

    def tile_idx(ps, ti):
        return (nt - 1 - ti) * ps + (nt - 1) * (1 - ps)

    return pl.pallas_call(
        body, name="s5_bwd", grid=(nb, 2, nt),
        in_specs=[pl.BlockSpec((rt, cw), lambda j, ps, ti: (nt - 1 - ti, j)),
                  pl.BlockSpec((rt, cw), lambda j, ps, ti: (tile_idx(ps, ti), j)),
                  pl.BlockSpec((None, rt, cb2), lambda j, ps, ti: (j, tile_idx(ps, ti), 0)),
                  pl.BlockSpec((None, cb2, cw), lambda j, ps, ti: (j, 0, 0)),
                  pl.BlockSpec((None, 2, cb), lambda j, ps, ti: (j, 0, 0)),
                  pl.BlockSpec((None, cw, cb2), lambda j, ps, ti: (j, 0, 0))],
        out_specs=[pl.BlockSpec((rt, cw), lambda j, ps, ti: (tile_idx(ps, ti), j)),
                   pl.BlockSpec((None, cw, cb2), lambda j, ps, ti: (j, 0, 0)),
                   pl.BlockSpec((None, cw, cb2), lambda j, ps, ti: (j, 0, 0)),
                   pl.BlockSpec((None, 2, S5_LANES, cb), lambda j, ps, ti: (j, 0, 0, 0))],
        out_shape=[jax.ShapeDtypeStruct((n_rows, ws), F32), jax.ShapeDtypeStruct((nb, cw, cb2), F32),
                   jax.ShapeDtypeStruct((nb, cw, cb2), F32), jax.ShapeDtypeStruct((nb, 2, S5_LANES, cb), F32)],
        scratch_shapes=[pltpu.VMEM((rt, cb2), F32), pltpu.VMEM((S5_LANES, cb), F32), pltpu.VMEM((S5_LANES, cb), F32)],
        compiler_params=_params(("arbitrary", "arbitrary", "arbitrary")),
    )(dyp, up, s_all, bblk_t, ab, cblk_t)


def _s5_discretize(a_re, a_im, log_dt, b_re, b_im):
    dt = jnp.exp(log_dt)[:, None]
    mag = jnp.exp(a_re * dt)
    abar_re = mag * jnp.cos(a_im * dt)
    abar_im = mag * jnp.sin(a_im * dt)
    nr, ni = abar_re - 1.0, abar_im
    den = a_re * a_re + a_im * a_im
    fr = (nr * a_re + ni * a_im) / den
    fi = (ni * a_re - nr * a_im) / den
    bbar_re = fr[..., None] * b_re - fi[..., None] * b_im
    bbar_im = fr[..., None] * b_im + fi[..., None] * b_re
    return abar_re, abar_im, bbar_re, bbar_im


def _perm_rows(a):
    n, c = a.shape
    return a.reshape(S5_LANES, n // S5_LANES, c).transpose(1, 0, 2).reshape(n, c)


def _unperm_rows(a):
    n, c = a.shape
    return a.reshape(n // S5_LANES, S5_LANES, c).transpose(1, 0, 2).reshape(n, c)


HBM = pl.BlockSpec(memory_space=pltpu.HBM)
SEM = pl.BlockSpec(memory_space=pltpu.SEMAPHORE)
ANY = pl.BlockSpec(memory_space=pl.ANY)
EFFECT = pltpu.SideEffectType.DATAFLOW_SIDE_EFFECTING
COPIES_PER_BUFFER = {"ag_ici": 3, "ag_fwd": 3, "pair": N_CHIPS, "scatter": 3, "half": 1, "swap": 1, "bcast": 3}
PAIRED_KINDS = ("pair", "scatter", "swap")


def _place():
    x, y, c = lax.axis_index("x"), lax.axis_index("y"), lax.axis_index("c")
    chips = [(1 - x, y), (x, 1 - y), (1 - x, 1 - y)]
    return x, y, c, 2 * x + y, chips


def _n_copies(kind, n_bufs):
    if isinstance(kind, tuple):
        return len(kind[1])
    return COPIES_PER_BUFFER[kind] * (n_bufs // 2 if kind in PAIRED_KINDS else n_bufs)


def _comm_copies(kind, bufs):
    if isinstance(kind, tuple):
        full = _comm_copies(kind[0], bufs)
        return [full[k] for k in kind[1]]
    x, y, c, s, chips = _place()
    sib = (x, y, 1 - c)
    out = []
    if kind == "ag_ici":
        for w in bufs:
            for cx, cy in chips:
                out.append((w.at[s, c], w.at[s, c], w.at[2 * cx + cy, c], (cx, cy, c)))
    elif kind == "ag_fwd":
        for w in bufs:
            for cx, cy in chips:
                sj = 2 * cx + cy
                out.append((w.at[sj, c], w.at[sj, c], w.at[sj, 1 - c], sib))
    elif kind == "pair":
        n = len(bufs) // 2
        for g, got in zip(bufs[:n], bufs[n:]):
            for t in range(N_CHIPS):
                out.append((g.at[t, 1 - c], got.at[t], got.at[t], sib))
    elif kind == "scatter":
        n = len(bufs) // 2
        for p, got in zip(bufs[:n], bufs[n:]):
            for cx, cy in chips:
                out.append((p.at[2 * cx + cy], got.at[s], got.at[2 * cx + cy], (cx, cy, c)))
    elif kind == "half":
        for f in bufs:
            out.append((f.at[c], f.at[c], f.at[1 - c], sib))
    elif kind == "swap":
        n = len(bufs) // 2
        for v, got in zip(bufs[:n], bufs[n:]):
            out.append((v, got, got, sib))
    elif kind == "bcast":
        for w in bufs:
            for cx, cy in chips:
                out.append((w.at[s], w.at[s], w.at[2 * cx + cy], (cx, cy, c)))
    return out


def _comm_fused(name, kind, bufs):
    n = len(bufs)
    ncp = _n_copies(kind, n)

    def body(*refs):
        outs = refs[n:2 * n]
        send, recv = refs[2 * n:]
        copies = _comm_copies(kind, outs)
        started = []
        for k, (src, dst, _, peer) in enumerate(copies):
            cp = pltpu.make_async_remote_copy(src_ref=src, dst_ref=dst, send_sem=send.at[k], recv_sem=recv.at[k],
                                              device_id=peer, device_id_type=MESH)
            cp.start()
            started.append(cp)
        for k, (_, _, land, peer) in enumerate(copies):
            pltpu.make_async_remote_copy(src_ref=land, dst_ref=land, send_sem=send.at[k], recv_sem=recv.at[k],
                                         device_id=peer, device_id_type=MESH).wait_recv()
        for cp in started:
            cp.wait_send()

    return pl.pallas_call(
        body, name=name, in_specs=[ANY] * n, out_specs=[ANY] * n,
        out_shape=[jax.ShapeDtypeStruct(b.shape, b.dtype) for b in bufs],
        input_output_aliases={k: k for k in range(n)},
        scratch_shapes=[pltpu.SemaphoreType.DMA((ncp,))] * 2,
    )(*bufs)


def _comm_start(name, kind, bufs, after=None):
    n = len(bufs)
    ncp = _n_copies(kind, n)
    nx = 0 if after is None else 1

    def body(*refs):
        refs = refs[n + nx:]
        send, recv = refs[:ncp], refs[ncp:2 * ncp]
        outs = refs[2 * ncp:n + 2 * ncp]
        token = refs[n + 2 * ncp]
        for k, (src, dst, _, peer) in enumerate(_comm_copies(kind, outs)):
            pltpu.make_async_remote_copy(src_ref=src, dst_ref=dst, send_sem=send[k], recv_sem=recv[k],
                                         device_id=peer, device_id_type=MESH).start()
        token[...] = jnp.zeros(token.shape, token.dtype)

    res = pl.pallas_call(
        body, name=name, in_specs=[HBM] * n + [ANY] * nx,
        out_specs=[SEM] * (2 * ncp) + [HBM] * n + [pl.BlockSpec(memory_space=pltpu.VMEM)],
        out_shape=[pltpu.SemaphoreType.DMA(())] * (2 * ncp) + [pltpu.HBM(b.shape, b.dtype) for b in bufs]
        + [jax.ShapeDtypeStruct((8, 128), F32)],
        input_output_aliases={k: 2 * ncp + k for k in range(n)},
        compiler_params=pltpu.CompilerParams(has_side_effects=EFFECT),
    )(*[pltpu.with_memory_space_constraint(b, pltpu.HBM) for b in bufs], *([after] if nx else []))
    return list(res[:ncp]), list(res[ncp:2 * ncp]), list(res[2 * ncp:2 * ncp + n]), res[2 * ncp + n]


def _comm_wait(name, kind, bufs, send_sems, recv_sems, after):
    n = len(bufs)
    ncp = _n_copies(kind, n)

    def body(*refs):
        send, recv = refs[n:n + ncp], refs[n + ncp:n + 2 * ncp]
        outs = refs[n + 2 * ncp + 1:]
        for k, (src, _, land, peer) in enumerate(_comm_copies(kind, outs)):
            cp = pltpu.make_async_remote_copy(src_ref=src, dst_ref=land, send_sem=send[k], recv_sem=recv[k],
                                              device_id=peer, device_id_type=MESH)
            cp.wait_send()
            cp.wait_recv()

    return pl.pallas_call(
        body, name=name, in_specs=[HBM] * n + [SEM] * (2 * ncp) + [ANY], out_specs=[HBM] * n,
        out_shape=[pltpu.HBM(b.shape, b.dtype) for b in bufs],
        input_output_aliases={k: k for k in range(n)},
        compiler_params=pltpu.CompilerParams(has_side_effects=EFFECT),
    )(*bufs, *send_sems, *recv_sems, after)


def _pair_add(g, got, core):
    nchip, _, r, cw = g.shape
    tr = _tile(r, 512, 16)

    def body(c_ref, a_ref, b_ref, o_ref):
        o_ref[...] = a_ref[...] + b_ref[...]

    return pl.pallas_call(
        body, name="grads_pair_add",
        grid_spec=pltpu.PrefetchScalarGridSpec(
            num_scalar_prefetch=1, grid=(nchip, r // tr),
            in_specs=[pl.BlockSpec((None, None, tr, cw), lambda s, i, c_ref: (s, c_ref[0], i, 0)),
                      pl.BlockSpec((None, tr, cw), lambda s, i, c_ref: (s, i, 0))],
            out_specs=pl.BlockSpec((None, tr, cw), lambda s, i, c_ref: (s, i, 0))),
        out_shape=jax.ShapeDtypeStruct((nchip, r, cw), BF16),
        compiler_params=_params(("arbitrary", "arbitrary")),
    )(core, g, got)


def _chip_sum(parts, got, idx):
    _, r, cw = parts.shape
    tr = _tile(r, 512, 16)

    def body(i_ref, own_ref, a_ref, b_ref, c_ref, o_ref):
        o_ref[...] = ((own_ref[...].astype(F32) + a_ref[...].astype(F32)) + b_ref[...].astype(F32)) + c_ref[...].astype(F32)

    def slot(k):
        return pl.BlockSpec((None, tr, cw), lambda i, i_ref: (i_ref[k], i, 0))

    return pl.pallas_call(
        body, name="grads_chip_sum",
        grid_spec=pltpu.PrefetchScalarGridSpec(
            num_scalar_prefetch=1, grid=(r // tr,), in_specs=[slot(0), slot(1), slot(2), slot(3)], out_specs=slot(4)),
        out_shape=jax.ShapeDtypeStruct((2, r, cw), F32),
        compiler_params=_params(("arbitrary",)),
    )(idx, parts, got, got, got)


def _add_into_slot(v, got, chip):
    r, cw = v.shape
    tr = _tile(r, 256)

    def body(c_ref, a_ref, b_ref, o_ref):
        o_ref[...] = a_ref[...] + b_ref[...]

    tile = pl.BlockSpec((tr, cw), lambda i, c_ref: (i, 0))
    return pl.pallas_call(
        body, name="small_pair_add",
        grid_spec=pltpu.PrefetchScalarGridSpec(
            num_scalar_prefetch=1, grid=(r // tr,), in_specs=[tile, tile],
            out_specs=pl.BlockSpec((None, tr, cw), lambda i, c_ref: (c_ref[0], i, 0))),
        out_shape=jax.ShapeDtypeStruct((N_CHIPS, r, cw), F32),
        compiler_params=_params(("arbitrary",)),
    )(chip, v, got)


def _sum_slots(w):
    _, r, cw = w.shape
    tr = _tile(r, 256)

    def body(w_ref, o_ref):
        o_ref[...] = ((w_ref[0] + w_ref[1]) + w_ref[2]) + w_ref[3]

    return pl.pallas_call(
        body, name="small_chip_sum", grid=(r // tr,),
        in_specs=[pl.BlockSpec((N_CHIPS, tr, cw), lambda i: (0, i, 0))],
        out_specs=pl.BlockSpec((tr, cw), lambda i: (i, 0)),
        out_shape=jax.ShapeDtypeStruct((r, cw), F32),
        compiler_params=_params(("arbitrary",)),
    )(w)


def _adamw_math(w, g, m, v):
    m = ADAM_B1 * m + (1.0 - ADAM_B1) * g
    v = ADAM_B2 * v + (1.0 - ADAM_B2) * (g * g)
    m_hat = m / (1.0 - ADAM_B1 ** ADAM_STEP)
    v_hat = v / (1.0 - ADAM_B2 ** ADAM_STEP)
    delta = -ADAM_LR * (m_hat / (jnp.sqrt(v_hat) + ADAM_EPS) + ADAM_WD * w)
    return delta, m, v


def _adamw(name, w, m, v, g, g_half=0, g_row_off=0, tr=256):
    r, cw = w.shape
    tr = _tile(math.gcd(r, g_row_off) if g_row_off else r, tr)
    off = g_row_off // tr

    def body(w_ref, m_ref, v_ref, g_ref, go_ref, d_ref, mo_ref, vo_ref):
        g_v = g_ref[...]
        delta, m_n, v_n = _adamw_math(w_ref[...], g_v, m_ref[...], v_ref[...])
        go_ref[...] = g_v
        d_ref[...] = delta
        mo_ref[...] = m_n
        vo_ref[...] = v_n

    tile = pl.BlockSpec((tr, cw), lambda i: (i, 0))
    out = jax.ShapeDtypeStruct((r, cw), F32)
    return pl.pallas_call(
        body, name=name, grid=(r // tr,),
        in_specs=[tile, tile, tile, pl.BlockSpec((None, tr, cw), lambda i: (g_half, i + off, 0))],
        out_specs=[tile] * 4, out_shape=[out] * 4,
        compiler_params=_params(("arbitrary",)),
    )(w, m, v, g)


def kernel(x, mem, g_ffn1, w1_gate, w1_up, w1_down, g_mix, w_in, ssm_a_re, ssm_a_im, ssm_log_dt, ssm_b_re, ssm_b_im, ssm_c_re, ssm_c_im, ssm_d, w_glu, b_glu, w_pool, pool_scale, g_out_ssm, g_out_pool, w_out, g_xattn, g_mem, w_q, w_k, w_v, w_o, g_ffn2, w2_gate, w2_up, w2_down, g_final, loss_target, m_g_ffn1, m_w1_gate, m_w1_up, m_w1_down, m_g_mix, m_w_in, m_ssm_a_re, m_ssm_a_im, m_ssm_log_dt, m_ssm_b_re, m_ssm_b_im, m_ssm_c_re, m_ssm_c_im, m_ssm_d, m_w_glu, m_b_glu, m_w_pool, m_pool_scale, m_g_out_ssm, m_g_out_pool, m_w_out, m_g_xattn, m_g_mem, m_w_q, m_w_k, m_w_v, m_w_o, m_g_ffn2, m_w2_gate, m_w2_up, m_w2_down, m_g_final, v_g_ffn1, v_w1_gate, v_w1_up, v_w1_down, v_g_mix, v_w_in, v_ssm_a_re, v_ssm_a_im, v_ssm_log_dt, v_ssm_b_re, v_ssm_b_im, v_ssm_c_re, v_ssm_c_im, v_ssm_d, v_w_glu, v_b_glu, v_w_pool, v_pool_scale, v_g_out_ssm, v_g_out_pool, v_w_out, v_g_xattn, v_g_mem, v_w_q, v_w_k, v_w_v, v_w_o, v_g_ffn2, v_w2_gate, v_w2_up, v_w2_down, v_g_final):
    local = dict(locals())
    wts = {n: local[n] for n in WEIGHTS}
    mom = {n: local["m_" + n] for n in WEIGHTS}
    var = {n: local["v_" + n] for n in WEIGHTS}

    x2 = x[0]
    mem2 = mem[0]
    tgt = loss_target[0]
    t_rows, d = x2.shape
    fs = w1_gate.shape[-1]
    ds_ = w_in.shape[1]
    ws = d // 2
    n_pg = len(POOL_WINDOWS)
    pw = ws // n_pg
    n_grp = ws // SSM_GROUP
    n_state = ssm_a_re.shape[-1]
    cx_, cy_, cc_ = lax.axis_index("x"), lax.axis_index("y"), lax.axis_index("c")
    chip = (2 * cx_ + cy_).astype(jnp.int32)
    core = cc_.astype(jnp.int32).reshape(1)
    chip_idx = jnp.stack([chip, chip ^ 2, chip ^ 1, chip ^ 3, cc_.astype(jnp.int32)])

    glu_rows = w_glu[0].reshape(-1, d)
    pool_rows = w_pool[0].reshape(-1, d)
    n_glu, n_pool = glu_rows.shape[0], pool_rows.shape[0]
    PACKED = ['w_out', 'w_q', 'w_k', 'w_v', 'w_o']
    glu_at = len(PACKED) * ds_
    pool_at = glu_at + n_glu
    n_pad = -(pool_at + n_pool) % 32
    rp = pool_at + n_pool + n_pad

    def own_slot(src):
        src = src.astype(BF16)
        return lax.dynamic_update_slice(lax.empty((N_CHIPS,) + src.shape, BF16), src[None], (chip, 0, 0, 0))

    src_packed = jnp.concatenate([wts[n][0] for n in PACKED] + [glu_rows, pool_rows, jnp.zeros((n_pad, d), F32)], 0)
    src_up1 = jnp.stack([w1_gate[0], w1_up[0]]).astype(BF16)
    w_bufs = [own_slot(src_up1), own_slot(w1_down[0].reshape(2, fs // 2, d)),
              own_slot(w_in[0].reshape(2, ds_ // 2, d)), own_slot(src_packed.reshape(2, rp // 2, d)),
              own_slot(jnp.stack([w2_gate[0], w2_up[0]])), own_slot(w2_down[0].reshape(2, fs // 2, d))]
    near_send, near_recv, w0, ag_token = _comm_start("weights_start_near", ("ag_ici", (0, 1)), w_bufs[:1])

    def gathered(k, after):
        w = _comm_wait("weights_wait_%d" % k, "ag_ici", [w_bufs[k]], ag_send[3 * k:3 * k + 3],
                       ag_recv[3 * k:3 * k + 3], after)
        return _comm_fused("weights_forward_%d" % k, "ag_fwd", w)[0]

    def gathered_start(k, after):
        w = _comm_wait("weights_wait_%d" % k, "ag_ici", [w_bufs[k]], ag_send[3 * k:3 * k + 3],
                       ag_recv[3 * k:3 * k + 3], after)
        send, recv, thru, token = _comm_start("weights_forward_start_%d" % k, "ag_fwd", w)
        return (send, recv, thru), token

    def gathered_finish(k, handle, after):
        send, recv, thru = handle
        return _comm_wait("weights_forward_wait_%d" % k, "ag_fwd", thru, send, recv, after)[0]

    n1 = _rmsnorm("norm_ffn1", x2, wts['g_ffn1'].reshape(1, -1), deps=[ag_token])
    zero1 = jnp.zeros((1,), jnp.int32)
    near, far = jnp.stack([chip ^ 2, chip ^ 1]), (chip ^ 3).reshape(1)
    a1, b1, hm1 = _ffn_up("ffn1_up_own", n1, src_up1[None], zero1, chip.reshape(1), d, fs)
    n_rest = 3 * len(w_bufs) - 2
    rest_send, rest_recv, w_bufs, _ = _comm_start("weights_start_rest", ("ag_ici", tuple(range(2, 2 + n_rest))),
                                                  w0 + w_bufs[1:], after=hm1)
    ag_send, ag_recv = near_send + rest_send, near_recv + rest_recv
    w0 = _comm_wait("weights_wait_0_near", ("ag_ici", (0, 1)), [w_bufs[0]], ag_send[0:2], ag_recv[0:2], hm1)
    w0 = _comm_fused("weights_forward_0_near", ("ag_fwd", (0, 1)), w0)
    a1, b1, hm1 = _ffn_up("ffn1_up_near", n1, w0[0], near, near, d, fs, into=(a1, b1, hm1))
    w0 = _comm_wait("weights_wait_0_far", ("ag_ici", (2,)), w0, ag_send[2:3], ag_recv[2:3], hm1)
    ga1 = _comm_fused("weights_forward_0_far", ("ag_fwd", (2,)), w0)[0]
    a1, b1, hm1 = _ffn_up("ffn1_up_far", n1, ga1, far, far, d, fs, into=(a1, b1, hm1))
    gd1 = gathered(1, hm1).reshape(N_CHIPS, fs, d)
    h1, n2 = _mm_nn("ffn1_down", hm1, gd1, (None, fs, d), lambda j, k: (k, 0, 0), N_CHIPS, d, F32, res=x2, alpha=0.5,
                    norm_g=wts['g_mix'].reshape(1, -1))
    dd_bufs = {'w_in': gathered(2, h1).reshape(N_CHIPS, ds_, d)}
    DD = {n: q for q, n in enumerate(PACKED)}
    DD['w_in'] = 0

    def mm_dd(name, a, wname, out_dtype, res=None, norm_g=None):
        q = DD[wname]
        return _mm_nn(name, a, dd_bufs[wname], (N_CHIPS, ds_, d), lambda j, k: (0, q, 0), 1, d, out_dtype, res=res,
                      norm_g=norm_g, slabs=N_CHIPS)

    def mm_dd_t(name, pairs, out_dtype, deps=()):
        ps = [(dy, dd_bufs[w], (N_CHIPS, ds_, d), functools.partial(lambda s, q: (0, q, 0), q=DD[w])) for dy, w in pairs]
        return _mm_nt_cols(name, ps, N_CHIPS, ds_, [out_dtype], deps=deps, slabs=N_CHIPS)[0]

    def vec(n):
        return wts[n].reshape(1, -1)

    disc_in = (ssm_a_re[0], ssm_a_im[0], ssm_log_dt[0], ssm_b_re[0], ssm_b_im[0])
    (abar_re, abar_im, bbar_re, bbar_im), disc_vjp = jax.vjp(_s5_discretize, *disc_in)
    gpb = min(S5_GROUPS_PER_BLOCK, n_grp)
    nb = n_grp // gpb
    cb = gpb * n_state
    eye = jnp.eye(gpb, dtype=F32)

    def blockdiag(t):
        return jnp.einsum('jgph,gk->jghkp', t.reshape(nb, gpb, n_state, SSM_GROUP), eye).reshape(nb, gpb * SSM_GROUP, cb)

    def blockdiag_c(t):
        return jnp.einsum('jghp,gk->jkpgh', t.reshape(nb, gpb, SSM_GROUP, n_state), eye).reshape(nb, cb, gpb * SSM_GROUP)

    bblk = jnp.concatenate([blockdiag(bbar_re), blockdiag(bbar_im)], -1).astype(BF16)
    cblk = jnp.concatenate([blockdiag_c(ssm_c_re[0]), -blockdiag_c(ssm_c_im[0])], 1).astype(BF16)
    ab = jnp.stack([abar_re.reshape(nb, cb), abar_im.reshape(nb, cb)], 1)

    u = mm_dd("mix_in", n2, 'w_in', F32)

    up = _perm_rows(u[:, :ws]).astype(BF16)
    ylin_p, s_all = _s5_fwd(up, bblk, ab, cblk)
    ylin = _unperm_rows(ylin_p)

    def gelu_fn(r, v):
        y1 = r[0] + v[0] * r[1]
        y2 = jax.nn.gelu(y1)
        return [y2, y2], []
    fwd_pk, tok = gathered_start(3, ylin_p)
    fwd_a2, tok = gathered_start(4, tok)
    y2, y2b = _rowwise("s5_gelu", gelu_fn, [ylin, (u, 0, ws)], [vec('ssm_d')], [(ws, F32), (ws, BF16)], deps=[tok])
    packed = gathered_finish(3, fwd_pk, y2b).reshape(N_CHIPS, rp, d)
    for n in PACKED:
        dd_bufs[n] = packed
    wglu_full = packed[:, glu_at:glu_at + n_glu, :].reshape(ws, ws)
    wpool_full = packed[:, pool_at:pool_at + n_pool, :].reshape(N_CHIPS, n_pg, pw // N_CHIPS, pw)
    wpool_full = wpool_full.transpose(1, 0, 2, 3).reshape(n_pg, pw, pw)
    z = _mm_nn("s5_glu", y2b, wglu_full, (ws, ws), lambda j, k: (0, 0), 1, ws, F32)

    def glu_fn(r, v):
        y3 = r[0] * _sigmoid(r[1] + v[0])
        return [_rms_fwd(y3, v[1])], []
    m_ssm = _rowwise("s5_gate_norm", glu_fn, [y2, z], [vec('b_glu'), vec('g_out_ssm')], [(ws, BF16)])[0]

    pooled, zp = _pool_fwd(u, ws // pw, wpool_full, vec('pool_scale'))
    fwd_d2, tok = gathered_start(5, zp)
    m_pool = _rmsnorm("norm_pool", zp, vec('g_out_pool'), deps=[tok])
    merged = jnp.concatenate([m_ssm, m_pool], -1)
    h2, hn = mm_dd("mix_out", merged, 'w_out', F32, res=h1, norm_g=vec('g_xattn'))

    memn = _rmsnorm("norm_mem", mem2, vec('g_mem'))
    k_mem = mm_dd("attn_k", memn, 'w_k', BF16)
    v_mem = mm_dd("attn_v", memn, 'w_v', BF16)
    q = mm_dd("attn_q", hn, 'w_q', BF16)
    o = _attn_fwd(q, k_mem, v_mem)
    h3, n4 = mm_dd("attn_out", o, 'w_o', F32, res=h2, norm_g=vec('g_ffn2'))

    ga2 = gathered_finish(4, fwd_a2, h3)
    all_chips = jnp.arange(N_CHIPS, dtype=jnp.int32)
    a2, b2, hm2 = _ffn_up("ffn2_up", n4, ga2, all_chips, all_chips, d, fs)
    gd2 = gathered_finish(5, fwd_d2, hm2).reshape(N_CHIPS, fs, d)
    h4 = _mm_nn("ffn2_down", hm2, gd2, (None, fs, d), lambda j, k: (k, 0, 0), N_CHIPS, d, F32, res=h3, alpha=0.5)

    def loss_fn(r, v):
        h, t = r
        e = _rms_fwd(h, v[0]) - t
        dy = e * (1.0 / d)
        dh, dg = _rms_bwd(dy, h, v[0])
        part = jnp.sum(_colsum(e * e), axis=1, keepdims=True) * (0.5 / d)
        return [dh, 0.5 * dh], [_colsum(dg), jnp.broadcast_to(part, (1, 128))]
    dh4, dy_f2, dg_final, loss_row = _rowwise("loss_head", loss_fn, [h4, tgt], [g_final.reshape(1, -1)],
                                              [(d, F32), (d, BF16)], [d, 128])

    def rs_pair_start(tag, gbufs, after=None):
        land = [lax.empty((N_CHIPS,) + g.shape[2:], BF16) for g in gbufs]
        send, recv, thru, token = _comm_start(tag + "_pair_start", "pair", list(gbufs) + land, after=after)
        return (send, recv, thru), token

    def rs_scatter_start(tag, handle, after):
        send, recv, thru = handle
        n = len(thru) // 2
        res = _comm_wait(tag + "_pair_wait", "pair", thru, send, recv, after)
        parts = [_pair_add(g, r, core) for g, r in zip(res[:n], res[n:])]
        land = [lax.empty(p.shape, BF16) for p in parts]
        send, recv, thru, token = _comm_start(tag + "_scatter_start", "scatter", parts + land)
        return (send, recv, thru), token

    def rs_half_start(tag, handle, after):
        send, recv, thru = handle
        n = len(thru) // 2
        res = _comm_wait(tag + "_scatter_wait", "scatter", thru, send, recv, after)
        full = [_chip_sum(p, g2, chip_idx) for p, g2 in zip(res[:n], res[n:])]
        send, recv, thru, token = _comm_start(tag + "_half_start", "half", full)
        return (send, recv, thru), token

    def rs_finish(tag, handle, after):
        send, recv, thru = handle
        return _comm_wait(tag + "_half_wait", "half", thru, send, recv, after)

    wblk = (None, None, d, fs)

    def ffn_down_bwd(tag, dy_half, a, b, hm, gd_l, deps=()):
        da, db = _mm_nt_cols(tag + "_down_bwd", [(dy_half, gd_l, (None, fs, d), lambda s: (s, 0, 0))],
                             N_CHIPS, fs, [BF16, BF16], epi=_swiglu_bwd, extras=[a, b], deps=deps, row_parts=2)
        g_down = _mm_tn(tag + "_dw_down", hm, dy_half, fs, d // 2, N_CHIPS, 2, jax.ShapeDtypeStruct((N_CHIPS, fs, d), BF16),
                        (None, fs, d // 2), lambda p, q: (p, 0, q), tt=2048)
        return da, db, g_down.reshape(N_CHIPS, 2, fs // 2, d)

    def ffn_up_bwd(tag, da, db, ga_l, deps=()):
        return _mm_nt_k(tag + "_up_bwd", [(da, ga_l, wblk, lambda s: (s, 0, 0, 0)), (db, ga_l, wblk, lambda s: (s, 1, 0, 0))],
                        N_CHIPS, d, F32, deps=deps)

    def ffn_dw(name, dact, n_in, deps=()):
        return _mm_tn(name, n_in, dact, d // 2, fs, 2, N_CHIPS, jax.ShapeDtypeStruct((N_CHIPS, 2, d // 2, fs), BF16),
                      (None, None, d // 2, fs), lambda p, q: (q, p, 0, 0), tt=2048, deps=deps)

    def dw_dd(name, a, dy, wname, grad_b2):
        q = DD[wname]
        rows = ds_ if wname == 'w_in' else rp
        return _mm_tn(name, a, dy, ds_, d, N_CHIPS, 1, jax.ShapeDtypeStruct((N_CHIPS, rows, d), BF16),
                      (None, ds_, d), lambda p, qq: (p, q, 0), into=grad_b2, tt=2048)

    def norm_bwd(name, dn, h, gname, dres, deps=(), scale=1.0):
        def fn(r, v):
            dx, dg = _rms_bwd(r[0], r[1], v[0])
            tot = dx + r[2]
            return [tot, scale * tot], [_colsum(dg)]
        return _rowwise(name, fn, [dn, h, dres], [vec(gname)], [(d, F32), (d, BF16)], [d], deps=deps)

    da2, db2, g_down2 = ffn_down_bwd("ffn2", dy_f2, a2, b2, hm2, gd2)
    dn4 = ffn_up_bwd("ffn2", da2, db2, ga2)
    g_gate2 = ffn_dw("ffn2_dw_gate", da2, n4)
    g_up2 = ffn_dw("ffn2_dw_up", db2, n4)
    rs_f2, tok = rs_pair_start("ffn2", [g_gate2, g_up2, g_down2])
    dh3, dh3b, dg_ffn2 = norm_bwd("norm_ffn2_bwd", dn4, h3, 'g_ffn2', dh4, deps=[tok])
    rs_f2, tok = rs_scatter_start("ffn2", rs_f2, dh3b)

    do = mm_dd_t("attn_out_bwd", [(dh3b, 'w_o')], BF16, deps=[tok])
    grad_b2 = dw_dd("attn_dw_o", o, dh3b, 'w_o', None)
    dq, dk, dv = _attn_bwd(q, k_mem, v_mem, do)
    dkb, dvb = dk.astype(BF16), dv.astype(BF16)
    grad_b2 = dw_dd("attn_dw_q", hn, dq, 'w_q', grad_b2)
    dhn = mm_dd_t("attn_q_bwd", [(dq, 'w_q')], F32)
    dh2, dh2b, dg_xattn = norm_bwd("norm_xattn_bwd", dhn, h2, 'g_xattn', dh3)
    grad_b2 = dw_dd("attn_dw_k", memn, dkb, 'w_k', grad_b2)
    grad_b2 = dw_dd("attn_dw_v", memn, dvb, 'w_v', grad_b2)
    dmemn = mm_dd_t("attn_kv_bwd", [(dkb, 'w_k'), (dvb, 'w_v')], F32)
    dg_mem = _rowwise("norm_mem_bwd", lambda r, v: ([], [_colsum(_rms_bwd(r[0], r[1], v[0])[1])]),
                      [dmemn, mem2], [vec('g_mem')], [], [d])[0]

    dmerged = mm_dd_t("mix_out_bwd", [(dh2b, 'w_out')], F32)
    grad_b2 = dw_dd("mix_dw_out", merged, dh2b, 'w_out', grad_b2)

    def gate_bwd_fn(r, v):
        dm, y2_v, z_v = r
        sg = _sigmoid(z_v + v[0])
        y3 = y2_v * sg
        dy3, dg = _rms_bwd(dm, y3, v[1])
        dz = dy3 * y3 * (1.0 - sg)
        return [dy3 * sg, dz], [_colsum(dg), _colsum(dz)]
    dy2a, dzb, dg_out_ssm, db_glu = _rowwise("s5_gate_norm_bwd", gate_bwd_fn, [(dmerged, 0, ws), y2, z],
                                             [vec('b_glu'), vec('g_out_ssm')], [(ws, F32), (ws, BF16)], [ws, ws])
    dy2b_ = _mm_nt_cols("s5_glu_bwd", [(dzb, wglu_full, (ws, ws), lambda s: (0, 0))], 1, ws, [F32])[0]
    dw_glu = _mm_tn("s5_dw_glu", y2b, dzb, ws, ws, 1, 1, jax.ShapeDtypeStruct((ws, ws), F32), (ws, ws), lambda p, q: (0, 0))

    def gelu_bwd_fn(r, v):
        dy2 = r[0] + r[1]
        us = r[3]
        y1 = r[2] + v[0] * us
        kk = math.sqrt(2.0 / math.pi)
        th = jnp.tanh(kk * (y1 + 0.044715 * y1 * y1 * y1))
        dgelu = 0.5 * (1.0 + th) + 0.5 * y1 * (1.0 - th * th) * kk * (1.0 + 3.0 * 0.044715 * y1 * y1)
        dy1 = dy2 * dgelu
        return [dy1, dy1 * v[0]], [_colsum(dy1 * us)]
    dy1b, du_skip, d_ssm_d = _rowwise("s5_gelu_bwd", gelu_bwd_fn, [dy2a, dy2b_, ylin, (u, 0, ws)], [vec('ssm_d')],
                                      [(ws, BF16), (ws, F32)], [ws])

    bblk_t = jnp.swapaxes(bblk, 1, 2)
    cblk_t = jnp.swapaxes(cblk, 1, 2)
    du_p, d_bblk, d_cblk_t, d_ab = _s5_bwd(_perm_rows(dy1b), up, s_all, bblk_t, ab, cblk_t)
    du_ssm = _unperm_rows(du_p)

    dzp, dg_out_pool = _rowwise("norm_pool_bwd", lambda r, v: (lambda dx, dg: ([dx], [_colsum(dg)]))(*_rms_bwd(r[0], r[1], v[0])),
                                [(dmerged, 1, ws), zp], [vec('g_out_pool')], [(ws, F32)], [ws])
    dps, dw_pool, d_pool_scale = _pool_bwd1(dzp, pooled, wpool_full, vec('pool_scale'))
    du_pool = _pool_bwd2(dps, n_pg)

    dub = _rowwise("mix_du", lambda r, v: ([jnp.concatenate([r[0] + r[1], r[2]], -1)], []),
                   [du_ssm, du_skip, du_pool], [], [(d, BF16)])[0]
    dn2 = mm_dd_t("mix_in_bwd", [(dub, 'w_in')], F32)
    grad_in = dw_dd("mix_dw_in", n2, dub, 'w_in', None)
    tail_g = jnp.concatenate([
        dw_glu.reshape(N_CHIPS, n_glu, d),
        dw_pool.reshape(n_pg, N_CHIPS, pw // N_CHIPS, pw).transpose(1, 0, 2, 3).reshape(N_CHIPS, n_pool, d),
        jnp.zeros((N_CHIPS, n_pad, d), F32)], 1).astype(BF16)
    grad_b2 = lax.dynamic_update_slice(grad_b2, tail_g, (0, glu_at, 0))
    rs_mix, tok = rs_pair_start("mixers", [grad_b2.reshape(N_CHIPS, 2, rp // 2, d),
                                           grad_in.reshape(N_CHIPS, 2, ds_ // 2, d)])
    dh1, dy_f1, dg_mix = norm_bwd("norm_mix_bwd", dn2, h1, 'g_mix', dh2, deps=[tok], scale=0.5)
    rs_mix, tok = rs_scatter_start("mixers", rs_mix, dy_f1)

    da1, db1, g_down1 = ffn_down_bwd("ffn1", dy_f1, a1, b1, hm1, gd1, deps=[tok])
    rs_d1, tok = rs_pair_start("ffn1_down", [g_down1])
    dn1 = ffn_up_bwd("ffn1", da1, db1, ga1, deps=[tok])
    rs_d1, tok = rs_scatter_start("ffn1_down", rs_d1, dn1)
    grad_x, _, dg_ffn1 = norm_bwd("norm_ffn1_bwd", dn1, x2, 'g_ffn1', dh1, deps=[tok])

    def undiag(t):
        return jnp.einsum('jghkp,gk->jgph', t.reshape(nb, gpb, SSM_GROUP, gpb, n_state), eye).reshape(n_grp, n_state, SSM_GROUP)

    d_bbar_re, d_bbar_im = undiag(d_bblk[:, :, :cb]), undiag(d_bblk[:, :, cb:])
    d_c_re = undiag(d_cblk_t[:, :, :cb]).transpose(0, 2, 1)
    d_c_im = -undiag(d_cblk_t[:, :, cb:]).transpose(0, 2, 1)
    d_abar = jnp.sum(d_ab, axis=2).reshape(nb, 2, gpb, n_state)
    d_abar_re = d_abar[:, 0].reshape(n_grp, n_state)
    d_abar_im = d_abar[:, 1].reshape(n_grp, n_state)
    d_a_re, d_a_im, d_log_dt, d_b_re, d_b_im = disc_vjp((d_abar_re, d_abar_im, d_bbar_re, d_bbar_im))

    small_g = {'g_ffn1': dg_ffn1, 'g_mix': dg_mix, 'ssm_a_re': d_a_re, 'ssm_a_im': d_a_im, 'ssm_log_dt': d_log_dt,
               'ssm_b_re': d_b_re, 'ssm_b_im': d_b_im, 'ssm_c_re': d_c_re, 'ssm_c_im': d_c_im, 'ssm_d': d_ssm_d,
               'b_glu': db_glu, 'pool_scale': d_pool_scale, 'g_out_ssm': dg_out_ssm, 'g_out_pool': dg_out_pool,
               'g_xattn': dg_xattn, 'g_mem': dg_mem, 'g_ffn2': dg_ffn2, 'g_final': dg_final}
    sizes = [wts[n].size for n in SMALL]
    total = sum(sizes) + 128
    rows_s = -(-total // (128 * 256)) * 256
    flat = jnp.concatenate([small_g[n].reshape(-1) for n in SMALL] + [loss_row.reshape(-1)])
    flat = jnp.pad(flat, (0, rows_s * 128 - total)).reshape(rows_s, 128)
    sw_send, sw_recv, sw_thru, tok = _comm_start("small_swap_start", "swap", [flat, lax.empty(flat.shape, F32)])
    g_gate1 = ffn_dw("ffn1_dw_gate", da1, n1, deps=[tok])
    rs_g1, tok_g1 = rs_pair_start("ffn1_gate", [g_gate1])
    sw_v, sw_got = _comm_wait("small_swap_wait", "swap", sw_thru, sw_send, sw_recv, tok_g1)
    slots = _add_into_slot(sw_v, sw_got, chip.reshape(1))
    bc_send, bc_recv, bc_thru, tok = _comm_start("small_bcast_start", "bcast", [slots])
    g_up1 = ffn_dw("ffn1_dw_up", db1, n1, deps=[tok])
    rs_g1, tok = rs_scatter_start("ffn1_gate", rs_g1, g_up1)
    slots, = _comm_wait("small_bcast_wait", "bcast", bc_thru, bc_send, bc_recv, tok)
    red = _sum_slots(slots).reshape(-1)
    loss = red[sum(sizes)]

    def flat_small(t):
        return jnp.pad(jnp.concatenate([t[n].reshape(-1) for n in SMALL]), (0, rows_s * 128 - sum(sizes))).reshape(rows_s, 128)
    sg_, sd_, sm_, sv_ = _adamw("adamw_small", flat_small(wts), flat_small(mom), flat_small(var), red.reshape(1, rows_s, 128))
    out = {}
    off = 0
    for n, sz in zip(SMALL, sizes):
        for key, arr in (('grad', sg_), ('delta', sd_), ('m', sm_), ('v', sv_)):
            out[key, n] = arr.reshape(-1)[off:off + sz].reshape(wts[n].shape)
        off += sz

    def upd(n, g_arr, half, row_off, shape2):
        res = _adamw("adamw_" + n, wts[n].reshape(shape2), mom[n].reshape(shape2), var[n].reshape(shape2), g_arr, half, row_off)
        for key, arr in zip(('grad', 'delta', 'm', 'v'), res):
            out[key, n] = arr.reshape(wts[n].shape)
        return res[3]

    rs_u1, tok = rs_pair_start("ffn1_up", [g_up1], after=sv_)
    rs_f2, tok = rs_half_start("ffn2", rs_f2, tok)
    rs_u1, tok = rs_scatter_start("ffn1_up", rs_u1, tok)
    rs_mix, tok = rs_half_start("mixers", rs_mix, tok)
    full_gate2, full_up2, full_down2 = rs_finish("ffn2", rs_f2, tok)
    upd('w2_gate', full_gate2.reshape(1, d, fs), 0, 0, (d, fs))
    upd('w2_up', full_up2.reshape(1, d, fs), 0, 0, (d, fs))
    last = upd('w2_down', full_down2.reshape(1, fs, d), 0, 0, (fs, d))
    rs_d1, tok = rs_half_start("ffn1_down", rs_d1, last)
    full_b2, full_in = rs_finish("mixers", rs_mix, tok)
    full_b2 = full_b2.reshape(1, rp, d)
    last = upd('w_in', full_in.reshape(1, ds_, d), 0, 0, (ds_, d))
    for n in PACKED:
        last = upd(n, full_b2, 0, DD[n] * ds_, (ds_, d))
    glu_shape, pool_shape = (ws // N_CHIPS, ws), (n_pg * pw // N_CHIPS, pw)
    upd('w_glu', full_b2[:, glu_at:glu_at + n_glu].reshape((1,) + glu_shape), 0, 0, glu_shape)
    upd('w_pool', full_b2[:, pool_at:pool_at + n_pool].reshape((1,) + pool_shape), 0, 0, pool_shape)
    full_down1, = rs_finish("ffn1_down", rs_d1, last)
    last = upd('w1_down', full_down1.reshape(1, fs, d), 0, 0, (fs, d))
    rs_g1, tok = rs_half_start("ffn1_gate", rs_g1, last)
    rs_u1, tok = rs_half_start("ffn1_up", rs_u1, tok)
    full_gate1, = rs_finish("ffn1_gate", rs_g1, tok)
    last = upd('w1_gate', full_gate1.reshape(1, d, fs), 0, 0, (d, fs))
    full_up1, = rs_finish("ffn1_up", rs_u1, last)
    upd('w1_up', full_up1.reshape(1, d, fs), 0, 0, (d, fs))

    return (loss, grad_x[None], *[out['grad', n] for n in WEIGHTS], *[out['delta', n] for n in WEIGHTS],
            *[out['m', n] for n in WEIGHTS], *[out['v', n] for n in WEIGHTS])
```

```python
import functools
import math

import jax
import jax.numpy as jnp
from jax import lax
from jax.experimental import pallas as pl
from jax.experimental.pallas import tpu as pltpu

F32 = jnp.float32
BF16 = jnp.bfloat16
EPS = 1e-6
ADAM_LR, ADAM_B1, ADAM_B2, ADAM_EPS, ADAM_WD, ADAM_STEP = 0.001, 0.9, 0.999, 1e-08, 0.01, 10
POOL_WINDOWS = (2, 4, 8, 16)
SSM_GROUP = 16
S5_GROUPS_PER_BLOCK = 16
S5_LANES = 8
MEM_HEADS = 4
N_CHIPS = 4
VMEM_LIMIT_V7X = 56 * 1024 * 1024
MESH = pl.DeviceIdType.MESH

WEIGHTS = ['g_ffn1', 'w1_gate', 'w1_up', 'w1_down', 'g_mix', 'w_in', 'ssm_a_re', 'ssm_a_im', 'ssm_log_dt',
           'ssm_b_re', 'ssm_b_im', 'ssm_c_re', 'ssm_c_im', 'ssm_d', 'w_glu', 'b_glu', 'w_pool', 'pool_scale',
           'g_out_ssm', 'g_out_pool', 'w_out', 'g_xattn', 'g_mem', 'w_q', 'w_k', 'w_v', 'w_o', 'g_ffn2',
           'w2_gate', 'w2_up', 'w2_down', 'g_final']
BIG = ['w1_gate', 'w1_up', 'w1_down', 'w_in', 'w_glu', 'w_pool', 'w_out', 'w_q', 'w_k', 'w_v', 'w_o',
       'w2_gate', 'w2_up', 'w2_down']
SMALL = [n for n in WEIGHTS if n not in BIG]


def _tile(n, target, mult=8):
    best = None
    for d in range(1, n + 1):
        if n % d == 0 and d <= target and d % mult == 0:
            best = d
    return best if best is not None else n


def _params(sem=None):
    if sem is None:
        return pltpu.CompilerParams(vmem_limit_bytes=VMEM_LIMIT_V7X)
    return pltpu.CompilerParams(dimension_semantics=sem, vmem_limit_bytes=VMEM_LIMIT_V7X)


def _sigmoid(x):
    return 1.0 / (1.0 + jnp.exp(-x))


def _sigmoid_approx(x):
    return pl.reciprocal(1.0 + jnp.exp(-x), approx=True)


def _rms_fwd(x, g):
    r = lax.rsqrt(jnp.mean(x * x, axis=-1, keepdims=True) + EPS)
    return x * r * g


def _rms_bwd(dy, x, g):
    r = lax.rsqrt(jnp.mean(x * x, axis=-1, keepdims=True) + EPS)
    dxh = dy * g
    dx = r * dxh - x * (r * r * r) * jnp.mean(dxh * x, axis=-1, keepdims=True)
    return dx, dy * x * r


def _colsum(v):
    return jnp.sum(v, axis=0, keepdims=True)


def _rowwise(name, fn, rows, vecs, out_defs, red_defs=(), tm=256, deps=()):
    rows = [r if isinstance(r, tuple) else (r, 0, r.shape[1]) for r in rows]
    t_rows = rows[0][0].shape[0]
    tm = _tile(t_rows, tm)
    nr, nv, no, nd, nx = len(rows), len(vecs), len(out_defs), len(red_defs), len(deps)

    def body(*refs):
        r, v = refs[:nr], refs[nr:nr + nv]
        o, d = refs[nr + nv + nx:nr + nv + nx + no], refs[nr + nv + nx + no:]
        outs, reds = fn([x[...] for x in r], [x[...] for x in v])
        for ref, val in zip(o, outs):
            ref[...] = val.astype(ref.dtype)
        if nd:
            @pl.when(pl.program_id(0) == 0)
            def _():
                for ref in d:
                    ref[...] = jnp.zeros(ref.shape, ref.dtype)
            for ref, val in zip(d, reds):
                ref[...] += val

    in_specs = [pl.BlockSpec((tm, w), functools.partial(lambda i, cb: (i, cb), cb=cb)) for (_, cb, w) in rows]
    in_specs += [pl.BlockSpec(v.shape, lambda i: (0, 0)) for v in vecs]
    in_specs += [pl.BlockSpec(memory_space=pl.ANY)] * nx
    out_specs = [pl.BlockSpec((tm, w), lambda i: (i, 0)) for (w, _) in out_defs]
    out_specs += [pl.BlockSpec((1, w), lambda i: (0, 0)) for w in red_defs]
    out_shape = [jax.ShapeDtypeStruct((t_rows, w), dt) for (w, dt) in out_defs]
    out_shape += [jax.ShapeDtypeStruct((1, w), F32) for w in red_defs]
    res = pl.pallas_call(
        body, name=name, grid=(t_rows // tm,), in_specs=in_specs, out_specs=out_specs, out_shape=out_shape,
        compiler_params=_params(("arbitrary",)),
    )(*[r[0] for r in rows], *vecs, *deps)
    return res


def _rmsnorm(name, x, g, tm=256, deps=()):
    return _rowwise(name, lambda r, v: ([_rms_fwd(r[0].astype(F32), v[0])], []), [x], [g],
                    [(x.shape[1], BF16)], tm=tm, deps=deps)[0]


def _mm_nn(name, a, b, b_block, b_idx, nk, n_out, out_dtype, res=None, alpha=1.0, norm_g=None, tm=512, slabs=1):
    t_rows = a.shape[0]
    bk, tn = slabs * b_block[-2], b_block[-1]
    tm = _tile(t_rows, tm)
    nj = n_out // tn
    has_res = res is not None
    has_norm = norm_g is not None
    assert not has_norm or nj == 1

    def body(*refs):
        a_ref, b_ref = refs[0], refs[1]
        res_ref = refs[2] if has_res else None
        g_ref = refs[2 + has_res] if has_norm else None
        o_ref = refs[2 + has_res + has_norm]
        n_ref = refs[3 + has_res + has_norm] if has_norm else None
        k = pl.program_id(2)
        w = b_ref[...].reshape(bk, tn) if slabs > 1 else b_ref[...]
        p = jnp.dot(a_ref[...], w, preferred_element_type=F32)

        def finish(r):
            if has_res:
                r = res_ref[...] + alpha * r
            o_ref[...] = r.astype(o_ref.dtype)
            if has_norm:
                n_ref[...] = _rms_fwd(r, g_ref[...]).astype(n_ref.dtype)

        if nk == 1:
            finish(p)
            return
        acc_ref = refs[3 + has_res + 2 * has_norm]

        @pl.when(k == 0)
        def _():
            acc_ref[...] = p

        @pl.when(k > 0)
        def _():
            acc_ref[...] += p

        @pl.when(k == nk - 1)
        def _():
            finish(acc_ref[...])

    in_specs = [pl.BlockSpec((tm, bk), lambda j, i, k: (i, k)),
                pl.BlockSpec(b_block, lambda j, i, k: b_idx(j, k))]
    args = [a, b]
    if has_res:
        in_specs.append(pl.BlockSpec((tm, tn), lambda j, i, k: (i, j)))
        args.append(res)
    tile = pl.BlockSpec((tm, tn), lambda j, i, k: (i, j))
    out_specs, out_shape = tile, jax.ShapeDtypeStruct((t_rows, n_out), out_dtype)
    if has_norm:
        in_specs.append(pl.BlockSpec((1, n_out), lambda j, i, k: (0, 0)))
        args.append(norm_g)
        out_specs, out_shape = [tile, tile], [out_shape, jax.ShapeDtypeStruct((t_rows, n_out), BF16)]
    return pl.pallas_call(
        body, name=name, grid=(nj, t_rows // tm, nk), in_specs=in_specs, out_specs=out_specs, out_shape=out_shape,
        scratch_shapes=[pltpu.VMEM((tm, tn), F32)] if nk > 1 else [],
        compiler_params=_params(("arbitrary", "arbitrary", "arbitrary")),
    )(*args)


def _dot_nt(x, w):
    return lax.dot_general(x, w, (((1,), (1,)), ((), ())), preferred_element_type=F32)


def _dot_tn(x, y):
    return lax.dot_general(x, y, (((0,), (0,)), ((), ())), preferred_element_type=F32)


def _mm_nt_cols(name, pairs, ns, bn, out_defs, epi=None, extras=(), tm=512, deps=(), slabs=1, row_parts=1):
    t_rows = pairs[0][0].shape[0]
    tm = _tile(t_rows, tm)
    npair, nex, no, nx = len(pairs), len(extras), len(out_defs), len(deps)
    ns, bn = ns // slabs, bn * slabs
    rp = tm // row_parts

    def body(*refs):
        ws = [refs[2 * p + 1][...] for p in range(npair)]
        if slabs > 1:
            ws = [w.reshape(bn, w.shape[-1]) for w in ws]
        for part_i in range(row_parts):
            rows = slice(part_i * rp, (part_i + 1) * rp)
            acc = None
            for p in range(npair):
                part = _dot_nt(refs[2 * p][rows, :], ws[p])
                acc = part if acc is None else acc + part
            ex = [r[rows, :] for r in refs[2 * npair:2 * npair + nex]]
            outs = epi(acc, *ex) if epi is not None else (acc,)
            for ref, val in zip(refs[2 * npair + nex + nx:], outs):
                ref[rows, :] = val.astype(ref.dtype)

    in_specs, args = [], []
    for (dy, w, w_block, w_idx) in pairs:
        in_specs.append(pl.BlockSpec((tm, dy.shape[1]), lambda s, i: (i, 0)))
        in_specs.append(pl.BlockSpec(w_block, functools.partial(lambda s, i, f: f(s), f=w_idx)))
        args += [dy, w]
    for e in extras:
        in_specs.append(pl.BlockSpec((tm, bn), lambda s, i: (i, s)))
        args.append(e)
    in_specs += [pl.BlockSpec(memory_space=pl.ANY)] * nx
    args += list(deps)
    res = pl.pallas_call(
        body, name=name, grid=(ns, t_rows // tm), in_specs=in_specs,
        out_specs=[pl.BlockSpec((tm, bn), lambda s, i: (i, s)) for _ in range(no)],
        out_shape=[jax.ShapeDtypeStruct((t_rows, ns * bn), dt) for dt in out_defs],
        compiler_params=_params(("arbitrary", "arbitrary")),
    )(*args)
    return res


def _mm_nt_k(name, pairs, ns, n_out, out_dtype, tm=512, deps=()):
    t_rows = pairs[0][0].shape[0]
    tm = _tile(t_rows, tm)
    npair, nx = len(pairs), len(deps)

    def body(*refs):
        o_ref, acc_ref = refs[2 * npair + nx], refs[2 * npair + nx + 1]
        s = pl.program_id(1)
        acc = None
        for p in range(npair):
            part = _dot_nt(refs[2 * p][...], refs[2 * p + 1][...])
            acc = part if acc is None else acc + part

        @pl.when(s == 0)
        def _():
            acc_ref[...] = acc

        @pl.when(s > 0)
        def _():
            acc_ref[...] += acc

        @pl.when(s == ns - 1)
        def _():
            o_ref[...] = acc_ref[...].astype(o_ref.dtype)

    in_specs, args = [], []
    for (a, w, w_block, w_idx) in pairs:
        in_specs.append(pl.BlockSpec((tm, w_block[-1]), lambda i, s: (i, s)))
        in_specs.append(pl.BlockSpec(w_block, functools.partial(lambda i, s, f: f(s), f=w_idx)))
        args += [a, w]
    in_specs += [pl.BlockSpec(memory_space=pl.ANY)] * nx
    args += list(deps)
    return pl.pallas_call(
        body, name=name, grid=(t_rows // tm, ns), in_specs=in_specs,
        out_specs=pl.BlockSpec((tm, n_out), lambda i, s: (i, 0)),
        out_shape=jax.ShapeDtypeStruct((t_rows, n_out), out_dtype),
        scratch_shapes=[pltpu.VMEM((tm, n_out), F32)],
        compiler_params=_params(("arbitrary", "arbitrary")),
    )(*args)


def _mm_tn(name, a, b, bk, bn, n_p, n_q, out_shape, out_block, out_idx, into=None, a_off=0, b_off=0, tt=512,
           deps=()):
    t_rows = a.shape[0]
    tt = _tile(t_rows, tt, 16)
    nt = t_rows // tt
    has_into = into is not None
    nx = len(deps)

    def body(*refs):
        a_ref, b_ref = refs[0], refs[1]
        o_ref, acc_ref = refs[2 + has_into + nx], refs[3 + has_into + nx]
        t = pl.program_id(2)
        part = _dot_tn(a_ref[...], b_ref[...])

        @pl.when(t == 0)
        def _():
            acc_ref[...] = part

        @pl.when(t > 0)
        def _():
            acc_ref[...] += part

        @pl.when(t == nt - 1)
        def _():
            o_ref[...] = acc_ref[...].astype(o_ref.dtype)

    in_specs = [pl.BlockSpec((tt, bk), lambda p, q, t: (t, p + a_off)),
                pl.BlockSpec((tt, bn), lambda p, q, t: (t, q + b_off))]
    args = [a, b]
    aliases = {}
    if has_into:
        in_specs.append(pl.BlockSpec(memory_space=pl.ANY))
        args.append(into)
        aliases = {2: 0}
        out_shape = jax.ShapeDtypeStruct(into.shape, into.dtype)
    in_specs += [pl.BlockSpec(memory_space=pl.ANY)] * nx
    args += list(deps)
    return pl.pallas_call(
        body, name=name, grid=(n_p, n_q, nt), in_specs=in_specs,
        out_specs=pl.BlockSpec(out_block, lambda p, q, t: out_idx(p, q)),
        out_shape=out_shape, scratch_shapes=[pltpu.VMEM((bk, bn), F32)],
        input_output_aliases=aliases,
        compiler_params=_params(("arbitrary", "arbitrary", "arbitrary")),
    )(*args)


def _ffn_up(name, n, ga, slots, cols, d_model, fs, into=None, tm=512):
    t_rows = n.shape[0]
    tm = _tile(t_rows, tm)
    n_sh = slots.shape[0]
    has_into = into is not None

    row_parts = 2 if tm % 32 == 0 else 1

    def body(slot_ref, col_ref, n_ref, wg_ref, wu_ref, *refs):
        a_ref, b_ref, h_ref = refs[3 * has_into:]
        wg, wu = wg_ref[...], wu_ref[...]
        for part in range(row_parts):
            rows = slice(part * (tm // row_parts), (part + 1) * (tm // row_parts))
            x = n_ref[rows, :]
            a = jnp.dot(x, wg, preferred_element_type=F32)
            b = jnp.dot(x, wu, preferred_element_type=F32)
            a_ref[rows, :] = a.astype(a_ref.dtype)
            b_ref[rows, :] = b.astype(b_ref.dtype)
            h_ref[rows, :] = (a * _sigmoid_approx(a) * b).astype(h_ref.dtype)

    w_block = (None, None, d_model, fs)
    out = jax.ShapeDtypeStruct((t_rows, N_CHIPS * fs), BF16)
    in_specs = [pl.BlockSpec((tm, d_model), lambda s, i, sl, co: (i, 0)),
                pl.BlockSpec(w_block, lambda s, i, sl, co: (sl[s], 0, 0, 0)),
                pl.BlockSpec(w_block, lambda s, i, sl, co: (sl[s], 1, 0, 0))]
    args = [slots, cols, n, ga, ga]
    aliases = {}
    if has_into:
        in_specs += [pl.BlockSpec(memory_space=pl.ANY)] * 3
        args += list(into)
        aliases = {5: 0, 6: 1, 7: 2}
    return pl.pallas_call(
        body, name=name,
        grid_spec=pltpu.PrefetchScalarGridSpec(
            num_scalar_prefetch=2, grid=(n_sh, t_rows // tm), in_specs=in_specs,
            out_specs=[pl.BlockSpec((tm, fs), lambda s, i, sl, co: (i, co[s]))] * 3),
        out_shape=[out, out, out], input_output_aliases=aliases,
        compiler_params=_params(("arbitrary", "arbitrary")),
    )(*args)


def _swiglu_bwd(dh, a, b):
    a = a.astype(F32)
    b = b.astype(F32)
    sg = _sigmoid_approx(a)
    return dh * b * sg * (1.0 + a * (1.0 - sg)), dh * a * sg


def _attn_fwd(q, k, v, tm=512):
    t_rows, d_model = q.shape
    n_mem = k.shape[0]
    hd = d_model // MEM_HEADS
    scale = hd ** -0.5
    tm = _tile(t_rows, tm)

    def body(q_ref, k_ref, v_ref, o_ref):
        for h in range(MEM_HEADS):
            cols = slice(h * hd, (h + 1) * hd)
            s = _dot_nt(q_ref[:, cols], k_ref[:, cols]) * scale
            s = s - jnp.max(s, axis=-1, keepdims=True)
            e = jnp.exp(s)
            p = e / jnp.sum(e, axis=-1, keepdims=True)
            o_ref[:, cols] = jnp.dot(p.astype(BF16), v_ref[:, cols], preferred_element_type=F32).astype(o_ref.dtype)

    return pl.pallas_call(
        body, name="attn_fwd", grid=(t_rows // tm,),
        in_specs=[pl.BlockSpec((tm, d_model), lambda i: (i, 0)),
                  pl.BlockSpec((n_mem, d_model), lambda i: (0, 0)),
                  pl.BlockSpec((n_mem, d_model), lambda i: (0, 0))],
        out_specs=pl.BlockSpec((tm, d_model), lambda i: (i, 0)),
        out_shape=jax.ShapeDtypeStruct((t_rows, d_model), BF16),
        compiler_params=_params(("arbitrary",)),
    )(q, k, v)


def _attn_bwd(q, k, v, do, tm=512):
    t_rows, d_model = q.shape
    n_mem = k.shape[0]
    hd = d_model // MEM_HEADS
    scale = hd ** -0.5
    tm = _tile(t_rows, tm, 16)

    def body(q_ref, k_ref, v_ref, do_ref, dq_ref, dk_ref, dv_ref):
        @pl.when(pl.program_id(0) == 0)
        def _():
            dk_ref[...] = jnp.zeros(dk_ref.shape, F32)
            dv_ref[...] = jnp.zeros(dv_ref.shape, F32)

        for h in range(MEM_HEADS):
            cols = slice(h * hd, (h + 1) * hd)
            qh, kh, vh, doh = q_ref[:, cols], k_ref[:, cols], v_ref[:, cols], do_ref[:, cols]
            s = _dot_nt(qh, kh) * scale
            s = s - jnp.max(s, axis=-1, keepdims=True)
            e = jnp.exp(s)
            p = e / jnp.sum(e, axis=-1, keepdims=True)
            dv_ref[:, cols] += _dot_tn(p.astype(BF16), doh)
            dp = _dot_nt(doh, vh)
            ds = (p * (dp - jnp.sum(dp * p, axis=-1, keepdims=True)) * scale).astype(BF16)
            dq_ref[:, cols] = jnp.dot(ds, kh, preferred_element_type=F32).astype(dq_ref.dtype)
            dk_ref[:, cols] += _dot_tn(ds, qh)

    full = pl.BlockSpec((n_mem, d_model), lambda i: (0, 0))
    tile = pl.BlockSpec((tm, d_model), lambda i: (i, 0))
    return pl.pallas_call(
        body, name="attn_bwd", grid=(t_rows // tm,),
        in_specs=[tile, full, full, tile], out_specs=[tile, full, full],
        out_shape=[jax.ShapeDtypeStruct((t_rows, d_model), BF16),
                   jax.ShapeDtypeStruct((n_mem, d_model), F32), jax.ShapeDtypeStruct((n_mem, d_model), F32)],
        compiler_params=_params(("arbitrary",)),
    )(q, k, v, do)


def _split_bf16(v):
    hi = v.astype(BF16)
    return hi, (v - hi.astype(F32)).astype(BF16)


def _pool_window(g):
    return jnp.left_shift(jnp.int32(POOL_WINDOWS[0]), g)


def _pool_fwd(u, col_off, w_pool, scale, tt=256):
    t_rows = u.shape[0]
    pw = w_pool.shape[-1]
    ng = w_pool.shape[0]
    tt = _tile(t_rows, tt, 16)
    nt = t_rows // tt
    assert POOL_WINDOWS == tuple(2 << i for i in range(ng)) and tt >= POOL_WINDOWS[-1]

    def body(vc_ref, vp_ref, w_ref, sc_ref, pooled_ref, z_ref):
        g, i = pl.program_id(0), pl.program_id(1)
        w = _pool_window(g)
        r = lax.broadcasted_iota(jnp.int32, (tt, tt), 0)
        c = lax.broadcasted_iota(jnp.int32, (tt, tt), 1)
        band_c = ((c <= r) & (c > r - w)).astype(BF16)
        band_p = (c > r - w + tt).astype(BF16)
        vc = vc_ref[...]
        ch, cl = _split_bf16(vc)
        ph, plo = _split_bf16(vp_ref[...] * (i > 0).astype(F32))
        sums = (jnp.dot(band_c, ch, preferred_element_type=F32) + jnp.dot(band_c, cl, preferred_element_type=F32)
                + jnp.dot(band_p, ph, preferred_element_type=F32) + jnp.dot(band_p, plo, preferred_element_type=F32))
        t = i * tt + lax.broadcasted_iota(jnp.int32, (tt, 1), 0)
        cnt = jnp.minimum(t + 1, w).astype(F32)
        pooled = (sums / cnt - vc).astype(BF16)
        pooled_ref[...] = pooled
        z_ref[...] = jnp.dot(pooled, w_ref[...], preferred_element_type=F32) * sc_ref[...]

    return pl.pallas_call(
        body, name="pool_fwd", grid=(ng, nt),
        in_specs=[pl.BlockSpec((tt, pw), lambda g, i: (i, col_off + g)),
                  pl.BlockSpec((tt, pw), lambda g, i: (jnp.maximum(i - 1, 0), col_off + g)),
                  pl.BlockSpec((None, pw, pw), lambda g, i: (g, 0, 0)),
                  pl.BlockSpec((1, pw), lambda g, i: (0, g))],
        out_specs=[pl.BlockSpec((tt, pw), lambda g, i: (i, g))] * 2,
        out_shape=[jax.ShapeDtypeStruct((t_rows, ng * pw), BF16), jax.ShapeDtypeStruct((t_rows, ng * pw), F32)],
        compiler_params=_params(("arbitrary", "arbitrary")),
    )(u, u, w_pool, scale)


def _pool_bwd1(dz, pooled, w_pool, scale, tt=256):
    t_rows = dz.shape[0]
    pw = w_pool.shape[-1]
    ng = w_pool.shape[0]
    tt = _tile(t_rows, tt, 16)
    nt = t_rows // tt

    def body(dz_ref, p_ref, w_ref, sc_ref, dp_ref, dw_ref, dsc_ref):
        g, i = pl.program_id(0), pl.program_id(1)
        w = _pool_window(g)

        @pl.when(i == 0)
        def _():
            dw_ref[...] = jnp.zeros(dw_ref.shape, F32)
            dsc_ref[...] = jnp.zeros(dsc_ref.shape, F32)

        dz_v = dz_ref[...]
        pooled = p_ref[...]
        zpre = jnp.dot(pooled, w_ref[...], preferred_element_type=F32)
        dsc_ref[...] += _colsum(dz_v * zpre)
        dzs = (dz_v * sc_ref[...]).astype(BF16)
        dw_ref[...] += _dot_tn(pooled, dzs)
        t = i * tt + lax.broadcasted_iota(jnp.int32, (tt, 1), 0)
        cnt = jnp.minimum(t + 1, w).astype(F32)
        dp_ref[...] = _dot_nt(dzs, w_ref[...]) / cnt

    return pl.pallas_call(
        body, name="pool_bwd1", grid=(ng, nt),
        in_specs=[pl.BlockSpec((tt, pw), lambda g, i: (i, g)),
                  pl.BlockSpec((tt, pw), lambda g, i: (i, g)),
                  pl.BlockSpec((None, pw, pw), lambda g, i: (g, 0, 0)),
                  pl.BlockSpec((1, pw), lambda g, i: (0, g))],
        out_specs=[pl.BlockSpec((tt, pw), lambda g, i: (i, g)),
                   pl.BlockSpec((None, pw, pw), lambda g, i: (g, 0, 0)),
                   pl.BlockSpec((1, pw), lambda g, i: (0, g))],
        out_shape=[jax.ShapeDtypeStruct((t_rows, ng * pw), F32), jax.ShapeDtypeStruct((ng, pw, pw), F32),
                   jax.ShapeDtypeStruct((1, ng * pw), F32)],
        compiler_params=_params(("arbitrary", "arbitrary")),
    )(dz, pooled, w_pool, scale)


def _pool_bwd2(dps, ng, tt=256):
    t_rows, width = dps.shape
    pw = width // ng
    tt = _tile(t_rows, tt, 16)
    nt = t_rows // tt

    def body(dc_ref, dn_ref, dv_ref):
        g, i = pl.program_id(0), pl.program_id(1)
        w = _pool_window(g)
        r = lax.broadcasted_iota(jnp.int32, (tt, tt), 0)
        c = lax.broadcasted_iota(jnp.int32, (tt, tt), 1)
        band_c = ((c >= r) & (c < r + w)).astype(BF16)
        band_n = (c < r + w - tt).astype(BF16)
        dc = dc_ref[...]
        ch, cl = _split_bf16(dc)
        nh, nl = _split_bf16(dn_ref[...] * (i < nt - 1).astype(F32))
        sums = (jnp.dot(band_c, ch, preferred_element_type=F32) + jnp.dot(band_c, cl, preferred_element_type=F32)
                + jnp.dot(band_n, nh, preferred_element_type=F32) + jnp.dot(band_n, nl, preferred_element_type=F32))
        t = i * tt + lax.broadcasted_iota(jnp.int32, (tt, 1), 0)
        cnt = jnp.minimum(t + 1, w).astype(F32)
        dv_ref[...] = sums - dc * cnt

    return pl.pallas_call(
        body, name="pool_bwd2", grid=(ng, nt),
        in_specs=[pl.BlockSpec((tt, pw), lambda g, i: (i, g)),
                  pl.BlockSpec((tt, pw), lambda g, i: (jnp.minimum(i + 1, nt - 1), g))],
        out_specs=pl.BlockSpec((tt, pw), lambda g, i: (i, g)),
        out_shape=jax.ShapeDtypeStruct((t_rows, width), F32),
        compiler_params=_params(("arbitrary", "arbitrary")),
    )(dps, dps)


def _cpow(ar, ai, n):
    rr, ri, br, bi = None, None, ar, ai
    while n:
        if n & 1:
            rr, ri = (br, bi) if rr is None else (rr * br - ri * bi, rr * bi + ri * br)
        n >>= 1
        if n:
            br, bi = br * br - bi * bi, 2.0 * br * bi
    return rr, ri


def _chunk_carries(st_re, st_im, pr, pi, order):
    cb = st_re.shape[1]
    sub = lax.broadcasted_iota(jnp.int32, (S5_LANES, cb), 0)
    cr = jnp.zeros((S5_LANES, cb), F32)
    ci = jnp.zeros((S5_LANES, cb), F32)
    prev_r = jnp.zeros((1, cb), F32)
    prev_i = jnp.zeros((1, cb), F32)
    for k, src in order:
        er, ei = st_re[src:src + 1, :], st_im[src:src + 1, :]
        nr = er + pr * prev_r - pi * prev_i
        ni = ei + pr * prev_i + pi * prev_r
        cr = jnp.where(sub == k, jnp.broadcast_to(nr, (S5_LANES, cb)), cr)
        ci = jnp.where(sub == k, jnp.broadcast_to(ni, (S5_LANES, cb)), ci)
        prev_r, prev_i = nr, ni
    return cr, ci


def _s5_fwd(up, bblk, ab, cblk, tt=128):
    n_rows, ws = up.shape
    nb, cw, cb2 = bblk.shape
    cb = cb2 // 2
    lc = n_rows // S5_LANES
    tt = _tile(lc, tt, 1)
    nt = lc // tt
    rt = S5_LANES * tt

    def body(u_ref, b_ref, ab_ref, c_ref, y_ref, s_ref, bu_ref, st_re, st_im):
        ps, ti = pl.program_id(1), pl.program_id(2)
        ar = jnp.broadcast_to(ab_ref[0:1, :], (S5_LANES, cb))
        ai = jnp.broadcast_to(ab_ref[1:2, :], (S5_LANES, cb))

        @pl.when((ps == 0) & (ti == 0))
        def _():
            st_re[...] = jnp.zeros(st_re.shape, F32)
            st_im[...] = jnp.zeros(st_im.shape, F32)

        @pl.when((ps == 1) & (ti == 0))
        def _():
            pr, pi = _cpow(ab_ref[0:1, :], ab_ref[1:2, :], lc)
            cr, ci = _chunk_carries(st_re, st_im, pr, pi, [(k, k - 1) for k in range(1, S5_LANES)])
            st_re[...] = cr
            st_im[...] = ci

        bu_ref[...] = jnp.dot(u_ref[...], b_ref[...], preferred_element_type=F32)

        def step(t, carry, store):
            sr, si = carry
            rows = pl.ds(pl.multiple_of(t * S5_LANES, S5_LANES), S5_LANES)
            nr = ar * sr - ai * si + bu_ref[rows, 0:cb]
            ni = ar * si + ai * sr + bu_ref[rows, cb:cb2]
            if store:
                s_ref[rows, 0:cb] = nr
                s_ref[rows, cb:cb2] = ni
            return nr, ni

        @pl.when(ps == 0)
        def _():
            sr, si = lax.fori_loop(0, tt, functools.partial(step, store=False), (st_re[...], st_im[...]))
            st_re[...] = sr
            st_im[...] = si

        @pl.when(ps == 1)
        def _():
            sr, si = lax.fori_loop(0, tt, functools.partial(step, store=True), (st_re[...], st_im[...]))
            st_re[...] = sr
            st_im[...] = si
            y_ref[...] = jnp.dot(s_ref[...].astype(BF16), c_ref[...], preferred_element_type=F32)

    return pl.pallas_call(
        body, name="s5_fwd", grid=(nb, 2, nt),
        in_specs=[pl.BlockSpec((rt, cw), lambda j, ps, ti: (ti, j)),
                  pl.BlockSpec((None, cw, cb2), lambda j, ps, ti: (j, 0, 0)),
                  pl.BlockSpec((None, 2, cb), lambda j, ps, ti: (j, 0, 0)),
                  pl.BlockSpec((None, cb2, cw), lambda j, ps, ti: (j, 0, 0))],
        out_specs=[pl.BlockSpec((rt, cw), lambda j, ps, ti: (ti * ps, j)),
                   pl.BlockSpec((None, rt, cb2), lambda j, ps, ti: (j, ti * ps, 0))],
        out_shape=[jax.ShapeDtypeStruct((n_rows, ws), F32), jax.ShapeDtypeStruct((nb, n_rows, cb2), F32)],
        scratch_shapes=[pltpu.VMEM((rt, cb2), F32), pltpu.VMEM((S5_LANES, cb), F32), pltpu.VMEM((S5_LANES, cb), F32)],
        compiler_params=_params(("arbitrary", "arbitrary", "arbitrary")),
    )(up, bblk, ab, cblk)


def _s5_bwd(dyp, up, s_all, bblk_t, ab, cblk_t, tt=128):
    n_rows, ws = up.shape
    nb, cb2, cw = bblk_t.shape
    cb = cb2 // 2
    lc = n_rows // S5_LANES
    tt = _tile(lc, tt, 1)
    nt = lc // tt
    rt = S5_LANES * tt

    def body(dy_ref, u_ref, s_ref, bt_ref, ab_ref, ct_ref, du_ref, db_ref, dc_ref, da_ref, ds_ref, st_re, st_im):
        ps, ti = pl.program_id(1), pl.program_id(2)
        ar = jnp.broadcast_to(ab_ref[0:1, :], (S5_LANES, cb))
        ai = jnp.broadcast_to(ab_ref[1:2, :], (S5_LANES, cb))

        @pl.when((ps == 0) & (ti == 0))
        def _():
            st_re[...] = jnp.zeros(st_re.shape, F32)
            st_im[...] = jnp.zeros(st_im.shape, F32)
            db_ref[...] = jnp.zeros(db_ref.shape, F32)
            dc_ref[...] = jnp.zeros(dc_ref.shape, F32)
            da_ref[...] = jnp.zeros(da_ref.shape, F32)

        @pl.when((ps == 1) & (ti == 0))
        def _():
            pr, pi = _cpow(ab_ref[0:1, :], -ab_ref[1:2, :], lc)
            cr, ci = _chunk_carries(st_re, st_im, pr, pi, [(k, k + 1) for k in range(S5_LANES - 2, -1, -1)])
            st_re[...] = cr
            st_im[...] = ci

        ds_ref[...] = jnp.dot(dy_ref[...], ct_ref[...], preferred_element_type=F32)

        def rows_of(i):
            return pl.ds(pl.multiple_of((tt - 1 - i) * S5_LANES, S5_LANES), S5_LANES)

        def step0(i, carry):
            gr, gi = carry
            rows = rows_of(i)
            return (ar * gr + ai * gi + ds_ref[rows, 0:cb], ar * gi - ai * gr + ds_ref[rows, cb:cb2])

        def step1(i, carry):
            gr, gi, acr, aci = carry
            rows = rows_of(i)
            sr, si = s_ref[rows, 0:cb], s_ref[rows, cb:cb2]
            acr = acr + sr * gr + si * gi
            aci = aci + sr * gi - si * gr
            nr = ar * gr + ai * gi + ds_ref[rows, 0:cb]
            ni = ar * gi - ai * gr + ds_ref[rows, cb:cb2]
            ds_ref[rows, 0:cb] = nr
            ds_ref[rows, cb:cb2] = ni
            return nr, ni, acr, aci

        @pl.when(ps == 0)
        def _():
            gr, gi = lax.fori_loop(0, tt, step0, (st_re[...], st_im[...]))
            st_re[...] = gr
            st_im[...] = gi

        @pl.when(ps == 1)
        def _():
            zero = jnp.zeros((S5_LANES, cb), F32)
            gr, gi, acr, aci = lax.fori_loop(0, tt, step1, (st_re[...], st_im[...], zero, zero))
            st_re[...] = gr
            st_im[...] = gi
            da_ref[0] += acr
            da_ref[1] += aci
            dsb = ds_ref[...].astype(BF16)
            du_ref[...] = jnp.dot(dsb, bt_ref[...], preferred_element_type=F32)
            db_ref[...] += _dot_tn(u_ref[...], dsb)
            dc_ref[...] += _dot_tn(dy_ref[...], s_ref[...].astype(BF16))

    def tile_idx(ps, ti):
        return (nt - 1 - ti) * ps + (nt - 1) * (1 - ps)

    return pl.pallas_call(
        body, name="s5_bwd", grid=(nb, 2, nt),
        in_specs=[pl.BlockSpec((rt, cw), lambda j, ps, ti: (nt - 1 - ti, j)),
                  pl.BlockSpec((rt, cw), lambda j, ps, ti: (tile_idx(ps, ti), j)),
                  pl.BlockSpec((None, rt, cb2), lambda j, ps, ti: (j, tile_idx(ps, ti), 0)),
                  pl.BlockSpec((None, cb2, cw), lambda j, ps, ti: (j, 0, 0)),
                  pl.BlockSpec((None, 2, cb), lambda j, ps, ti: (j, 0, 0)),
                  pl.BlockSpec((None, cw, cb2), lambda j, ps, ti: (j, 0, 0))],
        out_specs=[pl.BlockSpec((rt, cw), lambda j, ps, ti: (tile_idx(ps, ti), j)),
                   pl.BlockSpec((None, cw, cb2), lambda j, ps, ti: (j, 0, 0)),
                   pl.BlockSpec((None, cw, cb2), lambda j, ps, ti: (j, 0, 0)),
                   pl.BlockSpec((None, 2, S5_LANES, cb), lambda j, ps, ti: (j, 0, 0, 0))],
        out_shape=[jax.ShapeDtypeStruct((n_rows, ws), F32), jax.ShapeDtypeStruct((nb, cw, cb2), F32),
                   jax.ShapeDtypeStruct((nb, cw, cb2), F32), jax.ShapeDtypeStruct((nb, 2, S5_LANES, cb), F32)],
        scratch_shapes=[pltpu.VMEM((rt, cb2), F32), pltpu.VMEM((S5_LANES, cb), F32), pltpu.VMEM((S5_LANES, cb), F32)],
        compiler_params=_params(("arbitrary", "arbitrary", "arbitrary")),
    )(dyp, up, s_all, bblk_t, ab, cblk_t)


def _s5_discretize(a_re, a_im, log_dt, b_re, b_im):
    dt = jnp.exp(log_dt)[:, None]
    mag = jnp.exp(a_re * dt)
    abar_re = mag * jnp.cos(a_im * dt)
    abar_im = mag * jnp.sin(a_im * dt)
    nr, ni = abar_re - 1.0, abar_im
    den = a_re * a_re + a_im * a_im
    fr = (nr * a_re + ni * a_im) / den
    fi = (ni * a_re - nr * a_im) / den
    bbar_re = fr[..., None] * b_re - fi[..., None] * b_im
    bbar_im = fr[..., None] * b_im + fi[..., None] * b_re
    return abar_re, abar_im, bbar_re, bbar_im


def _perm_rows(a):
    n, c = a.shape
    return a.reshape(S5_LANES, n // S5_LANES, c).transpose(1, 0, 2).reshape(n, c)


def _unperm_rows(a):
    n, c = a.shape
    return a.reshape(n // S5_LANES, S5_LANES, c).transpose(1, 0, 2).reshape(n, c)


HBM = pl.BlockSpec(memory_space=pltpu.HBM)
SEM = pl.BlockSpec(memory_space=pltpu.SEMAPHORE)
ANY = pl.BlockSpec(memory_space=pl.ANY)
EFFECT = pltpu.SideEffectType.DATAFLOW_SIDE_EFFECTING
COPIES_PER_BUFFER = {"ag_ici": 3, "ag_fwd": 3, "pair": N_CHIPS, "scatter": 3, "half": 1, "swap": 1, "bcast": 3}
PAIRED_KINDS = ("pair", "scatter", "swap")


def _place():
    x, y, c = lax.axis_index("x"), lax.axis_index("y"), lax.axis_index("c")
    chips = [(1 - x, y), (x, 1 - y), (1 - x, 1 - y)]
    return x, y, c, 2 * x + y, chips


def _n_copies(kind, n_bufs):
    if isinstance(kind, tuple):
        return len(kind[1])
    return COPIES_PER_BUFFER[kind] * (n_bufs // 2 if kind in PAIRED_KINDS else n_bufs)


def _comm_copies(kind, bufs):
    if isinstance(kind, tuple):
        full = _comm_copies(kind[0], bufs)
        return [full[k] for k in kind[1]]
    x, y, c, s, chips = _place()
    sib = (x, y, 1 - c)
    out = []
    if kind == "ag_ici":
        for w in bufs:
            for cx, cy in chips:
                out.append((w.at[s, c], w.at[s, c], w.at[2 * cx + cy, c], (cx, cy, c)))
    elif kind == "ag_fwd":
        for w in bufs:
            for cx, cy in chips:
                sj = 2 * cx + cy
                out.append((w.at[sj, c], w.at[sj, c], w.at[sj, 1 - c], sib))
    elif kind == "pair":
        n = len(bufs) // 2
        for g, got in zip(bufs[:n], bufs[n:]):
            for t in range(N_CHIPS):
                out.append((g.at[t, 1 - c], got.at[t], got.at[t], sib))
    elif kind == "scatter":
        n = len(bufs) // 2
        for p, got in zip(bufs[:n], bufs[n:]):
            for cx, cy in chips:
                out.append((p.at[2 * cx + cy], got.at[s], got.at[2 * cx + cy], (cx, cy, c)))
    elif kind == "half":
        for f in bufs:
            out.append((f.at[c], f.at[c], f.at[1 - c], sib))
    elif kind == "swap":
        n = len(bufs) // 2
        for v, got in zip(bufs[:n], bufs[n:]):
            out.append((v, got, got, sib))
    elif kind == "bcast":
        for w in bufs:
            for cx, cy in chips:
                out.append((w.at[s], w.at[s], w.at[2 * cx + cy], (cx, cy, c)))
    return out


def _comm_fused(name, kind, bufs):
    n = len(bufs)
    ncp = _n_copies(kind, n)

    def body(*refs):
        outs = refs[n:2 * n]
        send, recv = refs[2 * n:]
        copies = _comm_copies(kind, outs)
        started = []
        for k, (src, dst, _, peer) in enumerate(copies):
            cp = pltpu.make_async_remote_copy(src_ref=src, dst_ref=dst, send_sem=send.at[k], recv_sem=recv.at[k],
                                              device_id=peer, device_id_type=MESH)
            cp.start()
            started.append(cp)
        for k, (_, _, land, peer) in enumerate(copies):
            pltpu.make_async_remote_copy(src_ref=land, dst_ref=land, send_sem=send.at[k], recv_sem=recv.at[k],
                                         device_id=peer, device_id_type=MESH).wait_recv()
        for cp in started:
            cp.wait_send()

    return pl.pallas_call(
        body, name=name, in_specs=[ANY] * n, out_specs=[ANY] * n,
        out_shape=[jax.ShapeDtypeStruct(b.shape, b.dtype) for b in bufs],
        input_output_aliases={k: k for k in range(n)},
        scratch_shapes=[pltpu.SemaphoreType.DMA((ncp,))] * 2,
    )(*bufs)


def _comm_start(name, kind, bufs, after=None):
    n = len(bufs)
    ncp = _n_copies(kind, n)
    nx = 0 if after is None else 1

    def body(*refs):
        refs = refs[n + nx:]
        send, recv = refs[:ncp], refs[ncp:2 * ncp]
        outs = refs[2 * ncp:n + 2 * ncp]
        token = refs[n + 2 * ncp]
        for k, (src, dst, _, peer) in enumerate(_comm_copies(kind, outs)):
            pltpu.make_async_remote_copy(src_ref=src, dst_ref=dst, send_sem=send[k], recv_sem=recv[k],
                                         device_id=peer, device_id_type=MESH).start()
        token[...] = jnp.zeros(token.shape, token.dtype)

    res = pl.pallas_call(
        body, name=name, in_specs=[HBM] * n + [ANY] * nx,
        out_specs=[SEM] * (2 * ncp) + [HBM] * n + [pl.BlockSpec(memory_space=pltpu.VMEM)],
        out_shape=[pltpu.SemaphoreType.DMA(())] * (2 * ncp) + [pltpu.HBM(b.shape, b.dtype) for b in bufs]
        + [jax.ShapeDtypeStruct((8, 128), F32)],
        input_output_aliases={k: 2 * ncp + k for k in range(n)},
        compiler_params=pltpu.CompilerParams(has_side_effects=EFFECT),
    )(*[pltpu.with_memory_space_constraint(b, pltpu.HBM) for b in bufs], *([after] if nx else []))
    return list(res[:ncp]), list(res[ncp:2 * ncp]), list(res[2 * ncp:2 * ncp + n]), res[2 * ncp + n]


def _comm_wait(name, kind, bufs, send_sems, recv_sems, after):
    n = len(bufs)
    ncp = _n_copies(kind, n)

    def body(*refs):
        send, recv = refs[n:n + ncp], refs[n + ncp:n + 2 * ncp]
        outs = refs[n + 2 * ncp + 1:]
        for k, (src, _, land, peer) in enumerate(_comm_copies(kind, outs)):
            cp = pltpu.make_async_remote_copy(src_ref=src, dst_ref=land, send_sem=send[k], recv_sem=recv[k],
                                              device_id=peer, device_id_type=MESH)
            cp.wait_send()
            cp.wait_recv()

    return pl.pallas_call(
        body, name=name, in_specs=[HBM] * n + [SEM] * (2 * ncp) + [ANY], out_specs=[HBM] * n,
        out_shape=[pltpu.HBM(b.shape, b.dtype) for b in bufs],
        input_output_aliases={k: k for k in range(n)},
        compiler_params=pltpu.CompilerParams(has_side_effects=EFFECT),
    )(*bufs, *send_sems, *recv_sems, after)


def _pair_add(g, got, core):
    nchip, _, r, cw = g.shape
    tr = _tile(r, 512, 16)

    def body(c_ref, a_ref, b_ref, o_ref):
        o_ref[...] = a_ref[...] + b_ref[...]

    return pl.pallas_call(
        body, name="grads_pair_add",
        grid_spec=pltpu.PrefetchScalarGridSpec(
            num_scalar_prefetch=1, grid=(nchip, r // tr),
            in_specs=[pl.BlockSpec((None, None, tr, cw), lambda s, i, c_ref: (s, c_ref[0], i, 0)),
                      pl.BlockSpec((None, tr, cw), lambda s, i, c_ref: (s, i, 0))],
            out_specs=pl.BlockSpec((None, tr, cw), lambda s, i, c_ref: (s, i, 0))),
        out_shape=jax.ShapeDtypeStruct((nchip, r, cw), BF16),
        compiler_params=_params(("arbitrary", "arbitrary")),
    )(core, g, got)


def _chip_sum(parts, got, idx):
    _, r, cw = parts.shape
    tr = _tile(r, 512, 16)

    def body(i_ref, own_ref, a_ref, b_ref, c_ref, o_ref):
        o_ref[...] = ((own_ref[...].astype(F32) + a_ref[...].astype(F32)) + b_ref[...].astype(F32)) + c_ref[...].astype(F32)

    def slot(k):
        return pl.BlockSpec((None, tr, cw), lambda i, i_ref: (i_ref[k], i, 0))

    return pl.pallas_call(
        body, name="grads_chip_sum",
        grid_spec=pltpu.PrefetchScalarGridSpec(
            num_scalar_prefetch=1, grid=(r // tr,), in_specs=[slot(0), slot(1), slot(2), slot(3)], out_specs=slot(4)),
        out_shape=jax.ShapeDtypeStruct((2, r, cw), F32),
        compiler_params=_params(("arbitrary",)),
    )(idx, parts, got, got, got)


def _add_into_slot(v, got, chip):
    r, cw = v.shape
    tr = _tile(r, 256)

    def body(c_ref, a_ref, b_ref, o_ref):
        o_ref[...] = a_ref[...] + b_ref[...]

    tile = pl.BlockSpec((tr, cw), lambda i, c_ref: (i, 0))
    return pl.pallas_call(
        body, name="small_pair_add",
        grid_spec=pltpu.PrefetchScalarGridSpec(
            num_scalar_prefetch=1, grid=(r // tr,), in_specs=[tile, tile],
            out_specs=pl.BlockSpec((None, tr, cw), lambda i, c_ref: (c_ref[0], i, 0))),
        out_shape=jax.ShapeDtypeStruct((N_CHIPS, r, cw), F32),
        compiler_params=_params(("arbitrary",)),
    )(chip, v, got)


def _sum_slots(w):
    _, r, cw = w.shape
    tr = _tile(r, 256)

    def body(w_ref, o_ref):
        o_ref[...] = ((w_ref[0] + w_ref[1]) + w_ref[2]) + w_ref[3]

    return pl.pallas_call(
        body, name="small_chip_sum", grid=(r // tr,),
        in_specs=[pl.BlockSpec((N_CHIPS, tr, cw), lambda i: (0, i, 0))],
        out_specs=pl.BlockSpec((tr, cw), lambda i: (i, 0)),
        out_shape=jax.ShapeDtypeStruct((r, cw), F32),
        compiler_params=_params(("arbitrary",)),
    )(w)


def _adamw_math(w, g, m, v):
    m = ADAM_B1 * m + (1.0 - ADAM_B1) * g
    v = ADAM_B2 * v + (1.0 - ADAM_B2) * (g * g)
    m_hat = m / (1.0 - ADAM_B1 ** ADAM_STEP)
    v_hat = v / (1.0 - ADAM_B2 ** ADAM_STEP)
    delta = -ADAM_LR * (m_hat / (jnp.sqrt(v_hat) + ADAM_EPS) + ADAM_WD * w)
    return delta, m, v


def _adamw(name, w, m, v, g, g_half=0, g_row_off=0, tr=256):
    r, cw = w.shape
    tr = _tile(math.gcd(r, g_row_off) if g_row_off else r, tr)
    off = g_row_off // tr

    def body(w_ref, m_ref, v_ref, g_ref, go_ref, d_ref, mo_ref, vo_ref):
        g_v = g_ref[...]
        delta, m_n, v_n = _adamw_math(w_ref[...], g_v, m_ref[...], v_ref[...])
        go_ref[...] = g_v
        d_ref[...] = delta
        mo_ref[...] = m_n
        vo_ref[...] = v_n

    tile = pl.BlockSpec((tr, cw), lambda i: (i, 0))
    out = jax.ShapeDtypeStruct((r, cw), F32)
    return pl.pallas_call(
        body, name=name, grid=(r // tr,),
        in_specs=[tile, tile, tile, pl.BlockSpec((None, tr, cw), lambda i: (g_half, i + off, 0))],
        out_specs=[tile] * 4, out_shape=[out] * 4,
        compiler_params=_params(("arbitrary",)),
    )(w, m, v, g)


def kernel(x, mem, g_ffn1, w1_gate, w1_up, w1_down, g_mix, w_in, ssm_a_re, ssm_a_im, ssm_log_dt, ssm_b_re, ssm_b_im, ssm_c_re, ssm_c_im, ssm_d, w_glu, b_glu, w_pool, pool_scale, g_out_ssm, g_out_pool, w_out, g_xattn, g_mem, w_q, w_k, w_v, w_o, g_ffn2, w2_gate, w2_up, w2_down, g_final, loss_target, m_g_ffn1, m_w1_gate, m_w1_up, m_w1_down, m_g_mix, m_w_in, m_ssm_a_re, m_ssm_a_im, m_ssm_log_dt, m_ssm_b_re, m_ssm_b_im, m_ssm_c_re, m_ssm_c_im, m_ssm_d, m_w_glu, m_b_glu, m_w_pool, m_pool_scale, m_g_out_ssm, m_g_out_pool, m_w_out, m_g_xattn, m_g_mem, m_w_q, m_w_k, m_w_v, m_w_o, m_g_ffn2, m_w2_gate, m_w2_up, m_w2_down, m_g_final, v_g_ffn1, v_w1_gate, v_w1_up, v_w1_down, v_g_mix, v_w_in, v_ssm_a_re, v_ssm_a_im, v_ssm_log_dt, v_ssm_b_re, v_ssm_b_im, v_ssm_c_re, v_ssm_c_im, v_ssm_d, v_w_glu, v_b_glu, v_w_pool, v_pool_scale, v_g_out_ssm, v_g_out_pool, v_w_out, v_g_xattn, v_g_mem, v_w_q, v_w_k, v_w_v, v_w_o, v_g_ffn2, v_w2_gate, v_w2_up, v_w2_down, v_g_final):
    local = dict(locals())
    wts = {n: local[n] for n in WEIGHTS}
    mom = {n: local["m_" + n] for n in WEIGHTS}
    var = {n: local["v_" + n] for n in WEIGHTS}

    x2 = x[0]
    mem2 = mem[0]
    tgt = loss_target[0]
    t_rows, d = x2.shape
    fs = w1_gate.shape[-1]
    ds_ = w_in.shape[1]
    ws = d // 2
    n_pg = len(POOL_WINDOWS)
    pw = ws // n_pg
    n_grp = ws // SSM_GROUP
    n_state = ssm_a_re.shape[-1]
    cx_, cy_, cc_ = lax.axis_index("x"), lax.axis_index("y"), lax.axis_index("c")
    chip = (2 * cx_ + cy_).astype(jnp.int32)
    core = cc_.astype(jnp.int32).reshape(1)
    chip_idx = jnp.stack([chip, chip ^ 2, chip ^ 1, chip ^ 3, cc_.astype(jnp.int32)])

    glu_rows = w_glu[0].reshape(-1, d)
    pool_rows = w_pool[0].reshape(-1, d)
    n_glu, n_pool = glu_rows.shape[0], pool_rows.shape[0]
    PACKED = ['w_out', 'w_q', 'w_k', 'w_v', 'w_o']
    glu_at = len(PACKED) * ds_
    pool_at = glu_at + n_glu
    n_pad = -(pool_at + n_pool) % 32
    rp = pool_at + n_pool + n_pad

    def own_slot(src):
        src = src.astype(BF16)
        return lax.dynamic_update_slice(lax.empty((N_CHIPS,) + src.shape, BF16), src[None], (chip, 0, 0, 0))

    src_packed = jnp.concatenate([wts[n][0] for n in PACKED] + [glu_rows, pool_rows, jnp.zeros((n_pad, d), F32)], 0)
    src_up1 = jnp.stack([w1_gate[0], w1_up[0]]).astype(BF16)
    w_bufs = [own_slot(src_up1), own_slot(w1_down[0].reshape(2, fs // 2, d)),
              own_slot(w_in[0].reshape(2, ds_ // 2, d)), own_slot(src_packed.reshape(2, rp // 2, d)),
              own_slot(jnp.stack([w2_gate[0], w2_up[0]])), own_slot(w2_down[0].reshape(2, fs // 2, d))]
    near_send, near_recv, w0, ag_token = _comm_start("weights_start_near", ("ag_ici", (0, 1)), w_bufs[:1])

    def gathered(k, after):
        w = _comm_wait("weights_wait_%d" % k, "ag_ici", [w_bufs[k]], ag_send[3 * k:3 * k + 3],
                       ag_recv[3 * k:3 * k + 3], after)
        return _comm_fused("weights_forward_%d" % k, "ag_fwd", w)[0]

    def gathered_start(k, after):
        w = _comm_wait("weights_wait_%d" % k, "ag_ici", [w_bufs[k]], ag_send[3 * k:3 * k + 3],
                       ag_recv[3 * k:3 * k + 3], after)
        send, recv, thru, token = _comm_start("weights_forward_start_%d" % k, "ag_fwd", w)
        return (send, recv, thru), token

    def gathered_finish(k, handle, after):
        send, recv, thru = handle
        return _comm_wait("weights_forward_wait_%d" % k, "ag_fwd", thru, send, recv, after)[0]

    n1 = _rmsnorm("norm_ffn1", x2, wts['g_ffn1'].reshape(1, -1), deps=[ag_token])
    zero1 = jnp.zeros((1,), jnp.int32)
    near, far = jnp.stack([chip ^ 2, chip ^ 1]), (chip ^ 3).reshape(1)
    a1, b1, hm1 = _ffn_up("ffn1_up_own", n1, src_up1[None], zero1, chip.reshape(1), d, fs)
    n_rest = 3 * len(w_bufs) - 2
    rest_send, rest_recv, w_bufs, _ = _comm_start("weights_start_rest", ("ag_ici", tuple(range(2, 2 + n_rest))),
                                                  w0 + w_bufs[1:], after=hm1)
    ag_send, ag_recv = near_send + rest_send, near_recv + rest_recv
    w0 = _comm_wait("weights_wait_0_near", ("ag_ici", (0, 1)), [w_bufs[0]], ag_send[0:2], ag_recv[0:2], hm1)
    w0 = _comm_fused("weights_forward_0_near", ("ag_fwd", (0, 1)), w0)
    a1, b1, hm1 = _ffn_up("ffn1_up_near", n1, w0[0], near, near, d, fs, into=(a1, b1, hm1))
    w0 = _comm_wait("weights_wait_0_far", ("ag_ici", (2,)), w0, ag_send[2:3], ag_recv[2:3], hm1)
    ga1 = _comm_fused("weights_forward_0_far", ("ag_fwd", (2,)), w0)[0]
    a1, b1, hm1 = _ffn_up("ffn1_up_far", n1, ga1, far, far, d, fs, into=(a1, b1, hm1))
    gd1 = gathered(1, hm1).reshape(N_CHIPS, fs, d)
    h1 = _mm_nn("ffn1_down", hm1, gd1, (N_CHIPS, fs, d // 2), lambda j, k: (0, 0, j), 1, d, F32, res=x2, alpha=0.5,
                slabs=N_CHIPS)
    n2 = _rmsnorm("norm_mix", h1, wts['g_mix'].reshape(1, -1))
    dd_bufs = {'w_in': gathered(2, h1).reshape(N_CHIPS, ds_, d)}
    DD = {n: q for q, n in enumerate(PACKED)}
    DD['w_in'] = 0

    def mm_dd(name, a, wname, out_dtype, res=None, norm_g=None):
        q = DD[wname]
        return _mm_nn(name, a, dd_bufs[wname], (N_CHIPS, ds_, d), lambda j, k: (0, q, 0), 1, d, out_dtype, res=res,
                      norm_g=norm_g, slabs=N_CHIPS)

    def mm_dd_t(name, pairs, out_dtype, deps=()):
        ps = [(dy, dd_bufs[w], (N_CHIPS, ds_, d), functools.partial(lambda s, q: (0, q, 0), q=DD[w])) for dy, w in pairs]
        return _mm_nt_cols(name, ps, N_CHIPS, ds_, [out_dtype], deps=deps, slabs=N_CHIPS)[0]

    def vec(n):
        return wts[n].reshape(1, -1)

    disc_in = (ssm_a_re[0], ssm_a_im[0], ssm_log_dt[0], ssm_b_re[0], ssm_b_im[0])
    (abar_re, abar_im, bbar_re, bbar_im), disc_vjp = jax.vjp(_s5_discretize, *disc_in)
    gpb = min(S5_GROUPS_PER_BLOCK, n_grp)
    nb = n_grp // gpb
    cb = gpb * n_state
    eye = jnp.eye(gpb, dtype=F32)

    def blockdiag(t):
        return jnp.einsum('jgph,gk->jghkp', t.reshape(nb, gpb, n_state, SSM_GROUP), eye).reshape(nb, gpb * SSM_GROUP, cb)

    def blockdiag_c(t):
        return jnp.einsum('jghp,gk->jkpgh', t.reshape(nb, gpb, SSM_GROUP, n_state), eye).reshape(nb, cb, gpb * SSM_GROUP)

    bblk = jnp.concatenate([blockdiag(bbar_re), blockdiag(bbar_im)], -1).astype(BF16)
    cblk = jnp.concatenate([blockdiag_c(ssm_c_re[0]), -blockdiag_c(ssm_c_im[0])], 1).astype(BF16)
    ab = jnp.stack([abar_re.reshape(nb, cb), abar_im.reshape(nb, cb)], 1)

    u = mm_dd("mix_in", n2, 'w_in', F32)

    up = _perm_rows(u[:, :ws]).astype(BF16)
    ylin_p, s_all = _s5_fwd(up, bblk, ab, cblk)
    ylin = _unperm_rows(ylin_p)

    def gelu_fn(r, v):
        y1 = r[0] + v[0] * r[1]
        y2 = jax.nn.gelu(y1)
        return [y2, y2], []
    fwd_pk, tok = gathered_start(3, ylin_p)
    fwd_a2, tok = gathered_start(4, tok)
    y2, y2b = _rowwise("s5_gelu", gelu_fn, [ylin, (u, 0, ws)], [vec('ssm_d')], [(ws, F32), (ws, BF16)], deps=[tok])
    packed = gathered_finish(3, fwd_pk, y2b).reshape(N_CHIPS, rp, d)
    for n in PACKED:
        dd_bufs[n] = packed
    wglu_full = packed[:, glu_at:glu_at + n_glu, :].reshape(ws, ws)
    wpool_full = packed[:, pool_at:pool_at + n_pool, :].reshape(N_CHIPS, n_pg, pw // N_CHIPS, pw)
    wpool_full = wpool_full.transpose(1, 0, 2, 3).reshape(n_pg, pw, pw)
    z = _mm_nn("s5_glu", y2b, wglu_full, (ws, ws), lambda j, k: (0, 0), 1, ws, F32)

    def glu_fn(r, v):
        y3 = r[0] * _sigmoid(r[1] + v[0])
        return [_rms_fwd(y3, v[1])], []
    m_ssm = _rowwise("s5_gate_norm", glu_fn, [y2, z], [vec('b_glu'), vec('g_out_ssm')], [(ws, BF16)])[0]

    pooled, zp = _pool_fwd(u, ws // pw, wpool_full, vec('pool_scale'))
    fwd_d2, tok = gathered_start(5, zp)
    m_pool = _rmsnorm("norm_pool", zp, vec('g_out_pool'), deps=[tok])
    merged = jnp.concatenate([m_ssm, m_pool], -1)
    h2, hn = mm_dd("mix_out", merged, 'w_out', F32, res=h1, norm_g=vec('g_xattn'))

    memn = _rmsnorm("norm_mem", mem2, vec('g_mem'))
    k_mem = mm_dd("attn_k", memn, 'w_k', BF16)
    v_mem = mm_dd("attn_v", memn, 'w_v', BF16)
    q = mm_dd("attn_q", hn, 'w_q', BF16)
    o = _attn_fwd(q, k_mem, v_mem)
    h3, n4 = mm_dd("attn_out", o, 'w_o', F32, res=h2, norm_g=vec('g_ffn2'))

    ga2 = gathered_finish(4, fwd_a2, h3)
    all_chips = jnp.arange(N_CHIPS, dtype=jnp.int32)
    a2, b2, hm2 = _ffn_up("ffn2_up", n4, ga2, all_chips, all_chips, d, fs)
    gd2 = gathered_finish(5, fwd_d2, hm2).reshape(N_CHIPS, fs, d)
    h4 = _mm_nn("ffn2_down", hm2, gd2, (N_CHIPS, fs, d // 2), lambda j, k: (0, 0, j), 1, d, F32, res=h3, alpha=0.5,
                slabs=N_CHIPS)

    def loss_fn(r, v):
        h, t = r
        e = _rms_fwd(h, v[0]) - t
        dy = e * (1.0 / d)
        dh, dg = _rms_bwd(dy, h, v[0])
        part = jnp.sum(_colsum(e * e), axis=1, keepdims=True) * (0.5 / d)
        return [dh, 0.5 * dh], [_colsum(dg), jnp.broadcast_to(part, (1, 128))]
    dh4, dy_f2, dg_final, loss_row = _rowwise("loss_head", loss_fn, [h4, tgt], [g_final.reshape(1, -1)],
                                              [(d, F32), (d, BF16)], [d, 128])

    def rs_pair_start(tag, gbufs, after=None):
        land = [lax.empty((N_CHIPS,) + g.shape[2:], BF16) for g in gbufs]
        send, recv, thru, token = _comm_start(tag + "_pair_start", "pair", list(gbufs) + land, after=after)
        return (send, recv, thru), token

    def rs_scatter_start(tag, handle, after):
        send, recv, thru = handle
        n = len(thru) // 2
        res = _comm_wait(tag + "_pair_wait", "pair", thru, send, recv, after)
        parts = [_pair_add(g, r, core) for g, r in zip(res[:n], res[n:])]
        land = [lax.empty(p.shape, BF16) for p in parts]
        send, recv, thru, token = _comm_start(tag + "_scatter_start", "scatter", parts + land)
        return (send, recv, thru), token

    def rs_half_start(tag, handle, after):
        send, recv, thru = handle
        n = len(thru) // 2
        res = _comm_wait(tag + "_scatter_wait", "scatter", thru, send, recv, after)
        full = [_chip_sum(p, g2, chip_idx) for p, g2 in zip(res[:n], res[n:])]
        send, recv, thru, token = _comm_start(tag + "_half_start", "half", full)
        return (send, recv, thru), token

    def rs_finish(tag, handle, after):
        send, recv, thru = handle
        return _comm_wait(tag + "_half_wait", "half", thru, send, recv, after)

    wblk = (None, None, d, fs)

    def ffn_down_bwd(tag, dy_half, a, b, hm, gd_l, deps=()):
        da, db = _mm_nt_cols(tag + "_down_bwd", [(dy_half, gd_l, (None, fs, d), lambda s: (s, 0, 0))],
                             N_CHIPS, fs, [BF16, BF16], epi=_swiglu_bwd, extras=[a, b], deps=deps, tm=1024, row_parts=4)
        g_down = _mm_tn(tag + "_dw_down", hm, dy_half, fs, d // 2, N_CHIPS, 2, jax.ShapeDtypeStruct((N_CHIPS, fs, d), BF16),
                        (None, fs, d // 2), lambda p, q: (p, 0, q), tt=2048)
        return da, db, g_down.reshape(N_CHIPS, 2, fs // 2, d)

    def ffn_up_bwd(tag, da, db, ga_l, deps=()):
        return _mm_nt_k(tag + "_up_bwd", [(da, ga_l, wblk, lambda s: (s, 0, 0, 0)), (db, ga_l, wblk, lambda s: (s, 1, 0, 0))],
                        N_CHIPS, d, BF16, deps=deps)

    def ffn_dw(name, dact, n_in, deps=()):
        return _mm_tn(name, n_in, dact, d // 2, fs, 2, N_CHIPS, jax.ShapeDtypeStruct((N_CHIPS, 2, d // 2, fs), BF16),
                      (None, None, d // 2, fs), lambda p, q: (q, p, 0, 0), tt=2048, deps=deps)

    def dw_dd(name, a, dy, wname, grad_b2):
        q = DD[wname]
        rows = ds_ if wname == 'w_in' else rp
        return _mm_tn(name, a, dy, ds_, d, N_CHIPS, 1, jax.ShapeDtypeStruct((N_CHIPS, rows, d), BF16),
                      (None, ds_, d), lambda p, qq: (p, q, 0), into=grad_b2, tt=2048)

    def norm_bwd(name, dn, h, gname, dres, deps=(), scale=1.0):
        def fn(r, v):
            dx, dg = _rms_bwd(r[0].astype(F32), r[1], v[0])
            tot = dx + r[2]
            return [tot, scale * tot], [_colsum(dg)]
        return _rowwise(name, fn, [dn, h, dres], [vec(gname)], [(d, F32), (d, BF16)], [d], deps=deps)

    da2, db2, g_down2 = ffn_down_bwd("ffn2", dy_f2, a2, b2, hm2, gd2)
    dn4 = ffn_up_bwd("ffn2", da2, db2, ga2)
    g_gate2 = ffn_dw("ffn2_dw_gate", da2, n4)
    g_up2 = ffn_dw("ffn2_dw_up", db2, n4)
    rs_f2, tok = rs_pair_start("ffn2", [g_gate2, g_up2, g_down2])
    dh3, dh3b, dg_ffn2 = norm_bwd("norm_ffn2_bwd", dn4, h3, 'g_ffn2', dh4, deps=[tok])
    rs_f2, tok = rs_scatter_start("ffn2", rs_f2, dh3b)

    do = mm_dd_t("attn_out_bwd", [(dh3b, 'w_o')], BF16, deps=[tok])
    grad_b2 = dw_dd("attn_dw_o", o, dh3b, 'w_o', None)
    dq, dk, dv = _attn_bwd(q, k_mem, v_mem, do)
    dkb, dvb = dk.astype(BF16), dv.astype(BF16)
    grad_b2 = dw_dd("attn_dw_q", hn, dq, 'w_q', grad_b2)
    dhn = mm_dd_t("attn_q_bwd", [(dq, 'w_q')], BF16)
    dh2, dh2b, dg_xattn = norm_bwd("norm_xattn_bwd", dhn, h2, 'g_xattn', dh3)
    grad_b2 = dw_dd("attn_dw_k", memn, dkb, 'w_k', grad_b2)
    grad_b2 = dw_dd("attn_dw_v", memn, dvb, 'w_v', grad_b2)
    dmemn = mm_dd_t("attn_kv_bwd", [(dkb, 'w_k'), (dvb, 'w_v')], F32)
    dg_mem = _rowwise("norm_mem_bwd", lambda r, v: ([], [_colsum(_rms_bwd(r[0], r[1], v[0])[1])]),
                      [dmemn, mem2], [vec('g_mem')], [], [d])[0]

    dmerged = mm_dd_t("mix_out_bwd", [(dh2b, 'w_out')], F32)
    grad_b2 = dw_dd("mix_dw_out", merged, dh2b, 'w_out', grad_b2)

    def gate_bwd_fn(r, v):
        dm, y2_v, z_v = r
        sg = _sigmoid(z_v + v[0])
        y3 = y2_v * sg
        dy3, dg = _rms_bwd(dm, y3, v[1])
        dz = dy3 * y3 * (1.0 - sg)
        return [dy3 * sg, dz], [_colsum(dg), _colsum(dz)]
    dy2a, dzb, dg_out_ssm, db_glu = _rowwise("s5_gate_norm_bwd", gate_bwd_fn, [(dmerged, 0, ws), y2, z],
                                             [vec('b_glu'), vec('g_out_ssm')], [(ws, F32), (ws, BF16)], [ws, ws])
    dy2b_ = _mm_nt_cols("s5_glu_bwd", [(dzb, wglu_full, (ws, ws), lambda s: (0, 0))], 1, ws, [F32])[0]
    dw_glu = _mm_tn("s5_dw_glu", y2b, dzb, ws, ws, 1, 1, jax.ShapeDtypeStruct((ws, ws), F32), (ws, ws), lambda p, q: (0, 0))

    def gelu_bwd_fn(r, v):
        dy2 = r[0] + r[1]
        us = r[3]
        y1 = r[2] + v[0] * us
        kk = math.sqrt(2.0 / math.pi)
        th = jnp.tanh(kk * (y1 + 0.044715 * y1 * y1 * y1))
        dgelu = 0.5 * (1.0 + th) + 0.5 * y1 * (1.0 - th * th) * kk * (1.0 + 3.0 * 0.044715 * y1 * y1)
        dy1 = dy2 * dgelu
        return [dy1, dy1 * v[0]], [_colsum(dy1 * us)]
    dy1b, du_skip, d_ssm_d = _rowwise("s5_gelu_bwd", gelu_bwd_fn, [dy2a, dy2b_, ylin, (u, 0, ws)], [vec('ssm_d')],
                                      [(ws, BF16), (ws, F32)], [ws])

    bblk_t = jnp.swapaxes(bblk, 1, 2)
    cblk_t = jnp.swapaxes(cblk, 1, 2)
    du_p, d_bblk, d_cblk_t, d_ab = _s5_bwd(_perm_rows(dy1b), up, s_all, bblk_t, ab, cblk_t)
    du_ssm = _unperm_rows(du_p)

    dzp, dg_out_pool = _rowwise("norm_pool_bwd", lambda r, v: (lambda dx, dg: ([dx], [_colsum(dg)]))(*_rms_bwd(r[0], r[1], v[0])),
                                [(dmerged, 1, ws), zp], [vec('g_out_pool')], [(ws, F32)], [ws])
    dps, dw_pool, d_pool_scale = _pool_bwd1(dzp, pooled, wpool_full, vec('pool_scale'))
    du_pool = _pool_bwd2(dps, n_pg)

    dub = _rowwise("mix_du", lambda r, v: ([jnp.concatenate([r[0] + r[1], r[2]], -1)], []),
                   [du_ssm, du_skip, du_pool], [], [(d, BF16)])[0]
    dn2 = mm_dd_t("mix_in_bwd", [(dub, 'w_in')], BF16)
    grad_in = dw_dd("mix_dw_in", n2, dub, 'w_in', None)
    tail_g = jnp.concatenate([
        dw_glu.reshape(N_CHIPS, n_glu, d),
        dw_pool.reshape(n_pg, N_CHIPS, pw // N_CHIPS, pw).transpose(1, 0, 2, 3).reshape(N_CHIPS, n_pool, d),
        jnp.zeros((N_CHIPS, n_pad, d), F32)], 1).astype(BF16)
    grad_b2 = lax.dynamic_update_slice(grad_b2, tail_g, (0, glu_at, 0))
    rs_mix, tok = rs_pair_start("mixers", [grad_b2.reshape(N_CHIPS, 2, rp // 2, d),
                                           grad_in.reshape(N_CHIPS, 2, ds_ // 2, d)])
    dh1, dy_f1, dg_mix = norm_bwd("norm_mix_bwd", dn2, h1, 'g_mix', dh2, deps=[tok], scale=0.5)
    rs_mix, tok = rs_scatter_start("mixers", rs_mix, dy_f1)

    da1, db1, g_down1 = ffn_down_bwd("ffn1", dy_f1, a1, b1, hm1, gd1, deps=[tok])
    rs_d1, tok = rs_pair_start("ffn1_down", [g_down1])
    dn1 = ffn_up_bwd("ffn1", da1, db1, ga1, deps=[tok])
    rs_d1, tok = rs_scatter_start("ffn1_down", rs_d1, dn1)
    grad_x, _, dg_ffn1 = norm_bwd("norm_ffn1_bwd", dn1, x2, 'g_ffn1', dh1, deps=[tok])

    def undiag(t):
        return jnp.einsum('jghkp,gk->jgph', t.reshape(nb, gpb, SSM_GROUP, gpb, n_state), eye).reshape(n_grp, n_state, SSM_GROUP)

    d_bbar_re, d_bbar_im = undiag(d_bblk[:, :, :cb]), undiag(d_bblk[:, :, cb:])
    d_c_re = undiag(d_cblk_t[:, :, :cb]).transpose(0, 2, 1)
    d_c_im = -undiag(d_cblk_t[:, :, cb:]).transpose(0, 2, 1)
    d_abar = jnp.sum(d_ab, axis=2).reshape(nb, 2, gpb, n_state)
    d_abar_re = d_abar[:, 0].reshape(n_grp, n_state)
    d_abar_im = d_abar[:, 1].reshape(n_grp, n_state)
    d_a_re, d_a_im, d_log_dt, d_b_re, d_b_im = disc_vjp((d_abar_re, d_abar_im, d_bbar_re, d_bbar_im))

    small_g = {'g_ffn1': dg_ffn1, 'g_mix': dg_mix, 'ssm_a_re': d_a_re, 'ssm_a_im': d_a_im, 'ssm_log_dt': d_log_dt,
               'ssm_b_re': d_b_re, 'ssm_b_im': d_b_im, 'ssm_c_re': d_c_re, 'ssm_c_im': d_c_im, 'ssm_d': d_ssm_d,
               'b_glu': db_glu, 'pool_scale': d_pool_scale, 'g_out_ssm': dg_out_ssm, 'g_out_pool': dg_out_pool,
               'g_xattn': dg_xattn, 'g_mem': dg_mem, 'g_ffn2': dg_ffn2, 'g_final': dg_final}
    sizes = [wts[n].size for n in SMALL]
    total = sum(sizes) + 128
    rows_s = -(-total // (128 * 256)) * 256
    flat = jnp.concatenate([small_g[n].reshape(-1) for n in SMALL] + [loss_row.reshape(-1)])
    flat = jnp.pad(flat, (0, rows_s * 128 - total)).reshape(rows_s, 128)
    sw_send, sw_recv, sw_thru, tok = _comm_start("small_swap_start", "swap", [flat, lax.empty(flat.shape, F32)])
    g_gate1 = ffn_dw("ffn1_dw_gate", da1, n1, deps=[tok])
    rs_g1, tok_g1 = rs_pair_start("ffn1_gate", [g_gate1])
    sw_v, sw_got = _comm_wait("small_swap_wait", "swap", sw_thru, sw_send, sw_recv, tok_g1)
    slots = _add_into_slot(sw_v, sw_got, chip.reshape(1))
    bc_send, bc_recv, bc_thru, tok = _comm_start("small_bcast_start", "bcast", [slots])
    g_up1 = ffn_dw("ffn1_dw_up", db1, n1, deps=[tok])
    rs_g1, tok = rs_scatter_start("ffn1_gate", rs_g1, g_up1)
    slots, = _comm_wait("small_bcast_wait", "bcast", bc_thru, bc_send, bc_recv, tok)
    red = _sum_slots(slots).reshape(-1)
    loss = red[sum(sizes)]

    def flat_small(t):
        return jnp.pad(jnp.concatenate([t[n].reshape(-1) for n in SMALL]), (0, rows_s * 128 - sum(sizes))).reshape(rows_s, 128)
    sg_, sd_, sm_, sv_ = _adamw("adamw_small", flat_small(wts), flat_small(mom), flat_small(var), red.reshape(1, rows_s, 128))
    out = {}
    off = 0
    for n, sz in zip(SMALL, sizes):
        for key, arr in (('grad', sg_), ('delta', sd_), ('m', sm_), ('v', sv_)):
            out[key, n] = arr.reshape(-1)[off:off + sz].reshape(wts[n].shape)
        off += sz

    def upd(n, g_arr, half, row_off, shape2):
        res = _adamw("adamw_" + n, wts[n].reshape(shape2), mom[n].reshape(shape2), var[n].reshape(shape2), g_arr, half, row_off)
        for key, arr in zip(('grad', 'delta', 'm', 'v'), res):
            out[key, n] = arr.reshape(wts[n].shape)
        return res[3]

    rs_u1, tok = rs_pair_start("ffn1_up", [g_up1], after=sv_)
    rs_f2, tok = rs_half_start("ffn2", rs_f2, tok)
    rs_u1, tok = rs_scatter_start("ffn1_up", rs_u1, tok)
    rs_mix, tok = rs_half_start("mixers", rs_mix, tok)
    full_gate2, full_up2, full_down2 = rs_finish("ffn2", rs_f2, tok)
    upd('w2_gate', full_gate2.reshape(1, d, fs), 0, 0, (d, fs))
    upd('w2_up', full_up2.reshape(1, d, fs), 0, 0, (d, fs))
    last = upd('w2_down', full_down2.reshape(1, fs, d), 0, 0, (fs, d))
    rs_d1, tok = rs_half_start("ffn1_down", rs_d1, last)
    full_b2, full_in = rs_finish("mixers", rs_mix, tok)
    full_b2 = full_b2.reshape(1, rp, d)
    last = upd('w_in', full_in.reshape(1, ds_, d), 0, 0, (ds_, d))
    for n in PACKED:
        last = upd(n, full_b2, 0, DD[n] * ds_, (ds_, d))
    glu_shape, pool_shape = (ws // N_CHIPS, ws), (n_pg * pw // N_CHIPS, pw)
    upd('w_glu', full_b2[:, glu_at:glu_at + n_glu].reshape((1,) + glu_shape), 0, 0, glu_shape)
    upd('w_pool', full_b2[:, pool_at:pool_at + n_pool].reshape((1,) + pool_shape), 0, 0, pool_shape)
    full_down1, = rs_finish("ffn1_down", rs_d1, last)
    last = upd('w1_down', full_down1.reshape(1, fs, d), 0, 0, (fs, d))
    rs_g1, tok = rs_half_start("ffn1_gate", rs_g1, last)
    rs_u1, tok = rs_half_start("ffn1_up", rs_u1, tok)
    full_gate1, = rs_finish("ffn1_gate", rs_g1, tok)
    last = upd('w1_gate', full_gate1.reshape(1, d, fs), 0, 0, (d, fs))
    full_up1, = rs_finish("ffn1_up", rs_u1, last)
    upd('w1_up', full_up1.reshape(1, d, fs), 0, 0, (d, fs))

    return (loss, grad_x[None], *[out['grad', n] for n in WEIGHTS], *[out['delta', n] for n in WEIGHTS],
            *[out['m', n] for n in WEIGHTS], *[out['v', n] for n in WEIGHTS])
```

```python
import functools
import math

import jax
import jax.numpy as jnp
from jax import lax
from jax.experimental import pallas as pl
from jax.experimental.pallas import tpu as pltpu

F32 = jnp.float32
BF16 = jnp.bfloat16
EPS = 1e-6
ADAM_LR, ADAM_B1, ADAM_B2, ADAM_EPS, ADAM_WD, ADAM_STEP = 0.001, 0.9, 0.999, 1e-08, 0.01, 10
POOL_WINDOWS = (2, 4, 8, 16)
SSM_GROUP = 16
S5_GROUPS_PER_BLOCK = 16
S5_LANES = 8
MEM_HEADS = 4
N_CHIPS = 4
VMEM_LIMIT_V7X = 56 * 1024 * 1024
MESH = pl.DeviceIdType.MESH

WEIGHTS = ['g_ffn1', 'w1_gate', 'w1_up', 'w1_down', 'g_mix', 'w_in', 'ssm_a_re', 'ssm_a_im', 'ssm_log_dt',
           'ssm_b_re', 'ssm_b_im', 'ssm_c_re', 'ssm_c_im', 'ssm_d', 'w_glu', 'b_glu', 'w_pool', 'pool_scale',
           'g_out_ssm', 'g_out_pool', 'w_out', 'g_xattn', 'g_mem', 'w_q', 'w_k', 'w_v', 'w_o', 'g_ffn2',
           'w2_gate', 'w2_up', 'w2_down', 'g_final']
BIG = ['w1_gate', 'w1_up', 'w1_down', 'w_in', 'w_glu', 'w_pool', 'w_out', 'w_q', 'w_k', 'w_v', 'w_o',
       'w2_gate', 'w2_up', 'w2_down']
SMALL = [n for n in WEIGHTS if n not in BIG]


def _tile(n, target, mult=8):
    best = None
    for d in range(1, n + 1):
        if n % d == 0 and d <= target and d % mult == 0:
            best = d
    return best if best is not None else n


def _params(sem=None):
    if sem is None:
        return pltpu.CompilerParams(vmem_limit_bytes=VMEM_LIMIT_V7X)
    return pltpu.CompilerParams(dimension_semantics=sem, vmem_limit_bytes=VMEM_LIMIT_V7X)


def _sigmoid(x):
    return 1.0 / (1.0 + jnp.exp(-x))


def _sigmoid_approx(x):
    return pl.reciprocal(1.0 + jnp.exp(-x), approx=True)


def _rms_fwd(x, g):
    r = lax.rsqrt(jnp.mean(x * x, axis=-1, keepdims=True) + EPS)
    return x * r * g


def _rms_bwd(dy, x, g):
    r = lax.rsqrt(jnp.mean(x * x, axis=-1, keepdims=True) + EPS)
    dxh = dy * g
    dx = r * dxh - x * (r * r * r) * jnp.mean(dxh * x, axis=-1, keepdims=True)
    return dx, dy * x * r


def _colsum(v):
    return jnp.sum(v, axis=0, keepdims=True)


def _rowwise(name, fn, rows, vecs, out_defs, red_defs=(), tm=256, deps=()):
    rows = [r if isinstance(r, tuple) else (r, 0, r.shape[1]) for r in rows]
    t_rows = rows[0][0].shape[0]
    tm = _tile(t_rows, tm)
    nr, nv, no, nd, nx = len(rows), len(vecs), len(out_defs), len(red_defs), len(deps)

    def body(*refs):
        r, v = refs[:nr], refs[nr:nr + nv]
        o, d = refs[nr + nv + nx:nr + nv + nx + no], refs[nr + nv + nx + no:]
        outs, reds = fn([x[...] for x in r], [x[...] for x in v])
        for ref, val in zip(o, outs):
            ref[...] = val.astype(ref.dtype)
        if nd:
            @pl.when(pl.program_id(0) == 0)
            def _():
                for ref in d:
                    ref[...] = jnp.zeros(ref.shape, ref.dtype)
            for ref, val in zip(d, reds):
                ref[...] += val

    in_specs = [pl.BlockSpec((tm, w), functools.partial(lambda i, cb: (i, cb), cb=cb)) for (_, cb, w) in rows]
    in_specs += [pl.BlockSpec(v.shape, lambda i: (0, 0)) for v in vecs]
    in_specs += [pl.BlockSpec(memory_space=pl.ANY)] * nx
    out_specs = [pl.BlockSpec((tm, w), lambda i: (i, 0)) for (w, _) in out_defs]
    out_specs += [pl.BlockSpec((1, w), lambda i: (0, 0)) for w in red_defs]
    out_shape = [jax.ShapeDtypeStruct((t_rows, w), dt) for (w, dt) in out_defs]
    out_shape += [jax.ShapeDtypeStruct((1, w), F32) for w in red_defs]
    res = pl.pallas_call(
        body, name=name, grid=(t_rows // tm,), in_specs=in_specs, out_specs=out_specs, out_shape=out_shape,
        compiler_params=_params(("arbitrary",)),
    )(*[r[0] for r in rows], *vecs, *deps)
    return res


def _rmsnorm(name, x, g, tm=256, deps=()):
    return _rowwise(name, lambda r, v: ([_rms_fwd(r[0].astype(F32), v[0])], []), [x], [g],
                    [(x.shape[1], BF16)], tm=tm, deps=deps)[0]


def _mm_nn(name, a, b, b_block, b_idx, nk, n_out, out_dtype, res=None, alpha=1.0, norm_g=None, tm=512, slabs=1):
    t_rows = a.shape[0]
    bk, tn = slabs * b_block[-2], b_block[-1]
    tm = _tile(t_rows, tm)
    nj = n_out // tn
    has_res = res is not None
    has_norm = norm_g is not None
    assert not has_norm or nj == 1

    def body(*refs):
        a_ref, b_ref = refs[0], refs[1]
        res_ref = refs[2] if has_res else None
        g_ref = refs[2 + has_res] if has_norm else None
        o_ref = refs[2 + has_res + has_norm]
        n_ref = refs[3 + has_res + has_norm] if has_norm else None
        k = pl.program_id(2)
        w = b_ref[...].reshape(bk, tn) if slabs > 1 else b_ref[...]
        p = jnp.dot(a_ref[...], w, preferred_element_type=F32)

        def finish(r):
            if has_res:
                r = res_ref[...] + alpha * r
            o_ref[...] = r.astype(o_ref.dtype)
            if has_norm:
                n_ref[...] = _rms_fwd(r, g_ref[...]).astype(n_ref.dtype)

        if nk == 1:
            finish(p)
            return
        acc_ref = refs[3 + has_res + 2 * has_norm]

        @pl.when(k == 0)
        def _():
            acc_ref[...] = p

        @pl.when(k > 0)
        def _():
            acc_ref[...] += p

        @pl.when(k == nk - 1)
        def _():
            finish(acc_ref[...])

    in_specs = [pl.BlockSpec((tm, bk), lambda j, i, k: (i, k)),
                pl.BlockSpec(b_block, lambda j, i, k: b_idx(j, k))]
    args = [a, b]
    if has_res:
        in_specs.append(pl.BlockSpec((tm, tn), lambda j, i, k: (i, j)))
        args.append(res)
    tile = pl.BlockSpec((tm, tn), lambda j, i, k: (i, j))
    out_specs, out_shape = tile, jax.ShapeDtypeStruct((t_rows, n_out), out_dtype)
    if has_norm:
        in_specs.append(pl.BlockSpec((1, n_out), lambda j, i, k: (0, 0)))
        args.append(norm_g)
        out_specs, out_shape = [tile, tile], [out_shape, jax.ShapeDtypeStruct((t_rows, n_out), BF16)]
    return pl.pallas_call(
        body, name=name, grid=(nj, t_rows // tm, nk), in_specs=in_specs, out_specs=out_specs, out_shape=out_shape,
        scratch_shapes=[pltpu.VMEM((tm, tn), F32)] if nk > 1 else [],
        compiler_params=_params(("arbitrary", "arbitrary", "arbitrary")),
    )(*args)


def _dot_nt(x, w):
    return lax.dot_general(x, w, (((1,), (1,)), ((), ())), preferred_element_type=F32)


def _dot_tn(x, y):
    return lax.dot_general(x, y, (((0,), (0,)), ((), ())), preferred_element_type=F32)


def _mm_nt_cols(name, pairs, ns, bn, out_defs, epi=None, extras=(), tm=512, deps=(), slabs=1, row_parts=1):
    t_rows = pairs[0][0].shape[0]
    tm = _tile(t_rows, tm)
    npair, nex, no, nx = len(pairs), len(extras), len(out_defs), len(deps)
    ns, bn = ns // slabs, bn * slabs
    rp = tm // row_parts

    def body(*refs):
        ws = [refs[2 * p + 1][...] for p in range(npair)]
        if slabs > 1:
            ws = [w.reshape(bn, w.shape[-1]) for w in ws]
        for part_i in range(row_parts):
            rows = slice(part_i * rp, (part_i + 1) * rp)
            acc = None
            for p in range(npair):
                part = _dot_nt(refs[2 * p][rows, :], ws[p])
                acc = part if acc is None else acc + part
            ex = [r[rows, :] for r in refs[2 * npair:2 * npair + nex]]
            outs = epi(acc, *ex) if epi is not None else (acc,)
            for ref, val in zip(refs[2 * npair + nex + nx:], outs):
                ref[rows, :] = val.astype(ref.dtype)

    in_specs, args = [], []
    for (dy, w, w_block, w_idx) in pairs:
        in_specs.append(pl.BlockSpec((tm, dy.shape[1]), lambda s, i: (i, 0)))
        in_specs.append(pl.BlockSpec(w_block, functools.partial(lambda s, i, f: f(s), f=w_idx)))
        args += [dy, w]
    for e in extras:
        in_specs.append(pl.BlockSpec((tm, bn), lambda s, i: (i, s)))
        args.append(e)
    in_specs += [pl.BlockSpec(memory_space=pl.ANY)] * nx
    args += list(deps)
    res = pl.pallas_call(
        body, name=name, grid=(ns, t_rows // tm), in_specs=in_specs,
        out_specs=[pl.BlockSpec((tm, bn), lambda s, i: (i, s)) for _ in range(no)],
        out_shape=[jax.ShapeDtypeStruct((t_rows, ns * bn), dt) for dt in out_defs],
        compiler_params=_params(("arbitrary", "arbitrary")),
    )(*args)
    return res


def _mm_nt_k(name, pairs, ns, n_out, out_dtype, tm=512, deps=()):
    t_rows = pairs[0][0].shape[0]
    tm = _tile(t_rows, tm)
    npair, nx = len(pairs), len(deps)

    def body(*refs):
        o_ref, acc_ref = refs[2 * npair + nx], refs[2 * npair + nx + 1]
        s = pl.program_id(1)
        acc = None
        for p in range(npair):
            part = _dot_nt(refs[2 * p][...], refs[2 * p + 1][...])
            acc = part if acc is None else acc + part

        @pl.when(s == 0)
        def _():
            acc_ref[...] = acc

        @pl.when(s > 0)
        def _():
            acc_ref[...] += acc

        @pl.when(s == ns - 1)
        def _():
            o_ref[...] = acc_ref[...].astype(o_ref.dtype)

    in_specs, args = [], []
    for (a, w, w_block, w_idx) in pairs:
        in_specs.append(pl.BlockSpec((tm, w_block[-1]), lambda i, s: (i, s)))
        in_specs.append(pl.BlockSpec(w_block, functools.partial(lambda i, s, f: f(s), f=w_idx)))
        args += [a, w]
    in_specs += [pl.BlockSpec(memory_space=pl.ANY)] * nx
    args += list(deps)
    return pl.pallas_call(
        body, name=name, grid=(t_rows // tm, ns), in_specs=in_specs,
        out_specs=pl.BlockSpec((tm, n_out), lambda i, s: (i, 0)),
        out_shape=jax.ShapeDtypeStruct((t_rows, n_out), out_dtype),
        scratch_shapes=[pltpu.VMEM((tm, n_out), F32)],
        compiler_params=_params(("arbitrary", "arbitrary")),
    )(*args)


def _mm_tn(name, a, b, bk, bn, n_p, n_q, out_shape, out_block, out_idx, into=None, a_off=0, b_off=0, tt=512,
           deps=()):
    t_rows = a.shape[0]
    tt = _tile(t_rows, tt, 16)
    nt = t_rows // tt
    has_into = into is not None
    nx = len(deps)

    def body(*refs):
        a_ref, b_ref = refs[0], refs[1]
        o_ref, acc_ref = refs[2 + has_into + nx], refs[3 + has_into + nx]
        t = pl.program_id(2)
        part = _dot_tn(a_ref[...], b_ref[...])

        @pl.when(t == 0)
        def _():
            acc_ref[...] = part

        @pl.when(t > 0)
        def _():
            acc_ref[...] += part

        @pl.when(t == nt - 1)
        def _():
            o_ref[...] = acc_ref[...].astype(o_ref.dtype)

    in_specs = [pl.BlockSpec((tt, bk), lambda p, q, t: (t, p + a_off)),
                pl.BlockSpec((tt, bn), lambda p, q, t: (t, q + b_off))]
    args = [a, b]
    aliases = {}
    if has_into:
        in_specs.append(pl.BlockSpec(memory_space=pl.ANY))
        args.append(into)
        aliases = {2: 0}
        out_shape = jax.ShapeDtypeStruct(into.shape, into.dtype)
    in_specs += [pl.BlockSpec(memory_space=pl.ANY)] * nx
    args += list(deps)
    return pl.pallas_call(
        body, name=name, grid=(n_p, n_q, nt), in_specs=in_specs,
        out_specs=pl.BlockSpec(out_block, lambda p, q, t: out_idx(p, q)),
        out_shape=out_shape, scratch_shapes=[pltpu.VMEM((bk, bn), F32)],
        input_output_aliases=aliases,
        compiler_params=_params(("arbitrary", "arbitrary", "arbitrary")),
    )(*args)


def _ffn_up(name, n, ga, slots, cols, d_model, fs, into=None, tm=512):
    t_rows = n.shape[0]
    tm = _tile(t_rows, tm)
    n_sh = slots.shape[0]
    has_into = into is not None

    row_parts = 2 if tm % 32 == 0 else 1

    def body(slot_ref, col_ref, n_ref, wg_ref, wu_ref, *refs):
        a_ref, b_ref, h_ref = refs[3 * has_into:]
        wg, wu = wg_ref[...], wu_ref[...]
        for part in range(row_parts):
            rows = slice(part * (tm // row_parts), (part + 1) * (tm // row_parts))
            x = n_ref[rows, :]
            a = jnp.dot(x, wg, preferred_element_type=F32)
            b = jnp.dot(x, wu, preferred_element_type=F32)
            a_ref[rows, :] = a.astype(a_ref.dtype)
            b_ref[rows, :] = b.astype(b_ref.dtype)
            h_ref[rows, :] = (a * _sigmoid_approx(a) * b).astype(h_ref.dtype)

    w_block = (None, None, d_model, fs)
    out = jax.ShapeDtypeStruct((t_rows, N_CHIPS * fs), BF16)
    in_specs = [pl.BlockSpec((tm, d_model), lambda s, i, sl, co: (i, 0)),
                pl.BlockSpec(w_block, lambda s, i, sl, co: (sl[s], 0, 0, 0)),
                pl.BlockSpec(w_block, lambda s, i, sl, co: (sl[s], 1, 0, 0))]
    args = [slots, cols, n, ga, ga]
    aliases = {}
    if has_into:
        in_specs += [pl.BlockSpec(memory_space=pl.ANY)] * 3
        args += list(into)
        aliases = {5: 0, 6: 1, 7: 2}
    return pl.pallas_call(
        body, name=name,
        grid_spec=pltpu.PrefetchScalarGridSpec(
            num_scalar_prefetch=2, grid=(n_sh, t_rows // tm), in_specs=in_specs,
            out_specs=[pl.BlockSpec((tm, fs), lambda s, i, sl, co: (i, co[s]))] * 3),
        out_shape=[out, out, out], input_output_aliases=aliases,
        compiler_params=_params(("arbitrary", "arbitrary")),
    )(*args)


def _swiglu_bwd(dh, a, b):
    a = a.astype(F32)
    b = b.astype(F32)
    sg = _sigmoid_approx(a)
    return dh * b * sg * (1.0 + a * (1.0 - sg)), dh * a * sg


def _attn_fwd(q, k, v, tm=512):
    t_rows, d_model = q.shape
    n_mem = k.shape[0]
    hd = d_model // MEM_HEADS
    scale = hd ** -0.5
    tm = _tile(t_rows, tm)

    def body(q_ref, k_ref, v_ref, o_ref):
        for h in range(MEM_HEADS):
            cols = slice(h * hd, (h + 1) * hd)
            s = _dot_nt(q_ref[:, cols], k_ref[:, cols]) * scale
            s = s - jnp.max(s, axis=-1, keepdims=True)
            e = jnp.exp(s)
            p = e / jnp.sum(e, axis=-1, keepdims=True)
            o_ref[:, cols] = jnp.dot(p.astype(BF16), v_ref[:, cols], preferred_element_type=F32).astype(o_ref.dtype)

    return pl.pallas_call(
        body, name="attn_fwd", grid=(t_rows // tm,),
        in_specs=[pl.BlockSpec((tm, d_model), lambda i: (i, 0)),
                  pl.BlockSpec((n_mem, d_model), lambda i: (0, 0)),
                  pl.BlockSpec((n_mem, d_model), lambda i: (0, 0))],
        out_specs=pl.BlockSpec((tm, d_model), lambda i: (i, 0)),
        out_shape=jax.ShapeDtypeStruct((t_rows, d_model), BF16),
        compiler_params=_params(("arbitrary",)),
    )(q, k, v)


def _attn_bwd(q, k, v, do, tm=512):
    t_rows, d_model = q.shape
    n_mem = k.shape[0]
    hd = d_model // MEM_HEADS
    scale = hd ** -0.5
    tm = _tile(t_rows, tm, 16)

    def body(q_ref, k_ref, v_ref, do_ref, dq_ref, dk_ref, dv_ref):
        @pl.when(pl.program_id(0) == 0)
        def _():
            dk_ref[...] = jnp.zeros(dk_ref.shape, F32)
            dv_ref[...] = jnp.zeros(dv_ref.shape, F32)

        for h in range(MEM_HEADS):
            cols = slice(h * hd, (h + 1) * hd)
            qh, kh, vh, doh = q_ref[:, cols], k_ref[:, cols], v_ref[:, cols], do_ref[:, cols]
            s = _dot_nt(qh, kh) * scale
            s = s - jnp.max(s, axis=-1, keepdims=True)
            e = jnp.exp(s)
            p = e / jnp.sum(e, axis=-1, keepdims=True)
            dv_ref[:, cols] += _dot_tn(p.astype(BF16), doh)
            dp = _dot_nt(doh, vh)
            ds = (p * (dp - jnp.sum(dp * p, axis=-1, keepdims=True)) * scale).astype(BF16)
            dq_ref[:, cols] = jnp.dot(ds, kh, preferred_element_type=F32).astype(dq_ref.dtype)
            dk_ref[:, cols] += _dot_tn(ds, qh)

    full = pl.BlockSpec((n_mem, d_model), lambda i: (0, 0))
    tile = pl.BlockSpec((tm, d_model), lambda i: (i, 0))
    return pl.pallas_call(
        body, name="attn_bwd", grid=(t_rows // tm,),
        in_specs=[tile, full, full, tile], out_specs=[tile, full, full],
        out_shape=[jax.ShapeDtypeStruct((t_rows, d_model), BF16),
                   jax.ShapeDtypeStruct((n_mem, d_model), F32), jax.ShapeDtypeStruct((n_mem, d_model), F32)],
        compiler_params=_params(("arbitrary",)),
    )(q, k, v, do)


def _split_bf16(v):
    hi = v.astype(BF16)
    return hi, (v - hi.astype(F32)).astype(BF16)


def _pool_fwd(u, col_blk, w_pool, scale, tt=256):
    t_rows = u.shape[0]
    ng, pw = w_pool.shape[0], w_pool.shape[-1]
    width = ng * pw
    tt = _tile(t_rows, tt, 16)
    nt = t_rows // tt
    assert len(POOL_WINDOWS) == ng and tt >= max(POOL_WINDOWS)

    def body(vc_ref, vp_ref, w_ref, sc_ref, pooled_ref, z_ref):
        i = pl.program_id(0)
        r = lax.broadcasted_iota(jnp.int32, (tt, tt), 0)
        c = lax.broadcasted_iota(jnp.int32, (tt, tt), 1)
        t = i * tt + lax.broadcasted_iota(jnp.int32, (tt, 1), 0)
        first = (i > 0).astype(F32)
        for g, w in enumerate(POOL_WINDOWS):
            cols = slice(g * pw, (g + 1) * pw)
            band_c = ((c <= r) & (c > r - w)).astype(BF16)
            band_p = (c > r - w + tt).astype(BF16)
            vc = vc_ref[:, cols]
            ch, cl = _split_bf16(vc)
            ph, plo = _split_bf16(vp_ref[:, cols] * first)
            sums = (jnp.dot(band_c, ch, preferred_element_type=F32) + jnp.dot(band_c, cl, preferred_element_type=F32)
                    + jnp.dot(band_p, ph, preferred_element_type=F32) + jnp.dot(band_p, plo, preferred_element_type=F32))
            cnt = jnp.minimum(t + 1, w).astype(F32)
            pooled = (sums / cnt - vc).astype(BF16)
            pooled_ref[:, cols] = pooled
            z_ref[:, cols] = jnp.dot(pooled, w_ref[g], preferred_element_type=F32) * sc_ref[:, cols]

    tile = pl.BlockSpec((tt, width), lambda i: (i, 0))
    return pl.pallas_call(
        body, name="pool_fwd", grid=(nt,),
        in_specs=[pl.BlockSpec((tt, width), lambda i: (i, col_blk)),
                  pl.BlockSpec((tt, width), lambda i: (jnp.maximum(i - 1, 0), col_blk)),
                  pl.BlockSpec((ng, pw, pw), lambda i: (0, 0, 0)),
                  pl.BlockSpec((1, width), lambda i: (0, 0))],
        out_specs=[tile, tile],
        out_shape=[jax.ShapeDtypeStruct((t_rows, width), BF16), jax.ShapeDtypeStruct((t_rows, width), F32)],
        compiler_params=_params(("arbitrary",)),
    )(u, u, w_pool, scale)


def _pool_bwd1(dz, pooled, w_pool, scale, tt=256):
    t_rows = dz.shape[0]
    ng, pw = w_pool.shape[0], w_pool.shape[-1]
    width = ng * pw
    tt = _tile(t_rows, tt, 16)
    nt = t_rows // tt

    def body(dz_ref, p_ref, w_ref, sc_ref, dp_ref, dw_ref, dsc_ref):
        i = pl.program_id(0)

        @pl.when(i == 0)
        def _():
            dw_ref[...] = jnp.zeros(dw_ref.shape, F32)
            dsc_ref[...] = jnp.zeros(dsc_ref.shape, F32)

        t = i * tt + lax.broadcasted_iota(jnp.int32, (tt, 1), 0)
        for g, w in enumerate(POOL_WINDOWS):
            cols = slice(g * pw, (g + 1) * pw)
            dz_v = dz_ref[:, cols]
            pooled_v = p_ref[:, cols]
            zpre = jnp.dot(pooled_v, w_ref[g], preferred_element_type=F32)
            dsc_ref[:, cols] += _colsum(dz_v * zpre)
            dzs = (dz_v * sc_ref[:, cols]).astype(BF16)
            dw_ref[g] += _dot_tn(pooled_v, dzs)
            cnt = jnp.minimum(t + 1, w).astype(F32)
            dp_ref[:, cols] = _dot_nt(dzs, w_ref[g]) / cnt

    tile = pl.BlockSpec((tt, width), lambda i: (i, 0))
    whole_w = pl.BlockSpec((ng, pw, pw), lambda i: (0, 0, 0))
    vec_w = pl.BlockSpec((1, width), lambda i: (0, 0))
    return pl.pallas_call(
        body, name="pool_bwd1", grid=(nt,),
        in_specs=[tile, tile, whole_w, vec_w], out_specs=[tile, whole_w, vec_w],
        out_shape=[jax.ShapeDtypeStruct((t_rows, width), F32), jax.ShapeDtypeStruct((ng, pw, pw), F32),
                   jax.ShapeDtypeStruct((1, width), F32)],
        compiler_params=_params(("arbitrary",)),
    )(dz, pooled, w_pool, scale)


def _pool_bwd2(dps, ng, tt=256):
    t_rows, width = dps.shape
    pw = width // ng
    tt = _tile(t_rows, tt, 16)
    nt = t_rows // tt

    def body(dc_ref, dn_ref, dv_ref):
        i = pl.program_id(0)
        r = lax.broadcasted_iota(jnp.int32, (tt, tt), 0)
        c = lax.broadcasted_iota(jnp.int32, (tt, tt), 1)
        t = i * tt + lax.broadcasted_iota(jnp.int32, (tt, 1), 0)
        last = (i < nt - 1).astype(F32)
        for g, w in enumerate(POOL_WINDOWS):
            cols = slice(g * pw, (g + 1) * pw)
            band_c = ((c >= r) & (c < r + w)).astype(BF16)
            band_n = (c < r + w - tt).astype(BF16)
            dc = dc_ref[:, cols]
            ch, cl = _split_bf16(dc)
            nh, nl = _split_bf16(dn_ref[:, cols] * last)
            sums = (jnp.dot(band_c, ch, preferred_element_type=F32) + jnp.dot(band_c, cl, preferred_element_type=F32)
                    + jnp.dot(band_n, nh, preferred_element_type=F32) + jnp.dot(band_n, nl, preferred_element_type=F32))
            cnt = jnp.minimum(t + 1, w).astype(F32)
            dv_ref[:, cols] = sums - dc * cnt

    tile = pl.BlockSpec((tt, width), lambda i: (i, 0))
    return pl.pallas_call(
        body, name="pool_bwd2", grid=(nt,),
        in_specs=[tile, pl.BlockSpec((tt, width), lambda i: (jnp.minimum(i + 1, nt - 1), 0))],
        out_specs=tile, out_shape=jax.ShapeDtypeStruct((t_rows, width), F32),
        compiler_params=_params(("arbitrary",)),
    )(dps, dps)


def _cpow(ar, ai, n):
    rr, ri, br, bi = None, None, ar, ai
    while n:
        if n & 1:
            rr, ri = (br, bi) if rr is None else (rr * br - ri * bi, rr * bi + ri * br)
        n >>= 1
        if n:
            br, bi = br * br - bi * bi, 2.0 * br * bi
    return rr, ri


def _chunk_carries(st_re, st_im, pr, pi, order):
    cb = st_re.shape[1]
    sub = lax.broadcasted_iota(jnp.int32, (S5_LANES, cb), 0)
    cr = jnp.zeros((S5_LANES, cb), F32)
    ci = jnp.zeros((S5_LANES, cb), F32)
    prev_r = jnp.zeros((1, cb), F32)
    prev_i = jnp.zeros((1, cb), F32)
    for k, src in order:
        er, ei = st_re[src:src + 1, :], st_im[src:src + 1, :]
        nr = er + pr * prev_r - pi * prev_i
        ni = ei + pr * prev_i + pi * prev_r
        cr = jnp.where(sub == k, jnp.broadcast_to(nr, (S5_LANES, cb)), cr)
        ci = jnp.where(sub == k, jnp.broadcast_to(ni, (S5_LANES, cb)), ci)
        prev_r, prev_i = nr, ni
    return cr, ci


def _s5_fwd(up, bblk, ab, cblk, tt=128):
    n_rows, ws = up.shape
    nb, cw, cb2 = bblk.shape
    cb = cb2 // 2
    lc = n_rows // S5_LANES
    tt = _tile(lc, tt, 1)
    nt = lc // tt
    rt = S5_LANES * tt

    def body(u_ref, b_ref, ab_ref, c_ref, y_ref, s_ref, bu_ref, st_re, st_im):
        ps, ti = pl.program_id(1), pl.program_id(2)
        ar = jnp.broadcast_to(ab_ref[0:1, :], (S5_LANES, cb))
        ai = jnp.broadcast_to(ab_ref[1:2, :], (S5_LANES, cb))

        @pl.when((ps == 0) & (ti == 0))
        def _():
            st_re[...] = jnp.zeros(st_re.shape, F32)
            st_im[...] = jnp.zeros(st_im.shape, F32)

        @pl.when((ps == 1) & (ti == 0))
        def _():
            pr, pi = _cpow(ab_ref[0:1, :], ab_ref[1:2, :], lc)
            cr, ci = _chunk_carries(st_re, st_im, pr, pi, [(k, k - 1) for k in range(1, S5_LANES)])
            st_re[...] = cr
            st_im[...] = ci

        bu_ref[...] = jnp.dot(u_ref[...], b_ref[...], preferred_element_type=F32)

        def step(t, carry, store):
            sr, si = carry
            rows = pl.ds(pl.multiple_of(t * S5_LANES, S5_LANES), S5_LANES)
            nr = ar * sr - ai * si + bu_ref[rows, 0:cb]
            ni = ar * si + ai * sr + bu_ref[rows, cb:cb2]
            if store:
                s_ref[rows, 0:cb] = nr
                s_ref[rows, cb:cb2] = ni
            return nr, ni

        @pl.when(ps == 0)
        def _():
            sr, si = lax.fori_loop(0, tt, functools.partial(step, store=False), (st_re[...], st_im[...]))
            st_re[...] = sr
            st_im[...] = si

        @pl.when(ps == 1)
        def _():
            sr, si = lax.fori_loop(0, tt, functools.partial(step, store=True), (st_re[...], st_im[...]))
            st_re[...] = sr
            st_im[...] = si
            y_ref[...] = jnp.dot(s_ref[...].astype(BF16), c_ref[...], preferred_element_type=F32)

    return pl.pallas_call(
        body, name="s5_fwd", grid=(nb, 2, nt),
        in_specs=[pl.BlockSpec((rt, cw), lambda j, ps, ti: (ti, j)),
                  pl.BlockSpec((None, cw, cb2), lambda j, ps, ti: (j, 0, 0)),
                  pl.BlockSpec((None, 2, cb), lambda j, ps, ti: (j, 0, 0)),
                  pl.BlockSpec((None, cb2, cw), lambda j, ps, ti: (j, 0, 0))],
        out_specs=[pl.BlockSpec((rt, cw), lambda j, ps, ti: (ti * ps, j)),
                   pl.BlockSpec((None, rt, cb2), lambda j, ps, ti: (j, ti * ps, 0))],
        out_shape=[jax.ShapeDtypeStruct((n_rows, ws), F32), jax.ShapeDtypeStruct((nb, n_rows, cb2), F32)],
        scratch_shapes=[pltpu.VMEM((rt, cb2), F32), pltpu.VMEM((S5_LANES, cb), F32), pltpu.VMEM((S5_LANES, cb), F32)],
        compiler_params=_params(("arbitrary", "arbitrary", "arbitrary")),
    )(up, bblk, ab, cblk)


def _s5_bwd(dyp, up, s_all, bblk_t, ab, cblk_t, tt=128):
    n_rows, ws = up.shape
    nb, cb2, cw = bblk_t.shape
    cb = cb2 // 2
    lc = n_rows // S5_LANES
    tt = _tile(lc, tt, 1)
    nt = lc // tt
    rt = S5_LANES * tt

    def body(dy_ref, u_ref, s_ref, bt_ref, ab_ref, ct_ref, du_ref, db_ref, dc_ref, da_ref, ds_ref, st_re, st_im):
        ps, ti = pl.program_id(1), pl.program_id(2)
        ar = jnp.broadcast_to(ab_ref[0:1, :], (S5_LANES, cb))
        ai = jnp.broadcast_to(ab_ref[1:2, :], (S5_LANES, cb))

        @pl.when((ps == 0) & (ti == 0))
        def _():
            st_re[...] = jnp.zeros(st_re.shape, F32)
            st_im[...] = jnp.zeros(st_im.shape, F32)
            db_ref[...] = jnp.zeros(db_ref.shape, F32)
            dc_ref[...] = jnp.zeros(dc_ref.shape, F32)
            da_ref[...] = jnp.zeros(da_ref.shape, F32)

        @pl.when((ps == 1) & (ti == 0))
        def _():
            pr, pi = _cpow(ab_ref[0:1, :], -ab_ref[1:2, :], lc)
            cr, ci = _chunk_carries(st_re, st_im, pr, pi, [(k, k + 1) for k in range(S5_LANES - 2, -1, -1)])
            st_re[...] = cr
            st_im[...] = ci

        ds_ref[...] = jnp.dot(dy_ref[...], ct_ref[...], preferred_element_type=F32)

        def rows_of(i):
            return pl.ds(pl.multiple_of((tt - 1 - i) * S5_LANES, S5_LANES), S5_LANES)

        def step0(i, carry):
            gr, gi = carry
            rows = rows_of(i)
            return (ar * gr + ai * gi + ds_ref[rows, 0:cb], ar * gi - ai * gr + ds_ref[rows, cb:cb2])

        def step1(i, carry):
            gr, gi, acr, aci = carry
            rows = rows_of(i)
            sr, si = s_ref[rows, 0:cb], s_ref[rows, cb:cb2]
            acr = acr + sr * gr + si * gi
            aci = aci + sr * gi - si * gr
            nr = ar * gr + ai * gi + ds_ref[rows, 0:cb]
            ni = ar * gi - ai * gr + ds_ref[rows, cb:cb2]
            ds_ref[rows, 0:cb] = nr
            ds_ref[rows, cb:cb2] = ni
            return nr, ni, acr, aci

        @pl.when(ps == 0)
        def _():
            gr, gi = lax.fori_loop(0, tt, step0, (st_re[...], st_im[...]))
            st_re[...] = gr
            st_im[...] = gi

        @pl.when(ps == 1)
        def _():
            zero = jnp.zeros((S5_LANES, cb), F32)
            gr, gi, acr, aci = lax.fori_loop(0, tt, step1, (st_re[...], st_im[...], zero, zero))
            st_re[...] = gr
            st_im[...] = gi
            da_ref[0] += acr
            da_ref[1] += aci
            dsb = ds_ref[...].astype(BF16)
            du_ref[...] = jnp.dot(dsb, bt_ref[...], preferred_element_type=F32)
            db_ref[...] += _dot_tn(u_ref[...], dsb)
            dc_ref[...] += _dot_tn(dy_ref[...], s_ref[...].astype(BF16))

    def tile_idx(ps, ti):
        return (nt - 1 - ti) * ps + (nt - 1) * (1 - ps)

    return pl.pallas_call(
        body, name="s5_bwd", grid=(nb, 2, nt),
        in_specs=[pl.BlockSpec((rt, cw), lambda j, ps, ti: (nt - 1 - ti, j)),
                  pl.BlockSpec((rt, cw), lambda j, ps, ti: (tile_idx(ps, ti), j)),
                  pl.BlockSpec((None, rt, cb2), lambda j, ps, ti: (j, tile_idx(ps, ti), 0)),
                  pl.BlockSpec((None, cb2, cw), lambda j, ps, ti: (j, 0, 0)),
                  pl.BlockSpec((None, 2, cb), lambda j, ps, ti: (j, 0, 0)),
                  pl.BlockSpec((None, cw, cb2), lambda j, ps, ti: (j, 0, 0))],
        out_specs=[pl.BlockSpec((rt, cw), lambda j, ps, ti: (tile_idx(ps, ti), j)),
                   pl.BlockSpec((None, cw, cb2), lambda j, ps, ti: (j, 0, 0)),
                   pl.BlockSpec((None, cw, cb2), lambda j, ps, ti: (j, 0, 0)),
                   pl.BlockSpec((None, 2, S5_LANES, cb), lambda j, ps, ti: (j, 0, 0, 0))],
        out_shape=[jax.ShapeDtypeStruct((n_rows, ws), F32), jax.ShapeDtypeStruct((nb, cw, cb2), F32),
                   jax.ShapeDtypeStruct((nb, cw, cb2), F32), jax.ShapeDtypeStruct((nb, 2, S5_LANES, cb), F32)],
        scratch_shapes=[pltpu.VMEM((rt, cb2), F32), pltpu.VMEM((S5_LANES, cb), F32), pltpu.VMEM((S5_LANES, cb), F32)],
        compiler_params=_params(("arbitrary", "arbitrary", "arbitrary")),
    )(dyp, up, s_all, bblk_t, ab, cblk_t)


def _s5_discretize(a_re, a_im, log_dt, b_re, b_im):
    dt = jnp.exp(log_dt)[:, None]
    mag = jnp.exp(a_re * dt)
    abar_re = mag * jnp.cos(a_im * dt)
    abar_im = mag * jnp.sin(a_im * dt)
    nr, ni = abar_re - 1.0, abar_im
    den = a_re * a_re + a_im * a_im
    fr = (nr * a_re + ni * a_im) / den
    fi = (ni * a_re - nr * a_im) / den
    bbar_re = fr[..., None] * b_re - fi[..., None] * b_im
    bbar_im = fr[..., None] * b_im + fi[..., None] * b_re
    return abar_re, abar_im, bbar_re, bbar_im


def _perm_rows(a):
    n, c = a.shape
    return a.reshape(S5_LANES, n // S5_LANES, c).transpose(1, 0, 2).reshape(n, c)


def _unperm_rows(a):
    n, c = a.shape
    return a.reshape(n // S5_LANES, S5_LANES, c).transpose(1, 0, 2).reshape(n, c)


HBM = pl.BlockSpec(memory_space=pltpu.HBM)
SEM = pl.BlockSpec(memory_space=pltpu.SEMAPHORE)
ANY = pl.BlockSpec(memory_space=pl.ANY)
EFFECT = pltpu.SideEffectType.DATAFLOW_SIDE_EFFECTING
COPIES_PER_BUFFER = {"ag_ici": 3, "ag_fwd": 3, "pair": N_CHIPS, "scatter": 3, "half": 1, "swap": 1, "bcast": 3}
PAIRED_KINDS = ("pair", "scatter", "swap")


def _place():
    x, y, c = lax.axis_index("x"), lax.axis_index("y"), lax.axis_index("c")
    chips = [(1 - x, y), (x, 1 - y), (1 - x, 1 - y)]
    return x, y, c, 2 * x + y, chips


def _n_copies(kind, n_bufs):
    if isinstance(kind, tuple):
        return len(kind[1])
    return COPIES_PER_BUFFER[kind] * (n_bufs // 2 if kind in PAIRED_KINDS else n_bufs)


def _comm_copies(kind, bufs):
    if isinstance(kind, tuple):
        full = _comm_copies(kind[0], bufs)
        return [full[k] for k in kind[1]]
    x, y, c, s, chips = _place()
    sib = (x, y, 1 - c)
    out = []
    if kind == "ag_ici":
        for w in bufs:
            for cx, cy in chips:
                out.append((w.at[s, c], w.at[s, c], w.at[2 * cx + cy, c], (cx, cy, c)))
    elif kind == "ag_fwd":
        for w in bufs:
            for cx, cy in chips:
                sj = 2 * cx + cy
                out.append((w.at[sj, c], w.at[sj, c], w.at[sj, 1 - c], sib))
    elif kind == "pair":
        n = len(bufs) // 2
        for g, got in zip(bufs[:n], bufs[n:]):
            for t in range(N_CHIPS):
                out.append((g.at[t, 1 - c], got.at[t], got.at[t], sib))
    elif kind == "scatter":
        n = len(bufs) // 2
        for p, got in zip(bufs[:n], bufs[n:]):
            for cx, cy in chips:
                out.append((p.at[2 * cx + cy], got.at[s], got.at[2 * cx + cy], (cx, cy, c)))
    elif kind == "half":
        for f in bufs:
            out.append((f.at[c], f.at[c], f.at[1 - c], sib))
    elif kind == "swap":
        n = len(bufs) // 2
        for v, got in zip(bufs[:n], bufs[n:]):
            out.append((v, got, got, sib))
    elif kind == "bcast":
        for w in bufs:
            for cx, cy in chips:
                out.append((w.at[s], w.at[s], w.at[2 * cx + cy], (cx, cy, c)))
    return out


def _comm_fused(name, kind, bufs):
    n = len(bufs)
    ncp = _n_copies(kind, n)

    def body(*refs):
        outs = refs[n:2 * n]
        send, recv = refs[2 * n:]
        copies = _comm_copies(kind, outs)
        started = []
        for k, (src, dst, _, peer) in enumerate(copies):
            cp = pltpu.make_async_remote_copy(src_ref=src, dst_ref=dst, send_sem=send.at[k], recv_sem=recv.at[k],
                                              device_id=peer, device_id_type=MESH)
            cp.start()
            started.append(cp)
        for k, (_, _, land, peer) in enumerate(copies):
            pltpu.make_async_remote_copy(src_ref=land, dst_ref=land, send_sem=send.at[k], recv_sem=recv.at[k],
                                         device_id=peer, device_id_type=MESH).wait_recv()
        for cp in started:
            cp.wait_send()

    return pl.pallas_call(
        body, name=name, in_specs=[ANY] * n, out_specs=[ANY] * n,
        out_shape=[jax.ShapeDtypeStruct(b.shape, b.dtype) for b in bufs],
        input_output_aliases={k: k for k in range(n)},
        scratch_shapes=[pltpu.SemaphoreType.DMA((ncp,))] * 2,
    )(*bufs)


def _comm_start(name, kind, bufs, after=None):
    n = len(bufs)
    ncp = _n_copies(kind, n)
    nx = 0 if after is None else 1

    def body(*refs):
        refs = refs[n + nx:]
        send, recv = refs[:ncp], refs[ncp:2 * ncp]
        outs = refs[2 * ncp:n + 2 * ncp]
        token = refs[n + 2 * ncp]
        for k, (src, dst, _, peer) in enumerate(_comm_copies(kind, outs)):
            pltpu.make_async_remote_copy(src_ref=src, dst_ref=dst, send_sem=send[k], recv_sem=recv[k],
                                         device_id=peer, device_id_type=MESH).start()
        token[...] = jnp.zeros(token.shape, token.dtype)

    res = pl.pallas_call(
        body, name=name, in_specs=[HBM] * n + [ANY] * nx,
        out_specs=[SEM] * (2 * ncp) + [HBM] * n + [pl.BlockSpec(memory_space=pltpu.VMEM)],
        out_shape=[pltpu.SemaphoreType.DMA(())] * (2 * ncp) + [pltpu.HBM(b.shape, b.dtype) for b in bufs]
        + [jax.ShapeDtypeStruct((8, 128), F32)],
        input_output_aliases={k: 2 * ncp + k for k in range(n)},
        compiler_params=pltpu.CompilerParams(has_side_effects=EFFECT),
    )(*[pltpu.with_memory_space_constraint(b, pltpu.HBM) for b in bufs], *([after] if nx else []))
    return list(res[:ncp]), list(res[ncp:2 * ncp]), list(res[2 * ncp:2 * ncp + n]), res[2 * ncp + n]


def _comm_wait(name, kind, bufs, send_sems, recv_sems, after):
    n = len(bufs)
    ncp = _n_copies(kind, n)

    def body(*refs):
        send, recv = refs[n:n + ncp], refs[n + ncp:n + 2 * ncp]
        outs = refs[n + 2 * ncp + 1:]
        for k, (src, _, land, peer) in enumerate(_comm_copies(kind, outs)):
            cp = pltpu.make_async_remote_copy(src_ref=src, dst_ref=land, send_sem=send[k], recv_sem=recv[k],
                                              device_id=peer, device_id_type=MESH)
            cp.wait_send()
            cp.wait_recv()

    return pl.pallas_call(
        body, name=name, in_specs=[HBM] * n + [SEM] * (2 * ncp) + [ANY], out_specs=[HBM] * n,
        out_shape=[pltpu.HBM(b.shape, b.dtype) for b in bufs],
        input_output_aliases={k: k for k in range(n)},
        compiler_params=pltpu.CompilerParams(has_side_effects=EFFECT),
    )(*bufs, *send_sems, *recv_sems, after)


def _pair_add(g, got, core):
    nchip, _, r, cw = g.shape
    tr = _tile(r, 512, 16)

    def body(c_ref, a_ref, b_ref, o_ref):
        o_ref[...] = a_ref[...] + b_ref[...]

    return pl.pallas_call(
        body, name="grads_pair_add",
        grid_spec=pltpu.PrefetchScalarGridSpec(
            num_scalar_prefetch=1, grid=(nchip, r // tr),
            in_specs=[pl.BlockSpec((None, None, tr, cw), lambda s, i, c_ref: (s, c_ref[0], i, 0)),
                      pl.BlockSpec((None, tr, cw), lambda s, i, c_ref: (s, i, 0))],
            out_specs=pl.BlockSpec((None, tr, cw), lambda s, i, c_ref: (s, i, 0))),
        out_shape=jax.ShapeDtypeStruct((nchip, r, cw), BF16),
        compiler_params=_params(("arbitrary", "arbitrary")),
    )(core, g, got)


def _chip_sum(parts, got, idx):
    _, r, cw = parts.shape
    tr = _tile(r, 512, 16)

    def body(i_ref, own_ref, a_ref, b_ref, c_ref, o_ref):
        o_ref[...] = ((own_ref[...].astype(F32) + a_ref[...].astype(F32)) + b_ref[...].astype(F32)) + c_ref[...].astype(F32)

    def slot(k):
        return pl.BlockSpec((None, tr, cw), lambda i, i_ref: (i_ref[k], i, 0))

    return pl.pallas_call(
        body, name="grads_chip_sum",
        grid_spec=pltpu.PrefetchScalarGridSpec(
            num_scalar_prefetch=1, grid=(r // tr,), in_specs=[slot(0), slot(1), slot(2), slot(3)], out_specs=slot(4)),
        out_shape=jax.ShapeDtypeStruct((2, r, cw), F32),
        compiler_params=_params(("arbitrary",)),
    )(idx, parts, got, got, got)


def _add_into_slot(v, got, chip):
    r, cw = v.shape
    tr = _tile(r, 256)

    def body(c_ref, a_ref, b_ref, o_ref):
        o_ref[...] = a_ref[...] + b_ref[...]

    tile = pl.BlockSpec((tr, cw), lambda i, c_ref: (i, 0))
    return pl.pallas_call(
        body, name="small_pair_add",
        grid_spec=pltpu.PrefetchScalarGridSpec(
            num_scalar_prefetch=1, grid=(r // tr,), in_specs=[tile, tile],
            out_specs=pl.BlockSpec((None, tr, cw), lambda i, c_ref: (c_ref[0], i, 0))),
        out_shape=jax.ShapeDtypeStruct((N_CHIPS, r, cw), F32),
        compiler_params=_params(("arbitrary",)),
    )(chip, v, got)


def _sum_slots(w):
    _, r, cw = w.shape
    tr = _tile(r, 256)

    def body(w_ref, o_ref):
        o_ref[...] = ((w_ref[0] + w_ref[1]) + w_ref[2]) + w_ref[3]

    return pl.pallas_call(
        body, name="small_chip_sum", grid=(r // tr,),
        in_specs=[pl.BlockSpec((N_CHIPS, tr, cw), lambda i: (0, i, 0))],
        out_specs=pl.BlockSpec((tr, cw), lambda i: (i, 0)),
        out_shape=jax.ShapeDtypeStruct((r, cw), F32),
        compiler_params=_params(("arbitrary",)),
    )(w)


def _adamw_math(w, g, m, v):
    m = ADAM_B1 * m + (1.0 - ADAM_B1) * g
    v = ADAM_B2 * v + (1.0 - ADAM_B2) * (g * g)
    m_hat = m / (1.0 - ADAM_B1 ** ADAM_STEP)
    v_hat = v / (1.0 - ADAM_B2 ** ADAM_STEP)
    delta = -ADAM_LR * (m_hat / (jnp.sqrt(v_hat) + ADAM_EPS) + ADAM_WD * w)
    return delta, m, v


def _adamw(name, w, m, v, g, g_half=0, g_row_off=0, tr=256):
    r, cw = w.shape
    tr = _tile(math.gcd(r, g_row_off) if g_row_off else r, tr)
    off = g_row_off // tr

    def body(w_ref, m_ref, v_ref, g_ref, go_ref, d_ref, mo_ref, vo_ref):
        g_v = g_ref[...]
        delta, m_n, v_n = _adamw_math(w_ref[...], g_v, m_ref[...], v_ref[...])
        go_ref[...] = g_v
        d_ref[...] = delta
        mo_ref[...] = m_n
        vo_ref[...] = v_n

    tile = pl.BlockSpec((tr, cw), lambda i: (i, 0))
    out = jax.ShapeDtypeStruct((r, cw), F32)
    return pl.pallas_call(
        body, name=name, grid=(r // tr,),
        in_specs=[tile, tile, tile, pl.BlockSpec((None, tr, cw), lambda i: (g_half, i + off, 0))],
        out_specs=[tile] * 4, out_shape=[out] * 4,
        compiler_params=_params(("arbitrary",)),
    )(w, m, v, g)


def kernel(x, mem, g_ffn1, w1_gate, w1_up, w1_down, g_mix, w_in, ssm_a_re, ssm_a_im, ssm_log_dt, ssm_b_re, ssm_b_im, ssm_c_re, ssm_c_im, ssm_d, w_glu, b_glu, w_pool, pool_scale, g_out_ssm, g_out_pool, w_out, g_xattn, g_mem, w_q, w_k, w_v, w_o, g_ffn2, w2_gate, w2_up, w2_down, g_final, loss_target, m_g_ffn1, m_w1_gate, m_w1_up, m_w1_down, m_g_mix, m_w_in, m_ssm_a_re, m_ssm_a_im, m_ssm_log_dt, m_ssm_b_re, m_ssm_b_im, m_ssm_c_re, m_ssm_c_im, m_ssm_d, m_w_glu, m_b_glu, m_w_pool, m_pool_scale, m_g_out_ssm, m_g_out_pool, m_w_out, m_g_xattn, m_g_mem, m_w_q, m_w_k, m_w_v, m_w_o, m_g_ffn2, m_w2_gate, m_w2_up, m_w2_down, m_g_final, v_g_ffn1, v_w1_gate, v_w1_up, v_w1_down, v_g_mix, v_w_in, v_ssm_a_re, v_ssm_a_im, v_ssm_log_dt, v_ssm_b_re, v_ssm_b_im, v_ssm_c_re, v_ssm_c_im, v_ssm_d, v_w_glu, v_b_glu, v_w_pool, v_pool_scale, v_g_out_ssm, v_g_out_pool, v_w_out, v_g_xattn, v_g_mem, v_w_q, v_w_k, v_w_v, v_w_o, v_g_ffn2, v_w2_gate, v_w2_up, v_w2_down, v_g_final):
    local = dict(locals())
    wts = {n: local[n] for n in WEIGHTS}
    mom = {n: local["m_" + n] for n in WEIGHTS}
    var = {n: local["v_" + n] for n in WEIGHTS}

    x2 = x[0]
    mem2 = mem[0]
    tgt = loss_target[0]
    t_rows, d = x2.shape
    fs = w1_gate.shape[-1]
    ds_ = w_in.shape[1]
    ws = d // 2
    n_pg = len(POOL_WINDOWS)
    pw = ws // n_pg
    n_grp = ws // SSM_GROUP
    n_state = ssm_a_re.shape[-1]
    cx_, cy_, cc_ = lax.axis_index("x"), lax.axis_index("y"), lax.axis_index("c")
    chip = (2 * cx_ + cy_).astype(jnp.int32)
    core = cc_.astype(jnp.int32).reshape(1)
    chip_idx = jnp.stack([chip, chip ^ 2, chip ^ 1, chip ^ 3, cc_.astype(jnp.int32)])

    glu_rows = w_glu[0].reshape(-1, d)
    pool_rows = w_pool[0].reshape(-1, d)
    n_glu, n_pool = glu_rows.shape[0], pool_rows.shape[0]
    PACKED = ['w_out', 'w_q', 'w_k', 'w_v', 'w_o']
    glu_at = len(PACKED) * ds_
    pool_at = glu_at + n_glu
    n_pad = -(pool_at + n_pool) % 32
    rp = pool_at + n_pool + n_pad

    def own_slot(src):
        src = src.astype(BF16)
        return lax.dynamic_update_slice(lax.empty((N_CHIPS,) + src.shape, BF16), src[None], (chip, 0, 0, 0))

    src_packed = jnp.concatenate([wts[n][0] for n in PACKED] + [glu_rows, pool_rows, jnp.zeros((n_pad, d), F32)], 0)
    src_up1 = jnp.stack([w1_gate[0], w1_up[0]]).astype(BF16)
    w_bufs = [own_slot(src_up1), own_slot(w1_down[0].reshape(2, fs // 2, d)),
              own_slot(w_in[0].reshape(2, ds_ // 2, d)), own_slot(src_packed.reshape(2, rp // 2, d)),
              own_slot(jnp.stack([w2_gate[0], w2_up[0]])), own_slot(w2_down[0].reshape(2, fs // 2, d))]
    near_send, near_recv, w0, ag_token = _comm_start("weights_start_near", ("ag_ici", (0, 1)), w_bufs[:1])

    def gathered(k, after):
        w = _comm_wait("weights_wait_%d" % k, "ag_ici", [w_bufs[k]], ag_send[3 * k:3 * k + 3],
                       ag_recv[3 * k:3 * k + 3], after)
        return _comm_fused("weights_forward_%d" % k, "ag_fwd", w)[0]

    def gathered_start(k, after):
        w = _comm_wait("weights_wait_%d" % k, "ag_ici", [w_bufs[k]], ag_send[3 * k:3 * k + 3],
                       ag_recv[3 * k:3 * k + 3], after)
        send, recv, thru, token = _comm_start("weights_forward_start_%d" % k, "ag_fwd", w)
        return (send, recv, thru), token

    def gathered_finish(k, handle, after):
        send, recv, thru = handle
        return _comm_wait("weights_forward_wait_%d" % k, "ag_fwd", thru, send, recv, after)[0]

    n1 = _rmsnorm("norm_ffn1", x2, wts['g_ffn1'].reshape(1, -1), deps=[ag_token])
    zero1 = jnp.zeros((1,), jnp.int32)
    near, far = jnp.stack([chip ^ 2, chip ^ 1]), (chip ^ 3).reshape(1)
    a1, b1, hm1 = _ffn_up("ffn1_up_own", n1, src_up1[None], zero1, chip.reshape(1), d, fs)
    n_rest = 3 * len(w_bufs) - 2
    rest_send, rest_recv, w_bufs, _ = _comm_start("weights_start_rest", ("ag_ici", tuple(range(2, 2 + n_rest))),
                                                  w0 + w_bufs[1:], after=hm1)
    ag_send, ag_recv = near_send + rest_send, near_recv + rest_recv
    w0 = _comm_wait("weights_wait_0_near", ("ag_ici", (0, 1)), [w_bufs[0]], ag_send[0:2], ag_recv[0:2], hm1)
    w0 = _comm_fused("weights_forward_0_near", ("ag_fwd", (0, 1)), w0)
    a1, b1, hm1 = _ffn_up("ffn1_up_near", n1, w0[0], near, near, d, fs, into=(a1, b1, hm1))
    w0 = _comm_wait("weights_wait_0_far", ("ag_ici", (2,)), w0, ag_send[2:3], ag_recv[2:3], hm1)
    ga1 = _comm_fused("weights_forward_0_far", ("ag_fwd", (2,)), w0)[0]
    a1, b1, hm1 = _ffn_up("ffn1_up_far", n1, ga1, far, far, d, fs, into=(a1, b1, hm1))
    gd1 = gathered(1, hm1).reshape(N_CHIPS, fs, d)
    h1 = _mm_nn("ffn1_down", hm1, gd1, (N_CHIPS, fs, d // 2), lambda j, k: (0, 0, j), 1, d, F32, res=x2, alpha=0.5,
                slabs=N_CHIPS)
    n2 = _rmsnorm("norm_mix", h1, wts['g_mix'].reshape(1, -1))
    dd_bufs = {'w_in': gathered(2, h1).reshape(N_CHIPS, ds_, d)}
    DD = {n: q for q, n in enumerate(PACKED)}
    DD['w_in'] = 0

    def mm_dd(name, a, wname, out_dtype, res=None, norm_g=None):
        q = DD[wname]
        return _mm_nn(name, a, dd_bufs[wname], (N_CHIPS, ds_, d), lambda j, k: (0, q, 0), 1, d, out_dtype, res=res,
                      norm_g=norm_g, slabs=N_CHIPS)

    def mm_dd_t(name, pairs, out_dtype, deps=()):
        ps = [(dy, dd_bufs[w], (N_CHIPS, ds_, d), functools.partial(lambda s, q: (0, q, 0), q=DD[w])) for dy, w in pairs]
        return _mm_nt_cols(name, ps, N_CHIPS, ds_, [out_dtype], deps=deps, slabs=N_CHIPS)[0]

    def vec(n):
        return wts[n].reshape(1, -1)

    disc_in = (ssm_a_re[0], ssm_a_im[0], ssm_log_dt[0], ssm_b_re[0], ssm_b_im[0])
    (abar_re, abar_im, bbar_re, bbar_im), disc_vjp = jax.vjp(_s5_discretize, *disc_in)
    gpb = min(S5_GROUPS_PER_BLOCK, n_grp)
    nb = n_grp // gpb
    cb = gpb * n_state
    eye = jnp.eye(gpb, dtype=F32)

    def blockdiag(t):
        return jnp.einsum('jgph,gk->jghkp', t.reshape(nb, gpb, n_state, SSM_GROUP), eye).reshape(nb, gpb * SSM_GROUP, cb)

    def blockdiag_c(t):
        return jnp.einsum('jghp,gk->jkpgh', t.reshape(nb, gpb, SSM_GROUP, n_state), eye).reshape(nb, cb, gpb * SSM_GROUP)

    bblk = jnp.concatenate([blockdiag(bbar_re), blockdiag(bbar_im)], -1).astype(BF16)
    cblk = jnp.concatenate([blockdiag_c(ssm_c_re[0]), -blockdiag_c(ssm_c_im[0])], 1).astype(BF16)
    ab = jnp.stack([abar_re.reshape(nb, cb), abar_im.reshape(nb, cb)], 1)

    u = mm_dd("mix_in", n2, 'w_in', F32)

    up = _perm_rows(u[:, :ws]).astype(BF16)
    ylin_p, s_all = _s5_fwd(up, bblk, ab, cblk)
    ylin = _unperm_rows(ylin_p)

    def gelu_fn(r, v):
        y1 = r[0] + v[0] * r[1]
        y2 = jax.nn.gelu(y1)
        return [y2, y2], []
    fwd_pk, tok = gathered_start(3, ylin_p)
    y2, y2b = _rowwise("s5_gelu", gelu_fn, [ylin, (u, 0, ws)], [vec('ssm_d')], [(ws, F32), (ws, BF16)], deps=[tok])
    packed = gathered_finish(3, fwd_pk, y2b).reshape(N_CHIPS, rp, d)
    for n in PACKED:
        dd_bufs[n] = packed
    wglu_full = packed[:, glu_at:glu_at + n_glu, :].reshape(ws, ws)
    wpool_full = packed[:, pool_at:pool_at + n_pool, :].reshape(N_CHIPS, n_pg, pw // N_CHIPS, pw)
    wpool_full = wpool_full.transpose(1, 0, 2, 3).reshape(n_pg, pw, pw)
    z = _mm_nn("s5_glu", y2b, wglu_full, (ws, ws), lambda j, k: (0, 0), 1, ws, F32)

    def glu_fn(r, v):
        y3 = r[0] * _sigmoid(r[1] + v[0])
        return [_rms_fwd(y3, v[1])], []
    m_ssm = _rowwise("s5_gate_norm", glu_fn, [y2, z], [vec('b_glu'), vec('g_out_ssm')], [(ws, BF16)])[0]

    pooled, zp = _pool_fwd(u, 1, wpool_full, vec('pool_scale'))
    fwd_a2, tok = gathered_start(4, zp)
    m_pool = _rmsnorm("norm_pool", zp, vec('g_out_pool'), deps=[tok])
    merged = jnp.concatenate([m_ssm, m_pool], -1)
    h2, hn = mm_dd("mix_out", merged, 'w_out', F32, res=h1, norm_g=vec('g_xattn'))

    q = mm_dd("attn_q", hn, 'w_q', BF16)
    fwd_d2, tok = gathered_start(5, q)
    memn = _rmsnorm("norm_mem", mem2, vec('g_mem'), deps=[tok])
    k_mem = mm_dd("attn_k", memn, 'w_k', BF16)
    v_mem = mm_dd("attn_v", memn, 'w_v', BF16)
    o = _attn_fwd(q, k_mem, v_mem)
    h3, n4 = mm_dd("attn_out", o, 'w_o', F32, res=h2, norm_g=vec('g_ffn2'))

    ga2 = gathered_finish(4, fwd_a2, h3)
    all_chips = jnp.arange(N_CHIPS, dtype=jnp.int32)
    a2, b2, hm2 = _ffn_up("ffn2_up", n4, ga2, all_chips, all_chips, d, fs)
    gd2 = gathered_finish(5, fwd_d2, hm2).reshape(N_CHIPS, fs, d)
    h4 = _mm_nn("ffn2_down", hm2, gd2, (N_CHIPS, fs, d // 2), lambda j, k: (0, 0, j), 1, d, F32, res=h3, alpha=0.5,
                slabs=N_CHIPS)

    def loss_fn(r, v):
        h, t = r
        e = _rms_fwd(h, v[0]) - t
        dy = e * (1.0 / d)
        dh, dg = _rms_bwd(dy, h, v[0])
        part = jnp.sum(_colsum(e * e), axis=1, keepdims=True) * (0.5 / d)
        return [dh, 0.5 * dh], [_colsum(dg), jnp.broadcast_to(part, (1, 128))]
    dh4, dy_f2, dg_final, loss_row = _rowwise("loss_head", loss_fn, [h4, tgt], [g_final.reshape(1, -1)],
                                              [(d, F32), (d, BF16)], [d, 128])

    def rs_pair_start(tag, gbufs, after=None):
        land = [lax.empty((N_CHIPS,) + g.shape[2:], BF16) for g in gbufs]
        send, recv, thru, token = _comm_start(tag + "_pair_start", "pair", list(gbufs) + land, after=after)
        return (send, recv, thru), token

    def rs_scatter_start(tag, handle, after):
        send, recv, thru = handle
        n = len(thru) // 2
        res = _comm_wait(tag + "_pair_wait", "pair", thru, send, recv, after)
        parts = [_pair_add(g, r, core) for g, r in zip(res[:n], res[n:])]
        land = [lax.empty(p.shape, BF16) for p in parts]
        send, recv, thru, token = _comm_start(tag + "_scatter_start", "scatter", parts + land)
        return (send, recv, thru), token

    def rs_half_start(tag, handle, after):
        send, recv, thru = handle
        n = len(thru) // 2
        res = _comm_wait(tag + "_scatter_wait", "scatter", thru, send, recv, after)
        full = [_chip_sum(p, g2, chip_idx) for p, g2 in zip(res[:n], res[n:])]
        send, recv, thru, token = _comm_start(tag + "_half_start", "half", full)
        return (send, recv, thru), token

    def rs_finish(tag, handle, after):
        send, recv, thru = handle
        return _comm_wait(tag + "_half_wait", "half", thru, send, recv, after)

    wblk = (None, None, d, fs)

    def ffn_down_bwd(tag, dy_half, a, b, hm, gd_l, deps=()):
        da, db = _mm_nt_cols(tag + "_down_bwd", [(dy_half, gd_l, (None, fs, d), lambda s: (s, 0, 0))],
                             N_CHIPS, fs, [BF16, BF16], epi=_swiglu_bwd, extras=[a, b], deps=deps, tm=1024, row_parts=4)
        g_down = _mm_tn(tag + "_dw_down", hm, dy_half, fs, d // 2, N_CHIPS, 2, jax.ShapeDtypeStruct((N_CHIPS, fs, d), BF16),
                        (None, fs, d // 2), lambda p, q: (p, 0, q), tt=2048)
        return da, db, g_down.reshape(N_CHIPS, 2, fs // 2, d)

    def ffn_up_bwd(tag, da, db, ga_l, deps=()):
        return _mm_nt_k(tag + "_up_bwd", [(da, ga_l, wblk, lambda s: (s, 0, 0, 0)), (db, ga_l, wblk, lambda s: (s, 1, 0, 0))],
                        N_CHIPS, d, BF16, deps=deps)

    def ffn_dw(name, dact, n_in, deps=()):
        return _mm_tn(name, n_in, dact, d // 2, fs, 2, N_CHIPS, jax.ShapeDtypeStruct((N_CHIPS, 2, d // 2, fs), BF16),
                      (None, None, d // 2, fs), lambda p, q: (q, p, 0, 0), tt=2048, deps=deps)

    def dw_dd(name, a, dy, wname, grad_b2):
        q = DD[wname]
        rows = ds_ if wname == 'w_in' else rp
        return _mm_tn(name, a, dy, ds_, d, N_CHIPS, 1, jax.ShapeDtypeStruct((N_CHIPS, rows, d), BF16),
                      (None, ds_, d), lambda p, qq: (p, q, 0), into=grad_b2, tt=2048)

    def norm_bwd(name, dn, h, gname, dres, deps=(), scale=1.0):
        def fn(r, v):
            dx, dg = _rms_bwd(r[0].astype(F32), r[1], v[0])
            tot = dx + r[2]
            return [tot, scale * tot], [_colsum(dg)]
        return _rowwise(name, fn, [dn, h, dres], [vec(gname)], [(d, F32), (d, BF16)], [d], deps=deps)

    da2, db2, g_down2 = ffn_down_bwd("ffn2", dy_f2, a2, b2, hm2, gd2)
    dn4 = ffn_up_bwd("ffn2", da2, db2, ga2)
    g_gate2 = ffn_dw("ffn2_dw_gate", da2, n4)
    g_up2 = ffn_dw("ffn2_dw_up", db2, n4)
    rs_f2, tok = rs_pair_start("ffn2", [g_gate2, g_up2, g_down2])
    dh3, dh3b, dg_ffn2 = norm_bwd("norm_ffn2_bwd", dn4, h3, 'g_ffn2', dh4, deps=[tok])
    rs_f2, tok = rs_scatter_start("ffn2", rs_f2, dh3b)

    do = mm_dd_t("attn_out_bwd", [(dh3b, 'w_o')], BF16, deps=[tok])
    grad_b2 = dw_dd("attn_dw_o", o, dh3b, 'w_o', None)
    dq, dk, dv = _attn_bwd(q, k_mem, v_mem, do)
    dkb, dvb = dk.astype(BF16), dv.astype(BF16)
    grad_b2 = dw_dd("attn_dw_q", hn, dq, 'w_q', grad_b2)
    dhn = mm_dd_t("attn_q_bwd", [(dq, 'w_q')], BF16)
    dh2, dh2b, dg_xattn = norm_bwd("norm_xattn_bwd", dhn, h2, 'g_xattn', dh3)
    grad_b2 = dw_dd("attn_dw_k", memn, dkb, 'w_k', grad_b2)
    grad_b2 = dw_dd("attn_dw_v", memn, dvb, 'w_v', grad_b2)
    dmemn = mm_dd_t("attn_kv_bwd", [(dkb, 'w_k'), (dvb, 'w_v')], F32)
    dg_mem = _rowwise("norm_mem_bwd", lambda r, v: ([], [_colsum(_rms_bwd(r[0], r[1], v[0])[1])]),
                      [dmemn, mem2], [vec('g_mem')], [], [d])[0]

    dmerged = mm_dd_t("mix_out_bwd", [(dh2b, 'w_out')], F32)
    grad_b2 = dw_dd("mix_dw_out", merged, dh2b, 'w_out', grad_b2)

    def gate_bwd_fn(r, v):
        dm, y2_v, z_v = r
        sg = _sigmoid(z_v + v[0])
        y3 = y2_v * sg
        dy3, dg = _rms_bwd(dm, y3, v[1])
        dz = dy3 * y3 * (1.0 - sg)
        return [dy3 * sg, dz], [_colsum(dg), _colsum(dz)]
    dy2a, dzb, dg_out_ssm, db_glu = _rowwise("s5_gate_norm_bwd", gate_bwd_fn, [(dmerged, 0, ws), y2, z],
                                             [vec('b_glu'), vec('g_out_ssm')], [(ws, F32), (ws, BF16)], [ws, ws])
    dy2b_ = _mm_nt_cols("s5_glu_bwd", [(dzb, wglu_full, (ws, ws), lambda s: (0, 0))], 1, ws, [F32])[0]
    dw_glu = _mm_tn("s5_dw_glu", y2b, dzb, ws, ws, 1, 1, jax.ShapeDtypeStruct((ws, ws), F32), (ws, ws), lambda p, q: (0, 0))

    def gelu_bwd_fn(r, v):
        dy2 = r[0] + r[1]
        us = r[3]
        y1 = r[2] + v[0] * us
        kk = math.sqrt(2.0 / math.pi)
        th = jnp.tanh(kk * (y1 + 0.044715 * y1 * y1 * y1))
        dgelu = 0.5 * (1.0 + th) + 0.5 * y1 * (1.0 - th * th) * kk * (1.0 + 3.0 * 0.044715 * y1 * y1)
        dy1 = dy2 * dgelu
        return [dy1, dy1 * v[0]], [_colsum(dy1 * us)]
    dy1b, du_skip, d_ssm_d = _rowwise("s5_gelu_bwd", gelu_bwd_fn, [dy2a, dy2b_, ylin, (u, 0, ws)], [vec('ssm_d')],
                                      [(ws, BF16), (ws, F32)], [ws])

    bblk_t = jnp.swapaxes(bblk, 1, 2)
    cblk_t = jnp.swapaxes(cblk, 1, 2)
    du_p, d_bblk, d_cblk_t, d_ab = _s5_bwd(_perm_rows(dy1b), up, s_all, bblk_t, ab, cblk_t)
    du_ssm = _unperm_rows(du_p)

    dzp, dg_out_pool = _rowwise("norm_pool_bwd", lambda r, v: (lambda dx, dg: ([dx], [_colsum(dg)]))(*_rms_bwd(r[0], r[1], v[0])),
                                [(dmerged, 1, ws), zp], [vec('g_out_pool')], [(ws, F32)], [ws])
    dps, dw_pool, d_pool_scale = _pool_bwd1(dzp, pooled, wpool_full, vec('pool_scale'))
    du_pool = _pool_bwd2(dps, n_pg)

    dub = _rowwise("mix_du", lambda r, v: ([jnp.concatenate([r[0] + r[1], r[2]], -1)], []),
                   [du_ssm, du_skip, du_pool], [], [(d, BF16)])[0]
    dn2 = mm_dd_t("mix_in_bwd", [(dub, 'w_in')], BF16)
    grad_in = dw_dd("mix_dw_in", n2, dub, 'w_in', None)
    tail_g = jnp.concatenate([
        dw_glu.reshape(N_CHIPS, n_glu, d),
        dw_pool.reshape(n_pg, N_CHIPS, pw // N_CHIPS, pw).transpose(1, 0, 2, 3).reshape(N_CHIPS, n_pool, d),
        jnp.zeros((N_CHIPS, n_pad, d), F32)], 1).astype(BF16)
    grad_b2 = lax.dynamic_update_slice(grad_b2, tail_g, (0, glu_at, 0))
    rs_mix, tok = rs_pair_start("mixers", [grad_b2.reshape(N_CHIPS, 2, rp // 2, d),
                                           grad_in.reshape(N_CHIPS, 2, ds_ // 2, d)])
    dh1, dy_f1, dg_mix = norm_bwd("norm_mix_bwd", dn2, h1, 'g_mix', dh2, deps=[tok], scale=0.5)
    rs_mix, tok = rs_scatter_start("mixers", rs_mix, dy_f1)

    da1, db1, g_down1 = ffn_down_bwd("ffn1", dy_f1, a1, b1, hm1, gd1, deps=[tok])
    rs_d1, tok = rs_pair_start("ffn1_down", [g_down1])
    dn1 = ffn_up_bwd("ffn1", da1, db1, ga1, deps=[tok])
    rs_d1, tok = rs_scatter_start("ffn1_down", rs_d1, dn1)
    grad_x, _, dg_ffn1 = norm_bwd("norm_ffn1_bwd", dn1, x2, 'g_ffn1', dh1, deps=[tok])

    def undiag(t):
        return jnp.einsum('jghkp,gk->jgph', t.reshape(nb, gpb, SSM_GROUP, gpb, n_state), eye).reshape(n_grp, n_state, SSM_GROUP)

    d_bbar_re, d_bbar_im = undiag(d_bblk[:, :, :cb]), undiag(d_bblk[:, :, cb:])
    d_c_re = undiag(d_cblk_t[:, :, :cb]).transpose(0, 2, 1)
    d_c_im = -undiag(d_cblk_t[:, :, cb:]).transpose(0, 2, 1)
    d_abar = jnp.sum(d_ab, axis=2).reshape(nb, 2, gpb, n_state)
    d_abar_re = d_abar[:, 0].reshape(n_grp, n_state)
    d_abar_im = d_abar[:, 1].reshape(n_grp, n_state)
    d_a_re, d_a_im, d_log_dt, d_b_re, d_b_im = disc_vjp((d_abar_re, d_abar_im, d_bbar_re, d_bbar_im))

    small_g = {'g_ffn1': dg_ffn1, 'g_mix': dg_mix, 'ssm_a_re': d_a_re, 'ssm_a_im': d_a_im, 'ssm_log_dt': d_log_dt,
               'ssm_b_re': d_b_re, 'ssm_b_im': d_b_im, 'ssm_c_re': d_c_re, 'ssm_c_im': d_c_im, 'ssm_d': d_ssm_d,
               'b_glu': db_glu, 'pool_scale': d_pool_scale, 'g_out_ssm': dg_out_ssm, 'g_out_pool': dg_out_pool,
               'g_xattn': dg_xattn, 'g_mem': dg_mem, 'g_ffn2': dg_ffn2, 'g_final': dg_final}
    sizes = [wts[n].size for n in SMALL]
    total = sum(sizes) + 128
    rows_s = -(-total // (128 * 256)) * 256
    flat = jnp.concatenate([small_g[n].reshape(-1) for n in SMALL] + [loss_row.reshape(-1)])
    flat = jnp.pad(flat, (0, rows_s * 128 - total)).reshape(rows_s, 128)
    sw_send, sw_recv, sw_thru, tok = _comm_start("small_swap_start", "swap", [flat, lax.empty(flat.shape, F32)])
    g_gate1 = ffn_dw("ffn1_dw_gate", da1, n1, deps=[tok])
    rs_g1, tok_g1 = rs_pair_start("ffn1_gate", [g_gate1])
    sw_v, sw_got = _comm_wait("small_swap_wait", "swap", sw_thru, sw_send, sw_recv, tok_g1)
    slots = _add_into_slot(sw_v, sw_got, chip.reshape(1))
    bc_send, bc_recv, bc_thru, tok = _comm_start("small_bcast_start", "bcast", [slots])
    g_up1 = ffn_dw("ffn1_dw_up", db1, n1, deps=[tok])
    rs_g1, tok = rs_scatter_start("ffn1_gate", rs_g1, g_up1)
    slots, = _comm_wait("small_bcast_wait", "bcast", bc_thru, bc_send, bc_recv, tok)
    red = _sum_slots(slots).reshape(-1)
    loss = red[sum(sizes)]

    def flat_small(t):
        return jnp.pad(jnp.concatenate([t[n].reshape(-1) for n in SMALL]), (0, rows_s * 128 - sum(sizes))).reshape(rows_s, 128)
    sg_, sd_, sm_, sv_ = _adamw("adamw_small", flat_small(wts), flat_small(mom), flat_small(var), red.reshape(1, rows_s, 128))
    out = {}
    off = 0
    for n, sz in zip(SMALL, sizes):
        for key, arr in (('grad', sg_), ('delta', sd_), ('m', sm_), ('v', sv_)):
            out[key, n] = arr.reshape(-1)[off:off + sz].reshape(wts[n].shape)
        off += sz

    def upd(n, g_arr, half, row_off, shape2):
        res = _adamw("adamw_" + n, wts[n].reshape(shape2), mom[n].reshape(shape2), var[n].reshape(shape2), g_arr, half, row_off)
        for key, arr in zip(('grad', 'delta', 'm', 'v'), res):
            out[key, n] = arr.reshape(wts[n].shape)
        return res[3]

    rs_u1, tok = rs_pair_start("ffn1_up", [g_up1], after=sv_)
    rs_f2, tok = rs_half_start("ffn2", rs_f2, tok)
    rs_u1, tok = rs_scatter_start("ffn1_up", rs_u1, tok)
    rs_mix, tok = rs_half_start("mixers", rs_mix, tok)
    full_gate2, full_up2, full_down2 = rs_finish("ffn2", rs_f2, tok)
    upd('w2_gate', full_gate2.reshape(1, d, fs), 0, 0, (d, fs))
    upd('w2_up', full_up2.reshape(1, d, fs), 0, 0, (d, fs))
    last = upd('w2_down', full_down2.reshape(1, fs, d), 0, 0, (fs, d))
    rs_d1, tok = rs_half_start("ffn1_down", rs_d1, last)
    full_b2, full_in = rs_finish("mixers", rs_mix, tok)
    full_b2 = full_b2.reshape(1, rp, d)
    last = upd('w_in', full_in.reshape(1, ds_, d), 0, 0, (ds_, d))
    for n in PACKED:
        last = upd(n, full_b2, 0, DD[n] * ds_, (ds_, d))
    glu_shape, pool_shape = (ws // N_CHIPS, ws), (n_pg * pw // N_CHIPS, pw)
    upd('w_glu', full_b2[:, glu_at:glu_at + n_glu].reshape((1,) + glu_shape), 0, 0, glu_shape)
    upd('w_pool', full_b2[:, pool_at:pool_at + n_pool].reshape((1,) + pool_shape), 0, 0, pool_shape)
    full_down1, = rs_finish("ffn1_down", rs_d1, last)
    last = upd('w1_down', full_down1.reshape(1, fs, d), 0, 0, (fs, d))
    rs_g1, tok = rs_half_start("ffn1_gate", rs_g1, last)
    rs_u1, tok = rs_half_start("ffn1_up", rs_u1, tok)
    full_gate1, = rs_finish("ffn1_gate", rs_g1, tok)
    last = upd('w1_gate', full_gate1.reshape(1, d, fs), 0, 0, (d, fs))
    full_up1, = rs_finish("ffn1_up", rs_u1, last)
    upd('w1_up', full_up1.reshape(1, d, fs), 0, 0, (d, fs))

    return (loss, grad_x[None], *[out['grad', n] for n in WEIGHTS], *[out['delta', n] for n in WEIGHTS],
            *[out['m', n] for n in WEIGHTS], *[out['v', n] for n in WEIGHTS])
```

```python
import functools
import math

import jax
import jax.numpy as jnp
from jax import lax
from jax.experimental import pallas as pl
from jax.experimental.pallas import tpu as pltpu

F32 = jnp.float32
BF16 = jnp.bfloat16
EPS = 1e-6
ADAM_LR, ADAM_B1, ADAM_B2, ADAM_EPS, ADAM_WD, ADAM_STEP = 0.001, 0.9, 0.999, 1e-08, 0.01, 10
POOL_WINDOWS = (2, 4, 8, 16)
SSM_GROUP = 16
S5_GROUPS_PER_BLOCK = 16
S5_LANES = 8
MEM_HEADS = 4
N_CHIPS = 4
VMEM_LIMIT_V7X = 56 * 1024 * 1024
MESH = pl.DeviceIdType.MESH

WEIGHTS = ['g_ffn1', 'w1_gate', 'w1_up', 'w1_down', 'g_mix', 'w_in', 'ssm_a_re', 'ssm_a_im', 'ssm_log_dt',
           'ssm_b_re', 'ssm_b_im', 'ssm_c_re', 'ssm_c_im', 'ssm_d', 'w_glu', 'b_glu', 'w_pool', 'pool_scale',
           'g_out_ssm', 'g_out_pool', 'w_out', 'g_xattn', 'g_mem', 'w_q', 'w_k', 'w_v', 'w_o', 'g_ffn2',
           'w2_gate', 'w2_up', 'w2_down', 'g_final']
BIG = ['w1_gate', 'w1_up', 'w1_down', 'w_in', 'w_glu', 'w_pool', 'w_out', 'w_q', 'w_k', 'w_v', 'w_o',
       'w2_gate', 'w2_up', 'w2_down']
SMALL = [n for n in WEIGHTS if n not in BIG]


def _tile(n, target, mult=8):
    best = None
    for d in range(1, n + 1):
        if n % d == 0 and d <= target and d % mult == 0:
            best = d
    return best if best is not None else n


def _params(sem=None):
    if sem is None:
        return pltpu.CompilerParams(vmem_limit_bytes=VMEM_LIMIT_V7X)
    return pltpu.CompilerParams(dimension_semantics=sem, vmem_limit_bytes=VMEM_LIMIT_V7X)


def _sigmoid(x):
    return 1.0 / (1.0 + jnp.exp(-x))


def _sigmoid_approx(x):
    return pl.reciprocal(1.0 + jnp.exp(-x), approx=True)


def _rms_fwd(x, g):
    r = lax.rsqrt(jnp.mean(x * x, axis=-1, keepdims=True) + EPS)
    return x * r * g


def _rms_bwd(dy, x, g):
    r = lax.rsqrt(jnp.mean(x * x, axis=-1, keepdims=True) + EPS)
    dxh = dy * g
    dx = r * dxh - x * (r * r * r) * jnp.mean(dxh * x, axis=-1, keepdims=True)
    return dx, dy * x * r


def _colsum(v):
    return jnp.sum(v, axis=0, keepdims=True)


def _rowwise(name, fn, rows, vecs, out_defs, red_defs=(), tm=256, deps=()):
    rows = [r if isinstance(r, tuple) else (r, 0, r.shape[1]) for r in rows]
    t_rows = rows[0][0].shape[0]
    tm = _tile(t_rows, tm)
    nr, nv, no, nd, nx = len(rows), len(vecs), len(out_defs), len(red_defs), len(deps)

    def body(*refs):
        r, v = refs[:nr], refs[nr:nr + nv]
        o, d = refs[nr + nv + nx:nr + nv + nx + no], refs[nr + nv + nx + no:]
        outs, reds = fn([x[...] for x in r], [x[...] for x in v])
        for ref, val in zip(o, outs):
            ref[...] = val.astype(ref.dtype)
        if nd:
            @pl.when(pl.program_id(0) == 0)
            def _():
                for ref in d:
                    ref[...] = jnp.zeros(ref.shape, ref.dtype)
            for ref, val in zip(d, reds):
                ref[...] += val

    in_specs = [pl.BlockSpec((tm, w), functools.partial(lambda i, cb: (i, cb), cb=cb)) for (_, cb, w) in rows]
    in_specs += [pl.BlockSpec(v.shape, lambda i: (0, 0)) for v in vecs]
    in_specs += [pl.BlockSpec(memory_space=pl.ANY)] * nx
    out_specs = [pl.BlockSpec((tm, w), lambda i: (i, 0)) for (w, _) in out_defs]
    out_specs += [pl.BlockSpec((1, w), lambda i: (0, 0)) for w in red_defs]
    out_shape = [jax.ShapeDtypeStruct((t_rows, w), dt) for (w, dt) in out_defs]
    out_shape += [jax.ShapeDtypeStruct((1, w), F32) for w in red_defs]
    res = pl.pallas_call(
        body, name=name, grid=(t_rows // tm,), in_specs=in_specs, out_specs=out_specs, out_shape=out_shape,
        compiler_params=_params(("arbitrary",)),
    )(*[r[0] for r in rows], *vecs, *deps)
    return res


def _rmsnorm(name, x, g, tm=256, deps=()):
    return _rowwise(name, lambda r, v: ([_rms_fwd(r[0].astype(F32), v[0])], []), [x], [g],
                    [(x.shape[1], BF16)], tm=tm, deps=deps)[0]


def _mm_nn(name, a, b, b_block, b_idx, nk, n_out, out_dtype, res=None, alpha=1.0, norm_g=None, tm=512, slabs=1):
    t_rows = a.shape[0]
    bk, tn = slabs * b_block[-2], b_block[-1]
    tm = _tile(t_rows, tm)
    nj = n_out // tn
    has_res = res is not None
    has_norm = norm_g is not None
    assert not has_norm or nj == 1

    def body(*refs):
        a_ref, b_ref = refs[0], refs[1]
        res_ref = refs[2] if has_res else None
        g_ref = refs[2 + has_res] if has_norm else None
        o_ref = refs[2 + has_res + has_norm]
        n_ref = refs[3 + has_res + has_norm] if has_norm else None
        k = pl.program_id(2)
        w = b_ref[...].reshape(bk, tn) if slabs > 1 else b_ref[...]
        p = jnp.dot(a_ref[...], w, preferred_element_type=F32)

        def finish(r):
            if has_res:
                r = res_ref[...] + alpha * r
            o_ref[...] = r.astype(o_ref.dtype)
            if has_norm:
                n_ref[...] = _rms_fwd(r, g_ref[...]).astype(n_ref.dtype)

        if nk == 1:
            finish(p)
            return
        acc_ref = refs[3 + has_res + 2 * has_norm]

        @pl.when(k == 0)
        def _():
            acc_ref[...] = p

        @pl.when(k > 0)
        def _():
            acc_ref[...] += p

        @pl.when(k == nk - 1)
        def _():
            finish(acc_ref[...])

    in_specs = [pl.BlockSpec((tm, bk), lambda j, i, k: (i, k)),
                pl.BlockSpec(b_block, lambda j, i, k: b_idx(j, k))]
    args = [a, b]
    if has_res:
        in_specs.append(pl.BlockSpec((tm, tn), lambda j, i, k: (i, j)))
        args.append(res)
    tile = pl.BlockSpec((tm, tn), lambda j, i, k: (i, j))
    out_specs, out_shape = tile, jax.ShapeDtypeStruct((t_rows, n_out), out_dtype)
    if has_norm:
        in_specs.append(pl.BlockSpec((1, n_out), lambda j, i, k: (0, 0)))
        args.append(norm_g)
        out_specs, out_shape = [tile, tile], [out_shape, jax.ShapeDtypeStruct((t_rows, n_out), BF16)]
    return pl.pallas_call(
        body, name=name, grid=(nj, t_rows // tm, nk), in_specs=in_specs, out_specs=out_specs, out_shape=out_shape,
        scratch_shapes=[pltpu.VMEM((tm, tn), F32)] if nk > 1 else [],
        compiler_params=_params(("arbitrary", "arbitrary", "arbitrary")),
    )(*args)


def _dot_nt(x, w):
    return lax.dot_general(x, w, (((1,), (1,)), ((), ())), preferred_element_type=F32)


def _dot_tn(x, y):
    return lax.dot_general(x, y, (((0,), (0,)), ((), ())), preferred_element_type=F32)


def _mm_nt_cols(name, pairs, ns, bn, out_defs, epi=None, extras=(), tm=512, deps=(), slabs=1, row_parts=1):
    t_rows = pairs[0][0].shape[0]
    tm = _tile(t_rows, tm)
    npair, nex, no, nx = len(pairs), len(extras), len(out_defs), len(deps)
    ns, bn = ns // slabs, bn * slabs
    rp = tm // row_parts

    def body(*refs):
        ws = [refs[2 * p + 1][...] for p in range(npair)]
        if slabs > 1:
            ws = [w.reshape(bn, w.shape[-1]) for w in ws]
        for part_i in range(row_parts):
            rows = slice(part_i * rp, (part_i + 1) * rp)
            acc = None
            for p in range(npair):
                part = _dot_nt(refs[2 * p][rows, :], ws[p])
                acc = part if acc is None else acc + part
            ex = [r[rows, :] for r in refs[2 * npair:2 * npair + nex]]
            outs = epi(acc, *ex) if epi is not None else (acc,)
            for ref, val in zip(refs[2 * npair + nex + nx:], outs):
                ref[rows, :] = val.astype(ref.dtype)

    in_specs, args = [], []
    for (dy, w, w_block, w_idx) in pairs:
        in_specs.append(pl.BlockSpec((tm, dy.shape[1]), lambda s, i: (i, 0)))
        in_specs.append(pl.BlockSpec(w_block, functools.partial(lambda s, i, f: f(s), f=w_idx)))
        args += [dy, w]
    for e in extras:
        in_specs.append(pl.BlockSpec((tm, bn), lambda s, i: (i, s)))
        args.append(e)
    in_specs += [pl.BlockSpec(memory_space=pl.ANY)] * nx
    args += list(deps)
    res = pl.pallas_call(
        body, name=name, grid=(ns, t_rows // tm), in_specs=in_specs,
        out_specs=[pl.BlockSpec((tm, bn), lambda s, i: (i, s)) for _ in range(no)],
        out_shape=[jax.ShapeDtypeStruct((t_rows, ns * bn), dt) for dt in out_defs],
        compiler_params=_params(("arbitrary", "arbitrary")),
    )(*args)
    return res


def _mm_nt_k(name, pairs, ns, n_out, out_dtype, tm=512, deps=()):
    t_rows = pairs[0][0].shape[0]
    tm = _tile(t_rows, tm)
    npair, nx = len(pairs), len(deps)

    def body(*refs):
        o_ref, acc_ref = refs[2 * npair + nx], refs[2 * npair + nx + 1]
        s = pl.program_id(1)
        acc = None
        for p in range(npair):
            part = _dot_nt(refs[2 * p][...], refs[2 * p + 1][...])
            acc = part if acc is None else acc + part

        @pl.when(s == 0)
        def _():
            acc_ref[...] = acc

        @pl.when(s > 0)
        def _():
            acc_ref[...] += acc

        @pl.when(s == ns - 1)
        def _():
            o_ref[...] = acc_ref[...].astype(o_ref.dtype)

    in_specs, args = [], []
    for (a, w, w_block, w_idx) in pairs:
        in_specs.append(pl.BlockSpec((tm, w_block[-1]), lambda i, s: (i, s)))
        in_specs.append(pl.BlockSpec(w_block, functools.partial(lambda i, s, f: f(s), f=w_idx)))
        args += [a, w]
    in_specs += [pl.BlockSpec(memory_space=pl.ANY)] * nx
    args += list(deps)
    return pl.pallas_call(
        body, name=name, grid=(t_rows // tm, ns), in_specs=in_specs,
        out_specs=pl.BlockSpec((tm, n_out), lambda i, s: (i, 0)),
        out_shape=jax.ShapeDtypeStruct((t_rows, n_out), out_dtype),
        scratch_shapes=[pltpu.VMEM((tm, n_out), F32)],
        compiler_params=_params(("arbitrary", "arbitrary")),
    )(*args)


def _mm_tn(name, a, b, bk, bn, n_p, n_q, out_shape, out_block, out_idx, into=None, a_off=0, b_off=0, tt=512,
           deps=()):
    t_rows = a.shape[0]
    tt = _tile(t_rows, tt, 16)
    nt = t_rows // tt
    has_into = into is not None
    nx = len(deps)

    def body(*refs):
        a_ref, b_ref = refs[0], refs[1]
        o_ref, acc_ref = refs[2 + has_into + nx], refs[3 + has_into + nx]
        t = pl.program_id(2)
        part = _dot_tn(a_ref[...], b_ref[...])

        @pl.when(t == 0)
        def _():
            acc_ref[...] = part

        @pl.when(t > 0)
        def _():
            acc_ref[...] += part

        @pl.when(t == nt - 1)
        def _():
            o_ref[...] = acc_ref[...].astype(o_ref.dtype)

    in_specs = [pl.BlockSpec((tt, bk), lambda p, q, t: (t, p + a_off)),
                pl.BlockSpec((tt, bn), lambda p, q, t: (t, q + b_off))]
    args = [a, b]
    aliases = {}
    if has_into:
        in_specs.append(pl.BlockSpec(memory_space=pl.ANY))
        args.append(into)
        aliases = {2: 0}
        out_shape = jax.ShapeDtypeStruct(into.shape, into.dtype)
    in_specs += [pl.BlockSpec(memory_space=pl.ANY)] * nx
    args += list(deps)
    return pl.pallas_call(
        body, name=name, grid=(n_p, n_q, nt), in_specs=in_specs,
        out_specs=pl.BlockSpec(out_block, lambda p, q, t: out_idx(p, q)),
        out_shape=out_shape, scratch_shapes=[pltpu.VMEM((bk, bn), F32)],
        input_output_aliases=aliases,
        compiler_params=_params(("arbitrary", "arbitrary", "arbitrary")),
    )(*args)


def _ffn_up(name, n, ga, slots, cols, d_model, fs, into=None, tm=1024):
    t_rows = n.shape[0]
    tm = _tile(t_rows, tm)
    n_sh = slots.shape[0]
    has_into = into is not None

    row_parts = 4 if tm % 64 == 0 else 1

    def body(slot_ref, col_ref, n_ref, wg_ref, wu_ref, *refs):
        a_ref, b_ref, h_ref = refs[3 * has_into:]
        wg, wu = wg_ref[...], wu_ref[...]
        for part in range(row_parts):
            rows = slice(part * (tm // row_parts), (part + 1) * (tm // row_parts))
            x = n_ref[rows, :]
            a = jnp.dot(x, wg, preferred_element_type=F32)
            b = jnp.dot(x, wu, preferred_element_type=F32)
            a_ref[rows, :] = a.astype(a_ref.dtype)
            b_ref[rows, :] = b.astype(b_ref.dtype)
            h_ref[rows, :] = (a * _sigmoid_approx(a) * b).astype(h_ref.dtype)

    w_block = (None, None, d_model, fs)
    out = jax.ShapeDtypeStruct((t_rows, N_CHIPS * fs), BF16)
    in_specs = [pl.BlockSpec((tm, d_model), lambda s, i, sl, co: (i, 0)),
                pl.BlockSpec(w_block, lambda s, i, sl, co: (sl[s], 0, 0, 0)),
                pl.BlockSpec(w_block, lambda s, i, sl, co: (sl[s], 1, 0, 0))]
    args = [slots, cols, n, ga, ga]
    aliases = {}
    if has_into:
        in_specs += [pl.BlockSpec(memory_space=pl.ANY)] * 3
        args += list(into)
        aliases = {5: 0, 6: 1, 7: 2}
    return pl.pallas_call(
        body, name=name,
        grid_spec=pltpu.PrefetchScalarGridSpec(
            num_scalar_prefetch=2, grid=(n_sh, t_rows // tm), in_specs=in_specs,
            out_specs=[pl.BlockSpec((tm, fs), lambda s, i, sl, co: (i, co[s]))] * 3),
        out_shape=[out, out, out], input_output_aliases=aliases,
        compiler_params=_params(("arbitrary", "arbitrary")),
    )(*args)


def _swiglu_bwd(dh, a, b):
    a = a.astype(F32)
    b = b.astype(F32)
    sg = _sigmoid_approx(a)
    return dh * b * sg * (1.0 + a * (1.0 - sg)), dh * a * sg


def _attn_fwd(q, k, v, tm=512):
    t_rows, d_model = q.shape
    n_mem = k.shape[0]
    hd = d_model // MEM_HEADS
    scale = hd ** -0.5
    tm = _tile(t_rows, tm)

    def body(q_ref, k_ref, v_ref, o_ref):
        for h in range(MEM_HEADS):
            cols = slice(h * hd, (h + 1) * hd)
            s = _dot_nt(q_ref[:, cols], k_ref[:, cols]) * scale
            s = s - jnp.max(s, axis=-1, keepdims=True)
            e = jnp.exp(s)
            p = e / jnp.sum(e, axis=-1, keepdims=True)
            o_ref[:, cols] = jnp.dot(p.astype(BF16), v_ref[:, cols], preferred_element_type=F32).astype(o_ref.dtype)

    return pl.pallas_call(
        body, name="attn_fwd", grid=(t_rows // tm,),
        in_specs=[pl.BlockSpec((tm, d_model), lambda i: (i, 0)),
                  pl.BlockSpec((n_mem, d_model), lambda i: (0, 0)),
                  pl.BlockSpec((n_mem, d_model), lambda i: (0, 0))],
        out_specs=pl.BlockSpec((tm, d_model), lambda i: (i, 0)),
        out_shape=jax.ShapeDtypeStruct((t_rows, d_model), BF16),
        compiler_params=_params(("arbitrary",)),
    )(q, k, v)


def _attn_bwd(q, k, v, do, tm=512):
    t_rows, d_model = q.shape
    n_mem = k.shape[0]
    hd = d_model // MEM_HEADS
    scale = hd ** -0.5
    tm = _tile(t_rows, tm, 16)

    def body(q_ref, k_ref, v_ref, do_ref, dq_ref, dk_ref, dv_ref):
        @pl.when(pl.program_id(0) == 0)
        def _():
            dk_ref[...] = jnp.zeros(dk_ref.shape, F32)
            dv_ref[...] = jnp.zeros(dv_ref.shape, F32)

        for h in range(MEM_HEADS):
            cols = slice(h * hd, (h + 1) * hd)
            qh, kh, vh, doh = q_ref[:, cols], k_ref[:, cols], v_ref[:, cols], do_ref[:, cols]
            s = _dot_nt(qh, kh) * scale
            s = s - jnp.max(s, axis=-1, keepdims=True)
            e = jnp.exp(s)
            p = e / jnp.sum(e, axis=-1, keepdims=True)
            dv_ref[:, cols] += _dot_tn(p.astype(BF16), doh)
            dp = _dot_nt(doh, vh)
            ds = (p * (dp - jnp.sum(dp * p, axis=-1, keepdims=True)) * scale).astype(BF16)
            dq_ref[:, cols] = jnp.dot(ds, kh, preferred_element_type=F32).astype(dq_ref.dtype)
            dk_ref[:, cols] += _dot_tn(ds, qh)

    full = pl.BlockSpec((n_mem, d_model), lambda i: (0, 0))
    tile = pl.BlockSpec((tm, d_model), lambda i: (i, 0))
    return pl.pallas_call(
        body, name="attn_bwd", grid=(t_rows // tm,),
        in_specs=[tile, full, full, tile], out_specs=[tile, full, full],
        out_shape=[jax.ShapeDtypeStruct((t_rows, d_model), BF16),
                   jax.ShapeDtypeStruct((n_mem, d_model), F32), jax.ShapeDtypeStruct((n_mem, d_model), F32)],
        compiler_params=_params(("arbitrary",)),
    )(q, k, v, do)


def _split_bf16(v):
    hi = v.astype(BF16)
    return hi, (v - hi.astype(F32)).astype(BF16)


def _pool_fwd(u, col_blk, w_pool, scale, tt=256):
    t_rows = u.shape[0]
    ng, pw = w_pool.shape[0], w_pool.shape[-1]
    width = ng * pw
    tt = _tile(t_rows, tt, 16)
    nt = t_rows // tt
    assert len(POOL_WINDOWS) == ng and tt >= max(POOL_WINDOWS)

    def body(vc_ref, vp_ref, w_ref, sc_ref, pooled_ref, z_ref):
        i = pl.program_id(0)
        r = lax.broadcasted_iota(jnp.int32, (tt, tt), 0)
        c = lax.broadcasted_iota(jnp.int32, (tt, tt), 1)
        t = i * tt + lax.broadcasted_iota(jnp.int32, (tt, 1), 0)
        first = (i > 0).astype(F32)
        for g, w in enumerate(POOL_WINDOWS):
            cols = slice(g * pw, (g + 1) * pw)
            band_c = ((c <= r) & (c > r - w)).astype(BF16)
            band_p = (c > r - w + tt).astype(BF16)
            vc = vc_ref[:, cols]
            ch, cl = _split_bf16(vc)
            ph, plo = _split_bf16(vp_ref[:, cols] * first)
            sums = (jnp.dot(band_c, ch, preferred_element_type=F32) + jnp.dot(band_c, cl, preferred_element_type=F32)
                    + jnp.dot(band_p, ph, preferred_element_type=F32) + jnp.dot(band_p, plo, preferred_element_type=F32))
            cnt = jnp.minimum(t + 1, w).astype(F32)
            pooled = (sums / cnt - vc).astype(BF16)
            pooled_ref[:, cols] = pooled
            z_ref[:, cols] = jnp.dot(pooled, w_ref[g], preferred_element_type=F32) * sc_ref[:, cols]

    tile = pl.BlockSpec((tt, width), lambda i: (i, 0))
    return pl.pallas_call(
        body, name="pool_fwd", grid=(nt,),
        in_specs=[pl.BlockSpec((tt, width), lambda i: (i, col_blk)),
                  pl.BlockSpec((tt, width), lambda i: (jnp.maximum(i - 1, 0), col_blk)),
                  pl.BlockSpec((ng, pw, pw), lambda i: (0, 0, 0)),
                  pl.BlockSpec((1, width), lambda i: (0, 0))],
        out_specs=[tile, tile],
        out_shape=[jax.ShapeDtypeStruct((t_rows, width), BF16), jax.ShapeDtypeStruct((t_rows, width), F32)],
        compiler_params=_params(("arbitrary",)),
    )(u, u, w_pool, scale)


def _pool_bwd1(dz, pooled, w_pool, scale, tt=256):
    t_rows = dz.shape[0]
    ng, pw = w_pool.shape[0], w_pool.shape[-1]
    width = ng * pw
    tt = _tile(t_rows, tt, 16)
    nt = t_rows // tt

    def body(dz_ref, p_ref, w_ref, sc_ref, dp_ref, dw_ref, dsc_ref):
        i = pl.program_id(0)

        @pl.when(i == 0)
        def _():
            dw_ref[...] = jnp.zeros(dw_ref.shape, F32)
            dsc_ref[...] = jnp.zeros(dsc_ref.shape, F32)

        t = i * tt + lax.broadcasted_iota(jnp.int32, (tt, 1), 0)
        for g, w in enumerate(POOL_WINDOWS):
            cols = slice(g * pw, (g + 1) * pw)
            dz_v = dz_ref[:, cols]
            pooled_v = p_ref[:, cols]
            zpre = jnp.dot(pooled_v, w_ref[g], preferred_element_type=F32)
            dsc_ref[:, cols] += _colsum(dz_v * zpre)
            dzs = (dz_v * sc_ref[:, cols]).astype(BF16)
            dw_ref[g] += _dot_tn(pooled_v, dzs)
            cnt = jnp.minimum(t + 1, w).astype(F32)
            dp_ref[:, cols] = _dot_nt(dzs, w_ref[g]) / cnt

    tile = pl.BlockSpec((tt, width), lambda i: (i, 0))
    whole_w = pl.BlockSpec((ng, pw, pw), lambda i: (0, 0, 0))
    vec_w = pl.BlockSpec((1, width), lambda i: (0, 0))
    return pl.pallas_call(
        body, name="pool_bwd1", grid=(nt,),
        in_specs=[tile, tile, whole_w, vec_w], out_specs=[tile, whole_w, vec_w],
        out_shape=[jax.ShapeDtypeStruct((t_rows, width), F32), jax.ShapeDtypeStruct((ng, pw, pw), F32),
                   jax.ShapeDtypeStruct((1, width), F32)],
        compiler_params=_params(("arbitrary",)),
    )(dz, pooled, w_pool, scale)


def _pool_bwd2(dps, ng, tt=256):
    t_rows, width = dps.shape
    pw = width // ng
    tt = _tile(t_rows, tt, 16)
    nt = t_rows // tt

    def body(dc_ref, dn_ref, dv_ref):
        i = pl.program_id(0)
        r = lax.broadcasted_iota(jnp.int32, (tt, tt), 0)
        c = lax.broadcasted_iota(jnp.int32, (tt, tt), 1)
        t = i * tt + lax.broadcasted_iota(jnp.int32, (tt, 1), 0)
        last = (i < nt - 1).astype(F32)
        for g, w in enumerate(POOL_WINDOWS):
            cols = slice(g * pw, (g + 1) * pw)
            band_c = ((c >= r) & (c < r + w)).astype(BF16)
            band_n = (c < r + w - tt).astype(BF16)
            dc = dc_ref[:, cols]
            ch, cl = _split_bf16(dc)
            nh, nl = _split_bf16(dn_ref[:, cols] * last)
            sums = (jnp.dot(band_c, ch, preferred_element_type=F32) + jnp.dot(band_c, cl, preferred_element_type=F32)
                    + jnp.dot(band_n, nh, preferred_element_type=F32) + jnp.dot(band_n, nl, preferred_element_type=F32))
            cnt = jnp.minimum(t + 1, w).astype(F32)
            dv_ref[:, cols] = sums - dc * cnt

    tile = pl.BlockSpec((tt, width), lambda i: (i, 0))
    return pl.pallas_call(
        body, name="pool_bwd2", grid=(nt,),
        in_specs=[tile, pl.BlockSpec((tt, width), lambda i: (jnp.minimum(i + 1, nt - 1), 0))],
        out_specs=tile, out_shape=jax.ShapeDtypeStruct((t_rows, width), F32),
        compiler_params=_params(("arbitrary",)),
    )(dps, dps)


def _cpow(ar, ai, n):
    rr, ri, br, bi = None, None, ar, ai
    while n:
        if n & 1:
            rr, ri = (br, bi) if rr is None else (rr * br - ri * bi, rr * bi + ri * br)
        n >>= 1
        if n:
            br, bi = br * br - bi * bi, 2.0 * br * bi
    return rr, ri


def _chunk_carries(st_re, st_im, pr, pi, order):
    cb = st_re.shape[1]
    sub = lax.broadcasted_iota(jnp.int32, (S5_LANES, cb), 0)
    cr = jnp.zeros((S5_LANES, cb), F32)
    ci = jnp.zeros((S5_LANES, cb), F32)
    prev_r = jnp.zeros((1, cb), F32)
    prev_i = jnp.zeros((1, cb), F32)
    for k, src in order:
        er, ei = st_re[src:src + 1, :], st_im[src:src + 1, :]
        nr = er + pr * prev_r - pi * prev_i
        ni = ei + pr * prev_i + pi * prev_r
        cr = jnp.where(sub == k, jnp.broadcast_to(nr, (S5_LANES, cb)), cr)
        ci = jnp.where(sub == k, jnp.broadcast_to(ni, (S5_LANES, cb)), ci)
        prev_r, prev_i = nr, ni
    return cr, ci


def _s5_fwd(up, bblk, ab, cblk, tt=128):
    n_rows, ws = up.shape
    nb, cw, cb2 = bblk.shape
    cb = cb2 // 2
    lc = n_rows // S5_LANES
    tt = _tile(lc, tt, 1)
    nt = lc // tt
    rt = S5_LANES * tt

    def body(u_ref, b_ref, ab_ref, c_ref, y_ref, s_ref, bu_ref, st_re, st_im):
        ps, ti = pl.program_id(1), pl.program_id(2)
        ar = jnp.broadcast_to(ab_ref[0:1, :], (S5_LANES, cb))
        ai = jnp.broadcast_to(ab_ref[1:2, :], (S5_LANES, cb))

        @pl.when((ps == 0) & (ti == 0))
        def _():
            st_re[...] = jnp.zeros(st_re.shape, F32)
            st_im[...] = jnp.zeros(st_im.shape, F32)

        @pl.when((ps == 1) & (ti == 0))
        def _():
            pr, pi = _cpow(ab_ref[0:1, :], ab_ref[1:2, :], lc)
            cr, ci = _chunk_carries(st_re, st_im, pr, pi, [(k, k - 1) for k in range(1, S5_LANES)])
            st_re[...] = cr
            st_im[...] = ci

        bu_ref[...] = jnp.dot(u_ref[...], b_ref[...], preferred_element_type=F32)

        def step(t, carry, store):
            sr, si = carry
            rows = pl.ds(pl.multiple_of(t * S5_LANES, S5_LANES), S5_LANES)
            nr = ar * sr - ai * si + bu_ref[rows, 0:cb]
            ni = ar * si + ai * sr + bu_ref[rows, cb:cb2]
            if store:
                s_ref[rows, 0:cb] = nr
                s_ref[rows, cb:cb2] = ni
            return nr, ni

        @pl.when(ps == 0)
        def _():
            sr, si = lax.fori_loop(0, tt, functools.partial(step, store=False), (st_re[...], st_im[...]))
            st_re[...] = sr
            st_im[...] = si

        @pl.when(ps == 1)
        def _():
            sr, si = lax.fori_loop(0, tt, functools.partial(step, store=True), (st_re[...], st_im[...]))
            st_re[...] = sr
            st_im[...] = si
            y_ref[...] = jnp.dot(s_ref[...].astype(BF16), c_ref[...], preferred_element_type=F32)

    return pl.pallas_call(
        body, name="s5_fwd", grid=(nb, 2, nt),
        in_specs=[pl.BlockSpec((rt, cw), lambda j, ps, ti: (ti, j)),
                  pl.BlockSpec((None, cw, cb2), lambda j, ps, ti: (j, 0, 0)),
                  pl.BlockSpec((None, 2, cb), lambda j, ps, ti: (j, 0, 0)),
                  pl.BlockSpec((None, cb2, cw), lambda j, ps, ti: (j, 0, 0))],
        out_specs=[pl.BlockSpec((rt, cw), lambda j, ps, ti: (ti * ps, j)),
                   pl.BlockSpec((None, rt, cb2), lambda j, ps, ti: (j, ti * ps, 0))],
        out_shape=[jax.ShapeDtypeStruct((n_rows, ws), F32), jax.ShapeDtypeStruct((nb, n_rows, cb2), F32)],
        scratch_shapes=[pltpu.VMEM((rt, cb2), F32), pltpu.VMEM((S5_LANES, cb), F32), pltpu.VMEM((S5_LANES, cb), F32)],
        compiler_params=_params(("arbitrary", "arbitrary", "arbitrary")),
    )(up, bblk, ab, cblk)


def _s5_bwd(dyp, up, s_all, bblk_t, ab, cblk_t, tt=128):
    n_rows, ws = up.shape
    nb, cb2, cw = bblk_t.shape
    cb = cb2 // 2
    lc = n_rows // S5_LANES
    tt = _tile(lc, tt, 1)
    nt = lc // tt
    rt = S5_LANES * tt

    def body(dy_ref, u_ref, s_ref, bt_ref, ab_ref, ct_ref, du_ref, db_ref, dc_ref, da_ref, ds_ref, st_re, st_im):
        ps, ti = pl.program_id(1), pl.program_id(2)
        ar = jnp.broadcast_to(ab_ref[0:1, :], (S5_LANES, cb))
        ai = jnp.broadcast_to(ab_ref[1:2, :], (S5_LANES, cb))

        @pl.when((ps == 0) & (ti == 0))
        def _():
            st_re[...] = jnp.zeros(st_re.shape, F32)
            st_im[...] = jnp.zeros(st_im.shape, F32)
            db_ref[...] = jnp.zeros(db_ref.shape, F32)
            dc_ref[...] = jnp.zeros(dc_ref.shape, F32)
            da_ref[...] = jnp.zeros(da_ref.shape, F32)

        @pl.when((ps == 1) & (ti == 0))
        def _():
            pr, pi = _cpow(ab_ref[0:1, :], -ab_ref[1:2, :], lc)
            cr, ci = _chunk_carries(st_re, st_im, pr, pi, [(k, k + 1) for k in range(S5_LANES - 2, -1, -1)])
            st_re[...] = cr
            st_im[...] = ci

        ds_ref[...] = jnp.dot(dy_ref[...], ct_ref[...], preferred_element_type=F32)

        def rows_of(i):
            return pl.ds(pl.multiple_of((tt - 1 - i) * S5_LANES, S5_LANES), S5_LANES)

        def step0(i, carry):
            gr, gi = carry
            rows = rows_of(i)
            return (ar * gr + ai * gi + ds_ref[rows, 0:cb], ar * gi - ai * gr + ds_ref[rows, cb:cb2])

        def step1(i, carry):
            gr, gi, acr, aci = carry
            rows = rows_of(i)
            sr, si = s_ref[rows, 0:cb], s_ref[rows, cb:cb2]
            acr = acr + sr * gr + si * gi
            aci = aci + sr * gi - si * gr
            nr = ar * gr + ai * gi + ds_ref[rows, 0:cb]
            ni = ar * gi - ai * gr + ds_ref[rows, cb:cb2]
            ds_ref[rows, 0:cb] = nr
            ds_ref[rows, cb:cb2] = ni
            return nr, ni, acr, aci

        @pl.when(ps == 0)
        def _():
            gr, gi = lax.fori_loop(0, tt, step0, (st_re[...], st_im[...]))
            st_re[...] = gr
            st_im[...] = gi

        @pl.when(ps == 1)
        def _():
            zero = jnp.zeros((S5_LANES, cb), F32)
            gr, gi, acr, aci = lax.fori_loop(0, tt, step1, (st_re[...], st_im[...], zero, zero))
            st_re[...] = gr
            st_im[...] = gi
            da_ref[0] += acr
            da_ref[1] += aci
            dsb = ds_ref[...].astype(BF16)
            du_ref[...] = jnp.dot(dsb, bt_ref[...], preferred_element_type=F32)
            db_ref[...] += _dot_tn(u_ref[...], dsb)
            dc_ref[...] += _dot_tn(dy_ref[...], s_ref[...].astype(BF16))

    def tile_idx(ps, ti):
        return (nt - 1 - ti) * ps + (nt - 1) * (1 - ps)

    return pl.pallas_call(
        body, name="s5_bwd", grid=(nb, 2, nt),
        in_specs=[pl.BlockSpec((rt, cw), lambda j, ps, ti: (nt - 1 - ti, j)),
                  pl.BlockSpec((rt, cw), lambda j, ps, ti: (tile_idx(ps, ti), j)),
                  pl.BlockSpec((None, rt, cb2), lambda j, ps, ti: (j, tile_idx(ps, ti), 0)),
                  pl.BlockSpec((None, cb2, cw), lambda j, ps, ti: (j, 0, 0)),
                  pl.BlockSpec((None, 2, cb), lambda j, ps, ti: (j, 0, 0)),
                  pl.BlockSpec((None, cw, cb2), lambda j, ps, ti: (j, 0, 0))],
        out_specs=[pl.BlockSpec((rt, cw), lambda j, ps, ti: (tile_idx(ps, ti), j)),
                   pl.BlockSpec((None, cw, cb2), lambda j, ps, ti: (j, 0, 0)),
                   pl.BlockSpec((None, cw, cb2), lambda j, ps, ti: (j, 0, 0)),
                   pl.BlockSpec((None, 2, S5_LANES, cb), lambda j, ps, ti: (j, 0, 0, 0))],
        out_shape=[jax.ShapeDtypeStruct((n_rows, ws), F32), jax.ShapeDtypeStruct((nb, cw, cb2), F32),
                   jax.ShapeDtypeStruct((nb, cw, cb2), F32), jax.ShapeDtypeStruct((nb, 2, S5_LANES, cb), F32)],
        scratch_shapes=[pltpu.VMEM((rt, cb2), F32), pltpu.VMEM((S5_LANES, cb), F32), pltpu.VMEM((S5_LANES, cb), F32)],
        compiler_params=_params(("arbitrary", "arbitrary", "arbitrary")),
    )(dyp, up, s_all, bblk_t, ab, cblk_t)


def _s5_discretize(a_re, a_im, log_dt, b_re, b_im):
    dt = jnp.exp(log_dt)[:, None]
    mag = jnp.exp(a_re * dt)
    abar_re = mag * jnp.cos(a_im * dt)
    abar_im = mag * jnp.sin(a_im * dt)
    nr, ni = abar_re - 1.0, abar_im
    den = a_re * a_re + a_im * a_im
    fr = (nr * a_re + ni * a_im) / den
    fi = (ni * a_re - nr * a_im) / den
    bbar_re = fr[..., None] * b_re - fi[..., None] * b_im
    bbar_im = fr[..., None] * b_im + fi[..., None] * b_re
    return abar_re, abar_im, bbar_re, bbar_im


def _perm_rows(a):
    n, c = a.shape
    return a.reshape(S5_LANES, n // S5_LANES, c).transpose(1, 0, 2).reshape(n, c)


def _unperm_rows(a):
    n, c = a.shape
    return a.reshape(n // S5_LANES, S5_LANES, c).transpose(1, 0, 2).reshape(n, c)


HBM = pl.BlockSpec(memory_space=pltpu.HBM)
SEM = pl.BlockSpec(memory_space=pltpu.SEMAPHORE)
ANY = pl.BlockSpec(memory_space=pl.ANY)
EFFECT = pltpu.SideEffectType.DATAFLOW_SIDE_EFFECTING
COPIES_PER_BUFFER = {"ag_ici": 3, "ag_fwd": 3, "pair": N_CHIPS, "scatter": 3, "half": 1, "swap": 1, "bcast": 3}
PAIRED_KINDS = ("pair", "scatter", "swap")


def _place():
    x, y, c = lax.axis_index("x"), lax.axis_index("y"), lax.axis_index("c")
    chips = [(1 - x, y), (x, 1 - y), (1 - x, 1 - y)]
    return x, y, c, 2 * x + y, chips


def _n_copies(kind, n_bufs):
    if isinstance(kind, tuple):
        return len(kind[1])
    return COPIES_PER_BUFFER[kind] * (n_bufs // 2 if kind in PAIRED_KINDS else n_bufs)


def _comm_copies(kind, bufs):
    if isinstance(kind, tuple):
        full = _comm_copies(kind[0], bufs)
        return [full[k] for k in kind[1]]
    x, y, c, s, chips = _place()
    sib = (x, y, 1 - c)
    out = []
    if kind == "ag_ici":
        for w in bufs:
            for cx, cy in chips:
                out.append((w.at[s, c], w.at[s, c], w.at[2 * cx + cy, c], (cx, cy, c)))
    elif kind == "ag_fwd":
        for w in bufs:
            for cx, cy in chips:
                sj = 2 * cx + cy
                out.append((w.at[sj, c], w.at[sj, c], w.at[sj, 1 - c], sib))
    elif kind == "pair":
        n = len(bufs) // 2
        for g, got in zip(bufs[:n], bufs[n:]):
            for t in range(N_CHIPS):
                out.append((g.at[t, 1 - c], got.at[t], got.at[t], sib))
    elif kind == "scatter":
        n = len(bufs) // 2
        for p, got in zip(bufs[:n], bufs[n:]):
            for cx, cy in chips:
                out.append((p.at[2 * cx + cy], got.at[s], got.at[2 * cx + cy], (cx, cy, c)))
    elif kind == "half":
        for f in bufs:
            out.append((f.at[c], f.at[c], f.at[1 - c], sib))
    elif kind == "swap":
        n = len(bufs) // 2
        for v, got in zip(bufs[:n], bufs[n:]):
            out.append((v, got, got, sib))
    elif kind == "bcast":
        for w in bufs:
            for cx, cy in chips:
                out.append((w.at[s], w.at[s], w.at[2 * cx + cy], (cx, cy, c)))
    return out


def _comm_fused(name, kind, bufs):
    n = len(bufs)
    ncp = _n_copies(kind, n)

    def body(*refs):
        outs = refs[n:2 * n]
        send, recv = refs[2 * n:]
        copies = _comm_copies(kind, outs)
        started = []
        for k, (src, dst, _, peer) in enumerate(copies):
            cp = pltpu.make_async_remote_copy(src_ref=src, dst_ref=dst, send_sem=send.at[k], recv_sem=recv.at[k],
                                              device_id=peer, device_id_type=MESH)
            cp.start()
            started.append(cp)
        for k, (_, _, land, peer) in enumerate(copies):
            pltpu.make_async_remote_copy(src_ref=land, dst_ref=land, send_sem=send.at[k], recv_sem=recv.at[k],
                                         device_id=peer, device_id_type=MESH).wait_recv()
        for cp in started:
            cp.wait_send()

    return pl.pallas_call(
        body, name=name, in_specs=[ANY] * n, out_specs=[ANY] * n,
        out_shape=[jax.ShapeDtypeStruct(b.shape, b.dtype) for b in bufs],
        input_output_aliases={k: k for k in range(n)},
        scratch_shapes=[pltpu.SemaphoreType.DMA((ncp,))] * 2,
    )(*bufs)


def _comm_start(name, kind, bufs, after=None):
    n = len(bufs)
    ncp = _n_copies(kind, n)
    nx = 0 if after is None else 1

    def body(*refs):
        refs = refs[n + nx:]
        send, recv = refs[:ncp], refs[ncp:2 * ncp]
        outs = refs[2 * ncp:n + 2 * ncp]
        token = refs[n + 2 * ncp]
        for k, (src, dst, _, peer) in enumerate(_comm_copies(kind, outs)):
            pltpu.make_async_remote_copy(src_ref=src, dst_ref=dst, send_sem=send[k], recv_sem=recv[k],
                                         device_id=peer, device_id_type=MESH).start()
        token[...] = jnp.zeros(token.shape, token.dtype)

    res = pl.pallas_call(
        body, name=name, in_specs=[HBM] * n + [ANY] * nx,
        out_specs=[SEM] * (2 * ncp) + [HBM] * n + [pl.BlockSpec(memory_space=pltpu.VMEM)],
        out_shape=[pltpu.SemaphoreType.DMA(())] * (2 * ncp) + [pltpu.HBM(b.shape, b.dtype) for b in bufs]
        + [jax.ShapeDtypeStruct((8, 128), F32)],
        input_output_aliases={k: 2 * ncp + k for k in range(n)},
        compiler_params=pltpu.CompilerParams(has_side_effects=EFFECT),
    )(*[pltpu.with_memory_space_constraint(b, pltpu.HBM) for b in bufs], *([after] if nx else []))
    return list(res[:ncp]), list(res[ncp:2 * ncp]), list(res[2 * ncp:2 * ncp + n]), res[2 * ncp + n]


def _comm_wait(name, kind, bufs, send_sems, recv_sems, after):
    n = len(bufs)
    ncp = _n_copies(kind, n)

    def body(*refs):
        send, recv = refs[n:n + ncp], refs[n + ncp:n + 2 * ncp]
        outs = refs[n + 2 * ncp + 1:]
        for k, (src, _, land, peer) in enumerate(_comm_copies(kind, outs)):
            cp = pltpu.make_async_remote_copy(src_ref=src, dst_ref=land, send_sem=send[k], recv_sem=recv[k],
                                              device_id=peer, device_id_type=MESH)
            cp.wait_send()
            cp.wait_recv()

    return pl.pallas_call(
        body, name=name, in_specs=[HBM] * n + [SEM] * (2 * ncp) + [ANY], out_specs=[HBM] * n,
        out_shape=[pltpu.HBM(b.shape, b.dtype) for b in bufs],
        input_output_aliases={k: k for k in range(n)},
        compiler_params=pltpu.CompilerParams(has_side_effects=EFFECT),
    )(*bufs, *send_sems, *recv_sems, after)


def _pair_add(g, got, core):
    nchip, _, r, cw = g.shape
    tr = _tile(r, 512, 16)

    def body(c_ref, a_ref, b_ref, o_ref):
        o_ref[...] = a_ref[...] + b_ref[...]

    return pl.pallas_call(
        body, name="grads_pair_add",
        grid_spec=pltpu.PrefetchScalarGridSpec(
            num_scalar_prefetch=1, grid=(nchip, r // tr),
            in_specs=[pl.BlockSpec((None, None, tr, cw), lambda s, i, c_ref: (s, c_ref[0], i, 0)),
                      pl.BlockSpec((None, tr, cw), lambda s, i, c_ref: (s, i, 0))],
            out_specs=pl.BlockSpec((None, tr, cw), lambda s, i, c_ref: (s, i, 0))),
        out_shape=jax.ShapeDtypeStruct((nchip, r, cw), BF16),
        compiler_params=_params(("arbitrary", "arbitrary")),
    )(core, g, got)


def _chip_sum(parts, got, idx):
    _, r, cw = parts.shape
    tr = _tile(r, 512, 16)

    def body(i_ref, own_ref, a_ref, b_ref, c_ref, o_ref):
        o_ref[...] = ((own_ref[...].astype(F32) + a_ref[...].astype(F32)) + b_ref[...].astype(F32)) + c_ref[...].astype(F32)

    def slot(k):
        return pl.BlockSpec((None, tr, cw), lambda i, i_ref: (i_ref[k], i, 0))

    return pl.pallas_call(
        body, name="grads_chip_sum",
        grid_spec=pltpu.PrefetchScalarGridSpec(
            num_scalar_prefetch=1, grid=(r // tr,), in_specs=[slot(0), slot(1), slot(2), slot(3)], out_specs=slot(4)),
        out_shape=jax.ShapeDtypeStruct((2, r, cw), F32),
        compiler_params=_params(("arbitrary",)),
    )(idx, parts, got, got, got)


def _add_into_slot(v, got, chip):
    r, cw = v.shape
    tr = _tile(r, 256)

    def body(c_ref, a_ref, b_ref, o_ref):
        o_ref[...] = a_ref[...] + b_ref[...]

    tile = pl.BlockSpec((tr, cw), lambda i, c_ref: (i, 0))
    return pl.pallas_call(
        body, name="small_pair_add",
        grid_spec=pltpu.PrefetchScalarGridSpec(
            num_scalar_prefetch=1, grid=(r // tr,), in_specs=[tile, tile],
            out_specs=pl.BlockSpec((None, tr, cw), lambda i, c_ref: (c_ref[0], i, 0))),
        out_shape=jax.ShapeDtypeStruct((N_CHIPS, r, cw), F32),
        compiler_params=_params(("arbitrary",)),
    )(chip, v, got)


def _sum_slots(w):
    _, r, cw = w.shape
    tr = _tile(r, 256)

    def body(w_ref, o_ref):
        o_ref[...] = ((w_ref[0] + w_ref[1]) + w_ref[2]) + w_ref[3]

    return pl.pallas_call(
        body, name="small_chip_sum", grid=(r // tr,),
        in_specs=[pl.BlockSpec((N_CHIPS, tr, cw), lambda i: (0, i, 0))],
        out_specs=pl.BlockSpec((tr, cw), lambda i: (i, 0)),
        out_shape=jax.ShapeDtypeStruct((r, cw), F32),
        compiler_params=_params(("arbitrary",)),
    )(w)


def _adamw_math(w, g, m, v):
    m = ADAM_B1 * m + (1.0 - ADAM_B1) * g
    v = ADAM_B2 * v + (1.0 - ADAM_B2) * (g * g)
    m_hat = m / (1.0 - ADAM_B1 ** ADAM_STEP)
    v_hat = v / (1.0 - ADAM_B2 ** ADAM_STEP)
    delta = -ADAM_LR * (m_hat / (jnp.sqrt(v_hat) + ADAM_EPS) + ADAM_WD * w)
    return delta, m, v


def _adamw(name, w, m, v, g, g_half=0, g_row_off=0, tr=256):
    r, cw = w.shape
    tr = _tile(math.gcd(r, g_row_off) if g_row_off else r, tr)
    off = g_row_off // tr

    def body(w_ref, m_ref, v_ref, g_ref, go_ref, d_ref, mo_ref, vo_ref):
        g_v = g_ref[...]
        delta, m_n, v_n = _adamw_math(w_ref[...], g_v, m_ref[...], v_ref[...])
        go_ref[...] = g_v
        d_ref[...] = delta
        mo_ref[...] = m_n
        vo_ref[...] = v_n

    tile = pl.BlockSpec((tr, cw), lambda i: (i, 0))
    out = jax.ShapeDtypeStruct((r, cw), F32)
    return pl.pallas_call(
        body, name=name, grid=(r // tr,),
        in_specs=[tile, tile, tile, pl.BlockSpec((None, tr, cw), lambda i: (g_half, i + off, 0))],
        out_specs=[tile] * 4, out_shape=[out] * 4,
        compiler_params=_params(("arbitrary",)),
    )(w, m, v, g)


def kernel(x, mem, g_ffn1, w1_gate, w1_up, w1_down, g_mix, w_in, ssm_a_re, ssm_a_im, ssm_log_dt, ssm_b_re, ssm_b_im, ssm_c_re, ssm_c_im, ssm_d, w_glu, b_glu, w_pool, pool_scale, g_out_ssm, g_out_pool, w_out, g_xattn, g_mem, w_q, w_k, w_v, w_o, g_ffn2, w2_gate, w2_up, w2_down, g_final, loss_target, m_g_ffn1, m_w1_gate, m_w1_up, m_w1_down, m_g_mix, m_w_in, m_ssm_a_re, m_ssm_a_im, m_ssm_log_dt, m_ssm_b_re, m_ssm_b_im, m_ssm_c_re, m_ssm_c_im, m_ssm_d, m_w_glu, m_b_glu, m_w_pool, m_pool_scale, m_g_out_ssm, m_g_out_pool, m_w_out, m_g_xattn, m_g_mem, m_w_q, m_w_k, m_w_v, m_w_o, m_g_ffn2, m_w2_gate, m_w2_up, m_w2_down, m_g_final, v_g_ffn1, v_w1_gate, v_w1_up, v_w1_down, v_g_mix, v_w_in, v_ssm_a_re, v_ssm_a_im, v_ssm_log_dt, v_ssm_b_re, v_ssm_b_im, v_ssm_c_re, v_ssm_c_im, v_ssm_d, v_w_glu, v_b_glu, v_w_pool, v_pool_scale, v_g_out_ssm, v_g_out_pool, v_w_out, v_g_xattn, v_g_mem, v_w_q, v_w_k, v_w_v, v_w_o, v_g_ffn2, v_w2_gate, v_w2_up, v_w2_down, v_g_final):
    local = dict(locals())
    wts = {n: local[n] for n in WEIGHTS}
    mom = {n: local["m_" + n] for n in WEIGHTS}
    var = {n: local["v_" + n] for n in WEIGHTS}

    x2 = x[0]
    mem2 = mem[0]
    tgt = loss_target[0]
    t_rows, d = x2.shape
    fs = w1_gate.shape[-1]
    ds_ = w_in.shape[1]
    ws = d // 2
    n_pg = len(POOL_WINDOWS)
    pw = ws // n_pg
    n_grp = ws // SSM_GROUP
    n_state = ssm_a_re.shape[-1]
    cx_, cy_, cc_ = lax.axis_index("x"), lax.axis_index("y"), lax.axis_index("c")
    chip = (2 * cx_ + cy_).astype(jnp.int32)
    core = cc_.astype(jnp.int32).reshape(1)
    chip_idx = jnp.stack([chip, chip ^ 2, chip ^ 1, chip ^ 3, cc_.astype(jnp.int32)])

    glu_rows = w_glu[0].reshape(-1, d)
    pool_rows = w_pool[0].reshape(-1, d)
    n_glu, n_pool = glu_rows.shape[0], pool_rows.shape[0]
    PACKED = ['w_out', 'w_q', 'w_k', 'w_v', 'w_o']
    glu_at = len(PACKED) * ds_
    pool_at = glu_at + n_glu
    n_pad = -(pool_at + n_pool) % 32
    rp = pool_at + n_pool + n_pad

    def own_slot(src):
        src = src.astype(BF16)
        return lax.dynamic_update_slice(lax.empty((N_CHIPS,) + src.shape, BF16), src[None], (chip, 0, 0, 0))

    src_packed = jnp.concatenate([wts[n][0] for n in PACKED] + [glu_rows, pool_rows, jnp.zeros((n_pad, d), F32)], 0)
    src_up1 = jnp.stack([w1_gate[0], w1_up[0]]).astype(BF16)
    w_bufs = [own_slot(src_up1), own_slot(w1_down[0].reshape(2, fs // 2, d)),
              own_slot(w_in[0].reshape(2, ds_ // 2, d)), own_slot(src_packed.reshape(2, rp // 2, d)),
              own_slot(jnp.stack([w2_gate[0], w2_up[0]])), own_slot(w2_down[0].reshape(2, fs // 2, d))]
    near_send, near_recv, w0, ag_token = _comm_start("weights_start_near", ("ag_ici", (0, 1)), w_bufs[:1])

    def gathered(k, after):
        w = _comm_wait("weights_wait_%d" % k, "ag_ici", [w_bufs[k]], ag_send[3 * k:3 * k + 3],
                       ag_recv[3 * k:3 * k + 3], after)
        return _comm_fused("weights_forward_%d" % k, "ag_fwd", w)[0]

    def gathered_start(k, after):
        w = _comm_wait("weights_wait_%d" % k, "ag_ici", [w_bufs[k]], ag_send[3 * k:3 * k + 3],
                       ag_recv[3 * k:3 * k + 3], after)
        send, recv, thru, token = _comm_start("weights_forward_start_%d" % k, "ag_fwd", w)
        return (send, recv, thru), token

    def gathered_finish(k, handle, after):
        send, recv, thru = handle
        return _comm_wait("weights_forward_wait_%d" % k, "ag_fwd", thru, send, recv, after)[0]

    n1 = _rmsnorm("norm_ffn1", x2, wts['g_ffn1'].reshape(1, -1), deps=[ag_token])
    zero1 = jnp.zeros((1,), jnp.int32)
    near, far = jnp.stack([chip ^ 2, chip ^ 1]), (chip ^ 3).reshape(1)
    a1, b1, hm1 = _ffn_up("ffn1_up_own", n1, src_up1[None], zero1, chip.reshape(1), d, fs)
    n_rest = 3 * len(w_bufs) - 2
    rest_send, rest_recv, w_bufs, _ = _comm_start("weights_start_rest", ("ag_ici", tuple(range(2, 2 + n_rest))),
                                                  w0 + w_bufs[1:], after=hm1)
    ag_send, ag_recv = near_send + rest_send, near_recv + rest_recv
    w0 = _comm_wait("weights_wait_0_near", ("ag_ici", (0, 1)), [w_bufs[0]], ag_send[0:2], ag_recv[0:2], hm1)
    w0 = _comm_fused("weights_forward_0_near", ("ag_fwd", (0, 1)), w0)
    a1, b1, hm1 = _ffn_up("ffn1_up_near", n1, w0[0], near, near, d, fs, into=(a1, b1, hm1))
    w0 = _comm_wait("weights_wait_0_far", ("ag_ici", (2,)), w0, ag_send[2:3], ag_recv[2:3], hm1)
    ga1 = _comm_fused("weights_forward_0_far", ("ag_fwd", (2,)), w0)[0]
    a1, b1, hm1 = _ffn_up("ffn1_up_far", n1, ga1, far, far, d, fs, into=(a1, b1, hm1))
    gd1 = gathered(1, hm1).reshape(N_CHIPS, fs, d)
    h1 = _mm_nn("ffn1_down", hm1, gd1, (N_CHIPS, fs, d // 2), lambda j, k: (0, 0, j), 1, d, F32, res=x2, alpha=0.5,
                slabs=N_CHIPS)
    n2 = _rmsnorm("norm_mix", h1, wts['g_mix'].reshape(1, -1))
    dd_bufs = {'w_in': gathered(2, h1).reshape(N_CHIPS, ds_, d)}
    DD = {n: q for q, n in enumerate(PACKED)}
    DD['w_in'] = 0

    def mm_dd(name, a, wname, out_dtype, res=None, norm_g=None):
        q = DD[wname]
        return _mm_nn(name, a, dd_bufs[wname], (N_CHIPS, ds_, d), lambda j, k: (0, q, 0), 1, d, out_dtype, res=res,
                      norm_g=norm_g, slabs=N_CHIPS)

    def mm_dd_t(name, pairs, out_dtype, deps=()):
        ps = [(dy, dd_bufs[w], (N_CHIPS, ds_, d), functools.partial(lambda s, q: (0, q, 0), q=DD[w])) for dy, w in pairs]
        return _mm_nt_cols(name, ps, N_CHIPS, ds_, [out_dtype], deps=deps, slabs=N_CHIPS)[0]

    def vec(n):
        return wts[n].reshape(1, -1)

    disc_in = (ssm_a_re[0], ssm_a_im[0], ssm_log_dt[0], ssm_b_re[0], ssm_b_im[0])
    (abar_re, abar_im, bbar_re, bbar_im), disc_vjp = jax.vjp(_s5_discretize, *disc_in)
    gpb = min(S5_GROUPS_PER_BLOCK, n_grp)
    nb = n_grp // gpb
    cb = gpb * n_state
    eye = jnp.eye(gpb, dtype=F32)

    def blockdiag(t):
        return jnp.einsum('jgph,gk->jghkp', t.reshape(nb, gpb, n_state, SSM_GROUP), eye).reshape(nb, gpb * SSM_GROUP, cb)

    def blockdiag_c(t):
        return jnp.einsum('jghp,gk->jkpgh', t.reshape(nb, gpb, SSM_GROUP, n_state), eye).reshape(nb, cb, gpb * SSM_GROUP)

    bblk = jnp.concatenate([blockdiag(bbar_re), blockdiag(bbar_im)], -1).astype(BF16)
    cblk = jnp.concatenate([blockdiag_c(ssm_c_re[0]), -blockdiag_c(ssm_c_im[0])], 1).astype(BF16)
    ab = jnp.stack([abar_re.reshape(nb, cb), abar_im.reshape(nb, cb)], 1)

    u = mm_dd("mix_in", n2, 'w_in', F32)

    up = _perm_rows(u[:, :ws]).astype(BF16)
    ylin_p, s_all = _s5_fwd(up, bblk, ab, cblk)
    ylin = _unperm_rows(ylin_p)

    def gelu_fn(r, v):
        y1 = r[0] + v[0] * r[1]
        y2 = jax.nn.gelu(y1)
        return [y2, y2], []
    fwd_pk, tok = gathered_start(3, ylin_p)
    y2, y2b = _rowwise("s5_gelu", gelu_fn, [ylin, (u, 0, ws)], [vec('ssm_d')], [(ws, F32), (ws, BF16)], deps=[tok])
    packed = gathered_finish(3, fwd_pk, y2b).reshape(N_CHIPS, rp, d)
    for n in PACKED:
        dd_bufs[n] = packed
    wglu_full = packed[:, glu_at:glu_at + n_glu, :].reshape(ws, ws)
    wpool_full = packed[:, pool_at:pool_at + n_pool, :].reshape(N_CHIPS, n_pg, pw // N_CHIPS, pw)
    wpool_full = wpool_full.transpose(1, 0, 2, 3).reshape(n_pg, pw, pw)
    z = _mm_nn("s5_glu", y2b, wglu_full, (ws, ws), lambda j, k: (0, 0), 1, ws, F32)

    def glu_fn(r, v):
        y3 = r[0] * _sigmoid(r[1] + v[0])
        return [_rms_fwd(y3, v[1])], []
    m_ssm = _rowwise("s5_gate_norm", glu_fn, [y2, z], [vec('b_glu'), vec('g_out_ssm')], [(ws, BF16)])[0]

    pooled, zp = _pool_fwd(u, 1, wpool_full, vec('pool_scale'))
    fwd_a2, tok = gathered_start(4, zp)
    m_pool = _rmsnorm("norm_pool", zp, vec('g_out_pool'), deps=[tok])
    merged = jnp.concatenate([m_ssm, m_pool], -1)
    h2, hn = mm_dd("mix_out", merged, 'w_out', F32, res=h1, norm_g=vec('g_xattn'))

    q = mm_dd("attn_q", hn, 'w_q', BF16)
    fwd_d2, tok = gathered_start(5, q)
    memn = _rmsnorm("norm_mem", mem2, vec('g_mem'), deps=[tok])
    k_mem = mm_dd("attn_k", memn, 'w_k', BF16)
    v_mem = mm_dd("attn_v", memn, 'w_v', BF16)
    o = _attn_fwd(q, k_mem, v_mem)
    h3, n4 = mm_dd("attn_out", o, 'w_o', F32, res=h2, norm_g=vec('g_ffn2'))

    ga2 = gathered_finish(4, fwd_a2, h3)
    all_chips = jnp.arange(N_CHIPS, dtype=jnp.int32)
    a2, b2, hm2 = _ffn_up("ffn2_up", n4, ga2, all_chips, all_chips, d, fs)
    gd2 = gathered_finish(5, fwd_d2, hm2).reshape(N_CHIPS, fs, d)
    h4 = _mm_nn("ffn2_down", hm2, gd2, (N_CHIPS, fs, d // 2), lambda j, k: (0, 0, j), 1, d, F32, res=h3, alpha=0.5,
                slabs=N_CHIPS)

    def loss_fn(r, v):
        h, t = r
        e = _rms_fwd(h, v[0]) - t
        dy = e * (1.0 / d)
        dh, dg = _rms_bwd(dy, h, v[0])
        part = jnp.sum(_colsum(e * e), axis=1, keepdims=True) * (0.5 / d)
        return [dh, 0.5 * dh], [_colsum(dg), jnp.broadcast_to(part, (1, 128))]
    dh4, dy_f2, dg_final, loss_row = _rowwise("loss_head", loss_fn, [h4, tgt], [g_final.reshape(1, -1)],
                                              [(d, F32), (d, BF16)], [d, 128])

    def rs_pair_start(tag, gbufs, after=None):
        land = [lax.empty((N_CHIPS,) + g.shape[2:], BF16) for g in gbufs]
        send, recv, thru, token = _comm_start(tag + "_pair_start", "pair", list(gbufs) + land, after=after)
        return (send, recv, thru), token

    def rs_scatter_start(tag, handle, after):
        send, recv, thru = handle
        n = len(thru) // 2
        res = _comm_wait(tag + "_pair_wait", "pair", thru, send, recv, after)
        parts = [_pair_add(g, r, core) for g, r in zip(res[:n], res[n:])]
        land = [lax.empty(p.shape, BF16) for p in parts]
        send, recv, thru, token = _comm_start(tag + "_scatter_start", "scatter", parts + land)
        return (send, recv, thru), token

    def rs_half_start(tag, handle, after):
        send, recv, thru = handle
        n = len(thru) // 2
        res = _comm_wait(tag + "_scatter_wait", "scatter", thru, send, recv, after)
        full = [_chip_sum(p, g2, chip_idx) for p, g2 in zip(res[:n], res[n:])]
        send, recv, thru, token = _comm_start(tag + "_half_start", "half", full)
        return (send, recv, thru), token

    def rs_finish(tag, handle, after):
        send, recv, thru = handle
        return _comm_wait(tag + "_half_wait", "half", thru, send, recv, after)

    wblk = (None, None, d, fs)

    def ffn_down_bwd(tag, dy_half, a, b, hm, gd_l, deps=()):
        da, db = _mm_nt_cols(tag + "_down_bwd", [(dy_half, gd_l, (None, fs, d), lambda s: (s, 0, 0))],
                             N_CHIPS, fs, [BF16, BF16], epi=_swiglu_bwd, extras=[a, b], deps=deps, tm=1024, row_parts=4)
        g_down = _mm_tn(tag + "_dw_down", hm, dy_half, fs, d // 2, N_CHIPS, 2, jax.ShapeDtypeStruct((N_CHIPS, fs, d), BF16),
                        (None, fs, d // 2), lambda p, q: (p, 0, q), tt=2048)
        return da, db, g_down.reshape(N_CHIPS, 2, fs // 2, d)

    def ffn_up_bwd(tag, da, db, ga_l, deps=()):
        return _mm_nt_k(tag + "_up_bwd", [(da, ga_l, wblk, lambda s: (s, 0, 0, 0)), (db, ga_l, wblk, lambda s: (s, 1, 0, 0))],
                        N_CHIPS, d, BF16, deps=deps)

    def ffn_dw(name, dact, n_in, deps=()):
        return _mm_tn(name, n_in, dact, d // 2, fs, 2, N_CHIPS, jax.ShapeDtypeStruct((N_CHIPS, 2, d // 2, fs), BF16),
                      (None, None, d // 2, fs), lambda p, q: (q, p, 0, 0), tt=2048, deps=deps)

    def dw_dd(name, a, dy, wname, grad_b2):
        q = DD[wname]
        rows = ds_ if wname == 'w_in' else rp
        return _mm_tn(name, a, dy, ds_, d, N_CHIPS, 1, jax.ShapeDtypeStruct((N_CHIPS, rows, d), BF16),
                      (None, ds_, d), lambda p, qq: (p, q, 0), into=grad_b2, tt=2048)

    def norm_bwd(name, dn, h, gname, dres, deps=(), scale=1.0):
        def fn(r, v):
            dx, dg = _rms_bwd(r[0].astype(F32), r[1], v[0])
            tot = dx + r[2]
            return [tot, scale * tot], [_colsum(dg)]
        return _rowwise(name, fn, [dn, h, dres], [vec(gname)], [(d, F32), (d, BF16)], [d], deps=deps)

    da2, db2, g_down2 = ffn_down_bwd("ffn2", dy_f2, a2, b2, hm2, gd2)
    dn4 = ffn_up_bwd("ffn2", da2, db2, ga2)
    g_gate2 = ffn_dw("ffn2_dw_gate", da2, n4)
    g_up2 = ffn_dw("ffn2_dw_up", db2, n4)
    rs_f2, tok = rs_pair_start("ffn2", [g_gate2, g_up2, g_down2])
    dh3, dh3b, dg_ffn2 = norm_bwd("norm_ffn2_bwd", dn4, h3, 'g_ffn2', dh4, deps=[tok])
    rs_f2, tok = rs_scatter_start("ffn2", rs_f2, dh3b)

    do = mm_dd_t("attn_out_bwd", [(dh3b, 'w_o')], BF16, deps=[tok])
    grad_b2 = dw_dd("attn_dw_o", o, dh3b, 'w_o', None)
    dq, dk, dv = _attn_bwd(q, k_mem, v_mem, do)
    dkb, dvb = dk.astype(BF16), dv.astype(BF16)
    grad_b2 = dw_dd("attn_dw_q", hn, dq, 'w_q', grad_b2)
    dhn = mm_dd_t("attn_q_bwd", [(dq, 'w_q')], BF16)
    dh2, dh2b, dg_xattn = norm_bwd("norm_xattn_bwd", dhn, h2, 'g_xattn', dh3)
    grad_b2 = dw_dd("attn_dw_k", memn, dkb, 'w_k', grad_b2)
    grad_b2 = dw_dd("attn_dw_v", memn, dvb, 'w_v', grad_b2)
    dmemn = mm_dd_t("attn_kv_bwd", [(dkb, 'w_k'), (dvb, 'w_v')], F32)
    dg_mem = _rowwise("norm_mem_bwd", lambda r, v: ([], [_colsum(_rms_bwd(r[0], r[1], v[0])[1])]),
                      [dmemn, mem2], [vec('g_mem')], [], [d])[0]

    dmerged = mm_dd_t("mix_out_bwd", [(dh2b, 'w_out')], F32)
    grad_b2 = dw_dd("mix_dw_out", merged, dh2b, 'w_out', grad_b2)

    def gate_bwd_fn(r, v):
        dm, y2_v, z_v = r
        sg = _sigmoid(z_v + v[0])
        y3 = y2_v * sg
        dy3, dg = _rms_bwd(dm, y3, v[1])
        dz = dy3 * y3 * (1.0 - sg)
        return [dy3 * sg, dz], [_colsum(dg), _colsum(dz)]
    dy2a, dzb, dg_out_ssm, db_glu = _rowwise("s5_gate_norm_bwd", gate_bwd_fn, [(dmerged, 0, ws), y2, z],
                                             [vec('b_glu'), vec('g_out_ssm')], [(ws, F32), (ws, BF16)], [ws, ws])
    dy2b_ = _mm_nt_cols("s5_glu_bwd", [(dzb, wglu_full, (ws, ws), lambda s: (0, 0))], 1, ws, [F32])[0]
    dw_glu = _mm_tn("s5_dw_glu", y2b, dzb, ws, ws, 1, 1, jax.ShapeDtypeStruct((ws, ws), F32), (ws, ws), lambda p, q: (0, 0))

    def gelu_bwd_fn(r, v):
        dy2 = r[0] + r[1]
        us = r[3]
        y1 = r[2] + v[0] * us
        kk = math.sqrt(2.0 / math.pi)
        th = jnp.tanh(kk * (y1 + 0.044715 * y1 * y1 * y1))
        dgelu = 0.5 * (1.0 + th) + 0.5 * y1 * (1.0 - th * th) * kk * (1.0 + 3.0 * 0.044715 * y1 * y1)
        dy1 = dy2 * dgelu
        return [dy1, dy1 * v[0]], [_colsum(dy1 * us)]
    dy1b, du_skip, d_ssm_d = _rowwise("s5_gelu_bwd", gelu_bwd_fn, [dy2a, dy2b_, ylin, (u, 0, ws)], [vec('ssm_d')],
                                      [(ws, BF16), (ws, F32)], [ws])

    bblk_t = jnp.swapaxes(bblk, 1, 2)
    cblk_t = jnp.swapaxes(cblk, 1, 2)
    du_p, d_bblk, d_cblk_t, d_ab = _s5_bwd(_perm_rows(dy1b), up, s_all, bblk_t, ab, cblk_t)
    du_ssm = _unperm_rows(du_p)

    dzp, dg_out_pool = _rowwise("norm_pool_bwd", lambda r, v: (lambda dx, dg: ([dx], [_colsum(dg)]))(*_rms_bwd(r[0], r[1], v[0])),
                                [(dmerged, 1, ws), zp], [vec('g_out_pool')], [(ws, F32)], [ws])
    dps, dw_pool, d_pool_scale = _pool_bwd1(dzp, pooled, wpool_full, vec('pool_scale'))
    du_pool = _pool_bwd2(dps, n_pg)

    dub = _rowwise("mix_du", lambda r, v: ([jnp.concatenate([r[0] + r[1], r[2]], -1)], []),
                   [du_ssm, du_skip, du_pool], [], [(d, BF16)])[0]
    dn2 = mm_dd_t("mix_in_bwd", [(dub, 'w_in')], BF16)
    grad_in = dw_dd("mix_dw_in", n2, dub, 'w_in', None)
    tail_g = jnp.concatenate([
        dw_glu.reshape(N_CHIPS, n_glu, d),
        dw_pool.reshape(n_pg, N_CHIPS, pw // N_CHIPS, pw).transpose(1, 0, 2, 3).reshape(N_CHIPS, n_pool, d),
        jnp.zeros((N_CHIPS, n_pad, d), F32)], 1).astype(BF16)
    grad_b2 = lax.dynamic_update_slice(grad_b2, tail_g, (0, glu_at, 0))
    rs_mix, tok = rs_pair_start("mixers", [grad_b2.reshape(N_CHIPS, 2, rp // 2, d),
                                           grad_in.reshape(N_CHIPS, 2, ds_ // 2, d)])
    dh1, dy_f1, dg_mix = norm_bwd("norm_mix_bwd", dn2, h1, 'g_mix', dh2, deps=[tok], scale=0.5)
    rs_mix, tok = rs_scatter_start("mixers", rs_mix, dy_f1)

    da1, db1, g_down1 = ffn_down_bwd("ffn1", dy_f1, a1, b1, hm1, gd1, deps=[tok])
    rs_d1, tok = rs_pair_start("ffn1_down", [g_down1])
    dn1 = ffn_up_bwd("ffn1", da1, db1, ga1, deps=[tok])
    rs_d1, tok = rs_scatter_start("ffn1_down", rs_d1, dn1)
    grad_x, _, dg_ffn1 = norm_bwd("norm_ffn1_bwd", dn1, x2, 'g_ffn1', dh1, deps=[tok])

    def undiag(t):
        return jnp.einsum('jghkp,gk->jgph', t.reshape(nb, gpb, SSM_GROUP, gpb, n_state), eye).reshape(n_grp, n_state, SSM_GROUP)

    d_bbar_re, d_bbar_im = undiag(d_bblk[:, :, :cb]), undiag(d_bblk[:, :, cb:])
    d_c_re = undiag(d_cblk_t[:, :, :cb]).transpose(0, 2, 1)
    d_c_im = -undiag(d_cblk_t[:, :, cb:]).transpose(0, 2, 1)
    d_abar = jnp.sum(d_ab, axis=2).reshape(nb, 2, gpb, n_state)
    d_abar_re = d_abar[:, 0].reshape(n_grp, n_state)
    d_abar_im = d_abar[:, 1].reshape(n_grp, n_state)
    d_a_re, d_a_im, d_log_dt, d_b_re, d_b_im = disc_vjp((d_abar_re, d_abar_im, d_bbar_re, d_bbar_im))

    small_g = {'g_ffn1': dg_ffn1, 'g_mix': dg_mix, 'ssm_a_re': d_a_re, 'ssm_a_im': d_a_im, 'ssm_log_dt': d_log_dt,
               'ssm_b_re': d_b_re, 'ssm_b_im': d_b_im, 'ssm_c_re': d_c_re, 'ssm_c_im': d_c_im, 'ssm_d': d_ssm_d,
               'b_glu': db_glu, 'pool_scale': d_pool_scale, 'g_out_ssm': dg_out_ssm, 'g_out_pool': dg_out_pool,
               'g_xattn': dg_xattn, 'g_mem': dg_mem, 'g_ffn2': dg_ffn2, 'g_final': dg_final}
    sizes = [wts[n].size for n in SMALL]
    total = sum(sizes) + 128
    rows_s = -(-total // (128 * 256)) * 256
    flat = jnp.concatenate([small_g[n].reshape(-1) for n in SMALL] + [loss_row.reshape(-1)])
    flat = jnp.pad(flat, (0, rows_s * 128 - total)).reshape(rows_s, 128)
    sw_send, sw_recv, sw_thru, tok = _comm_start("small_swap_start", "swap", [flat, lax.empty(flat.shape, F32)])
    g_gate1 = ffn_dw("ffn1_dw_gate", da1, n1, deps=[tok])
    rs_g1, tok_g1 = rs_pair_start("ffn1_gate", [g_gate1])
    sw_v, sw_got = _comm_wait("small_swap_wait", "swap", sw_thru, sw_send, sw_recv, tok_g1)
    slots = _add_into_slot(sw_v, sw_got, chip.reshape(1))
    bc_send, bc_recv, bc_thru, tok = _comm_start("small_bcast_start", "bcast", [slots])
    g_up1 = ffn_dw("ffn1_dw_up", db1, n1, deps=[tok])
    rs_g1, tok = rs_scatter_start("ffn1_gate", rs_g1, g_up1)
    slots, = _comm_wait("small_bcast_wait", "bcast", bc_thru, bc_send, bc_recv, tok)
    red = _sum_slots(slots).reshape(-1)
    loss = red[sum(sizes)]

    def flat_small(t):
        return jnp.pad(jnp.concatenate([t[n].reshape(-1) for n in SMALL]), (0, rows_s * 128 - sum(sizes))).reshape(rows_s, 128)
    sg_, sd_, sm_, sv_ = _adamw("adamw_small", flat_small(wts), flat_small(mom), flat_small(var), red.reshape(1, rows_s, 128))
    out = {}
    off = 0
    for n, sz in zip(SMALL, sizes):
        for key, arr in (('grad', sg_), ('delta', sd_), ('m', sm_), ('v', sv_)):
            out[key, n] = arr.reshape(-1)[off:off + sz].reshape(wts[n].shape)
        off += sz

    def upd(n, g_arr, half, row_off, shape2):
        res = _adamw("adamw_" + n, wts[n].reshape(shape2), mom[n].reshape(shape2), var[n].reshape(shape2), g_arr, half, row_off)
        for key, arr in zip(('grad', 'delta', 'm', 'v'), res):
            out[key, n] = arr.reshape(wts[n].shape)
        return res[3]

    rs_u1, tok = rs_pair_start("ffn1_up", [g_up1], after=sv_)
    rs_f2, tok = rs_half_start("ffn2", rs_f2, tok)
    rs_u1, tok = rs_scatter_start("ffn1_up", rs_u1, tok)
    rs_mix, tok = rs_half_start("mixers", rs_mix, tok)
    full_gate2, full_up2, full_down2 = rs_finish("ffn2", rs_f2, tok)
    upd('w2_gate', full_gate2.reshape(1, d, fs), 0, 0, (d, fs))
    upd('w2_up', full_up2.reshape(1, d, fs), 0, 0, (d, fs))
    last = upd('w2_down', full_down2.reshape(1, fs, d), 0, 0, (fs, d))
    rs_d1, tok = rs_half_start("ffn1_down", rs_d1, last)
    full_b2, full_in = rs_finish("mixers", rs_mix, tok)
    full_b2 = full_b2.reshape(1, rp, d)
    last = upd('w_in', full_in.reshape(1, ds_, d), 0, 0, (ds_, d))
    for n in PACKED:
        last = upd(n, full_b2, 0, DD[n] * ds_, (ds_, d))
    glu_shape, pool_shape = (ws // N_CHIPS, ws), (n_pg * pw // N_CHIPS, pw)
    upd('w_glu', full_b2[:, glu_at:glu_at + n_glu].reshape((1,) + glu_shape), 0, 0, glu_shape)
    upd('w_pool', full_b2[:, pool_at:pool_at + n_pool].reshape((1,) + pool_shape), 0, 0, pool_shape)
    full_down1, = rs_finish("ffn1_down", rs_d1, last)
    last = upd('w1_down', full_down1.reshape(1, fs, d), 0, 0, (fs, d))
    rs_g1, tok = rs_half_start("ffn1_gate", rs_g1, last)
    rs_u1, tok = rs_half_start("ffn1_up", rs_u1, tok)
    full_gate1, = rs_finish("ffn1_gate", rs_g1, tok)
    last = upd('w1_gate', full_gate1.reshape(1, d, fs), 0, 0, (d, fs))
    full_up1, = rs_finish("ffn1_up", rs_u1, last)
    upd('w1_up', full_up1.reshape(1, d, fs), 0, 0, (d, fs))

    return (loss, grad_x[None], *[out['grad', n] for n in WEIGHTS], *[out['delta', n] for n in WEIGHTS],
            *[out['m', n] for n in WEIGHTS], *[out['v', n] for n in WEIGHTS])
```

```python
import functools
import math

import jax
import jax.numpy as jnp
from jax import lax
from jax.experimental import pallas as pl
from jax.experimental.pallas import tpu as pltpu

F32 = jnp.float32
BF16 = jnp.bfloat16
EPS = 1e-6
ADAM_LR, ADAM_B1, ADAM_B2, ADAM_EPS, ADAM_WD, ADAM_STEP = 0.001, 0.9, 0.999, 1e-08, 0.01, 10
POOL_WINDOWS = (2, 4, 8, 16)
SSM_GROUP = 16
S5_GROUPS_PER_BLOCK = 16
S5_LANES = 8
MEM_HEADS = 4
N_CHIPS = 4
VMEM_LIMIT_V7X = 56 * 1024 * 1024
MESH = pl.DeviceIdType.MESH

WEIGHTS = ['g_ffn1', 'w1_gate', 'w1_up', 'w1_down', 'g_mix', 'w_in', 'ssm_a_re', 'ssm_a_im', 'ssm_log_dt',
           'ssm_b_re', 'ssm_b_im', 'ssm_c_re', 'ssm_c_im', 'ssm_d', 'w_glu', 'b_glu', 'w_pool', 'pool_scale',
           'g_out_ssm', 'g_out_pool', 'w_out', 'g_xattn', 'g_mem', 'w_q', 'w_k', 'w_v', 'w_o', 'g_ffn2',
           'w2_gate', 'w2_up', 'w2_down', 'g_final']
BIG = ['w1_gate', 'w1_up', 'w1_down', 'w_in', 'w_glu', 'w_pool', 'w_out', 'w_q', 'w_k', 'w_v', 'w_o',
       'w2_gate', 'w2_up', 'w2_down']
SMALL = [n for n in WEIGHTS if n not in BIG]


def _tile(n, target, mult=8):
    best = None
    for d in range(1, n + 1):
        if n % d == 0 and d <= target and d % mult == 0:
            best = d
    return best if best is not None else n


def _params(sem=None):
    if sem is None:
        return pltpu.CompilerParams(vmem_limit_bytes=VMEM_LIMIT_V7X)
    return pltpu.CompilerParams(dimension_semantics=sem, vmem_limit_bytes=VMEM_LIMIT_V7X)


def _sigmoid(x):
    return 1.0 / (1.0 + jnp.exp(-x))


def _sigmoid_approx(x):
    return pl.reciprocal(1.0 + jnp.exp(-x), approx=True)


def _rms_fwd(x, g):
    r = lax.rsqrt(jnp.mean(x * x, axis=-1, keepdims=True) + EPS)
    return x * r * g


def _rms_bwd(dy, x, g):
    r = lax.rsqrt(jnp.mean(x * x, axis=-1, keepdims=True) + EPS)
    dxh = dy * g
    dx = r * dxh - x * (r * r * r) * jnp.mean(dxh * x, axis=-1, keepdims=True)
    return dx, dy * x * r


def _colsum(v):
    return jnp.sum(v, axis=0, keepdims=True)


def _rowwise(name, fn, rows, vecs, out_defs, red_defs=(), tm=256, deps=()):
    rows = [r if isinstance(r, tuple) else (r, 0, r.shape[1]) for r in rows]
    t_rows = rows[0][0].shape[0]
    tm = _tile(t_rows, tm)
    nr, nv, no, nd, nx = len(rows), len(vecs), len(out_defs), len(red_defs), len(deps)

    def body(*refs):
        r, v = refs[:nr], refs[nr:nr + nv]
        o, d = refs[nr + nv + nx:nr + nv + nx + no], refs[nr + nv + nx + no:]
        outs, reds = fn([x[...] for x in r], [x[...] for x in v])
        for ref, val in zip(o, outs):
            ref[...] = val.astype(ref.dtype)
        if nd:
            @pl.when(pl.program_id(0) == 0)
            def _():
                for ref in d:
                    ref[...] = jnp.zeros(ref.shape, ref.dtype)
            for ref, val in zip(d, reds):
                ref[...] += val

    in_specs = [pl.BlockSpec((tm, w), functools.partial(lambda i, cb: (i, cb), cb=cb)) for (_, cb, w) in rows]
    in_specs += [pl.BlockSpec(v.shape, lambda i: (0, 0)) for v in vecs]
    in_specs += [pl.BlockSpec(memory_space=pl.ANY)] * nx
    out_specs = [pl.BlockSpec((tm, w), lambda i: (i, 0)) for (w, _) in out_defs]
    out_specs += [pl.BlockSpec((1, w), lambda i: (0, 0)) for w in red_defs]
    out_shape = [jax.ShapeDtypeStruct((t_rows, w), dt) for (w, dt) in out_defs]
    out_shape += [jax.ShapeDtypeStruct((1, w), F32) for w in red_defs]
    res = pl.pallas_call(
        body, name=name, grid=(t_rows // tm,), in_specs=in_specs, out_specs=out_specs, out_shape=out_shape,
        compiler_params=_params(("arbitrary",)),
    )(*[r[0] for r in rows], *vecs, *deps)
    return res


def _rmsnorm(name, x, g, tm=256, deps=()):
    return _rowwise(name, lambda r, v: ([_rms_fwd(r[0].astype(F32), v[0])], []), [x], [g],
                    [(x.shape[1], BF16)], tm=tm, deps=deps)[0]


def _mm_nn(name, a, b, b_block, b_idx, nk, n_out, out_dtype, res=None, alpha=1.0, norm_g=None, tm=512, slabs=1):
    t_rows = a.shape[0]
    bk, tn = slabs * b_block[-2], b_block[-1]
    tm = _tile(t_rows, tm)
    nj = n_out // tn
    has_res = res is not None
    has_norm = norm_g is not None
    assert not has_norm or nj == 1

    def body(*refs):
        a_ref, b_ref = refs[0], refs[1]
        res_ref = refs[2] if has_res else None
        g_ref = refs[2 + has_res] if has_norm else None
        o_ref = refs[2 + has_res + has_norm]
        n_ref = refs[3 + has_res + has_norm] if has_norm else None
        k = pl.program_id(2)
        w = b_ref[...].reshape(bk, tn) if slabs > 1 else b_ref[...]
        p = jnp.dot(a_ref[...], w, preferred_element_type=F32)

        def finish(r):
            if has_res:
                r = res_ref[...] + alpha * r
            o_ref[...] = r.astype(o_ref.dtype)
            if has_norm:
                n_ref[...] = _rms_fwd(r, g_ref[...]).astype(n_ref.dtype)

        if nk == 1:
            finish(p)
            return
        acc_ref = refs[3 + has_res + 2 * has_norm]

        @pl.when(k == 0)
        def _():
            acc_ref[...] = p

        @pl.when(k > 0)
        def _():
            acc_ref[...] += p

        @pl.when(k == nk - 1)
        def _():
            finish(acc_ref[...])

    in_specs = [pl.BlockSpec((tm, bk), lambda j, i, k: (i, k)),
                pl.BlockSpec(b_block, lambda j, i, k: b_idx(j, k))]
    args = [a, b]
    if has_res:
        in_specs.append(pl.BlockSpec((tm, tn), lambda j, i, k: (i, j)))
        args.append(res)
    tile = pl.BlockSpec((tm, tn), lambda j, i, k: (i, j))
    out_specs, out_shape = tile, jax.ShapeDtypeStruct((t_rows, n_out), out_dtype)
    if has_norm:
        in_specs.append(pl.BlockSpec((1, n_out), lambda j, i, k: (0, 0)))
        args.append(norm_g)
        out_specs, out_shape = [tile, tile], [out_shape, jax.ShapeDtypeStruct((t_rows, n_out), BF16)]
    return pl.pallas_call(
        body, name=name, grid=(nj, t_rows // tm, nk), in_specs=in_specs, out_specs=out_specs, out_shape=out_shape,
        scratch_shapes=[pltpu.VMEM((tm, tn), F32)] if nk > 1 else [],
        compiler_params=_params(("arbitrary", "arbitrary", "arbitrary")),
    )(*args)


def _dot_nt(x, w):
    return lax.dot_general(x, w, (((1,), (1,)), ((), ())), preferred_element_type=F32)


def _dot_tn(x, y):
    return lax.dot_general(x, y, (((0,), (0,)), ((), ())), preferred_element_type=F32)


def _mm_nt_cols(name, pairs, ns, bn, out_defs, epi=None, extras=(), tm=512, deps=(), slabs=1, row_parts=1):
    t_rows = pairs[0][0].shape[0]
    tm = _tile(t_rows, tm)
    npair, nex, no, nx = len(pairs), len(extras), len(out_defs), len(deps)
    ns, bn = ns // slabs, bn * slabs
    rp = tm // row_parts

    def body(*refs):
        ws = [refs[2 * p + 1][...] for p in range(npair)]
        if slabs > 1:
            ws = [w.reshape(bn, w.shape[-1]) for w in ws]
        for part_i in range(row_parts):
            rows = slice(part_i * rp, (part_i + 1) * rp)
            acc = None
            for p in range(npair):
                part = _dot_nt(refs[2 * p][rows, :], ws[p])
                acc = part if acc is None else acc + part
            ex = [r[rows, :] for r in refs[2 * npair:2 * npair + nex]]
            outs = epi(acc, *ex) if epi is not None else (acc,)
            for ref, val in zip(refs[2 * npair + nex + nx:], outs):
                ref[rows, :] = val.astype(ref.dtype)

    in_specs, args = [], []
    for (dy, w, w_block, w_idx) in pairs:
        in_specs.append(pl.BlockSpec((tm, dy.shape[1]), lambda s, i: (i, 0)))
        in_specs.append(pl.BlockSpec(w_block, functools.partial(lambda s, i, f: f(s), f=w_idx)))
        args += [dy, w]
    for e in extras:
        in_specs.append(pl.BlockSpec((tm, bn), lambda s, i: (i, s)))
        args.append(e)
    in_specs += [pl.BlockSpec(memory_space=pl.ANY)] * nx
    args += list(deps)
    res = pl.pallas_call(
        body, name=name, grid=(ns, t_rows // tm), in_specs=in_specs,
        out_specs=[pl.BlockSpec((tm, bn), lambda s, i: (i, s)) for _ in range(no)],
        out_shape=[jax.ShapeDtypeStruct((t_rows, ns * bn), dt) for dt in out_defs],
        compiler_params=_params(("arbitrary", "arbitrary")),
    )(*args)
    return res


def _mm_nt_k(name, pairs, ns, n_out, out_dtype, tm=512, deps=()):
    t_rows = pairs[0][0].shape[0]
    tm = _tile(t_rows, tm)
    npair, nx = len(pairs), len(deps)

    def body(*refs):
        o_ref, acc_ref = refs[2 * npair + nx], refs[2 * npair + nx + 1]
        s = pl.program_id(1)
        acc = None
        for p in range(npair):
            part = _dot_nt(refs[2 * p][...], refs[2 * p + 1][...])
            acc = part if acc is None else acc + part

        if ns == 1:
            o_ref[...] = acc.astype(o_ref.dtype)
            return

        @pl.when(s == 0)
        def _():
            acc_ref[...] = acc

        @pl.when((s > 0) & (s < ns - 1))
        def _():
            acc_ref[...] += acc

        @pl.when(s == ns - 1)
        def _():
            o_ref[...] = (acc_ref[...] + acc).astype(o_ref.dtype)

    in_specs, args = [], []
    for (a, w, w_block, w_idx) in pairs:
        in_specs.append(pl.BlockSpec((tm, w_block[-1]), lambda i, s: (i, s)))
        in_specs.append(pl.BlockSpec(w_block, functools.partial(lambda i, s, f: f(s), f=w_idx)))
        args += [a, w]
    in_specs += [pl.BlockSpec(memory_space=pl.ANY)] * nx
    args += list(deps)
    return pl.pallas_call(
        body, name=name, grid=(t_rows // tm, ns), in_specs=in_specs,
        out_specs=pl.BlockSpec((tm, n_out), lambda i, s: (i, 0)),
        out_shape=jax.ShapeDtypeStruct((t_rows, n_out), out_dtype),
        scratch_shapes=[pltpu.VMEM((tm, n_out), F32)],
        compiler_params=_params(("arbitrary", "arbitrary")),
    )(*args)


def _mm_tn(name, a, b, bk, bn, n_p, n_q, out_shape, out_block, out_idx, into=None, a_off=0, b_off=0, tt=512,
           deps=()):
    t_rows = a.shape[0]
    tt = _tile(t_rows, tt, 16)
    nt = t_rows // tt
    has_into = into is not None
    nx = len(deps)

    def body(*refs):
        a_ref, b_ref = refs[0], refs[1]
        o_ref, acc_ref = refs[2 + has_into + nx], refs[3 + has_into + nx]
        t = pl.program_id(2)
        part = _dot_tn(a_ref[...], b_ref[...])
        if nt == 1:
            o_ref[...] = part.astype(o_ref.dtype)
            return

        @pl.when(t == 0)
        def _():
            acc_ref[...] = part

        @pl.when((t > 0) & (t < nt - 1))
        def _():
            acc_ref[...] += part

        @pl.when(t == nt - 1)
        def _():
            o_ref[...] = (acc_ref[...] + part).astype(o_ref.dtype)

    in_specs = [pl.BlockSpec((tt, bk), lambda p, q, t: (t, p + a_off)),
                pl.BlockSpec((tt, bn), lambda p, q, t: (t, q + b_off))]
    args = [a, b]
    aliases = {}
    if has_into:
        in_specs.append(pl.BlockSpec(memory_space=pl.ANY))
        args.append(into)
        aliases = {2: 0}
        out_shape = jax.ShapeDtypeStruct(into.shape, into.dtype)
    in_specs += [pl.BlockSpec(memory_space=pl.ANY)] * nx
    args += list(deps)
    return pl.pallas_call(
        body, name=name, grid=(n_p, n_q, nt), in_specs=in_specs,
        out_specs=pl.BlockSpec(out_block, lambda p, q, t: out_idx(p, q)),
        out_shape=out_shape, scratch_shapes=[pltpu.VMEM((bk, bn), F32)],
        input_output_aliases=aliases,
        compiler_params=_params(("arbitrary", "arbitrary", "arbitrary")),
    )(*args)


def _ffn_up(name, n, ga, slots, cols, d_model, fs, into=None, tm=512):
    t_rows = n.shape[0]
    tm = _tile(t_rows, tm)
    n_sh = slots.shape[0]
    has_into = into is not None

    row_parts = 2 if tm % 32 == 0 else 1

    def body(slot_ref, col_ref, n_ref, wg_ref, wu_ref, *refs):
        a_ref, b_ref, h_ref = refs[3 * has_into:]
        wg, wu = wg_ref[...], wu_ref[...]
        for part in range(row_parts):
            rows = slice(part * (tm // row_parts), (part + 1) * (tm // row_parts))
            x = n_ref[rows, :]
            a = jnp.dot(x, wg, preferred_element_type=F32)
            b = jnp.dot(x, wu, preferred_element_type=F32)
            a_ref[rows, :] = a.astype(a_ref.dtype)
            b_ref[rows, :] = b.astype(b_ref.dtype)
            h_ref[rows, :] = (a * _sigmoid_approx(a) * b).astype(h_ref.dtype)

    w_block = (None, None, d_model, fs)
    out = jax.ShapeDtypeStruct((t_rows, N_CHIPS * fs), BF16)
    in_specs = [pl.BlockSpec((tm, d_model), lambda s, i, sl, co: (i, 0)),
                pl.BlockSpec(w_block, lambda s, i, sl, co: (sl[s], 0, 0, 0)),
                pl.BlockSpec(w_block, lambda s, i, sl, co: (sl[s], 1, 0, 0))]
    args = [slots, cols, n, ga, ga]
    aliases = {}
    if has_into:
        in_specs += [pl.BlockSpec(memory_space=pl.ANY)] * 3
        args += list(into)
        aliases = {5: 0, 6: 1, 7: 2}
    return pl.pallas_call(
        body, name=name,
        grid_spec=pltpu.PrefetchScalarGridSpec(
            num_scalar_prefetch=2, grid=(n_sh, t_rows // tm), in_specs=in_specs,
            out_specs=[pl.BlockSpec((tm, fs), lambda s, i, sl, co: (i, co[s]))] * 3),
        out_shape=[out, out, out], input_output_aliases=aliases,
        compiler_params=_params(("arbitrary", "arbitrary")),
    )(*args)


def _swiglu_bwd(dh, a, b):
    a = a.astype(F32)
    b = b.astype(F32)
    sg = _sigmoid_approx(a)
    return dh * b * sg * (1.0 + a * (1.0 - sg)), dh * a * sg


def _attn_fwd(q, k, v, tm=512):
    t_rows, d_model = q.shape
    n_mem = k.shape[0]
    hd = d_model // MEM_HEADS
    scale = hd ** -0.5
    tm = _tile(t_rows, tm)

    def body(q_ref, k_ref, v_ref, o_ref):
        for h in range(MEM_HEADS):
            cols = slice(h * hd, (h + 1) * hd)
            s = _dot_nt(q_ref[:, cols], k_ref[:, cols]) * scale
            s = s - jnp.max(s, axis=-1, keepdims=True)
            e = jnp.exp(s)
            p = e / jnp.sum(e, axis=-1, keepdims=True)
            o_ref[:, cols] = jnp.dot(p.astype(BF16), v_ref[:, cols], preferred_element_type=F32).astype(o_ref.dtype)

    return pl.pallas_call(
        body, name="attn_fwd", grid=(t_rows // tm,),
        in_specs=[pl.BlockSpec((tm, d_model), lambda i: (i, 0)),
                  pl.BlockSpec((n_mem, d_model), lambda i: (0, 0)),
                  pl.BlockSpec((n_mem, d_model), lambda i: (0, 0))],
        out_specs=pl.BlockSpec((tm, d_model), lambda i: (i, 0)),
        out_shape=jax.ShapeDtypeStruct((t_rows, d_model), BF16),
        compiler_params=_params(("arbitrary",)),
    )(q, k, v)


def _attn_bwd(q, k, v, do, tm=512):
    t_rows, d_model = q.shape
    n_mem = k.shape[0]
    hd = d_model // MEM_HEADS
    scale = hd ** -0.5
    tm = _tile(t_rows, tm, 16)

    def body(q_ref, k_ref, v_ref, do_ref, dq_ref, dk_ref, dv_ref):
        @pl.when(pl.program_id(0) == 0)
        def _():
            dk_ref[...] = jnp.zeros(dk_ref.shape, F32)
            dv_ref[...] = jnp.zeros(dv_ref.shape, F32)

        for h in range(MEM_HEADS):
            cols = slice(h * hd, (h + 1) * hd)
            qh, kh, vh, doh = q_ref[:, cols], k_ref[:, cols], v_ref[:, cols], do_ref[:, cols]
            s = _dot_nt(qh, kh) * scale
            s = s - jnp.max(s, axis=-1, keepdims=True)
            e = jnp.exp(s)
            p = e / jnp.sum(e, axis=-1, keepdims=True)
            dv_ref[:, cols] += _dot_tn(p.astype(BF16), doh)
            dp = _dot_nt(doh, vh)
            ds = (p * (dp - jnp.sum(dp * p, axis=-1, keepdims=True)) * scale).astype(BF16)
            dq_ref[:, cols] = jnp.dot(ds, kh, preferred_element_type=F32).astype(dq_ref.dtype)
            dk_ref[:, cols] += _dot_tn(ds, qh)

    full = pl.BlockSpec((n_mem, d_model), lambda i: (0, 0))
    tile = pl.BlockSpec((tm, d_model), lambda i: (i, 0))
    return pl.pallas_call(
        body, name="attn_bwd", grid=(t_rows // tm,),
        in_specs=[tile, full, full, tile], out_specs=[tile, full, full],
        out_shape=[jax.ShapeDtypeStruct((t_rows, d_model), BF16),
                   jax.ShapeDtypeStruct((n_mem, d_model), F32), jax.ShapeDtypeStruct((n_mem, d_model), F32)],
        compiler_params=_params(("arbitrary",)),
    )(q, k, v, do)


def _split_bf16(v):
    hi = v.astype(BF16)
    return hi, (v - hi.astype(F32)).astype(BF16)


def _pool_fwd(u, col_blk, w_pool, scale, tt=256):
    t_rows = u.shape[0]
    ng, pw = w_pool.shape[0], w_pool.shape[-1]
    width = ng * pw
    tt = _tile(t_rows, tt, 16)
    nt = t_rows // tt
    assert len(POOL_WINDOWS) == ng and tt >= max(POOL_WINDOWS)

    def body(vc_ref, vp_ref, w_ref, sc_ref, pooled_ref, z_ref):
        i = pl.program_id(0)
        r = lax.broadcasted_iota(jnp.int32, (tt, tt), 0)
        c = lax.broadcasted_iota(jnp.int32, (tt, tt), 1)
        t = i * tt + lax.broadcasted_iota(jnp.int32, (tt, 1), 0)
        first = (i > 0).astype(F32)
        for g, w in enumerate(POOL_WINDOWS):
            cols = slice(g * pw, (g + 1) * pw)
            band_c = ((c <= r) & (c > r - w)).astype(BF16)
            band_p = (c > r - w + tt).astype(BF16)
            vc = vc_ref[:, cols]
            ch, cl = _split_bf16(vc)
            ph, plo = _split_bf16(vp_ref[:, cols] * first)
            sums = (jnp.dot(band_c, ch, preferred_element_type=F32) + jnp.dot(band_c, cl, preferred_element_type=F32)
                    + jnp.dot(band_p, ph, preferred_element_type=F32) + jnp.dot(band_p, plo, preferred_element_type=F32))
            cnt = jnp.minimum(t + 1, w).astype(F32)
            pooled = (sums / cnt - vc).astype(BF16)
            pooled_ref[:, cols] = pooled
            z_ref[:, cols] = jnp.dot(pooled, w_ref[g], preferred_element_type=F32) * sc_ref[:, cols]

    tile = pl.BlockSpec((tt, width), lambda i: (i, 0))
    return pl.pallas_call(
        body, name="pool_fwd", grid=(nt,),
        in_specs=[pl.BlockSpec((tt, width), lambda i: (i, col_blk)),
                  pl.BlockSpec((tt, width), lambda i: (jnp.maximum(i - 1, 0), col_blk)),
                  pl.BlockSpec((ng, pw, pw), lambda i: (0, 0, 0)),
                  pl.BlockSpec((1, width), lambda i: (0, 0))],
        out_specs=[tile, tile],
        out_shape=[jax.ShapeDtypeStruct((t_rows, width), BF16), jax.ShapeDtypeStruct((t_rows, width), F32)],
        compiler_params=_params(("arbitrary",)),
    )(u, u, w_pool, scale)


def _pool_bwd1(dz, pooled, w_pool, scale, tt=256):
    t_rows = dz.shape[0]
    ng, pw = w_pool.shape[0], w_pool.shape[-1]
    width = ng * pw
    tt = _tile(t_rows, tt, 16)
    nt = t_rows // tt

    def body(dz_ref, p_ref, w_ref, sc_ref, dp_ref, dw_ref, dsc_ref):
        i = pl.program_id(0)

        @pl.when(i == 0)
        def _():
            dw_ref[...] = jnp.zeros(dw_ref.shape, F32)
            dsc_ref[...] = jnp.zeros(dsc_ref.shape, F32)

        t = i * tt + lax.broadcasted_iota(jnp.int32, (tt, 1), 0)
        for g, w in enumerate(POOL_WINDOWS):
            cols = slice(g * pw, (g + 1) * pw)
            dz_v = dz_ref[:, cols]
            pooled_v = p_ref[:, cols]
            zpre = jnp.dot(pooled_v, w_ref[g], preferred_element_type=F32)
            dsc_ref[:, cols] += _colsum(dz_v * zpre)
            dzs = (dz_v * sc_ref[:, cols]).astype(BF16)
            dw_ref[g] += _dot_tn(pooled_v, dzs)
            cnt = jnp.minimum(t + 1, w).astype(F32)
            dp_ref[:, cols] = _dot_nt(dzs, w_ref[g]) / cnt

    tile = pl.BlockSpec((tt, width), lambda i: (i, 0))
    whole_w = pl.BlockSpec((ng, pw, pw), lambda i: (0, 0, 0))
    vec_w = pl.BlockSpec((1, width), lambda i: (0, 0))
    return pl.pallas_call(
        body, name="pool_bwd1", grid=(nt,),
        in_specs=[tile, tile, whole_w, vec_w], out_specs=[tile, whole_w, vec_w],
        out_shape=[jax.ShapeDtypeStruct((t_rows, width), F32), jax.ShapeDtypeStruct((ng, pw, pw), F32),
                   jax.ShapeDtypeStruct((1, width), F32)],
        compiler_params=_params(("arbitrary",)),
    )(dz, pooled, w_pool, scale)


def _pool_bwd2(dps, ng, tt=256):
    t_rows, width = dps.shape
    pw = width // ng
    tt = _tile(t_rows, tt, 16)
    nt = t_rows // tt

    def body(dc_ref, dn_ref, dv_ref):
        i = pl.program_id(0)
        r = lax.broadcasted_iota(jnp.int32, (tt, tt), 0)
        c = lax.broadcasted_iota(jnp.int32, (tt, tt), 1)
        t = i * tt + lax.broadcasted_iota(jnp.int32, (tt, 1), 0)
        last = (i < nt - 1).astype(F32)
        for g, w in enumerate(POOL_WINDOWS):
            cols = slice(g * pw, (g + 1) * pw)
            band_c = ((c >= r) & (c < r + w)).astype(BF16)
            band_n = (c < r + w - tt).astype(BF16)
            dc = dc_ref[:, cols]
            ch, cl = _split_bf16(dc)
            nh, nl = _split_bf16(dn_ref[:, cols] * last)
            sums = (jnp.dot(band_c, ch, preferred_element_type=F32) + jnp.dot(band_c, cl, preferred_element_type=F32)
                    + jnp.dot(band_n, nh, preferred_element_type=F32) + jnp.dot(band_n, nl, preferred_element_type=F32))
            cnt = jnp.minimum(t + 1, w).astype(F32)
            dv_ref[:, cols] = sums - dc * cnt

    tile = pl.BlockSpec((tt, width), lambda i: (i, 0))
    return pl.pallas_call(
        body, name="pool_bwd2", grid=(nt,),
        in_specs=[tile, pl.BlockSpec((tt, width), lambda i: (jnp.minimum(i + 1, nt - 1), 0))],
        out_specs=tile, out_shape=jax.ShapeDtypeStruct((t_rows, width), F32),
        compiler_params=_params(("arbitrary",)),
    )(dps, dps)


def _cpow(ar, ai, n):
    rr, ri, br, bi = None, None, ar, ai
    while n:
        if n & 1:
            rr, ri = (br, bi) if rr is None else (rr * br - ri * bi, rr * bi + ri * br)
        n >>= 1
        if n:
            br, bi = br * br - bi * bi, 2.0 * br * bi
    return rr, ri


def _chunk_carries(st_re, st_im, pr, pi, order):
    cb = st_re.shape[1]
    sub = lax.broadcasted_iota(jnp.int32, (S5_LANES, cb), 0)
    cr = jnp.zeros((S5_LANES, cb), F32)
    ci = jnp.zeros((S5_LANES, cb), F32)
    prev_r = jnp.zeros((1, cb), F32)
    prev_i = jnp.zeros((1, cb), F32)
    for k, src in order:
        er, ei = st_re[src:src + 1, :], st_im[src:src + 1, :]
        nr = er + pr * prev_r - pi * prev_i
        ni = ei + pr * prev_i + pi * prev_r
        cr = jnp.where(sub == k, jnp.broadcast_to(nr, (S5_LANES, cb)), cr)
        ci = jnp.where(sub == k, jnp.broadcast_to(ni, (S5_LANES, cb)), ci)
        prev_r, prev_i = nr, ni
    return cr, ci


def _s5_fwd(up, bblk, ab, cblk, tt=128):
    n_rows, ws = up.shape
    nb, cw, cb2 = bblk.shape
    cb = cb2 // 2
    lc = n_rows // S5_LANES
    tt = _tile(lc, tt, 1)
    nt = lc // tt
    rt = S5_LANES * tt

    def body(u_ref, b_ref, ab_ref, c_ref, y_ref, s_ref, bu_ref, st_re, st_im):
        ps, ti = pl.program_id(1), pl.program_id(2)
        ar = jnp.broadcast_to(ab_ref[0:1, :], (S5_LANES, cb))
        ai = jnp.broadcast_to(ab_ref[1:2, :], (S5_LANES, cb))

        @pl.when((ps == 0) & (ti == 0))
        def _():
            st_re[...] = jnp.zeros(st_re.shape, F32)
            st_im[...] = jnp.zeros(st_im.shape, F32)

        @pl.when((ps == 1) & (ti == 0))
        def _():
            pr, pi = _cpow(ab_ref[0:1, :], ab_ref[1:2, :], lc)
            cr, ci = _chunk_carries(st_re, st_im, pr, pi, [(k, k - 1) for k in range(1, S5_LANES)])
            st_re[...] = cr
            st_im[...] = ci

        bu_ref[...] = jnp.dot(u_ref[...], b_ref[...], preferred_element_type=F32)

        def step(t, carry, store):
            sr, si = carry
            rows = pl.ds(pl.multiple_of(t * S5_LANES, S5_LANES), S5_LANES)
            nr = ar * sr - ai * si + bu_ref[rows, 0:cb]
            ni = ar * si + ai * sr + bu_ref[rows, cb:cb2]
            if store:
                s_ref[rows, 0:cb] = nr
                s_ref[rows, cb:cb2] = ni
            return nr, ni

        @pl.when(ps == 0)
        def _():
            sr, si = lax.fori_loop(0, tt, functools.partial(step, store=False), (st_re[...], st_im[...]))
            st_re[...] = sr
            st_im[...] = si

        @pl.when(ps == 1)
        def _():
            sr, si = lax.fori_loop(0, tt, functools.partial(step, store=True), (st_re[...], st_im[...]))
            st_re[...] = sr
            st_im[...] = si
            y_ref[...] = jnp.dot(s_ref[...].astype(BF16), c_ref[...], preferred_element_type=F32)

    return pl.pallas_call(
        body, name="s5_fwd", grid=(nb, 2, nt),
        in_specs=[pl.BlockSpec((rt, cw), lambda j, ps, ti: (ti, j)),
                  pl.BlockSpec((None, cw, cb2), lambda j, ps, ti: (j, 0, 0)),
                  pl.BlockSpec((None, 2, cb), lambda j, ps, ti: (j, 0, 0)),
                  pl.BlockSpec((None, cb2, cw), lambda j, ps, ti: (j, 0, 0))],
        out_specs=[pl.BlockSpec((rt, cw), lambda j, ps, ti: (ti * ps, j)),
                   pl.BlockSpec((None, rt, cb2), lambda j, ps, ti: (j, ti * ps, 0))],
        out_shape=[jax.ShapeDtypeStruct((n_rows, ws), F32), jax.ShapeDtypeStruct((nb, n_rows, cb2), F32)],
        scratch_shapes=[pltpu.VMEM((rt, cb2), F32), pltpu.VMEM((S5_LANES, cb), F32), pltpu.VMEM((S5_LANES, cb), F32)],
        compiler_params=_params(("arbitrary", "arbitrary", "arbitrary")),
    )(up, bblk, ab, cblk)


def _s5_bwd(dyp, up, s_all, bblk_t, ab, cblk_t, tt=128):
    n_rows, ws = up.shape
    nb, cb2, cw = bblk_t.shape
    cb = cb2 // 2
    lc = n_rows // S5_LANES
    tt = _tile(lc, tt, 1)
    nt = lc // tt
    rt = S5_LANES * tt

    def body(dy_ref, u_ref, s_ref, bt_ref, ab_ref, ct_ref, du_ref, db_ref, dc_ref, da_ref, ds_ref, st_re, st_im):
        ps, ti = pl.program_id(1), pl.program_id(2)
        ar = jnp.broadcast_to(ab_ref[0:1, :], (S5_LANES, cb))
        ai = jnp.broadcast_to(ab_ref[1:2, :], (S5_LANES, cb))

        @pl.when((ps == 0) & (ti == 0))
        def _():
            st_re[...] = jnp.zeros(st_re.shape, F32)
            st_im[...] = jnp.zeros(st_im.shape, F32)
            db_ref[...] = jnp.zeros(db_ref.shape, F32)
            dc_ref[...] = jnp.zeros(dc_ref.shape, F32)
            da_ref[...] = jnp.zeros(da_ref.shape, F32)

        @pl.when((ps == 1) & (ti == 0))
        def _():
            pr, pi = _cpow(ab_ref[0:1, :], -ab_ref[1:2, :], lc)
            cr, ci = _chunk_carries(st_re, st_im, pr, pi, [(k, k + 1) for k in range(S5_LANES - 2, -1, -1)])
            st_re[...] = cr
            st_im[...] = ci

        ds_ref[...] = jnp.dot(dy_ref[...], ct_ref[...], preferred_element_type=F32)

        def rows_of(i):
            return pl.ds(pl.multiple_of((tt - 1 - i) * S5_LANES, S5_LANES), S5_LANES)

        def step0(i, carry):
            gr, gi = carry
            rows = rows_of(i)
            return (ar * gr + ai * gi + ds_ref[rows, 0:cb], ar * gi - ai * gr + ds_ref[rows, cb:cb2])

        def step1(i, carry):
            gr, gi, acr, aci = carry
            rows = rows_of(i)
            sr, si = s_ref[rows, 0:cb], s_ref[rows, cb:cb2]
            acr = acr + sr * gr + si * gi
            aci = aci + sr * gi - si * gr
            nr = ar * gr + ai * gi + ds_ref[rows, 0:cb]
            ni = ar * gi - ai * gr + ds_ref[rows, cb:cb2]
            ds_ref[rows, 0:cb] = nr
            ds_ref[rows, cb:cb2] = ni
            return nr, ni, acr, aci

        @pl.when(ps == 0)
        def _():
            gr, gi = lax.fori_loop(0, tt, step0, (st_re[...], st_im[...]))
            st_re[...] = gr
            st_im[...] = gi

        @pl.when(ps == 1)
        def _():
            zero = jnp.zeros((S5_LANES, cb), F32)
            gr, gi, acr, aci = lax.fori_loop(0, tt, step1, (st_re[...], st_im[...], zero, zero))
            st_re[...] = gr
            st_im[...] = gi
            da_ref[0] += acr
            da_ref[1] += aci
            dsb = ds_ref[...].astype(BF16)
            du_ref[...] = jnp.dot(dsb, bt_ref[...], preferred_element_type=F32)
            db_ref[...] += _dot_tn(u_ref[...], dsb)
            dc_ref[...] += _dot_tn(dy_ref[...], s_ref[...].astype(BF16))

    def tile_idx(ps, ti):
        return (nt - 1 - ti) * ps + (nt - 1) * (1 - ps)

    return pl.pallas_call(
        body, name="s5_bwd", grid=(nb, 2, nt),
        in_specs=[pl.BlockSpec((rt, cw), lambda j, ps, ti: (nt - 1 - ti, j)),
                  pl.BlockSpec((rt, cw), lambda j, ps, ti: (tile_idx(ps, ti), j)),
                  pl.BlockSpec((None, rt, cb2), lambda j, ps, ti: (j, tile_idx(ps, ti), 0)),
                  pl.BlockSpec((None, cb2, cw), lambda j, ps, ti: (j, 0, 0)),
                  pl.BlockSpec((None, 2, cb), lambda j, ps, ti: (j, 0, 0)),
                  pl.BlockSpec((None, cw, cb2), lambda j, ps, ti: (j, 0, 0))],
        out_specs=[pl.BlockSpec((rt, cw), lambda j, ps, ti: (tile_idx(ps, ti), j)),
                   pl.BlockSpec((None, cw, cb2), lambda j, ps, ti: (j, 0, 0)),
                   pl.BlockSpec((None, cw, cb2), lambda j, ps, ti: (j, 0, 0)),
                   pl.BlockSpec((None, 2, S5_LANES, cb), lambda j, ps, ti: (j, 0, 0, 0))],
        out_shape=[jax.ShapeDtypeStruct((n_rows, ws), F32), jax.ShapeDtypeStruct((nb, cw, cb2), F32),
                   jax.ShapeDtypeStruct((nb, cw, cb2), F32), jax.ShapeDtypeStruct((nb, 2, S5_LANES, cb), F32)],
        scratch_shapes=[pltpu.VMEM((rt, cb2), F32), pltpu.VMEM((S5_LANES, cb), F32), pltpu.VMEM((S5_LANES, cb), F32)],
        compiler_params=_params(("arbitrary", "arbitrary", "arbitrary")),
    )(dyp, up, s_all, bblk_t, ab, cblk_t)


def _s5_discretize(a_re, a_im, log_dt, b_re, b_im):
    dt = jnp.exp(log_dt)[:, None]
    mag = jnp.exp(a_re * dt)
    abar_re = mag * jnp.cos(a_im * dt)
    abar_im = mag * jnp.sin(a_im * dt)
    nr, ni = abar_re - 1.0, abar_im
    den = a_re * a_re + a_im * a_im
    fr = (nr * a_re + ni * a_im) / den
    fi = (ni * a_re - nr * a_im) / den
    bbar_re = fr[..., None] * b_re - fi[..., None] * b_im
    bbar_im = fr[..., None] * b_im + fi[..., None] * b_re
    return abar_re, abar_im, bbar_re, bbar_im


def _perm_rows(a):
    n, c = a.shape
    return a.reshape(S5_LANES, n // S5_LANES, c).transpose(1, 0, 2).reshape(n, c)


def _unperm_rows(a):
    n, c = a.shape
    return a.reshape(n // S5_LANES, S5_LANES, c).transpose(1, 0, 2).reshape(n, c)


HBM = pl.BlockSpec(memory_space=pltpu.HBM)
SEM = pl.BlockSpec(memory_space=pltpu.SEMAPHORE)
ANY = pl.BlockSpec(memory_space=pl.ANY)
EFFECT = pltpu.SideEffectType.DATAFLOW_SIDE_EFFECTING
COPIES_PER_BUFFER = {"ag_ici": 3, "ag_fwd": 3, "pair": N_CHIPS, "scatter": 3, "half": 1, "swap": 1, "bcast": 3}
PAIRED_KINDS = ("pair", "scatter", "swap")


def _place():
    x, y, c = lax.axis_index("x"), lax.axis_index("y"), lax.axis_index("c")
    chips = [(1 - x, y), (x, 1 - y), (1 - x, 1 - y)]
    return x, y, c, 2 * x + y, chips


def _n_copies(kind, n_bufs):
    if isinstance(kind, tuple):
        return len(kind[1])
    return COPIES_PER_BUFFER[kind] * (n_bufs // 2 if kind in PAIRED_KINDS else n_bufs)


def _comm_copies(kind, bufs):
    if isinstance(kind, tuple):
        full = _comm_copies(kind[0], bufs)
        return [full[k] for k in kind[1]]
    x, y, c, s, chips = _place()
    sib = (x, y, 1 - c)
    out = []
    if kind == "ag_ici":
        for w in bufs:
            for cx, cy in chips:
                out.append((w.at[s, c], w.at[s, c], w.at[2 * cx + cy, c], (cx, cy, c)))
    elif kind == "ag_fwd":
        for w in bufs:
            for cx, cy in chips:
                sj = 2 * cx + cy
                out.append((w.at[sj, c], w.at[sj, c], w.at[sj, 1 - c], sib))
    elif kind == "pair":
        n = len(bufs) // 2
        for g, got in zip(bufs[:n], bufs[n:]):
            for t in range(N_CHIPS):
                out.append((g.at[t, 1 - c], got.at[t], got.at[t], sib))
    elif kind == "scatter":
        n = len(bufs) // 2
        for p, got in zip(bufs[:n], bufs[n:]):
            for cx, cy in chips:
                out.append((p.at[2 * cx + cy], got.at[s], got.at[2 * cx + cy], (cx, cy, c)))
    elif kind == "half":
        for f in bufs:
            out.append((f.at[c], f.at[c], f.at[1 - c], sib))
    elif kind == "swap":
        n = len(bufs) // 2
        for v, got in zip(bufs[:n], bufs[n:]):
            out.append((v, got, got, sib))
    elif kind == "bcast":
        for w in bufs:
            for cx, cy in chips:
                out.append((w.at[s], w.at[s], w.at[2 * cx + cy], (cx, cy, c)))
    return out


def _comm_fused(name, kind, bufs):
    n = len(bufs)
    ncp = _n_copies(kind, n)

    def body(*refs):
        outs = refs[n:2 * n]
        send, recv = refs[2 * n:]
        copies = _comm_copies(kind, outs)
        started = []
        for k, (src, dst, _, peer) in enumerate(copies):
            cp = pltpu.make_async_remote_copy(src_ref=src, dst_ref=dst, send_sem=send.at[k], recv_sem=recv.at[k],
                                              device_id=peer, device_id_type=MESH)
            cp.start()
            started.append(cp)
        for k, (_, _, land, peer) in enumerate(copies):
            pltpu.make_async_remote_copy(src_ref=land, dst_ref=land, send_sem=send.at[k], recv_sem=recv.at[k],
                                         device_id=peer, device_id_type=MESH).wait_recv()
        for cp in started:
            cp.wait_send()

    return pl.pallas_call(
        body, name=name, in_specs=[ANY] * n, out_specs=[ANY] * n,
        out_shape=[jax.ShapeDtypeStruct(b.shape, b.dtype) for b in bufs],
        input_output_aliases={k: k for k in range(n)},
        scratch_shapes=[pltpu.SemaphoreType.DMA((ncp,))] * 2,
    )(*bufs)


def _comm_start(name, kind, bufs, after=None):
    n = len(bufs)
    ncp = _n_copies(kind, n)
    nx = 0 if after is None else 1

    def body(*refs):
        refs = refs[n + nx:]
        send, recv = refs[:ncp], refs[ncp:2 * ncp]
        outs = refs[2 * ncp:n + 2 * ncp]
        token = refs[n + 2 * ncp]
        for k, (src, dst, _, peer) in enumerate(_comm_copies(kind, outs)):
            pltpu.make_async_remote_copy(src_ref=src, dst_ref=dst, send_sem=send[k], recv_sem=recv[k],
                                         device_id=peer, device_id_type=MESH).start()
        token[...] = jnp.zeros(token.shape, token.dtype)

    res = pl.pallas_call(
        body, name=name, in_specs=[HBM] * n + [ANY] * nx,
        out_specs=[SEM] * (2 * ncp) + [HBM] * n + [pl.BlockSpec(memory_space=pltpu.VMEM)],
        out_shape=[pltpu.SemaphoreType.DMA(())] * (2 * ncp) + [pltpu.HBM(b.shape, b.dtype) for b in bufs]
        + [jax.ShapeDtypeStruct((8, 128), F32)],
        input_output_aliases={k: 2 * ncp + k for k in range(n)},
        compiler_params=pltpu.CompilerParams(has_side_effects=EFFECT),
    )(*[pltpu.with_memory_space_constraint(b, pltpu.HBM) for b in bufs], *([after] if nx else []))
    return list(res[:ncp]), list(res[ncp:2 * ncp]), list(res[2 * ncp:2 * ncp + n]), res[2 * ncp + n]


def _comm_wait(name, kind, bufs, send_sems, recv_sems, after):
    n = len(bufs)
    ncp = _n_copies(kind, n)

    def body(*refs):
        send, recv = refs[n:n + ncp], refs[n + ncp:n + 2 * ncp]
        outs = refs[n + 2 * ncp + 1:]
        for k, (src, _, land, peer) in enumerate(_comm_copies(kind, outs)):
            cp = pltpu.make_async_remote_copy(src_ref=src, dst_ref=land, send_sem=send[k], recv_sem=recv[k],
                                              device_id=peer, device_id_type=MESH)
            cp.wait_send()
            cp.wait_recv()

    return pl.pallas_call(
        body, name=name, in_specs=[HBM] * n + [SEM] * (2 * ncp) + [ANY], out_specs=[HBM] * n,
        out_shape=[pltpu.HBM(b.shape, b.dtype) for b in bufs],
        input_output_aliases={k: k for k in range(n)},
        compiler_params=pltpu.CompilerParams(has_side_effects=EFFECT),
    )(*bufs, *send_sems, *recv_sems, after)


def _pair_add(g, got, core):
    nchip, _, r, cw = g.shape
    tr = _tile(r, 512, 16)

    def body(c_ref, a_ref, b_ref, o_ref):
        o_ref[...] = a_ref[...] + b_ref[...]

    return pl.pallas_call(
        body, name="grads_pair_add",
        grid_spec=pltpu.PrefetchScalarGridSpec(
            num_scalar_prefetch=1, grid=(nchip, r // tr),
            in_specs=[pl.BlockSpec((None, None, tr, cw), lambda s, i, c_ref: (s, c_ref[0], i, 0)),
                      pl.BlockSpec((None, tr, cw), lambda s, i, c_ref: (s, i, 0))],
            out_specs=pl.BlockSpec((None, tr, cw), lambda s, i, c_ref: (s, i, 0))),
        out_shape=jax.ShapeDtypeStruct((nchip, r, cw), BF16),
        compiler_params=_params(("arbitrary", "arbitrary")),
    )(core, g, got)


def _chip_sum(parts, got, idx):
    _, r, cw = parts.shape
    tr = _tile(r, 512, 16)

    def body(i_ref, own_ref, a_ref, b_ref, c_ref, o_ref):
        o_ref[...] = ((own_ref[...].astype(F32) + a_ref[...].astype(F32)) + b_ref[...].astype(F32)) + c_ref[...].astype(F32)

    def slot(k):
        return pl.BlockSpec((None, tr, cw), lambda i, i_ref: (i_ref[k], i, 0))

    return pl.pallas_call(
        body, name="grads_chip_sum",
        grid_spec=pltpu.PrefetchScalarGridSpec(
            num_scalar_prefetch=1, grid=(r // tr,), in_specs=[slot(0), slot(1), slot(2), slot(3)], out_specs=slot(4)),
        out_shape=jax.ShapeDtypeStruct((2, r, cw), F32),
        compiler_params=_params(("arbitrary",)),
    )(idx, parts, got, got, got)


def _add_into_slot(v, got, chip):
    r, cw = v.shape
    tr = _tile(r, 256)

    def body(c_ref, a_ref, b_ref, o_ref):
        o_ref[...] = a_ref[...] + b_ref[...]

    tile = pl.BlockSpec((tr, cw), lambda i, c_ref: (i, 0))
    return pl.pallas_call(
        body, name="small_pair_add",
        grid_spec=pltpu.PrefetchScalarGridSpec(
            num_scalar_prefetch=1, grid=(r // tr,), in_specs=[tile, tile],
            out_specs=pl.BlockSpec((None, tr, cw), lambda i, c_ref: (c_ref[0], i, 0))),
        out_shape=jax.ShapeDtypeStruct((N_CHIPS, r, cw), F32),
        compiler_params=_params(("arbitrary",)),
    )(chip, v, got)


def _sum_slots(w):
    _, r, cw = w.shape
    tr = _tile(r, 256)

    def body(w_ref, o_ref):
        o_ref[...] = ((w_ref[0] + w_ref[1]) + w_ref[2]) + w_ref[3]

    return pl.pallas_call(
        body, name="small_chip_sum", grid=(r // tr,),
        in_specs=[pl.BlockSpec((N_CHIPS, tr, cw), lambda i: (0, i, 0))],
        out_specs=pl.BlockSpec((tr, cw), lambda i: (i, 0)),
        out_shape=jax.ShapeDtypeStruct((r, cw), F32),
        compiler_params=_params(("arbitrary",)),
    )(w)


def _adamw_math(w, g, m, v):
    m = ADAM_B1 * m + (1.0 - ADAM_B1) * g
    v = ADAM_B2 * v + (1.0 - ADAM_B2) * (g * g)
    m_hat = m / (1.0 - ADAM_B1 ** ADAM_STEP)
    v_hat = v / (1.0 - ADAM_B2 ** ADAM_STEP)
    delta = -ADAM_LR * (m_hat / (jnp.sqrt(v_hat) + ADAM_EPS) + ADAM_WD * w)
    return delta, m, v


def _adamw(name, w, m, v, g, g_half=0, g_row_off=0, tr=256):
    r, cw = w.shape
    tr = _tile(math.gcd(r, g_row_off) if g_row_off else r, tr)
    off = g_row_off // tr

    def body(w_ref, m_ref, v_ref, g_ref, go_ref, d_ref, mo_ref, vo_ref):
        g_v = g_ref[...]
        delta, m_n, v_n = _adamw_math(w_ref[...], g_v, m_ref[...], v_ref[...])
        go_ref[...] = g_v
        d_ref[...] = delta
        mo_ref[...] = m_n
        vo_ref[...] = v_n

    tile = pl.BlockSpec((tr, cw), lambda i: (i, 0))
    out = jax.ShapeDtypeStruct((r, cw), F32)
    return pl.pallas_call(
        body, name=name, grid=(r // tr,),
        in_specs=[tile, tile, tile, pl.BlockSpec((None, tr, cw), lambda i: (g_half, i + off, 0))],
        out_specs=[tile] * 4, out_shape=[out] * 4,
        compiler_params=_params(("arbitrary",)),
    )(w, m, v, g)


def kernel(x, mem, g_ffn1, w1_gate, w1_up, w1_down, g_mix, w_in, ssm_a_re, ssm_a_im, ssm_log_dt, ssm_b_re, ssm_b_im, ssm_c_re, ssm_c_im, ssm_d, w_glu, b_glu, w_pool, pool_scale, g_out_ssm, g_out_pool, w_out, g_xattn, g_mem, w_q, w_k, w_v, w_o, g_ffn2, w2_gate, w2_up, w2_down, g_final, loss_target, m_g_ffn1, m_w1_gate, m_w1_up, m_w1_down, m_g_mix, m_w_in, m_ssm_a_re, m_ssm_a_im, m_ssm_log_dt, m_ssm_b_re, m_ssm_b_im, m_ssm_c_re, m_ssm_c_im, m_ssm_d, m_w_glu, m_b_glu, m_w_pool, m_pool_scale, m_g_out_ssm, m_g_out_pool, m_w_out, m_g_xattn, m_g_mem, m_w_q, m_w_k, m_w_v, m_w_o, m_g_ffn2, m_w2_gate, m_w2_up, m_w2_down, m_g_final, v_g_ffn1, v_w1_gate, v_w1_up, v_w1_down, v_g_mix, v_w_in, v_ssm_a_re, v_ssm_a_im, v_ssm_log_dt, v_ssm_b_re, v_ssm_b_im, v_ssm_c_re, v_ssm_c_im, v_ssm_d, v_w_glu, v_b_glu, v_w_pool, v_pool_scale, v_g_out_ssm, v_g_out_pool, v_w_out, v_g_xattn, v_g_mem, v_w_q, v_w_k, v_w_v, v_w_o, v_g_ffn2, v_w2_gate, v_w2_up, v_w2_down, v_g_final):
    local = dict(locals())
    wts = {n: local[n] for n in WEIGHTS}
    mom = {n: local["m_" + n] for n in WEIGHTS}
    var = {n: local["v_" + n] for n in WEIGHTS}

    x2 = x[0]
    mem2 = mem[0]
    tgt = loss_target[0]
    t_rows, d = x2.shape
    fs = w1_gate.shape[-1]
    ds_ = w_in.shape[1]
    ws = d // 2
    n_pg = len(POOL_WINDOWS)
    pw = ws // n_pg
    n_grp = ws // SSM_GROUP
    n_state = ssm_a_re.shape[-1]
    cx_, cy_, cc_ = lax.axis_index("x"), lax.axis_index("y"), lax.axis_index("c")
    chip = (2 * cx_ + cy_).astype(jnp.int32)
    core = cc_.astype(jnp.int32).reshape(1)
    chip_idx = jnp.stack([chip, chip ^ 2, chip ^ 1, chip ^ 3, cc_.astype(jnp.int32)])

    glu_rows = w_glu[0].reshape(-1, d)
    pool_rows = w_pool[0].reshape(-1, d)
    n_glu, n_pool = glu_rows.shape[0], pool_rows.shape[0]
    PACKED = ['w_out', 'w_q', 'w_k', 'w_v', 'w_o']
    glu_at = len(PACKED) * ds_
    pool_at = glu_at + n_glu
    n_pad = -(pool_at + n_pool) % 32
    rp = pool_at + n_pool + n_pad

    def own_slot(src):
        src = src.astype(BF16)
        return lax.dynamic_update_slice(lax.empty((N_CHIPS,) + src.shape, BF16), src[None], (chip, 0, 0, 0))

    src_packed = jnp.concatenate([wts[n][0] for n in PACKED] + [glu_rows, pool_rows, jnp.zeros((n_pad, d), F32)], 0)
    src_up1 = jnp.stack([w1_gate[0], w1_up[0]]).astype(BF16)
    w_bufs = [own_slot(src_up1), own_slot(w1_down[0].reshape(2, fs // 2, d)),
              own_slot(w_in[0].reshape(2, ds_ // 2, d)), own_slot(src_packed.reshape(2, rp // 2, d)),
              own_slot(jnp.stack([w2_gate[0], w2_up[0]])), own_slot(w2_down[0].reshape(2, fs // 2, d))]
    near_send, near_recv, w0, ag_token = _comm_start("weights_start_near", ("ag_ici", (0, 1)), w_bufs[:1])

    def gathered(k, after):
        w = _comm_wait("weights_wait_%d" % k, "ag_ici", [w_bufs[k]], ag_send[3 * k:3 * k + 3],
                       ag_recv[3 * k:3 * k + 3], after)
        return _comm_fused("weights_forward_%d" % k, "ag_fwd", w)[0]

    def gathered_start(k, after):
        w = _comm_wait("weights_wait_%d" % k, "ag_ici", [w_bufs[k]], ag_send[3 * k:3 * k + 3],
                       ag_recv[3 * k:3 * k + 3], after)
        send, recv, thru, token = _comm_start("weights_forward_start_%d" % k, "ag_fwd", w)
        return (send, recv, thru), token

    def gathered_finish(k, handle, after):
        send, recv, thru = handle
        return _comm_wait("weights_forward_wait_%d" % k, "ag_fwd", thru, send, recv, after)[0]

    n1 = _rmsnorm("norm_ffn1", x2, wts['g_ffn1'].reshape(1, -1), deps=[ag_token])
    zero1 = jnp.zeros((1,), jnp.int32)
    near, far = jnp.stack([chip ^ 2, chip ^ 1]), (chip ^ 3).reshape(1)
    a1, b1, hm1 = _ffn_up("ffn1_up_own", n1, src_up1[None], zero1, chip.reshape(1), d, fs)
    n_rest = 3 * len(w_bufs) - 2
    rest_send, rest_recv, w_bufs, _ = _comm_start("weights_start_rest", ("ag_ici", tuple(range(2, 2 + n_rest))),
                                                  w0 + w_bufs[1:], after=hm1)
    ag_send, ag_recv = near_send + rest_send, near_recv + rest_recv
    w0 = _comm_wait("weights_wait_0_near", ("ag_ici", (0, 1)), [w_bufs[0]], ag_send[0:2], ag_recv[0:2], hm1)
    w0 = _comm_fused("weights_forward_0_near", ("ag_fwd", (0, 1)), w0)
    a1, b1, hm1 = _ffn_up("ffn1_up_near", n1, w0[0], near, near, d, fs, into=(a1, b1, hm1))
    w0 = _comm_wait("weights_wait_0_far", ("ag_ici", (2,)), w0, ag_send[2:3], ag_recv[2:3], hm1)
    ga1 = _comm_fused("weights_forward_0_far", ("ag_fwd", (2,)), w0)[0]
    a1, b1, hm1 = _ffn_up("ffn1_up_far", n1, ga1, far, far, d, fs, into=(a1, b1, hm1))
    gd1 = gathered(1, hm1).reshape(N_CHIPS, fs, d)
    h1 = _mm_nn("ffn1_down", hm1, gd1, (N_CHIPS, fs, d // 2), lambda j, k: (0, 0, j), 1, d, F32, res=x2, alpha=0.5,
                slabs=N_CHIPS)
    n2 = _rmsnorm("norm_mix", h1, wts['g_mix'].reshape(1, -1))
    dd_bufs = {'w_in': gathered(2, h1).reshape(N_CHIPS, ds_, d)}
    DD = {n: q for q, n in enumerate(PACKED)}
    DD['w_in'] = 0

    def mm_dd(name, a, wname, out_dtype, res=None, norm_g=None):
        q = DD[wname]
        return _mm_nn(name, a, dd_bufs[wname], (N_CHIPS, ds_, d), lambda j, k: (0, q, 0), 1, d, out_dtype, res=res,
                      norm_g=norm_g, slabs=N_CHIPS)

    def mm_dd_t(name, pairs, out_dtype, deps=()):
        ps = [(dy, dd_bufs[w], (N_CHIPS, ds_, d), functools.partial(lambda s, q: (0, q, 0), q=DD[w])) for dy, w in pairs]
        return _mm_nt_cols(name, ps, N_CHIPS, ds_, [out_dtype], deps=deps, slabs=N_CHIPS)[0]

    def vec(n):
        return wts[n].reshape(1, -1)

    disc_in = (ssm_a_re[0], ssm_a_im[0], ssm_log_dt[0], ssm_b_re[0], ssm_b_im[0])
    (abar_re, abar_im, bbar_re, bbar_im), disc_vjp = jax.vjp(_s5_discretize, *disc_in)
    gpb = min(S5_GROUPS_PER_BLOCK, n_grp)
    nb = n_grp // gpb
    cb = gpb * n_state
    eye = jnp.eye(gpb, dtype=F32)

    def blockdiag(t):
        return jnp.einsum('jgph,gk->jghkp', t.reshape(nb, gpb, n_state, SSM_GROUP), eye).reshape(nb, gpb * SSM_GROUP, cb)

    def blockdiag_c(t):
        return jnp.einsum('jghp,gk->jkpgh', t.reshape(nb, gpb, SSM_GROUP, n_state), eye).reshape(nb, cb, gpb * SSM_GROUP)

    bblk = jnp.concatenate([blockdiag(bbar_re), blockdiag(bbar_im)], -1).astype(BF16)
    cblk = jnp.concatenate([blockdiag_c(ssm_c_re[0]), -blockdiag_c(ssm_c_im[0])], 1).astype(BF16)
    ab = jnp.stack([abar_re.reshape(nb, cb), abar_im.reshape(nb, cb)], 1)

    u = mm_dd("mix_in", n2, 'w_in', F32)

    up = _perm_rows(u[:, :ws]).astype(BF16)
    ylin_p, s_all = _s5_fwd(up, bblk, ab, cblk)
    ylin = _unperm_rows(ylin_p)

    def gelu_fn(r, v):
        y1 = r[0] + v[0] * r[1]
        y2 = jax.nn.gelu(y1)
        return [y2, y2], []
    fwd_pk, tok = gathered_start(3, ylin_p)
    y2, y2b = _rowwise("s5_gelu", gelu_fn, [ylin, (u, 0, ws)], [vec('ssm_d')], [(ws, F32), (ws, BF16)], deps=[tok])
    packed = gathered_finish(3, fwd_pk, y2b).reshape(N_CHIPS, rp, d)
    for n in PACKED:
        dd_bufs[n] = packed
    wglu_full = packed[:, glu_at:glu_at + n_glu, :].reshape(ws, ws)
    wpool_full = packed[:, pool_at:pool_at + n_pool, :].reshape(N_CHIPS, n_pg, pw // N_CHIPS, pw)
    wpool_full = wpool_full.transpose(1, 0, 2, 3).reshape(n_pg, pw, pw)
    z = _mm_nn("s5_glu", y2b, wglu_full, (ws, ws), lambda j, k: (0, 0), 1, ws, F32)

    def glu_fn(r, v):
        y3 = r[0] * _sigmoid(r[1] + v[0])
        return [_rms_fwd(y3, v[1])], []
    m_ssm = _rowwise("s5_gate_norm", glu_fn, [y2, z], [vec('b_glu'), vec('g_out_ssm')], [(ws, BF16)])[0]

    pooled, zp = _pool_fwd(u, 1, wpool_full, vec('pool_scale'))
    fwd_a2, tok = gathered_start(4, zp)
    m_pool = _rmsnorm("norm_pool", zp, vec('g_out_pool'), deps=[tok])
    merged = jnp.concatenate([m_ssm, m_pool], -1)
    h2, hn = mm_dd("mix_out", merged, 'w_out', F32, res=h1, norm_g=vec('g_xattn'))

    q = mm_dd("attn_q", hn, 'w_q', BF16)
    fwd_d2, tok = gathered_start(5, q)
    memn = _rmsnorm("norm_mem", mem2, vec('g_mem'), deps=[tok])
    k_mem = mm_dd("attn_k", memn, 'w_k', BF16)
    v_mem = mm_dd("attn_v", memn, 'w_v', BF16)
    o = _attn_fwd(q, k_mem, v_mem)
    h3, n4 = mm_dd("attn_out", o, 'w_o', F32, res=h2, norm_g=vec('g_ffn2'))

    ga2 = gathered_finish(4, fwd_a2, h3)
    all_chips = jnp.arange(N_CHIPS, dtype=jnp.int32)
    a2, b2, hm2 = _ffn_up("ffn2_up", n4, ga2, all_chips, all_chips, d, fs)
    gd2 = gathered_finish(5, fwd_d2, hm2).reshape(N_CHIPS, fs, d)
    h4 = _mm_nn("ffn2_down", hm2, gd2, (N_CHIPS, fs, d // 2), lambda j, k: (0, 0, j), 1, d, F32, res=h3, alpha=0.5,
                slabs=N_CHIPS)

    def loss_fn(r, v):
        h, t = r
        e = _rms_fwd(h, v[0]) - t
        dy = e * (1.0 / d)
        dh, dg = _rms_bwd(dy, h, v[0])
        part = jnp.sum(_colsum(e * e), axis=1, keepdims=True) * (0.5 / d)
        return [dh, 0.5 * dh], [_colsum(dg), jnp.broadcast_to(part, (1, 128))]
    dh4, dy_f2, dg_final, loss_row = _rowwise("loss_head", loss_fn, [h4, tgt], [g_final.reshape(1, -1)],
                                              [(d, F32), (d, BF16)], [d, 128])

    def rs_pair_start(tag, gbufs, after=None):
        land = [lax.empty((N_CHIPS,) + g.shape[2:], BF16) for g in gbufs]
        send, recv, thru, token = _comm_start(tag + "_pair_start", "pair", list(gbufs) + land, after=after)
        return (send, recv, thru), token

    def rs_scatter_start(tag, handle, after):
        send, recv, thru = handle
        n = len(thru) // 2
        res = _comm_wait(tag + "_pair_wait", "pair", thru, send, recv, after)
        parts = [_pair_add(g, r, core) for g, r in zip(res[:n], res[n:])]
        land = [lax.empty(p.shape, BF16) for p in parts]
        send, recv, thru, token = _comm_start(tag + "_scatter_start", "scatter", parts + land)
        return (send, recv, thru), token

    def rs_half_start(tag, handle, after):
        send, recv, thru = handle
        n = len(thru) // 2
        res = _comm_wait(tag + "_scatter_wait", "scatter", thru, send, recv, after)
        full = [_chip_sum(p, g2, chip_idx) for p, g2 in zip(res[:n], res[n:])]
        send, recv, thru, token = _comm_start(tag + "_half_start", "half", full)
        return (send, recv, thru), token

    def rs_finish(tag, handle, after):
        send, recv, thru = handle
        return _comm_wait(tag + "_half_wait", "half", thru, send, recv, after)

    wblk = (None, None, d, fs)

    def ffn_down_bwd(tag, dy_half, a, b, hm, gd_l, deps=()):
        da, db = _mm_nt_cols(tag + "_down_bwd", [(dy_half, gd_l, (None, fs, d), lambda s: (s, 0, 0))],
                             N_CHIPS, fs, [BF16, BF16], epi=_swiglu_bwd, extras=[a, b], deps=deps, tm=1024, row_parts=4)
        g_down = _mm_tn(tag + "_dw_down", hm, dy_half, fs, d // 2, N_CHIPS, 2, jax.ShapeDtypeStruct((N_CHIPS, fs, d), BF16),
                        (None, fs, d // 2), lambda p, q: (p, 0, q), tt=2048)
        return da, db, g_down.reshape(N_CHIPS, 2, fs // 2, d)

    def ffn_up_bwd(tag, da, db, ga_l, deps=()):
        return _mm_nt_k(tag + "_up_bwd", [(da, ga_l, wblk, lambda s: (s, 0, 0, 0)), (db, ga_l, wblk, lambda s: (s, 1, 0, 0))],
                        N_CHIPS, d, BF16, deps=deps)

    def ffn_dw(name, dact, n_in, deps=()):
        return _mm_tn(name, n_in, dact, d // 2, fs, 2, N_CHIPS, jax.ShapeDtypeStruct((N_CHIPS, 2, d // 2, fs), BF16),
                      (None, None, d // 2, fs), lambda p, q: (q, p, 0, 0), tt=2048, deps=deps)

    def dw_dd(name, a, dy, wname, grad_b2):
        q = DD[wname]
        rows = ds_ if wname == 'w_in' else rp
        return _mm_tn(name, a, dy, ds_, d, N_CHIPS, 1, jax.ShapeDtypeStruct((N_CHIPS, rows, d), BF16),
                      (None, ds_, d), lambda p, qq: (p, q, 0), into=grad_b2, tt=2048)

    def norm_bwd(name, dn, h, gname, dres, deps=(), scale=1.0):
        def fn(r, v):
            dx, dg = _rms_bwd(r[0].astype(F32), r[1], v[0])
            tot = dx + r[2]
            return [tot, scale * tot], [_colsum(dg)]
        return _rowwise(name, fn, [dn, h, dres], [vec(gname)], [(d, F32), (d, BF16)], [d], deps=deps)

    da2, db2, g_down2 = ffn_down_bwd("ffn2", dy_f2, a2, b2, hm2, gd2)
    dn4 = ffn_up_bwd("ffn2", da2, db2, ga2)
    g_gate2 = ffn_dw("ffn2_dw_gate", da2, n4)
    g_up2 = ffn_dw("ffn2_dw_up", db2, n4)
    rs_f2, tok = rs_pair_start("ffn2", [g_gate2, g_up2, g_down2])
    dh3, dh3b, dg_ffn2 = norm_bwd("norm_ffn2_bwd", dn4, h3, 'g_ffn2', dh4, deps=[tok])
    rs_f2, tok = rs_scatter_start("ffn2", rs_f2, dh3b)

    do = mm_dd_t("attn_out_bwd", [(dh3b, 'w_o')], BF16, deps=[tok])
    grad_b2 = dw_dd("attn_dw_o", o, dh3b, 'w_o', None)
    dq, dk, dv = _attn_bwd(q, k_mem, v_mem, do)
    dkb, dvb = dk.astype(BF16), dv.astype(BF16)
    grad_b2 = dw_dd("attn_dw_q", hn, dq, 'w_q', grad_b2)
    dhn = mm_dd_t("attn_q_bwd", [(dq, 'w_q')], BF16)
    dh2, dh2b, dg_xattn = norm_bwd("norm_xattn_bwd", dhn, h2, 'g_xattn', dh3)
    grad_b2 = dw_dd("attn_dw_k", memn, dkb, 'w_k', grad_b2)
    grad_b2 = dw_dd("attn_dw_v", memn, dvb, 'w_v', grad_b2)
    dmemn = mm_dd_t("attn_kv_bwd", [(dkb, 'w_k'), (dvb, 'w_v')], F32)
    dg_mem = _rowwise("norm_mem_bwd", lambda r, v: ([], [_colsum(_rms_bwd(r[0], r[1], v[0])[1])]),
                      [dmemn, mem2], [vec('g_mem')], [], [d])[0]

    dmerged = mm_dd_t("mix_out_bwd", [(dh2b, 'w_out')], F32)
    grad_b2 = dw_dd("mix_dw_out", merged, dh2b, 'w_out', grad_b2)

    def gate_bwd_fn(r, v):
        dm, y2_v, z_v = r
        sg = _sigmoid(z_v + v[0])
        y3 = y2_v * sg
        dy3, dg = _rms_bwd(dm, y3, v[1])
        dz = dy3 * y3 * (1.0 - sg)
        return [dy3 * sg, dz], [_colsum(dg), _colsum(dz)]
    dy2a, dzb, dg_out_ssm, db_glu = _rowwise("s5_gate_norm_bwd", gate_bwd_fn, [(dmerged, 0, ws), y2, z],
                                             [vec('b_glu'), vec('g_out_ssm')], [(ws, F32), (ws, BF16)], [ws, ws])
    dy2b_ = _mm_nt_cols("s5_glu_bwd", [(dzb, wglu_full, (ws, ws), lambda s: (0, 0))], 1, ws, [F32])[0]
    dw_glu = _mm_tn("s5_dw_glu", y2b, dzb, ws, ws, 1, 1, jax.ShapeDtypeStruct((ws, ws), F32), (ws, ws), lambda p, q: (0, 0))

    def gelu_bwd_fn(r, v):
        dy2 = r[0] + r[1]
        us = r[3]
        y1 = r[2] + v[0] * us
        kk = math.sqrt(2.0 / math.pi)
        th = jnp.tanh(kk * (y1 + 0.044715 * y1 * y1 * y1))
        dgelu = 0.5 * (1.0 + th) + 0.5 * y1 * (1.0 - th * th) * kk * (1.0 + 3.0 * 0.044715 * y1 * y1)
        dy1 = dy2 * dgelu
        return [dy1, dy1 * v[0]], [_colsum(dy1 * us)]
    dy1b, du_skip, d_ssm_d = _rowwise("s5_gelu_bwd", gelu_bwd_fn, [dy2a, dy2b_, ylin, (u, 0, ws)], [vec('ssm_d')],
                                      [(ws, BF16), (ws, F32)], [ws])

    bblk_t = jnp.swapaxes(bblk, 1, 2)
    cblk_t = jnp.swapaxes(cblk, 1, 2)
    du_p, d_bblk, d_cblk_t, d_ab = _s5_bwd(_perm_rows(dy1b), up, s_all, bblk_t, ab, cblk_t)
    du_ssm = _unperm_rows(du_p)

    dzp, dg_out_pool = _rowwise("norm_pool_bwd", lambda r, v: (lambda dx, dg: ([dx], [_colsum(dg)]))(*_rms_bwd(r[0], r[1], v[0])),
                                [(dmerged, 1, ws), zp], [vec('g_out_pool')], [(ws, F32)], [ws])
    dps, dw_pool, d_pool_scale = _pool_bwd1(dzp, pooled, wpool_full, vec('pool_scale'))
    du_pool = _pool_bwd2(dps, n_pg)

    dub = _rowwise("mix_du", lambda r, v: ([jnp.concatenate([r[0] + r[1], r[2]], -1)], []),
                   [du_ssm, du_skip, du_pool], [], [(d, BF16)])[0]
    dn2 = mm_dd_t("mix_in_bwd", [(dub, 'w_in')], BF16)
    grad_in = dw_dd("mix_dw_in", n2, dub, 'w_in', None)
    tail_g = jnp.concatenate([
        dw_glu.reshape(N_CHIPS, n_glu, d),
        dw_pool.reshape(n_pg, N_CHIPS, pw // N_CHIPS, pw).transpose(1, 0, 2, 3).reshape(N_CHIPS, n_pool, d),
        jnp.zeros((N_CHIPS, n_pad, d), F32)], 1).astype(BF16)
    grad_b2 = lax.dynamic_update_slice(grad_b2, tail_g, (0, glu_at, 0))
    rs_mix, tok = rs_pair_start("mixers", [grad_b2.reshape(N_CHIPS, 2, rp // 2, d),
                                           grad_in.reshape(N_CHIPS, 2, ds_ // 2, d)])
    dh1, dy_f1, dg_mix = norm_bwd("norm_mix_bwd", dn2, h1, 'g_mix', dh2, deps=[tok], scale=0.5)
    rs_mix, tok = rs_scatter_start("mixers", rs_mix, dy_f1)

    da1, db1, g_down1 = ffn_down_bwd("ffn1", dy_f1, a1, b1, hm1, gd1, deps=[tok])
    rs_d1, tok = rs_pair_start("ffn1_down", [g_down1])
    dn1 = ffn_up_bwd("ffn1", da1, db1, ga1, deps=[tok])
    rs_d1, tok = rs_scatter_start("ffn1_down", rs_d1, dn1)
    grad_x, _, dg_ffn1 = norm_bwd("norm_ffn1_bwd", dn1, x2, 'g_ffn1', dh1, deps=[tok])

    def undiag(t):
        return jnp.einsum('jghkp,gk->jgph', t.reshape(nb, gpb, SSM_GROUP, gpb, n_state), eye).reshape(n_grp, n_state, SSM_GROUP)

    d_bbar_re, d_bbar_im = undiag(d_bblk[:, :, :cb]), undiag(d_bblk[:, :, cb:])
    d_c_re = undiag(d_cblk_t[:, :, :cb]).transpose(0, 2, 1)
    d_c_im = -undiag(d_cblk_t[:, :, cb:]).transpose(0, 2, 1)
    d_abar = jnp.sum(d_ab, axis=2).reshape(nb, 2, gpb, n_state)
    d_abar_re = d_abar[:, 0].reshape(n_grp, n_state)
    d_abar_im = d_abar[:, 1].reshape(n_grp, n_state)
    d_a_re, d_a_im, d_log_dt, d_b_re, d_b_im = disc_vjp((d_abar_re, d_abar_im, d_bbar_re, d_bbar_im))

    small_g = {'g_ffn1': dg_ffn1, 'g_mix': dg_mix, 'ssm_a_re': d_a_re, 'ssm_a_im': d_a_im, 'ssm_log_dt': d_log_dt,
               'ssm_b_re': d_b_re, 'ssm_b_im': d_b_im, 'ssm_c_re': d_c_re, 'ssm_c_im': d_c_im, 'ssm_d': d_ssm_d,
               'b_glu': db_glu, 'pool_scale': d_pool_scale, 'g_out_ssm': dg_out_ssm, 'g_out_pool': dg_out_pool,
               'g_xattn': dg_xattn, 'g_mem': dg_mem, 'g_ffn2': dg_ffn2, 'g_final': dg_final}
    sizes = [wts[n].size for n in SMALL]
    total = sum(sizes) + 128
    rows_s = -(-total // (128 * 256)) * 256
    flat = jnp.concatenate([small_g[n].reshape(-1) for n in SMALL] + [loss_row.reshape(-1)])
    flat = jnp.pad(flat, (0, rows_s * 128 - total)).reshape(rows_s, 128)
    sw_send, sw_recv, sw_thru, tok = _comm_start("small_swap_start", "swap", [flat, lax.empty(flat.shape, F32)])
    g_gate1 = ffn_dw("ffn1_dw_gate", da1, n1, deps=[tok])
    rs_g1, tok_g1 = rs_pair_start("ffn1_gate", [g_gate1])
    sw_v, sw_got = _comm_wait("small_swap_wait", "swap", sw_thru, sw_send, sw_recv, tok_g1)
    slots = _add_into_slot(sw_v, sw_got, chip.reshape(1))
    bc_send, bc_recv, bc_thru, tok = _comm_start("small_bcast_start", "bcast", [slots])
    g_up1 = ffn_dw("ffn1_dw_up", db1, n1, deps=[tok])
    rs_g1, tok = rs_scatter_start("ffn1_gate", rs_g1, g_up1)
    slots, = _comm_wait("small_bcast_wait", "bcast", bc_thru, bc_send, bc_recv, tok)
    red = _sum_slots(slots).reshape(-1)
    loss = red[sum(sizes)]

    def flat_small(t):
        return jnp.pad(jnp.concatenate([t[n].reshape(-1) for n in SMALL]), (0, rows_s * 128 - sum(sizes))).reshape(rows_s, 128)
    sg_, sd_, sm_, sv_ = _adamw("adamw_small", flat_small(wts), flat_small(mom), flat_small(var), red.reshape(1, rows_s, 128))
    out = {}
    off = 0
    for n, sz in zip(SMALL, sizes):
        for key, arr in (('grad', sg_), ('delta', sd_), ('m', sm_), ('v', sv_)):
            out[key, n] = arr.reshape(-1)[off:off + sz].reshape(wts[n].shape)
        off += sz

    def upd(n, g_arr, half, row_off, shape2):
        res = _adamw("adamw_" + n, wts[n].reshape(shape2), mom[n].reshape(shape2), var[n].reshape(shape2), g_arr, half, row_off)
        for key, arr in zip(('grad', 'delta', 'm', 'v'), res):
            out[key, n] = arr.reshape(wts[n].shape)
        return res[3]

    rs_u1, tok = rs_pair_start("ffn1_up", [g_up1], after=sv_)
    rs_f2, tok = rs_half_start("ffn2", rs_f2, tok)
    rs_u1, tok = rs_scatter_start("ffn1_up", rs_u1, tok)
    rs_mix, tok = rs_half_start("mixers", rs_mix, tok)
    full_gate2, full_up2, full_down2 = rs_finish("ffn2", rs_f2, tok)
    upd('w2_gate', full_gate2.reshape(1, d, fs), 0, 0, (d, fs))
    upd('w2_up', full_up2.reshape(1, d, fs), 0, 0, (d, fs))
    last = upd('w2_down', full_down2.reshape(1, fs, d), 0, 0, (fs, d))
    rs_d1, tok = rs_half_start("ffn1_down", rs_d1, last)
    full_b2, full_in = rs_finish("mixers", rs_mix, tok)
    full_b2 = full_b2.reshape(1, rp, d)
    last = upd('w_in', full_in.reshape(1, ds_, d), 0, 0, (ds_, d))
    for n in PACKED:
        last = upd(n, full_b2, 0, DD[n] * ds_, (ds_, d))
    glu_shape, pool_shape = (ws // N_CHIPS, ws), (n_pg * pw // N_CHIPS, pw)
    upd('w_glu', full_b2[:, glu_at:glu_at + n_glu].reshape((1,) + glu_shape), 0, 0, glu_shape)
    upd('w_pool', full_b2[:, pool_at:pool_at + n_pool].reshape((1,) + pool_shape), 0, 0, pool_shape)
    full_down1, = rs_finish("ffn1_down", rs_d1, last)
    last = upd('w1_down', full_down1.reshape(1, fs, d), 0, 0, (fs, d))
    rs_g1, tok = rs_half_start("ffn1_gate", rs_g1, last)
    rs_u1, tok = rs_half_start("ffn1_up", rs_u1, tok)
    full_gate1, = rs_finish("ffn1_gate", rs_g1, tok)
    last = upd('w1_gate', full_gate1.reshape(1, d, fs), 0, 0, (d, fs))
    full_up1, = rs_finish("ffn1_up", rs_u1, last)
    upd('w1_up', full_up1.reshape(1, d, fs), 0, 0, (d, fs))

    return (loss, grad_x[None], *[out['grad', n] for n in WEIGHTS], *[out['delta', n] for n in WEIGHTS],
            *[out['m', n] for n in WEIGHTS], *[out['v', n] for n in WEIGHTS])
```

```python
import functools
import math

import jax
import jax.numpy as jnp
from jax import lax
from jax.experimental import pallas as pl
from jax.experimental.pallas import tpu as pltpu

F32 = jnp.float32
BF16 = jnp.bfloat16
EPS = 1e-6
ADAM_LR, ADAM_B1, ADAM_B2, ADAM_EPS, ADAM_WD, ADAM_STEP = 0.001, 0.9, 0.999, 1e-08, 0.01, 10
POOL_WINDOWS = (2, 4, 8, 16)
SSM_GROUP = 16
S5_GROUPS_PER_BLOCK = 16
S5_LANES = 8
MEM_HEADS = 4
N_CHIPS = 4
VMEM_LIMIT_V7X = 56 * 1024 * 1024
MESH = pl.DeviceIdType.MESH

WEIGHTS = ['g_ffn1', 'w1_gate', 'w1_up', 'w1_down', 'g_mix', 'w_in', 'ssm_a_re', 'ssm_a_im', 'ssm_log_dt',
           'ssm_b_re', 'ssm_b_im', 'ssm_c_re', 'ssm_c_im', 'ssm_d', 'w_glu', 'b_glu', 'w_pool', 'pool_scale',
           'g_out_ssm', 'g_out_pool', 'w_out', 'g_xattn', 'g_mem', 'w_q', 'w_k', 'w_v', 'w_o', 'g_ffn2',
           'w2_gate', 'w2_up', 'w2_down', 'g_final']
BIG = ['w1_gate', 'w1_up', 'w1_down', 'w_in', 'w_glu', 'w_pool', 'w_out', 'w_q', 'w_k', 'w_v', 'w_o',
       'w2_gate', 'w2_up', 'w2_down']
SMALL = [n for n in WEIGHTS if n not in BIG]


def _tile(n, target, mult=8):
    best = None
    for d in range(1, n + 1):
        if n % d == 0 and d <= target and d % mult == 0:
            best = d
    return best if best is not None else n


def _params(sem=None):
    if sem is None:
        return pltpu.CompilerParams(vmem_limit_bytes=VMEM_LIMIT_V7X)
    return pltpu.CompilerParams(dimension_semantics=sem, vmem_limit_bytes=VMEM_LIMIT_V7X)


def _sigmoid(x):
    return 1.0 / (1.0 + jnp.exp(-x))


def _sigmoid_approx(x):
    return pl.reciprocal(1.0 + jnp.exp(-x), approx=True)


def _rms_fwd(x, g):
    r = lax.rsqrt(jnp.mean(x * x, axis=-1, keepdims=True) + EPS)
    return x * r * g


def _rms_bwd(dy, x, g):
    r = lax.rsqrt(jnp.mean(x * x, axis=-1, keepdims=True) + EPS)
    dxh = dy * g
    dx = r * dxh - x * (r * r * r) * jnp.mean(dxh * x, axis=-1, keepdims=True)
    return dx, dy * x * r


def _colsum(v):
    return jnp.sum(v, axis=0, keepdims=True)


def _rowwise(name, fn, rows, vecs, out_defs, red_defs=(), tm=512, deps=()):
    rows = [r if isinstance(r, tuple) else (r, 0, r.shape[1]) for r in rows]
    t_rows = rows[0][0].shape[0]
    tm = _tile(t_rows, tm)
    nr, nv, no, nd, nx = len(rows), len(vecs), len(out_defs), len(red_defs), len(deps)

    def body(*refs):
        r, v = refs[:nr], refs[nr:nr + nv]
        o, d = refs[nr + nv + nx:nr + nv + nx + no], refs[nr + nv + nx + no:]
        outs, reds = fn([x[...] for x in r], [x[...] for x in v])
        for ref, val in zip(o, outs):
            ref[...] = val.astype(ref.dtype)
        if nd:
            @pl.when(pl.program_id(0) == 0)
            def _():
                for ref in d:
                    ref[...] = jnp.zeros(ref.shape, ref.dtype)
            for ref, val in zip(d, reds):
                ref[...] += val

    in_specs = [pl.BlockSpec((tm, w), functools.partial(lambda i, cb: (i, cb), cb=cb)) for (_, cb, w) in rows]
    in_specs += [pl.BlockSpec(v.shape, lambda i: (0, 0)) for v in vecs]
    in_specs += [pl.BlockSpec(memory_space=pl.ANY)] * nx
    out_specs = [pl.BlockSpec((tm, w), lambda i: (i, 0)) for (w, _) in out_defs]
    out_specs += [pl.BlockSpec((1, w), lambda i: (0, 0)) for w in red_defs]
    out_shape = [jax.ShapeDtypeStruct((t_rows, w), dt) for (w, dt) in out_defs]
    out_shape += [jax.ShapeDtypeStruct((1, w), F32) for w in red_defs]
    res = pl.pallas_call(
        body, name=name, grid=(t_rows // tm,), in_specs=in_specs, out_specs=out_specs, out_shape=out_shape,
        compiler_params=_params(("arbitrary",)),
    )(*[r[0] for r in rows], *vecs, *deps)
    return res


def _rmsnorm(name, x, g, tm=256, deps=()):
    return _rowwise(name, lambda r, v: ([_rms_fwd(r[0].astype(F32), v[0])], []), [x], [g],
                    [(x.shape[1], BF16)], tm=tm, deps=deps)[0]


def _mm_nn(name, a, b, b_block, b_idx, nk, n_out, out_dtype, res=None, alpha=1.0, norm_g=None, tm=512, slabs=1):
    t_rows = a.shape[0]
    bk, tn = slabs * b_block[-2], b_block[-1]
    tm = _tile(t_rows, tm)
    nj = n_out // tn
    has_res = res is not None
    has_norm = norm_g is not None
    assert not has_norm or nj == 1

    def body(*refs):
        a_ref, b_ref = refs[0], refs[1]
        res_ref = refs[2] if has_res else None
        g_ref = refs[2 + has_res] if has_norm else None
        o_ref = refs[2 + has_res + has_norm]
        n_ref = refs[3 + has_res + has_norm] if has_norm else None
        k = pl.program_id(2)
        w = b_ref[...].reshape(bk, tn) if slabs > 1 else b_ref[...]
        p = jnp.dot(a_ref[...], w, preferred_element_type=F32)

        def finish(r):
            if has_res:
                r = res_ref[...] + alpha * r
            o_ref[...] = r.astype(o_ref.dtype)
            if has_norm:
                n_ref[...] = _rms_fwd(r, g_ref[...]).astype(n_ref.dtype)

        if nk == 1:
            finish(p)
            return
        acc_ref = refs[3 + has_res + 2 * has_norm]

        @pl.when(k == 0)
        def _():
            acc_ref[...] = p

        @pl.when(k > 0)
        def _():
            acc_ref[...] += p

        @pl.when(k == nk - 1)
        def _():
            finish(acc_ref[...])

    in_specs = [pl.BlockSpec((tm, bk), lambda j, i, k: (i, k)),
                pl.BlockSpec(b_block, lambda j, i, k: b_idx(j, k))]
    args = [a, b]
    if has_res:
        in_specs.append(pl.BlockSpec((tm, tn), lambda j, i, k: (i, j)))
        args.append(res)
    tile = pl.BlockSpec((tm, tn), lambda j, i, k: (i, j))
    out_specs, out_shape = tile, jax.ShapeDtypeStruct((t_rows, n_out), out_dtype)
    if has_norm:
        in_specs.append(pl.BlockSpec((1, n_out), lambda j, i, k: (0, 0)))
        args.append(norm_g)
        out_specs, out_shape = [tile, tile], [out_shape, jax.ShapeDtypeStruct((t_rows, n_out), BF16)]
    return pl.pallas_call(
        body, name=name, grid=(nj, t_rows // tm, nk), in_specs=in_specs, out_specs=out_specs, out_shape=out_shape,
        scratch_shapes=[pltpu.VMEM((tm, tn), F32)] if nk > 1 else [],
        compiler_params=_params(("arbitrary", "arbitrary", "arbitrary")),
    )(*args)


def _dot_nt(x, w):
    return lax.dot_general(x, w, (((1,), (1,)), ((), ())), preferred_element_type=F32)


def _dot_tn(x, y):
    return lax.dot_general(x, y, (((0,), (0,)), ((), ())), preferred_element_type=F32)


def _mm_nt_cols(name, pairs, ns, bn, out_defs, epi=None, extras=(), tm=512, deps=(), slabs=1, row_parts=1):
    t_rows = pairs[0][0].shape[0]
    tm = _tile(t_rows, tm)
    npair, nex, no, nx = len(pairs), len(extras), len(out_defs), len(deps)
    ns, bn = ns // slabs, bn * slabs
    rp = tm // row_parts

    def body(*refs):
        ws = [refs[2 * p + 1][...] for p in range(npair)]
        if slabs > 1:
            ws = [w.reshape(bn, w.shape[-1]) for w in ws]
        for part_i in range(row_parts):
            rows = slice(part_i * rp, (part_i + 1) * rp)
            acc = None
            for p in range(npair):
                part = _dot_nt(refs[2 * p][rows, :], ws[p])
                acc = part if acc is None else acc + part
            ex = [r[rows, :] for r in refs[2 * npair:2 * npair + nex]]
            outs = epi(acc, *ex) if epi is not None else (acc,)
            for ref, val in zip(refs[2 * npair + nex + nx:], outs):
                ref[rows, :] = val.astype(ref.dtype)

    in_specs, args = [], []
    for (dy, w, w_block, w_idx) in pairs:
        in_specs.append(pl.BlockSpec((tm, dy.shape[1]), lambda s, i: (i, 0)))
        in_specs.append(pl.BlockSpec(w_block, functools.partial(lambda s, i, f: f(s), f=w_idx)))
        args += [dy, w]
    for e in extras:
        in_specs.append(pl.BlockSpec((tm, bn), lambda s, i: (i, s)))
        args.append(e)
    in_specs += [pl.BlockSpec(memory_space=pl.ANY)] * nx
    args += list(deps)
    res = pl.pallas_call(
        body, name=name, grid=(ns, t_rows // tm), in_specs=in_specs,
        out_specs=[pl.BlockSpec((tm, bn), lambda s, i: (i, s)) for _ in range(no)],
        out_shape=[jax.ShapeDtypeStruct((t_rows, ns * bn), dt) for dt in out_defs],
        compiler_params=_params(("arbitrary", "arbitrary")),
    )(*args)
    return res


def _mm_nt_k(name, pairs, ns, n_out, out_dtype, tm=512, deps=()):
    t_rows = pairs[0][0].shape[0]
    tm = _tile(t_rows, tm)
    npair, nx = len(pairs), len(deps)

    def body(*refs):
        o_ref, acc_ref = refs[2 * npair + nx], refs[2 * npair + nx + 1]
        s = pl.program_id(1)
        acc = None
        for p in range(npair):
            part = _dot_nt(refs[2 * p][...], refs[2 * p + 1][...])
            acc = part if acc is None else acc + part

        if ns == 1:
            o_ref[...] = acc.astype(o_ref.dtype)
            return

        @pl.when(s == 0)
        def _():
            acc_ref[...] = acc

        @pl.when((s > 0) & (s < ns - 1))
        def _():
            acc_ref[...] += acc

        @pl.when(s == ns - 1)
        def _():
            o_ref[...] = (acc_ref[...] + acc).astype(o_ref.dtype)

    in_specs, args = [], []
    for (a, w, w_block, w_idx) in pairs:
        in_specs.append(pl.BlockSpec((tm, w_block[-1]), lambda i, s: (i, s)))
        in_specs.append(pl.BlockSpec(w_block, functools.partial(lambda i, s, f: f(s), f=w_idx)))
        args += [a, w]
    in_specs += [pl.BlockSpec(memory_space=pl.ANY)] * nx
    args += list(deps)
    return pl.pallas_call(
        body, name=name, grid=(t_rows // tm, ns), in_specs=in_specs,
        out_specs=pl.BlockSpec((tm, n_out), lambda i, s: (i, 0)),
        out_shape=jax.ShapeDtypeStruct((t_rows, n_out), out_dtype),
        scratch_shapes=[pltpu.VMEM((tm, n_out), F32)],
        compiler_params=_params(("arbitrary", "arbitrary")),
    )(*args)


def _mm_tn(name, a, b, bk, bn, n_p, n_q, out_shape, out_block, out_idx, into=None, a_off=0, b_off=0, tt=512,
           deps=()):
    t_rows = a.shape[0]
    tt = _tile(t_rows, tt, 16)
    nt = t_rows // tt
    has_into = into is not None
    nx = len(deps)

    def body(*refs):
        a_ref, b_ref = refs[0], refs[1]
        o_ref, acc_ref = refs[2 + has_into + nx], refs[3 + has_into + nx]
        t = pl.program_id(2)
        part = _dot_tn(a_ref[...], b_ref[...])
        if nt == 1:
            o_ref[...] = part.astype(o_ref.dtype)
            return

        @pl.when(t == 0)
        def _():
            acc_ref[...] = part

        @pl.when((t > 0) & (t < nt - 1))
        def _():
            acc_ref[...] += part

        @pl.when(t == nt - 1)
        def _():
            o_ref[...] = (acc_ref[...] + part).astype(o_ref.dtype)

    in_specs = [pl.BlockSpec((tt, bk), lambda p, q, t: (t, p + a_off)),
                pl.BlockSpec((tt, bn), lambda p, q, t: (t, q + b_off))]
    args = [a, b]
    aliases = {}
    if has_into:
        in_specs.append(pl.BlockSpec(memory_space=pl.ANY))
        args.append(into)
        aliases = {2: 0}
        out_shape = jax.ShapeDtypeStruct(into.shape, into.dtype)
    in_specs += [pl.BlockSpec(memory_space=pl.ANY)] * nx
    args += list(deps)
    return pl.pallas_call(
        body, name=name, grid=(n_p, n_q, nt), in_specs=in_specs,
        out_specs=pl.BlockSpec(out_block, lambda p, q, t: out_idx(p, q)),
        out_shape=out_shape, scratch_shapes=[pltpu.VMEM((bk, bn), F32)],
        input_output_aliases=aliases,
        compiler_params=_params(("arbitrary", "arbitrary", "arbitrary")),
    )(*args)


def _ffn_up(name, n, ga, slots, cols, d_model, fs, into=None, tm=512):
    t_rows = n.shape[0]
    tm = _tile(t_rows, tm)
    n_sh = slots.shape[0]
    has_into = into is not None

    row_parts = 2 if tm % 32 == 0 else 1

    def body(slot_ref, col_ref, n_ref, wg_ref, wu_ref, *refs):
        a_ref, b_ref, h_ref = refs[3 * has_into:]
        wg, wu = wg_ref[...], wu_ref[...]
        for part in range(row_parts):
            rows = slice(part * (tm // row_parts), (part + 1) * (tm // row_parts))
            x = n_ref[rows, :]
            a = jnp.dot(x, wg, preferred_element_type=F32)
            b = jnp.dot(x, wu, preferred_element_type=F32)
            a_ref[rows, :] = a.astype(a_ref.dtype)
            b_ref[rows, :] = b.astype(b_ref.dtype)
            h_ref[rows, :] = (a * _sigmoid_approx(a) * b).astype(h_ref.dtype)

    w_block = (None, None, d_model, fs)
    out = jax.ShapeDtypeStruct((t_rows, N_CHIPS * fs), BF16)
    in_specs = [pl.BlockSpec((tm, d_model), lambda s, i, sl, co: (i, 0)),
                pl.BlockSpec(w_block, lambda s, i, sl, co: (sl[s], 0, 0, 0)),
                pl.BlockSpec(w_block, lambda s, i, sl, co: (sl[s], 1, 0, 0))]
    args = [slots, cols, n, ga, ga]
    aliases = {}
    if has_into:
        in_specs += [pl.BlockSpec(memory_space=pl.ANY)] * 3
        args += list(into)
        aliases = {5: 0, 6: 1, 7: 2}
    return pl.pallas_call(
        body, name=name,
        grid_spec=pltpu.PrefetchScalarGridSpec(
            num_scalar_prefetch=2, grid=(n_sh, t_rows // tm), in_specs=in_specs,
            out_specs=[pl.BlockSpec((tm, fs), lambda s, i, sl, co: (i, co[s]))] * 3),
        out_shape=[out, out, out], input_output_aliases=aliases,
        compiler_params=_params(("arbitrary", "arbitrary")),
    )(*args)


def _swiglu_bwd(dh, a, b):
    a = a.astype(F32)
    b = b.astype(F32)
    sg = _sigmoid_approx(a)
    return dh * b * sg * (1.0 + a * (1.0 - sg)), dh * a * sg


def _attn_fwd(q, k, v, tm=512):
    t_rows, d_model = q.shape
    n_mem = k.shape[0]
    hd = d_model // MEM_HEADS
    scale = hd ** -0.5
    tm = _tile(t_rows, tm)

    def body(q_ref, k_ref, v_ref, o_ref):
        for h in range(MEM_HEADS):
            cols = slice(h * hd, (h + 1) * hd)
            s = _dot_nt(q_ref[:, cols], k_ref[:, cols]) * scale
            s = s - jnp.max(s, axis=-1, keepdims=True)
            e = jnp.exp(s)
            p = e / jnp.sum(e, axis=-1, keepdims=True)
            o_ref[:, cols] = jnp.dot(p.astype(BF16), v_ref[:, cols], preferred_element_type=F32).astype(o_ref.dtype)

    return pl.pallas_call(
        body, name="attn_fwd", grid=(t_rows // tm,),
        in_specs=[pl.BlockSpec((tm, d_model), lambda i: (i, 0)),
                  pl.BlockSpec((n_mem, d_model), lambda i: (0, 0)),
                  pl.BlockSpec((n_mem, d_model), lambda i: (0, 0))],
        out_specs=pl.BlockSpec((tm, d_model), lambda i: (i, 0)),
        out_shape=jax.ShapeDtypeStruct((t_rows, d_model), BF16),
        compiler_params=_params(("arbitrary",)),
    )(q, k, v)


def _attn_bwd(q, k, v, do, tm=512):
    t_rows, d_model = q.shape
    n_mem = k.shape[0]
    hd = d_model // MEM_HEADS
    scale = hd ** -0.5
    tm = _tile(t_rows, tm, 16)

    def body(q_ref, k_ref, v_ref, do_ref, dq_ref, dk_ref, dv_ref):
        @pl.when(pl.program_id(0) == 0)
        def _():
            dk_ref[...] = jnp.zeros(dk_ref.shape, F32)
            dv_ref[...] = jnp.zeros(dv_ref.shape, F32)

        for h in range(MEM_HEADS):
            cols = slice(h * hd, (h + 1) * hd)
            qh, kh, vh, doh = q_ref[:, cols], k_ref[:, cols], v_ref[:, cols], do_ref[:, cols]
            s = _dot_nt(qh, kh) * scale
            s = s - jnp.max(s, axis=-1, keepdims=True)
            e = jnp.exp(s)
            p = e / jnp.sum(e, axis=-1, keepdims=True)
            dv_ref[:, cols] += _dot_tn(p.astype(BF16), doh)
            dp = _dot_nt(doh, vh)
            ds = (p * (dp - jnp.sum(dp * p, axis=-1, keepdims=True)) * scale).astype(BF16)
            dq_ref[:, cols] = jnp.dot(ds, kh, preferred_element_type=F32).astype(dq_ref.dtype)
            dk_ref[:, cols] += _dot_tn(ds, qh)

    full = pl.BlockSpec((n_mem, d_model), lambda i: (0, 0))
    tile = pl.BlockSpec((tm, d_model), lambda i: (i, 0))
    return pl.pallas_call(
        body, name="attn_bwd", grid=(t_rows // tm,),
        in_specs=[tile, full, full, tile], out_specs=[tile, full, full],
        out_shape=[jax.ShapeDtypeStruct((t_rows, d_model), BF16),
                   jax.ShapeDtypeStruct((n_mem, d_model), F32), jax.ShapeDtypeStruct((n_mem, d_model), F32)],
        compiler_params=_params(("arbitrary",)),
    )(q, k, v, do)


def _split_bf16(v):
    hi = v.astype(BF16)
    return hi, (v - hi.astype(F32)).astype(BF16)


def _pool_fwd(u, col_blk, w_pool, scale, tt=256):
    t_rows = u.shape[0]
    ng, pw = w_pool.shape[0], w_pool.shape[-1]
    width = ng * pw
    tt = _tile(t_rows, tt, 16)
    nt = t_rows // tt
    assert len(POOL_WINDOWS) == ng and tt >= max(POOL_WINDOWS)

    def body(vc_ref, vp_ref, w_ref, sc_ref, pooled_ref, z_ref):
        i = pl.program_id(0)
        r = lax.broadcasted_iota(jnp.int32, (tt, tt), 0)
        c = lax.broadcasted_iota(jnp.int32, (tt, tt), 1)
        t = i * tt + lax.broadcasted_iota(jnp.int32, (tt, 1), 0)
        first = (i > 0).astype(F32)
        for g, w in enumerate(POOL_WINDOWS):
            cols = slice(g * pw, (g + 1) * pw)
            band_c = ((c <= r) & (c > r - w)).astype(BF16)
            band_p = (c > r - w + tt).astype(BF16)
            vc = vc_ref[:, cols]
            ch, cl = _split_bf16(vc)
            ph, plo = _split_bf16(vp_ref[:, cols] * first)
            sums = (jnp.dot(band_c, ch, preferred_element_type=F32) + jnp.dot(band_c, cl, preferred_element_type=F32)
                    + jnp.dot(band_p, ph, preferred_element_type=F32) + jnp.dot(band_p, plo, preferred_element_type=F32))
            cnt = jnp.minimum(t + 1, w).astype(F32)
            pooled = (sums / cnt - vc).astype(BF16)
            pooled_ref[:, cols] = pooled
            z_ref[:, cols] = jnp.dot(pooled, w_ref[g], preferred_element_type=F32) * sc_ref[:, cols]

    tile = pl.BlockSpec((tt, width), lambda i: (i, 0))
    return pl.pallas_call(
        body, name="pool_fwd", grid=(nt,),
        in_specs=[pl.BlockSpec((tt, width), lambda i: (i, col_blk)),
                  pl.BlockSpec((tt, width), lambda i: (jnp.maximum(i - 1, 0), col_blk)),
                  pl.BlockSpec((ng, pw, pw), lambda i: (0, 0, 0)),
                  pl.BlockSpec((1, width), lambda i: (0, 0))],
        out_specs=[tile, tile],
        out_shape=[jax.ShapeDtypeStruct((t_rows, width), BF16), jax.ShapeDtypeStruct((t_rows, width), F32)],
        compiler_params=_params(("arbitrary",)),
    )(u, u, w_pool, scale)


def _pool_bwd1(dz, pooled, w_pool, scale, tt=256):
    t_rows = dz.shape[0]
    ng, pw = w_pool.shape[0], w_pool.shape[-1]
    width = ng * pw
    tt = _tile(t_rows, tt, 16)
    nt = t_rows // tt

    def body(dz_ref, p_ref, w_ref, sc_ref, dp_ref, dw_ref, dsc_ref):
        i = pl.program_id(0)

        @pl.when(i == 0)
        def _():
            dw_ref[...] = jnp.zeros(dw_ref.shape, F32)
            dsc_ref[...] = jnp.zeros(dsc_ref.shape, F32)

        t = i * tt + lax.broadcasted_iota(jnp.int32, (tt, 1), 0)
        for g, w in enumerate(POOL_WINDOWS):
            cols = slice(g * pw, (g + 1) * pw)
            dz_v = dz_ref[:, cols]
            pooled_v = p_ref[:, cols]
            zpre = jnp.dot(pooled_v, w_ref[g], preferred_element_type=F32)
            dsc_ref[:, cols] += _colsum(dz_v * zpre)
            dzs = (dz_v * sc_ref[:, cols]).astype(BF16)
            dw_ref[g] += _dot_tn(pooled_v, dzs)
            cnt = jnp.minimum(t + 1, w).astype(F32)
            dp_ref[:, cols] = _dot_nt(dzs, w_ref[g]) / cnt

    tile = pl.BlockSpec((tt, width), lambda i: (i, 0))
    whole_w = pl.BlockSpec((ng, pw, pw), lambda i: (0, 0, 0))
    vec_w = pl.BlockSpec((1, width), lambda i: (0, 0))
    return pl.pallas_call(
        body, name="pool_bwd1", grid=(nt,),
        in_specs=[tile, tile, whole_w, vec_w], out_specs=[tile, whole_w, vec_w],
        out_shape=[jax.ShapeDtypeStruct((t_rows, width), F32), jax.ShapeDtypeStruct((ng, pw, pw), F32),
                   jax.ShapeDtypeStruct((1, width), F32)],
        compiler_params=_params(("arbitrary",)),
    )(dz, pooled, w_pool, scale)


def _pool_bwd2(dps, ng, tt=256):
    t_rows, width = dps.shape
    pw = width // ng
    tt = _tile(t_rows, tt, 16)
    nt = t_rows // tt

    def body(dc_ref, dn_ref, dv_ref):
        i = pl.program_id(0)
        r = lax.broadcasted_iota(jnp.int32, (tt, tt), 0)
        c = lax.broadcasted_iota(jnp.int32, (tt, tt), 1)
        t = i * tt + lax.broadcasted_iota(jnp.int32, (tt, 1), 0)
        last = (i < nt - 1).astype(F32)
        for g, w in enumerate(POOL_WINDOWS):
            cols = slice(g * pw, (g + 1) * pw)
            band_c = ((c >= r) & (c < r + w)).astype(BF16)
            band_n = (c < r + w - tt).astype(BF16)
            dc = dc_ref[:, cols]
            ch, cl = _split_bf16(dc)
            nh, nl = _split_bf16(dn_ref[:, cols] * last)
            sums = (jnp.dot(band_c, ch, preferred_element_type=F32) + jnp.dot(band_c, cl, preferred_element_type=F32)
                    + jnp.dot(band_n, nh, preferred_element_type=F32) + jnp.dot(band_n, nl, preferred_element_type=F32))
            cnt = jnp.minimum(t + 1, w).astype(F32)
            dv_ref[:, cols] = sums - dc * cnt

    tile = pl.BlockSpec((tt, width), lambda i: (i, 0))
    return pl.pallas_call(
        body, name="pool_bwd2", grid=(nt,),
        in_specs=[tile, pl.BlockSpec((tt, width), lambda i: (jnp.minimum(i + 1, nt - 1), 0))],
        out_specs=tile, out_shape=jax.ShapeDtypeStruct((t_rows, width), F32),
        compiler_params=_params(("arbitrary",)),
    )(dps, dps)


def _cpow(ar, ai, n):
    rr, ri, br, bi = None, None, ar, ai
    while n:
        if n & 1:
            rr, ri = (br, bi) if rr is None else (rr * br - ri * bi, rr * bi + ri * br)
        n >>= 1
        if n:
            br, bi = br * br - bi * bi, 2.0 * br * bi
    return rr, ri


def _chunk_carries(st_re, st_im, pr, pi, order):
    cb = st_re.shape[1]
    sub = lax.broadcasted_iota(jnp.int32, (S5_LANES, cb), 0)
    cr = jnp.zeros((S5_LANES, cb), F32)
    ci = jnp.zeros((S5_LANES, cb), F32)
    prev_r = jnp.zeros((1, cb), F32)
    prev_i = jnp.zeros((1, cb), F32)
    for k, src in order:
        er, ei = st_re[src:src + 1, :], st_im[src:src + 1, :]
        nr = er + pr * prev_r - pi * prev_i
        ni = ei + pr * prev_i + pi * prev_r
        cr = jnp.where(sub == k, jnp.broadcast_to(nr, (S5_LANES, cb)), cr)
        ci = jnp.where(sub == k, jnp.broadcast_to(ni, (S5_LANES, cb)), ci)
        prev_r, prev_i = nr, ni
    return cr, ci


def _s5_fwd(up, bblk, ab, cblk, tt=128):
    n_rows, ws = up.shape
    nb, cw, cb2 = bblk.shape
    cb = cb2 // 2
    lc = n_rows // S5_LANES
    tt = _tile(lc, tt, 1)
    nt = lc // tt
    rt = S5_LANES * tt

    def body(u_ref, b_ref, ab_ref, c_ref, y_ref, s_ref, bu_ref, st_re, st_im):
        ps, ti = pl.program_id(1), pl.program_id(2)
        ar = jnp.broadcast_to(ab_ref[0:1, :], (S5_LANES, cb))
        ai = jnp.broadcast_to(ab_ref[1:2, :], (S5_LANES, cb))

        @pl.when((ps == 0) & (ti == 0))
        def _():
            st_re[...] = jnp.zeros(st_re.shape, F32)
            st_im[...] = jnp.zeros(st_im.shape, F32)

        @pl.when((ps == 1) & (ti == 0))
        def _():
            pr, pi = _cpow(ab_ref[0:1, :], ab_ref[1:2, :], lc)
            cr, ci = _chunk_carries(st_re, st_im, pr, pi, [(k, k - 1) for k in range(1, S5_LANES)])
            st_re[...] = cr
            st_im[...] = ci

        bu_ref[...] = jnp.dot(u_ref[...], b_ref[...], preferred_element_type=F32)

        def step(t, carry, store):
            sr, si = carry
            rows = pl.ds(pl.multiple_of(t * S5_LANES, S5_LANES), S5_LANES)
            nr = ar * sr - ai * si + bu_ref[rows, 0:cb]
            ni = ar * si + ai * sr + bu_ref[rows, cb:cb2]
            if store:
                s_ref[rows, 0:cb] = nr
                s_ref[rows, cb:cb2] = ni
            return nr, ni

        @pl.when(ps == 0)
        def _():
            sr, si = lax.fori_loop(0, tt, functools.partial(step, store=False), (st_re[...], st_im[...]))
            st_re[...] = sr
            st_im[...] = si

        @pl.when(ps == 1)
        def _():
            sr, si = lax.fori_loop(0, tt, functools.partial(step, store=True), (st_re[...], st_im[...]))
            st_re[...] = sr
            st_im[...] = si
            y_ref[...] = jnp.dot(s_ref[...].astype(BF16), c_ref[...], preferred_element_type=F32)

    return pl.pallas_call(
        body, name="s5_fwd", grid=(nb, 2, nt),
        in_specs=[pl.BlockSpec((rt, cw), lambda j, ps, ti: (ti, j)),
                  pl.BlockSpec((None, cw, cb2), lambda j, ps, ti: (j, 0, 0)),
                  pl.BlockSpec((None, 2, cb), lambda j, ps, ti: (j, 0, 0)),
                  pl.BlockSpec((None, cb2, cw), lambda j, ps, ti: (j, 0, 0))],
        out_specs=[pl.BlockSpec((rt, cw), lambda j, ps, ti: (ti * ps, j)),
                   pl.BlockSpec((None, rt, cb2), lambda j, ps, ti: (j, ti * ps, 0))],
        out_shape=[jax.ShapeDtypeStruct((n_rows, ws), F32), jax.ShapeDtypeStruct((nb, n_rows, cb2), F32)],
        scratch_shapes=[pltpu.VMEM((rt, cb2), F32), pltpu.VMEM((S5_LANES, cb), F32), pltpu.VMEM((S5_LANES, cb), F32)],
        compiler_params=_params(("arbitrary", "arbitrary", "arbitrary")),
    )(up, bblk, ab, cblk)


def _s5_bwd(dyp, up, s_all, bblk_t, ab, cblk_t, tt=128):
    n_rows, ws = up.shape
    nb, cb2, cw = bblk_t.shape
    cb = cb2 // 2
    lc = n_rows // S5_LANES
    tt = _tile(lc, tt, 1)
    nt = lc // tt
    rt = S5_LANES * tt

    def body(dy_ref, u_ref, s_ref, bt_ref, ab_ref, ct_ref, du_ref, db_ref, dc_ref, da_ref, ds_ref, st_re, st_im):
        ps, ti = pl.program_id(1), pl.program_id(2)
        ar = jnp.broadcast_to(ab_ref[0:1, :], (S5_LANES, cb))
        ai = jnp.broadcast_to(ab_ref[1:2, :], (S5_LANES, cb))

        @pl.when((ps == 0) & (ti == 0))
        def _():
            st_re[...] = jnp.zeros(st_re.shape, F32)
            st_im[...] = jnp.zeros(st_im.shape, F32)
            db_ref[...] = jnp.zeros(db_ref.shape, F32)
            dc_ref[...] = jnp.zeros(dc_ref.shape, F32)
            da_ref[...] = jnp.zeros(da_ref.shape, F32)

        @pl.when((ps == 1) & (ti == 0))
        def _():
            pr, pi = _cpow(ab_ref[0:1, :], -ab_ref[1:2, :], lc)
            cr, ci = _chunk_carries(st_re, st_im, pr, pi, [(k, k + 1) for k in range(S5_LANES - 2, -1, -1)])
            st_re[...] = cr
            st_im[...] = ci

        ds_ref[...] = jnp.dot(dy_ref[...], ct_ref[...], preferred_element_type=F32)

        def rows_of(i):
            return pl.ds(pl.multiple_of((tt - 1 - i) * S5_LANES, S5_LANES), S5_LANES)

        def step0(i, carry):
            gr, gi = carry
            rows = rows_of(i)
            return (ar * gr + ai * gi + ds_ref[rows, 0:cb], ar * gi - ai * gr + ds_ref[rows, cb:cb2])

        def step1(i, carry):
            gr, gi, acr, aci = carry
            rows = rows_of(i)
            sr, si = s_ref[rows, 0:cb], s_ref[rows, cb:cb2]
            acr = acr + sr * gr + si * gi
            aci = aci + sr * gi - si * gr
            nr = ar * gr + ai * gi + ds_ref[rows, 0:cb]
            ni = ar * gi - ai * gr + ds_ref[rows, cb:cb2]
            ds_ref[rows, 0:cb] = nr
            ds_ref[rows, cb:cb2] = ni
            return nr, ni, acr, aci

        @pl.when(ps == 0)
        def _():
            gr, gi = lax.fori_loop(0, tt, step0, (st_re[...], st_im[...]))
            st_re[...] = gr
            st_im[...] = gi

        @pl.when(ps == 1)
        def _():
            zero = jnp.zeros((S5_LANES, cb), F32)
            gr, gi, acr, aci = lax.fori_loop(0, tt, step1, (st_re[...], st_im[...], zero, zero))
            st_re[...] = gr
            st_im[...] = gi
            da_ref[0] += acr
            da_ref[1] += aci
            dsb = ds_ref[...].astype(BF16)
            du_ref[...] = jnp.dot(dsb, bt_ref[...], preferred_element_type=F32)
            db_ref[...] += _dot_tn(u_ref[...], dsb)
            dc_ref[...] += _dot_tn(dy_ref[...], s_ref[...].astype(BF16))

    def tile_idx(ps, ti):
        return (nt - 1 - ti) * ps + (nt - 1) * (1 - ps)

    return pl.pallas_call(
        body, name="s5_bwd", grid=(nb, 2, nt),
        in_specs=[pl.BlockSpec((rt, cw), lambda j, ps, ti: (nt - 1 - ti, j)),
                  pl.BlockSpec((rt, cw), lambda j, ps, ti: (tile_idx(ps, ti), j)),
                  pl.BlockSpec((None, rt, cb2), lambda j, ps, ti: (j, tile_idx(ps, ti), 0)),
                  pl.BlockSpec((None, cb2, cw), lambda j, ps, ti: (j, 0, 0)),
                  pl.BlockSpec((None, 2, cb), lambda j, ps, ti: (j, 0, 0)),
                  pl.BlockSpec((None, cw, cb2), lambda j, ps, ti: (j, 0, 0))],
        out_specs=[pl.BlockSpec((rt, cw), lambda j, ps, ti: (tile_idx(ps, ti), j)),
                   pl.BlockSpec((None, cw, cb2), lambda j, ps, ti: (j, 0, 0)),
                   pl.BlockSpec((None, cw, cb2), lambda j, ps, ti: (j, 0, 0)),
                   pl.BlockSpec((None, 2, S5_LANES, cb), lambda j, ps, ti: (j, 0, 0, 0))],
        out_shape=[jax.ShapeDtypeStruct((n_rows, ws), F32), jax.ShapeDtypeStruct((nb, cw, cb2), F32),
                   jax.ShapeDtypeStruct((nb, cw, cb2), F32), jax.ShapeDtypeStruct((nb, 2, S5_LANES, cb), F32)],
        scratch_shapes=[pltpu.VMEM((rt, cb2), F32), pltpu.VMEM((S5_LANES, cb), F32), pltpu.VMEM((S5_LANES, cb), F32)],
        compiler_params=_params(("arbitrary", "arbitrary", "arbitrary")),
    )(dyp, up, s_all, bblk_t, ab, cblk_t)


def _s5_discretize(a_re, a_im, log_dt, b_re, b_im):
    dt = jnp.exp(log_dt)[:, None]
    mag = jnp.exp(a_re * dt)
    abar_re = mag * jnp.cos(a_im * dt)
    abar_im = mag * jnp.sin(a_im * dt)
    nr, ni = abar_re - 1.0, abar_im
    den = a_re * a_re + a_im * a_im
    fr = (nr * a_re + ni * a_im) / den
    fi = (ni * a_re - nr * a_im) / den
    bbar_re = fr[..., None] * b_re - fi[..., None] * b_im
    bbar_im = fr[..., None] * b_im + fi[..., None] * b_re
    return abar_re, abar_im, bbar_re, bbar_im


def _perm_rows(a):
    n, c = a.shape
    return a.reshape(S5_LANES, n // S5_LANES, c).transpose(1, 0, 2).reshape(n, c)


def _unperm_rows(a):
    n, c = a.shape
    return a.reshape(n // S5_LANES, S5_LANES, c).transpose(1, 0, 2).reshape(n, c)


HBM = pl.BlockSpec(memory_space=pltpu.HBM)
SEM = pl.BlockSpec(memory_space=pltpu.SEMAPHORE)
ANY = pl.BlockSpec(memory_space=pl.ANY)
EFFECT = pltpu.SideEffectType.DATAFLOW_SIDE_EFFECTING
COPIES_PER_BUFFER = {"ag_ici": 3, "ag_fwd": 3, "pair": N_CHIPS, "scatter": 3, "half": 1, "swap": 1, "bcast": 3}
PAIRED_KINDS = ("pair", "scatter", "swap")


def _place():
    x, y, c = lax.axis_index("x"), lax.axis_index("y"), lax.axis_index("c")
    chips = [(1 - x, y), (x, 1 - y), (1 - x, 1 - y)]
    return x, y, c, 2 * x + y, chips


def _n_copies(kind, n_bufs):
    if isinstance(kind, tuple):
        return len(kind[1])
    return COPIES_PER_BUFFER[kind] * (n_bufs // 2 if kind in PAIRED_KINDS else n_bufs)


def _comm_copies(kind, bufs):
    if isinstance(kind, tuple):
        full = _comm_copies(kind[0], bufs)
        return [full[k] for k in kind[1]]
    x, y, c, s, chips = _place()
    sib = (x, y, 1 - c)
    out = []
    if kind == "ag_ici":
        for w in bufs:
            for cx, cy in chips:
                out.append((w.at[s, c], w.at[s, c], w.at[2 * cx + cy, c], (cx, cy, c)))
    elif kind == "ag_fwd":
        for w in bufs:
            for cx, cy in chips:
                sj = 2 * cx + cy
                out.append((w.at[sj, c], w.at[sj, c], w.at[sj, 1 - c], sib))
    elif kind == "pair":
        n = len(bufs) // 2
        for g, got in zip(bufs[:n], bufs[n:]):
            for t in range(N_CHIPS):
                out.append((g.at[t, 1 - c], got.at[t], got.at[t], sib))
    elif kind == "scatter":
        n = len(bufs) // 2
        for p, got in zip(bufs[:n], bufs[n:]):
            for cx, cy in chips:
                out.append((p.at[2 * cx + cy], got.at[s], got.at[2 * cx + cy], (cx, cy, c)))
    elif kind == "half":
        for f in bufs:
            out.append((f.at[c], f.at[c], f.at[1 - c], sib))
    elif kind == "swap":
        n = len(bufs) // 2
        for v, got in zip(bufs[:n], bufs[n:]):
            out.append((v, got, got, sib))
    elif kind == "bcast":
        for w in bufs:
            for cx, cy in chips:
                out.append((w.at[s], w.at[s], w.at[2 * cx + cy], (cx, cy, c)))
    return out


def _comm_fused(name, kind, bufs):
    n = len(bufs)
    ncp = _n_copies(kind, n)

    def body(*refs):
        outs = refs[n:2 * n]
        send, recv = refs[2 * n:]
        copies = _comm_copies(kind, outs)
        started = []
        for k, (src, dst, _, peer) in enumerate(copies):
            cp = pltpu.make_async_remote_copy(src_ref=src, dst_ref=dst, send_sem=send.at[k], recv_sem=recv.at[k],
                                              device_id=peer, device_id_type=MESH)
            cp.start()
            started.append(cp)
        for k, (_, _, land, peer) in enumerate(copies):
            pltpu.make_async_remote_copy(src_ref=land, dst_ref=land, send_sem=send.at[k], recv_sem=recv.at[k],
                                         device_id=peer, device_id_type=MESH).wait_recv()
        for cp in started:
            cp.wait_send()

    return pl.pallas_call(
        body, name=name, in_specs=[ANY] * n, out_specs=[ANY] * n,
        out_shape=[jax.ShapeDtypeStruct(b.shape, b.dtype) for b in bufs],
        input_output_aliases={k: k for k in range(n)},
        scratch_shapes=[pltpu.SemaphoreType.DMA((ncp,))] * 2,
    )(*bufs)


def _comm_start(name, kind, bufs, after=None):
    n = len(bufs)
    ncp = _n_copies(kind, n)
    nx = 0 if after is None else 1

    def body(*refs):
        refs = refs[n + nx:]
        send, recv = refs[:ncp], refs[ncp:2 * ncp]
        outs = refs[2 * ncp:n + 2 * ncp]
        token = refs[n + 2 * ncp]
        for k, (src, dst, _, peer) in enumerate(_comm_copies(kind, outs)):
            pltpu.make_async_remote_copy(src_ref=src, dst_ref=dst, send_sem=send[k], recv_sem=recv[k],
                                         device_id=peer, device_id_type=MESH).start()
        token[...] = jnp.zeros(token.shape, token.dtype)

    res = pl.pallas_call(
        body, name=name, in_specs=[HBM] * n + [ANY] * nx,
        out_specs=[SEM] * (2 * ncp) + [HBM] * n + [pl.BlockSpec(memory_space=pltpu.VMEM)],
        out_shape=[pltpu.SemaphoreType.DMA(())] * (2 * ncp) + [pltpu.HBM(b.shape, b.dtype) for b in bufs]
        + [jax.ShapeDtypeStruct((8, 128), F32)],
        input_output_aliases={k: 2 * ncp + k for k in range(n)},
        compiler_params=pltpu.CompilerParams(has_side_effects=EFFECT),
    )(*[pltpu.with_memory_space_constraint(b, pltpu.HBM) for b in bufs], *([after] if nx else []))
    return list(res[:ncp]), list(res[ncp:2 * ncp]), list(res[2 * ncp:2 * ncp + n]), res[2 * ncp + n]


def _comm_wait(name, kind, bufs, send_sems, recv_sems, after):
    n = len(bufs)
    ncp = _n_copies(kind, n)

    def body(*refs):
        send, recv = refs[n:n + ncp], refs[n + ncp:n + 2 * ncp]
        outs = refs[n + 2 * ncp + 1:]
        for k, (src, _, land, peer) in enumerate(_comm_copies(kind, outs)):
            cp = pltpu.make_async_remote_copy(src_ref=src, dst_ref=land, send_sem=send[k], recv_sem=recv[k],
                                              device_id=peer, device_id_type=MESH)
            cp.wait_send()
            cp.wait_recv()

    return pl.pallas_call(
        body, name=name, in_specs=[HBM] * n + [SEM] * (2 * ncp) + [ANY], out_specs=[HBM] * n,
        out_shape=[pltpu.HBM(b.shape, b.dtype) for b in bufs],
        input_output_aliases={k: k for k in range(n)},
        compiler_params=pltpu.CompilerParams(has_side_effects=EFFECT),
    )(*bufs, *send_sems, *recv_sems, after)


def _pair_add(g, got, core):
    nchip, _, r, cw = g.shape
    tr = _tile(r, 512, 16)

    def body(c_ref, a_ref, b_ref, o_ref):
        o_ref[...] = a_ref[...] + b_ref[...]

    return pl.pallas_call(
        body, name="grads_pair_add",
        grid_spec=pltpu.PrefetchScalarGridSpec(
            num_scalar_prefetch=1, grid=(nchip, r // tr),
            in_specs=[pl.BlockSpec((None, None, tr, cw), lambda s, i, c_ref: (s, c_ref[0], i, 0)),
                      pl.BlockSpec((None, tr, cw), lambda s, i, c_ref: (s, i, 0))],
            out_specs=pl.BlockSpec((None, tr, cw), lambda s, i, c_ref: (s, i, 0))),
        out_shape=jax.ShapeDtypeStruct((nchip, r, cw), BF16),
        compiler_params=_params(("arbitrary", "arbitrary")),
    )(core, g, got)


def _chip_sum(parts, got, idx):
    _, r, cw = parts.shape
    tr = _tile(r, 512, 16)

    def body(i_ref, own_ref, a_ref, b_ref, c_ref, o_ref):
        o_ref[...] = ((own_ref[...].astype(F32) + a_ref[...].astype(F32)) + b_ref[...].astype(F32)) + c_ref[...].astype(F32)

    def slot(k):
        return pl.BlockSpec((None, tr, cw), lambda i, i_ref: (i_ref[k], i, 0))

    return pl.pallas_call(
        body, name="grads_chip_sum",
        grid_spec=pltpu.PrefetchScalarGridSpec(
            num_scalar_prefetch=1, grid=(r // tr,), in_specs=[slot(0), slot(1), slot(2), slot(3)], out_specs=slot(4)),
        out_shape=jax.ShapeDtypeStruct((2, r, cw), F32),
        compiler_params=_params(("arbitrary",)),
    )(idx, parts, got, got, got)


def _add_into_slot(v, got, chip):
    r, cw = v.shape
    tr = _tile(r, 256)

    def body(c_ref, a_ref, b_ref, o_ref):
        o_ref[...] = a_ref[...] + b_ref[...]

    tile = pl.BlockSpec((tr, cw), lambda i, c_ref: (i, 0))
    return pl.pallas_call(
        body, name="small_pair_add",
        grid_spec=pltpu.PrefetchScalarGridSpec(
            num_scalar_prefetch=1, grid=(r // tr,), in_specs=[tile, tile],
            out_specs=pl.BlockSpec((None, tr, cw), lambda i, c_ref: (c_ref[0], i, 0))),
        out_shape=jax.ShapeDtypeStruct((N_CHIPS, r, cw), F32),
        compiler_params=_params(("arbitrary",)),
    )(chip, v, got)


def _sum_slots(w):
    _, r, cw = w.shape
    tr = _tile(r, 256)

    def body(w_ref, o_ref):
        o_ref[...] = ((w_ref[0] + w_ref[1]) + w_ref[2]) + w_ref[3]

    return pl.pallas_call(
        body, name="small_chip_sum", grid=(r // tr,),
        in_specs=[pl.BlockSpec((N_CHIPS, tr, cw), lambda i: (0, i, 0))],
        out_specs=pl.BlockSpec((tr, cw), lambda i: (i, 0)),
        out_shape=jax.ShapeDtypeStruct((r, cw), F32),
        compiler_params=_params(("arbitrary",)),
    )(w)


def _adamw_math(w, g, m, v):
    m = ADAM_B1 * m + (1.0 - ADAM_B1) * g
    v = ADAM_B2 * v + (1.0 - ADAM_B2) * (g * g)
    m_hat = m / (1.0 - ADAM_B1 ** ADAM_STEP)
    v_hat = v / (1.0 - ADAM_B2 ** ADAM_STEP)
    delta = -ADAM_LR * (m_hat / (jnp.sqrt(v_hat) + ADAM_EPS) + ADAM_WD * w)
    return delta, m, v


def _adamw(name, w, m, v, g, g_half=0, g_row_off=0, tr=256):
    r, cw = w.shape
    tr = _tile(math.gcd(r, g_row_off) if g_row_off else r, tr)
    off = g_row_off // tr

    def body(w_ref, m_ref, v_ref, g_ref, go_ref, d_ref, mo_ref, vo_ref):
        g_v = g_ref[...]
        delta, m_n, v_n = _adamw_math(w_ref[...], g_v, m_ref[...], v_ref[...])
        go_ref[...] = g_v
        d_ref[...] = delta
        mo_ref[...] = m_n
        vo_ref[...] = v_n

    tile = pl.BlockSpec((tr, cw), lambda i: (i, 0))
    out = jax.ShapeDtypeStruct((r, cw), F32)
    return pl.pallas_call(
        body, name=name, grid=(r // tr,),
        in_specs=[tile, tile, tile, pl.BlockSpec((None, tr, cw), lambda i: (g_half, i + off, 0))],
        out_specs=[tile] * 4, out_shape=[out] * 4,
        compiler_params=_params(("arbitrary",)),
    )(w, m, v, g)


def kernel(x, mem, g_ffn1, w1_gate, w1_up, w1_down, g_mix, w_in, ssm_a_re, ssm_a_im, ssm_log_dt, ssm_b_re, ssm_b_im, ssm_c_re, ssm_c_im, ssm_d, w_glu, b_glu, w_pool, pool_scale, g_out_ssm, g_out_pool, w_out, g_xattn, g_mem, w_q, w_k, w_v, w_o, g_ffn2, w2_gate, w2_up, w2_down, g_final, loss_target, m_g_ffn1, m_w1_gate, m_w1_up, m_w1_down, m_g_mix, m_w_in, m_ssm_a_re, m_ssm_a_im, m_ssm_log_dt, m_ssm_b_re, m_ssm_b_im, m_ssm_c_re, m_ssm_c_im, m_ssm_d, m_w_glu, m_b_glu, m_w_pool, m_pool_scale, m_g_out_ssm, m_g_out_pool, m_w_out, m_g_xattn, m_g_mem, m_w_q, m_w_k, m_w_v, m_w_o, m_g_ffn2, m_w2_gate, m_w2_up, m_w2_down, m_g_final, v_g_ffn1, v_w1_gate, v_w1_up, v_w1_down, v_g_mix, v_w_in, v_ssm_a_re, v_ssm_a_im, v_ssm_log_dt, v_ssm_b_re, v_ssm_b_im, v_ssm_c_re, v_ssm_c_im, v_ssm_d, v_w_glu, v_b_glu, v_w_pool, v_pool_scale, v_g_out_ssm, v_g_out_pool, v_w_out, v_g_xattn, v_g_mem, v_w_q, v_w_k, v_w_v, v_w_o, v_g_ffn2, v_w2_gate, v_w2_up, v_w2_down, v_g_final):
    local = dict(locals())
    wts = {n: local[n] for n in WEIGHTS}
    mom = {n: local["m_" + n] for n in WEIGHTS}
    var = {n: local["v_" + n] for n in WEIGHTS}

    x2 = x[0]
    mem2 = mem[0]
    tgt = loss_target[0]
    t_rows, d = x2.shape
    fs = w1_gate.shape[-1]
    ds_ = w_in.shape[1]
    ws = d // 2
    n_pg = len(POOL_WINDOWS)
    pw = ws // n_pg
    n_grp = ws // SSM_GROUP
    n_state = ssm_a_re.shape[-1]
    cx_, cy_, cc_ = lax.axis_index("x"), lax.axis_index("y"), lax.axis_index("c")
    chip = (2 * cx_ + cy_).astype(jnp.int32)
    core = cc_.astype(jnp.int32).reshape(1)
    chip_idx = jnp.stack([chip, chip ^ 2, chip ^ 1, chip ^ 3, cc_.astype(jnp.int32)])

    glu_rows = w_glu[0].reshape(-1, d)
    pool_rows = w_pool[0].reshape(-1, d)
    n_glu, n_pool = glu_rows.shape[0], pool_rows.shape[0]
    PACKED = ['w_out', 'w_q', 'w_k', 'w_v', 'w_o']
    glu_at = len(PACKED) * ds_
    pool_at = glu_at + n_glu
    n_pad = -(pool_at + n_pool) % 32
    rp = pool_at + n_pool + n_pad

    def own_slot(src):
        src = src.astype(BF16)
        return lax.dynamic_update_slice(lax.empty((N_CHIPS,) + src.shape, BF16), src[None], (chip, 0, 0, 0))

    src_packed = jnp.concatenate([wts[n][0] for n in PACKED] + [glu_rows, pool_rows, jnp.zeros((n_pad, d), F32)], 0)
    src_up1 = jnp.stack([w1_gate[0], w1_up[0]]).astype(BF16)
    w_bufs = [own_slot(src_up1), own_slot(w1_down[0].reshape(2, fs // 2, d)),
              own_slot(w_in[0].reshape(2, ds_ // 2, d)), own_slot(src_packed.reshape(2, rp // 2, d)),
              own_slot(jnp.stack([w2_gate[0], w2_up[0]])), own_slot(w2_down[0].reshape(2, fs // 2, d))]
    near_send, near_recv, w0, ag_token = _comm_start("weights_start_near", ("ag_ici", (0, 1)), w_bufs[:1])

    def gathered(k, after):
        w = _comm_wait("weights_wait_%d" % k, "ag_ici", [w_bufs[k]], ag_send[3 * k:3 * k + 3],
                       ag_recv[3 * k:3 * k + 3], after)
        return _comm_fused("weights_forward_%d" % k, "ag_fwd", w)[0]

    def gathered_start(k, after):
        w = _comm_wait("weights_wait_%d" % k, "ag_ici", [w_bufs[k]], ag_send[3 * k:3 * k + 3],
                       ag_recv[3 * k:3 * k + 3], after)
        send, recv, thru, token = _comm_start("weights_forward_start_%d" % k, "ag_fwd", w)
        return (send, recv, thru), token

    def gathered_finish(k, handle, after):
        send, recv, thru = handle
        return _comm_wait("weights_forward_wait_%d" % k, "ag_fwd", thru, send, recv, after)[0]

    n1 = _rmsnorm("norm_ffn1", x2, wts['g_ffn1'].reshape(1, -1), deps=[ag_token])
    zero1 = jnp.zeros((1,), jnp.int32)
    near, far = jnp.stack([chip ^ 2, chip ^ 1]), (chip ^ 3).reshape(1)
    a1, b1, hm1 = _ffn_up("ffn1_up_own", n1, src_up1[None], zero1, chip.reshape(1), d, fs)
    n_rest = 3 * len(w_bufs) - 2
    rest_send, rest_recv, w_bufs, _ = _comm_start("weights_start_rest", ("ag_ici", tuple(range(2, 2 + n_rest))),
                                                  w0 + w_bufs[1:], after=hm1)
    ag_send, ag_recv = near_send + rest_send, near_recv + rest_recv
    w0 = _comm_wait("weights_wait_0_near", ("ag_ici", (0, 1)), [w_bufs[0]], ag_send[0:2], ag_recv[0:2], hm1)
    w0 = _comm_fused("weights_forward_0_near", ("ag_fwd", (0, 1)), w0)
    a1, b1, hm1 = _ffn_up("ffn1_up_near", n1, w0[0], near, near, d, fs, into=(a1, b1, hm1))
    w0 = _comm_wait("weights_wait_0_far", ("ag_ici", (2,)), w0, ag_send[2:3], ag_recv[2:3], hm1)
    ga1 = _comm_fused("weights_forward_0_far", ("ag_fwd", (2,)), w0)[0]
    a1, b1, hm1 = _ffn_up("ffn1_up_far", n1, ga1, far, far, d, fs, into=(a1, b1, hm1))
    gd1 = gathered(1, hm1).reshape(N_CHIPS, fs, d)
    h1 = _mm_nn("ffn1_down", hm1, gd1, (N_CHIPS, fs, d // 2), lambda j, k: (0, 0, j), 1, d, F32, res=x2, alpha=0.5,
                slabs=N_CHIPS)
    n2 = _rmsnorm("norm_mix", h1, wts['g_mix'].reshape(1, -1))
    dd_bufs = {'w_in': gathered(2, h1).reshape(N_CHIPS, ds_, d)}
    DD = {n: q for q, n in enumerate(PACKED)}
    DD['w_in'] = 0

    def mm_dd(name, a, wname, out_dtype, res=None, norm_g=None):
        q = DD[wname]
        return _mm_nn(name, a, dd_bufs[wname], (N_CHIPS, ds_, d), lambda j, k: (0, q, 0), 1, d, out_dtype, res=res,
                      norm_g=norm_g, slabs=N_CHIPS)

    def mm_dd_t(name, pairs, out_dtype, deps=()):
        ps = [(dy, dd_bufs[w], (N_CHIPS, ds_, d), functools.partial(lambda s, q: (0, q, 0), q=DD[w])) for dy, w in pairs]
        return _mm_nt_cols(name, ps, N_CHIPS, ds_, [out_dtype], deps=deps, slabs=N_CHIPS)[0]

    def vec(n):
        return wts[n].reshape(1, -1)

    disc_in = (ssm_a_re[0], ssm_a_im[0], ssm_log_dt[0], ssm_b_re[0], ssm_b_im[0])
    (abar_re, abar_im, bbar_re, bbar_im), disc_vjp = jax.vjp(_s5_discretize, *disc_in)
    gpb = min(S5_GROUPS_PER_BLOCK, n_grp)
    nb = n_grp // gpb
    cb = gpb * n_state
    eye = jnp.eye(gpb, dtype=F32)

    def blockdiag(t):
        return jnp.einsum('jgph,gk->jghkp', t.reshape(nb, gpb, n_state, SSM_GROUP), eye).reshape(nb, gpb * SSM_GROUP, cb)

    def blockdiag_c(t):
        return jnp.einsum('jghp,gk->jkpgh', t.reshape(nb, gpb, SSM_GROUP, n_state), eye).reshape(nb, cb, gpb * SSM_GROUP)

    bblk = jnp.concatenate([blockdiag(bbar_re), blockdiag(bbar_im)], -1).astype(BF16)
    cblk = jnp.concatenate([blockdiag_c(ssm_c_re[0]), -blockdiag_c(ssm_c_im[0])], 1).astype(BF16)
    ab = jnp.stack([abar_re.reshape(nb, cb), abar_im.reshape(nb, cb)], 1)

    u = mm_dd("mix_in", n2, 'w_in', F32)

    up = _perm_rows(u[:, :ws]).astype(BF16)
    ylin_p, s_all = _s5_fwd(up, bblk, ab, cblk)
    ylin = _unperm_rows(ylin_p)

    def gelu_fn(r, v):
        y1 = r[0] + v[0] * r[1]
        y2 = jax.nn.gelu(y1)
        return [y2, y2], []
    fwd_pk, tok = gathered_start(3, ylin_p)
    y2, y2b = _rowwise("s5_gelu", gelu_fn, [ylin, (u, 0, ws)], [vec('ssm_d')], [(ws, F32), (ws, BF16)], deps=[tok])
    packed = gathered_finish(3, fwd_pk, y2b).reshape(N_CHIPS, rp, d)
    for n in PACKED:
        dd_bufs[n] = packed
    wglu_full = packed[:, glu_at:glu_at + n_glu, :].reshape(ws, ws)
    wpool_full = packed[:, pool_at:pool_at + n_pool, :].reshape(N_CHIPS, n_pg, pw // N_CHIPS, pw)
    wpool_full = wpool_full.transpose(1, 0, 2, 3).reshape(n_pg, pw, pw)
    z = _mm_nn("s5_glu", y2b, wglu_full, (ws, ws), lambda j, k: (0, 0), 1, ws, F32)

    def glu_fn(r, v):
        y3 = r[0] * _sigmoid(r[1] + v[0])
        return [_rms_fwd(y3, v[1])], []
    m_ssm = _rowwise("s5_gate_norm", glu_fn, [y2, z], [vec('b_glu'), vec('g_out_ssm')], [(ws, BF16)])[0]

    pooled, zp = _pool_fwd(u, 1, wpool_full, vec('pool_scale'))
    fwd_a2, tok = gathered_start(4, zp)
    m_pool = _rmsnorm("norm_pool", zp, vec('g_out_pool'), deps=[tok])
    merged = jnp.concatenate([m_ssm, m_pool], -1)
    h2, hn = mm_dd("mix_out", merged, 'w_out', F32, res=h1, norm_g=vec('g_xattn'))

    q = mm_dd("attn_q", hn, 'w_q', BF16)
    fwd_d2, tok = gathered_start(5, q)
    memn = _rmsnorm("norm_mem", mem2, vec('g_mem'), deps=[tok])
    k_mem = mm_dd("attn_k", memn, 'w_k', BF16)
    v_mem = mm_dd("attn_v", memn, 'w_v', BF16)
    o = _attn_fwd(q, k_mem, v_mem)
    h3, n4 = mm_dd("attn_out", o, 'w_o', F32, res=h2, norm_g=vec('g_ffn2'))

    ga2 = gathered_finish(4, fwd_a2, h3)
    all_chips = jnp.arange(N_CHIPS, dtype=jnp.int32)
    a2, b2, hm2 = _ffn_up("ffn2_up", n4, ga2, all_chips, all_chips, d, fs)
    gd2 = gathered_finish(5, fwd_d2, hm2).reshape(N_CHIPS, fs, d)
    h4 = _mm_nn("ffn2_down", hm2, gd2, (N_CHIPS, fs, d // 2), lambda j, k: (0, 0, j), 1, d, F32, res=h3, alpha=0.5,
                slabs=N_CHIPS)

    def loss_fn(r, v):
        h, t = r
        e = _rms_fwd(h, v[0]) - t
        dy = e * (1.0 / d)
        dh, dg = _rms_bwd(dy, h, v[0])
        part = jnp.sum(_colsum(e * e), axis=1, keepdims=True) * (0.5 / d)
        return [dh, 0.5 * dh], [_colsum(dg), jnp.broadcast_to(part, (1, 128))]
    dh4, dy_f2, dg_final, loss_row = _rowwise("loss_head", loss_fn, [h4, tgt], [g_final.reshape(1, -1)],
                                              [(d, F32), (d, BF16)], [d, 128])

    def rs_pair_start(tag, gbufs, after=None):
        land = [lax.empty((N_CHIPS,) + g.shape[2:], BF16) for g in gbufs]
        send, recv, thru, token = _comm_start(tag + "_pair_start", "pair", list(gbufs) + land, after=after)
        return (send, recv, thru), token

    def rs_scatter_start(tag, handle, after):
        send, recv, thru = handle
        n = len(thru) // 2
        res = _comm_wait(tag + "_pair_wait", "pair", thru, send, recv, after)
        parts = [_pair_add(g, r, core) for g, r in zip(res[:n], res[n:])]
        land = [lax.empty(p.shape, BF16) for p in parts]
        send, recv, thru, token = _comm_start(tag + "_scatter_start", "scatter", parts + land)
        return (send, recv, thru), token

    def rs_half_start(tag, handle, after):
        send, recv, thru = handle
        n = len(thru) // 2
        res = _comm_wait(tag + "_scatter_wait", "scatter", thru, send, recv, after)
        full = [_chip_sum(p, g2, chip_idx) for p, g2 in zip(res[:n], res[n:])]
        send, recv, thru, token = _comm_start(tag + "_half_start", "half", full)
        return (send, recv, thru), token

    def rs_finish(tag, handle, after):
        send, recv, thru = handle
        return _comm_wait(tag + "_half_wait", "half", thru, send, recv, after)

    wblk = (None, None, d, fs)

    def ffn_down_bwd(tag, dy_half, a, b, hm, gd_l, deps=()):
        da, db = _mm_nt_cols(tag + "_down_bwd", [(dy_half, gd_l, (None, fs, d), lambda s: (s, 0, 0))],
                             N_CHIPS, fs, [BF16, BF16], epi=_swiglu_bwd, extras=[a, b], deps=deps, tm=1024, row_parts=4)
        g_down = _mm_tn(tag + "_dw_down", hm, dy_half, fs, d // 2, N_CHIPS, 2, jax.ShapeDtypeStruct((N_CHIPS, fs, d), BF16),
                        (None, fs, d // 2), lambda p, q: (p, 0, q), tt=2048)
        return da, db, g_down.reshape(N_CHIPS, 2, fs // 2, d)

    def ffn_up_bwd(tag, da, db, ga_l, deps=()):
        return _mm_nt_k(tag + "_up_bwd", [(da, ga_l, wblk, lambda s: (s, 0, 0, 0)), (db, ga_l, wblk, lambda s: (s, 1, 0, 0))],
                        N_CHIPS, d, BF16, deps=deps)

    def ffn_dw(name, dact, n_in, deps=()):
        return _mm_tn(name, n_in, dact, d // 2, fs, 2, N_CHIPS, jax.ShapeDtypeStruct((N_CHIPS, 2, d // 2, fs), BF16),
                      (None, None, d // 2, fs), lambda p, q: (q, p, 0, 0), tt=2048, deps=deps)

    def dw_dd(name, a, dy, wname, grad_b2):
        q = DD[wname]
        rows = ds_ if wname == 'w_in' else rp
        return _mm_tn(name, a, dy, ds_, d, N_CHIPS, 1, jax.ShapeDtypeStruct((N_CHIPS, rows, d), BF16),
                      (None, ds_, d), lambda p, qq: (p, q, 0), into=grad_b2, tt=2048)

    def norm_bwd(name, dn, h, gname, dres, deps=(), scale=1.0):
        def fn(r, v):
            dx, dg = _rms_bwd(r[0].astype(F32), r[1], v[0])
            tot = dx + r[2]
            return [tot, scale * tot], [_colsum(dg)]
        return _rowwise(name, fn, [dn, h, dres], [vec(gname)], [(d, F32), (d, BF16)], [d], deps=deps)

    da2, db2, g_down2 = ffn_down_bwd("ffn2", dy_f2, a2, b2, hm2, gd2)
    dn4 = ffn_up_bwd("ffn2", da2, db2, ga2)
    g_gate2 = ffn_dw("ffn2_dw_gate", da2, n4)
    g_up2 = ffn_dw("ffn2_dw_up", db2, n4)
    rs_f2, tok = rs_pair_start("ffn2", [g_gate2, g_up2, g_down2])
    dh3, dh3b, dg_ffn2 = norm_bwd("norm_ffn2_bwd", dn4, h3, 'g_ffn2', dh4, deps=[tok])
    rs_f2, tok = rs_scatter_start("ffn2", rs_f2, dh3b)

    do = mm_dd_t("attn_out_bwd", [(dh3b, 'w_o')], BF16, deps=[tok])
    grad_b2 = dw_dd("attn_dw_o", o, dh3b, 'w_o', None)
    dq, dk, dv = _attn_bwd(q, k_mem, v_mem, do)
    dkb, dvb = dk.astype(BF16), dv.astype(BF16)
    grad_b2 = dw_dd("attn_dw_q", hn, dq, 'w_q', grad_b2)
    dhn = mm_dd_t("attn_q_bwd", [(dq, 'w_q')], BF16)
    dh2, dh2b, dg_xattn = norm_bwd("norm_xattn_bwd", dhn, h2, 'g_xattn', dh3)
    grad_b2 = dw_dd("attn_dw_k", memn, dkb, 'w_k', grad_b2)
    grad_b2 = dw_dd("attn_dw_v", memn, dvb, 'w_v', grad_b2)
    dmemn = mm_dd_t("attn_kv_bwd", [(dkb, 'w_k'), (dvb, 'w_v')], F32)
    dg_mem = _rowwise("norm_mem_bwd", lambda r, v: ([], [_colsum(_rms_bwd(r[0], r[1], v[0])[1])]),
                      [dmemn, mem2], [vec('g_mem')], [], [d])[0]

    dmerged = mm_dd_t("mix_out_bwd", [(dh2b, 'w_out')], F32)
    grad_b2 = dw_dd("mix_dw_out", merged, dh2b, 'w_out', grad_b2)

    def gate_bwd_fn(r, v):
        dm, y2_v, z_v = r
        sg = _sigmoid(z_v + v[0])
        y3 = y2_v * sg
        dy3, dg = _rms_bwd(dm, y3, v[1])
        dz = dy3 * y3 * (1.0 - sg)
        return [dy3 * sg, dz], [_colsum(dg), _colsum(dz)]
    dy2a, dzb, dg_out_ssm, db_glu = _rowwise("s5_gate_norm_bwd", gate_bwd_fn, [(dmerged, 0, ws), y2, z],
                                             [vec('b_glu'), vec('g_out_ssm')], [(ws, F32), (ws, BF16)], [ws, ws])
    dy2b_ = _mm_nt_cols("s5_glu_bwd", [(dzb, wglu_full, (ws, ws), lambda s: (0, 0))], 1, ws, [F32])[0]
    dw_glu = _mm_tn("s5_dw_glu", y2b, dzb, ws, ws, 1, 1, jax.ShapeDtypeStruct((ws, ws), F32), (ws, ws), lambda p, q: (0, 0))

    def gelu_bwd_fn(r, v):
        dy2 = r[0] + r[1]
        us = r[3]
        y1 = r[2] + v[0] * us
        kk = math.sqrt(2.0 / math.pi)
        th = jnp.tanh(kk * (y1 + 0.044715 * y1 * y1 * y1))
        dgelu = 0.5 * (1.0 + th) + 0.5 * y1 * (1.0 - th * th) * kk * (1.0 + 3.0 * 0.044715 * y1 * y1)
        dy1 = dy2 * dgelu
        return [dy1, dy1 * v[0]], [_colsum(dy1 * us)]
    dy1b, du_skip, d_ssm_d = _rowwise("s5_gelu_bwd", gelu_bwd_fn, [dy2a, dy2b_, ylin, (u, 0, ws)], [vec('ssm_d')],
                                      [(ws, BF16), (ws, F32)], [ws])

    bblk_t = jnp.swapaxes(bblk, 1, 2)
    cblk_t = jnp.swapaxes(cblk, 1, 2)
    du_p, d_bblk, d_cblk_t, d_ab = _s5_bwd(_perm_rows(dy1b), up, s_all, bblk_t, ab, cblk_t)
    du_ssm = _unperm_rows(du_p)

    dzp, dg_out_pool = _rowwise("norm_pool_bwd", lambda r, v: (lambda dx, dg: ([dx], [_colsum(dg)]))(*_rms_bwd(r[0], r[1], v[0])),
                                [(dmerged, 1, ws), zp], [vec('g_out_pool')], [(ws, F32)], [ws])
    dps, dw_pool, d_pool_scale = _pool_bwd1(dzp, pooled, wpool_full, vec('pool_scale'))
    du_pool = _pool_bwd2(dps, n_pg)

    dub = _rowwise("mix_du", lambda r, v: ([jnp.concatenate([r[0] + r[1], r[2]], -1)], []),
                   [du_ssm, du_skip, du_pool], [], [(d, BF16)])[0]
    dn2 = mm_dd_t("mix_in_bwd", [(dub, 'w_in')], BF16)
    grad_in = dw_dd("mix_dw_in", n2, dub, 'w_in', None)
    tail_g = jnp.concatenate([
        dw_glu.reshape(N_CHIPS, n_glu, d),
        dw_pool.reshape(n_pg, N_CHIPS, pw // N_CHIPS, pw).transpose(1, 0, 2, 3).reshape(N_CHIPS, n_pool, d),
        jnp.zeros((N_CHIPS, n_pad, d), F32)], 1).astype(BF16)
    grad_b2 = lax.dynamic_update_slice(grad_b2, tail_g, (0, glu_at, 0))
    rs_mix, tok = rs_pair_start("mixers", [grad_b2.reshape(N_CHIPS, 2, rp // 2, d),
                                           grad_in.reshape(N_CHIPS, 2, ds_ // 2, d)])
    dh1, dy_f1, dg_mix = norm_bwd("norm_mix_bwd", dn2, h1, 'g_mix', dh2, deps=[tok], scale=0.5)
    rs_mix, tok = rs_scatter_start("mixers", rs_mix, dy_f1)

    da1, db1, g_down1 = ffn_down_bwd("ffn1", dy_f1, a1, b1, hm1, gd1, deps=[tok])
    rs_d1, tok = rs_pair_start("ffn1_down", [g_down1])
    dn1 = ffn_up_bwd("ffn1", da1, db1, ga1, deps=[tok])
    rs_d1, tok = rs_scatter_start("ffn1_down", rs_d1, dn1)
    grad_x, _, dg_ffn1 = norm_bwd("norm_ffn1_bwd", dn1, x2, 'g_ffn1', dh1, deps=[tok])

    def undiag(t):
        return jnp.einsum('jghkp,gk->jgph', t.reshape(nb, gpb, SSM_GROUP, gpb, n_state), eye).reshape(n_grp, n_state, SSM_GROUP)

    d_bbar_re, d_bbar_im = undiag(d_bblk[:, :, :cb]), undiag(d_bblk[:, :, cb:])
    d_c_re = undiag(d_cblk_t[:, :, :cb]).transpose(0, 2, 1)
    d_c_im = -undiag(d_cblk_t[:, :, cb:]).transpose(0, 2, 1)
    d_abar = jnp.sum(d_ab, axis=2).reshape(nb, 2, gpb, n_state)
    d_abar_re = d_abar[:, 0].reshape(n_grp, n_state)
    d_abar_im = d_abar[:, 1].reshape(n_grp, n_state)
    d_a_re, d_a_im, d_log_dt, d_b_re, d_b_im = disc_vjp((d_abar_re, d_abar_im, d_bbar_re, d_bbar_im))

    small_g = {'g_ffn1': dg_ffn1, 'g_mix': dg_mix, 'ssm_a_re': d_a_re, 'ssm_a_im': d_a_im, 'ssm_log_dt': d_log_dt,
               'ssm_b_re': d_b_re, 'ssm_b_im': d_b_im, 'ssm_c_re': d_c_re, 'ssm_c_im': d_c_im, 'ssm_d': d_ssm_d,
               'b_glu': db_glu, 'pool_scale': d_pool_scale, 'g_out_ssm': dg_out_ssm, 'g_out_pool': dg_out_pool,
               'g_xattn': dg_xattn, 'g_mem': dg_mem, 'g_ffn2': dg_ffn2, 'g_final': dg_final}
    sizes = [wts[n].size for n in SMALL]
    total = sum(sizes) + 128
    rows_s = -(-total // (128 * 256)) * 256
    flat = jnp.concatenate([small_g[n].reshape(-1) for n in SMALL] + [loss_row.reshape(-1)])
    flat = jnp.pad(flat, (0, rows_s * 128 - total)).reshape(rows_s, 128)
    sw_send, sw_recv, sw_thru, tok = _comm_start("small_swap_start", "swap", [flat, lax.empty(flat.shape, F32)])
    g_gate1 = ffn_dw("ffn1_dw_gate", da1, n1, deps=[tok])
    rs_g1, tok_g1 = rs_pair_start("ffn1_gate", [g_gate1])
    sw_v, sw_got = _comm_wait("small_swap_wait", "swap", sw_thru, sw_send, sw_recv, tok_g1)
    slots = _add_into_slot(sw_v, sw_got, chip.reshape(1))
    bc_send, bc_recv, bc_thru, tok = _comm_start("small_bcast_start", "bcast", [slots])
    g_up1 = ffn_dw("ffn1_dw_up", db1, n1, deps=[tok])
    rs_g1, tok = rs_scatter_start("ffn1_gate", rs_g1, g_up1)
    slots, = _comm_wait("small_bcast_wait", "bcast", bc_thru, bc_send, bc_recv, tok)
    red = _sum_slots(slots).reshape(-1)
    loss = red[sum(sizes)]

    def flat_small(t):
        return jnp.pad(jnp.concatenate([t[n].reshape(-1) for n in SMALL]), (0, rows_s * 128 - sum(sizes))).reshape(rows_s, 128)
    sg_, sd_, sm_, sv_ = _adamw("adamw_small", flat_small(wts), flat_small(mom), flat_small(var), red.reshape(1, rows_s, 128))
    out = {}
    off = 0
    for n, sz in zip(SMALL, sizes):
        for key, arr in (('grad', sg_), ('delta', sd_), ('m', sm_), ('v', sv_)):
            out[key, n] = arr.reshape(-1)[off:off + sz].reshape(wts[n].shape)
        off += sz

    def upd(n, g_arr, half, row_off, shape2):
        res = _adamw("adamw_" + n, wts[n].reshape(shape2), mom[n].reshape(shape2), var[n].reshape(shape2), g_arr, half, row_off)
        for key, arr in zip(('grad', 'delta', 'm', 'v'), res):
            out[key, n] = arr.reshape(wts[n].shape)
        return res[3]

    rs_u1, tok = rs_pair_start("ffn1_up", [g_up1], after=sv_)
    rs_f2, tok = rs_half_start("ffn2", rs_f2, tok)
    rs_u1, tok = rs_scatter_start("ffn1_up", rs_u1, tok)
    rs_mix, tok = rs_half_start("mixers", rs_mix, tok)
    full_gate2, full_up2, full_down2 = rs_finish("ffn2", rs_f2, tok)
    upd('w2_gate', full_gate2.reshape(1, d, fs), 0, 0, (d, fs))
    upd('w2_up', full_up2.reshape(1, d, fs), 0, 0, (d, fs))
    last = upd('w2_down', full_down2.reshape(1, fs, d), 0, 0, (fs, d))
    rs_d1, tok = rs_half_start("ffn1_down", rs_d1, last)
    full_b2, full_in = rs_finish("mixers", rs_mix, tok)
    full_b2 = full_b2.reshape(1, rp, d)
    last = upd('w_in', full_in.reshape(1, ds_, d), 0, 0, (ds_, d))
    for n in PACKED:
        last = upd(n, full_b2, 0, DD[n] * ds_, (ds_, d))
    glu_shape, pool_shape = (ws // N_CHIPS, ws), (n_pg * pw // N_CHIPS, pw)
    upd('w_glu', full_b2[:, glu_at:glu_at + n_glu].reshape((1,) + glu_shape), 0, 0, glu_shape)
    upd('w_pool', full_b2[:, pool_at:pool_at + n_pool].reshape((1,) + pool_shape), 0, 0, pool_shape)
    full_down1, = rs_finish("ffn1_down", rs_d1, last)
    last = upd('w1_down', full_down1.reshape(1, fs, d), 0, 0, (fs, d))
    rs_g1, tok = rs_half_start("ffn1_gate", rs_g1, last)
    rs_u1, tok = rs_half_start("ffn1_up", rs_u1, tok)
    full_gate1, = rs_finish("ffn1_gate", rs_g1, tok)
    last = upd('w1_gate', full_gate1.reshape(1, d, fs), 0, 0, (d, fs))
    full_up1, = rs_finish("ffn1_up", rs_u1, last)
    upd('w1_up', full_up1.reshape(1, d, fs), 0, 0, (d, fs))

    return (loss, grad_x[None], *[out['grad', n] for n in WEIGHTS], *[out['delta', n] for n in WEIGHTS],
            *[out['m', n] for n in WEIGHTS], *[out['v', n] for n in WEIGHTS])
```
